```python
import math
import jax, jax.numpy as jnp
from jax import lax
import numpy as np

D_MODEL = 2048
BATCH = 2
SEQ = 16384
DEPTH = 2

GRID_W = 64
CTX_LEN = 256
NORM_EPS = 1e-6
ROPE_THETA = 10000.0
Q_BLOCK = 128
D_FF = 4 * D_MODEL
N_MOD = 6

A_WIDTH = D_MODEL // 2
A_QK_DIM = 64
A_V_DIM = 2 * A_QK_DIM
A_HEADS = A_WIDTH // A_V_DIM
A_COLS = 3 * A_WIDTH
B_WIDTH = D_MODEL - A_WIDTH
B_HEAD = 64
B_HEADS = B_WIDTH // B_HEAD
B_DECAY_RANK = 64
B_ICL_RANK = 64
B_GATE_RANK = 160
B_GN_EPS = 64e-5
B_SPLITS = (B_WIDTH, B_WIDTH, B_WIDTH, B_DECAY_RANK, B_ICL_RANK, B_GATE_RANK)
B_COLS = sum(B_SPLITS)
C_WIDTH = D_MODEL // 2
HY_COLS = 3 * C_WIDTH
HY_EMB = 33
HY_ORDER = 64
HY_TARGET = 1e-2
HY_FAST_PCT = 0.3
HY_SLOW_PCT = 1.5
D_WIDTH = D_MODEL - C_WIDTH
D_HEAD = 128
D_HEADS = D_WIDTH // D_HEAD
D_KV_HEADS = D_HEADS // 4
KV_COLS = 2 * D_KV_HEADS * D_HEAD

kernel_name = 'hybrid_diffattn_rwkv7_hyena_gqa_flow_block'


def rmsnorm(x, g):
    xf = x.astype(jnp.float32)
    y = xf * lax.rsqrt(jnp.mean(xf * xf, axis=-1, keepdims=True) + NORM_EPS)
    return (y * g.astype(jnp.float32)).astype(x.dtype)


def modulate(x, g, shift, scale):
    return rmsnorm(x, g) * (1 + scale) + shift


def split_cols(z, sizes):
    return jnp.split(z, [int(i) for i in np.cumsum(sizes)[:-1]], axis=-1)


def shift3(u):
    pad = jnp.pad(u, ((0, 0), (1, 1), (0, 0)))
    return pad[:, :-2], pad[:, 2:]


def grid_rope_tables(n_tok, dim, dtype):
    rows = n_tok // GRID_W
    row = jnp.repeat(jnp.arange(rows, dtype=jnp.float32), GRID_W)
    col = jnp.tile(jnp.arange(GRID_W, dtype=jnp.float32), rows)
    half = dim // 2
    inv = ROPE_THETA ** (-jnp.arange(0, half, 2, dtype=jnp.float32) / half)
    ang = jnp.stack([row[:, None] * inv, col[:, None] * inv], axis=1)
    return jnp.cos(ang).astype(dtype), jnp.sin(ang).astype(dtype)


def apply_rope(x, cos, sin):
    b, n, h, d = x.shape
    xr = x.reshape(b, n, h, 2, 2, d // 4)
    x1, x2 = xr[..., 0, :], xr[..., 1, :]
    c, s = cos[None, :, None], sin[None, :, None]
    return jnp.stack([x1 * c - x2 * s, x2 * c + x1 * s], axis=-2).reshape(b, n, h, d)


def sweep_query_blocks(block_fn, qs):
    b, n = qs[0].shape[:2]
    nb = n // Q_BLOCK
    blocks = tuple(q.reshape(b, nb, Q_BLOCK, *q.shape[2:]).swapaxes(0, 1) for q in qs)
    out = lax.map(lambda blk: block_fn(*blk), blocks)
    return out.swapaxes(0, 1).reshape(b, n, *out.shape[3:])


def mlp(h, w1, w2):
    return jnp.square(jax.nn.relu(h @ w1)) @ w2


def diff_attention(q1, q2, k1, k2, v, lam):
    scale = A_QK_DIM ** -0.5
    s1 = jnp.einsum('bqhd,bkhd->bhqk', q1, k1).astype(jnp.float32) * scale
    s2 = jnp.einsum('bqhd,bkhd->bhqk', q2, k2).astype(jnp.float32) * scale
    p = jax.nn.softmax(s1, axis=-1) - lam * jax.nn.softmax(s2, axis=-1)
    return jnp.einsum('bhqk,bkhd->bqhd', p.astype(v.dtype), v)


def gqa_attention(q, k, v):
    b, nq, h, d = q.shape
    n_kv = k.shape[2]
    qg = q.reshape(b, nq, n_kv, h // n_kv, d)
    s = jnp.einsum('bqngd,bknd->bngqk', qg, k).astype(jnp.float32) * d ** -0.5
    p = jax.nn.softmax(s, axis=-1).astype(v.dtype)
    return jnp.einsum('bngqk,bknd->bqngd', p, v).reshape(b, nq, h, d)


def rwkv7_scan(r, w, k, v, a, b, state0, reverse):
    def step(S, inp):
        r_t, w_t, k_t, v_t, a_t, b_t = inp
        Sa = jnp.einsum('bhij,bhj->bhi', S, a_t)
        S = S * w_t[:, :, None, :] + Sa[..., None] * b_t[:, :, None, :] + v_t[..., None] * k_t[:, :, None, :]
        return S, jnp.einsum('bhij,bhj->bhi', S, r_t)
    xs = tuple(jnp.swapaxes(z, 0, 1) for z in (r, w, k, v, a, b))
    S, ys = lax.scan(step, state0, xs, reverse=reverse)
    return jnp.swapaxes(ys, 0, 1), S


def rwkv7_prepare(cols, p):
    prev, nxt = shift3(cols)
    cols = cols + p['mu_prev'] * (prev - cols) + p['mu_next'] * (nxt - cols)
    r, k, v, wd, ad, gd = split_cols(cols, B_SPLITS)
    bsz, n = cols.shape[:2]
    heads = lambda z: z.astype(jnp.float32).reshape(bsz, n, B_HEADS, B_HEAD)
    kk = heads(k * p['k_k'])
    kk = kk * lax.rsqrt(jnp.sum(kk * kk, axis=-1, keepdims=True) + 1e-12)
    g = jax.nn.sigmoid(gd) @ p['g2']
    dirs = []
    for d in ('f', 'b'):
        w_log = -jax.nn.softplus(-(p['w0_' + d] + jnp.tanh(wd) @ p['w2_' + d])) - 0.5
        a = jax.nn.sigmoid(p['a0_' + d] + ad @ p['a2_' + d])
        k_d = k * (1 + (a - 1) * p['k_a'])
        dirs.append((jnp.exp(-jnp.exp(heads(w_log))), heads(a), heads(k_d)))
    return heads(r), heads(v), kk, g, dirs


def rwkv7_run(prep, states0, p):
    r, v, kk, g, dirs = prep
    (w_f, a_f, k_f), (w_b, a_b, k_b) = dirs
    y_f, s_f = rwkv7_scan(r, w_f, k_f, v, -kk, kk * a_f, states0[0], False)
    y_b, s_b = rwkv7_scan(r, w_b, k_b, v, -kk, kk * a_b, states0[1], True)
    y = y_f + y_b
    mu = jnp.mean(y, axis=-1, keepdims=True)
    var = jnp.mean(jnp.square(y - mu), axis=-1, keepdims=True)
    y = (y - mu) * lax.rsqrt(var + B_GN_EPS)
    bonus = (jnp.sum(r * k_f * p['r_k'], -1, keepdims=True) + jnp.sum(r * k_b * p['r_k'], -1, keepdims=True)) * v
    bsz, n = y.shape[:2]
    y = y.reshape(bsz, n, B_WIDTH) * p['lnx_g'] + p['lnx_b'] + bonus.reshape(bsz, n, B_WIDTH)
    return (y * g).astype(g.dtype), (s_f, s_b)


def even_mixer(h_lat, h_ctx, p, layer_idx):
    bsz, n = h_lat.shape[:2]
    n_ctx = h_ctx.shape[1]
    p_lat = h_lat @ p['w_in']
    p_ctx = h_ctx @ p['w_in']
    lam_init = 0.8 - 0.6 * math.exp(-0.3 * layer_idx)
    lam = (jnp.exp(jnp.sum(p['lam_q1'] * p['lam_k1']).astype(jnp.float32))
           - jnp.exp(jnp.sum(p['lam_q2'] * p['lam_k2']).astype(jnp.float32)) + lam_init)

    def qkv(cols, m):
        q, k, v = jnp.split(cols[..., :A_COLS], 3, axis=-1)
        return (q.reshape(bsz, m, 2 * A_HEADS, A_QK_DIM), k.reshape(bsz, m, 2 * A_HEADS, A_QK_DIM),
                v.reshape(bsz, m, A_HEADS, A_V_DIM))

    def pair(z):
        return z[:, :, 0::2], z[:, :, 1::2]

    def post(o):
        return (rmsnorm(o, p['subln_g']) * (1 - lam_init)).reshape(o.shape[0], o.shape[1], A_WIDTH)

    q_l, k_l, v_l = qkv(p_lat, n)
    q_c, k_c, v_c = qkv(p_ctx, n_ctx)
    cos, sin = grid_rope_tables(n, A_QK_DIM, h_lat.dtype)
    q_l, k_l = apply_rope(q_l, cos, sin), apply_rope(k_l, cos, sin)
    k1, k2 = pair(jnp.concatenate([k_l, k_c], axis=1))
    v_all = jnp.concatenate([v_l, v_c], axis=1)
    o_lat = sweep_query_blocks(lambda qa, qb: diff_attention(qa, qb, k1, k2, v_all, lam), pair(q_l))
    o_ctx = diff_attention(*pair(q_c), *pair(k_c), v_c, lam)

    zero = jnp.zeros((bsz, B_HEADS, B_HEAD, B_HEAD), jnp.float32)
    y_c, states_c = rwkv7_run(rwkv7_prepare(p_ctx[..., A_COLS:], p), (zero, zero), p)
    y_l, _ = rwkv7_run(rwkv7_prepare(p_lat[..., A_COLS:], p), states_c, p)

    out_lat = jnp.concatenate([post(o_lat), y_l], axis=-1) @ p['w_out']
    out_ctx = jnp.concatenate([post(o_ctx), y_c], axis=-1) @ p['w_out']
    return out_lat, out_ctx


def hyena_filters(n, p):
    t = jnp.linspace(0.0, 1.0, n, dtype=jnp.float32)[:, None]
    bands = (HY_EMB - 1) // 2
    ang = (2 * math.pi / n) * jnp.arange(n, dtype=jnp.float32)[:, None] * jnp.linspace(1e-4, bands - 1, bands, dtype=jnp.float32)[None]
    z = jnp.concatenate([t, jnp.cos(ang), -jnp.sin(ang)], axis=-1)
    f32 = lambda a: a.astype(jnp.float32)
    freq = f32(p['f_freq'])
    h = jnp.sin(freq * (z @ f32(p['f_w1']) + f32(p['f_b1'])))
    h = jnp.sin(freq * (h @ f32(p['f_w2']) + f32(p['f_b2'])))
    h = jnp.sin(freq * (h @ f32(p['f_w3']) + f32(p['f_b3'])))
    h = (h @ f32(p['f_w4'])).reshape(n, 2, C_WIDTH)
    deltas = jnp.abs(jnp.linspace(math.log(HY_TARGET) / HY_SLOW_PCT, math.log(HY_TARGET) / HY_FAST_PCT, C_WIDTH, dtype=jnp.float32))
    h = h * jnp.exp(-t * deltas)[:, None, :]
    h = h / jnp.sum(jnp.abs(h), axis=(0, 1), keepdims=True)
    return h[:, 0], h[:, 1]


def long_conv(u, h_fwd, h_bwd, bias):
    n, ch = u.shape[1], u.shape[2]
    kern = jnp.concatenate([h_fwd[:1] + h_bwd[:1], h_fwd[1:], jnp.zeros((1, ch), jnp.float32), h_bwd[:0:-1]], axis=0)
    uf = u.astype(jnp.float32)
    y = jnp.fft.irfft(jnp.fft.rfft(uf, n=2 * n, axis=1) * jnp.fft.rfft(kern, axis=0)[None], n=2 * n, axis=1)[:, :n]
    return (y + uf * bias.astype(jnp.float32)).astype(u.dtype)


def hyena(cols, p):
    prev, nxt = shift3(cols)
    u = prev * p['conv_w'][0] + cols * p['conv_w'][1] + nxt * p['conv_w'][2] + p['conv_b']
    x0, x1, v = jnp.split(u, 3, axis=-1)
    h_fwd, h_bwd = hyena_filters(cols.shape[1], p)
    return x0 * long_conv(v * x1, h_fwd, h_bwd, p['hy_bias'])


def odd_mixer(h_lat, h_ctx, p, ctx_out):
    bsz, n = h_lat.shape[:2]
    n_ctx = h_ctx.shape[1]
    p_lat = h_lat @ p['w_in']

    def kv(cols, m):
        k, v = jnp.split(cols[..., -KV_COLS:], 2, axis=-1)
        return rmsnorm(k.reshape(bsz, m, D_KV_HEADS, D_HEAD), p['k_norm_g']), v.reshape(bsz, m, D_KV_HEADS, D_HEAD)

    def q_of(cols, m):
        return rmsnorm(cols[..., HY_COLS:HY_COLS + D_WIDTH].reshape(bsz, m, D_HEADS, D_HEAD), p['q_norm_g'])

    p_ctx = h_ctx @ p['w_in'] if ctx_out else h_ctx @ p['w_in'][:, -KV_COLS:]
    k_c, v_c = kv(p_ctx, n_ctx)
    cos, sin = grid_rope_tables(n, D_HEAD, h_lat.dtype)
    q_l = apply_rope(q_of(p_lat, n), cos, sin)
    k_l, v_l = kv(p_lat, n)
    k_all = jnp.concatenate([apply_rope(k_l, cos, sin), k_c], axis=1)
    v_all = jnp.concatenate([v_l, v_c], axis=1)
    o_lat = sweep_query_blocks(lambda qb: gqa_attention(qb, k_all, v_all), (q_l,))
    out_lat = jnp.concatenate([hyena(p_lat[..., :HY_COLS], p), o_lat.reshape(bsz, n, D_WIDTH)], axis=-1) @ p['w_out']
    if not ctx_out:
        return out_lat, None
    o_ctx = gqa_attention(q_of(p_ctx, n_ctx), k_c, v_c)
    out_ctx = jnp.concatenate([hyena(p_ctx[..., :HY_COLS], p), o_ctx.reshape(bsz, n_ctx, D_WIDTH)], axis=-1) @ p['w_out']
    return out_lat, out_ctx


def setup_inputs(seed: int = 0) -> dict:
    key = jax.random.key(seed)
    ks = iter(jax.random.split(key, 64))

    def nrm(shape, scale):
        return jax.random.normal(next(ks), shape, jnp.float32) * scale

    def gain(shape):
        return 1.0 + nrm(shape, 0.01)

    def unif(shape, lo, hi):
        return jax.random.uniform(next(ks), shape, jnp.float32, lo, hi)

    D = D_MODEL
    inp = {}
    inp['x'] = nrm((BATCH, SEQ, D), 1.0)
    inp['c'] = nrm((BATCH, D), 1.0)
    inp['ctx'] = nrm((BATCH, CTX_LEN, D), 1.0)
    inp['c_ctx'] = nrm((D,), 1.0)
    inp['l0_w_mod'] = nrm((D, N_MOD * D), 0.5 * D ** -0.5)
    inp['l0_b_mod'] = nrm((N_MOD * D,), 0.02)
    inp['l0_norm1_g'] = gain((D,))
    inp['l0_norm2_g'] = gain((D,))
    inp['l0_w_in'] = nrm((D, A_COLS + B_COLS), D ** -0.5)
    inp['l0_lam_q1'] = nrm((A_QK_DIM,), 0.1)
    inp['l0_lam_k1'] = nrm((A_QK_DIM,), 0.1)
    inp['l0_lam_q2'] = nrm((A_QK_DIM,), 0.1)
    inp['l0_lam_k2'] = nrm((A_QK_DIM,), 0.1)
    inp['l0_subln_g'] = gain((A_V_DIM,))
    inp['l0_mu_prev'] = unif((B_COLS,), 0.0, 0.5)
    inp['l0_mu_next'] = unif((B_COLS,), 0.0, 0.5)
    inp['l0_w0_f'] = unif((B_WIDTH,), -6.0, 1.0)
    inp['l0_w2_f'] = nrm((B_DECAY_RANK, B_WIDTH), 0.5 * B_DECAY_RANK ** -0.5)
    inp['l0_a0_f'] = nrm((B_WIDTH,), 0.1)
    inp['l0_a2_f'] = nrm((B_ICL_RANK, B_WIDTH), B_ICL_RANK ** -0.5)
    inp['l0_w0_b'] = unif((B_WIDTH,), -6.0, 1.0)
    inp['l0_w2_b'] = nrm((B_DECAY_RANK, B_WIDTH), 0.5 * B_DECAY_RANK ** -0.5)
    inp['l0_a0_b'] = nrm((B_WIDTH,), 0.1)
    inp['l0_a2_b'] = nrm((B_ICL_RANK, B_WIDTH), B_ICL_RANK ** -0.5)
    inp['l0_g2'] = nrm((B_GATE_RANK, B_WIDTH), B_GATE_RANK ** -0.5)
    inp['l0_k_k'] = 0.85 + nrm((B_WIDTH,), 0.02)
    inp['l0_k_a'] = gain((B_WIDTH,))
    inp['l0_r_k'] = nrm((B_HEADS, B_HEAD), 0.1)
    inp['l0_lnx_g'] = gain((B_WIDTH,))
    inp['l0_lnx_b'] = nrm((B_WIDTH,), 0.01)
    inp['l0_w_out'] = nrm((D, D), D ** -0.5)
    inp['l0_mlp_w1'] = nrm((D, D_FF), D ** -0.5)
    inp['l0_mlp_w2'] = nrm((D_FF, D), D_FF ** -0.5)
    inp['l1_w_mod'] = nrm((D, N_MOD * D), 0.5 * D ** -0.5)
    inp['l1_b_mod'] = nrm((N_MOD * D,), 0.02)
    inp['l1_norm1_g'] = gain((D,))
    inp['l1_norm2_g'] = gain((D,))
    inp['l1_w_in'] = nrm((D, HY_COLS + D_WIDTH + KV_COLS), D ** -0.5)
    inp['l1_conv_w'] = nrm((3, HY_COLS), 3 ** -0.5)
    inp['l1_conv_b'] = nrm((HY_COLS,), 0.02)
    inp['l1_f_w1'] = nrm((HY_EMB, HY_ORDER), HY_EMB ** -0.5)
    inp['l1_f_b1'] = nrm((HY_ORDER,), 0.1)
    inp['l1_f_w2'] = nrm((HY_ORDER, HY_ORDER), HY_ORDER ** -0.5)
    inp['l1_f_b2'] = nrm((HY_ORDER,), 0.1)
    inp['l1_f_w3'] = nrm((HY_ORDER, HY_ORDER), HY_ORDER ** -0.5)
    inp['l1_f_b3'] = nrm((HY_ORDER,), 0.1)
    inp['l1_f_w4'] = nrm((HY_ORDER, 2 * C_WIDTH), HY_ORDER ** -0.5)
    inp['l1_f_freq'] = gain((HY_ORDER,))
    inp['l1_hy_bias'] = nrm((C_WIDTH,), 0.5)
    inp['l1_q_norm_g'] = gain((D_HEAD,))
    inp['l1_k_norm_g'] = gain((D_HEAD,))
    inp['l1_w_out'] = nrm((D, D), D ** -0.5)
    inp['l1_mlp_w1'] = nrm((D, D_FF), D ** -0.5)
    inp['l1_mlp_w2'] = nrm((D_FF, D), D_FF ** -0.5)
    inp['final_g'] = gain((D,))
    return inp


def reference(x, c, ctx, c_ctx,
              l0_w_mod, l0_b_mod, l0_norm1_g, l0_norm2_g, l0_w_in, l0_lam_q1, l0_lam_k1, l0_lam_q2, l0_lam_k2,
              l0_subln_g, l0_mu_prev, l0_mu_next, l0_w0_f, l0_w2_f, l0_a0_f, l0_a2_f, l0_w0_b, l0_w2_b, l0_a0_b,
              l0_a2_b, l0_g2, l0_k_k, l0_k_a, l0_r_k, l0_lnx_g, l0_lnx_b, l0_w_out, l0_mlp_w1, l0_mlp_w2,
              l1_w_mod, l1_b_mod, l1_norm1_g, l1_norm2_g, l1_w_in, l1_conv_w, l1_conv_b, l1_f_w1, l1_f_b1,
              l1_f_w2, l1_f_b2, l1_f_w3, l1_f_b3, l1_f_w4, l1_f_freq, l1_hy_bias, l1_q_norm_g, l1_k_norm_g,
              l1_w_out, l1_mlp_w1, l1_mlp_w2, final_g):
    p0 = dict(w_mod=l0_w_mod, b_mod=l0_b_mod, norm1_g=l0_norm1_g, norm2_g=l0_norm2_g, w_in=l0_w_in,
              lam_q1=l0_lam_q1, lam_k1=l0_lam_k1, lam_q2=l0_lam_q2, lam_k2=l0_lam_k2, subln_g=l0_subln_g,
              mu_prev=l0_mu_prev, mu_next=l0_mu_next, w0_f=l0_w0_f, w2_f=l0_w2_f, a0_f=l0_a0_f, a2_f=l0_a2_f,
              w0_b=l0_w0_b, w2_b=l0_w2_b, a0_b=l0_a0_b, a2_b=l0_a2_b, g2=l0_g2, k_k=l0_k_k, k_a=l0_k_a,
              r_k=l0_r_k, lnx_g=l0_lnx_g, lnx_b=l0_lnx_b, w_out=l0_w_out, mlp_w1=l0_mlp_w1, mlp_w2=l0_mlp_w2)
    p1 = dict(w_mod=l1_w_mod, b_mod=l1_b_mod, norm1_g=l1_norm1_g, norm2_g=l1_norm2_g, w_in=l1_w_in,
              conv_w=l1_conv_w, conv_b=l1_conv_b, f_w1=l1_f_w1, f_b1=l1_f_b1, f_w2=l1_f_w2, f_b2=l1_f_b2,
              f_w3=l1_f_w3, f_b3=l1_f_b3, f_w4=l1_f_w4, f_freq=l1_f_freq, hy_bias=l1_hy_bias,
              q_norm_g=l1_q_norm_g, k_norm_g=l1_k_norm_g, w_out=l1_w_out, mlp_w1=l1_mlp_w1, mlp_w2=l1_mlp_w2)
    layers = [p0, p1]
    ctx_s = ctx
    for i in range(DEPTH):
        p = layers[i]
        last = i == DEPTH - 1
        mods = jnp.split(jax.nn.silu(c) @ p['w_mod'] + p['b_mod'], N_MOD, axis=-1)
        sh1, sc1, g1, sh2, sc2, g2 = [m[:, None, :] for m in mods]
        csh1, csc1, cg1, csh2, csc2, cg2 = jnp.split(jax.nn.silu(c_ctx) @ p['w_mod'] + p['b_mod'], N_MOD)
        h_lat = modulate(x, p['norm1_g'], sh1, sc1)
        h_ctx = modulate(ctx_s, p['norm1_g'], csh1, csc1)
        if i % 2 == 0:
            o_lat, o_ctx = even_mixer(h_lat, h_ctx, p, i)
        else:
            o_lat, o_ctx = odd_mixer(h_lat, h_ctx, p, not last)
        x = x + g1 * o_lat
        x = x + g2 * mlp(modulate(x, p['norm2_g'], sh2, sc2), p['mlp_w1'], p['mlp_w2'])
        if not last:
            ctx_s = ctx_s + cg1 * o_ctx
            ctx_s = ctx_s + cg2 * mlp(modulate(ctx_s, p['norm2_g'], csh2, csc2), p['mlp_w1'], p['mlp_w2'])
    return rmsnorm(x, final_g)
```

```python
import functools
import math

import numpy as np
import jax
import jax.numpy as jnp
from jax import lax
from jax.experimental import pallas as pl
from jax.experimental.pallas import tpu as pltpu

F32 = jnp.float32
BF16 = jnp.bfloat16
HI = lax.Precision.HIGHEST

NORM_EPS = 1e-6
ROPE_THETA = 10000.0
GRID_W = 64
N_MOD = 6
A_QK_DIM = 64
A_V_DIM = 128
B_HEAD = 64
B_DECAY_RANK = 64
B_ICL_RANK = 64
B_GATE_RANK = 160
B_GN_EPS = 64e-5
HY_EMB = 33
HY_ORDER = 64
HY_TARGET = 1e-2
HY_FAST_PCT = 0.3
HY_SLOW_PCT = 1.5
D_HEAD = 128
D_KV_GROUP = 4

LANE = 128
VMEM_LIMIT = 56 * 1024 * 1024
LOG2E = 1.4426950408889634


def _cp(sem, vmem=VMEM_LIMIT):
    return pltpu.CompilerParams(dimension_semantics=sem, vmem_limit_bytes=vmem)


def _pick(n, pref, step=8):
    t = max(step, min(n, pref) // step * step)
    while n % t:
        t -= step
    return t


def _mods_body(c_ref, w_ref, b_ref, o_ref):
    c = c_ref[...]
    s = c * jax.nn.sigmoid(c)
    o_ref[...] = jnp.dot(s, w_ref[...], preferred_element_type=F32, precision=HI) + b_ref[...]


def _mods(cvec, w_mod, b_mod):
    m, d = cvec.shape
    n = w_mod.shape[1]
    tn = _pick(n, 1024, LANE)
    return pl.pallas_call(
        _mods_body,
        grid=(n // tn,),
        in_specs=[pl.BlockSpec((m, d), lambda j: (0, 0)),
                  pl.BlockSpec((d, tn), lambda j: (0, j)),
                  pl.BlockSpec((1, tn), lambda j: (0, j))],
        out_specs=pl.BlockSpec((m, tn), lambda j: (0, j)),
        out_shape=jax.ShapeDtypeStruct((m, n), F32),
        compiler_params=_cp(("arbitrary",)),
        name="mods",
    )(cvec, w_mod, b_mod.reshape(1, n))


def _norm_mod(x, a, b):
    ms = jnp.mean(x * x, axis=-1, keepdims=True)
    return x * lax.rsqrt(ms + NORM_EPS) * a + b


def _inproj_body(x_ref, a_ref, b_ref, w_ref, o_ref, xn_ref):
    @pl.when(pl.program_id(2) == 0)
    def _():
        xn_ref[...] = _norm_mod(x_ref[0], a_ref[0], b_ref[0]).astype(BF16)

    o_ref[0] = jnp.dot(xn_ref[...], w_ref[...], preferred_element_type=F32).astype(o_ref.dtype)


def _inproj(x, a, b, w, out_dtype=BF16, tm_pref=1024, tn_pref=512):
    bsz, n, d = x.shape
    nn = w.shape[1]
    tm, tn = _pick(n, tm_pref), _pick(nn, tn_pref, LANE)
    return pl.pallas_call(
        _inproj_body,
        grid=(bsz, n // tm, nn // tn),
        in_specs=[pl.BlockSpec((1, tm, d), lambda bi, i, j: (bi, i, 0)),
                  pl.BlockSpec((1, 1, d), lambda bi, i, j: (bi, 0, 0)),
                  pl.BlockSpec((1, 1, d), lambda bi, i, j: (bi, 0, 0)),
                  pl.BlockSpec((d, tn), lambda bi, i, j: (0, j))],
        out_specs=pl.BlockSpec((1, tm, tn), lambda bi, i, j: (bi, i, j)),
        out_shape=jax.ShapeDtypeStruct((bsz, n, nn), out_dtype),
        scratch_shapes=[pltpu.VMEM((tm, d), BF16)],
        compiler_params=_cp(("parallel", "parallel", "arbitrary")),
        name="inproj",
    )(x, a, b, w)


def _outproj_body(oa_ref, ob_ref, wa_ref, wb_ref, x_ref, g_ref, y_ref):
    acc = jnp.dot(oa_ref[0], wa_ref[...], preferred_element_type=F32)
    acc += jnp.dot(ob_ref[0], wb_ref[...], preferred_element_type=F32)
    y_ref[0] = x_ref[0] + g_ref[0] * acc


def _outproj(oa, ob, w, x, g, tm_pref=1024, tn_pref=512):
    bsz, n, ka = oa.shape
    kb = ob.shape[2]
    d = w.shape[1]
    tm, tn = _pick(n, tm_pref), _pick(d, tn_pref, LANE)
    return pl.pallas_call(
        _outproj_body,
        grid=(bsz, n // tm, d // tn),
        in_specs=[pl.BlockSpec((1, tm, ka), lambda bi, i, j: (bi, i, 0)),
                  pl.BlockSpec((1, tm, kb), lambda bi, i, j: (bi, i, 0)),
                  pl.BlockSpec((ka, tn), lambda bi, i, j: (0, j)),
                  pl.BlockSpec((kb, tn), lambda bi, i, j: (0, j)),
                  pl.BlockSpec((1, tm, tn), lambda bi, i, j: (bi, i, j)),
                  pl.BlockSpec((1, 1, tn), lambda bi, i, j: (bi, 0, j))],
        out_specs=pl.BlockSpec((1, tm, tn), lambda bi, i, j: (bi, i, j)),
        out_shape=jax.ShapeDtypeStruct((bsz, n, d), F32),
        compiler_params=_cp(("parallel", "parallel", "parallel")),
        name="outproj",
    )(oa, ob, w[:ka], w[ka:], x, g)


def _mlp_body(x_ref, a_ref, b_ref, g_ref, w1_ref, w2_ref, fg_ref, y_ref, xn_ref, acc_ref, *, nf, final):
    f = pl.program_id(2)

    @pl.when(f == 0)
    def _():
        xn_ref[...] = _norm_mod(x_ref[0], a_ref[0], b_ref[0]).astype(BF16)
        acc_ref[...] = jnp.zeros_like(acc_ref)

    h = jnp.dot(xn_ref[...], w1_ref[...], preferred_element_type=F32)
    h = jnp.square(jnp.maximum(h, 0.0)).astype(BF16)
    acc_ref[...] += jnp.dot(h, w2_ref[...], preferred_element_type=F32)

    @pl.when(f == nf - 1)
    def _():
        y = x_ref[0] + g_ref[0] * acc_ref[...]
        if final:
            ms = jnp.mean(y * y, axis=-1, keepdims=True)
            y = y * lax.rsqrt(ms + NORM_EPS) * fg_ref[...]
        y_ref[0] = y


def _mlp(x, a, b, g, w1, w2, final_g=None, tm_pref=512, tf_pref=512):
    bsz, n, d = x.shape
    dff = w1.shape[1]
    tm, tf = _pick(n, tm_pref), _pick(dff, tf_pref, LANE)
    nf = dff // tf
    final = final_g is not None
    fg = (final_g if final else jnp.ones((d,), F32)).reshape(1, d).astype(F32)
    return pl.pallas_call(
        functools.partial(_mlp_body, nf=nf, final=final),
        grid=(bsz, n // tm, nf),
        in_specs=[pl.BlockSpec((1, tm, d), lambda bi, i, f: (bi, i, 0)),
                  pl.BlockSpec((1, 1, d), lambda bi, i, f: (bi, 0, 0)),
                  pl.BlockSpec((1, 1, d), lambda bi, i, f: (bi, 0, 0)),
                  pl.BlockSpec((1, 1, d), lambda bi, i, f: (bi, 0, 0)),
                  pl.BlockSpec((d, tf), lambda bi, i, f: (0, f)),
                  pl.BlockSpec((tf, d), lambda bi, i, f: (f, 0)),
                  pl.BlockSpec((1, d), lambda bi, i, f: (0, 0))],
        out_specs=pl.BlockSpec((1, tm, d), lambda bi, i, f: (bi, i, 0)),
        out_shape=jax.ShapeDtypeStruct((bsz, n, d), F32),
        scratch_shapes=[pltpu.VMEM((tm, d), BF16), pltpu.VMEM((tm, d), F32)],
        compiler_params=_cp(("parallel", "parallel", "arbitrary")),
        name="mlp",
    )(x, a, b, g, w1, w2, fg)


def _rope_tables(n, dim, reps):
    rows = n // GRID_W
    row = jnp.repeat(jnp.arange(rows, dtype=F32), GRID_W)
    col = jnp.tile(jnp.arange(GRID_W, dtype=F32), rows)
    half = dim // 2
    inv = ROPE_THETA ** (-jnp.arange(0, half, 2, dtype=F32) / half)
    ar, ac = row[:, None] * inv, col[:, None] * inv
    cos = jnp.concatenate([jnp.cos(ar), jnp.cos(ar), jnp.cos(ac), jnp.cos(ac)], axis=-1)
    sin = jnp.concatenate([-jnp.sin(ar), jnp.sin(ar), -jnp.sin(ac), jnp.sin(ac)], axis=-1)
    return jnp.tile(cos, (1, reps)), jnp.tile(sin, (1, reps))


def _swap_matrix(dim, reps):
    q = dim // 4
    width = dim * reps
    p = np.zeros((width, width), np.float32)
    for j in range(width):
        base, r = (j // dim) * dim, j % dim
        axis, which, f = r // (2 * q), (r % (2 * q)) // q, r % q
        p[base + axis * 2 * q + (1 - which) * q + f, j] = 1.0
    return p


def _qkprep_body(x_ref, cos_ref, sin_ref, p_ref, g_ref, gs_ref, o_ref, *, mode, norm, rope, scale, tq):
    x = x_ref[0]
    y = x.astype(F32)
    if rope:
        ys = jnp.dot(x, p_ref[...], preferred_element_type=F32)
    if norm:
        rs = lax.rsqrt(jnp.mean(y * y, axis=-1, keepdims=True) + NORM_EPS)
        y = y * rs * g_ref[...]
        if rope:
            ys = ys * rs * gs_ref[...]
    if rope:
        y = y * cos_ref[...] + ys * sin_ref[...]
    if scale != 1.0:
        y = y * scale
    if mode == "diffq":
        lane = lax.broadcasted_iota(jnp.int32, y.shape, 1)
        o_ref[0, 0, :tq] = jnp.where(lane < A_QK_DIM, y, 0.0).astype(o_ref.dtype)
        o_ref[0, 0, tq:] = jnp.where(lane >= A_QK_DIM, y, 0.0).astype(o_ref.dtype)
    elif mode == "gqaq":
        o_ref[0, 0] = y.astype(o_ref.dtype)
    else:
        o_ref[0] = y.astype(o_ref.dtype)


def _qkprep(p, col0, heads, *, mode, dim, gain=None, rope=True, scale=1.0, tq_pref=512):
    bsz, n, _ = p.shape
    tq = _pick(n, tq_pref)
    nb = n // tq
    cb = col0 // LANE
    reps = LANE // dim
    if rope:
        cos, sin = _rope_tables(n, dim, reps)
        pm = jnp.asarray(_swap_matrix(dim, reps), BF16)
    else:
        cos = sin = jnp.zeros((n, LANE), F32)
        pm = jnp.zeros((LANE, LANE), BF16)
    norm = gain is not None
    if norm:
        g = jnp.tile(gain.astype(F32), reps).reshape(1, LANE)
        gs = jnp.dot(g, jnp.asarray(_swap_matrix(dim, reps)), precision=HI)
    else:
        g = gs = jnp.ones((1, LANE), F32)
    if mode == "diffq":
        out_shape = (bsz, heads, 2 * n, LANE)
        out_spec = pl.BlockSpec((1, 1, 2 * tq, LANE), lambda b, h, i: (b, h, i, 0))
    elif mode == "gqaq":
        out_shape = (bsz, heads // D_KV_GROUP, D_KV_GROUP * n, LANE)
        out_spec = pl.BlockSpec((1, 1, tq, LANE),
                                lambda b, h, i: (b, h // D_KV_GROUP, i * D_KV_GROUP + h % D_KV_GROUP, 0))
    else:
        out_shape = (bsz, n, heads * LANE)
        out_spec = pl.BlockSpec((1, tq, LANE), lambda b, h, i: (b, i, h))
    return pl.pallas_call(
        functools.partial(_qkprep_body, mode=mode, norm=norm, rope=rope, scale=scale, tq=tq),
        grid=(bsz, heads, nb),
        in_specs=[pl.BlockSpec((1, tq, LANE), lambda b, h, i: (b, i, cb + h)),
                  pl.BlockSpec((tq, LANE), lambda b, h, i: (i, 0)),
                  pl.BlockSpec((tq, LANE), lambda b, h, i: (i, 0)),
                  pl.BlockSpec((LANE, LANE), lambda b, h, i: (0, 0)),
                  pl.BlockSpec((1, LANE), lambda b, h, i: (0, 0)),
                  pl.BlockSpec((1, LANE), lambda b, h, i: (0, 0))],
        out_specs=out_spec,
        out_shape=jax.ShapeDtypeStruct(out_shape, BF16),
        compiler_params=_cp(("parallel", "parallel", "parallel")),
        name="qkprep_" + mode,
    )(p, cos, sin, pm, g, gs)


def _flash_body(q_ref, k_ref, v_ref, e1_ref, e2_ref, o_ref, m_ref, l_ref, acc_ref, *, nkv, ts, nsub, mode, tq):
    kv = pl.program_id(3)

    @pl.when(kv == 0)
    def _():
        m_ref[...] = jnp.full_like(m_ref, -jnp.inf)
        l_ref[...] = jnp.zeros_like(l_ref)
        acc_ref[...] = jnp.zeros_like(acc_ref)

    q = q_ref[0, 0]

    def sub(c, carry):
        off = pl.multiple_of(c * ts, ts)
        k = k_ref[0, pl.ds(off, ts), :]
        v = v_ref[0, pl.ds(off, ts), :]
        s = lax.dot_general(q, k, (((1,), (1,)), ((), ())), preferred_element_type=F32)
        m_prev = m_ref[...]
        m_new = jnp.maximum(m_prev, jnp.max(s, axis=1, keepdims=True))
        alpha = jnp.exp2(m_prev - m_new)
        p = jnp.exp2(s - jnp.concatenate([m_new] * (ts // LANE), axis=1))
        l_ref[...] = alpha * l_ref[...] + jnp.sum(p, axis=1, keepdims=True)
        acc_ref[...] = alpha * acc_ref[...] + jnp.dot(p.astype(BF16), v, preferred_element_type=F32)
        m_ref[...] = m_new
        return carry

    lax.fori_loop(0, nsub, sub, 0)

    @pl.when(kv == nkv - 1)
    def _():
        o = acc_ref[...] / l_ref[...]
        if mode == "diff":
            d = o[:tq] - e1_ref[...] * o[tq:]
            ms = jnp.mean(d * d, axis=-1, keepdims=True)
            o_ref[0] = (d * lax.rsqrt(ms + NORM_EPS) * e2_ref[...]).astype(o_ref.dtype)
        else:
            for j in range(D_KV_GROUP):
                o_ref[0, :, j * LANE:(j + 1) * LANE] = o[j * tq:(j + 1) * tq].astype(o_ref.dtype)


def _flash(q, k, v, kcol0, vcol0, e1, e2, *, mode, n, tq, tk_pref=3328, ts=256):
    bsz, hk, _, _ = q.shape
    nk = k.shape[1]
    g = 2 if mode == "diff" else D_KV_GROUP
    mq = g * tq
    tk = ts
    for cand in range(ts, min(nk, tk_pref) + 1, ts):
        if nk % cand == 0:
            tk = cand
    nkv = nk // tk
    kb, vb = kcol0 // LANE, vcol0 // LANE
    ow = LANE if mode == "diff" else D_KV_GROUP * LANE
    return pl.pallas_call(
        functools.partial(_flash_body, nkv=nkv, ts=ts, nsub=tk // ts, mode=mode, tq=tq),
        grid=(bsz, hk, n // tq, nkv),
        in_specs=[pl.BlockSpec((1, 1, mq, LANE), lambda b, h, i, j: (b, h, i, 0)),
                  pl.BlockSpec((1, tk, LANE), lambda b, h, i, j: (b, j, kb + h)),
                  pl.BlockSpec((1, tk, LANE), lambda b, h, i, j: (b, j, vb + h)),
                  pl.BlockSpec((1, LANE), lambda b, h, i, j: (0, 0)),
                  pl.BlockSpec((1, LANE), lambda b, h, i, j: (0, 0))],
        out_specs=pl.BlockSpec((1, tq, ow), lambda b, h, i, j: (b, i, h)),
        out_shape=jax.ShapeDtypeStruct((bsz, n, hk * ow), BF16),
        scratch_shapes=[pltpu.VMEM((mq, LANE), F32), pltpu.VMEM((mq, LANE), F32), pltpu.VMEM((mq, LANE), F32)],
        compiler_params=_cp(("parallel", "parallel", "parallel", "arbitrary")),
        name="flash_" + mode,
    )(q, k, v, e1, e2)


def _diff_attention(pa_lat, pa_ctx, lam, lam_init, subln_g):
    n, nctx = pa_lat.shape[1], pa_ctx.shape[1]
    aw = pa_lat.shape[2] // 3
    heads = aw // LANE
    scale = A_QK_DIM ** -0.5 * LOG2E
    tq_l, tq_c = _pick(n, 512), _pick(nctx, 512)
    q_l = _qkprep(pa_lat, 0, heads, mode="diffq", dim=A_QK_DIM, scale=scale, tq_pref=tq_l)
    k_l = _qkprep(pa_lat, aw, heads, mode="tok", dim=A_QK_DIM)
    q_c = _qkprep(pa_ctx, 0, heads, mode="diffq", dim=A_QK_DIM, rope=False, scale=scale, tq_pref=tq_c)
    k_all = jnp.concatenate([k_l, pa_ctx[..., aw:2 * aw]], axis=1)
    v_all = jnp.concatenate([pa_lat[..., 2 * aw:], pa_ctx[..., 2 * aw:]], axis=1)
    e1 = jnp.full((1, LANE), lam, F32)
    e2 = (subln_g.astype(F32) * (1.0 - lam_init)).reshape(1, LANE)
    o_lat = _flash(q_l, k_all, v_all, 0, 0, e1, e2, mode="diff", n=n, tq=tq_l)
    o_ctx = _flash(q_c, pa_ctx, pa_ctx, aw, 2 * aw, e1, e2, mode="diff", n=nctx, tq=tq_c)
    return o_lat, o_ctx


def _gqa_attention(p_lat, pkv_ctx, qcol0, q_norm_g, k_norm_g):
    n = p_lat.shape[1]
    hq, hk = 2 * D_KV_GROUP, 2
    kcol0 = qcol0 + hq * D_HEAD
    vcol0 = kcol0 + hk * D_HEAD
    scale = D_HEAD ** -0.5 * LOG2E
    tq = _pick(n, 256)
    q = _qkprep(p_lat, qcol0, hq, mode="gqaq", dim=D_HEAD, gain=q_norm_g, scale=scale, tq_pref=tq)
    k_l = _qkprep(p_lat, kcol0, hk, mode="tok", dim=D_HEAD, gain=k_norm_g)
    k_c = _qkprep(pkv_ctx, 0, hk, mode="tok", dim=D_HEAD, gain=k_norm_g, rope=False)
    k_all = jnp.concatenate([k_l, k_c], axis=1)
    v_all = jnp.concatenate([p_lat[..., vcol0:vcol0 + hk * D_HEAD], pkv_ctx[..., hk * D_HEAD:]], axis=1)
    dummy = jnp.zeros((1, LANE), F32)
    return _flash(q, k_all, v_all, 0, 0, dummy, dummy, mode="gqa", n=n, tq=tq)


RW_CHUNK = 64
RW_HALO = 8


def _mm_hi(a, b):
    return jnp.dot(a, b, preferred_element_type=F32, precision=HI)


def _rwprep_body(x_ref, hp_ref, hn_ref, mup_ref, mun_ref, vec_ref, w2f_ref, w2b_ref, a2f_ref, a2b_ref,
                 g2_ref, sel_ref, selt_ref,
                 r_ref, v_ref, kk_ref, lwf_ref, bf_ref, kf_ref, lwb_ref, bb_ref, kb_ref, g_ref, bonus_ref,
                 *, nb, tm, bw):
    i = pl.program_id(1)
    x = x_ref[0].astype(F32)
    row = lax.broadcasted_iota(jnp.int32, x.shape, 0)
    prev_edge = jnp.where(i > 0, hp_ref[0, RW_HALO - 1:RW_HALO, :].astype(F32), 0.0)
    next_edge = jnp.where(i < nb - 1, hn_ref[0, 0:1, :].astype(F32), 0.0)
    prev = jnp.where(row == 0, prev_edge, pltpu.roll(x, 1, 0))
    nxt = jnp.where(row == tm - 1, next_edge, pltpu.roll(x, tm - 1, 0))
    xs = x + mup_ref[...] * (prev - x) + mun_ref[...] * (nxt - x)
    r, k, v = xs[:, :bw], xs[:, bw:2 * bw], xs[:, 2 * bw:3 * bw]
    lr = xs[:, 3 * bw:3 * bw + LANE]
    gd = xs[:, 3 * bw + LANE:3 * bw + 3 * LANE]
    k_k, k_a, r_k = vec_ref[0:1], vec_ref[1:2], vec_ref[2:3]
    w0f, a0f, w0b, a0b = vec_ref[3:4], vec_ref[4:5], vec_ref[5:6], vec_ref[6:7]
    sel, selt = sel_ref[...], selt_ref[...]

    def head_sum(z):
        return _mm_hi(_mm_hi(z, sel), selt)

    kk = k * k_k
    kk = kk * lax.rsqrt(head_sum(kk * kk) + 1e-12)
    g_ref[0] = _mm_hi(jax.nn.sigmoid(gd), g2_ref[...]).astype(g_ref.dtype)
    th = jnp.tanh(lr)
    ksum = jnp.zeros_like(k)
    for (w0, a0, w2_ref, a2_ref, lw_ref, b_ref, kd_ref) in (
            (w0f, a0f, w2f_ref, a2f_ref, lwf_ref, bf_ref, kf_ref),
            (w0b, a0b, w2b_ref, a2b_ref, lwb_ref, bb_ref, kb_ref)):
        w_log = -jax.nn.softplus(-(w0 + _mm_hi(th, w2_ref[...]))) - 0.5
        logw = -jnp.exp(w_log)
        a = jax.nn.sigmoid(a0 + _mm_hi(lr, a2_ref[...]))
        k_d = k * (1.0 + (a - 1.0) * k_a)
        ksum = ksum + k_d
        bvec = kk * a
        for h in range(bw // B_HEAD):
            sl = slice(h * B_HEAD, (h + 1) * B_HEAD)
            lw_ref[0, h] = logw[:, sl]
            b_ref[0, h] = bvec[:, sl]
            kd_ref[0, h] = k_d[:, sl]
    for h in range(bw // B_HEAD):
        sl = slice(h * B_HEAD, (h + 1) * B_HEAD)
        r_ref[0, h] = r[:, sl]
        v_ref[0, h] = v[:, sl]
        kk_ref[0, h] = kk[:, sl]
    bonus_ref[0] = head_sum(r * ksum * r_k) * v


def _rwkv_prepare(pr, p):
    bsz, n, cols = pr.shape
    bw = p["w0_f"].shape[0]
    nh = bw // B_HEAD
    tm = _pick(n, 256)
    nb = n // tm
    hb = tm // RW_HALO
    pad = cols - 3 * bw

    def padded(vec):
        return jnp.concatenate([vec.astype(F32), jnp.zeros((cols - vec.shape[0],), F32)]).reshape(1, cols)

    vecs = jnp.stack([p["k_k"], p["k_a"], p["r_k"].reshape(-1), p["w0_f"], p["a0_f"], p["w0_b"], p["a0_b"],
                      jnp.zeros((bw,), F32)]).astype(F32)
    zr = jnp.zeros((B_DECAY_RANK, bw), F32)
    w2 = {d: jnp.concatenate([p["w2_" + d].astype(F32), zr], axis=0) for d in "fb"}
    a2 = {d: jnp.concatenate([zr, p["a2_" + d].astype(F32)], axis=0) for d in "fb"}
    g2 = jnp.concatenate([p["g2"].astype(F32), jnp.zeros((2 * LANE - B_GATE_RANK, bw), F32)], axis=0)
    sel_np = np.zeros((bw, LANE), np.float32)
    sel_np[np.arange(bw), np.arange(bw) // B_HEAD] = 1.0
    sel, selt = jnp.asarray(sel_np), jnp.asarray(sel_np.T)
    assert pad == 3 * LANE and B_DECAY_RANK + B_ICL_RANK == LANE

    full = lambda shape: pl.BlockSpec(shape, lambda b, i: tuple(0 for _ in shape))
    hm = pl.BlockSpec((1, nh, tm, B_HEAD), lambda b, i: (b, 0, i, 0))
    tok = pl.BlockSpec((1, tm, bw), lambda b, i: (b, i, 0))
    hm_shape = jax.ShapeDtypeStruct((bsz, nh, n, B_HEAD), F32)
    outs = pl.pallas_call(
        functools.partial(_rwprep_body, nb=nb, tm=tm, bw=bw),
        grid=(bsz, nb),
        in_specs=[pl.BlockSpec((1, tm, cols), lambda b, i: (b, i, 0)),
                  pl.BlockSpec((1, RW_HALO, cols), lambda b, i: (b, jnp.maximum(i * hb - 1, 0), 0)),
                  pl.BlockSpec((1, RW_HALO, cols), lambda b, i: (b, jnp.minimum((i + 1) * hb, nb * hb - 1), 0)),
                  full((1, cols)), full((1, cols)), full((8, bw)),
                  full((LANE, bw)), full((LANE, bw)), full((LANE, bw)), full((LANE, bw)),
                  full((2 * LANE, bw)), full((bw, LANE)), full((LANE, bw))],
        out_specs=[hm] * 9 + [tok, tok],
        out_shape=[hm_shape] * 9 + [jax.ShapeDtypeStruct((bsz, n, bw), BF16),
                                    jax.ShapeDtypeStruct((bsz, n, bw), F32)],
        compiler_params=_cp(("parallel", "parallel")),
        name="rwkv_prepare",
    )(pr, pr, pr, padded(p["mu_prev"]), padded(p["mu_next"]), vecs, w2["f"], w2["b"], a2["f"], a2["b"],
      g2, sel, selt)
    r, v, kk, lwf, bf, kf, lwb, bb, kb, g, bonus = outs
    return dict(r=r, v=v, kk=kk, g=g, bonus=bonus, f=(lwf, bf, kf), b=(lwb, bb, kb))


def _bmm(a, b, dims):
    return jnp.einsum(dims, a, b, preferred_element_type=F32, precision=HI)


def _rwscan_body(r_ref, v_ref, kk_ref, b_ref, kd_ref, lw_ref, s0_ref, y_ref, sT_ref, s_ref, *, reverse, nc, ck):
    c = pl.program_id(1)

    @pl.when(c == 0)
    def _():
        s_ref[...] = s0_ref[0]

    r, v, kk, b, kd, lw = r_ref[0], v_ref[0], kk_ref[0], b_ref[0], kd_ref[0], lw_ref[0]
    nh = r.shape[0]
    ti = lax.broadcasted_iota(jnp.int32, (ck, ck), 0)
    si = lax.broadcasted_iota(jnp.int32, (ck, ck), 1)
    incl = (si >= ti) if reverse else (si <= ti)
    strict = (si > ti) if reverse else (si < ti)
    tri = jnp.broadcast_to(incl.astype(F32)[None], (nh, ck, ck))
    cum = _bmm(tri, lw, "hts,hsj->htj")
    total = cum[:, 0:1, :] if reverse else cum[:, ck - 1:ck, :]
    at = -kk * jnp.exp(cum - lw)
    winv = jnp.exp(-cum)
    bt, kt = b * winv, kd * winv
    rt = r * jnp.exp(cum)
    wrest = jnp.exp(total - cum)
    btw, ktw = b * wrest, kd * wrest

    def masked(m, z):
        return jnp.where(m[None], z, 0.0)

    lab = masked(strict, _bmm(at, bt, "htj,hsj->hts"))
    lak = masked(strict, _bmm(at, kt, "htj,hsj->hts"))
    mrb = masked(incl, _bmm(rt, bt, "htj,hsj->hts"))
    mrk = masked(incl, _bmm(rt, kt, "htj,hsj->hts"))
    eye = jnp.broadcast_to((ti == si).astype(F32)[None], (nh, ck, ck))
    tinv, pw = eye + lab, lab
    span = 2
    while span < ck:
        pw = _bmm(pw, pw, "hts,hsu->htu")
        tinv = tinv + _bmm(tinv, pw, "hts,hsu->htu")
        span *= 2
    p1 = _bmm(tinv, at, "hts,hsj->htj")
    q1 = _bmm(tinv, _bmm(lak, v, "hts,hsi->hti"), "hts,hsi->hti")
    s = s_ref[...]
    u = _bmm(p1, s, "htj,hij->hti") + q1
    y_ref[0] = _bmm(rt, s, "htj,hij->hti") + _bmm(mrb, u, "hts,hsi->hti") + _bmm(mrk, v, "hts,hsi->hti")
    s_new = s * jnp.exp(total) + _bmm(u, btw, "hti,htj->hij") + _bmm(v, ktw, "hti,htj->hij")
    s_ref[...] = s_new

    @pl.when(c == nc - 1)
    def _():
        sT_ref[0] = s_new


def _rwkv_scan(r, v, kk, b, kd, lw, s0, reverse):
    bsz, nh, n, hd = r.shape
    ck = min(RW_CHUNK, n)
    nc = n // ck
    cidx = (lambda b_, c: (b_, 0, nc - 1 - c, 0)) if reverse else (lambda b_, c: (b_, 0, c, 0))
    seq = pl.BlockSpec((1, nh, ck, hd), cidx)
    st = pl.BlockSpec((1, nh, hd, hd), lambda b_, c: (b_, 0, 0, 0))
    return pl.pallas_call(
        functools.partial(_rwscan_body, reverse=reverse, nc=nc, ck=ck),
        grid=(bsz, nc),
        in_specs=[seq] * 6 + [st],
        out_specs=[seq, st],
        out_shape=[jax.ShapeDtypeStruct((bsz, nh, n, hd), F32), jax.ShapeDtypeStruct((bsz, nh, hd, hd), F32)],
        scratch_shapes=[pltpu.VMEM((nh, hd, hd), F32)],
        compiler_params=_cp(("parallel", "arbitrary")),
        name="rwkv_scan_" + ("bwd" if reverse else "fwd"),
    )(r, v, kk, b, kd, lw, s0)


def _rwpost_body(yf_ref, yb_ref, g_ref, bonus_ref, lg_ref, lb_ref, o_ref):
    y = yf_ref[0] + yb_ref[0]
    mu = jnp.mean(y, axis=-1, keepdims=True)
    var = jnp.mean(jnp.square(y - mu), axis=-1, keepdims=True)
    y = (y - mu) * lax.rsqrt(var + B_GN_EPS)
    ytok = jnp.concatenate([y[h] for h in range(y.shape[0])], axis=-1)
    ytok = ytok * lg_ref[...] + lb_ref[...] + bonus_ref[0]
    o_ref[0] = (ytok * g_ref[0].astype(F32)).astype(o_ref.dtype)


def _rwkv_post(y_f, y_b, g, bonus, lnx_g, lnx_b):
    bsz, nh, n, hd = y_f.shape
    bw = nh * hd
    tm = _pick(n, 256)
    hm = pl.BlockSpec((1, nh, tm, hd), lambda b, i: (b, 0, i, 0))
    tok = pl.BlockSpec((1, tm, bw), lambda b, i: (b, i, 0))
    vec = pl.BlockSpec((1, bw), lambda b, i: (0, 0))
    return pl.pallas_call(
        _rwpost_body,
        grid=(bsz, n // tm),
        in_specs=[hm, hm, tok, tok, vec, vec],
        out_specs=tok,
        out_shape=jax.ShapeDtypeStruct((bsz, n, bw), BF16),
        compiler_params=_cp(("parallel", "parallel")),
        name="rwkv_post",
    )(y_f, y_b, g, bonus, lnx_g.reshape(1, bw).astype(F32), lnx_b.reshape(1, bw).astype(F32))


def _rwkv_run(prep, states0, p):
    (lwf, bf, kf), (lwb, bb, kb) = prep["f"], prep["b"]
    y_f, s_f = _rwkv_scan(prep["r"], prep["v"], prep["kk"], bf, kf, lwf, states0[0], False)
    y_b, s_b = _rwkv_scan(prep["r"], prep["v"], prep["kk"], bb, kb, lwb, states0[1], True)
    return _rwkv_post(y_f, y_b, prep["g"], prep["bonus"], p["lnx_g"], p["lnx_b"]), (s_f, s_b)


HY_N2 = 128
HY_HALO = 8


def _hypre_body(x_ref, hp_ref, hn_ref, cw_ref, cb_ref, x0_ref, uv_ref, *, nb, tm, cw):
    i = pl.program_id(1)
    x = x_ref[0].astype(F32)
    row = lax.broadcasted_iota(jnp.int32, x.shape, 0)
    prev_edge = jnp.where(i > 0, hp_ref[0, HY_HALO - 1:HY_HALO, :].astype(F32), 0.0)
    next_edge = jnp.where(i < nb - 1, hn_ref[0, 0:1, :].astype(F32), 0.0)
    prev = jnp.where(row == 0, prev_edge, pltpu.roll(x, 1, 0))
    nxt = jnp.where(row == tm - 1, next_edge, pltpu.roll(x, tm - 1, 0))
    u = prev * cw_ref[0:1] + x * cw_ref[1:2] + nxt * cw_ref[2:3] + cb_ref[...]
    x0_ref[0] = u[:, :cw]
    uv_ref[0] = u[:, 2 * cw:3 * cw] * u[:, cw:2 * cw]


def _hyena_pre(p, conv_w, conv_b):
    bsz, n, _ = p.shape
    hc = conv_w.shape[1]
    cw = hc // 3
    tm = _pick(n, 256)
    nb = n // tm
    hb = tm // HY_HALO
    cwp = jnp.concatenate([conv_w.astype(F32), jnp.zeros((5, hc), F32)], axis=0)
    out = pl.BlockSpec((1, tm, cw), lambda b, i: (b, i, 0))
    return pl.pallas_call(
        functools.partial(_hypre_body, nb=nb, tm=tm, cw=cw),
        grid=(bsz, nb),
        in_specs=[pl.BlockSpec((1, tm, hc), lambda b, i: (b, i, 0)),
                  pl.BlockSpec((1, HY_HALO, hc), lambda b, i: (b, jnp.maximum(i * hb - 1, 0), 0)),
                  pl.BlockSpec((1, HY_HALO, hc), lambda b, i: (b, jnp.minimum((i + 1) * hb, nb * hb - 1), 0)),
                  pl.BlockSpec((8, hc), lambda b, i: (0, 0)),
                  pl.BlockSpec((1, hc), lambda b, i: (0, 0))],
        out_specs=[out, out],
        out_shape=[jax.ShapeDtypeStruct((bsz, n, cw), F32)] * 2,
        compiler_params=_cp(("parallel", "parallel")),
        name="hyena_pre",
    )(p, p, p, cwp, conv_b.reshape(1, hc).astype(F32))


def _hyfilt_body(z_ref, t_ref, z0_ref, w1_ref, w2_ref, w3_ref, w4_ref, w4b_ref, vec_ref, dl_ref,
                 k_ref, s_ref, *, half_tiles):
    i = pl.program_id(0)
    b1, b2, b3, fr = vec_ref[0:1], vec_ref[1:2], vec_ref[2:3], vec_ref[3:4]

    def mlp3(z):
        h = jnp.sin(fr * (_mm_hi(z, w1_ref[...]) + b1))
        h = jnp.sin(fr * (_mm_hi(h, w2_ref[...]) + b2))
        return jnp.sin(fr * (_mm_hi(h, w3_ref[...]) + b3))

    raw = _mm_hi(mlp3(z_ref[...]), w4_ref[...]) * jnp.exp(-t_ref[...] * dl_ref[...])

    @pl.when(i == 0)
    def _():
        s_ref[...] = jnp.zeros_like(s_ref)

    s_ref[...] += jnp.sum(jnp.abs(raw), axis=0, keepdims=True)
    hb0 = _mm_hi(mlp3(z0_ref[...]), w4b_ref[...])
    row = lax.broadcasted_iota(jnp.int32, raw.shape, 0)
    raw = raw + jnp.where((row == 0) & (i == 0), hb0[0:1], 0.0)
    k_ref[...] = jnp.where((row == 0) & (i == half_tiles), 0.0, raw)


def _hyena_kernel_taps(n, p):
    cw = p["f_w4"].shape[1] // 2
    f32 = lambda a: a.astype(F32)
    t = jnp.linspace(0.0, 1.0, n, dtype=F32)[:, None]
    bands = (HY_EMB - 1) // 2
    ang = (2 * math.pi / n) * jnp.arange(n, dtype=F32)[:, None] * jnp.linspace(1e-4, bands - 1, bands, dtype=F32)[None]
    z = jnp.concatenate([t, jnp.cos(ang), -jnp.sin(ang), jnp.zeros((n, HY_ORDER - HY_EMB), F32)], axis=-1)
    pos = np.concatenate([np.arange(n), [0], np.arange(n - 1, 0, -1)])
    zz, tt = z[pos], t[pos]
    w1 = jnp.concatenate([f32(p["f_w1"]), jnp.zeros((HY_ORDER - HY_EMB, HY_ORDER), F32)], axis=0)
    vecs = jnp.stack([f32(p["f_b1"]), f32(p["f_b2"]), f32(p["f_b3"]), f32(p["f_freq"])]
                     + [jnp.zeros((HY_ORDER,), F32)] * 4)
    deltas = jnp.abs(jnp.linspace(math.log(HY_TARGET) / HY_SLOW_PCT, math.log(HY_TARGET) / HY_FAST_PCT, cw,
                                  dtype=F32)).reshape(1, cw)
    tr = _pick(n, 512)
    half_tiles = n // tr
    full = lambda shape: pl.BlockSpec(shape, lambda i: tuple(0 for _ in shape))
    w4 = f32(p["f_w4"])
    return pl.pallas_call(
        functools.partial(_hyfilt_body, half_tiles=half_tiles),
        grid=(2 * half_tiles,),
        in_specs=[pl.BlockSpec((tr, HY_ORDER), lambda i: (i, 0)),
                  pl.BlockSpec((tr, 1), lambda i: (i, 0)),
                  full((8, HY_ORDER)), full((HY_ORDER, HY_ORDER)), full((HY_ORDER, HY_ORDER)),
                  full((HY_ORDER, HY_ORDER)),
                  pl.BlockSpec((HY_ORDER, cw), lambda i: (0, i // half_tiles)),
                  pl.BlockSpec((HY_ORDER, cw), lambda i: (0, 1)),
                  full((8, HY_ORDER)), full((1, cw))],
        out_specs=[pl.BlockSpec((tr, cw), lambda i: (i, 0)), pl.BlockSpec((1, cw), lambda i: (0, 0))],
        out_shape=[jax.ShapeDtypeStruct((2 * n, cw), F32), jax.ShapeDtypeStruct((1, cw), F32)],
        compiler_params=_cp(("arbitrary",)),
        name="hyena_filter",
    )(zz, tt, jnp.broadcast_to(z[0:1], (8, HY_ORDER)), w1, f32(p["f_w2"]), f32(p["f_w3"]), w4, w4, vecs, deltas)


def _dft_consts(n):
    nn = 2 * n
    n2 = HY_N2
    n1 = nn // n2
    a1 = 2 * np.pi * np.outer(np.arange(n1), np.arange(n1)) / n1
    a2 = 2 * np.pi * np.outer(np.arange(n2), np.arange(n2)) / n2
    f1 = np.concatenate([np.cos(a1), -np.sin(a1)], axis=0)
    f1inv = np.concatenate([np.cos(a1), -np.sin(a1)], axis=1) / nn
    c2, s2 = np.cos(a2), -np.sin(a2)
    m2 = np.block([[c2, -s2], [s2, c2]])
    m2inv = np.block([[c2, s2], [-s2, c2]])
    at = 2 * np.pi * np.outer(np.arange(n2), np.arange(n1)) / nn
    tw = np.stack([np.cos(at), -np.sin(at)])
    c = lambda a: jnp.asarray(a.astype(np.float32))
    return dict(n1=n1, n2=n2, f1=c(f1), f1inv=c(f1inv), m2=c(m2), m2inv=c(m2inv),
                tw_s2=c(tw[:, :, :, None]),
                tw_f1=c(np.transpose(tw, (0, 2, 1))[:, :, :, None]))


def _dft1_body(f_ref, x_ref, tw_ref, o_ref, *, n1):
    a = _mm_hi(f_ref[...], x_ref[0])
    ar, ai = a[:n1], a[n1:]
    twr, twi = tw_ref[0, 0], tw_ref[1, 0]
    o_ref[0, 0] = ar * twr - ai * twi
    o_ref[0, 1] = ar * twi + ai * twr


def _dft_stage1(x, consts, rows):
    bsz, _, cw = x.shape
    n1, n2 = consts["n1"], consts["n2"]
    xv = x.reshape(bsz, rows, n2 * cw)
    f1 = consts["f1"][:, :rows]
    return pl.pallas_call(
        functools.partial(_dft1_body, n1=n1),
        grid=(bsz, n2),
        in_specs=[pl.BlockSpec((2 * n1, rows), lambda b, j: (0, 0)),
                  pl.BlockSpec((1, rows, cw), lambda b, j: (b, 0, j)),
                  pl.BlockSpec((2, 1, n1, 1), lambda b, j: (0, j, 0, 0))],
        out_specs=pl.BlockSpec((1, 2, n1, cw), lambda b, j: (b, 0, 0, j)),
        out_shape=jax.ShapeDtypeStruct((bsz, 2, n1, n2 * cw), F32),
        compiler_params=_cp(("parallel", "parallel")),
        name="hyena_dft1",
    )(f1, xv, consts["tw_s2"])


def _dftmid_body(a_ref, m2_ref, m2i_ref, h_ref, tw_ref, is_ref, o_ref, *, n2, filt):
    a = a_ref[0, :, 0].reshape(2 * n2, a_ref.shape[-1])
    x = _mm_hi(m2_ref[...], a)
    xr, xi = x[:n2], x[n2:]
    if filt:
        o_ref[0, 0, 0] = xr * is_ref[...]
        o_ref[0, 1, 0] = xi * is_ref[...]
        return
    hr, hi = h_ref[0, 0, 0], h_ref[0, 1, 0]
    y = jnp.concatenate([xr * hr - xi * hi, xr * hi + xi * hr], axis=0)
    z = _mm_hi(m2i_ref[...], y)
    zr, zi = z[:n2], z[n2:]
    twr, twi = tw_ref[0, 0], tw_ref[1, 0]
    o_ref[0, 0, 0] = zr * twr + zi * twi
    o_ref[0, 1, 0] = zi * twr - zr * twi


def _dft_mid(a, h, inv_s, consts, filt):
    bsz = a.shape[0]
    n1, n2 = consts["n1"], consts["n2"]
    cw = a.shape[-1] // n2
    av = a.reshape(bsz, 2, n1, n2, cw)
    blk = pl.BlockSpec((1, 2, 1, n2, cw), lambda b, f: (b, 0, f, 0, 0))
    hblk = pl.BlockSpec((1, 2, 1, n2, cw), (lambda b, f: (0, 0, 0, 0, 0)) if filt else (lambda b, f: (0, 0, f, 0, 0)))
    return pl.pallas_call(
        functools.partial(_dftmid_body, n2=n2, filt=filt),
        grid=(bsz, n1),
        in_specs=[blk,
                  pl.BlockSpec((2 * n2, 2 * n2), lambda b, f: (0, 0)),
                  pl.BlockSpec((2 * n2, 2 * n2), lambda b, f: (0, 0)),
                  hblk,
                  pl.BlockSpec((2, 1, n2, 1), lambda b, f: (0, f, 0, 0)),
                  pl.BlockSpec((1, cw), lambda b, f: (0, 0))],
        out_specs=blk,
        out_shape=jax.ShapeDtypeStruct((bsz, 2, n1, n2, cw), F32),
        compiler_params=_cp(("parallel", "parallel")),
        name="hyena_dftmid_" + ("filter" if filt else "conv"),
    )(av, consts["m2"], consts["m2inv"], h, consts["tw_f1"], inv_s)


def _dft3_body(f_ref, z_ref, x0_ref, uv_ref, bias_ref, o_ref):
    y = _mm_hi(f_ref[...], z_ref[0])
    o_ref[0] = (x0_ref[0] * (y + uv_ref[0] * bias_ref[...])).astype(o_ref.dtype)


def _dft_stage1_inv(z, x0, uv, bias, consts):
    bsz, n, cw = x0.shape
    n1, n2 = consts["n1"], consts["n2"]
    hr = n1 // 2
    zv = z.reshape(bsz, 2 * n1, n2 * cw)
    tokv = lambda t: t.reshape(bsz, hr, n2 * cw)
    tile = pl.BlockSpec((1, hr, cw), lambda b, j: (b, 0, j))
    out = pl.pallas_call(
        _dft3_body,
        grid=(bsz, n2),
        in_specs=[pl.BlockSpec((hr, 2 * n1), lambda b, j: (0, 0)),
                  pl.BlockSpec((1, 2 * n1, cw), lambda b, j: (b, 0, j)),
                  tile, tile,
                  pl.BlockSpec((1, cw), lambda b, j: (0, 0))],
        out_specs=tile,
        out_shape=jax.ShapeDtypeStruct((bsz, hr, n2 * cw), BF16),
        compiler_params=_cp(("parallel", "parallel")),
        name="hyena_dft3",
    )(consts["f1inv"][:hr], zv, tokv(x0), tokv(uv), bias.reshape(1, cw).astype(F32))
    return out.reshape(bsz, n, cw)


def _hyena(p_lat, p):
    n = p_lat.shape[1]
    consts = _dft_consts(n)
    n1 = consts["n1"]
    x0, uv = _hyena_pre(p_lat, p["conv_w"], p["conv_b"])
    taps, sabs = _hyena_kernel_taps(n, p)
    cw = taps.shape[1]
    ones = jnp.ones((1, cw), F32)
    hk = _dft_stage1(taps[None], consts, n1)
    hspec = _dft_mid(hk, jnp.zeros((1, 2, 1, HY_N2, cw), F32), 1.0 / sabs, consts, True)
    a = _dft_stage1(uv, consts, n1 // 2)
    z = _dft_mid(a, hspec, ones, consts, False)
    return _dft_stage1_inv(z, x0, uv, p["hy_bias"], consts)


def _mod_vectors(c, c_ctx, p):
    bsz, d = c.shape
    cvec = jnp.concatenate([c, c_ctx[None], jnp.zeros((8 - bsz - 1, d), c.dtype)], axis=0).astype(F32)
    m = _mods(cvec, p["w_mod"].astype(F32), p["b_mod"].astype(F32))
    lat = [v[:, None, :] for v in jnp.split(m[:bsz], N_MOD, axis=-1)]
    cx = [jnp.broadcast_to(v[:, None, :], (bsz, 1, d)) for v in jnp.split(m[bsz:bsz + 1], N_MOD, axis=-1)]
    return lat, cx


def _affine(norm_g, shift, scale):
    return norm_g.astype(F32) * (1.0 + scale), shift


def kernel(x, c, ctx, c_ctx,
           l0_w_mod, l0_b_mod, l0_norm1_g, l0_norm2_g, l0_w_in, l0_lam_q1, l0_lam_k1, l0_lam_q2, l0_lam_k2,
           l0_subln_g, l0_mu_prev, l0_mu_next, l0_w0_f, l0_w2_f, l0_a0_f, l0_a2_f, l0_w0_b, l0_w2_b, l0_a0_b,
           l0_a2_b, l0_g2, l0_k_k, l0_k_a, l0_r_k, l0_lnx_g, l0_lnx_b, l0_w_out, l0_mlp_w1, l0_mlp_w2,
           l1_w_mod, l1_b_mod, l1_norm1_g, l1_norm2_g, l1_w_in, l1_conv_w, l1_conv_b, l1_f_w1, l1_f_b1,
           l1_f_w2, l1_f_b2, l1_f_w3, l1_f_b3, l1_f_w4, l1_f_freq, l1_hy_bias, l1_q_norm_g, l1_k_norm_g,
           l1_w_out, l1_mlp_w1, l1_mlp_w2, final_g):
    bsz, n, d = x.shape
    x = x.astype(F32)
    ctx_s = ctx.astype(F32)
    bf = lambda w: w.astype(BF16)

    p0 = dict(w_mod=l0_w_mod, b_mod=l0_b_mod, mu_prev=l0_mu_prev, mu_next=l0_mu_next, w0_f=l0_w0_f, w2_f=l0_w2_f,
              a0_f=l0_a0_f, a2_f=l0_a2_f, w0_b=l0_w0_b, w2_b=l0_w2_b, a0_b=l0_a0_b, a2_b=l0_a2_b, g2=l0_g2,
              k_k=l0_k_k, k_a=l0_k_a, r_k=l0_r_k, lnx_g=l0_lnx_g, lnx_b=l0_lnx_b)
    (sh1, sc1, g1, sh2, sc2, g2), (csh1, csc1, cg1, csh2, csc2, cg2) = _mod_vectors(c, c_ctx, p0)
    aw = 3 * (d // 2)
    w_attn = bf(l0_w_in[:, :aw])
    w_rw = l0_w_in[:, aw:]
    rw_cols = 3 * l0_w0_f.shape[0] + 3 * LANE
    w_rw = bf(jnp.concatenate([w_rw, jnp.zeros((d, rw_cols - w_rw.shape[1]), w_rw.dtype)], axis=1))
    a1, b1 = _affine(l0_norm1_g, sh1, sc1)
    ca1, cb1 = _affine(l0_norm1_g, csh1, csc1)
    pa_lat, pr_lat = _inproj(x, a1, b1, w_attn), _inproj(x, a1, b1, w_rw)
    pa_ctx, pr_ctx = _inproj(ctx_s, ca1, cb1, w_attn), _inproj(ctx_s, ca1, cb1, w_rw)
    lam_init = 0.8 - 0.6 * math.exp(-0.3 * 0)
    lam = (jnp.exp(jnp.sum(l0_lam_q1 * l0_lam_k1).astype(F32))
           - jnp.exp(jnp.sum(l0_lam_q2 * l0_lam_k2).astype(F32)) + lam_init)
    oa_lat, oa_ctx = _diff_attention(pa_lat, pa_ctx, lam, lam_init, l0_subln_g)
    nh, hd = l0_r_k.shape
    zero = jnp.zeros((bsz, nh, hd, hd), F32)
    y_ctx, states_c = _rwkv_run(_rwkv_prepare(pr_ctx, p0), (zero, zero), p0)
    y_lat, _ = _rwkv_run(_rwkv_prepare(pr_lat, p0), states_c, p0)
    w_out0 = bf(l0_w_out)
    x = _outproj(oa_lat, y_lat, w_out0, x, g1)
    ctx_s = _outproj(oa_ctx, y_ctx, w_out0, ctx_s, cg1)
    w1, w2 = bf(l0_mlp_w1), bf(l0_mlp_w2)
    a2, b2 = _affine(l0_norm2_g, sh2, sc2)
    ca2, cb2 = _affine(l0_norm2_g, csh2, csc2)
    x = _mlp(x, a2, b2, g2, w1, w2)
    ctx_s = _mlp(ctx_s, ca2, cb2, cg2, w1, w2)

    p1 = dict(w_mod=l1_w_mod, b_mod=l1_b_mod, conv_w=l1_conv_w, conv_b=l1_conv_b, f_w1=l1_f_w1, f_b1=l1_f_b1,
              f_w2=l1_f_w2, f_b2=l1_f_b2, f_w3=l1_f_w3, f_b3=l1_f_b3, f_w4=l1_f_w4, f_freq=l1_f_freq,
              hy_bias=l1_hy_bias)
    (sh1, sc1, g1, sh2, sc2, g2), (csh1, csc1, _, _, _, _) = _mod_vectors(c, c_ctx, p1)
    hy_cols = l1_conv_w.shape[1]
    kv_cols = 2 * (2 * D_HEAD)
    w_in1 = bf(l1_w_in)
    a1, b1 = _affine(l1_norm1_g, sh1, sc1)
    ca1, cb1 = _affine(l1_norm1_g, csh1, csc1)
    p_lat = _inproj(x, a1, b1, w_in1)
    pkv_ctx = _inproj(ctx_s, ca1, cb1, w_in1[:, -kv_cols:])
    o_hy = _hyena(p_lat, p1)
    o_at = _gqa_attention(p_lat, pkv_ctx, hy_cols, l1_q_norm_g, l1_k_norm_g)
    x = _outproj(o_hy, o_at, bf(l1_w_out), x, g1)
    a2, b2 = _affine(l1_norm2_g, sh2, sc2)
    return _mlp(x, a2, b2, g2, bf(l1_mlp_w1), bf(l1_mlp_w2), final_g=final_g)
```

```python
import functools
import math

import numpy as np
import jax
import jax.numpy as jnp
from jax import lax
from jax.experimental import pallas as pl
from jax.experimental.pallas import tpu as pltpu

F32 = jnp.float32
BF16 = jnp.bfloat16
HI = lax.Precision.HIGHEST

NORM_EPS = 1e-6
ROPE_THETA = 10000.0
GRID_W = 64
N_MOD = 6
A_QK_DIM = 64
A_V_DIM = 128
B_HEAD = 64
B_DECAY_RANK = 64
B_ICL_RANK = 64
B_GATE_RANK = 160
B_GN_EPS = 64e-5
HY_EMB = 33
HY_ORDER = 64
HY_TARGET = 1e-2
HY_FAST_PCT = 0.3
HY_SLOW_PCT = 1.5
D_HEAD = 128
D_KV_GROUP = 4

LANE = 128
VMEM_LIMIT = 56 * 1024 * 1024
LOG2E = 1.4426950408889634


def _cp(sem, vmem=VMEM_LIMIT):
    return pltpu.CompilerParams(dimension_semantics=sem, vmem_limit_bytes=vmem)


def _pick(n, pref, step=8):
    t = max(step, min(n, pref) // step * step)
    while n % t:
        t -= step
    return t


def _mods_body(c_ref, w_ref, b_ref, o_ref):
    c = c_ref[...]
    s = c * jax.nn.sigmoid(c)
    o_ref[...] = jnp.dot(s, w_ref[...], preferred_element_type=F32, precision=HI) + b_ref[...]


def _mods(cvec, w_mod, b_mod):
    m, d = cvec.shape
    n = w_mod.shape[1]
    tn = _pick(n, 1024, LANE)
    return pl.pallas_call(
        _mods_body,
        grid=(n // tn,),
        in_specs=[pl.BlockSpec((m, d), lambda j: (0, 0)),
                  pl.BlockSpec((d, tn), lambda j: (0, j)),
                  pl.BlockSpec((1, tn), lambda j: (0, j))],
        out_specs=pl.BlockSpec((m, tn), lambda j: (0, j)),
        out_shape=jax.ShapeDtypeStruct((m, n), F32),
        compiler_params=_cp(("arbitrary",)),
        name="mods",
    )(cvec, w_mod, b_mod.reshape(1, n))


def _norm_mod(x, a, b):
    ms = jnp.mean(x * x, axis=-1, keepdims=True)
    return x * lax.rsqrt(ms + NORM_EPS) * a + b


def _inproj_body(x_ref, a_ref, b_ref, w_ref, o_ref, xn_ref):
    @pl.when(pl.program_id(2) == 0)
    def _():
        xn_ref[...] = _norm_mod(x_ref[0], a_ref[0], b_ref[0]).astype(BF16)

    o_ref[0] = jnp.dot(xn_ref[...], w_ref[...], preferred_element_type=F32).astype(o_ref.dtype)


def _inproj(x, a, b, w, out_dtype=BF16, tm_pref=1024, tn_pref=512):
    bsz, n, d = x.shape
    nn = w.shape[1]
    tm, tn = _pick(n, tm_pref), _pick(nn, tn_pref, LANE)
    return pl.pallas_call(
        _inproj_body,
        grid=(bsz, n // tm, nn // tn),
        in_specs=[pl.BlockSpec((1, tm, d), lambda bi, i, j: (bi, i, 0)),
                  pl.BlockSpec((1, 1, d), lambda bi, i, j: (bi, 0, 0)),
                  pl.BlockSpec((1, 1, d), lambda bi, i, j: (bi, 0, 0)),
                  pl.BlockSpec((d, tn), lambda bi, i, j: (0, j))],
        out_specs=pl.BlockSpec((1, tm, tn), lambda bi, i, j: (bi, i, j)),
        out_shape=jax.ShapeDtypeStruct((bsz, n, nn), out_dtype),
        scratch_shapes=[pltpu.VMEM((tm, d), BF16)],
        compiler_params=_cp(("parallel", "parallel", "arbitrary")),
        name="inproj",
    )(x, a, b, w)


def _outproj_body(oa_ref, ob_ref, wa_ref, wb_ref, x_ref, g_ref, y_ref):
    acc = jnp.dot(oa_ref[0], wa_ref[...], preferred_element_type=F32)
    acc += jnp.dot(ob_ref[0], wb_ref[...], preferred_element_type=F32)
    y_ref[0] = x_ref[0] + g_ref[0] * acc


def _outproj(oa, ob, w, x, g, tm_pref=1024, tn_pref=512):
    bsz, n, ka = oa.shape
    kb = ob.shape[2]
    d = w.shape[1]
    tm, tn = _pick(n, tm_pref), _pick(d, tn_pref, LANE)
    return pl.pallas_call(
        _outproj_body,
        grid=(bsz, n // tm, d // tn),
        in_specs=[pl.BlockSpec((1, tm, ka), lambda bi, i, j: (bi, i, 0)),
                  pl.BlockSpec((1, tm, kb), lambda bi, i, j: (bi, i, 0)),
                  pl.BlockSpec((ka, tn), lambda bi, i, j: (0, j)),
                  pl.BlockSpec((kb, tn), lambda bi, i, j: (0, j)),
                  pl.BlockSpec((1, tm, tn), lambda bi, i, j: (bi, i, j)),
                  pl.BlockSpec((1, 1, tn), lambda bi, i, j: (bi, 0, j))],
        out_specs=pl.BlockSpec((1, tm, tn), lambda bi, i, j: (bi, i, j)),
        out_shape=jax.ShapeDtypeStruct((bsz, n, d), F32),
        compiler_params=_cp(("parallel", "parallel", "parallel")),
        name="outproj",
    )(oa, ob, w[:ka], w[ka:], x, g)


def _mlp_body(x_ref, a_ref, b_ref, g_ref, w1_ref, w2_ref, fg_ref, y_ref, xn_ref, acc_ref, *, nf, final):
    f = pl.program_id(2)

    @pl.when(f == 0)
    def _():
        xn_ref[...] = _norm_mod(x_ref[0], a_ref[0], b_ref[0]).astype(BF16)
        acc_ref[...] = jnp.zeros_like(acc_ref)

    h = jnp.dot(xn_ref[...], w1_ref[...], preferred_element_type=F32)
    h = jnp.square(jnp.maximum(h, 0.0)).astype(BF16)
    acc_ref[...] += jnp.dot(h, w2_ref[...], preferred_element_type=F32)

    @pl.when(f == nf - 1)
    def _():
        y = x_ref[0] + g_ref[0] * acc_ref[...]
        if final:
            ms = jnp.mean(y * y, axis=-1, keepdims=True)
            y = y * lax.rsqrt(ms + NORM_EPS) * fg_ref[...]
        y_ref[0] = y


def _mlp(x, a, b, g, w1, w2, final_g=None, tm_pref=512, tf_pref=512):
    bsz, n, d = x.shape
    dff = w1.shape[1]
    tm, tf = _pick(n, tm_pref), _pick(dff, tf_pref, LANE)
    nf = dff // tf
    final = final_g is not None
    fg = (final_g if final else jnp.ones((d,), F32)).reshape(1, d).astype(F32)
    return pl.pallas_call(
        functools.partial(_mlp_body, nf=nf, final=final),
        grid=(bsz, n // tm, nf),
        in_specs=[pl.BlockSpec((1, tm, d), lambda bi, i, f: (bi, i, 0)),
                  pl.BlockSpec((1, 1, d), lambda bi, i, f: (bi, 0, 0)),
                  pl.BlockSpec((1, 1, d), lambda bi, i, f: (bi, 0, 0)),
                  pl.BlockSpec((1, 1, d), lambda bi, i, f: (bi, 0, 0)),
                  pl.BlockSpec((d, tf), lambda bi, i, f: (0, f)),
                  pl.BlockSpec((tf, d), lambda bi, i, f: (f, 0)),
                  pl.BlockSpec((1, d), lambda bi, i, f: (0, 0))],
        out_specs=pl.BlockSpec((1, tm, d), lambda bi, i, f: (bi, i, 0)),
        out_shape=jax.ShapeDtypeStruct((bsz, n, d), F32),
        scratch_shapes=[pltpu.VMEM((tm, d), BF16), pltpu.VMEM((tm, d), F32)],
        compiler_params=_cp(("parallel", "parallel", "arbitrary")),
        name="mlp",
    )(x, a, b, g, w1, w2, fg)


def _rope_tables(n, dim, reps):
    rows = n // GRID_W
    row = jnp.repeat(jnp.arange(rows, dtype=F32), GRID_W)
    col = jnp.tile(jnp.arange(GRID_W, dtype=F32), rows)
    half = dim // 2
    inv = ROPE_THETA ** (-jnp.arange(0, half, 2, dtype=F32) / half)
    ar, ac = row[:, None] * inv, col[:, None] * inv
    cos = jnp.concatenate([jnp.cos(ar), jnp.cos(ar), jnp.cos(ac), jnp.cos(ac)], axis=-1)
    sin = jnp.concatenate([-jnp.sin(ar), jnp.sin(ar), -jnp.sin(ac), jnp.sin(ac)], axis=-1)
    return jnp.tile(cos, (1, reps)), jnp.tile(sin, (1, reps))


def _swap_matrix(dim, reps):
    q = dim // 4
    width = dim * reps
    p = np.zeros((width, width), np.float32)
    for j in range(width):
        base, r = (j // dim) * dim, j % dim
        axis, which, f = r // (2 * q), (r % (2 * q)) // q, r % q
        p[base + axis * 2 * q + (1 - which) * q + f, j] = 1.0
    return p


def _qkprep_body(x_ref, cos_ref, sin_ref, p_ref, g_ref, gs_ref, o_ref, *, mode, norm, rope, scale, tq):
    x = x_ref[0]
    y = x.astype(F32)
    if rope:
        ys = jnp.dot(x, p_ref[...], preferred_element_type=F32)
    if norm:
        rs = lax.rsqrt(jnp.mean(y * y, axis=-1, keepdims=True) + NORM_EPS)
        y = y * rs * g_ref[...]
        if rope:
            ys = ys * rs * gs_ref[...]
    if rope:
        y = y * cos_ref[...] + ys * sin_ref[...]
    if scale != 1.0:
        y = y * scale
    if mode == "diffq":
        lane = lax.broadcasted_iota(jnp.int32, y.shape, 1)
        o_ref[0, 0, :tq] = jnp.where(lane < A_QK_DIM, y, 0.0).astype(o_ref.dtype)
        o_ref[0, 0, tq:] = jnp.where(lane >= A_QK_DIM, y, 0.0).astype(o_ref.dtype)
    elif mode == "gqaq":
        o_ref[0, 0] = y.astype(o_ref.dtype)
    else:
        o_ref[0] = y.astype(o_ref.dtype)


def _qkprep(p, col0, heads, *, mode, dim, gain=None, rope=True, scale=1.0, tq_pref=512):
    bsz, n, _ = p.shape
    tq = _pick(n, tq_pref)
    nb = n // tq
    cb = col0 // LANE
    reps = LANE // dim
    if rope:
        cos, sin = _rope_tables(n, dim, reps)
        pm = jnp.asarray(_swap_matrix(dim, reps), BF16)
    else:
        cos = sin = jnp.zeros((n, LANE), F32)
        pm = jnp.zeros((LANE, LANE), BF16)
    norm = gain is not None
    if norm:
        g = jnp.tile(gain.astype(F32), reps).reshape(1, LANE)
        gs = jnp.dot(g, jnp.asarray(_swap_matrix(dim, reps)), precision=HI)
    else:
        g = gs = jnp.ones((1, LANE), F32)
    if mode == "diffq":
        out_shape = (bsz, heads, 2 * n, LANE)
        out_spec = pl.BlockSpec((1, 1, 2 * tq, LANE), lambda b, h, i: (b, h, i, 0))
    elif mode == "gqaq":
        out_shape = (bsz, heads // D_KV_GROUP, D_KV_GROUP * n, LANE)
        out_spec = pl.BlockSpec((1, 1, tq, LANE),
                                lambda b, h, i: (b, h // D_KV_GROUP, i * D_KV_GROUP + h % D_KV_GROUP, 0))
    else:
        out_shape = (bsz, n, heads * LANE)
        out_spec = pl.BlockSpec((1, tq, LANE), lambda b, h, i: (b, i, h))
    return pl.pallas_call(
        functools.partial(_qkprep_body, mode=mode, norm=norm, rope=rope, scale=scale, tq=tq),
        grid=(bsz, heads, nb),
        in_specs=[pl.BlockSpec((1, tq, LANE), lambda b, h, i: (b, i, cb + h)),
                  pl.BlockSpec((tq, LANE), lambda b, h, i: (i, 0)),
                  pl.BlockSpec((tq, LANE), lambda b, h, i: (i, 0)),
                  pl.BlockSpec((LANE, LANE), lambda b, h, i: (0, 0)),
                  pl.BlockSpec((1, LANE), lambda b, h, i: (0, 0)),
                  pl.BlockSpec((1, LANE), lambda b, h, i: (0, 0))],
        out_specs=out_spec,
        out_shape=jax.ShapeDtypeStruct(out_shape, BF16),
        compiler_params=_cp(("parallel", "parallel", "parallel")),
        name="qkprep_" + mode,
    )(p, cos, sin, pm, g, gs)


FLASH_KEYS = 1280


def _flash_body(q_ref, k_ref, v_ref, e1_ref, e2_ref, o_ref, m_ref, l_ref, acc_ref, *, ts, nsub, mode, tq):
    m_ref[...] = jnp.full_like(m_ref, -jnp.inf)
    l_ref[...] = jnp.zeros_like(l_ref)
    acc_ref[...] = jnp.zeros_like(acc_ref)
    q = q_ref[0, 0]

    def sub(c, carry):
        off = pl.multiple_of(c * ts, ts)
        k = k_ref[0, pl.ds(off, ts), :]
        v = v_ref[0, pl.ds(off, ts), :]
        s = lax.dot_general(q, k, (((1,), (1,)), ((), ())), preferred_element_type=F32)
        m_prev = m_ref[...]
        m_new = jnp.maximum(m_prev, jnp.max(s, axis=1, keepdims=True))
        alpha = jnp.exp2(m_prev - m_new)
        p = jnp.exp2(s - jnp.concatenate([m_new] * (ts // LANE), axis=1))
        l_ref[...] = alpha * l_ref[...] + jnp.sum(p, axis=1, keepdims=True)
        acc_ref[...] = alpha * acc_ref[...] + jnp.dot(p.astype(BF16), v, preferred_element_type=F32)
        m_ref[...] = m_new
        return carry

    lax.fori_loop(0, nsub, sub, 0)

    o = acc_ref[...] / l_ref[...]
    if mode == "diff":
        d = o[:tq] - e1_ref[...] * o[tq:]
        ms = jnp.mean(d * d, axis=-1, keepdims=True)
        o_ref[0] = (d * lax.rsqrt(ms + NORM_EPS) * e2_ref[...]).astype(o_ref.dtype)
    else:
        for j in range(D_KV_GROUP):
            o_ref[0, :, j * LANE:(j + 1) * LANE] = o[j * tq:(j + 1) * tq].astype(o_ref.dtype)


def _flash(q, k, v, kcol0, vcol0, e1, e2, *, mode, n, tq):
    bsz, hk, _, _ = q.shape
    nk = k.shape[1]
    g = 2 if mode == "diff" else D_KV_GROUP
    mq = g * tq
    ts = _pick(nk, FLASH_KEYS, 2 * LANE)
    kb, vb = kcol0 // LANE, vcol0 // LANE
    ow = LANE if mode == "diff" else D_KV_GROUP * LANE
    return pl.pallas_call(
        functools.partial(_flash_body, ts=ts, nsub=nk // ts, mode=mode, tq=tq),
        grid=(bsz, hk, n // tq),
        in_specs=[pl.BlockSpec((1, 1, mq, LANE), lambda b, h, i: (b, h, i, 0)),
                  pl.BlockSpec((1, nk, LANE), lambda b, h, i: (b, 0, kb + h)),
                  pl.BlockSpec((1, nk, LANE), lambda b, h, i: (b, 0, vb + h)),
                  pl.BlockSpec((1, LANE), lambda b, h, i: (0, 0)),
                  pl.BlockSpec((1, LANE), lambda b, h, i: (0, 0))],
        out_specs=pl.BlockSpec((1, tq, ow), lambda b, h, i: (b, i, h)),
        out_shape=jax.ShapeDtypeStruct((bsz, n, hk * ow), BF16),
        scratch_shapes=[pltpu.VMEM((mq, LANE), F32), pltpu.VMEM((mq, LANE), F32), pltpu.VMEM((mq, LANE), F32)],
        compiler_params=_cp(("parallel", "parallel", "arbitrary")),
        name="flash_" + mode,
    )(q, k, v, e1, e2)


def _diff_attention(pa_lat, pa_ctx, lam, lam_init, subln_g):
    n, nctx = pa_lat.shape[1], pa_ctx.shape[1]
    aw = pa_lat.shape[2] // 3
    heads = aw // LANE
    scale = A_QK_DIM ** -0.5 * LOG2E
    tq_l, tq_c = _pick(n, 512), _pick(nctx, 512)
    q_l = _qkprep(pa_lat, 0, heads, mode="diffq", dim=A_QK_DIM, scale=scale, tq_pref=tq_l)
    k_l = _qkprep(pa_lat, aw, heads, mode="tok", dim=A_QK_DIM)
    q_c = _qkprep(pa_ctx, 0, heads, mode="diffq", dim=A_QK_DIM, rope=False, scale=scale, tq_pref=tq_c)
    k_all = jnp.concatenate([k_l, pa_ctx[..., aw:2 * aw]], axis=1)
    v_all = jnp.concatenate([pa_lat[..., 2 * aw:], pa_ctx[..., 2 * aw:]], axis=1)
    e1 = jnp.full((1, LANE), lam, F32)
    e2 = (subln_g.astype(F32) * (1.0 - lam_init)).reshape(1, LANE)
    o_lat = _flash(q_l, k_all, v_all, 0, 0, e1, e2, mode="diff", n=n, tq=tq_l)
    o_ctx = _flash(q_c, pa_ctx, pa_ctx, aw, 2 * aw, e1, e2, mode="diff", n=nctx, tq=tq_c)
    return o_lat, o_ctx


def _gqa_attention(p_lat, pkv_ctx, qcol0, q_norm_g, k_norm_g):
    n = p_lat.shape[1]
    hq, hk = 2 * D_KV_GROUP, 2
    kcol0 = qcol0 + hq * D_HEAD
    vcol0 = kcol0 + hk * D_HEAD
    scale = D_HEAD ** -0.5 * LOG2E
    tq = _pick(n, 256)
    q = _qkprep(p_lat, qcol0, hq, mode="gqaq", dim=D_HEAD, gain=q_norm_g, scale=scale, tq_pref=tq)
    k_l = _qkprep(p_lat, kcol0, hk, mode="tok", dim=D_HEAD, gain=k_norm_g)
    k_c = _qkprep(pkv_ctx, 0, hk, mode="tok", dim=D_HEAD, gain=k_norm_g, rope=False)
    k_all = jnp.concatenate([k_l, k_c], axis=1)
    v_all = jnp.concatenate([p_lat[..., vcol0:vcol0 + hk * D_HEAD], pkv_ctx[..., hk * D_HEAD:]], axis=1)
    dummy = jnp.zeros((1, LANE), F32)
    return _flash(q, k_all, v_all, 0, 0, dummy, dummy, mode="gqa", n=n, tq=tq)


RW_CHUNK = 64
RW_HALO = 8


def _mm_hi(a, b):
    return jnp.dot(a, b, preferred_element_type=F32, precision=HI)


def _rwprep_body(x_ref, hp_ref, hn_ref, mup_ref, mun_ref, vec_ref, w2f_ref, w2b_ref, a2f_ref, a2b_ref,
                 g2_ref, sel_ref, selt_ref,
                 r_ref, v_ref, kk_ref, lwf_ref, bf_ref, kf_ref, lwb_ref, bb_ref, kb_ref, g_ref, bonus_ref,
                 *, nb, tm, bw):
    i = pl.program_id(1)
    x = x_ref[0].astype(F32)
    row = lax.broadcasted_iota(jnp.int32, x.shape, 0)
    prev_edge = jnp.where(i > 0, hp_ref[0, RW_HALO - 1:RW_HALO, :].astype(F32), 0.0)
    next_edge = jnp.where(i < nb - 1, hn_ref[0, 0:1, :].astype(F32), 0.0)
    prev = jnp.where(row == 0, prev_edge, pltpu.roll(x, 1, 0))
    nxt = jnp.where(row == tm - 1, next_edge, pltpu.roll(x, tm - 1, 0))
    xs = x + mup_ref[...] * (prev - x) + mun_ref[...] * (nxt - x)
    r, k, v = xs[:, :bw], xs[:, bw:2 * bw], xs[:, 2 * bw:3 * bw]
    lr = xs[:, 3 * bw:3 * bw + LANE]
    gd = xs[:, 3 * bw + LANE:3 * bw + 3 * LANE]
    k_k, k_a, r_k = vec_ref[0:1], vec_ref[1:2], vec_ref[2:3]
    w0f, a0f, w0b, a0b = vec_ref[3:4], vec_ref[4:5], vec_ref[5:6], vec_ref[6:7]
    sel, selt = sel_ref[...], selt_ref[...]

    def head_sum(z):
        return _mm_hi(_mm_hi(z, sel), selt)

    kk = k * k_k
    kk = kk * lax.rsqrt(head_sum(kk * kk) + 1e-12)
    g_ref[0] = _mm_hi(jax.nn.sigmoid(gd), g2_ref[...]).astype(g_ref.dtype)
    th = jnp.tanh(lr)
    ksum = jnp.zeros_like(k)
    for (w0, a0, w2_ref, a2_ref, lw_ref, b_ref, kd_ref) in (
            (w0f, a0f, w2f_ref, a2f_ref, lwf_ref, bf_ref, kf_ref),
            (w0b, a0b, w2b_ref, a2b_ref, lwb_ref, bb_ref, kb_ref)):
        w_log = -jax.nn.softplus(-(w0 + _mm_hi(th, w2_ref[...]))) - 0.5
        logw = -jnp.exp(w_log)
        a = jax.nn.sigmoid(a0 + _mm_hi(lr, a2_ref[...]))
        k_d = k * (1.0 + (a - 1.0) * k_a)
        ksum = ksum + k_d
        bvec = kk * a
        for h in range(bw // B_HEAD):
            sl = slice(h * B_HEAD, (h + 1) * B_HEAD)
            lw_ref[0, h] = logw[:, sl]
            b_ref[0, h] = bvec[:, sl]
            kd_ref[0, h] = k_d[:, sl]
    for h in range(bw // B_HEAD):
        sl = slice(h * B_HEAD, (h + 1) * B_HEAD)
        r_ref[0, h] = r[:, sl]
        v_ref[0, h] = v[:, sl]
        kk_ref[0, h] = kk[:, sl]
    bonus_ref[0] = head_sum(r * ksum * r_k) * v


def _rwkv_prepare(pr, p):
    bsz, n, cols = pr.shape
    bw = p["w0_f"].shape[0]
    nh = bw // B_HEAD
    tm = _pick(n, 256)
    nb = n // tm
    hb = tm // RW_HALO
    pad = cols - 3 * bw

    def padded(vec):
        return jnp.concatenate([vec.astype(F32), jnp.zeros((cols - vec.shape[0],), F32)]).reshape(1, cols)

    vecs = jnp.stack([p["k_k"], p["k_a"], p["r_k"].reshape(-1), p["w0_f"], p["a0_f"], p["w0_b"], p["a0_b"],
                      jnp.zeros((bw,), F32)]).astype(F32)
    zr = jnp.zeros((B_DECAY_RANK, bw), F32)
    w2 = {d: jnp.concatenate([p["w2_" + d].astype(F32), zr], axis=0) for d in "fb"}
    a2 = {d: jnp.concatenate([zr, p["a2_" + d].astype(F32)], axis=0) for d in "fb"}
    g2 = jnp.concatenate([p["g2"].astype(F32), jnp.zeros((2 * LANE - B_GATE_RANK, bw), F32)], axis=0)
    sel_np = np.zeros((bw, LANE), np.float32)
    sel_np[np.arange(bw), np.arange(bw) // B_HEAD] = 1.0
    sel, selt = jnp.asarray(sel_np), jnp.asarray(sel_np.T)
    assert pad == 3 * LANE and B_DECAY_RANK + B_ICL_RANK == LANE

    full = lambda shape: pl.BlockSpec(shape, lambda b, i: tuple(0 for _ in shape))
    hm = pl.BlockSpec((1, nh, tm, B_HEAD), lambda b, i: (b, 0, i, 0))
    tok = pl.BlockSpec((1, tm, bw), lambda b, i: (b, i, 0))
    hm_shape = jax.ShapeDtypeStruct((bsz, nh, n, B_HEAD), F32)
    outs = pl.pallas_call(
        functools.partial(_rwprep_body, nb=nb, tm=tm, bw=bw),
        grid=(bsz, nb),
        in_specs=[pl.BlockSpec((1, tm, cols), lambda b, i: (b, i, 0)),
                  pl.BlockSpec((1, RW_HALO, cols), lambda b, i: (b, jnp.maximum(i * hb - 1, 0), 0)),
                  pl.BlockSpec((1, RW_HALO, cols), lambda b, i: (b, jnp.minimum((i + 1) * hb, nb * hb - 1), 0)),
                  full((1, cols)), full((1, cols)), full((8, bw)),
                  full((LANE, bw)), full((LANE, bw)), full((LANE, bw)), full((LANE, bw)),
                  full((2 * LANE, bw)), full((bw, LANE)), full((LANE, bw))],
        out_specs=[hm] * 9 + [tok, tok],
        out_shape=[hm_shape] * 9 + [jax.ShapeDtypeStruct((bsz, n, bw), BF16),
                                    jax.ShapeDtypeStruct((bsz, n, bw), F32)],
        compiler_params=_cp(("parallel", "parallel")),
        name="rwkv_prepare",
    )(pr, pr, pr, padded(p["mu_prev"]), padded(p["mu_next"]), vecs, w2["f"], w2["b"], a2["f"], a2["b"],
      g2, sel, selt)
    r, v, kk, lwf, bf, kf, lwb, bb, kb, g, bonus = outs
    return dict(r=r, v=v, kk=kk, g=g, bonus=bonus, f=(lwf, bf, kf), b=(lwb, bb, kb))


RW_PREC = dict(lm="bf16", inv="bf16", pq="bf16", state="bf16", out="bf16")


def _bmm(a, b, dims, prec="f32"):
    if prec == "f32":
        return jnp.einsum(dims, a, b, preferred_element_type=F32, precision=HI)
    ah, bh = a.astype(BF16), b.astype(BF16)
    mm = lambda u, w: jnp.einsum(dims, u, w, preferred_element_type=F32)
    if prec == "bf16":
        return mm(ah, bh)
    al, bl = (a - ah.astype(F32)).astype(BF16), (b - bh.astype(F32)).astype(BF16)
    return mm(ah, bh) + (mm(ah, bl) + mm(al, bh))


def _bmm_exact_lhs(a, b, dims):
    ab = a.astype(BF16)
    b1 = b.astype(BF16)
    r1 = b - b1.astype(F32)
    b2 = r1.astype(BF16)
    b3 = (r1 - b2.astype(F32)).astype(BF16)
    mm = lambda w: jnp.einsum(dims, ab, w, preferred_element_type=F32)
    return mm(b1) + (mm(b2) + mm(b3))


def _rwscan_body(r_ref, v_ref, kk_ref, b_ref, kd_ref, lw_ref, s0_ref, y_ref, sT_ref, s_ref, *, reverse, nc, ck):
    c = pl.program_id(1)

    @pl.when(c == 0)
    def _():
        s_ref[...] = s0_ref[0]

    r, v, kk, b, kd, lw = r_ref[0], v_ref[0], kk_ref[0], b_ref[0], kd_ref[0], lw_ref[0]
    nh = r.shape[0]
    ti = lax.broadcasted_iota(jnp.int32, (ck, ck), 0)
    si = lax.broadcasted_iota(jnp.int32, (ck, ck), 1)
    incl = (si >= ti) if reverse else (si <= ti)
    strict = (si > ti) if reverse else (si < ti)
    tri = jnp.broadcast_to(incl.astype(F32)[None], (nh, ck, ck))
    pc = RW_PREC
    cum = _bmm_exact_lhs(tri, lw, "hts,hsj->htj")
    total = cum[:, 0:1, :] if reverse else cum[:, ck - 1:ck, :]
    at = -kk * jnp.exp(cum - lw)
    winv = jnp.exp(-cum)
    bt, kt = b * winv, kd * winv
    rt = r * jnp.exp(cum)
    wrest = jnp.exp(total - cum)
    btw, ktw = b * wrest, kd * wrest

    def masked(m, z):
        return jnp.where(m[None], z, 0.0)

    lab = masked(strict, _bmm(at, bt, "htj,hsj->hts", pc["lm"]))
    lak = masked(strict, _bmm(at, kt, "htj,hsj->hts", pc["lm"]))
    mrb = masked(incl, _bmm(rt, bt, "htj,hsj->hts", pc["lm"]))
    mrk = masked(incl, _bmm(rt, kt, "htj,hsj->hts", pc["lm"]))
    eye = jnp.broadcast_to((ti == si).astype(F32)[None], (nh, ck, ck))
    tinv, pw = eye + lab, lab
    span = 2
    while span < ck:
        pw = _bmm(pw, pw, "hts,hsu->htu", pc["inv"])
        tinv = tinv + _bmm(tinv, pw, "hts,hsu->htu", pc["inv"])
        span *= 2
    p1 = _bmm(tinv, at, "hts,hsj->htj", pc["pq"])
    q1 = _bmm(tinv, _bmm(lak, v, "hts,hsi->hti", pc["pq"]), "hts,hsi->hti", pc["pq"])
    s = s_ref[...]
    u = _bmm(p1, s, "htj,hij->hti", pc["state"]) + q1
    y_ref[0] = (_bmm(rt, s, "htj,hij->hti", pc["out"]) + _bmm(mrb, u, "hts,hsi->hti", pc["out"])
                + _bmm(mrk, v, "hts,hsi->hti", pc["out"]))
    s_new = (s * jnp.exp(total) + _bmm(u, btw, "hti,htj->hij", pc["state"])
             + _bmm(v, ktw, "hti,htj->hij", pc["state"]))
    s_ref[...] = s_new

    @pl.when(c == nc - 1)
    def _():
        sT_ref[0] = s_new


def _rwkv_scan(r, v, kk, b, kd, lw, s0, reverse):
    bsz, nh, n, hd = r.shape
    ck = min(RW_CHUNK, n)
    nc = n // ck
    cidx = (lambda b_, c: (b_, 0, nc - 1 - c, 0)) if reverse else (lambda b_, c: (b_, 0, c, 0))
    seq = pl.BlockSpec((1, nh, ck, hd), cidx)
    st = pl.BlockSpec((1, nh, hd, hd), lambda b_, c: (b_, 0, 0, 0))
    return pl.pallas_call(
        functools.partial(_rwscan_body, reverse=reverse, nc=nc, ck=ck),
        grid=(bsz, nc),
        in_specs=[seq] * 6 + [st],
        out_specs=[seq, st],
        out_shape=[jax.ShapeDtypeStruct((bsz, nh, n, hd), F32), jax.ShapeDtypeStruct((bsz, nh, hd, hd), F32)],
        scratch_shapes=[pltpu.VMEM((nh, hd, hd), F32)],
        compiler_params=_cp(("parallel", "arbitrary")),
        name="rwkv_scan_" + ("bwd" if reverse else "fwd"),
    )(r, v, kk, b, kd, lw, s0)


def _rwpost_body(yf_ref, yb_ref, g_ref, bonus_ref, lg_ref, lb_ref, o_ref):
    y = yf_ref[0] + yb_ref[0]
    mu = jnp.mean(y, axis=-1, keepdims=True)
    var = jnp.mean(jnp.square(y - mu), axis=-1, keepdims=True)
    y = (y - mu) * lax.rsqrt(var + B_GN_EPS)
    ytok = jnp.concatenate([y[h] for h in range(y.shape[0])], axis=-1)
    ytok = ytok * lg_ref[...] + lb_ref[...] + bonus_ref[0]
    o_ref[0] = (ytok * g_ref[0].astype(F32)).astype(o_ref.dtype)


def _rwkv_post(y_f, y_b, g, bonus, lnx_g, lnx_b):
    bsz, nh, n, hd = y_f.shape
    bw = nh * hd
    tm = _pick(n, 256)
    hm = pl.BlockSpec((1, nh, tm, hd), lambda b, i: (b, 0, i, 0))
    tok = pl.BlockSpec((1, tm, bw), lambda b, i: (b, i, 0))
    vec = pl.BlockSpec((1, bw), lambda b, i: (0, 0))
    return pl.pallas_call(
        _rwpost_body,
        grid=(bsz, n // tm),
        in_specs=[hm, hm, tok, tok, vec, vec],
        out_specs=tok,
        out_shape=jax.ShapeDtypeStruct((bsz, n, bw), BF16),
        compiler_params=_cp(("parallel", "parallel")),
        name="rwkv_post",
    )(y_f, y_b, g, bonus, lnx_g.reshape(1, bw).astype(F32), lnx_b.reshape(1, bw).astype(F32))


def _rwkv_run(prep, states0, p):
    (lwf, bf, kf), (lwb, bb, kb) = prep["f"], prep["b"]
    y_f, s_f = _rwkv_scan(prep["r"], prep["v"], prep["kk"], bf, kf, lwf, states0[0], False)
    y_b, s_b = _rwkv_scan(prep["r"], prep["v"], prep["kk"], bb, kb, lwb, states0[1], True)
    return _rwkv_post(y_f, y_b, prep["g"], prep["bonus"], p["lnx_g"], p["lnx_b"]), (s_f, s_b)


HY_N2 = 128
HY_HALO = 8


def _hypre_body(x_ref, hp_ref, hn_ref, cw_ref, cb_ref, x0_ref, uv_ref, *, nb, tm, cw):
    i = pl.program_id(1)
    x = x_ref[0].astype(F32)
    row = lax.broadcasted_iota(jnp.int32, x.shape, 0)
    prev_edge = jnp.where(i > 0, hp_ref[0, HY_HALO - 1:HY_HALO, :].astype(F32), 0.0)
    next_edge = jnp.where(i < nb - 1, hn_ref[0, 0:1, :].astype(F32), 0.0)
    prev = jnp.where(row == 0, prev_edge, pltpu.roll(x, 1, 0))
    nxt = jnp.where(row == tm - 1, next_edge, pltpu.roll(x, tm - 1, 0))
    u = prev * cw_ref[0:1] + x * cw_ref[1:2] + nxt * cw_ref[2:3] + cb_ref[...]
    x0_ref[0] = u[:, :cw]
    uv_ref[0] = u[:, 2 * cw:3 * cw] * u[:, cw:2 * cw]


def _hyena_pre(p, conv_w, conv_b):
    bsz, n, _ = p.shape
    hc = conv_w.shape[1]
    cw = hc // 3
    tm = _pick(n, 256)
    nb = n // tm
    hb = tm // HY_HALO
    cwp = jnp.concatenate([conv_w.astype(F32), jnp.zeros((5, hc), F32)], axis=0)
    out = pl.BlockSpec((1, tm, cw), lambda b, i: (b, i, 0))
    return pl.pallas_call(
        functools.partial(_hypre_body, nb=nb, tm=tm, cw=cw),
        grid=(bsz, nb),
        in_specs=[pl.BlockSpec((1, tm, hc), lambda b, i: (b, i, 0)),
                  pl.BlockSpec((1, HY_HALO, hc), lambda b, i: (b, jnp.maximum(i * hb - 1, 0), 0)),
                  pl.BlockSpec((1, HY_HALO, hc), lambda b, i: (b, jnp.minimum((i + 1) * hb, nb * hb - 1), 0)),
                  pl.BlockSpec((8, hc), lambda b, i: (0, 0)),
                  pl.BlockSpec((1, hc), lambda b, i: (0, 0))],
        out_specs=[out, out],
        out_shape=[jax.ShapeDtypeStruct((bsz, n, cw), F32)] * 2,
        compiler_params=_cp(("parallel", "parallel")),
        name="hyena_pre",
    )(p, p, p, cwp, conv_b.reshape(1, hc).astype(F32))


def _hyfilt_body(z_ref, t_ref, z0_ref, w1_ref, w2_ref, w3_ref, w4_ref, w4b_ref, vec_ref, dl_ref,
                 k_ref, s_ref, *, half_tiles):
    i = pl.program_id(0)
    b1, b2, b3, fr = vec_ref[0:1], vec_ref[1:2], vec_ref[2:3], vec_ref[3:4]

    def mlp3(z):
        h = jnp.sin(fr * (_mm_hi(z, w1_ref[...]) + b1))
        h = jnp.sin(fr * (_mm_hi(h, w2_ref[...]) + b2))
        return jnp.sin(fr * (_mm_hi(h, w3_ref[...]) + b3))

    raw = _mm_hi(mlp3(z_ref[...]), w4_ref[...]) * jnp.exp(-t_ref[...] * dl_ref[...])

    @pl.when(i == 0)
    def _():
        s_ref[...] = jnp.zeros_like(s_ref)

    s_ref[...] += jnp.sum(jnp.abs(raw), axis=0, keepdims=True)
    hb0 = _mm_hi(mlp3(z0_ref[...]), w4b_ref[...])
    row = lax.broadcasted_iota(jnp.int32, raw.shape, 0)
    raw = raw + jnp.where((row == 0) & (i == 0), hb0[0:1], 0.0)
    k_ref[...] = jnp.where((row == 0) & (i == half_tiles), 0.0, raw)


def _hyena_kernel_taps(n, p):
    cw = p["f_w4"].shape[1] // 2
    f32 = lambda a: a.astype(F32)
    t = jnp.linspace(0.0, 1.0, n, dtype=F32)[:, None]
    bands = (HY_EMB - 1) // 2
    ang = (2 * math.pi / n) * jnp.arange(n, dtype=F32)[:, None] * jnp.linspace(1e-4, bands - 1, bands, dtype=F32)[None]
    z = jnp.concatenate([t, jnp.cos(ang), -jnp.sin(ang), jnp.zeros((n, HY_ORDER - HY_EMB), F32)], axis=-1)
    pos = np.concatenate([np.arange(n), [0], np.arange(n - 1, 0, -1)])
    zz, tt = z[pos], t[pos]
    w1 = jnp.concatenate([f32(p["f_w1"]), jnp.zeros((HY_ORDER - HY_EMB, HY_ORDER), F32)], axis=0)
    vecs = jnp.stack([f32(p["f_b1"]), f32(p["f_b2"]), f32(p["f_b3"]), f32(p["f_freq"])]
                     + [jnp.zeros((HY_ORDER,), F32)] * 4)
    deltas = jnp.abs(jnp.linspace(math.log(HY_TARGET) / HY_SLOW_PCT, math.log(HY_TARGET) / HY_FAST_PCT, cw,
                                  dtype=F32)).reshape(1, cw)
    tr = _pick(n, 512)
    half_tiles = n // tr
    full = lambda shape: pl.BlockSpec(shape, lambda i: tuple(0 for _ in shape))
    w4 = f32(p["f_w4"])
    return pl.pallas_call(
        functools.partial(_hyfilt_body, half_tiles=half_tiles),
        grid=(2 * half_tiles,),
        in_specs=[pl.BlockSpec((tr, HY_ORDER), lambda i: (i, 0)),
                  pl.BlockSpec((tr, 1), lambda i: (i, 0)),
                  full((8, HY_ORDER)), full((HY_ORDER, HY_ORDER)), full((HY_ORDER, HY_ORDER)),
                  full((HY_ORDER, HY_ORDER)),
                  pl.BlockSpec((HY_ORDER, cw), lambda i: (0, i // half_tiles)),
                  pl.BlockSpec((HY_ORDER, cw), lambda i: (0, 1)),
                  full((8, HY_ORDER)), full((1, cw))],
        out_specs=[pl.BlockSpec((tr, cw), lambda i: (i, 0)), pl.BlockSpec((1, cw), lambda i: (0, 0))],
        out_shape=[jax.ShapeDtypeStruct((2 * n, cw), F32), jax.ShapeDtypeStruct((1, cw), F32)],
        compiler_params=_cp(("arbitrary",)),
        name="hyena_filter",
    )(zz, tt, jnp.broadcast_to(z[0:1], (8, HY_ORDER)), w1, f32(p["f_w2"]), f32(p["f_w3"]), w4, w4, vecs, deltas)


def _dft_consts(n):
    nn = 2 * n
    n2 = HY_N2
    n1 = nn // n2
    a1 = 2 * np.pi * np.outer(np.arange(n1), np.arange(n1)) / n1
    a2 = 2 * np.pi * np.outer(np.arange(n2), np.arange(n2)) / n2
    f1 = np.concatenate([np.cos(a1), -np.sin(a1)], axis=0)
    f1inv = np.concatenate([np.cos(a1), -np.sin(a1)], axis=1) / nn
    c2, s2 = np.cos(a2), -np.sin(a2)
    m2 = np.block([[c2, -s2], [s2, c2]])
    m2inv = np.block([[c2, s2], [-s2, c2]])
    at = 2 * np.pi * np.outer(np.arange(n2), np.arange(n1)) / nn
    tw = np.stack([np.cos(at), -np.sin(at)])
    c = lambda a: jnp.asarray(a.astype(np.float32))
    return dict(n1=n1, n2=n2, f1=c(f1), f1inv=c(f1inv), m2=c(m2), m2inv=c(m2inv),
                tw_s2=c(tw[:, :, :, None]),
                tw_f1=c(np.transpose(tw, (0, 2, 1))[:, :, :, None]))


def _dft1_body(f_ref, x_ref, tw_ref, o_ref, *, n1):
    a = _mm_hi(f_ref[...], x_ref[0])
    ar, ai = a[:n1], a[n1:]
    twr, twi = tw_ref[0, 0], tw_ref[1, 0]
    o_ref[0, 0] = ar * twr - ai * twi
    o_ref[0, 1] = ar * twi + ai * twr


def _dft_stage1(x, consts, rows):
    bsz, _, cw = x.shape
    n1, n2 = consts["n1"], consts["n2"]
    xv = x.reshape(bsz, rows, n2 * cw)
    f1 = consts["f1"][:, :rows]
    return pl.pallas_call(
        functools.partial(_dft1_body, n1=n1),
        grid=(bsz, n2),
        in_specs=[pl.BlockSpec((2 * n1, rows), lambda b, j: (0, 0)),
                  pl.BlockSpec((1, rows, cw), lambda b, j: (b, 0, j)),
                  pl.BlockSpec((2, 1, n1, 1), lambda b, j: (0, j, 0, 0))],
        out_specs=pl.BlockSpec((1, 2, n1, cw), lambda b, j: (b, 0, 0, j)),
        out_shape=jax.ShapeDtypeStruct((bsz, 2, n1, n2 * cw), F32),
        compiler_params=_cp(("parallel", "parallel")),
        name="hyena_dft1",
    )(f1, xv, consts["tw_s2"])


def _dftmid_body(a_ref, m2_ref, m2i_ref, h_ref, tw_ref, is_ref, o_ref, *, n2, filt):
    a = a_ref[0, :, 0].reshape(2 * n2, a_ref.shape[-1])
    x = _mm_hi(m2_ref[...], a)
    xr, xi = x[:n2], x[n2:]
    if filt:
        o_ref[0, 0, 0] = xr * is_ref[...]
        o_ref[0, 1, 0] = xi * is_ref[...]
        return
    hr, hi = h_ref[0, 0, 0], h_ref[0, 1, 0]
    y = jnp.concatenate([xr * hr - xi * hi, xr * hi + xi * hr], axis=0)
    z = _mm_hi(m2i_ref[...], y)
    zr, zi = z[:n2], z[n2:]
    twr, twi = tw_ref[0, 0], tw_ref[1, 0]
    o_ref[0, 0, 0] = zr * twr + zi * twi
    o_ref[0, 1, 0] = zi * twr - zr * twi


def _dft_mid(a, h, inv_s, consts, filt):
    bsz = a.shape[0]
    n1, n2 = consts["n1"], consts["n2"]
    cw = a.shape[-1] // n2
    av = a.reshape(bsz, 2, n1, n2, cw)
    blk = pl.BlockSpec((1, 2, 1, n2, cw), lambda b, f: (b, 0, f, 0, 0))
    hblk = pl.BlockSpec((1, 2, 1, n2, cw), (lambda b, f: (0, 0, 0, 0, 0)) if filt else (lambda b, f: (0, 0, f, 0, 0)))
    return pl.pallas_call(
        functools.partial(_dftmid_body, n2=n2, filt=filt),
        grid=(bsz, n1),
        in_specs=[blk,
                  pl.BlockSpec((2 * n2, 2 * n2), lambda b, f: (0, 0)),
                  pl.BlockSpec((2 * n2, 2 * n2), lambda b, f: (0, 0)),
                  hblk,
                  pl.BlockSpec((2, 1, n2, 1), lambda b, f: (0, f, 0, 0)),
                  pl.BlockSpec((1, cw), lambda b, f: (0, 0))],
        out_specs=blk,
        out_shape=jax.ShapeDtypeStruct((bsz, 2, n1, n2, cw), F32),
        compiler_params=_cp(("parallel", "parallel")),
        name="hyena_dftmid_" + ("filter" if filt else "conv"),
    )(av, consts["m2"], consts["m2inv"], h, consts["tw_f1"], inv_s)


def _dft3_body(f_ref, z_ref, x0_ref, uv_ref, bias_ref, o_ref):
    y = _mm_hi(f_ref[...], z_ref[0])
    o_ref[0] = (x0_ref[0] * (y + uv_ref[0] * bias_ref[...])).astype(o_ref.dtype)


def _dft_stage1_inv(z, x0, uv, bias, consts):
    bsz, n, cw = x0.shape
    n1, n2 = consts["n1"], consts["n2"]
    hr = n1 // 2
    zv = z.reshape(bsz, 2 * n1, n2 * cw)
    tokv = lambda t: t.reshape(bsz, hr, n2 * cw)
    tile = pl.BlockSpec((1, hr, cw), lambda b, j: (b, 0, j))
    out = pl.pallas_call(
        _dft3_body,
        grid=(bsz, n2),
        in_specs=[pl.BlockSpec((hr, 2 * n1), lambda b, j: (0, 0)),
                  pl.BlockSpec((1, 2 * n1, cw), lambda b, j: (b, 0, j)),
                  tile, tile,
                  pl.BlockSpec((1, cw), lambda b, j: (0, 0))],
        out_specs=tile,
        out_shape=jax.ShapeDtypeStruct((bsz, hr, n2 * cw), BF16),
        compiler_params=_cp(("parallel", "parallel")),
        name="hyena_dft3",
    )(consts["f1inv"][:hr], zv, tokv(x0), tokv(uv), bias.reshape(1, cw).astype(F32))
    return out.reshape(bsz, n, cw)


def _hyena(p_lat, p):
    n = p_lat.shape[1]
    consts = _dft_consts(n)
    n1 = consts["n1"]
    x0, uv = _hyena_pre(p_lat, p["conv_w"], p["conv_b"])
    taps, sabs = _hyena_kernel_taps(n, p)
    cw = taps.shape[1]
    ones = jnp.ones((1, cw), F32)
    hk = _dft_stage1(taps[None], consts, n1)
    hspec = _dft_mid(hk, jnp.zeros((1, 2, 1, HY_N2, cw), F32), 1.0 / sabs, consts, True)
    a = _dft_stage1(uv, consts, n1 // 2)
    z = _dft_mid(a, hspec, ones, consts, False)
    return _dft_stage1_inv(z, x0, uv, p["hy_bias"], consts)


def _mod_vectors(c, c_ctx, p):
    bsz, d = c.shape
    cvec = jnp.concatenate([c, c_ctx[None], jnp.zeros((8 - bsz - 1, d), c.dtype)], axis=0).astype(F32)
    m = _mods(cvec, p["w_mod"].astype(F32), p["b_mod"].astype(F32))
    lat = [v[:, None, :] for v in jnp.split(m[:bsz], N_MOD, axis=-1)]
    cx = [jnp.broadcast_to(v[:, None, :], (bsz, 1, d)) for v in jnp.split(m[bsz:bsz + 1], N_MOD, axis=-1)]
    return lat, cx


def _affine(norm_g, shift, scale):
    return norm_g.astype(F32) * (1.0 + scale), shift


def kernel(x, c, ctx, c_ctx,
           l0_w_mod, l0_b_mod, l0_norm1_g, l0_norm2_g, l0_w_in, l0_lam_q1, l0_lam_k1, l0_lam_q2, l0_lam_k2,
           l0_subln_g, l0_mu_prev, l0_mu_next, l0_w0_f, l0_w2_f, l0_a0_f, l0_a2_f, l0_w0_b, l0_w2_b, l0_a0_b,
           l0_a2_b, l0_g2, l0_k_k, l0_k_a, l0_r_k, l0_lnx_g, l0_lnx_b, l0_w_out, l0_mlp_w1, l0_mlp_w2,
           l1_w_mod, l1_b_mod, l1_norm1_g, l1_norm2_g, l1_w_in, l1_conv_w, l1_conv_b, l1_f_w1, l1_f_b1,
           l1_f_w2, l1_f_b2, l1_f_w3, l1_f_b3, l1_f_w4, l1_f_freq, l1_hy_bias, l1_q_norm_g, l1_k_norm_g,
           l1_w_out, l1_mlp_w1, l1_mlp_w2, final_g):
    bsz, n, d = x.shape
    x = x.astype(F32)
    ctx_s = ctx.astype(F32)
    bf = lambda w: w.astype(BF16)

    p0 = dict(w_mod=l0_w_mod, b_mod=l0_b_mod, mu_prev=l0_mu_prev, mu_next=l0_mu_next, w0_f=l0_w0_f, w2_f=l0_w2_f,
              a0_f=l0_a0_f, a2_f=l0_a2_f, w0_b=l0_w0_b, w2_b=l0_w2_b, a0_b=l0_a0_b, a2_b=l0_a2_b, g2=l0_g2,
              k_k=l0_k_k, k_a=l0_k_a, r_k=l0_r_k, lnx_g=l0_lnx_g, lnx_b=l0_lnx_b)
    (sh1, sc1, g1, sh2, sc2, g2), (csh1, csc1, cg1, csh2, csc2, cg2) = _mod_vectors(c, c_ctx, p0)
    aw = 3 * (d // 2)
    w_attn = bf(l0_w_in[:, :aw])
    w_rw = l0_w_in[:, aw:]
    rw_cols = 3 * l0_w0_f.shape[0] + 3 * LANE
    w_rw = bf(jnp.concatenate([w_rw, jnp.zeros((d, rw_cols - w_rw.shape[1]), w_rw.dtype)], axis=1))
    a1, b1 = _affine(l0_norm1_g, sh1, sc1)
    ca1, cb1 = _affine(l0_norm1_g, csh1, csc1)
    pa_lat, pr_lat = _inproj(x, a1, b1, w_attn), _inproj(x, a1, b1, w_rw)
    pa_ctx, pr_ctx = _inproj(ctx_s, ca1, cb1, w_attn), _inproj(ctx_s, ca1, cb1, w_rw)
    lam_init = 0.8 - 0.6 * math.exp(-0.3 * 0)
    lam = (jnp.exp(jnp.sum(l0_lam_q1 * l0_lam_k1).astype(F32))
           - jnp.exp(jnp.sum(l0_lam_q2 * l0_lam_k2).astype(F32)) + lam_init)
    oa_lat, oa_ctx = _diff_attention(pa_lat, pa_ctx, lam, lam_init, l0_subln_g)
    nh, hd = l0_r_k.shape
    zero = jnp.zeros((bsz, nh, hd, hd), F32)
    y_ctx, states_c = _rwkv_run(_rwkv_prepare(pr_ctx, p0), (zero, zero), p0)
    y_lat, _ = _rwkv_run(_rwkv_prepare(pr_lat, p0), states_c, p0)
    w_out0 = bf(l0_w_out)
    x = _outproj(oa_lat, y_lat, w_out0, x, g1)
    ctx_s = _outproj(oa_ctx, y_ctx, w_out0, ctx_s, cg1)
    w1, w2 = bf(l0_mlp_w1), bf(l0_mlp_w2)
    a2, b2 = _affine(l0_norm2_g, sh2, sc2)
    ca2, cb2 = _affine(l0_norm2_g, csh2, csc2)
    x = _mlp(x, a2, b2, g2, w1, w2)
    ctx_s = _mlp(ctx_s, ca2, cb2, cg2, w1, w2)

    p1 = dict(w_mod=l1_w_mod, b_mod=l1_b_mod, conv_w=l1_conv_w, conv_b=l1_conv_b, f_w1=l1_f_w1, f_b1=l1_f_b1,
              f_w2=l1_f_w2, f_b2=l1_f_b2, f_w3=l1_f_w3, f_b3=l1_f_b3, f_w4=l1_f_w4, f_freq=l1_f_freq,
              hy_bias=l1_hy_bias)
    (sh1, sc1, g1, sh2, sc2, g2), (csh1, csc1, _, _, _, _) = _mod_vectors(c, c_ctx, p1)
    hy_cols = l1_conv_w.shape[1]
    kv_cols = 2 * (2 * D_HEAD)
    w_in1 = bf(l1_w_in)
    a1, b1 = _affine(l1_norm1_g, sh1, sc1)
    ca1, cb1 = _affine(l1_norm1_g, csh1, csc1)
    p_lat = _inproj(x, a1, b1, w_in1)
    pkv_ctx = _inproj(ctx_s, ca1, cb1, w_in1[:, -kv_cols:])
    o_hy = _hyena(p_lat, p1)
    o_at = _gqa_attention(p_lat, pkv_ctx, hy_cols, l1_q_norm_g, l1_k_norm_g)
    x = _outproj(o_hy, o_at, bf(l1_w_out), x, g1)
    a2, b2 = _affine(l1_norm2_g, sh2, sc2)
    return _mlp(x, a2, b2, g2, bf(l1_mlp_w1), bf(l1_mlp_w2), final_g=final_g)
```

```python
import functools
import math

import numpy as np
import jax
import jax.numpy as jnp
from jax import lax
from jax.experimental import pallas as pl
from jax.experimental.pallas import tpu as pltpu

F32 = jnp.float32
BF16 = jnp.bfloat16
HI = lax.Precision.HIGHEST

NORM_EPS = 1e-6
ROPE_THETA = 10000.0
GRID_W = 64
N_MOD = 6
A_QK_DIM = 64
A_V_DIM = 128
B_HEAD = 64
B_DECAY_RANK = 64
B_ICL_RANK = 64
B_GATE_RANK = 160
B_GN_EPS = 64e-5
HY_EMB = 33
HY_ORDER = 64
HY_TARGET = 1e-2
HY_FAST_PCT = 0.3
HY_SLOW_PCT = 1.5
D_HEAD = 128
D_KV_GROUP = 4

LANE = 128
VMEM_LIMIT = 56 * 1024 * 1024
LOG2E = 1.4426950408889634


def _cp(sem, vmem=VMEM_LIMIT):
    return pltpu.CompilerParams(dimension_semantics=sem, vmem_limit_bytes=vmem)


def _pick(n, pref, step=8):
    t = max(step, min(n, pref) // step * step)
    while n % t:
        t -= step
    return t


def _mods_body(c_ref, w_ref, b_ref, o_ref):
    c = c_ref[...]
    s = c * jax.nn.sigmoid(c)
    o_ref[...] = jnp.dot(s, w_ref[...], preferred_element_type=F32, precision=HI) + b_ref[...]


def _mods(cvec, w_mod, b_mod):
    m, d = cvec.shape
    n = w_mod.shape[1]
    tn = _pick(n, 1024, LANE)
    return pl.pallas_call(
        _mods_body,
        grid=(n // tn,),
        in_specs=[pl.BlockSpec((m, d), lambda j: (0, 0)),
                  pl.BlockSpec((d, tn), lambda j: (0, j)),
                  pl.BlockSpec((1, tn), lambda j: (0, j))],
        out_specs=pl.BlockSpec((m, tn), lambda j: (0, j)),
        out_shape=jax.ShapeDtypeStruct((m, n), F32),
        compiler_params=_cp(("arbitrary",)),
        name="mods",
    )(cvec, w_mod, b_mod.reshape(1, n))


def _norm_mod(x, a, b):
    ms = jnp.mean(x * x, axis=-1, keepdims=True)
    return x * lax.rsqrt(ms + NORM_EPS) * a + b


def _inproj_body(x_ref, a_ref, b_ref, w_ref, o_ref, xn_ref):
    @pl.when(pl.program_id(2) == 0)
    def _():
        xn_ref[...] = _norm_mod(x_ref[0], a_ref[0], b_ref[0]).astype(BF16)

    o_ref[0] = jnp.dot(xn_ref[...], w_ref[...], preferred_element_type=F32).astype(o_ref.dtype)


def _inproj(x, a, b, w, out_dtype=BF16, tm_pref=1024, tn_pref=512):
    bsz, n, d = x.shape
    nn = w.shape[1]
    tm, tn = _pick(n, tm_pref), _pick(nn, tn_pref, LANE)
    return pl.pallas_call(
        _inproj_body,
        grid=(bsz, n // tm, nn // tn),
        in_specs=[pl.BlockSpec((1, tm, d), lambda bi, i, j: (bi, i, 0)),
                  pl.BlockSpec((1, 1, d), lambda bi, i, j: (bi, 0, 0)),
                  pl.BlockSpec((1, 1, d), lambda bi, i, j: (bi, 0, 0)),
                  pl.BlockSpec((d, tn), lambda bi, i, j: (0, j))],
        out_specs=pl.BlockSpec((1, tm, tn), lambda bi, i, j: (bi, i, j)),
        out_shape=jax.ShapeDtypeStruct((bsz, n, nn), out_dtype),
        scratch_shapes=[pltpu.VMEM((tm, d), BF16)],
        compiler_params=_cp(("parallel", "parallel", "arbitrary")),
        name="inproj",
    )(x, a, b, w)


def _outproj_body(oa_ref, ob_ref, wa_ref, wb_ref, x_ref, g_ref, y_ref):
    acc = jnp.dot(oa_ref[0], wa_ref[...], preferred_element_type=F32)
    acc += jnp.dot(ob_ref[0], wb_ref[...], preferred_element_type=F32)
    y_ref[0] = x_ref[0] + g_ref[0] * acc


def _outproj(oa, ob, w, x, g, tm_pref=1024, tn_pref=512):
    bsz, n, ka = oa.shape
    kb = ob.shape[2]
    d = w.shape[1]
    tm, tn = _pick(n, tm_pref), _pick(d, tn_pref, LANE)
    return pl.pallas_call(
        _outproj_body,
        grid=(bsz, n // tm, d // tn),
        in_specs=[pl.BlockSpec((1, tm, ka), lambda bi, i, j: (bi, i, 0)),
                  pl.BlockSpec((1, tm, kb), lambda bi, i, j: (bi, i, 0)),
                  pl.BlockSpec((ka, tn), lambda bi, i, j: (0, j)),
                  pl.BlockSpec((kb, tn), lambda bi, i, j: (0, j)),
                  pl.BlockSpec((1, tm, tn), lambda bi, i, j: (bi, i, j)),
                  pl.BlockSpec((1, 1, tn), lambda bi, i, j: (bi, 0, j))],
        out_specs=pl.BlockSpec((1, tm, tn), lambda bi, i, j: (bi, i, j)),
        out_shape=jax.ShapeDtypeStruct((bsz, n, d), F32),
        compiler_params=_cp(("parallel", "parallel", "parallel")),
        name="outproj",
    )(oa, ob, w[:ka], w[ka:], x, g)


def _mlp_body(x_ref, a_ref, b_ref, g_ref, w1_ref, w2_ref, fg_ref, y_ref, xn_ref, acc_ref, *, nf, final):
    f = pl.program_id(2)

    @pl.when(f == 0)
    def _():
        xn_ref[...] = _norm_mod(x_ref[0], a_ref[0], b_ref[0]).astype(BF16)
        acc_ref[...] = jnp.zeros_like(acc_ref)

    h = jnp.dot(xn_ref[...], w1_ref[...], preferred_element_type=F32)
    h = jnp.square(jnp.maximum(h, 0.0)).astype(BF16)
    acc_ref[...] += jnp.dot(h, w2_ref[...], preferred_element_type=F32)

    @pl.when(f == nf - 1)
    def _():
        y = x_ref[0] + g_ref[0] * acc_ref[...]
        if final:
            ms = jnp.mean(y * y, axis=-1, keepdims=True)
            y = y * lax.rsqrt(ms + NORM_EPS) * fg_ref[...]
        y_ref[0] = y


def _mlp(x, a, b, g, w1, w2, final_g=None, tm_pref=512, tf_pref=512):
    bsz, n, d = x.shape
    dff = w1.shape[1]
    tm, tf = _pick(n, tm_pref), _pick(dff, tf_pref, LANE)
    nf = dff // tf
    final = final_g is not None
    fg = (final_g if final else jnp.ones((d,), F32)).reshape(1, d).astype(F32)
    return pl.pallas_call(
        functools.partial(_mlp_body, nf=nf, final=final),
        grid=(bsz, n // tm, nf),
        in_specs=[pl.BlockSpec((1, tm, d), lambda bi, i, f: (bi, i, 0)),
                  pl.BlockSpec((1, 1, d), lambda bi, i, f: (bi, 0, 0)),
                  pl.BlockSpec((1, 1, d), lambda bi, i, f: (bi, 0, 0)),
                  pl.BlockSpec((1, 1, d), lambda bi, i, f: (bi, 0, 0)),
                  pl.BlockSpec((d, tf), lambda bi, i, f: (0, f)),
                  pl.BlockSpec((tf, d), lambda bi, i, f: (f, 0)),
                  pl.BlockSpec((1, d), lambda bi, i, f: (0, 0))],
        out_specs=pl.BlockSpec((1, tm, d), lambda bi, i, f: (bi, i, 0)),
        out_shape=jax.ShapeDtypeStruct((bsz, n, d), F32),
        scratch_shapes=[pltpu.VMEM((tm, d), BF16), pltpu.VMEM((tm, d), F32)],
        compiler_params=_cp(("parallel", "parallel", "arbitrary")),
        name="mlp",
    )(x, a, b, g, w1, w2, fg)


def _rope_tables(n, dim, reps):
    rows = n // GRID_W
    row = jnp.repeat(jnp.arange(rows, dtype=F32), GRID_W)
    col = jnp.tile(jnp.arange(GRID_W, dtype=F32), rows)
    half = dim // 2
    inv = ROPE_THETA ** (-jnp.arange(0, half, 2, dtype=F32) / half)
    ar, ac = row[:, None] * inv, col[:, None] * inv
    cos = jnp.concatenate([jnp.cos(ar), jnp.cos(ar), jnp.cos(ac), jnp.cos(ac)], axis=-1)
    sin = jnp.concatenate([-jnp.sin(ar), jnp.sin(ar), -jnp.sin(ac), jnp.sin(ac)], axis=-1)
    return jnp.tile(cos, (1, reps)), jnp.tile(sin, (1, reps))


def _swap_matrix(dim, reps):
    q = dim // 4
    width = dim * reps
    p = np.zeros((width, width), np.float32)
    for j in range(width):
        base, r = (j // dim) * dim, j % dim
        axis, which, f = r // (2 * q), (r % (2 * q)) // q, r % q
        p[base + axis * 2 * q + (1 - which) * q + f, j] = 1.0
    return p


def _qkprep_body(x_ref, cos_ref, sin_ref, p_ref, g_ref, gs_ref, o_ref, *, mode, norm, rope, scale, tq):
    x = x_ref[0]
    y = x.astype(F32)
    if rope:
        ys = jnp.dot(x, p_ref[...], preferred_element_type=F32)
    if norm:
        rs = lax.rsqrt(jnp.mean(y * y, axis=-1, keepdims=True) + NORM_EPS)
        y = y * rs * g_ref[...]
        if rope:
            ys = ys * rs * gs_ref[...]
    if rope:
        y = y * cos_ref[...] + ys * sin_ref[...]
    if scale != 1.0:
        y = y * scale
    if mode == "diffq":
        lane = lax.broadcasted_iota(jnp.int32, y.shape, 1)
        o_ref[0, 0, :tq] = jnp.where(lane < A_QK_DIM, y, 0.0).astype(o_ref.dtype)
        o_ref[0, 0, tq:] = jnp.where(lane >= A_QK_DIM, y, 0.0).astype(o_ref.dtype)
    elif mode == "gqaq":
        o_ref[0, 0] = y.astype(o_ref.dtype)
    else:
        o_ref[0] = y.astype(o_ref.dtype)


def _qkprep(p, col0, heads, *, mode, dim, gain=None, rope=True, scale=1.0, tq_pref=512):
    bsz, n, _ = p.shape
    tq = _pick(n, tq_pref)
    nb = n // tq
    cb = col0 // LANE
    reps = LANE // dim
    if rope:
        cos, sin = _rope_tables(n, dim, reps)
        pm = jnp.asarray(_swap_matrix(dim, reps), BF16)
    else:
        cos = sin = jnp.zeros((n, LANE), F32)
        pm = jnp.zeros((LANE, LANE), BF16)
    norm = gain is not None
    if norm:
        g = jnp.tile(gain.astype(F32), reps).reshape(1, LANE)
        gs = jnp.dot(g, jnp.asarray(_swap_matrix(dim, reps)), precision=HI)
    else:
        g = gs = jnp.ones((1, LANE), F32)
    if mode == "diffq":
        out_shape = (bsz, heads, 2 * n, LANE)
        out_spec = pl.BlockSpec((1, 1, 2 * tq, LANE), lambda b, h, i: (b, h, i, 0))
    elif mode == "gqaq":
        out_shape = (bsz, heads // D_KV_GROUP, D_KV_GROUP * n, LANE)
        out_spec = pl.BlockSpec((1, 1, tq, LANE),
                                lambda b, h, i: (b, h // D_KV_GROUP, i * D_KV_GROUP + h % D_KV_GROUP, 0))
    else:
        out_shape = (bsz, n, heads * LANE)
        out_spec = pl.BlockSpec((1, tq, LANE), lambda b, h, i: (b, i, h))
    return pl.pallas_call(
        functools.partial(_qkprep_body, mode=mode, norm=norm, rope=rope, scale=scale, tq=tq),
        grid=(bsz, heads, nb),
        in_specs=[pl.BlockSpec((1, tq, LANE), lambda b, h, i: (b, i, cb + h)),
                  pl.BlockSpec((tq, LANE), lambda b, h, i: (i, 0)),
                  pl.BlockSpec((tq, LANE), lambda b, h, i: (i, 0)),
                  pl.BlockSpec((LANE, LANE), lambda b, h, i: (0, 0)),
                  pl.BlockSpec((1, LANE), lambda b, h, i: (0, 0)),
                  pl.BlockSpec((1, LANE), lambda b, h, i: (0, 0))],
        out_specs=out_spec,
        out_shape=jax.ShapeDtypeStruct(out_shape, BF16),
        compiler_params=_cp(("parallel", "parallel", "parallel")),
        name="qkprep_" + mode,
    )(p, cos, sin, pm, g, gs)


FLASH_KEYS = 1280
FLASH_HEADROOM = 64.0


def _knorm_body(k_ref, o_ref):
    k = k_ref[0].astype(F32)
    nrm2 = jnp.max(jnp.sum(k * k, axis=1, keepdims=True), axis=0, keepdims=True)
    o_ref[0, 0] = jnp.broadcast_to(jnp.sqrt(nrm2), (8, LANE))


def _key_block_norms(k, kcol0, heads, ts):
    bsz, nk, _ = k.shape
    nsub = nk // ts
    kb = kcol0 // LANE
    out = pl.pallas_call(
        _knorm_body,
        grid=(bsz, heads, nsub),
        in_specs=[pl.BlockSpec((1, ts, LANE), lambda b, h, c: (b, c, kb + h))],
        out_specs=pl.BlockSpec((1, 1, 8, LANE), lambda b, h, c: (b, h, c, 0)),
        out_shape=jax.ShapeDtypeStruct((bsz, heads, nsub * 8, LANE), F32),
        compiler_params=_cp(("parallel", "parallel", "parallel")),
        name="key_block_norms",
    )(k)
    return out[:, :, ::8, 0].reshape(-1)


def _flash_body(kmax_ref, q_ref, k_ref, v_ref, e1_ref, e2_ref, o_ref, m_ref, l_ref, acc_ref, thr_ref,
                *, ts, nsub, mode, tq):
    base = (pl.program_id(0) * pl.num_programs(1) + pl.program_id(1)) * nsub
    q = q_ref[0, 0]

    def block(c):
        off = pl.multiple_of(c * ts, ts)
        k = k_ref[0, pl.ds(off, ts), :]
        s = lax.dot_general(q, k, (((1,), (1,)), ((), ())), preferred_element_type=F32)
        return s, v_ref[0, pl.ds(off, ts), :]

    def wide(m):
        return jnp.concatenate([m] * (ts // LANE), axis=1)

    s, v = block(0)
    m0 = jnp.broadcast_to(jnp.max(s, axis=1, keepdims=True), m_ref.shape)
    p = jnp.exp2(s - wide(m0))
    m_ref[...] = m0
    l_ref[...] = jnp.broadcast_to(jnp.sum(p, axis=1, keepdims=True), l_ref.shape)
    acc_ref[...] = jnp.dot(p.astype(BF16), v, preferred_element_type=F32)
    qf = q.astype(F32)
    qn = jnp.sqrt(jnp.sum(qf * qf, axis=1, keepdims=True))
    thr_ref[0] = jnp.min((m0[:, :1] + FLASH_HEADROOM) / qn)

    def step(c, carry):
        fixed = kmax_ref[base + c] * 1.001 <= thr_ref[0]

        @pl.when(fixed)
        def _():
            s, v = block(c)
            p = jnp.exp2(s - wide(m_ref[...]))
            l_ref[...] += jnp.sum(p, axis=1, keepdims=True)
            acc_ref[...] += jnp.dot(p.astype(BF16), v, preferred_element_type=F32)

        @pl.when(jnp.logical_not(fixed))
        def _():
            s, v = block(c)
            m_prev = m_ref[...]
            m_new = jnp.maximum(m_prev, jnp.max(s, axis=1, keepdims=True))
            alpha = jnp.exp2(m_prev - m_new)
            p = jnp.exp2(s - wide(m_new))
            l_ref[...] = alpha * l_ref[...] + jnp.sum(p, axis=1, keepdims=True)
            acc_ref[...] = alpha * acc_ref[...] + jnp.dot(p.astype(BF16), v, preferred_element_type=F32)
            m_ref[...] = m_new

        return carry

    lax.fori_loop(1, nsub, step, 0)

    o = acc_ref[...] / l_ref[...]
    if mode == "diff":
        d = o[:tq] - e1_ref[...] * o[tq:]
        ms = jnp.mean(d * d, axis=-1, keepdims=True)
        o_ref[0] = (d * lax.rsqrt(ms + NORM_EPS) * e2_ref[...]).astype(o_ref.dtype)
    else:
        for j in range(D_KV_GROUP):
            o_ref[0, :, j * LANE:(j + 1) * LANE] = o[j * tq:(j + 1) * tq].astype(o_ref.dtype)


def _flash(q, k, v, kcol0, vcol0, e1, e2, *, mode, n, tq):
    bsz, hk, _, _ = q.shape
    nk = k.shape[1]
    g = 2 if mode == "diff" else D_KV_GROUP
    mq = g * tq
    ts = _pick(nk, FLASH_KEYS, 2 * LANE)
    kb, vb = kcol0 // LANE, vcol0 // LANE
    ow = LANE if mode == "diff" else D_KV_GROUP * LANE
    kmax = _key_block_norms(k, kcol0, hk, ts)
    return pl.pallas_call(
        functools.partial(_flash_body, ts=ts, nsub=nk // ts, mode=mode, tq=tq),
        grid=(bsz, hk, n // tq),
        in_specs=[pl.BlockSpec(memory_space=pltpu.SMEM),
                  pl.BlockSpec((1, 1, mq, LANE), lambda b, h, i: (b, h, i, 0)),
                  pl.BlockSpec((1, nk, LANE), lambda b, h, i: (b, 0, kb + h)),
                  pl.BlockSpec((1, nk, LANE), lambda b, h, i: (b, 0, vb + h)),
                  pl.BlockSpec((1, LANE), lambda b, h, i: (0, 0)),
                  pl.BlockSpec((1, LANE), lambda b, h, i: (0, 0))],
        out_specs=pl.BlockSpec((1, tq, ow), lambda b, h, i: (b, i, h)),
        out_shape=jax.ShapeDtypeStruct((bsz, n, hk * ow), BF16),
        scratch_shapes=[pltpu.VMEM((mq, LANE), F32), pltpu.VMEM((mq, LANE), F32), pltpu.VMEM((mq, LANE), F32),
                        pltpu.SMEM((1,), F32)],
        compiler_params=_cp(("parallel", "parallel", "arbitrary")),
        name="flash_" + mode,
    )(kmax, q, k, v, e1, e2)


def _diff_attention(pa_lat, pa_ctx, lam, lam_init, subln_g):
    n, nctx = pa_lat.shape[1], pa_ctx.shape[1]
    aw = pa_lat.shape[2] // 3
    heads = aw // LANE
    scale = A_QK_DIM ** -0.5 * LOG2E
    tq_l, tq_c = _pick(n, 512), _pick(nctx, 512)
    q_l = _qkprep(pa_lat, 0, heads, mode="diffq", dim=A_QK_DIM, scale=scale, tq_pref=tq_l)
    k_l = _qkprep(pa_lat, aw, heads, mode="tok", dim=A_QK_DIM)
    q_c = _qkprep(pa_ctx, 0, heads, mode="diffq", dim=A_QK_DIM, rope=False, scale=scale, tq_pref=tq_c)
    k_all = jnp.concatenate([k_l, pa_ctx[..., aw:2 * aw]], axis=1)
    v_all = jnp.concatenate([pa_lat[..., 2 * aw:], pa_ctx[..., 2 * aw:]], axis=1)
    e1 = jnp.full((1, LANE), lam, F32)
    e2 = (subln_g.astype(F32) * (1.0 - lam_init)).reshape(1, LANE)
    o_lat = _flash(q_l, k_all, v_all, 0, 0, e1, e2, mode="diff", n=n, tq=tq_l)
    o_ctx = _flash(q_c, pa_ctx, pa_ctx, aw, 2 * aw, e1, e2, mode="diff", n=nctx, tq=tq_c)
    return o_lat, o_ctx


def _gqa_attention(p_lat, pkv_ctx, qcol0, q_norm_g, k_norm_g):
    n = p_lat.shape[1]
    hq, hk = 2 * D_KV_GROUP, 2
    kcol0 = qcol0 + hq * D_HEAD
    vcol0 = kcol0 + hk * D_HEAD
    scale = D_HEAD ** -0.5 * LOG2E
    tq = _pick(n, 256)
    q = _qkprep(p_lat, qcol0, hq, mode="gqaq", dim=D_HEAD, gain=q_norm_g, scale=scale, tq_pref=tq)
    k_l = _qkprep(p_lat, kcol0, hk, mode="tok", dim=D_HEAD, gain=k_norm_g)
    k_c = _qkprep(pkv_ctx, 0, hk, mode="tok", dim=D_HEAD, gain=k_norm_g, rope=False)
    k_all = jnp.concatenate([k_l, k_c], axis=1)
    v_all = jnp.concatenate([p_lat[..., vcol0:vcol0 + hk * D_HEAD], pkv_ctx[..., hk * D_HEAD:]], axis=1)
    dummy = jnp.zeros((1, LANE), F32)
    return _flash(q, k_all, v_all, 0, 0, dummy, dummy, mode="gqa", n=n, tq=tq)


RW_CHUNK = 64
RW_HALO = 8


def _mm_hi(a, b):
    return jnp.dot(a, b, preferred_element_type=F32, precision=HI)


def _mm_bf(a, b):
    return jnp.dot(a.astype(BF16), b.astype(BF16), preferred_element_type=F32)


def _split3(x):
    x1 = x.astype(BF16)
    r1 = x - x1.astype(F32)
    x2 = r1.astype(BF16)
    return x1, x2, (r1 - x2.astype(F32)).astype(BF16)


def _mm_x3(a, b):
    ah, al, _ = _split3(a)
    bh, bl, _ = _split3(b)
    d = lambda u, w: jnp.dot(u, w, preferred_element_type=F32)
    return d(ah, bh) + (d(ah, bl) + d(al, bh))


def _mm_sel(z, sel):
    z1, z2, z3 = _split3(z)
    d = lambda u: jnp.dot(u, sel, preferred_element_type=F32)
    return d(z1) + (d(z2) + d(z3))


def _rwprep_body(x_ref, hp_ref, hn_ref, mup_ref, mun_ref, vec_ref, w2f_ref, w2b_ref, a2f_ref, a2b_ref,
                 g2_ref, sel_ref, selt_ref,
                 r_ref, v_ref, kk_ref, lwf_ref, bf_ref, kf_ref, lwb_ref, bb_ref, kb_ref, g_ref, bonus_ref,
                 *, nb, tm, bw):
    i = pl.program_id(1)
    x = x_ref[0].astype(F32)
    row = lax.broadcasted_iota(jnp.int32, x.shape, 0)
    prev_edge = jnp.where(i > 0, hp_ref[0, RW_HALO - 1:RW_HALO, :].astype(F32), 0.0)
    next_edge = jnp.where(i < nb - 1, hn_ref[0, 0:1, :].astype(F32), 0.0)
    prev = jnp.where(row == 0, prev_edge, pltpu.roll(x, 1, 0))
    nxt = jnp.where(row == tm - 1, next_edge, pltpu.roll(x, tm - 1, 0))
    xs = x + mup_ref[...] * (prev - x) + mun_ref[...] * (nxt - x)
    r, k, v = xs[:, :bw], xs[:, bw:2 * bw], xs[:, 2 * bw:3 * bw]
    lr = xs[:, 3 * bw:3 * bw + LANE]
    gd = xs[:, 3 * bw + LANE:3 * bw + 3 * LANE]
    k_k, k_a, r_k = vec_ref[0:1], vec_ref[1:2], vec_ref[2:3]
    w0f, a0f, w0b, a0b = vec_ref[3:4], vec_ref[4:5], vec_ref[5:6], vec_ref[6:7]
    sel, selt = sel_ref[...], selt_ref[...]

    def head_sum(z):
        return _mm_sel(_mm_sel(z, sel), selt)

    kk = k * k_k
    kk = kk * lax.rsqrt(head_sum(kk * kk) + 1e-12)
    g_ref[0] = _mm_bf(jax.nn.sigmoid(gd), g2_ref[...]).astype(g_ref.dtype)
    th = jnp.tanh(lr)
    ksum = jnp.zeros_like(k)
    for (w0, a0, w2_ref, a2_ref, lw_ref, b_ref, kd_ref) in (
            (w0f, a0f, w2f_ref, a2f_ref, lwf_ref, bf_ref, kf_ref),
            (w0b, a0b, w2b_ref, a2b_ref, lwb_ref, bb_ref, kb_ref)):
        w_log = -jax.nn.softplus(-(w0 + _mm_bf(th, w2_ref[...]))) - 0.5
        logw = -jnp.exp(w_log)
        a = jax.nn.sigmoid(a0 + _mm_bf(lr, a2_ref[...]))
        k_d = k * (1.0 + (a - 1.0) * k_a)
        ksum = ksum + k_d
        bvec = kk * a
        for h in range(bw // B_HEAD):
            sl = slice(h * B_HEAD, (h + 1) * B_HEAD)
            lw_ref[0, h] = logw[:, sl]
            b_ref[0, h] = bvec[:, sl]
            kd_ref[0, h] = k_d[:, sl]
    for h in range(bw // B_HEAD):
        sl = slice(h * B_HEAD, (h + 1) * B_HEAD)
        r_ref[0, h] = r[:, sl]
        v_ref[0, h] = v[:, sl]
        kk_ref[0, h] = kk[:, sl]
    bonus_ref[0] = head_sum(r * ksum * r_k) * v


def _rwkv_prepare(pr, p):
    bsz, n, cols = pr.shape
    bw = p["w0_f"].shape[0]
    nh = bw // B_HEAD
    tm = _pick(n, 256)
    nb = n // tm
    hb = tm // RW_HALO
    pad = cols - 3 * bw

    def padded(vec):
        return jnp.concatenate([vec.astype(F32), jnp.zeros((cols - vec.shape[0],), F32)]).reshape(1, cols)

    vecs = jnp.stack([p["k_k"], p["k_a"], p["r_k"].reshape(-1), p["w0_f"], p["a0_f"], p["w0_b"], p["a0_b"],
                      jnp.zeros((bw,), F32)]).astype(F32)
    zr = jnp.zeros((B_DECAY_RANK, bw), F32)
    w2 = {d: jnp.concatenate([p["w2_" + d].astype(F32), zr], axis=0) for d in "fb"}
    a2 = {d: jnp.concatenate([zr, p["a2_" + d].astype(F32)], axis=0) for d in "fb"}
    g2 = jnp.concatenate([p["g2"].astype(F32), jnp.zeros((2 * LANE - B_GATE_RANK, bw), F32)], axis=0)
    sel_np = np.zeros((bw, LANE), np.float32)
    sel_np[np.arange(bw), np.arange(bw) // B_HEAD] = 1.0
    sel, selt = jnp.asarray(sel_np, BF16), jnp.asarray(sel_np.T, BF16)
    w2 = {d: w2[d].astype(BF16) for d in "fb"}
    a2 = {d: a2[d].astype(BF16) for d in "fb"}
    g2 = g2.astype(BF16)
    assert pad == 3 * LANE and B_DECAY_RANK + B_ICL_RANK == LANE

    full = lambda shape: pl.BlockSpec(shape, lambda b, i: tuple(0 for _ in shape))
    hm = pl.BlockSpec((1, nh, tm, B_HEAD), lambda b, i: (b, 0, i, 0))
    tok = pl.BlockSpec((1, tm, bw), lambda b, i: (b, i, 0))
    hm_shape = jax.ShapeDtypeStruct((bsz, nh, n, B_HEAD), F32)
    outs = pl.pallas_call(
        functools.partial(_rwprep_body, nb=nb, tm=tm, bw=bw),
        grid=(bsz, nb),
        in_specs=[pl.BlockSpec((1, tm, cols), lambda b, i: (b, i, 0)),
                  pl.BlockSpec((1, RW_HALO, cols), lambda b, i: (b, jnp.maximum(i * hb - 1, 0), 0)),
                  pl.BlockSpec((1, RW_HALO, cols), lambda b, i: (b, jnp.minimum((i + 1) * hb, nb * hb - 1), 0)),
                  full((1, cols)), full((1, cols)), full((8, bw)),
                  full((LANE, bw)), full((LANE, bw)), full((LANE, bw)), full((LANE, bw)),
                  full((2 * LANE, bw)), full((bw, LANE)), full((LANE, bw))],
        out_specs=[hm] * 9 + [tok, tok],
        out_shape=[hm_shape] * 9 + [jax.ShapeDtypeStruct((bsz, n, bw), BF16),
                                    jax.ShapeDtypeStruct((bsz, n, bw), F32)],
        compiler_params=_cp(("parallel", "parallel")),
        name="rwkv_prepare",
    )(pr, pr, pr, padded(p["mu_prev"]), padded(p["mu_next"]), vecs, w2["f"], w2["b"], a2["f"], a2["b"],
      g2, sel, selt)
    r, v, kk, lwf, bf, kf, lwb, bb, kb, g, bonus = outs
    return dict(r=r, v=v, kk=kk, g=g, bonus=bonus, f=(lwf, bf, kf), b=(lwb, bb, kb))


RW_PREC = dict(lm="bf16", inv="bf16", pq="bf16", state="bf16", out="bf16")


def _bmm(a, b, dims, prec="f32"):
    if prec == "f32":
        return jnp.einsum(dims, a, b, preferred_element_type=F32, precision=HI)
    ah, bh = a.astype(BF16), b.astype(BF16)
    mm = lambda u, w: jnp.einsum(dims, u, w, preferred_element_type=F32)
    if prec == "bf16":
        return mm(ah, bh)
    al, bl = (a - ah.astype(F32)).astype(BF16), (b - bh.astype(F32)).astype(BF16)
    return mm(ah, bh) + (mm(ah, bl) + mm(al, bh))


def _bmm_exact_lhs(a, b, dims):
    ab = a.astype(BF16)
    b1 = b.astype(BF16)
    r1 = b - b1.astype(F32)
    b2 = r1.astype(BF16)
    b3 = (r1 - b2.astype(F32)).astype(BF16)
    mm = lambda w: jnp.einsum(dims, ab, w, preferred_element_type=F32)
    return mm(b1) + (mm(b2) + mm(b3))


def _rwscan_body(r_ref, v_ref, kk_ref, b_ref, kd_ref, lw_ref, s0_ref, y_ref, sT_ref, s_ref, *, reverse, nc, ck):
    c = pl.program_id(1)

    @pl.when(c == 0)
    def _():
        s_ref[...] = s0_ref[0]

    r, v, kk, b, kd, lw = r_ref[0], v_ref[0], kk_ref[0], b_ref[0], kd_ref[0], lw_ref[0]
    nh = r.shape[0]
    ti = lax.broadcasted_iota(jnp.int32, (ck, ck), 0)
    si = lax.broadcasted_iota(jnp.int32, (ck, ck), 1)
    incl = (si >= ti) if reverse else (si <= ti)
    strict = (si > ti) if reverse else (si < ti)
    tri = jnp.broadcast_to(incl.astype(F32)[None], (nh, ck, ck))
    pc = RW_PREC
    cum = _bmm_exact_lhs(tri, lw, "hts,hsj->htj")
    total = cum[:, 0:1, :] if reverse else cum[:, ck - 1:ck, :]
    at = -kk * jnp.exp(cum - lw)
    winv = jnp.exp(-cum)
    bt, kt = b * winv, kd * winv
    rt = r * jnp.exp(cum)
    wrest = jnp.exp(total - cum)
    btw, ktw = b * wrest, kd * wrest

    def masked(m, z):
        return jnp.where(m[None], z, 0.0)

    lab = masked(strict, _bmm(at, bt, "htj,hsj->hts", pc["lm"]))
    lak = masked(strict, _bmm(at, kt, "htj,hsj->hts", pc["lm"]))
    mrb = masked(incl, _bmm(rt, bt, "htj,hsj->hts", pc["lm"]))
    mrk = masked(incl, _bmm(rt, kt, "htj,hsj->hts", pc["lm"]))
    eye = jnp.broadcast_to((ti == si).astype(F32)[None], (nh, ck, ck))
    tinv, pw = eye + lab, lab
    span = 2
    while span < ck:
        pw = _bmm(pw, pw, "hts,hsu->htu", pc["inv"])
        tinv = tinv + _bmm(tinv, pw, "hts,hsu->htu", pc["inv"])
        span *= 2
    p1 = _bmm(tinv, at, "hts,hsj->htj", pc["pq"])
    q1 = _bmm(tinv, _bmm(lak, v, "hts,hsi->hti", pc["pq"]), "hts,hsi->hti", pc["pq"])
    s = s_ref[...]
    u = _bmm(p1, s, "htj,hij->hti", pc["state"]) + q1
    y_ref[0] = (_bmm(rt, s, "htj,hij->hti", pc["out"]) + _bmm(mrb, u, "hts,hsi->hti", pc["out"])
                + _bmm(mrk, v, "hts,hsi->hti", pc["out"]))
    s_new = (s * jnp.exp(total) + _bmm(u, btw, "hti,htj->hij", pc["state"])
             + _bmm(v, ktw, "hti,htj->hij", pc["state"]))
    s_ref[...] = s_new

    @pl.when(c == nc - 1)
    def _():
        sT_ref[0] = s_new


def _rwkv_scan(r, v, kk, b, kd, lw, s0, reverse):
    bsz, nh, n, hd = r.shape
    ck = min(RW_CHUNK, n)
    nc = n // ck
    cidx = (lambda b_, c: (b_, 0, nc - 1 - c, 0)) if reverse else (lambda b_, c: (b_, 0, c, 0))
    seq = pl.BlockSpec((1, nh, ck, hd), cidx)
    st = pl.BlockSpec((1, nh, hd, hd), lambda b_, c: (b_, 0, 0, 0))
    return pl.pallas_call(
        functools.partial(_rwscan_body, reverse=reverse, nc=nc, ck=ck),
        grid=(bsz, nc),
        in_specs=[seq] * 6 + [st],
        out_specs=[seq, st],
        out_shape=[jax.ShapeDtypeStruct((bsz, nh, n, hd), F32), jax.ShapeDtypeStruct((bsz, nh, hd, hd), F32)],
        scratch_shapes=[pltpu.VMEM((nh, hd, hd), F32)],
        compiler_params=_cp(("parallel", "arbitrary")),
        name="rwkv_scan_" + ("bwd" if reverse else "fwd"),
    )(r, v, kk, b, kd, lw, s0)


def _rwpost_body(yf_ref, yb_ref, g_ref, bonus_ref, lg_ref, lb_ref, o_ref):
    y = yf_ref[0] + yb_ref[0]
    mu = jnp.mean(y, axis=-1, keepdims=True)
    var = jnp.mean(jnp.square(y - mu), axis=-1, keepdims=True)
    y = (y - mu) * lax.rsqrt(var + B_GN_EPS)
    ytok = jnp.concatenate([y[h] for h in range(y.shape[0])], axis=-1)
    ytok = ytok * lg_ref[...] + lb_ref[...] + bonus_ref[0]
    o_ref[0] = (ytok * g_ref[0].astype(F32)).astype(o_ref.dtype)


def _rwkv_post(y_f, y_b, g, bonus, lnx_g, lnx_b):
    bsz, nh, n, hd = y_f.shape
    bw = nh * hd
    tm = _pick(n, 256)
    hm = pl.BlockSpec((1, nh, tm, hd), lambda b, i: (b, 0, i, 0))
    tok = pl.BlockSpec((1, tm, bw), lambda b, i: (b, i, 0))
    vec = pl.BlockSpec((1, bw), lambda b, i: (0, 0))
    return pl.pallas_call(
        _rwpost_body,
        grid=(bsz, n // tm),
        in_specs=[hm, hm, tok, tok, vec, vec],
        out_specs=tok,
        out_shape=jax.ShapeDtypeStruct((bsz, n, bw), BF16),
        compiler_params=_cp(("parallel", "parallel")),
        name="rwkv_post",
    )(y_f, y_b, g, bonus, lnx_g.reshape(1, bw).astype(F32), lnx_b.reshape(1, bw).astype(F32))


def _rwkv_run(prep, states0, p):
    (lwf, bf, kf), (lwb, bb, kb) = prep["f"], prep["b"]
    y_f, s_f = _rwkv_scan(prep["r"], prep["v"], prep["kk"], bf, kf, lwf, states0[0], False)
    y_b, s_b = _rwkv_scan(prep["r"], prep["v"], prep["kk"], bb, kb, lwb, states0[1], True)
    return _rwkv_post(y_f, y_b, prep["g"], prep["bonus"], p["lnx_g"], p["lnx_b"]), (s_f, s_b)


HY_N2 = 128
HY_HALO = 8


def _hypre_body(x_ref, hp_ref, hn_ref, cw_ref, cb_ref, x0_ref, uv_ref, *, nb, tm, cw):
    i = pl.program_id(1)
    x = x_ref[0].astype(F32)
    row = lax.broadcasted_iota(jnp.int32, x.shape, 0)
    prev_edge = jnp.where(i > 0, hp_ref[0, HY_HALO - 1:HY_HALO, :].astype(F32), 0.0)
    next_edge = jnp.where(i < nb - 1, hn_ref[0, 0:1, :].astype(F32), 0.0)
    prev = jnp.where(row == 0, prev_edge, pltpu.roll(x, 1, 0))
    nxt = jnp.where(row == tm - 1, next_edge, pltpu.roll(x, tm - 1, 0))
    u = prev * cw_ref[0:1] + x * cw_ref[1:2] + nxt * cw_ref[2:3] + cb_ref[...]
    x0_ref[0] = u[:, :cw]
    uv_ref[0] = u[:, 2 * cw:3 * cw] * u[:, cw:2 * cw]


def _hyena_pre(p, conv_w, conv_b):
    bsz, n, _ = p.shape
    hc = conv_w.shape[1]
    cw = hc // 3
    tm = _pick(n, 256)
    nb = n // tm
    hb = tm // HY_HALO
    cwp = jnp.concatenate([conv_w.astype(F32), jnp.zeros((5, hc), F32)], axis=0)
    out = pl.BlockSpec((1, tm, cw), lambda b, i: (b, i, 0))
    return pl.pallas_call(
        functools.partial(_hypre_body, nb=nb, tm=tm, cw=cw),
        grid=(bsz, nb),
        in_specs=[pl.BlockSpec((1, tm, hc), lambda b, i: (b, i, 0)),
                  pl.BlockSpec((1, HY_HALO, hc), lambda b, i: (b, jnp.maximum(i * hb - 1, 0), 0)),
                  pl.BlockSpec((1, HY_HALO, hc), lambda b, i: (b, jnp.minimum((i + 1) * hb, nb * hb - 1), 0)),
                  pl.BlockSpec((8, hc), lambda b, i: (0, 0)),
                  pl.BlockSpec((1, hc), lambda b, i: (0, 0))],
        out_specs=[out, out],
        out_shape=[jax.ShapeDtypeStruct((bsz, n, cw), F32)] * 2,
        compiler_params=_cp(("parallel", "parallel")),
        name="hyena_pre",
    )(p, p, p, cwp, conv_b.reshape(1, hc).astype(F32))


def _hyfilt_body(z_ref, t_ref, z0_ref, w1_ref, w2_ref, w3_ref, w4_ref, w4b_ref, vec_ref, dl_ref,
                 k_ref, s_ref, *, half_tiles):
    i = pl.program_id(0)
    b1, b2, b3, fr = vec_ref[0:1], vec_ref[1:2], vec_ref[2:3], vec_ref[3:4]

    def mlp3(z):
        h = jnp.sin(fr * (_mm_hi(z, w1_ref[...]) + b1))
        h = jnp.sin(fr * (_mm_hi(h, w2_ref[...]) + b2))
        return jnp.sin(fr * (_mm_hi(h, w3_ref[...]) + b3))

    raw = _mm_hi(mlp3(z_ref[...]), w4_ref[...]) * jnp.exp(-t_ref[...] * dl_ref[...])

    @pl.when(i == 0)
    def _():
        s_ref[...] = jnp.zeros_like(s_ref)

    s_ref[...] += jnp.sum(jnp.abs(raw), axis=0, keepdims=True)
    hb0 = _mm_hi(mlp3(z0_ref[...]), w4b_ref[...])
    row = lax.broadcasted_iota(jnp.int32, raw.shape, 0)
    raw = raw + jnp.where((row == 0) & (i == 0), hb0[0:1], 0.0)
    k_ref[...] = jnp.where((row == 0) & (i == half_tiles), 0.0, raw)


def _hyena_kernel_taps(n, p):
    cw = p["f_w4"].shape[1] // 2
    f32 = lambda a: a.astype(F32)
    t = jnp.linspace(0.0, 1.0, n, dtype=F32)[:, None]
    bands = (HY_EMB - 1) // 2
    ang = (2 * math.pi / n) * jnp.arange(n, dtype=F32)[:, None] * jnp.linspace(1e-4, bands - 1, bands, dtype=F32)[None]
    z = jnp.concatenate([t, jnp.cos(ang), -jnp.sin(ang), jnp.zeros((n, HY_ORDER - HY_EMB), F32)], axis=-1)
    pos = np.concatenate([np.arange(n), [0], np.arange(n - 1, 0, -1)])
    zz, tt = z[pos], t[pos]
    w1 = jnp.concatenate([f32(p["f_w1"]), jnp.zeros((HY_ORDER - HY_EMB, HY_ORDER), F32)], axis=0)
    vecs = jnp.stack([f32(p["f_b1"]), f32(p["f_b2"]), f32(p["f_b3"]), f32(p["f_freq"])]
                     + [jnp.zeros((HY_ORDER,), F32)] * 4)
    deltas = jnp.abs(jnp.linspace(math.log(HY_TARGET) / HY_SLOW_PCT, math.log(HY_TARGET) / HY_FAST_PCT, cw,
                                  dtype=F32)).reshape(1, cw)
    tr = _pick(n, 512)
    half_tiles = n // tr
    full = lambda shape: pl.BlockSpec(shape, lambda i: tuple(0 for _ in shape))
    w4 = f32(p["f_w4"])
    return pl.pallas_call(
        functools.partial(_hyfilt_body, half_tiles=half_tiles),
        grid=(2 * half_tiles,),
        in_specs=[pl.BlockSpec((tr, HY_ORDER), lambda i: (i, 0)),
                  pl.BlockSpec((tr, 1), lambda i: (i, 0)),
                  full((8, HY_ORDER)), full((HY_ORDER, HY_ORDER)), full((HY_ORDER, HY_ORDER)),
                  full((HY_ORDER, HY_ORDER)),
                  pl.BlockSpec((HY_ORDER, cw), lambda i: (0, i // half_tiles)),
                  pl.BlockSpec((HY_ORDER, cw), lambda i: (0, 1)),
                  full((8, HY_ORDER)), full((1, cw))],
        out_specs=[pl.BlockSpec((tr, cw), lambda i: (i, 0)), pl.BlockSpec((1, cw), lambda i: (0, 0))],
        out_shape=[jax.ShapeDtypeStruct((2 * n, cw), F32), jax.ShapeDtypeStruct((1, cw), F32)],
        compiler_params=_cp(("arbitrary",)),
        name="hyena_filter",
    )(zz, tt, jnp.broadcast_to(z[0:1], (8, HY_ORDER)), w1, f32(p["f_w2"]), f32(p["f_w3"]), w4, w4, vecs, deltas)


def _dft_consts(n):
    nn = 2 * n
    n2 = HY_N2
    n1 = nn // n2
    a1 = 2 * np.pi * np.outer(np.arange(n1), np.arange(n1)) / n1
    a2 = 2 * np.pi * np.outer(np.arange(n2), np.arange(n2)) / n2
    f1 = np.concatenate([np.cos(a1), -np.sin(a1)], axis=0)
    f1inv = np.concatenate([np.cos(a1), -np.sin(a1)], axis=1) / nn
    c2, s2 = np.cos(a2), -np.sin(a2)
    m2 = np.block([[c2, -s2], [s2, c2]])
    m2inv = np.block([[c2, s2], [-s2, c2]])
    at = 2 * np.pi * np.outer(np.arange(n2), np.arange(n1)) / nn
    tw = np.stack([np.cos(at), -np.sin(at)])
    c = lambda a: jnp.asarray(a.astype(np.float32))
    return dict(n1=n1, n2=n2, f1=c(f1), f1inv=c(f1inv), m2=c(m2), m2inv=c(m2inv),
                tw_s2=c(tw[:, :, :, None]),
                tw_f1=c(np.transpose(tw, (0, 2, 1))[:, :, :, None]))


def _dft1_body(f_ref, x_ref, tw_ref, o_ref, *, n1):
    a = _mm_x3(f_ref[...], x_ref[0])
    ar, ai = a[:n1], a[n1:]
    twr, twi = tw_ref[0, 0], tw_ref[1, 0]
    o_ref[0, 0] = ar * twr - ai * twi
    o_ref[0, 1] = ar * twi + ai * twr


def _dft_stage1(x, consts, rows):
    bsz, _, cw = x.shape
    n1, n2 = consts["n1"], consts["n2"]
    xv = x.reshape(bsz, rows, n2 * cw)
    f1 = consts["f1"][:, :rows]
    return pl.pallas_call(
        functools.partial(_dft1_body, n1=n1),
        grid=(bsz, n2),
        in_specs=[pl.BlockSpec((2 * n1, rows), lambda b, j: (0, 0)),
                  pl.BlockSpec((1, rows, cw), lambda b, j: (b, 0, j)),
                  pl.BlockSpec((2, 1, n1, 1), lambda b, j: (0, j, 0, 0))],
        out_specs=pl.BlockSpec((1, 2, n1, cw), lambda b, j: (b, 0, 0, j)),
        out_shape=jax.ShapeDtypeStruct((bsz, 2, n1, n2 * cw), F32),
        compiler_params=_cp(("parallel", "parallel")),
        name="hyena_dft1",
    )(f1, xv, consts["tw_s2"])


def _dftmid_body(a_ref, m2_ref, m2i_ref, h_ref, tw_ref, is_ref, o_ref, *, n2, filt):
    a = a_ref[0, :, 0].reshape(2 * n2, a_ref.shape[-1])
    x = _mm_x3(m2_ref[...], a)
    xr, xi = x[:n2], x[n2:]
    if filt:
        o_ref[0, 0, 0] = xr * is_ref[...]
        o_ref[0, 1, 0] = xi * is_ref[...]
        return
    hr, hi = h_ref[0, 0, 0], h_ref[0, 1, 0]
    y = jnp.concatenate([xr * hr - xi * hi, xr * hi + xi * hr], axis=0)
    z = _mm_x3(m2i_ref[...], y)
    zr, zi = z[:n2], z[n2:]
    twr, twi = tw_ref[0, 0], tw_ref[1, 0]
    o_ref[0, 0, 0] = zr * twr + zi * twi
    o_ref[0, 1, 0] = zi * twr - zr * twi


def _dft_mid(a, h, inv_s, consts, filt):
    bsz = a.shape[0]
    n1, n2 = consts["n1"], consts["n2"]
    cw = a.shape[-1] // n2
    av = a.reshape(bsz, 2, n1, n2, cw)
    blk = pl.BlockSpec((1, 2, 1, n2, cw), lambda b, f: (b, 0, f, 0, 0))
    hblk = pl.BlockSpec((1, 2, 1, n2, cw), (lambda b, f: (0, 0, 0, 0, 0)) if filt else (lambda b, f: (0, 0, f, 0, 0)))
    return pl.pallas_call(
        functools.partial(_dftmid_body, n2=n2, filt=filt),
        grid=(bsz, n1),
        in_specs=[blk,
                  pl.BlockSpec((2 * n2, 2 * n2), lambda b, f: (0, 0)),
                  pl.BlockSpec((2 * n2, 2 * n2), lambda b, f: (0, 0)),
                  hblk,
                  pl.BlockSpec((2, 1, n2, 1), lambda b, f: (0, f, 0, 0)),
                  pl.BlockSpec((1, cw), lambda b, f: (0, 0))],
        out_specs=blk,
        out_shape=jax.ShapeDtypeStruct((bsz, 2, n1, n2, cw), F32),
        compiler_params=_cp(("parallel", "parallel")),
        name="hyena_dftmid_" + ("filter" if filt else "conv"),
    )(av, consts["m2"], consts["m2inv"], h, consts["tw_f1"], inv_s)


def _dft3_body(f_ref, z_ref, x0_ref, uv_ref, bias_ref, o_ref):
    y = _mm_x3(f_ref[...], z_ref[0])
    o_ref[0] = (x0_ref[0] * (y + uv_ref[0] * bias_ref[...])).astype(o_ref.dtype)


def _dft_stage1_inv(z, x0, uv, bias, consts):
    bsz, n, cw = x0.shape
    n1, n2 = consts["n1"], consts["n2"]
    hr = n1 // 2
    zv = z.reshape(bsz, 2 * n1, n2 * cw)
    tokv = lambda t: t.reshape(bsz, hr, n2 * cw)
    tile = pl.BlockSpec((1, hr, cw), lambda b, j: (b, 0, j))
    out = pl.pallas_call(
        _dft3_body,
        grid=(bsz, n2),
        in_specs=[pl.BlockSpec((hr, 2 * n1), lambda b, j: (0, 0)),
                  pl.BlockSpec((1, 2 * n1, cw), lambda b, j: (b, 0, j)),
                  tile, tile,
                  pl.BlockSpec((1, cw), lambda b, j: (0, 0))],
        out_specs=tile,
        out_shape=jax.ShapeDtypeStruct((bsz, hr, n2 * cw), BF16),
        compiler_params=_cp(("parallel", "parallel")),
        name="hyena_dft3",
    )(consts["f1inv"][:hr], zv, tokv(x0), tokv(uv), bias.reshape(1, cw).astype(F32))
    return out.reshape(bsz, n, cw)


def _hyena(p_lat, p):
    n = p_lat.shape[1]
    consts = _dft_consts(n)
    n1 = consts["n1"]
    x0, uv = _hyena_pre(p_lat, p["conv_w"], p["conv_b"])
    taps, sabs = _hyena_kernel_taps(n, p)
    cw = taps.shape[1]
    ones = jnp.ones((1, cw), F32)
    hk = _dft_stage1(taps[None], consts, n1)
    hspec = _dft_mid(hk, jnp.zeros((1, 2, 1, HY_N2, cw), F32), 1.0 / sabs, consts, True)
    a = _dft_stage1(uv, consts, n1 // 2)
    z = _dft_mid(a, hspec, ones, consts, False)
    return _dft_stage1_inv(z, x0, uv, p["hy_bias"], consts)


def _mod_vectors(c, c_ctx, p):
    bsz, d = c.shape
    cvec = jnp.concatenate([c, c_ctx[None], jnp.zeros((8 - bsz - 1, d), c.dtype)], axis=0).astype(F32)
    m = _mods(cvec, p["w_mod"].astype(F32), p["b_mod"].astype(F32))
    lat = [v[:, None, :] for v in jnp.split(m[:bsz], N_MOD, axis=-1)]
    cx = [jnp.broadcast_to(v[:, None, :], (bsz, 1, d)) for v in jnp.split(m[bsz:bsz + 1], N_MOD, axis=-1)]
    return lat, cx


def _affine(norm_g, shift, scale):
    return norm_g.astype(F32) * (1.0 + scale), shift


def kernel(x, c, ctx, c_ctx,
           l0_w_mod, l0_b_mod, l0_norm1_g, l0_norm2_g, l0_w_in, l0_lam_q1, l0_lam_k1, l0_lam_q2, l0_lam_k2,
           l0_subln_g, l0_mu_prev, l0_mu_next, l0_w0_f, l0_w2_f, l0_a0_f, l0_a2_f, l0_w0_b, l0_w2_b, l0_a0_b,
           l0_a2_b, l0_g2, l0_k_k, l0_k_a, l0_r_k, l0_lnx_g, l0_lnx_b, l0_w_out, l0_mlp_w1, l0_mlp_w2,
           l1_w_mod, l1_b_mod, l1_norm1_g, l1_norm2_g, l1_w_in, l1_conv_w, l1_conv_b, l1_f_w1, l1_f_b1,
           l1_f_w2, l1_f_b2, l1_f_w3, l1_f_b3, l1_f_w4, l1_f_freq, l1_hy_bias, l1_q_norm_g, l1_k_norm_g,
           l1_w_out, l1_mlp_w1, l1_mlp_w2, final_g):
    bsz, n, d = x.shape
    x = x.astype(F32)
    ctx_s = ctx.astype(F32)
    bf = lambda w: w.astype(BF16)

    p0 = dict(w_mod=l0_w_mod, b_mod=l0_b_mod, mu_prev=l0_mu_prev, mu_next=l0_mu_next, w0_f=l0_w0_f, w2_f=l0_w2_f,
              a0_f=l0_a0_f, a2_f=l0_a2_f, w0_b=l0_w0_b, w2_b=l0_w2_b, a0_b=l0_a0_b, a2_b=l0_a2_b, g2=l0_g2,
              k_k=l0_k_k, k_a=l0_k_a, r_k=l0_r_k, lnx_g=l0_lnx_g, lnx_b=l0_lnx_b)
    (sh1, sc1, g1, sh2, sc2, g2), (csh1, csc1, cg1, csh2, csc2, cg2) = _mod_vectors(c, c_ctx, p0)
    aw = 3 * (d // 2)
    w_attn = bf(l0_w_in[:, :aw])
    w_rw = l0_w_in[:, aw:]
    rw_cols = 3 * l0_w0_f.shape[0] + 3 * LANE
    w_rw = bf(jnp.concatenate([w_rw, jnp.zeros((d, rw_cols - w_rw.shape[1]), w_rw.dtype)], axis=1))
    a1, b1 = _affine(l0_norm1_g, sh1, sc1)
    ca1, cb1 = _affine(l0_norm1_g, csh1, csc1)
    pa_lat, pr_lat = _inproj(x, a1, b1, w_attn), _inproj(x, a1, b1, w_rw)
    pa_ctx, pr_ctx = _inproj(ctx_s, ca1, cb1, w_attn), _inproj(ctx_s, ca1, cb1, w_rw)
    lam_init = 0.8 - 0.6 * math.exp(-0.3 * 0)
    lam = (jnp.exp(jnp.sum(l0_lam_q1 * l0_lam_k1).astype(F32))
           - jnp.exp(jnp.sum(l0_lam_q2 * l0_lam_k2).astype(F32)) + lam_init)
    oa_lat, oa_ctx = _diff_attention(pa_lat, pa_ctx, lam, lam_init, l0_subln_g)
    nh, hd = l0_r_k.shape
    zero = jnp.zeros((bsz, nh, hd, hd), F32)
    y_ctx, states_c = _rwkv_run(_rwkv_prepare(pr_ctx, p0), (zero, zero), p0)
    y_lat, _ = _rwkv_run(_rwkv_prepare(pr_lat, p0), states_c, p0)
    w_out0 = bf(l0_w_out)
    x = _outproj(oa_lat, y_lat, w_out0, x, g1)
    ctx_s = _outproj(oa_ctx, y_ctx, w_out0, ctx_s, cg1)
    w1, w2 = bf(l0_mlp_w1), bf(l0_mlp_w2)
    a2, b2 = _affine(l0_norm2_g, sh2, sc2)
    ca2, cb2 = _affine(l0_norm2_g, csh2, csc2)
    x = _mlp(x, a2, b2, g2, w1, w2)
    ctx_s = _mlp(ctx_s, ca2, cb2, cg2, w1, w2)

    p1 = dict(w_mod=l1_w_mod, b_mod=l1_b_mod, conv_w=l1_conv_w, conv_b=l1_conv_b, f_w1=l1_f_w1, f_b1=l1_f_b1,
              f_w2=l1_f_w2, f_b2=l1_f_b2, f_w3=l1_f_w3, f_b3=l1_f_b3, f_w4=l1_f_w4, f_freq=l1_f_freq,
              hy_bias=l1_hy_bias)
    (sh1, sc1, g1, sh2, sc2, g2), (csh1, csc1, _, _, _, _) = _mod_vectors(c, c_ctx, p1)
    hy_cols = l1_conv_w.shape[1]
    kv_cols = 2 * (2 * D_HEAD)
    w_in1 = bf(l1_w_in)
    a1, b1 = _affine(l1_norm1_g, sh1, sc1)
    ca1, cb1 = _affine(l1_norm1_g, csh1, csc1)
    p_lat = _inproj(x, a1, b1, w_in1)
    pkv_ctx = _inproj(ctx_s, ca1, cb1, w_in1[:, -kv_cols:])
    o_hy = _hyena(p_lat, p1)
    o_at = _gqa_attention(p_lat, pkv_ctx, hy_cols, l1_q_norm_g, l1_k_norm_g)
    x = _outproj(o_hy, o_at, bf(l1_w_out), x, g1)
    a2, b2 = _affine(l1_norm2_g, sh2, sc2)
    return _mlp(x, a2, b2, g2, bf(l1_mlp_w1), bf(l1_mlp_w2), final_g=final_g)
```

```python
import functools
import math

import numpy as np
import jax
import jax.numpy as jnp
from jax import lax
from jax.experimental import pallas as pl
from jax.experimental.pallas import tpu as pltpu

F32 = jnp.float32
BF16 = jnp.bfloat16
HI = lax.Precision.HIGHEST

NORM_EPS = 1e-6
ROPE_THETA = 10000.0
GRID_W = 64
N_MOD = 6
A_QK_DIM = 64
A_V_DIM = 128
B_HEAD = 64
B_DECAY_RANK = 64
B_ICL_RANK = 64
B_GATE_RANK = 160
B_GN_EPS = 64e-5
HY_EMB = 33
HY_ORDER = 64
HY_TARGET = 1e-2
HY_FAST_PCT = 0.3
HY_SLOW_PCT = 1.5
D_HEAD = 128
D_KV_GROUP = 4

LANE = 128
VMEM_LIMIT = 56 * 1024 * 1024
LOG2E = 1.4426950408889634


def _cp(sem, vmem=VMEM_LIMIT):
    return pltpu.CompilerParams(dimension_semantics=sem, vmem_limit_bytes=vmem)


def _pick(n, pref, step=8):
    t = max(step, min(n, pref) // step * step)
    while n % t:
        t -= step
    return t


def _mods_body(c_ref, w_ref, b_ref, o_ref):
    c = c_ref[...]
    s = c * jax.nn.sigmoid(c)
    o_ref[...] = jnp.dot(s, w_ref[...], preferred_element_type=F32, precision=HI) + b_ref[...]


def _mods(cvec, w_mod, b_mod):
    m, d = cvec.shape
    n = w_mod.shape[1]
    tn = _pick(n, 1024, LANE)
    return pl.pallas_call(
        _mods_body,
        grid=(n // tn,),
        in_specs=[pl.BlockSpec((m, d), lambda j: (0, 0)),
                  pl.BlockSpec((d, tn), lambda j: (0, j)),
                  pl.BlockSpec((1, tn), lambda j: (0, j))],
        out_specs=pl.BlockSpec((m, tn), lambda j: (0, j)),
        out_shape=jax.ShapeDtypeStruct((m, n), F32),
        compiler_params=_cp(("arbitrary",)),
        name="mods",
    )(cvec, w_mod, b_mod.reshape(1, n))


def _norm_mod(x, a, b):
    ms = jnp.mean(x * x, axis=-1, keepdims=True)
    return x * lax.rsqrt(ms + NORM_EPS) * a + b


def _inproj_body(x_ref, a_ref, b_ref, w_ref, o_ref, xn_ref):
    @pl.when(pl.program_id(2) == 0)
    def _():
        xn_ref[...] = _norm_mod(x_ref[0], a_ref[0], b_ref[0]).astype(BF16)

    o_ref[0] = jnp.dot(xn_ref[...], w_ref[...], preferred_element_type=F32).astype(o_ref.dtype)


def _inproj(x, a, b, w, out_dtype=BF16, tm_pref=1024, tn_pref=512):
    bsz, n, d = x.shape
    nn = w.shape[1]
    tm, tn = _pick(n, tm_pref), _pick(nn, tn_pref, LANE)
    return pl.pallas_call(
        _inproj_body,
        grid=(bsz, n // tm, nn // tn),
        in_specs=[pl.BlockSpec((1, tm, d), lambda bi, i, j: (bi, i, 0)),
                  pl.BlockSpec((1, 1, d), lambda bi, i, j: (bi, 0, 0)),
                  pl.BlockSpec((1, 1, d), lambda bi, i, j: (bi, 0, 0)),
                  pl.BlockSpec((d, tn), lambda bi, i, j: (0, j))],
        out_specs=pl.BlockSpec((1, tm, tn), lambda bi, i, j: (bi, i, j)),
        out_shape=jax.ShapeDtypeStruct((bsz, n, nn), out_dtype),
        scratch_shapes=[pltpu.VMEM((tm, d), BF16)],
        compiler_params=_cp(("parallel", "parallel", "arbitrary")),
        name="inproj",
    )(x, a, b, w)


def _outproj_body(oa_ref, ob_ref, wa_ref, wb_ref, x_ref, g_ref, y_ref):
    acc = jnp.dot(oa_ref[0], wa_ref[...], preferred_element_type=F32)
    acc += jnp.dot(ob_ref[0], wb_ref[...], preferred_element_type=F32)
    y_ref[0] = x_ref[0] + g_ref[0] * acc


def _outproj(oa, ob, w, x, g, tm_pref=1024, tn_pref=512):
    bsz, n, ka = oa.shape
    kb = ob.shape[2]
    d = w.shape[1]
    tm, tn = _pick(n, tm_pref), _pick(d, tn_pref, LANE)
    return pl.pallas_call(
        _outproj_body,
        grid=(bsz, n // tm, d // tn),
        in_specs=[pl.BlockSpec((1, tm, ka), lambda bi, i, j: (bi, i, 0)),
                  pl.BlockSpec((1, tm, kb), lambda bi, i, j: (bi, i, 0)),
                  pl.BlockSpec((ka, tn), lambda bi, i, j: (0, j)),
                  pl.BlockSpec((kb, tn), lambda bi, i, j: (0, j)),
                  pl.BlockSpec((1, tm, tn), lambda bi, i, j: (bi, i, j)),
                  pl.BlockSpec((1, 1, tn), lambda bi, i, j: (bi, 0, j))],
        out_specs=pl.BlockSpec((1, tm, tn), lambda bi, i, j: (bi, i, j)),
        out_shape=jax.ShapeDtypeStruct((bsz, n, d), F32),
        compiler_params=_cp(("parallel", "parallel", "parallel")),
        name="outproj",
    )(oa, ob, w[:ka], w[ka:], x, g)


def _mlp_body(x_ref, a_ref, b_ref, g_ref, w1_ref, w2_ref, fg_ref, y_ref, xn_ref, acc_ref, *, nf, final):
    f = pl.program_id(2)

    @pl.when(f == 0)
    def _():
        xn_ref[...] = _norm_mod(x_ref[0], a_ref[0], b_ref[0]).astype(BF16)
        acc_ref[...] = jnp.zeros_like(acc_ref)

    h = jnp.dot(xn_ref[...], w1_ref[...], preferred_element_type=F32)
    h = jnp.square(jnp.maximum(h, 0.0)).astype(BF16)
    acc_ref[...] += jnp.dot(h, w2_ref[...], preferred_element_type=F32)

    @pl.when(f == nf - 1)
    def _():
        y = x_ref[0] + g_ref[0] * acc_ref[...]
        if final:
            ms = jnp.mean(y * y, axis=-1, keepdims=True)
            y = y * lax.rsqrt(ms + NORM_EPS) * fg_ref[...]
        y_ref[0] = y


def _mlp(x, a, b, g, w1, w2, final_g=None, tm_pref=512, tf_pref=512):
    bsz, n, d = x.shape
    dff = w1.shape[1]
    tm, tf = _pick(n, tm_pref), _pick(dff, tf_pref, LANE)
    nf = dff // tf
    final = final_g is not None
    fg = (final_g if final else jnp.ones((d,), F32)).reshape(1, d).astype(F32)
    return pl.pallas_call(
        functools.partial(_mlp_body, nf=nf, final=final),
        grid=(bsz, n // tm, nf),
        in_specs=[pl.BlockSpec((1, tm, d), lambda bi, i, f: (bi, i, 0)),
                  pl.BlockSpec((1, 1, d), lambda bi, i, f: (bi, 0, 0)),
                  pl.BlockSpec((1, 1, d), lambda bi, i, f: (bi, 0, 0)),
                  pl.BlockSpec((1, 1, d), lambda bi, i, f: (bi, 0, 0)),
                  pl.BlockSpec((d, tf), lambda bi, i, f: (0, f)),
                  pl.BlockSpec((tf, d), lambda bi, i, f: (f, 0)),
                  pl.BlockSpec((1, d), lambda bi, i, f: (0, 0))],
        out_specs=pl.BlockSpec((1, tm, d), lambda bi, i, f: (bi, i, 0)),
        out_shape=jax.ShapeDtypeStruct((bsz, n, d), F32),
        scratch_shapes=[pltpu.VMEM((tm, d), BF16), pltpu.VMEM((tm, d), F32)],
        compiler_params=_cp(("parallel", "parallel", "arbitrary")),
        name="mlp",
    )(x, a, b, g, w1, w2, fg)


def _rope_tables(n, dim, reps):
    rows = n // GRID_W
    row = jnp.repeat(jnp.arange(rows, dtype=F32), GRID_W)
    col = jnp.tile(jnp.arange(GRID_W, dtype=F32), rows)
    half = dim // 2
    inv = ROPE_THETA ** (-jnp.arange(0, half, 2, dtype=F32) / half)
    ar, ac = row[:, None] * inv, col[:, None] * inv
    cos = jnp.concatenate([jnp.cos(ar), jnp.cos(ar), jnp.cos(ac), jnp.cos(ac)], axis=-1)
    sin = jnp.concatenate([-jnp.sin(ar), jnp.sin(ar), -jnp.sin(ac), jnp.sin(ac)], axis=-1)
    return jnp.tile(cos, (1, reps)), jnp.tile(sin, (1, reps))


def _swap_matrix(dim, reps):
    q = dim // 4
    width = dim * reps
    p = np.zeros((width, width), np.float32)
    for j in range(width):
        base, r = (j // dim) * dim, j % dim
        axis, which, f = r // (2 * q), (r % (2 * q)) // q, r % q
        p[base + axis * 2 * q + (1 - which) * q + f, j] = 1.0
    return p


def _rope_norm(x, cos_ref, sin_ref, p_ref, g_ref, gs_ref, *, norm, rope, scale=1.0):
    y = x.astype(F32)
    if rope:
        ys = jnp.dot(x, p_ref[...], preferred_element_type=F32)
    if norm:
        rs = lax.rsqrt(jnp.mean(y * y, axis=-1, keepdims=True) + NORM_EPS)
        y = y * rs * g_ref[...]
        if rope:
            ys = ys * rs * gs_ref[...]
    if rope:
        y = y * cos_ref[...] + ys * sin_ref[...]
    return y * scale if scale != 1.0 else y


def _rope_operands(n, dim, gain, rope):
    reps = LANE // dim
    if rope:
        cos, sin = _rope_tables(n, dim, reps)
        pm = jnp.asarray(_swap_matrix(dim, reps), BF16)
    else:
        cos = sin = jnp.zeros((n, LANE), F32)
        pm = jnp.zeros((LANE, LANE), BF16)
    if gain is not None:
        g = jnp.tile(gain.astype(F32), reps).reshape(1, LANE)
        gs = jnp.dot(g, jnp.asarray(_swap_matrix(dim, reps)), precision=HI)
    else:
        g = gs = jnp.ones((1, LANE), F32)
    return cos, sin, pm, g, gs


def _kprep_body(x_ref, cos_ref, sin_ref, p_ref, g_ref, gs_ref, o_ref, *, heads, norm, rope):
    for h in range(heads):
        sl = slice(h * LANE, (h + 1) * LANE)
        y = _rope_norm(x_ref[0, :, sl], cos_ref, sin_ref, p_ref, g_ref, gs_ref, norm=norm, rope=rope)
        o_ref[0, :, sl] = y.astype(o_ref.dtype)


def _kprep(p, col0, heads, *, dim, gain=None, rope=True, tq_pref=512):
    bsz, n, _ = p.shape
    tq = _pick(n, tq_pref)
    w = heads * LANE
    cb = col0 // w
    cos, sin, pm, g, gs = _rope_operands(n, dim, gain, rope)
    return pl.pallas_call(
        functools.partial(_kprep_body, heads=heads, norm=gain is not None, rope=rope),
        grid=(bsz, n // tq),
        in_specs=[pl.BlockSpec((1, tq, w), lambda b, i: (b, i, cb)),
                  pl.BlockSpec((tq, LANE), lambda b, i: (i, 0)),
                  pl.BlockSpec((tq, LANE), lambda b, i: (i, 0)),
                  pl.BlockSpec((LANE, LANE), lambda b, i: (0, 0)),
                  pl.BlockSpec((1, LANE), lambda b, i: (0, 0)),
                  pl.BlockSpec((1, LANE), lambda b, i: (0, 0))],
        out_specs=pl.BlockSpec((1, tq, w), lambda b, i: (b, i, 0)),
        out_shape=jax.ShapeDtypeStruct((bsz, n, w), BF16),
        compiler_params=_cp(("parallel", "parallel")),
        name="kprep",
    )(p, cos, sin, pm, g, gs)


FLASH_KEYS = 1280
FLASH_HEADROOM = 64.0


def _knorm_body(k_ref, o_ref):
    k = k_ref[0].astype(F32)
    nrm2 = jnp.max(jnp.sum(k * k, axis=1, keepdims=True), axis=0, keepdims=True)
    o_ref[0, 0] = jnp.broadcast_to(jnp.sqrt(nrm2), (8, LANE))


def _key_block_norms(k, kcol0, heads, ts):
    bsz, nk, _ = k.shape
    nsub = nk // ts
    kb = kcol0 // LANE
    out = pl.pallas_call(
        _knorm_body,
        grid=(bsz, heads, nsub),
        in_specs=[pl.BlockSpec((1, ts, LANE), lambda b, h, c: (b, c, kb + h))],
        out_specs=pl.BlockSpec((1, 1, 8, LANE), lambda b, h, c: (b, h, c, 0)),
        out_shape=jax.ShapeDtypeStruct((bsz, heads, nsub * 8, LANE), F32),
        compiler_params=_cp(("parallel", "parallel", "parallel")),
        name="key_block_norms",
    )(k)
    return out[:, :, ::8, 0].reshape(-1)


def _flash_body(kmax_ref, qr_ref, cos_ref, sin_ref, pm_ref, g_ref, gs_ref, k_ref, v_ref, e1_ref, e2_ref, o_ref,
                q_ref, m_ref, l_ref, acc_ref, thr_ref, *, ts, nsub, mode, tq, norm, rope, scale):
    base = (pl.program_id(0) * pl.num_programs(1) + pl.program_id(1)) * nsub
    prep = functools.partial(_rope_norm, cos_ref=cos_ref, sin_ref=sin_ref, p_ref=pm_ref, g_ref=g_ref, gs_ref=gs_ref,
                             norm=norm, rope=rope, scale=scale)
    if mode == "diff":
        y = prep(qr_ref[0])
        lane = lax.broadcasted_iota(jnp.int32, y.shape, 1)
        q_ref[:tq] = jnp.where(lane < A_QK_DIM, y, 0.0).astype(BF16)
        q_ref[tq:] = jnp.where(lane >= A_QK_DIM, y, 0.0).astype(BF16)
    else:
        for j in range(D_KV_GROUP):
            q_ref[j * tq:(j + 1) * tq] = prep(qr_ref[0, :, j * LANE:(j + 1) * LANE]).astype(BF16)
    q = q_ref[...]

    def scores(off, size):
        k = k_ref[0, pl.ds(off, size), :]
        return lax.dot_general(q, k, (((1,), (1,)), ((), ())), preferred_element_type=F32)

    def block(c):
        off = pl.multiple_of(c * ts, ts)
        return scores(off, ts), v_ref[0, pl.ds(off, ts), :]

    def wide(m):
        return jnp.concatenate([m] * (ts // LANE), axis=1)

    m0 = jnp.broadcast_to(jnp.max(scores(0, 2 * LANE), axis=1, keepdims=True), m_ref.shape)
    m_ref[...] = m0
    l_ref[...] = jnp.zeros_like(l_ref)
    acc_ref[...] = jnp.zeros_like(acc_ref)
    qf = q.astype(F32)
    qn = jnp.sqrt(jnp.sum(qf * qf, axis=1, keepdims=True))
    thr_ref[0] = jnp.min((m0[:, :1] + FLASH_HEADROOM) / qn)

    def step(c, carry):
        fixed = kmax_ref[base + c] * 1.001 <= thr_ref[0]

        @pl.when(fixed)
        def _():
            s, v = block(c)
            p = jnp.exp2(s - wide(m_ref[...]))
            l_ref[...] += jnp.sum(p, axis=1, keepdims=True)
            acc_ref[...] += jnp.dot(p.astype(BF16), v, preferred_element_type=F32)

        @pl.when(jnp.logical_not(fixed))
        def _():
            s, v = block(c)
            m_prev = m_ref[...]
            m_new = jnp.maximum(m_prev, jnp.max(s, axis=1, keepdims=True))
            alpha = jnp.exp2(m_prev - m_new)
            p = jnp.exp2(s - wide(m_new))
            l_ref[...] = alpha * l_ref[...] + jnp.sum(p, axis=1, keepdims=True)
            acc_ref[...] = alpha * acc_ref[...] + jnp.dot(p.astype(BF16), v, preferred_element_type=F32)
            m_ref[...] = m_new

        return carry

    lax.fori_loop(0, nsub, step, 0)

    o = acc_ref[...] / l_ref[...]
    if mode == "diff":
        d = o[:tq] - e1_ref[...] * o[tq:]
        ms = jnp.mean(d * d, axis=-1, keepdims=True)
        o_ref[0] = (d * lax.rsqrt(ms + NORM_EPS) * e2_ref[...]).astype(o_ref.dtype)
    else:
        for j in range(D_KV_GROUP):
            o_ref[0, :, j * LANE:(j + 1) * LANE] = o[j * tq:(j + 1) * tq].astype(o_ref.dtype)


def _flash(pq, qcol0, k, v, kcol0, vcol0, e1, e2, *, mode, hk, dim, tq, gain=None, rope=True):
    bsz, n, _ = pq.shape
    nk = k.shape[1]
    g = 2 if mode == "diff" else D_KV_GROUP
    mq = g * tq
    qw = LANE if mode == "diff" else D_KV_GROUP * LANE
    qb = qcol0 // qw
    ts = _pick(nk, FLASH_KEYS, 2 * LANE)
    kb, vb = kcol0 // LANE, vcol0 // LANE
    ow = LANE if mode == "diff" else D_KV_GROUP * LANE
    kmax = _key_block_norms(k, kcol0, hk, ts)
    cos, sin, pm, gq, gqs = _rope_operands(n, dim, gain, rope)
    const = lambda shape: pl.BlockSpec(shape, lambda b, h, i: (0, 0))
    return pl.pallas_call(
        functools.partial(_flash_body, ts=ts, nsub=nk // ts, mode=mode, tq=tq, norm=gain is not None, rope=rope,
                          scale=dim ** -0.5 * LOG2E),
        grid=(bsz, hk, n // tq),
        in_specs=[pl.BlockSpec(memory_space=pltpu.SMEM),
                  pl.BlockSpec((1, tq, qw), lambda b, h, i: (b, i, qb + h)),
                  pl.BlockSpec((tq, LANE), lambda b, h, i: (i, 0)),
                  pl.BlockSpec((tq, LANE), lambda b, h, i: (i, 0)),
                  const((LANE, LANE)), const((1, LANE)), const((1, LANE)),
                  pl.BlockSpec((1, nk, LANE), lambda b, h, i: (b, 0, kb + h)),
                  pl.BlockSpec((1, nk, LANE), lambda b, h, i: (b, 0, vb + h)),
                  const((1, LANE)), const((1, LANE))],
        out_specs=pl.BlockSpec((1, tq, ow), lambda b, h, i: (b, i, h)),
        out_shape=jax.ShapeDtypeStruct((bsz, n, hk * ow), BF16),
        scratch_shapes=[pltpu.VMEM((mq, LANE), BF16), pltpu.VMEM((mq, LANE), F32), pltpu.VMEM((mq, LANE), F32),
                        pltpu.VMEM((mq, LANE), F32), pltpu.SMEM((1,), F32)],
        compiler_params=_cp(("parallel", "parallel", "arbitrary")),
        name="flash_" + mode,
    )(kmax, pq, cos, sin, pm, gq, gqs, k, v, e1, e2)


def _diff_attention(pa_lat, pa_ctx, lam, lam_init, subln_g):
    n, nctx = pa_lat.shape[1], pa_ctx.shape[1]
    aw = pa_lat.shape[2] // 3
    heads = aw // LANE
    k_l = _kprep(pa_lat, aw, heads, dim=A_QK_DIM)
    k_all = jnp.concatenate([k_l, pa_ctx[..., aw:2 * aw]], axis=1)
    v_all = jnp.concatenate([pa_lat[..., 2 * aw:], pa_ctx[..., 2 * aw:]], axis=1)
    e1 = jnp.full((1, LANE), lam, F32)
    e2 = (subln_g.astype(F32) * (1.0 - lam_init)).reshape(1, LANE)
    o_lat = _flash(pa_lat, 0, k_all, v_all, 0, 0, e1, e2, mode="diff", hk=heads, dim=A_QK_DIM, tq=_pick(n, 512))
    o_ctx = _flash(pa_ctx, 0, pa_ctx, pa_ctx, aw, 2 * aw, e1, e2, mode="diff", hk=heads, dim=A_QK_DIM,
                   tq=_pick(nctx, 512), rope=False)
    return o_lat, o_ctx


def _gqa_attention(p_lat, pkv_ctx, qcol0, q_norm_g, k_norm_g):
    n = p_lat.shape[1]
    hq, hk = 2 * D_KV_GROUP, 2
    kcol0 = qcol0 + hq * D_HEAD
    vcol0 = kcol0 + hk * D_HEAD
    k_l = _kprep(p_lat, kcol0, hk, dim=D_HEAD, gain=k_norm_g)
    k_c = _kprep(pkv_ctx, 0, hk, dim=D_HEAD, gain=k_norm_g, rope=False)
    k_all = jnp.concatenate([k_l, k_c], axis=1)
    v_all = jnp.concatenate([p_lat[..., vcol0:vcol0 + hk * D_HEAD], pkv_ctx[..., hk * D_HEAD:]], axis=1)
    dummy = jnp.zeros((1, LANE), F32)
    return _flash(p_lat, qcol0, k_all, v_all, 0, 0, dummy, dummy, mode="gqa", hk=hk, dim=D_HEAD,
                  tq=_pick(n, 256), gain=q_norm_g)


RW_CHUNK = 64
RW_HALO = 8


def _mm_hi(a, b):
    return jnp.dot(a, b, preferred_element_type=F32, precision=HI)


def _mm_bf(a, b):
    return jnp.dot(a.astype(BF16), b.astype(BF16), preferred_element_type=F32)


def _split3(x):
    x1 = x.astype(BF16)
    r1 = x - x1.astype(F32)
    x2 = r1.astype(BF16)
    return x1, x2, (r1 - x2.astype(F32)).astype(BF16)


def _mm_x3(a, b):
    ah, al, _ = _split3(a)
    bh, bl, _ = _split3(b)
    d = lambda u, w: jnp.dot(u, w, preferred_element_type=F32)
    return d(ah, bh) + (d(ah, bl) + d(al, bh))


def _mm_sel(z, sel):
    z1, z2, z3 = _split3(z)
    d = lambda u: jnp.dot(u, sel, preferred_element_type=F32)
    return d(z1) + (d(z2) + d(z3))


def _rwprep_body(x_ref, hp_ref, hn_ref, mup_ref, mun_ref, vec_ref, w2f_ref, w2b_ref, a2f_ref, a2b_ref,
                 g2_ref, sel_ref, selt_ref,
                 r_ref, v_ref, kk_ref, lwf_ref, bf_ref, kf_ref, lwb_ref, bb_ref, kb_ref, g_ref, bonus_ref,
                 *, nb, tm, bw):
    i = pl.program_id(1)
    x = x_ref[0].astype(F32)
    row = lax.broadcasted_iota(jnp.int32, x.shape, 0)
    prev_edge = jnp.where(i > 0, hp_ref[0, RW_HALO - 1:RW_HALO, :].astype(F32), 0.0)
    next_edge = jnp.where(i < nb - 1, hn_ref[0, 0:1, :].astype(F32), 0.0)
    prev = jnp.where(row == 0, prev_edge, pltpu.roll(x, 1, 0))
    nxt = jnp.where(row == tm - 1, next_edge, pltpu.roll(x, tm - 1, 0))
    xs = x + mup_ref[...] * (prev - x) + mun_ref[...] * (nxt - x)
    r, k, v = xs[:, :bw], xs[:, bw:2 * bw], xs[:, 2 * bw:3 * bw]
    lr = xs[:, 3 * bw:3 * bw + LANE]
    gd = xs[:, 3 * bw + LANE:3 * bw + 3 * LANE]
    k_k, k_a, r_k = vec_ref[0:1], vec_ref[1:2], vec_ref[2:3]
    w0f, a0f, w0b, a0b = vec_ref[3:4], vec_ref[4:5], vec_ref[5:6], vec_ref[6:7]
    sel, selt = sel_ref[...], selt_ref[...]

    def head_sum(z):
        return _mm_sel(_mm_sel(z, sel), selt)

    kk = k * k_k
    kk = kk * lax.rsqrt(head_sum(kk * kk) + 1e-12)
    g_ref[0] = _mm_bf(jax.nn.sigmoid(gd), g2_ref[...]).astype(g_ref.dtype)
    th = jnp.tanh(lr)
    ksum = jnp.zeros_like(k)
    for (w0, a0, w2_ref, a2_ref, lw_ref, b_ref, kd_ref) in (
            (w0f, a0f, w2f_ref, a2f_ref, lwf_ref, bf_ref, kf_ref),
            (w0b, a0b, w2b_ref, a2b_ref, lwb_ref, bb_ref, kb_ref)):
        w_log = -jax.nn.softplus(-(w0 + _mm_bf(th, w2_ref[...]))) - 0.5
        logw = -jnp.exp(w_log)
        a = jax.nn.sigmoid(a0 + _mm_bf(lr, a2_ref[...]))
        k_d = k * (1.0 + (a - 1.0) * k_a)
        ksum = ksum + k_d
        bvec = kk * a
        for h in range(bw // B_HEAD):
            sl = slice(h * B_HEAD, (h + 1) * B_HEAD)
            lw_ref[0, h] = logw[:, sl]
            b_ref[0, h] = bvec[:, sl]
            kd_ref[0, h] = k_d[:, sl]
    for h in range(bw // B_HEAD):
        sl = slice(h * B_HEAD, (h + 1) * B_HEAD)
        r_ref[0, h] = r[:, sl]
        v_ref[0, h] = v[:, sl]
        kk_ref[0, h] = kk[:, sl]
    bonus_ref[0] = head_sum(r * ksum * r_k) * v


def _rwkv_prepare(pr, p):
    bsz, n, cols = pr.shape
    bw = p["w0_f"].shape[0]
    nh = bw // B_HEAD
    tm = _pick(n, 256)
    nb = n // tm
    hb = tm // RW_HALO
    pad = cols - 3 * bw

    def padded(vec):
        return jnp.concatenate([vec.astype(F32), jnp.zeros((cols - vec.shape[0],), F32)]).reshape(1, cols)

    vecs = jnp.stack([p["k_k"], p["k_a"], p["r_k"].reshape(-1), p["w0_f"], p["a0_f"], p["w0_b"], p["a0_b"],
                      jnp.zeros((bw,), F32)]).astype(F32)
    zr = jnp.zeros((B_DECAY_RANK, bw), F32)
    w2 = {d: jnp.concatenate([p["w2_" + d].astype(F32), zr], axis=0) for d in "fb"}
    a2 = {d: jnp.concatenate([zr, p["a2_" + d].astype(F32)], axis=0) for d in "fb"}
    g2 = jnp.concatenate([p["g2"].astype(F32), jnp.zeros((2 * LANE - B_GATE_RANK, bw), F32)], axis=0)
    sel_np = np.zeros((bw, LANE), np.float32)
    sel_np[np.arange(bw), np.arange(bw) // B_HEAD] = 1.0
    sel, selt = jnp.asarray(sel_np, BF16), jnp.asarray(sel_np.T, BF16)
    w2 = {d: w2[d].astype(BF16) for d in "fb"}
    a2 = {d: a2[d].astype(BF16) for d in "fb"}
    g2 = g2.astype(BF16)
    assert pad == 3 * LANE and B_DECAY_RANK + B_ICL_RANK == LANE

    full = lambda shape: pl.BlockSpec(shape, lambda b, i: tuple(0 for _ in shape))
    hm = pl.BlockSpec((1, nh, tm, B_HEAD), lambda b, i: (b, 0, i, 0))
    tok = pl.BlockSpec((1, tm, bw), lambda b, i: (b, i, 0))
    hm_shape = jax.ShapeDtypeStruct((bsz, nh, n, B_HEAD), F32)
    outs = pl.pallas_call(
        functools.partial(_rwprep_body, nb=nb, tm=tm, bw=bw),
        grid=(bsz, nb),
        in_specs=[pl.BlockSpec((1, tm, cols), lambda b, i: (b, i, 0)),
                  pl.BlockSpec((1, RW_HALO, cols), lambda b, i: (b, jnp.maximum(i * hb - 1, 0), 0)),
                  pl.BlockSpec((1, RW_HALO, cols), lambda b, i: (b, jnp.minimum((i + 1) * hb, nb * hb - 1), 0)),
                  full((1, cols)), full((1, cols)), full((8, bw)),
                  full((LANE, bw)), full((LANE, bw)), full((LANE, bw)), full((LANE, bw)),
                  full((2 * LANE, bw)), full((bw, LANE)), full((LANE, bw))],
        out_specs=[hm] * 9 + [tok, tok],
        out_shape=[hm_shape] * 9 + [jax.ShapeDtypeStruct((bsz, n, bw), BF16),
                                    jax.ShapeDtypeStruct((bsz, n, bw), F32)],
        compiler_params=_cp(("parallel", "parallel")),
        name="rwkv_prepare",
    )(pr, pr, pr, padded(p["mu_prev"]), padded(p["mu_next"]), vecs, w2["f"], w2["b"], a2["f"], a2["b"],
      g2, sel, selt)
    r, v, kk, lwf, bf, kf, lwb, bb, kb, g, bonus = outs
    return dict(r=r, v=v, kk=kk, g=g, bonus=bonus, f=(lwf, bf, kf), b=(lwb, bb, kb))


RW_PREC = dict(lm="bf16", inv="bf16", pq="bf16", state="bf16", out="bf16")


def _bmm(a, b, dims, prec="f32"):
    if prec == "f32":
        return jnp.einsum(dims, a, b, preferred_element_type=F32, precision=HI)
    ah, bh = a.astype(BF16), b.astype(BF16)
    mm = lambda u, w: jnp.einsum(dims, u, w, preferred_element_type=F32)
    if prec == "bf16":
        return mm(ah, bh)
    al, bl = (a - ah.astype(F32)).astype(BF16), (b - bh.astype(F32)).astype(BF16)
    return mm(ah, bh) + (mm(ah, bl) + mm(al, bh))


def _bmm_exact_lhs(a, b, dims):
    ab = a.astype(BF16)
    b1 = b.astype(BF16)
    r1 = b - b1.astype(F32)
    b2 = r1.astype(BF16)
    b3 = (r1 - b2.astype(F32)).astype(BF16)
    mm = lambda w: jnp.einsum(dims, ab, w, preferred_element_type=F32)
    return mm(b1) + (mm(b2) + mm(b3))


def _rw_chunk(r, v, kk, b, kd, lw, s, reverse, ck):
    nh = r.shape[0]
    ti = lax.broadcasted_iota(jnp.int32, (ck, ck), 0)
    si = lax.broadcasted_iota(jnp.int32, (ck, ck), 1)
    incl = (si >= ti) if reverse else (si <= ti)
    strict = (si > ti) if reverse else (si < ti)
    tri = jnp.broadcast_to(incl.astype(F32)[None], (nh, ck, ck))
    pc = RW_PREC
    cum = _bmm_exact_lhs(tri, lw, "hts,hsj->htj")
    total = cum[:, 0:1, :] if reverse else cum[:, ck - 1:ck, :]
    at = -kk * jnp.exp(cum - lw)
    winv = jnp.exp(-cum)
    bt, kt = b * winv, kd * winv
    rt = r * jnp.exp(cum)
    wrest = jnp.exp(total - cum)
    btw, ktw = b * wrest, kd * wrest

    def masked(m, z):
        return jnp.where(m[None], z, 0.0)

    lab = masked(strict, _bmm(at, bt, "htj,hsj->hts", pc["lm"]))
    lak = masked(strict, _bmm(at, kt, "htj,hsj->hts", pc["lm"]))
    mrb = masked(incl, _bmm(rt, bt, "htj,hsj->hts", pc["lm"]))
    mrk = masked(incl, _bmm(rt, kt, "htj,hsj->hts", pc["lm"]))
    eye = jnp.broadcast_to((ti == si).astype(F32)[None], (nh, ck, ck))
    tinv, pw = eye + lab, lab
    span = 2
    while span < ck:
        pw = _bmm(pw, pw, "hts,hsu->htu", pc["inv"])
        tinv = tinv + _bmm(tinv, pw, "hts,hsu->htu", pc["inv"])
        span *= 2
    p1 = _bmm(tinv, at, "hts,hsj->htj", pc["pq"])
    q1 = _bmm(tinv, _bmm(lak, v, "hts,hsi->hti", pc["pq"]), "hts,hsi->hti", pc["pq"])
    u = _bmm(p1, s, "htj,hij->hti", pc["state"]) + q1
    y = (_bmm(rt, s, "htj,hij->hti", pc["out"]) + _bmm(mrb, u, "hts,hsi->hti", pc["out"])
         + _bmm(mrk, v, "hts,hsi->hti", pc["out"]))
    s_new = (s * jnp.exp(total) + _bmm(u, btw, "hti,htj->hij", pc["state"])
             + _bmm(v, ktw, "hti,htj->hij", pc["state"]))
    return y, s_new


def _rwscan_body(rf_ref, vf_ref, kkf_ref, bf_ref, kdf_ref, lwf_ref, rb_ref, vb_ref, kkb_ref, bb_ref, kdb_ref, lwb_ref,
                 s0f_ref, s0b_ref, yf_ref, yb_ref, sTf_ref, sTb_ref, sf_ref, sb_ref, *, nc, ck):
    c = pl.program_id(1)

    @pl.when(c == 0)
    def _():
        sf_ref[...] = s0f_ref[0]
        sb_ref[...] = s0b_ref[0]

    yf, sf = _rw_chunk(rf_ref[0], vf_ref[0], kkf_ref[0], bf_ref[0], kdf_ref[0], lwf_ref[0], sf_ref[...], False, ck)
    yb, sb = _rw_chunk(rb_ref[0], vb_ref[0], kkb_ref[0], bb_ref[0], kdb_ref[0], lwb_ref[0], sb_ref[...], True, ck)
    yf_ref[0] = yf
    yb_ref[0] = yb
    sf_ref[...] = sf
    sb_ref[...] = sb

    @pl.when(c == nc - 1)
    def _():
        sTf_ref[0] = sf
        sTb_ref[0] = sb


def _rwkv_scan(r, v, kk, fwd, bwd, s0f, s0b):
    bsz, nh, n, hd = r.shape
    ck = min(RW_CHUNK, n)
    nc = n // ck
    seq_f = pl.BlockSpec((1, nh, ck, hd), lambda b_, c: (b_, 0, c, 0))
    seq_b = pl.BlockSpec((1, nh, ck, hd), lambda b_, c: (b_, 0, nc - 1 - c, 0))
    st = pl.BlockSpec((1, nh, hd, hd), lambda b_, c: (b_, 0, 0, 0))
    y_shape = jax.ShapeDtypeStruct((bsz, nh, n, hd), F32)
    s_shape = jax.ShapeDtypeStruct((bsz, nh, hd, hd), F32)
    (lwf, bf, kf), (lwb, bb, kb) = fwd, bwd
    return pl.pallas_call(
        functools.partial(_rwscan_body, nc=nc, ck=ck),
        grid=(bsz, nc),
        in_specs=[seq_f] * 6 + [seq_b] * 6 + [st, st],
        out_specs=[seq_f, seq_b, st, st],
        out_shape=[y_shape, y_shape, s_shape, s_shape],
        scratch_shapes=[pltpu.VMEM((nh, hd, hd), F32), pltpu.VMEM((nh, hd, hd), F32)],
        compiler_params=_cp(("parallel", "arbitrary")),
        name="rwkv_scan",
    )(r, v, kk, bf, kf, lwf, r, v, kk, bb, kb, lwb, s0f, s0b)


def _rwpost_body(yf_ref, yb_ref, g_ref, bonus_ref, lg_ref, lb_ref, o_ref):
    y = yf_ref[0] + yb_ref[0]
    mu = jnp.mean(y, axis=-1, keepdims=True)
    var = jnp.mean(jnp.square(y - mu), axis=-1, keepdims=True)
    y = (y - mu) * lax.rsqrt(var + B_GN_EPS)
    ytok = jnp.concatenate([y[h] for h in range(y.shape[0])], axis=-1)
    ytok = ytok * lg_ref[...] + lb_ref[...] + bonus_ref[0]
    o_ref[0] = (ytok * g_ref[0].astype(F32)).astype(o_ref.dtype)


def _rwkv_post(y_f, y_b, g, bonus, lnx_g, lnx_b):
    bsz, nh, n, hd = y_f.shape
    bw = nh * hd
    tm = _pick(n, 256)
    hm = pl.BlockSpec((1, nh, tm, hd), lambda b, i: (b, 0, i, 0))
    tok = pl.BlockSpec((1, tm, bw), lambda b, i: (b, i, 0))
    vec = pl.BlockSpec((1, bw), lambda b, i: (0, 0))
    return pl.pallas_call(
        _rwpost_body,
        grid=(bsz, n // tm),
        in_specs=[hm, hm, tok, tok, vec, vec],
        out_specs=tok,
        out_shape=jax.ShapeDtypeStruct((bsz, n, bw), BF16),
        compiler_params=_cp(("parallel", "parallel")),
        name="rwkv_post",
    )(y_f, y_b, g, bonus, lnx_g.reshape(1, bw).astype(F32), lnx_b.reshape(1, bw).astype(F32))


def _rwkv_run(prep, states0, p):
    y_f, y_b, s_f, s_b = _rwkv_scan(prep["r"], prep["v"], prep["kk"], prep["f"], prep["b"], states0[0], states0[1])
    return _rwkv_post(y_f, y_b, prep["g"], prep["bonus"], p["lnx_g"], p["lnx_b"]), (s_f, s_b)


HY_N2 = 128
HY_HALO = 8


def _hypre_body(x_ref, hp_ref, hn_ref, cw_ref, cb_ref, x0_ref, uv_ref, *, nb, tm, cw):
    i = pl.program_id(1)
    x = x_ref[0].astype(F32)
    row = lax.broadcasted_iota(jnp.int32, x.shape, 0)
    prev_edge = jnp.where(i > 0, hp_ref[0, HY_HALO - 1:HY_HALO, :].astype(F32), 0.0)
    next_edge = jnp.where(i < nb - 1, hn_ref[0, 0:1, :].astype(F32), 0.0)
    prev = jnp.where(row == 0, prev_edge, pltpu.roll(x, 1, 0))
    nxt = jnp.where(row == tm - 1, next_edge, pltpu.roll(x, tm - 1, 0))
    u = prev * cw_ref[0:1] + x * cw_ref[1:2] + nxt * cw_ref[2:3] + cb_ref[...]
    x0_ref[0] = u[:, :cw]
    uv_ref[0] = u[:, 2 * cw:3 * cw] * u[:, cw:2 * cw]


def _hyena_pre(p, conv_w, conv_b):
    bsz, n, _ = p.shape
    hc = conv_w.shape[1]
    cw = hc // 3
    tm = _pick(n, 256)
    nb = n // tm
    hb = tm // HY_HALO
    cwp = jnp.concatenate([conv_w.astype(F32), jnp.zeros((5, hc), F32)], axis=0)
    out = pl.BlockSpec((1, tm, cw), lambda b, i: (b, i, 0))
    return pl.pallas_call(
        functools.partial(_hypre_body, nb=nb, tm=tm, cw=cw),
        grid=(bsz, nb),
        in_specs=[pl.BlockSpec((1, tm, hc), lambda b, i: (b, i, 0)),
                  pl.BlockSpec((1, HY_HALO, hc), lambda b, i: (b, jnp.maximum(i * hb - 1, 0), 0)),
                  pl.BlockSpec((1, HY_HALO, hc), lambda b, i: (b, jnp.minimum((i + 1) * hb, nb * hb - 1), 0)),
                  pl.BlockSpec((8, hc), lambda b, i: (0, 0)),
                  pl.BlockSpec((1, hc), lambda b, i: (0, 0))],
        out_specs=[out, out],
        out_shape=[jax.ShapeDtypeStruct((bsz, n, cw), F32)] * 2,
        compiler_params=_cp(("parallel", "parallel")),
        name="hyena_pre",
    )(p, p, p, cwp, conv_b.reshape(1, hc).astype(F32))


def _hyfilt_body(z_ref, t_ref, z0_ref, w1_ref, w2_ref, w3_ref, w4_ref, w4b_ref, vec_ref, dl_ref,
                 k_ref, s_ref, *, half_tiles):
    i = pl.program_id(0)
    b1, b2, b3, fr = vec_ref[0:1], vec_ref[1:2], vec_ref[2:3], vec_ref[3:4]

    def mlp3(z):
        h = jnp.sin(fr * (_mm_hi(z, w1_ref[...]) + b1))
        h = jnp.sin(fr * (_mm_hi(h, w2_ref[...]) + b2))
        return jnp.sin(fr * (_mm_hi(h, w3_ref[...]) + b3))

    raw = _mm_hi(mlp3(z_ref[...]), w4_ref[...]) * jnp.exp(-t_ref[...] * dl_ref[...])

    @pl.when(i == 0)
    def _():
        s_ref[...] = jnp.zeros_like(s_ref)

    s_ref[...] += jnp.sum(jnp.abs(raw), axis=0, keepdims=True)
    hb0 = _mm_hi(mlp3(z0_ref[...]), w4b_ref[...])
    row = lax.broadcasted_iota(jnp.int32, raw.shape, 0)
    raw = raw + jnp.where((row == 0) & (i == 0), hb0[0:1], 0.0)
    k_ref[...] = jnp.where((row == 0) & (i == half_tiles), 0.0, raw)


def _hyena_kernel_taps(n, p):
    cw = p["f_w4"].shape[1] // 2
    f32 = lambda a: a.astype(F32)
    row = jnp.arange(2 * n, dtype=jnp.int32)
    pos = jnp.where(row < n, row, (2 * n - row) % n).astype(F32)[:, None]
    tt = pos * jnp.float32(1.0 / (n - 1))
    bands = (HY_EMB - 1) // 2
    ang = (2 * math.pi / n) * pos * jnp.linspace(1e-4, bands - 1, bands, dtype=F32)[None]
    zz = jnp.concatenate([tt, jnp.cos(ang), -jnp.sin(ang), jnp.zeros((2 * n, HY_ORDER - HY_EMB), F32)], axis=-1)
    w1 = jnp.concatenate([f32(p["f_w1"]), jnp.zeros((HY_ORDER - HY_EMB, HY_ORDER), F32)], axis=0)
    vecs = jnp.stack([f32(p["f_b1"]), f32(p["f_b2"]), f32(p["f_b3"]), f32(p["f_freq"])]
                     + [jnp.zeros((HY_ORDER,), F32)] * 4)
    deltas = jnp.abs(jnp.linspace(math.log(HY_TARGET) / HY_SLOW_PCT, math.log(HY_TARGET) / HY_FAST_PCT, cw,
                                  dtype=F32)).reshape(1, cw)
    tr = _pick(n, 512)
    half_tiles = n // tr
    full = lambda shape: pl.BlockSpec(shape, lambda i: tuple(0 for _ in shape))
    w4 = f32(p["f_w4"])
    return pl.pallas_call(
        functools.partial(_hyfilt_body, half_tiles=half_tiles),
        grid=(2 * half_tiles,),
        in_specs=[pl.BlockSpec((tr, HY_ORDER), lambda i: (i, 0)),
                  pl.BlockSpec((tr, 1), lambda i: (i, 0)),
                  full((8, HY_ORDER)), full((HY_ORDER, HY_ORDER)), full((HY_ORDER, HY_ORDER)),
                  full((HY_ORDER, HY_ORDER)),
                  pl.BlockSpec((HY_ORDER, cw), lambda i: (0, i // half_tiles)),
                  pl.BlockSpec((HY_ORDER, cw), lambda i: (0, 1)),
                  full((8, HY_ORDER)), full((1, cw))],
        out_specs=[pl.BlockSpec((tr, cw), lambda i: (i, 0)), pl.BlockSpec((1, cw), lambda i: (0, 0))],
        out_shape=[jax.ShapeDtypeStruct((2 * n, cw), F32), jax.ShapeDtypeStruct((1, cw), F32)],
        compiler_params=_cp(("arbitrary",)),
        name="hyena_filter",
    )(zz, tt, jnp.broadcast_to(zz[0:1], (8, HY_ORDER)), w1, f32(p["f_w2"]), f32(p["f_w3"]), w4, w4, vecs, deltas)


def _dft_consts(n):
    nn = 2 * n
    n2 = HY_N2
    n1 = nn // n2
    a1 = 2 * np.pi * np.outer(np.arange(n1), np.arange(n1)) / n1
    a2 = 2 * np.pi * np.outer(np.arange(n2), np.arange(n2)) / n2
    f1 = np.concatenate([np.cos(a1), -np.sin(a1)], axis=0)
    f1inv = np.concatenate([np.cos(a1), -np.sin(a1)], axis=1) / nn
    c2, s2 = np.cos(a2), -np.sin(a2)
    m2 = np.block([[c2, -s2], [s2, c2]])
    m2inv = np.block([[c2, s2], [-s2, c2]])
    at = 2 * np.pi * np.outer(np.arange(n2), np.arange(n1)) / nn
    tw = np.stack([np.cos(at), -np.sin(at)])
    c = lambda a: jnp.asarray(a.astype(np.float32))
    return dict(n1=n1, n2=n2, f1=c(f1), f1inv=c(f1inv), m2=c(m2), m2inv=c(m2inv),
                tw_s2=c(tw[:, :, :, None]),
                tw_f1=c(np.transpose(tw, (0, 2, 1))[:, :, :, None]))


def _dft1_body(f_ref, x_ref, tw_ref, o_ref, *, n1):
    a = _mm_x3(f_ref[...], x_ref[0])
    ar, ai = a[:n1], a[n1:]
    twr, twi = tw_ref[0, 0], tw_ref[1, 0]
    o_ref[0, 0] = ar * twr - ai * twi
    o_ref[0, 1] = ar * twi + ai * twr


def _dft_stage1(x, consts, rows):
    bsz, _, cw = x.shape
    n1, n2 = consts["n1"], consts["n2"]
    xv = x.reshape(bsz, rows, n2 * cw)
    f1 = consts["f1"][:, :rows]
    return pl.pallas_call(
        functools.partial(_dft1_body, n1=n1),
        grid=(bsz, n2),
        in_specs=[pl.BlockSpec((2 * n1, rows), lambda b, j: (0, 0)),
                  pl.BlockSpec((1, rows, cw), lambda b, j: (b, 0, j)),
                  pl.BlockSpec((2, 1, n1, 1), lambda b, j: (0, j, 0, 0))],
        out_specs=pl.BlockSpec((1, 2, n1, cw), lambda b, j: (b, 0, 0, j)),
        out_shape=jax.ShapeDtypeStruct((bsz, 2, n1, n2 * cw), F32),
        compiler_params=_cp(("parallel", "parallel")),
        name="hyena_dft1",
    )(f1, xv, consts["tw_s2"])


def _dftmid_body(a_ref, m2_ref, m2i_ref, h_ref, tw_ref, is_ref, o_ref, *, n2, filt):
    a = a_ref[0, :, 0].reshape(2 * n2, a_ref.shape[-1])
    x = _mm_x3(m2_ref[...], a)
    xr, xi = x[:n2], x[n2:]
    if filt:
        o_ref[0, 0, 0] = xr * is_ref[...]
        o_ref[0, 1, 0] = xi * is_ref[...]
        return
    hr, hi = h_ref[0, 0, 0], h_ref[0, 1, 0]
    y = jnp.concatenate([xr * hr - xi * hi, xr * hi + xi * hr], axis=0)
    z = _mm_x3(m2i_ref[...], y)
    zr, zi = z[:n2], z[n2:]
    twr, twi = tw_ref[0, 0], tw_ref[1, 0]
    o_ref[0, 0, 0] = zr * twr + zi * twi
    o_ref[0, 1, 0] = zi * twr - zr * twi


def _dft_mid(a, h, inv_s, consts, filt):
    bsz = a.shape[0]
    n1, n2 = consts["n1"], consts["n2"]
    cw = a.shape[-1] // n2
    av = a.reshape(bsz, 2, n1, n2, cw)
    blk = pl.BlockSpec((1, 2, 1, n2, cw), lambda b, f: (b, 0, f, 0, 0))
    hblk = pl.BlockSpec((1, 2, 1, n2, cw), (lambda b, f: (0, 0, 0, 0, 0)) if filt else (lambda b, f: (0, 0, f, 0, 0)))
    return pl.pallas_call(
        functools.partial(_dftmid_body, n2=n2, filt=filt),
        grid=(bsz, n1),
        in_specs=[blk,
                  pl.BlockSpec((2 * n2, 2 * n2), lambda b, f: (0, 0)),
                  pl.BlockSpec((2 * n2, 2 * n2), lambda b, f: (0, 0)),
                  hblk,
                  pl.BlockSpec((2, 1, n2, 1), lambda b, f: (0, f, 0, 0)),
                  pl.BlockSpec((1, cw), lambda b, f: (0, 0))],
        out_specs=blk,
        out_shape=jax.ShapeDtypeStruct((bsz, 2, n1, n2, cw), F32),
        compiler_params=_cp(("parallel", "parallel")),
        name="hyena_dftmid_" + ("filter" if filt else "conv"),
    )(av, consts["m2"], consts["m2inv"], h, consts["tw_f1"], inv_s)


def _dft3_body(f_ref, z_ref, x0_ref, uv_ref, bias_ref, o_ref):
    y = _mm_x3(f_ref[...], z_ref[0])
    o_ref[0] = (x0_ref[0] * (y + uv_ref[0] * bias_ref[...])).astype(o_ref.dtype)


def _dft_stage1_inv(z, x0, uv, bias, consts):
    bsz, n, cw = x0.shape
    n1, n2 = consts["n1"], consts["n2"]
    hr = n1 // 2
    zv = z.reshape(bsz, 2 * n1, n2 * cw)
    tokv = lambda t: t.reshape(bsz, hr, n2 * cw)
    tile = pl.BlockSpec((1, hr, cw), lambda b, j: (b, 0, j))
    out = pl.pallas_call(
        _dft3_body,
        grid=(bsz, n2),
        in_specs=[pl.BlockSpec((hr, 2 * n1), lambda b, j: (0, 0)),
                  pl.BlockSpec((1, 2 * n1, cw), lambda b, j: (b, 0, j)),
                  tile, tile,
                  pl.BlockSpec((1, cw), lambda b, j: (0, 0))],
        out_specs=tile,
        out_shape=jax.ShapeDtypeStruct((bsz, hr, n2 * cw), BF16),
        compiler_params=_cp(("parallel", "parallel")),
        name="hyena_dft3",
    )(consts["f1inv"][:hr], zv, tokv(x0), tokv(uv), bias.reshape(1, cw).astype(F32))
    return out.reshape(bsz, n, cw)


def _hyena(p_lat, p):
    n = p_lat.shape[1]
    consts = _dft_consts(n)
    n1 = consts["n1"]
    x0, uv = _hyena_pre(p_lat, p["conv_w"], p["conv_b"])
    taps, sabs = _hyena_kernel_taps(n, p)
    cw = taps.shape[1]
    ones = jnp.ones((1, cw), F32)
    hk = _dft_stage1(taps[None], consts, n1)
    hspec = _dft_mid(hk, jnp.zeros((1, 2, 1, HY_N2, cw), F32), 1.0 / sabs, consts, True)
    a = _dft_stage1(uv, consts, n1 // 2)
    z = _dft_mid(a, hspec, ones, consts, False)
    return _dft_stage1_inv(z, x0, uv, p["hy_bias"], consts)


def _mod_vectors(c, c_ctx, p):
    bsz, d = c.shape
    cvec = jnp.concatenate([c, c_ctx[None], jnp.zeros((8 - bsz - 1, d), c.dtype)], axis=0).astype(F32)
    m = _mods(cvec, p["w_mod"].astype(F32), p["b_mod"].astype(F32))
    lat = [v[:, None, :] for v in jnp.split(m[:bsz], N_MOD, axis=-1)]
    cx = [jnp.broadcast_to(v[:, None, :], (bsz, 1, d)) for v in jnp.split(m[bsz:bsz + 1], N_MOD, axis=-1)]
    return lat, cx


def _affine(norm_g, shift, scale):
    return norm_g.astype(F32) * (1.0 + scale), shift


def kernel(x, c, ctx, c_ctx,
           l0_w_mod, l0_b_mod, l0_norm1_g, l0_norm2_g, l0_w_in, l0_lam_q1, l0_lam_k1, l0_lam_q2, l0_lam_k2,
           l0_subln_g, l0_mu_prev, l0_mu_next, l0_w0_f, l0_w2_f, l0_a0_f, l0_a2_f, l0_w0_b, l0_w2_b, l0_a0_b,
           l0_a2_b, l0_g2, l0_k_k, l0_k_a, l0_r_k, l0_lnx_g, l0_lnx_b, l0_w_out, l0_mlp_w1, l0_mlp_w2,
           l1_w_mod, l1_b_mod, l1_norm1_g, l1_norm2_g, l1_w_in, l1_conv_w, l1_conv_b, l1_f_w1, l1_f_b1,
           l1_f_w2, l1_f_b2, l1_f_w3, l1_f_b3, l1_f_w4, l1_f_freq, l1_hy_bias, l1_q_norm_g, l1_k_norm_g,
           l1_w_out, l1_mlp_w1, l1_mlp_w2, final_g):
    bsz, n, d = x.shape
    x = x.astype(F32)
    ctx_s = ctx.astype(F32)
    bf = lambda w: w.astype(BF16)

    p0 = dict(w_mod=l0_w_mod, b_mod=l0_b_mod, mu_prev=l0_mu_prev, mu_next=l0_mu_next, w0_f=l0_w0_f, w2_f=l0_w2_f,
              a0_f=l0_a0_f, a2_f=l0_a2_f, w0_b=l0_w0_b, w2_b=l0_w2_b, a0_b=l0_a0_b, a2_b=l0_a2_b, g2=l0_g2,
              k_k=l0_k_k, k_a=l0_k_a, r_k=l0_r_k, lnx_g=l0_lnx_g, lnx_b=l0_lnx_b)
    (sh1, sc1, g1, sh2, sc2, g2), (csh1, csc1, cg1, csh2, csc2, cg2) = _mod_vectors(c, c_ctx, p0)
    aw = 3 * (d // 2)
    w_attn = bf(l0_w_in[:, :aw])
    w_rw = l0_w_in[:, aw:]
    rw_cols = 3 * l0_w0_f.shape[0] + 3 * LANE
    w_rw = bf(jnp.concatenate([w_rw, jnp.zeros((d, rw_cols - w_rw.shape[1]), w_rw.dtype)], axis=1))
    a1, b1 = _affine(l0_norm1_g, sh1, sc1)
    ca1, cb1 = _affine(l0_norm1_g, csh1, csc1)
    pa_lat, pr_lat = _inproj(x, a1, b1, w_attn), _inproj(x, a1, b1, w_rw)
    pa_ctx, pr_ctx = _inproj(ctx_s, ca1, cb1, w_attn), _inproj(ctx_s, ca1, cb1, w_rw)
    lam_init = 0.8 - 0.6 * math.exp(-0.3 * 0)
    lam = (jnp.exp(jnp.sum(l0_lam_q1 * l0_lam_k1).astype(F32))
           - jnp.exp(jnp.sum(l0_lam_q2 * l0_lam_k2).astype(F32)) + lam_init)
    oa_lat, oa_ctx = _diff_attention(pa_lat, pa_ctx, lam, lam_init, l0_subln_g)
    nh, hd = l0_r_k.shape
    zero = jnp.zeros((bsz, nh, hd, hd), F32)
    y_ctx, states_c = _rwkv_run(_rwkv_prepare(pr_ctx, p0), (zero, zero), p0)
    y_lat, _ = _rwkv_run(_rwkv_prepare(pr_lat, p0), states_c, p0)
    w_out0 = bf(l0_w_out)
    x = _outproj(oa_lat, y_lat, w_out0, x, g1)
    ctx_s = _outproj(oa_ctx, y_ctx, w_out0, ctx_s, cg1)
    w1, w2 = bf(l0_mlp_w1), bf(l0_mlp_w2)
    a2, b2 = _affine(l0_norm2_g, sh2, sc2)
    ca2, cb2 = _affine(l0_norm2_g, csh2, csc2)
    x = _mlp(x, a2, b2, g2, w1, w2)
    ctx_s = _mlp(ctx_s, ca2, cb2, cg2, w1, w2)

    p1 = dict(w_mod=l1_w_mod, b_mod=l1_b_mod, conv_w=l1_conv_w, conv_b=l1_conv_b, f_w1=l1_f_w1, f_b1=l1_f_b1,
              f_w2=l1_f_w2, f_b2=l1_f_b2, f_w3=l1_f_w3, f_b3=l1_f_b3, f_w4=l1_f_w4, f_freq=l1_f_freq,
              hy_bias=l1_hy_bias)
    (sh1, sc1, g1, sh2, sc2, g2), (csh1, csc1, _, _, _, _) = _mod_vectors(c, c_ctx, p1)
    hy_cols = l1_conv_w.shape[1]
    kv_cols = 2 * (2 * D_HEAD)
    w_in1 = bf(l1_w_in)
    a1, b1 = _affine(l1_norm1_g, sh1, sc1)
    ca1, cb1 = _affine(l1_norm1_g, csh1, csc1)
    p_lat = _inproj(x, a1, b1, w_in1)
    pkv_ctx = _inproj(ctx_s, ca1, cb1, w_in1[:, -kv_cols:])
    o_hy = _hyena(p_lat, p1)
    o_at = _gqa_attention(p_lat, pkv_ctx, hy_cols, l1_q_norm_g, l1_k_norm_g)
    x = _outproj(o_hy, o_at, bf(l1_w_out), x, g1)
    a2, b2 = _affine(l1_norm2_g, sh2, sc2)
    return _mlp(x, a2, b2, g2, bf(l1_mlp_w1), bf(l1_mlp_w2), final_g=final_g)
```

```python
import functools
import math

import numpy as np
import jax
import jax.numpy as jnp
from jax import lax
from jax.experimental import pallas as pl
from jax.experimental.pallas import tpu as pltpu

F32 = jnp.float32
BF16 = jnp.bfloat16
HI = lax.Precision.HIGHEST

NORM_EPS = 1e-6
ROPE_THETA = 10000.0
GRID_W = 64
N_MOD = 6
A_QK_DIM = 64
A_V_DIM = 128
B_HEAD = 64
B_DECAY_RANK = 64
B_ICL_RANK = 64
B_GATE_RANK = 160
B_GN_EPS = 64e-5
HY_EMB = 33
HY_ORDER = 64
HY_TARGET = 1e-2
HY_FAST_PCT = 0.3
HY_SLOW_PCT = 1.5
D_HEAD = 128
D_KV_GROUP = 4

LANE = 128
VMEM_LIMIT = 56 * 1024 * 1024
LOG2E = 1.4426950408889634


def _cp(sem, vmem=VMEM_LIMIT):
    return pltpu.CompilerParams(dimension_semantics=sem, vmem_limit_bytes=vmem)


def _pick(n, pref, step=8):
    t = max(step, min(n, pref) // step * step)
    while n % t:
        t -= step
    return t


def _mods_body(c_ref, w_ref, b_ref, o_ref):
    c = c_ref[...]
    s = c * jax.nn.sigmoid(c)
    o_ref[...] = jnp.dot(s, w_ref[...], preferred_element_type=F32, precision=HI) + b_ref[...]


def _mods(cvec, w_mod, b_mod):
    m, d = cvec.shape
    n = w_mod.shape[1]
    tn = _pick(n, 1024, LANE)
    return pl.pallas_call(
        _mods_body,
        grid=(n // tn,),
        in_specs=[pl.BlockSpec((m, d), lambda j: (0, 0)),
                  pl.BlockSpec((d, tn), lambda j: (0, j)),
                  pl.BlockSpec((1, tn), lambda j: (0, j))],
        out_specs=pl.BlockSpec((m, tn), lambda j: (0, j)),
        out_shape=jax.ShapeDtypeStruct((m, n), F32),
        compiler_params=_cp(("arbitrary",)),
        name="mods",
    )(cvec, w_mod, b_mod.reshape(1, n))


def _norm_mod(x, a, b):
    ms = jnp.mean(x * x, axis=-1, keepdims=True)
    return x * lax.rsqrt(ms + NORM_EPS) * a + b


def _inproj_body(x_ref, a_ref, b_ref, w_ref, o_ref, xn_ref):
    @pl.when(pl.program_id(2) == 0)
    def _():
        xn_ref[...] = _norm_mod(x_ref[0], a_ref[0], b_ref[0]).astype(BF16)

    o_ref[0] = jnp.dot(xn_ref[...], w_ref[...], preferred_element_type=F32).astype(o_ref.dtype)


def _inproj(x, a, b, w, out_dtype=BF16, tm_pref=1024, tn_pref=512):
    bsz, n, d = x.shape
    nn = w.shape[1]
    tm, tn = _pick(n, tm_pref), _pick(nn, tn_pref, LANE)
    return pl.pallas_call(
        _inproj_body,
        grid=(bsz, n // tm, nn // tn),
        in_specs=[pl.BlockSpec((1, tm, d), lambda bi, i, j: (bi, i, 0)),
                  pl.BlockSpec((1, 1, d), lambda bi, i, j: (bi, 0, 0)),
                  pl.BlockSpec((1, 1, d), lambda bi, i, j: (bi, 0, 0)),
                  pl.BlockSpec((d, tn), lambda bi, i, j: (0, j))],
        out_specs=pl.BlockSpec((1, tm, tn), lambda bi, i, j: (bi, i, j)),
        out_shape=jax.ShapeDtypeStruct((bsz, n, nn), out_dtype),
        scratch_shapes=[pltpu.VMEM((tm, d), BF16)],
        compiler_params=_cp(("parallel", "parallel", "arbitrary")),
        name="inproj",
    )(x, a, b, w)


def _outproj_body(oa_ref, ob_ref, wa_ref, wb_ref, x_ref, g_ref, y_ref):
    acc = jnp.dot(oa_ref[0], wa_ref[...], preferred_element_type=F32)
    acc += jnp.dot(ob_ref[0], wb_ref[...], preferred_element_type=F32)
    y_ref[0] = x_ref[0] + g_ref[0] * acc


def _outproj(oa, ob, w, x, g, tm_pref=1024, tn_pref=512):
    bsz, n, ka = oa.shape
    kb = ob.shape[2]
    d = w.shape[1]
    tm, tn = _pick(n, tm_pref), _pick(d, tn_pref, LANE)
    return pl.pallas_call(
        _outproj_body,
        grid=(bsz, n // tm, d // tn),
        in_specs=[pl.BlockSpec((1, tm, ka), lambda bi, i, j: (bi, i, 0)),
                  pl.BlockSpec((1, tm, kb), lambda bi, i, j: (bi, i, 0)),
                  pl.BlockSpec((ka, tn), lambda bi, i, j: (0, j)),
                  pl.BlockSpec((kb, tn), lambda bi, i, j: (0, j)),
                  pl.BlockSpec((1, tm, tn), lambda bi, i, j: (bi, i, j)),
                  pl.BlockSpec((1, 1, tn), lambda bi, i, j: (bi, 0, j))],
        out_specs=pl.BlockSpec((1, tm, tn), lambda bi, i, j: (bi, i, j)),
        out_shape=jax.ShapeDtypeStruct((bsz, n, d), F32),
        compiler_params=_cp(("parallel", "parallel", "parallel")),
        name="outproj",
    )(oa, ob, w[:ka], w[ka:], x, g)


def _mlp_body(x_ref, a_ref, b_ref, g_ref, w1_ref, w2_ref, fg_ref, y_ref, xn_ref, acc_ref, *, nf, final):
    f = pl.program_id(2)

    @pl.when(f == 0)
    def _():
        xn_ref[...] = _norm_mod(x_ref[0], a_ref[0], b_ref[0]).astype(BF16)
        acc_ref[...] = jnp.zeros_like(acc_ref)

    h = jnp.dot(xn_ref[...], w1_ref[...], preferred_element_type=F32)
    h = jnp.square(jnp.maximum(h, 0.0)).astype(BF16)
    acc_ref[...] += jnp.dot(h, w2_ref[...], preferred_element_type=F32)

    @pl.when(f == nf - 1)
    def _():
        y = x_ref[0] + g_ref[0] * acc_ref[...]
        if final:
            ms = jnp.mean(y * y, axis=-1, keepdims=True)
            y = y * lax.rsqrt(ms + NORM_EPS) * fg_ref[...]
        y_ref[0] = y


def _mlp(x, a, b, g, w1, w2, final_g=None, tm_pref=512, tf_pref=512):
    bsz, n, d = x.shape
    dff = w1.shape[1]
    tm, tf = _pick(n, tm_pref), _pick(dff, tf_pref, LANE)
    nf = dff // tf
    final = final_g is not None
    fg = (final_g if final else jnp.ones((d,), F32)).reshape(1, d).astype(F32)
    return pl.pallas_call(
        functools.partial(_mlp_body, nf=nf, final=final),
        grid=(bsz, n // tm, nf),
        in_specs=[pl.BlockSpec((1, tm, d), lambda bi, i, f: (bi, i, 0)),
                  pl.BlockSpec((1, 1, d), lambda bi, i, f: (bi, 0, 0)),
                  pl.BlockSpec((1, 1, d), lambda bi, i, f: (bi, 0, 0)),
                  pl.BlockSpec((1, 1, d), lambda bi, i, f: (bi, 0, 0)),
                  pl.BlockSpec((d, tf), lambda bi, i, f: (0, f)),
                  pl.BlockSpec((tf, d), lambda bi, i, f: (f, 0)),
                  pl.BlockSpec((1, d), lambda bi, i, f: (0, 0))],
        out_specs=pl.BlockSpec((1, tm, d), lambda bi, i, f: (bi, i, 0)),
        out_shape=jax.ShapeDtypeStruct((bsz, n, d), F32),
        scratch_shapes=[pltpu.VMEM((tm, d), BF16), pltpu.VMEM((tm, d), F32)],
        compiler_params=_cp(("parallel", "parallel", "arbitrary")),
        name="mlp",
    )(x, a, b, g, w1, w2, fg)


def _rope_tables(n, dim, reps):
    rows = n // GRID_W
    row = jnp.repeat(jnp.arange(rows, dtype=F32), GRID_W)
    col = jnp.tile(jnp.arange(GRID_W, dtype=F32), rows)
    half = dim // 2
    inv = ROPE_THETA ** (-jnp.arange(0, half, 2, dtype=F32) / half)
    ar, ac = row[:, None] * inv, col[:, None] * inv
    cos = jnp.concatenate([jnp.cos(ar), jnp.cos(ar), jnp.cos(ac), jnp.cos(ac)], axis=-1)
    sin = jnp.concatenate([-jnp.sin(ar), jnp.sin(ar), -jnp.sin(ac), jnp.sin(ac)], axis=-1)
    return jnp.tile(cos, (1, reps)), jnp.tile(sin, (1, reps))


def _swap_matrix(dim, reps):
    q = dim // 4
    width = dim * reps
    p = np.zeros((width, width), np.float32)
    for j in range(width):
        base, r = (j // dim) * dim, j % dim
        axis, which, f = r // (2 * q), (r % (2 * q)) // q, r % q
        p[base + axis * 2 * q + (1 - which) * q + f, j] = 1.0
    return p


def _rope_norm(x, cos_ref, sin_ref, p_ref, g_ref, gs_ref, *, norm, rope, scale=1.0):
    y = x.astype(F32)
    if rope:
        ys = jnp.dot(x, p_ref[...], preferred_element_type=F32)
    if norm:
        rs = lax.rsqrt(jnp.mean(y * y, axis=-1, keepdims=True) + NORM_EPS)
        y = y * rs * g_ref[...]
        if rope:
            ys = ys * rs * gs_ref[...]
    if rope:
        y = y * cos_ref[...] + ys * sin_ref[...]
    return y * scale if scale != 1.0 else y


def _rope_operands(n, dim, gain, rope):
    reps = LANE // dim
    if rope:
        cos, sin = _rope_tables(n, dim, reps)
        pm = jnp.asarray(_swap_matrix(dim, reps), BF16)
    else:
        cos = sin = jnp.zeros((n, LANE), F32)
        pm = jnp.zeros((LANE, LANE), BF16)
    if gain is not None:
        g = jnp.tile(gain.astype(F32), reps).reshape(1, LANE)
        gs = jnp.dot(g, jnp.asarray(_swap_matrix(dim, reps)), precision=HI)
    else:
        g = gs = jnp.ones((1, LANE), F32)
    return cos, sin, pm, g, gs


def _kprep_body(x_ref, cos_ref, sin_ref, p_ref, g_ref, gs_ref, o_ref, *, heads, norm, rope):
    for h in range(heads):
        sl = slice(h * LANE, (h + 1) * LANE)
        y = _rope_norm(x_ref[0, :, sl], cos_ref, sin_ref, p_ref, g_ref, gs_ref, norm=norm, rope=rope)
        o_ref[0, :, sl] = y.astype(o_ref.dtype)


def _kprep(p, col0, heads, *, dim, gain=None, rope=True, tq_pref=512):
    bsz, n, _ = p.shape
    tq = _pick(n, tq_pref)
    w = heads * LANE
    cb = col0 // w
    cos, sin, pm, g, gs = _rope_operands(n, dim, gain, rope)
    return pl.pallas_call(
        functools.partial(_kprep_body, heads=heads, norm=gain is not None, rope=rope),
        grid=(bsz, n // tq),
        in_specs=[pl.BlockSpec((1, tq, w), lambda b, i: (b, i, cb)),
                  pl.BlockSpec((tq, LANE), lambda b, i: (i, 0)),
                  pl.BlockSpec((tq, LANE), lambda b, i: (i, 0)),
                  pl.BlockSpec((LANE, LANE), lambda b, i: (0, 0)),
                  pl.BlockSpec((1, LANE), lambda b, i: (0, 0)),
                  pl.BlockSpec((1, LANE), lambda b, i: (0, 0))],
        out_specs=pl.BlockSpec((1, tq, w), lambda b, i: (b, i, 0)),
        out_shape=jax.ShapeDtypeStruct((bsz, n, w), BF16),
        compiler_params=_cp(("parallel", "parallel")),
        name="kprep",
    )(p, cos, sin, pm, g, gs)


FLASH_KEYS = 1280
FLASH_HEADROOM = 64.0


def _knorm_body(k_ref, o_ref):
    k = k_ref[0].astype(F32)
    nrm2 = jnp.max(jnp.sum(k * k, axis=1, keepdims=True), axis=0, keepdims=True)
    o_ref[0, 0] = jnp.broadcast_to(jnp.sqrt(nrm2), (8, LANE))


def _key_block_norms(k, kcol0, heads, ts):
    bsz, nk, _ = k.shape
    nsub = nk // ts
    kb = kcol0 // LANE
    out = pl.pallas_call(
        _knorm_body,
        grid=(bsz, heads, nsub),
        in_specs=[pl.BlockSpec((1, ts, LANE), lambda b, h, c: (b, c, kb + h))],
        out_specs=pl.BlockSpec((1, 1, 8, LANE), lambda b, h, c: (b, h, c, 0)),
        out_shape=jax.ShapeDtypeStruct((bsz, heads, nsub * 8, LANE), F32),
        compiler_params=_cp(("parallel", "parallel", "parallel")),
        name="key_block_norms",
    )(k)
    return out[:, :, ::8, 0].reshape(-1)


def _flash_body(kmax_ref, qr_ref, cos_ref, sin_ref, pm_ref, g_ref, gs_ref, k_ref, v_ref, e1_ref, e2_ref, o_ref,
                q_ref, m_ref, l_ref, acc_ref, thr_ref, *, ts, nsub, mode, tq, norm, rope, scale):
    base = (pl.program_id(0) * pl.num_programs(1) + pl.program_id(1)) * nsub
    prep = functools.partial(_rope_norm, cos_ref=cos_ref, sin_ref=sin_ref, p_ref=pm_ref, g_ref=g_ref, gs_ref=gs_ref,
                             norm=norm, rope=rope, scale=scale)
    if mode == "diff":
        y = prep(qr_ref[0])
        lane = lax.broadcasted_iota(jnp.int32, y.shape, 1)
        q_ref[:tq] = jnp.where(lane < A_QK_DIM, y, 0.0).astype(BF16)
        q_ref[tq:] = jnp.where(lane >= A_QK_DIM, y, 0.0).astype(BF16)
    else:
        for j in range(D_KV_GROUP):
            q_ref[j * tq:(j + 1) * tq] = prep(qr_ref[0, :, j * LANE:(j + 1) * LANE]).astype(BF16)
    q = q_ref[...]

    def scores(off, size):
        k = k_ref[0, pl.ds(off, size), :]
        return lax.dot_general(q, k, (((1,), (1,)), ((), ())), preferred_element_type=F32)

    def block(c):
        off = pl.multiple_of(c * ts, ts)
        return scores(off, ts), v_ref[0, pl.ds(off, ts), :]

    def wide(m):
        return jnp.concatenate([m] * (ts // LANE), axis=1)

    m0 = jnp.broadcast_to(jnp.max(scores(0, 2 * LANE), axis=1, keepdims=True), m_ref.shape)
    m_ref[...] = m0
    l_ref[...] = jnp.zeros_like(l_ref)
    acc_ref[...] = jnp.zeros_like(acc_ref)
    qf = q.astype(F32)
    qn = jnp.sqrt(jnp.sum(qf * qf, axis=1, keepdims=True))
    thr_ref[0] = jnp.min((m0[:, :1] + FLASH_HEADROOM) / qn)

    def step(c, carry):
        fixed = kmax_ref[base + c] * 1.001 <= thr_ref[0]

        @pl.when(fixed)
        def _():
            s, v = block(c)
            p = jnp.exp2(s - wide(m_ref[...]))
            l_ref[...] += jnp.sum(p, axis=1, keepdims=True)
            acc_ref[...] += jnp.dot(p.astype(BF16), v, preferred_element_type=F32)

        @pl.when(jnp.logical_not(fixed))
        def _():
            s, v = block(c)
            m_prev = m_ref[...]
            m_new = jnp.maximum(m_prev, jnp.max(s, axis=1, keepdims=True))
            alpha = jnp.exp2(m_prev - m_new)
            p = jnp.exp2(s - wide(m_new))
            l_ref[...] = alpha * l_ref[...] + jnp.sum(p, axis=1, keepdims=True)
            acc_ref[...] = alpha * acc_ref[...] + jnp.dot(p.astype(BF16), v, preferred_element_type=F32)
            m_ref[...] = m_new

        return carry

    lax.fori_loop(0, nsub, step, 0)

    o = acc_ref[...] / l_ref[...]
    if mode == "diff":
        d = o[:tq] - e1_ref[...] * o[tq:]
        ms = jnp.mean(d * d, axis=-1, keepdims=True)
        o_ref[0] = (d * lax.rsqrt(ms + NORM_EPS) * e2_ref[...]).astype(o_ref.dtype)
    else:
        for j in range(D_KV_GROUP):
            o_ref[0, :, j * LANE:(j + 1) * LANE] = o[j * tq:(j + 1) * tq].astype(o_ref.dtype)


def _flash(pq, qcol0, k, v, kcol0, vcol0, e1, e2, *, mode, hk, dim, tq, gain=None, rope=True):
    bsz, n, _ = pq.shape
    nk = k.shape[1]
    g = 2 if mode == "diff" else D_KV_GROUP
    mq = g * tq
    qw = LANE if mode == "diff" else D_KV_GROUP * LANE
    qb = qcol0 // qw
    ts = _pick(nk, FLASH_KEYS, 2 * LANE)
    kb, vb = kcol0 // LANE, vcol0 // LANE
    ow = LANE if mode == "diff" else D_KV_GROUP * LANE
    kmax = _key_block_norms(k, kcol0, hk, ts)
    cos, sin, pm, gq, gqs = _rope_operands(n, dim, gain, rope)
    const = lambda shape: pl.BlockSpec(shape, lambda b, h, i: (0, 0))
    return pl.pallas_call(
        functools.partial(_flash_body, ts=ts, nsub=nk // ts, mode=mode, tq=tq, norm=gain is not None, rope=rope,
                          scale=dim ** -0.5 * LOG2E),
        grid=(bsz, hk, n // tq),
        in_specs=[pl.BlockSpec(memory_space=pltpu.SMEM),
                  pl.BlockSpec((1, tq, qw), lambda b, h, i: (b, i, qb + h)),
                  pl.BlockSpec((tq, LANE), lambda b, h, i: (i, 0)),
                  pl.BlockSpec((tq, LANE), lambda b, h, i: (i, 0)),
                  const((LANE, LANE)), const((1, LANE)), const((1, LANE)),
                  pl.BlockSpec((1, nk, LANE), lambda b, h, i: (b, 0, kb + h)),
                  pl.BlockSpec((1, nk, LANE), lambda b, h, i: (b, 0, vb + h)),
                  const((1, LANE)), const((1, LANE))],
        out_specs=pl.BlockSpec((1, tq, ow), lambda b, h, i: (b, i, h)),
        out_shape=jax.ShapeDtypeStruct((bsz, n, hk * ow), BF16),
        scratch_shapes=[pltpu.VMEM((mq, LANE), BF16), pltpu.VMEM((mq, LANE), F32), pltpu.VMEM((mq, LANE), F32),
                        pltpu.VMEM((mq, LANE), F32), pltpu.SMEM((1,), F32)],
        compiler_params=_cp(("parallel", "parallel", "arbitrary")),
        name="flash_" + mode,
    )(kmax, pq, cos, sin, pm, gq, gqs, k, v, e1, e2)


def _diff_attention(pa_lat, pa_ctx, lam, lam_init, subln_g):
    n, nctx = pa_lat.shape[1], pa_ctx.shape[1]
    aw = pa_lat.shape[2] // 3
    heads = aw // LANE
    k_l = _kprep(pa_lat, aw, heads, dim=A_QK_DIM)
    k_all = jnp.concatenate([k_l, pa_ctx[..., aw:2 * aw]], axis=1)
    v_all = jnp.concatenate([pa_lat[..., 2 * aw:], pa_ctx[..., 2 * aw:]], axis=1)
    e1 = jnp.full((1, LANE), lam, F32)
    e2 = (subln_g.astype(F32) * (1.0 - lam_init)).reshape(1, LANE)
    o_lat = _flash(pa_lat, 0, k_all, v_all, 0, 0, e1, e2, mode="diff", hk=heads, dim=A_QK_DIM, tq=_pick(n, 512))
    o_ctx = _flash(pa_ctx, 0, pa_ctx, pa_ctx, aw, 2 * aw, e1, e2, mode="diff", hk=heads, dim=A_QK_DIM,
                   tq=_pick(nctx, 512), rope=False)
    return o_lat, o_ctx


def _gqa_attention(p_lat, pkv_ctx, qcol0, q_norm_g, k_norm_g):
    n = p_lat.shape[1]
    hq, hk = 2 * D_KV_GROUP, 2
    kcol0 = qcol0 + hq * D_HEAD
    vcol0 = kcol0 + hk * D_HEAD
    k_l = _kprep(p_lat, kcol0, hk, dim=D_HEAD, gain=k_norm_g)
    k_c = _kprep(pkv_ctx, 0, hk, dim=D_HEAD, gain=k_norm_g, rope=False)
    k_all = jnp.concatenate([k_l, k_c], axis=1)
    v_all = jnp.concatenate([p_lat[..., vcol0:vcol0 + hk * D_HEAD], pkv_ctx[..., hk * D_HEAD:]], axis=1)
    dummy = jnp.zeros((1, LANE), F32)
    return _flash(p_lat, qcol0, k_all, v_all, 0, 0, dummy, dummy, mode="gqa", hk=hk, dim=D_HEAD,
                  tq=_pick(n, 256), gain=q_norm_g)


RW_CHUNK = 64
RW_HALO = 8


def _mm_hi(a, b):
    return jnp.dot(a, b, preferred_element_type=F32, precision=HI)


def _mm_bf(a, b):
    return jnp.dot(a.astype(BF16), b.astype(BF16), preferred_element_type=F32)


def _split3(x):
    x1 = x.astype(BF16)
    r1 = x - x1.astype(F32)
    x2 = r1.astype(BF16)
    return x1, x2, (r1 - x2.astype(F32)).astype(BF16)


def _mm_x3(a, b):
    ah, al, _ = _split3(a)
    bh, bl, _ = _split3(b)
    d = lambda u, w: jnp.dot(u, w, preferred_element_type=F32)
    return d(ah, bh) + (d(ah, bl) + d(al, bh))


def _mm_sel(z, sel):
    z1, z2, z3 = _split3(z)
    d = lambda u: jnp.dot(u, sel, preferred_element_type=F32)
    return d(z1) + (d(z2) + d(z3))


def _rwprep_body(x_ref, hp_ref, hn_ref, mup_ref, mun_ref, vec_ref, w2f_ref, w2b_ref, a2f_ref, a2b_ref,
                 g2_ref, sel_ref, selt_ref,
                 r_ref, v_ref, kk_ref, lwf_ref, bf_ref, kf_ref, lwb_ref, bb_ref, kb_ref, g_ref, bonus_ref,
                 *, nb, tm, bw):
    i = pl.program_id(1)
    x = x_ref[0].astype(F32)
    row = lax.broadcasted_iota(jnp.int32, x.shape, 0)
    prev_edge = jnp.where(i > 0, hp_ref[0, RW_HALO - 1:RW_HALO, :].astype(F32), 0.0)
    next_edge = jnp.where(i < nb - 1, hn_ref[0, 0:1, :].astype(F32), 0.0)
    prev = jnp.where(row == 0, prev_edge, pltpu.roll(x, 1, 0))
    nxt = jnp.where(row == tm - 1, next_edge, pltpu.roll(x, tm - 1, 0))
    xs = x + mup_ref[...] * (prev - x) + mun_ref[...] * (nxt - x)
    r, k, v = xs[:, :bw], xs[:, bw:2 * bw], xs[:, 2 * bw:3 * bw]
    lr = xs[:, 3 * bw:3 * bw + LANE]
    gd = xs[:, 3 * bw + LANE:3 * bw + 3 * LANE]
    k_k, k_a, r_k = vec_ref[0:1], vec_ref[1:2], vec_ref[2:3]
    w0f, a0f, w0b, a0b = vec_ref[3:4], vec_ref[4:5], vec_ref[5:6], vec_ref[6:7]
    sel, selt = sel_ref[...], selt_ref[...]

    def head_sum(z):
        return _mm_sel(_mm_sel(z, sel), selt)

    kk = k * k_k
    kk = kk * lax.rsqrt(head_sum(kk * kk) + 1e-12)
    g_ref[0] = _mm_bf(jax.nn.sigmoid(gd), g2_ref[...]).astype(g_ref.dtype)
    th = jnp.tanh(lr)
    ksum = jnp.zeros_like(k)
    for (w0, a0, w2_ref, a2_ref, lw_ref, b_ref, kd_ref) in (
            (w0f, a0f, w2f_ref, a2f_ref, lwf_ref, bf_ref, kf_ref),
            (w0b, a0b, w2b_ref, a2b_ref, lwb_ref, bb_ref, kb_ref)):
        w_log = -jax.nn.softplus(-(w0 + _mm_bf(th, w2_ref[...]))) - 0.5
        logw = -jnp.exp(w_log)
        a = jax.nn.sigmoid(a0 + _mm_bf(lr, a2_ref[...]))
        k_d = k * (1.0 + (a - 1.0) * k_a)
        ksum = ksum + k_d
        lw_ref[0] = logw
        b_ref[0] = kk * a
        kd_ref[0] = k_d
    r_ref[0] = r
    v_ref[0] = v
    kk_ref[0] = kk
    bonus_ref[0] = head_sum(r * ksum * r_k) * v


def _rwkv_prepare(pr, p):
    bsz, n, cols = pr.shape
    bw = p["w0_f"].shape[0]
    nh = bw // B_HEAD
    tm = _pick(n, 256)
    nb = n // tm
    hb = tm // RW_HALO
    pad = cols - 3 * bw

    def padded(vec):
        return jnp.concatenate([vec.astype(F32), jnp.zeros((cols - vec.shape[0],), F32)]).reshape(1, cols)

    vecs = jnp.stack([p["k_k"], p["k_a"], p["r_k"].reshape(-1), p["w0_f"], p["a0_f"], p["w0_b"], p["a0_b"],
                      jnp.zeros((bw,), F32)]).astype(F32)
    zr = jnp.zeros((B_DECAY_RANK, bw), F32)
    w2 = {d: jnp.concatenate([p["w2_" + d].astype(F32), zr], axis=0) for d in "fb"}
    a2 = {d: jnp.concatenate([zr, p["a2_" + d].astype(F32)], axis=0) for d in "fb"}
    g2 = jnp.concatenate([p["g2"].astype(F32), jnp.zeros((2 * LANE - B_GATE_RANK, bw), F32)], axis=0)
    sel_np = np.zeros((bw, LANE), np.float32)
    sel_np[np.arange(bw), np.arange(bw) // B_HEAD] = 1.0
    sel, selt = jnp.asarray(sel_np, BF16), jnp.asarray(sel_np.T, BF16)
    w2 = {d: w2[d].astype(BF16) for d in "fb"}
    a2 = {d: a2[d].astype(BF16) for d in "fb"}
    g2 = g2.astype(BF16)
    assert pad == 3 * LANE and B_DECAY_RANK + B_ICL_RANK == LANE

    full = lambda shape: pl.BlockSpec(shape, lambda b, i: tuple(0 for _ in shape))
    tok = pl.BlockSpec((1, tm, bw), lambda b, i: (b, i, 0))
    hm = tok
    hm_shape = jax.ShapeDtypeStruct((bsz, n, bw), F32)
    outs = pl.pallas_call(
        functools.partial(_rwprep_body, nb=nb, tm=tm, bw=bw),
        grid=(bsz, nb),
        in_specs=[pl.BlockSpec((1, tm, cols), lambda b, i: (b, i, 0)),
                  pl.BlockSpec((1, RW_HALO, cols), lambda b, i: (b, jnp.maximum(i * hb - 1, 0), 0)),
                  pl.BlockSpec((1, RW_HALO, cols), lambda b, i: (b, jnp.minimum((i + 1) * hb, nb * hb - 1), 0)),
                  full((1, cols)), full((1, cols)), full((8, bw)),
                  full((LANE, bw)), full((LANE, bw)), full((LANE, bw)), full((LANE, bw)),
                  full((2 * LANE, bw)), full((bw, LANE)), full((LANE, bw))],
        out_specs=[hm] * 9 + [tok, tok],
        out_shape=[hm_shape] * 9 + [jax.ShapeDtypeStruct((bsz, n, bw), BF16),
                                    jax.ShapeDtypeStruct((bsz, n, bw), F32)],
        compiler_params=_cp(("parallel", "parallel")),
        name="rwkv_prepare",
    )(pr, pr, pr, padded(p["mu_prev"]), padded(p["mu_next"]), vecs, w2["f"], w2["b"], a2["f"], a2["b"],
      g2, sel, selt)
    r, v, kk, lwf, bf, kf, lwb, bb, kb, g, bonus = outs
    return dict(r=r, v=v, kk=kk, g=g, bonus=bonus, f=(lwf, bf, kf), b=(lwb, bb, kb))


RW_GROUP = 4


def _bd(x):
    t = jnp.concatenate([x.astype(BF16)] * RW_GROUP, axis=1)
    r = lax.broadcasted_iota(jnp.int32, t.shape, 1) // B_HEAD
    c = lax.broadcasted_iota(jnp.int32, t.shape, 2) // B_HEAD
    return jnp.where(r == c, t, jnp.zeros_like(t))


def _bdot(a, b, dims):
    return jnp.einsum(dims, a.astype(BF16), b.astype(BF16), preferred_element_type=F32)


def _groups(x):
    gw = RW_GROUP * B_HEAD
    return jnp.stack([x[:, g * gw:(g + 1) * gw] for g in range(x.shape[1] // gw)])


def _rw_decays(r, kk, b, kd, lw, reverse):
    ck = r.shape[0]
    ti = lax.broadcasted_iota(jnp.int32, (ck, ck), 0)
    si = lax.broadcasted_iota(jnp.int32, (ck, ck), 1)
    tri = ((si >= ti) if reverse else (si <= ti)).astype(BF16)
    l1, l2, l3 = _split3(lw)
    tdot = lambda z: jnp.dot(tri, z, preferred_element_type=F32)
    cum = tdot(l1) + (tdot(l2) + tdot(l3))
    total = cum[0:1] if reverse else cum[ck - 1:ck]
    winv = jnp.exp(-cum)
    wrest = jnp.exp(total - cum)
    return (-kk * jnp.exp(cum - lw), b * winv, kd * winv, r * jnp.exp(cum), b * wrest, kd * wrest, jnp.exp(total))


def _rw_chunks(fwd, bwd, s_f, s_b):
    nseq = len(fwd)
    ck, w = fwd[0][0].shape
    gw = RW_GROUP * B_HEAD
    ng = w // gw
    parts = []
    for probs, rev in ((fwd, False), (bwd, True)):
        for (r, v, kk, b, kd, lw) in probs:
            parts.append([_groups(z) for z in _rw_decays(r, kk, b, kd, lw, rev)] + [_groups(v)])
    at, bt, kt, rt, btw, ktw, decay, v = [jnp.concatenate(zs, axis=0) for zs in zip(*parts)]
    s = jnp.concatenate([_groups(z) for z in list(s_f) + list(s_b)], axis=0)
    nslab = 2 * nseq * ng
    shape = (nslab, ck, gw)
    lag = lax.broadcasted_iota(jnp.int32, shape, 1) - lax.broadcasted_iota(jnp.int32, shape, 2) % ck
    lag = jnp.where(lax.broadcasted_iota(jnp.int32, shape, 0) >= nseq * ng, -lag, lag)
    incl, strict = lag >= 0, lag > 0
    eye = (lag == 0).astype(F32)

    bdv = _bd(v)
    lhs = jnp.concatenate([at, rt], axis=1)
    lb, lk = _bdot(lhs, _bd(bt), "gtk,gnk->gtn"), _bdot(lhs, _bd(kt), "gtk,gnk->gtn")
    lab, mrb = jnp.where(strict, lb[:, :ck], 0.0), jnp.where(incl, lb[:, ck:], 0.0)
    lak, mrk = jnp.where(strict, lk[:, :ck], 0.0), jnp.where(incl, lk[:, ck:], 0.0)
    tinv, pw = eye + lab, lab
    span = 2
    while span < ck:
        pw = _bdot(pw, _bd(pw), "gtk,gkn->gtn")
        tinv = tinv + _bdot(tinv, _bd(pw), "gtk,gkn->gtn")
        span *= 2
    p1 = _bdot(tinv, _bd(at), "gtk,gkn->gtn")
    q1 = _bdot(tinv, _bd(_bdot(lak, bdv, "gtk,gkn->gtn")), "gtk,gkn->gtn")
    bds = _bd(s)
    u = _bdot(p1, bds, "gtk,gnk->gtn") + q1
    y = _bdot(rt, bds, "gtk,gnk->gtn") + _bdot(mrb, _bd(u), "gtk,gkn->gtn") + _bdot(mrk, bdv, "gtk,gkn->gtn")
    z = _bdot(jnp.concatenate([u, v], axis=1), jnp.concatenate([btw, ktw], axis=1), "gtm,gtn->gmn")
    lane_head = lax.broadcasted_iota(jnp.int32, (nslab, B_HEAD, gw), 2) // B_HEAD
    upd = jnp.zeros_like(s)
    for h in range(RW_GROUP):
        upd = upd + jnp.where(lane_head == h, z[:, h * B_HEAD:(h + 1) * B_HEAD], 0.0)
    s_new = s * decay + upd
    lanes = lambda x, q: jnp.concatenate([x[q * ng + g] for g in range(ng)], axis=1)
    seqs = range(nseq)
    return ([lanes(y, q) for q in seqs], [lanes(y, nseq + q) for q in seqs],
            [lanes(s_new, q) for q in seqs], [lanes(s_new, nseq + q) for q in seqs])


def _rwscan_body(rf_ref, vf_ref, kkf_ref, bf_ref, kdf_ref, lwf_ref, rb_ref, vb_ref, kkb_ref, bb_ref, kdb_ref, lwb_ref,
                 s0f_ref, s0b_ref, yf_ref, yb_ref, sTf_ref, sTb_ref, sf_ref, sb_ref, *, nc, nseq):
    c = pl.program_id(0)

    @pl.when(c == 0)
    def _():
        sf_ref[...] = s0f_ref[...]
        sb_ref[...] = s0b_ref[...]

    seqs = range(nseq)
    yf, yb, sf, sb = _rw_chunks(
        [(rf_ref[q], vf_ref[q], kkf_ref[q], bf_ref[q], kdf_ref[q], lwf_ref[q]) for q in seqs],
        [(rb_ref[q], vb_ref[q], kkb_ref[q], bb_ref[q], kdb_ref[q], lwb_ref[q]) for q in seqs],
        [sf_ref[q] for q in seqs], [sb_ref[q] for q in seqs])
    for q in seqs:
        yf_ref[q] = yf[q]
        yb_ref[q] = yb[q]
        sf_ref[q] = sf[q]
        sb_ref[q] = sb[q]

    @pl.when(c == nc - 1)
    def _():
        for q in seqs:
            sTf_ref[q] = sf[q]
            sTb_ref[q] = sb[q]


def _rwkv_scan(r, v, kk, fwd, bwd, s0f, s0b):
    bsz, n, w = r.shape
    ck = RW_CHUNK
    assert n % ck == 0 and ck == B_HEAD and w % (RW_GROUP * B_HEAD) == 0
    nc = n // ck
    seq_f = pl.BlockSpec((bsz, ck, w), lambda c: (0, c, 0))
    seq_b = pl.BlockSpec((bsz, ck, w), lambda c: (0, nc - 1 - c, 0))
    st = pl.BlockSpec((bsz, B_HEAD, w), lambda c: (0, 0, 0))
    y_shape = jax.ShapeDtypeStruct((bsz, n, w), F32)
    s_shape = jax.ShapeDtypeStruct((bsz, B_HEAD, w), F32)
    (lwf, bf, kf), (lwb, bb, kb) = fwd, bwd
    return pl.pallas_call(
        functools.partial(_rwscan_body, nc=nc, nseq=bsz),
        grid=(nc,),
        in_specs=[seq_f] * 6 + [seq_b] * 6 + [st, st],
        out_specs=[seq_f, seq_b, st, st],
        out_shape=[y_shape, y_shape, s_shape, s_shape],
        scratch_shapes=[pltpu.VMEM((bsz, B_HEAD, w), F32), pltpu.VMEM((bsz, B_HEAD, w), F32)],
        compiler_params=_cp(("arbitrary",)),
        name="rwkv_scan",
    )(r, v, kk, bf, kf, lwf, r, v, kk, bb, kb, lwb, s0f, s0b)


def _rwpost_body(yf_ref, yb_ref, g_ref, bonus_ref, lg_ref, lb_ref, sel_ref, selt_ref, o_ref):
    sel, selt = sel_ref[...], selt_ref[...]
    head_mean = lambda z: _mm_sel(_mm_sel(z, sel), selt) * (1.0 / B_HEAD)
    y = yf_ref[0] + yb_ref[0]
    d = y - head_mean(y)
    y = d * lax.rsqrt(head_mean(d * d) + B_GN_EPS)
    y = y * lg_ref[...] + lb_ref[...] + bonus_ref[0]
    o_ref[0] = (y * g_ref[0].astype(F32)).astype(o_ref.dtype)


def _head_selectors(bw):
    sel_np = np.zeros((bw, LANE), np.float32)
    sel_np[np.arange(bw), np.arange(bw) // B_HEAD] = 1.0
    return jnp.asarray(sel_np, BF16), jnp.asarray(sel_np.T, BF16)


def _rwkv_post(y_f, y_b, g, bonus, lnx_g, lnx_b):
    bsz, n, bw = y_f.shape
    tm = _pick(n, 256)
    tok = pl.BlockSpec((1, tm, bw), lambda b, i: (b, i, 0))
    vec = pl.BlockSpec((1, bw), lambda b, i: (0, 0))
    sel, selt = _head_selectors(bw)
    return pl.pallas_call(
        _rwpost_body,
        grid=(bsz, n // tm),
        in_specs=[tok, tok, tok, tok, vec, vec,
                  pl.BlockSpec((bw, LANE), lambda b, i: (0, 0)), pl.BlockSpec((LANE, bw), lambda b, i: (0, 0))],
        out_specs=tok,
        out_shape=jax.ShapeDtypeStruct((bsz, n, bw), BF16),
        compiler_params=_cp(("parallel", "parallel")),
        name="rwkv_post",
    )(y_f, y_b, g, bonus, lnx_g.reshape(1, bw).astype(F32), lnx_b.reshape(1, bw).astype(F32), sel, selt)


def _rwkv_run(prep, states0, p):
    y_f, y_b, s_f, s_b = _rwkv_scan(prep["r"], prep["v"], prep["kk"], prep["f"], prep["b"], states0[0], states0[1])
    return _rwkv_post(y_f, y_b, prep["g"], prep["bonus"], p["lnx_g"], p["lnx_b"]), (s_f, s_b)


HY_N2 = 128
HY_HALO = 8


def _hypre_body(x_ref, hp_ref, hn_ref, cw_ref, cb_ref, x0_ref, uv_ref, *, nb, tm, cw):
    i = pl.program_id(1)
    x = x_ref[0].astype(F32)
    row = lax.broadcasted_iota(jnp.int32, x.shape, 0)
    prev_edge = jnp.where(i > 0, hp_ref[0, HY_HALO - 1:HY_HALO, :].astype(F32), 0.0)
    next_edge = jnp.where(i < nb - 1, hn_ref[0, 0:1, :].astype(F32), 0.0)
    prev = jnp.where(row == 0, prev_edge, pltpu.roll(x, 1, 0))
    nxt = jnp.where(row == tm - 1, next_edge, pltpu.roll(x, tm - 1, 0))
    u = prev * cw_ref[0:1] + x * cw_ref[1:2] + nxt * cw_ref[2:3] + cb_ref[...]
    x0_ref[0] = u[:, :cw]
    uv_ref[0] = u[:, 2 * cw:3 * cw] * u[:, cw:2 * cw]


def _hyena_pre(p, conv_w, conv_b):
    bsz, n, _ = p.shape
    hc = conv_w.shape[1]
    cw = hc // 3
    tm = _pick(n, 256)
    nb = n // tm
    hb = tm // HY_HALO
    cwp = jnp.concatenate([conv_w.astype(F32), jnp.zeros((5, hc), F32)], axis=0)
    out = pl.BlockSpec((1, tm, cw), lambda b, i: (b, i, 0))
    return pl.pallas_call(
        functools.partial(_hypre_body, nb=nb, tm=tm, cw=cw),
        grid=(bsz, nb),
        in_specs=[pl.BlockSpec((1, tm, hc), lambda b, i: (b, i, 0)),
                  pl.BlockSpec((1, HY_HALO, hc), lambda b, i: (b, jnp.maximum(i * hb - 1, 0), 0)),
                  pl.BlockSpec((1, HY_HALO, hc), lambda b, i: (b, jnp.minimum((i + 1) * hb, nb * hb - 1), 0)),
                  pl.BlockSpec((8, hc), lambda b, i: (0, 0)),
                  pl.BlockSpec((1, hc), lambda b, i: (0, 0))],
        out_specs=[out, out],
        out_shape=[jax.ShapeDtypeStruct((bsz, n, cw), F32)] * 2,
        compiler_params=_cp(("parallel", "parallel")),
        name="hyena_pre",
    )(p, p, p, cwp, conv_b.reshape(1, hc).astype(F32))


def _hyfilt_body(z_ref, t_ref, z0_ref, w1_ref, w2_ref, w3_ref, w4_ref, w4b_ref, vec_ref, dl_ref,
                 k_ref, s_ref, *, half_tiles):
    i = pl.program_id(0)
    b1, b2, b3, fr = vec_ref[0:1], vec_ref[1:2], vec_ref[2:3], vec_ref[3:4]

    def mlp3(z):
        h = jnp.sin(fr * (_mm_hi(z, w1_ref[...]) + b1))
        h = jnp.sin(fr * (_mm_hi(h, w2_ref[...]) + b2))
        return jnp.sin(fr * (_mm_hi(h, w3_ref[...]) + b3))

    raw = _mm_hi(mlp3(z_ref[...]), w4_ref[...]) * jnp.exp(-t_ref[...] * dl_ref[...])

    @pl.when(i == 0)
    def _():
        s_ref[...] = jnp.zeros_like(s_ref)

    s_ref[...] += jnp.sum(jnp.abs(raw), axis=0, keepdims=True)
    hb0 = _mm_hi(mlp3(z0_ref[...]), w4b_ref[...])
    row = lax.broadcasted_iota(jnp.int32, raw.shape, 0)
    raw = raw + jnp.where((row == 0) & (i == 0), hb0[0:1], 0.0)
    k_ref[...] = jnp.where((row == 0) & (i == half_tiles), 0.0, raw)


def _hyena_kernel_taps(n, p):
    cw = p["f_w4"].shape[1] // 2
    f32 = lambda a: a.astype(F32)
    row = jnp.arange(2 * n, dtype=jnp.int32)
    pos = jnp.where(row < n, row, (2 * n - row) % n).astype(F32)[:, None]
    tt = pos * jnp.float32(1.0 / (n - 1))
    bands = (HY_EMB - 1) // 2
    ang = (2 * math.pi / n) * pos * jnp.linspace(1e-4, bands - 1, bands, dtype=F32)[None]
    zz = jnp.concatenate([tt, jnp.cos(ang), -jnp.sin(ang), jnp.zeros((2 * n, HY_ORDER - HY_EMB), F32)], axis=-1)
    w1 = jnp.concatenate([f32(p["f_w1"]), jnp.zeros((HY_ORDER - HY_EMB, HY_ORDER), F32)], axis=0)
    vecs = jnp.stack([f32(p["f_b1"]), f32(p["f_b2"]), f32(p["f_b3"]), f32(p["f_freq"])]
                     + [jnp.zeros((HY_ORDER,), F32)] * 4)
    deltas = jnp.abs(jnp.linspace(math.log(HY_TARGET) / HY_SLOW_PCT, math.log(HY_TARGET) / HY_FAST_PCT, cw,
                                  dtype=F32)).reshape(1, cw)
    tr = _pick(n, 512)
    half_tiles = n // tr
    full = lambda shape: pl.BlockSpec(shape, lambda i: tuple(0 for _ in shape))
    w4 = f32(p["f_w4"])
    return pl.pallas_call(
        functools.partial(_hyfilt_body, half_tiles=half_tiles),
        grid=(2 * half_tiles,),
        in_specs=[pl.BlockSpec((tr, HY_ORDER), lambda i: (i, 0)),
                  pl.BlockSpec((tr, 1), lambda i: (i, 0)),
                  full((8, HY_ORDER)), full((HY_ORDER, HY_ORDER)), full((HY_ORDER, HY_ORDER)),
                  full((HY_ORDER, HY_ORDER)),
                  pl.BlockSpec((HY_ORDER, cw), lambda i: (0, i // half_tiles)),
                  pl.BlockSpec((HY_ORDER, cw), lambda i: (0, 1)),
                  full((8, HY_ORDER)), full((1, cw))],
        out_specs=[pl.BlockSpec((tr, cw), lambda i: (i, 0)), pl.BlockSpec((1, cw), lambda i: (0, 0))],
        out_shape=[jax.ShapeDtypeStruct((2 * n, cw), F32), jax.ShapeDtypeStruct((1, cw), F32)],
        compiler_params=_cp(("arbitrary",)),
        name="hyena_filter",
    )(zz, tt, jnp.broadcast_to(zz[0:1], (8, HY_ORDER)), w1, f32(p["f_w2"]), f32(p["f_w3"]), w4, w4, vecs, deltas)


def _dft_consts(n):
    nn = 2 * n
    n2 = HY_N2
    n1 = nn // n2
    n1h = n1 // 2 + 1
    nf = -(-n1h // 8) * 8
    live = (np.arange(nf) < n1h).astype(np.float64)
    a1 = 2 * np.pi * np.outer(np.arange(nf), np.arange(n1)) / n1
    f1 = np.concatenate([np.cos(a1), -np.sin(a1)], axis=0) * np.tile(live, 2)[:, None]
    wgt = live * np.where((np.arange(nf) == 0) | (np.arange(nf) == n1 // 2), 1.0, 2.0)
    f1inv = np.concatenate([np.cos(a1.T) * wgt, -np.sin(a1.T) * wgt], axis=1) / nn
    a2 = 2 * np.pi * np.outer(np.arange(n2), np.arange(n2)) / n2
    c2, s2 = np.cos(a2), -np.sin(a2)
    m2 = np.block([[c2, -s2], [s2, c2]])
    m2inv = np.block([[c2, s2], [-s2, c2]])
    at = 2 * np.pi * np.outer(np.arange(n2), np.arange(nf)) / nn
    tw = np.stack([np.cos(at), -np.sin(at)])
    c = lambda a: jnp.asarray(a.astype(np.float32))
    return dict(n1=n1, n2=n2, nf=nf, f1=c(f1), f1inv=c(f1inv), m2=c(m2), m2inv=c(m2inv),
                tw_s2=c(tw[:, :, :, None]),
                tw_f1=c(np.transpose(tw, (0, 2, 1))[:, :, :, None]))


def _dft1_body(f_ref, x_ref, tw_ref, o_ref, *, nf):
    a = _mm_x3(f_ref[...], x_ref[0])
    ar, ai = a[:nf], a[nf:]
    twr, twi = tw_ref[0, 0], tw_ref[1, 0]
    o_ref[0, 0] = ar * twr - ai * twi
    o_ref[0, 1] = ar * twi + ai * twr


def _dft_stage1(x, consts, rows):
    bsz, _, cw = x.shape
    n2, nf = consts["n2"], consts["nf"]
    xv = x.reshape(bsz, rows, n2 * cw)
    f1 = consts["f1"][:, :rows]
    return pl.pallas_call(
        functools.partial(_dft1_body, nf=nf),
        grid=(bsz, n2),
        in_specs=[pl.BlockSpec((2 * nf, rows), lambda b, j: (0, 0)),
                  pl.BlockSpec((1, rows, cw), lambda b, j: (b, 0, j)),
                  pl.BlockSpec((2, 1, nf, 1), lambda b, j: (0, j, 0, 0))],
        out_specs=pl.BlockSpec((1, 2, nf, cw), lambda b, j: (b, 0, 0, j)),
        out_shape=jax.ShapeDtypeStruct((bsz, 2, nf, n2 * cw), F32),
        compiler_params=_cp(("parallel", "parallel")),
        name="hyena_dft1",
    )(f1, xv, consts["tw_s2"])


def _dftmid_body(a_ref, m2_ref, m2i_ref, h_ref, tw_ref, is_ref, o_ref, *, n2, filt):
    a = a_ref[0, :, 0].reshape(2 * n2, a_ref.shape[-1])
    x = _mm_x3(m2_ref[...], a)
    xr, xi = x[:n2], x[n2:]
    if filt:
        o_ref[0, 0, 0] = xr * is_ref[...]
        o_ref[0, 1, 0] = xi * is_ref[...]
        return
    hr, hi = h_ref[0, 0, 0], h_ref[0, 1, 0]
    y = jnp.concatenate([xr * hr - xi * hi, xr * hi + xi * hr], axis=0)
    z = _mm_x3(m2i_ref[...], y)
    zr, zi = z[:n2], z[n2:]
    twr, twi = tw_ref[0, 0], tw_ref[1, 0]
    o_ref[0, 0, 0] = zr * twr + zi * twi
    o_ref[0, 1, 0] = zi * twr - zr * twi


def _dft_mid(a, h, inv_s, consts, filt):
    bsz = a.shape[0]
    n1, n2 = consts["nf"], consts["n2"]
    cw = a.shape[-1] // n2
    av = a.reshape(bsz, 2, n1, n2, cw)
    blk = pl.BlockSpec((1, 2, 1, n2, cw), lambda b, f: (b, 0, f, 0, 0))
    hblk = pl.BlockSpec((1, 2, 1, n2, cw), (lambda b, f: (0, 0, 0, 0, 0)) if filt else (lambda b, f: (0, 0, f, 0, 0)))
    return pl.pallas_call(
        functools.partial(_dftmid_body, n2=n2, filt=filt),
        grid=(bsz, n1),
        in_specs=[blk,
                  pl.BlockSpec((2 * n2, 2 * n2), lambda b, f: (0, 0)),
                  pl.BlockSpec((2 * n2, 2 * n2), lambda b, f: (0, 0)),
                  hblk,
                  pl.BlockSpec((2, 1, n2, 1), lambda b, f: (0, f, 0, 0)),
                  pl.BlockSpec((1, cw), lambda b, f: (0, 0))],
        out_specs=blk,
        out_shape=jax.ShapeDtypeStruct((bsz, 2, n1, n2, cw), F32),
        compiler_params=_cp(("parallel", "parallel")),
        name="hyena_dftmid_" + ("filter" if filt else "conv"),
    )(av, consts["m2"], consts["m2inv"], h, consts["tw_f1"], inv_s)


def _dft3_body(f_ref, z_ref, x0_ref, uv_ref, bias_ref, o_ref):
    y = _mm_x3(f_ref[...], z_ref[0])
    o_ref[0] = (x0_ref[0] * (y + uv_ref[0] * bias_ref[...])).astype(o_ref.dtype)


def _dft_stage1_inv(z, x0, uv, bias, consts):
    bsz, n, cw = x0.shape
    n1, n2 = consts["nf"], consts["n2"]
    hr = consts["n1"] // 2
    zv = z.reshape(bsz, 2 * n1, n2 * cw)
    tokv = lambda t: t.reshape(bsz, hr, n2 * cw)
    tile = pl.BlockSpec((1, hr, cw), lambda b, j: (b, 0, j))
    out = pl.pallas_call(
        _dft3_body,
        grid=(bsz, n2),
        in_specs=[pl.BlockSpec((hr, 2 * n1), lambda b, j: (0, 0)),
                  pl.BlockSpec((1, 2 * n1, cw), lambda b, j: (b, 0, j)),
                  tile, tile,
                  pl.BlockSpec((1, cw), lambda b, j: (0, 0))],
        out_specs=tile,
        out_shape=jax.ShapeDtypeStruct((bsz, hr, n2 * cw), BF16),
        compiler_params=_cp(("parallel", "parallel")),
        name="hyena_dft3",
    )(consts["f1inv"][:hr], zv, tokv(x0), tokv(uv), bias.reshape(1, cw).astype(F32))
    return out.reshape(bsz, n, cw)


def _hyena(p_lat, p):
    n = p_lat.shape[1]
    consts = _dft_consts(n)
    n1 = consts["n1"]
    x0, uv = _hyena_pre(p_lat, p["conv_w"], p["conv_b"])
    taps, sabs = _hyena_kernel_taps(n, p)
    cw = taps.shape[1]
    ones = jnp.ones((1, cw), F32)
    hk = _dft_stage1(taps[None], consts, n1)
    hspec = _dft_mid(hk, jnp.zeros((1, 2, 1, HY_N2, cw), F32), 1.0 / sabs, consts, True)
    a = _dft_stage1(uv, consts, n1 // 2)
    z = _dft_mid(a, hspec, ones, consts, False)
    return _dft_stage1_inv(z, x0, uv, p["hy_bias"], consts)


def _mod_vectors(c, c_ctx, p):
    bsz, d = c.shape
    cvec = jnp.concatenate([c, c_ctx[None], jnp.zeros((8 - bsz - 1, d), c.dtype)], axis=0).astype(F32)
    m = _mods(cvec, p["w_mod"].astype(F32), p["b_mod"].astype(F32))
    lat = [v[:, None, :] for v in jnp.split(m[:bsz], N_MOD, axis=-1)]
    cx = [jnp.broadcast_to(v[:, None, :], (bsz, 1, d)) for v in jnp.split(m[bsz:bsz + 1], N_MOD, axis=-1)]
    return lat, cx


def _affine(norm_g, shift, scale):
    return norm_g.astype(F32) * (1.0 + scale), shift


def kernel(x, c, ctx, c_ctx,
           l0_w_mod, l0_b_mod, l0_norm1_g, l0_norm2_g, l0_w_in, l0_lam_q1, l0_lam_k1, l0_lam_q2, l0_lam_k2,
           l0_subln_g, l0_mu_prev, l0_mu_next, l0_w0_f, l0_w2_f, l0_a0_f, l0_a2_f, l0_w0_b, l0_w2_b, l0_a0_b,
           l0_a2_b, l0_g2, l0_k_k, l0_k_a, l0_r_k, l0_lnx_g, l0_lnx_b, l0_w_out, l0_mlp_w1, l0_mlp_w2,
           l1_w_mod, l1_b_mod, l1_norm1_g, l1_norm2_g, l1_w_in, l1_conv_w, l1_conv_b, l1_f_w1, l1_f_b1,
           l1_f_w2, l1_f_b2, l1_f_w3, l1_f_b3, l1_f_w4, l1_f_freq, l1_hy_bias, l1_q_norm_g, l1_k_norm_g,
           l1_w_out, l1_mlp_w1, l1_mlp_w2, final_g):
    bsz, n, d = x.shape
    x = x.astype(F32)
    ctx_s = ctx.astype(F32)
    bf = lambda w: w.astype(BF16)

    p0 = dict(w_mod=l0_w_mod, b_mod=l0_b_mod, mu_prev=l0_mu_prev, mu_next=l0_mu_next, w0_f=l0_w0_f, w2_f=l0_w2_f,
              a0_f=l0_a0_f, a2_f=l0_a2_f, w0_b=l0_w0_b, w2_b=l0_w2_b, a0_b=l0_a0_b, a2_b=l0_a2_b, g2=l0_g2,
              k_k=l0_k_k, k_a=l0_k_a, r_k=l0_r_k, lnx_g=l0_lnx_g, lnx_b=l0_lnx_b)
    (sh1, sc1, g1, sh2, sc2, g2), (csh1, csc1, cg1, csh2, csc2, cg2) = _mod_vectors(c, c_ctx, p0)
    aw = 3 * (d // 2)
    w_attn = bf(l0_w_in[:, :aw])
    w_rw = l0_w_in[:, aw:]
    rw_cols = 3 * l0_w0_f.shape[0] + 3 * LANE
    w_rw = bf(jnp.concatenate([w_rw, jnp.zeros((d, rw_cols - w_rw.shape[1]), w_rw.dtype)], axis=1))
    a1, b1 = _affine(l0_norm1_g, sh1, sc1)
    ca1, cb1 = _affine(l0_norm1_g, csh1, csc1)
    pa_lat, pr_lat = _inproj(x, a1, b1, w_attn), _inproj(x, a1, b1, w_rw)
    pa_ctx, pr_ctx = _inproj(ctx_s, ca1, cb1, w_attn), _inproj(ctx_s, ca1, cb1, w_rw)
    lam_init = 0.8 - 0.6 * math.exp(-0.3 * 0)
    lam = (jnp.exp(jnp.sum(l0_lam_q1 * l0_lam_k1).astype(F32))
           - jnp.exp(jnp.sum(l0_lam_q2 * l0_lam_k2).astype(F32)) + lam_init)
    oa_lat, oa_ctx = _diff_attention(pa_lat, pa_ctx, lam, lam_init, l0_subln_g)
    nh, hd = l0_r_k.shape
    zero = jnp.zeros((bsz, hd, nh * hd), F32)
    y_ctx, states_c = _rwkv_run(_rwkv_prepare(pr_ctx, p0), (zero, zero), p0)
    y_lat, _ = _rwkv_run(_rwkv_prepare(pr_lat, p0), states_c, p0)
    w_out0 = bf(l0_w_out)
    x = _outproj(oa_lat, y_lat, w_out0, x, g1)
    ctx_s = _outproj(oa_ctx, y_ctx, w_out0, ctx_s, cg1)
    w1, w2 = bf(l0_mlp_w1), bf(l0_mlp_w2)
    a2, b2 = _affine(l0_norm2_g, sh2, sc2)
    ca2, cb2 = _affine(l0_norm2_g, csh2, csc2)
    x = _mlp(x, a2, b2, g2, w1, w2)
    ctx_s = _mlp(ctx_s, ca2, cb2, cg2, w1, w2)

    p1 = dict(w_mod=l1_w_mod, b_mod=l1_b_mod, conv_w=l1_conv_w, conv_b=l1_conv_b, f_w1=l1_f_w1, f_b1=l1_f_b1,
              f_w2=l1_f_w2, f_b2=l1_f_b2, f_w3=l1_f_w3, f_b3=l1_f_b3, f_w4=l1_f_w4, f_freq=l1_f_freq,
              hy_bias=l1_hy_bias)
    (sh1, sc1, g1, sh2, sc2, g2), (csh1, csc1, _, _, _, _) = _mod_vectors(c, c_ctx, p1)
    hy_cols = l1_conv_w.shape[1]
    kv_cols = 2 * (2 * D_HEAD)
    w_in1 = bf(l1_w_in)
    a1, b1 = _affine(l1_norm1_g, sh1, sc1)
    ca1, cb1 = _affine(l1_norm1_g, csh1, csc1)
    p_lat = _inproj(x, a1, b1, w_in1)
    pkv_ctx = _inproj(ctx_s, ca1, cb1, w_in1[:, -kv_cols:])
    o_hy = _hyena(p_lat, p1)
    o_at = _gqa_attention(p_lat, pkv_ctx, hy_cols, l1_q_norm_g, l1_k_norm_g)
    x = _outproj(o_hy, o_at, bf(l1_w_out), x, g1)
    a2, b2 = _affine(l1_norm2_g, sh2, sc2)
    return _mlp(x, a2, b2, g2, bf(l1_mlp_w1), bf(l1_mlp_w2), final_g=final_g)
```

```python
import functools
import math

import numpy as np
import jax
import jax.numpy as jnp
from jax import lax
from jax.experimental import pallas as pl
from jax.experimental.pallas import tpu as pltpu

F32 = jnp.float32
BF16 = jnp.bfloat16
HI = lax.Precision.HIGHEST

NORM_EPS = 1e-6
ROPE_THETA = 10000.0
GRID_W = 64
N_MOD = 6
A_QK_DIM = 64
A_V_DIM = 128
B_HEAD = 64
B_DECAY_RANK = 64
B_ICL_RANK = 64
B_GATE_RANK = 160
B_GN_EPS = 64e-5
HY_EMB = 33
HY_ORDER = 64
HY_TARGET = 1e-2
HY_FAST_PCT = 0.3
HY_SLOW_PCT = 1.5
D_HEAD = 128
D_KV_GROUP = 4

LANE = 128
VMEM_LIMIT = 56 * 1024 * 1024
LOG2E = 1.4426950408889634


def _cp(sem, vmem=VMEM_LIMIT):
    return pltpu.CompilerParams(dimension_semantics=sem, vmem_limit_bytes=vmem)


def _pick(n, pref, step=8):
    t = max(step, min(n, pref) // step * step)
    while n % t:
        t -= step
    return t


def _mods_body(c_ref, w_ref, b_ref, o_ref):
    c = c_ref[...]
    s = c * jax.nn.sigmoid(c)
    o_ref[...] = jnp.dot(s, w_ref[...], preferred_element_type=F32, precision=HI) + b_ref[...]


def _mods(cvec, w_mod, b_mod):
    m, d = cvec.shape
    n = w_mod.shape[1]
    tn = _pick(n, 1024, LANE)
    return pl.pallas_call(
        _mods_body,
        grid=(n // tn,),
        in_specs=[pl.BlockSpec((m, d), lambda j: (0, 0)),
                  pl.BlockSpec((d, tn), lambda j: (0, j)),
                  pl.BlockSpec((1, tn), lambda j: (0, j))],
        out_specs=pl.BlockSpec((m, tn), lambda j: (0, j)),
        out_shape=jax.ShapeDtypeStruct((m, n), F32),
        compiler_params=_cp(("arbitrary",)),
        name="mods",
    )(cvec, w_mod, b_mod.reshape(1, n))


def _norm_mod(x, a, b):
    ms = jnp.mean(x * x, axis=-1, keepdims=True)
    return x * lax.rsqrt(ms + NORM_EPS) * a + b


def _inproj_body(x_ref, a_ref, b_ref, w_ref, o_ref, xn_ref):
    @pl.when(pl.program_id(2) == 0)
    def _():
        xn_ref[...] = _norm_mod(x_ref[0], a_ref[0], b_ref[0]).astype(BF16)

    o_ref[0] = jnp.dot(xn_ref[...], w_ref[...], preferred_element_type=F32).astype(o_ref.dtype)


def _inproj(x, a, b, w, out_dtype=BF16, tm_pref=1024, tn_pref=512):
    bsz, n, d = x.shape
    nn = w.shape[1]
    tm, tn = _pick(n, tm_pref), _pick(nn, tn_pref, LANE)
    return pl.pallas_call(
        _inproj_body,
        grid=(bsz, n // tm, nn // tn),
        in_specs=[pl.BlockSpec((1, tm, d), lambda bi, i, j: (bi, i, 0)),
                  pl.BlockSpec((1, 1, d), lambda bi, i, j: (bi, 0, 0)),
                  pl.BlockSpec((1, 1, d), lambda bi, i, j: (bi, 0, 0)),
                  pl.BlockSpec((d, tn), lambda bi, i, j: (0, j))],
        out_specs=pl.BlockSpec((1, tm, tn), lambda bi, i, j: (bi, i, j)),
        out_shape=jax.ShapeDtypeStruct((bsz, n, nn), out_dtype),
        scratch_shapes=[pltpu.VMEM((tm, d), BF16)],
        compiler_params=_cp(("parallel", "parallel", "arbitrary")),
        name="inproj",
    )(x, a, b, w)


def _outproj_body(oa_ref, ob_ref, wa_ref, wb_ref, x_ref, g_ref, y_ref):
    acc = jnp.dot(oa_ref[0], wa_ref[...], preferred_element_type=F32)
    acc += jnp.dot(ob_ref[0], wb_ref[...], preferred_element_type=F32)
    y_ref[0] = x_ref[0] + g_ref[0] * acc


def _outproj(oa, ob, w, x, g, tm_pref=1024, tn_pref=512):
    bsz, n, ka = oa.shape
    kb = ob.shape[2]
    d = w.shape[1]
    tm, tn = _pick(n, tm_pref), _pick(d, tn_pref, LANE)
    return pl.pallas_call(
        _outproj_body,
        grid=(bsz, n // tm, d // tn),
        in_specs=[pl.BlockSpec((1, tm, ka), lambda bi, i, j: (bi, i, 0)),
                  pl.BlockSpec((1, tm, kb), lambda bi, i, j: (bi, i, 0)),
                  pl.BlockSpec((ka, tn), lambda bi, i, j: (0, j)),
                  pl.BlockSpec((kb, tn), lambda bi, i, j: (0, j)),
                  pl.BlockSpec((1, tm, tn), lambda bi, i, j: (bi, i, j)),
                  pl.BlockSpec((1, 1, tn), lambda bi, i, j: (bi, 0, j))],
        out_specs=pl.BlockSpec((1, tm, tn), lambda bi, i, j: (bi, i, j)),
        out_shape=jax.ShapeDtypeStruct((bsz, n, d), F32),
        compiler_params=_cp(("parallel", "parallel", "parallel")),
        name="outproj",
    )(oa, ob, w[:ka], w[ka:], x, g)


def _mlp_body(x_ref, a_ref, b_ref, g_ref, w1_ref, w2_ref, fg_ref, y_ref, xn_ref, acc_ref, *, nf, final):
    f = pl.program_id(2)

    @pl.when(f == 0)
    def _():
        xn_ref[...] = _norm_mod(x_ref[0], a_ref[0], b_ref[0]).astype(BF16)
        acc_ref[...] = jnp.zeros_like(acc_ref)

    h = jnp.dot(xn_ref[...], w1_ref[...], preferred_element_type=F32)
    h = jnp.square(jnp.maximum(h, 0.0)).astype(BF16)
    acc_ref[...] += jnp.dot(h, w2_ref[...], preferred_element_type=F32)

    @pl.when(f == nf - 1)
    def _():
        y = x_ref[0] + g_ref[0] * acc_ref[...]
        if final:
            ms = jnp.mean(y * y, axis=-1, keepdims=True)
            y = y * lax.rsqrt(ms + NORM_EPS) * fg_ref[...]
        y_ref[0] = y


def _mlp(x, a, b, g, w1, w2, final_g=None, tm_pref=512, tf_pref=1024):
    bsz, n, d = x.shape
    dff = w1.shape[1]
    tm, tf = _pick(n, tm_pref), _pick(dff, tf_pref, LANE)
    nf = dff // tf
    final = final_g is not None
    fg = (final_g if final else jnp.ones((d,), F32)).reshape(1, d).astype(F32)
    return pl.pallas_call(
        functools.partial(_mlp_body, nf=nf, final=final),
        grid=(bsz, n // tm, nf),
        in_specs=[pl.BlockSpec((1, tm, d), lambda bi, i, f: (bi, i, 0)),
                  pl.BlockSpec((1, 1, d), lambda bi, i, f: (bi, 0, 0)),
                  pl.BlockSpec((1, 1, d), lambda bi, i, f: (bi, 0, 0)),
                  pl.BlockSpec((1, 1, d), lambda bi, i, f: (bi, 0, 0)),
                  pl.BlockSpec((d, tf), lambda bi, i, f: (0, f)),
                  pl.BlockSpec((tf, d), lambda bi, i, f: (f, 0)),
                  pl.BlockSpec((1, d), lambda bi, i, f: (0, 0))],
        out_specs=pl.BlockSpec((1, tm, d), lambda bi, i, f: (bi, i, 0)),
        out_shape=jax.ShapeDtypeStruct((bsz, n, d), F32),
        scratch_shapes=[pltpu.VMEM((tm, d), BF16), pltpu.VMEM((tm, d), F32)],
        compiler_params=_cp(("parallel", "parallel", "arbitrary")),
        name="mlp",
    )(x, a, b, g, w1, w2, fg)


def _rope_tables(n, dim, reps):
    rows = n // GRID_W
    row = jnp.repeat(jnp.arange(rows, dtype=F32), GRID_W)
    col = jnp.tile(jnp.arange(GRID_W, dtype=F32), rows)
    half = dim // 2
    inv = ROPE_THETA ** (-jnp.arange(0, half, 2, dtype=F32) / half)
    ar, ac = row[:, None] * inv, col[:, None] * inv
    cos = jnp.concatenate([jnp.cos(ar), jnp.cos(ar), jnp.cos(ac), jnp.cos(ac)], axis=-1)
    sin = jnp.concatenate([-jnp.sin(ar), jnp.sin(ar), -jnp.sin(ac), jnp.sin(ac)], axis=-1)
    return jnp.tile(cos, (1, reps)), jnp.tile(sin, (1, reps))


def _swap_matrix(dim, reps):
    q = dim // 4
    width = dim * reps
    p = np.zeros((width, width), np.float32)
    for j in range(width):
        base, r = (j // dim) * dim, j % dim
        axis, which, f = r // (2 * q), (r % (2 * q)) // q, r % q
        p[base + axis * 2 * q + (1 - which) * q + f, j] = 1.0
    return p


def _rope_norm(x, cos_ref, sin_ref, p_ref, g_ref, gs_ref, *, norm, rope, scale=1.0):
    y = x.astype(F32)
    if rope:
        ys = jnp.dot(x, p_ref[...], preferred_element_type=F32)
    if norm:
        rs = lax.rsqrt(jnp.mean(y * y, axis=-1, keepdims=True) + NORM_EPS)
        y = y * rs * g_ref[...]
        if rope:
            ys = ys * rs * gs_ref[...]
    if rope:
        y = y * cos_ref[...] + ys * sin_ref[...]
    return y * scale if scale != 1.0 else y


def _rope_operands(n, dim, gain, rope):
    reps = LANE // dim
    if rope:
        cos, sin = _rope_tables(n, dim, reps)
        pm = jnp.asarray(_swap_matrix(dim, reps), BF16)
    else:
        cos = sin = jnp.zeros((n, LANE), F32)
        pm = jnp.zeros((LANE, LANE), BF16)
    if gain is not None:
        g = jnp.tile(gain.astype(F32), reps).reshape(1, LANE)
        gs = jnp.dot(g, jnp.asarray(_swap_matrix(dim, reps)), precision=HI)
    else:
        g = gs = jnp.ones((1, LANE), F32)
    return cos, sin, pm, g, gs


def _kprep_body(x_ref, cos_ref, sin_ref, p_ref, g_ref, gs_ref, o_ref, *, heads, norm, rope):
    for h in range(heads):
        sl = slice(h * LANE, (h + 1) * LANE)
        y = _rope_norm(x_ref[0, :, sl], cos_ref, sin_ref, p_ref, g_ref, gs_ref, norm=norm, rope=rope)
        o_ref[0, :, sl] = y.astype(o_ref.dtype)


def _kprep(p, col0, heads, *, dim, gain=None, rope=True, tq_pref=512):
    bsz, n, _ = p.shape
    tq = _pick(n, tq_pref)
    hb = heads
    while col0 % (hb * LANE):
        hb //= 2
    w = hb * LANE
    cb = col0 // w
    cos, sin, pm, g, gs = _rope_operands(n, dim, gain, rope)
    return pl.pallas_call(
        functools.partial(_kprep_body, heads=hb, norm=gain is not None, rope=rope),
        grid=(bsz, n // tq, heads // hb),
        in_specs=[pl.BlockSpec((1, tq, w), lambda b, i, j: (b, i, cb + j)),
                  pl.BlockSpec((tq, LANE), lambda b, i, j: (i, 0)),
                  pl.BlockSpec((tq, LANE), lambda b, i, j: (i, 0)),
                  pl.BlockSpec((LANE, LANE), lambda b, i, j: (0, 0)),
                  pl.BlockSpec((1, LANE), lambda b, i, j: (0, 0)),
                  pl.BlockSpec((1, LANE), lambda b, i, j: (0, 0))],
        out_specs=pl.BlockSpec((1, tq, w), lambda b, i, j: (b, i, j)),
        out_shape=jax.ShapeDtypeStruct((bsz, n, heads * LANE), BF16),
        compiler_params=_cp(("parallel", "parallel", "parallel")),
        name="kprep",
    )(p, cos, sin, pm, g, gs)


FLASH_KEYS = 1280
FLASH_HEADROOM = 64.0


def _knorm_body(k_ref, o_ref):
    k = k_ref[0].astype(F32)
    nrm2 = jnp.max(jnp.sum(k * k, axis=1, keepdims=True), axis=0, keepdims=True)
    o_ref[0, 0] = jnp.broadcast_to(jnp.sqrt(nrm2), (8, LANE))


def _key_block_norms(k, kcol0, heads, ts):
    bsz, nk, _ = k.shape
    nsub = nk // ts
    kb = kcol0 // LANE
    out = pl.pallas_call(
        _knorm_body,
        grid=(bsz, heads, nsub),
        in_specs=[pl.BlockSpec((1, ts, LANE), lambda b, h, c: (b, c, kb + h))],
        out_specs=pl.BlockSpec((1, 1, 8, LANE), lambda b, h, c: (b, h, c, 0)),
        out_shape=jax.ShapeDtypeStruct((bsz, heads, nsub * 8, LANE), F32),
        compiler_params=_cp(("parallel", "parallel", "parallel")),
        name="key_block_norms",
    )(k)
    return out[:, :, ::8, 0].reshape(-1)


def _flash_body(kmax_ref, qr_ref, cos_ref, sin_ref, pm_ref, g_ref, gs_ref, k_ref, v_ref, e1_ref, e2_ref, o_ref,
                q_ref, m_ref, l_ref, acc_ref, thr_ref, *, ts, nsub, mode, tq, norm, rope, scale):
    base = (pl.program_id(0) * pl.num_programs(1) + pl.program_id(1)) * nsub
    prep = functools.partial(_rope_norm, cos_ref=cos_ref, sin_ref=sin_ref, p_ref=pm_ref, g_ref=g_ref, gs_ref=gs_ref,
                             norm=norm, rope=rope, scale=scale)
    if mode == "diff":
        y = prep(qr_ref[0])
        lane = lax.broadcasted_iota(jnp.int32, y.shape, 1)
        q_ref[:tq] = jnp.where(lane < A_QK_DIM, y, 0.0).astype(BF16)
        q_ref[tq:] = jnp.where(lane >= A_QK_DIM, y, 0.0).astype(BF16)
    else:
        for j in range(D_KV_GROUP):
            q_ref[j * tq:(j + 1) * tq] = prep(qr_ref[0, :, j * LANE:(j + 1) * LANE]).astype(BF16)
    q = q_ref[...]

    def scores(off, size):
        k = k_ref[0, pl.ds(off, size), :]
        return lax.dot_general(q, k, (((1,), (1,)), ((), ())), preferred_element_type=F32)

    def block(c):
        off = pl.multiple_of(c * ts, ts)
        return scores(off, ts), v_ref[0, pl.ds(off, ts), :]

    def wide(m):
        return jnp.concatenate([m] * (ts // LANE), axis=1)

    m0 = jnp.broadcast_to(jnp.max(scores(0, 2 * LANE), axis=1, keepdims=True), m_ref.shape)
    m_ref[...] = m0
    l_ref[...] = jnp.zeros_like(l_ref)
    acc_ref[...] = jnp.zeros_like(acc_ref)
    qf = q.astype(F32)
    qn = jnp.sqrt(jnp.sum(qf * qf, axis=1, keepdims=True))
    thr_ref[0] = jnp.min((m0[:, :1] + FLASH_HEADROOM) / qn)

    def step(c, carry):
        fixed = kmax_ref[base + c] * 1.001 <= thr_ref[0]

        @pl.when(fixed)
        def _():
            s, v = block(c)
            p = jnp.exp2(s - wide(m_ref[...]))
            l_ref[...] += jnp.sum(p, axis=1, keepdims=True)
            acc_ref[...] += jnp.dot(p.astype(BF16), v, preferred_element_type=F32)

        @pl.when(jnp.logical_not(fixed))
        def _():
            s, v = block(c)
            m_prev = m_ref[...]
            m_new = jnp.maximum(m_prev, jnp.max(s, axis=1, keepdims=True))
            alpha = jnp.exp2(m_prev - m_new)
            p = jnp.exp2(s - wide(m_new))
            l_ref[...] = alpha * l_ref[...] + jnp.sum(p, axis=1, keepdims=True)
            acc_ref[...] = alpha * acc_ref[...] + jnp.dot(p.astype(BF16), v, preferred_element_type=F32)
            m_ref[...] = m_new

        return carry

    lax.fori_loop(0, nsub, step, 0)

    o = acc_ref[...] / l_ref[...]
    if mode == "diff":
        d = o[:tq] - e1_ref[...] * o[tq:]
        ms = jnp.mean(d * d, axis=-1, keepdims=True)
        o_ref[0] = (d * lax.rsqrt(ms + NORM_EPS) * e2_ref[...]).astype(o_ref.dtype)
    else:
        for j in range(D_KV_GROUP):
            o_ref[0, :, j * LANE:(j + 1) * LANE] = o[j * tq:(j + 1) * tq].astype(o_ref.dtype)


def _flash(pq, qcol0, k, v, kcol0, vcol0, e1, e2, *, mode, hk, dim, tq, gain=None, rope=True):
    bsz, n, _ = pq.shape
    nk = k.shape[1]
    g = 2 if mode == "diff" else D_KV_GROUP
    mq = g * tq
    qw = LANE if mode == "diff" else D_KV_GROUP * LANE
    qb = qcol0 // qw
    ts = _pick(nk, FLASH_KEYS, 2 * LANE)
    kb, vb = kcol0 // LANE, vcol0 // LANE
    ow = LANE if mode == "diff" else D_KV_GROUP * LANE
    kmax = _key_block_norms(k, kcol0, hk, ts)
    cos, sin, pm, gq, gqs = _rope_operands(n, dim, gain, rope)
    const = lambda shape: pl.BlockSpec(shape, lambda b, h, i: (0, 0))
    return pl.pallas_call(
        functools.partial(_flash_body, ts=ts, nsub=nk // ts, mode=mode, tq=tq, norm=gain is not None, rope=rope,
                          scale=dim ** -0.5 * LOG2E),
        grid=(bsz, hk, n // tq),
        in_specs=[pl.BlockSpec(memory_space=pltpu.SMEM),
                  pl.BlockSpec((1, tq, qw), lambda b, h, i: (b, i, qb + h)),
                  pl.BlockSpec((tq, LANE), lambda b, h, i: (i, 0)),
                  pl.BlockSpec((tq, LANE), lambda b, h, i: (i, 0)),
                  const((LANE, LANE)), const((1, LANE)), const((1, LANE)),
                  pl.BlockSpec((1, nk, LANE), lambda b, h, i: (b, 0, kb + h)),
                  pl.BlockSpec((1, nk, LANE), lambda b, h, i: (b, 0, vb + h)),
                  const((1, LANE)), const((1, LANE))],
        out_specs=pl.BlockSpec((1, tq, ow), lambda b, h, i: (b, i, h)),
        out_shape=jax.ShapeDtypeStruct((bsz, n, hk * ow), BF16),
        scratch_shapes=[pltpu.VMEM((mq, LANE), BF16), pltpu.VMEM((mq, LANE), F32), pltpu.VMEM((mq, LANE), F32),
                        pltpu.VMEM((mq, LANE), F32), pltpu.SMEM((1,), F32)],
        compiler_params=_cp(("parallel", "parallel", "arbitrary")),
        name="flash_" + mode,
    )(kmax, pq, cos, sin, pm, gq, gqs, k, v, e1, e2)


def _diff_attention(pa_lat, pa_ctx, c0, aw, lam, lam_init, subln_g):
    n, nctx = pa_lat.shape[1], pa_ctx.shape[1]
    heads = aw // LANE
    k_l = _kprep(pa_lat, c0 + aw, heads, dim=A_QK_DIM)
    k_all = jnp.concatenate([k_l, pa_ctx[..., c0 + aw:c0 + 2 * aw]], axis=1)
    v_all = jnp.concatenate([pa_lat[..., c0 + 2 * aw:c0 + 3 * aw], pa_ctx[..., c0 + 2 * aw:c0 + 3 * aw]], axis=1)
    e1 = jnp.full((1, LANE), lam, F32)
    e2 = (subln_g.astype(F32) * (1.0 - lam_init)).reshape(1, LANE)
    o_lat = _flash(pa_lat, c0, k_all, v_all, 0, 0, e1, e2, mode="diff", hk=heads, dim=A_QK_DIM, tq=_pick(n, 1024))
    o_ctx = _flash(pa_ctx, c0, pa_ctx, pa_ctx, c0 + aw, c0 + 2 * aw, e1, e2, mode="diff", hk=heads, dim=A_QK_DIM,
                   tq=_pick(nctx, 512), rope=False)
    return o_lat, o_ctx


def _gqa_attention(p_lat, pkv_ctx, qcol0, q_norm_g, k_norm_g):
    n = p_lat.shape[1]
    hq, hk = 2 * D_KV_GROUP, 2
    kcol0 = qcol0 + hq * D_HEAD
    vcol0 = kcol0 + hk * D_HEAD
    k_l = _kprep(p_lat, kcol0, hk, dim=D_HEAD, gain=k_norm_g)
    k_c = _kprep(pkv_ctx, 0, hk, dim=D_HEAD, gain=k_norm_g, rope=False)
    k_all = jnp.concatenate([k_l, k_c], axis=1)
    v_all = jnp.concatenate([p_lat[..., vcol0:vcol0 + hk * D_HEAD], pkv_ctx[..., hk * D_HEAD:]], axis=1)
    dummy = jnp.zeros((1, LANE), F32)
    return _flash(p_lat, qcol0, k_all, v_all, 0, 0, dummy, dummy, mode="gqa", hk=hk, dim=D_HEAD,
                  tq=_pick(n, 512), gain=q_norm_g)


RW_CHUNK = 64
RW_HALO = 8


def _mm_hi(a, b):
    return jnp.dot(a, b, preferred_element_type=F32, precision=HI)


def _mm_bf(a, b):
    return jnp.dot(a.astype(BF16), b.astype(BF16), preferred_element_type=F32)


def _split3(x):
    x1 = x.astype(BF16)
    r1 = x - x1.astype(F32)
    x2 = r1.astype(BF16)
    return x1, x2, (r1 - x2.astype(F32)).astype(BF16)


def _mm_x3(a, b):
    ah, al, _ = _split3(a)
    bh, bl, _ = _split3(b)
    d = lambda u, w: jnp.dot(u, w, preferred_element_type=F32)
    return d(ah, bh) + (d(ah, bl) + d(al, bh))


def _mm_sel(z, sel):
    z1, z2, z3 = _split3(z)
    d = lambda u: jnp.dot(u, sel, preferred_element_type=F32)
    return d(z1) + (d(z2) + d(z3))


def _rwprep_body(x_ref, hp_ref, hn_ref, mup_ref, mun_ref, vec_ref, w2f_ref, w2b_ref, a2f_ref, a2b_ref,
                 g2_ref, sel_ref, selt_ref,
                 r_ref, v_ref, kk_ref, lwf_ref, bf_ref, kf_ref, lwb_ref, bb_ref, kb_ref, g_ref, bonus_ref,
                 *, nb, tm, bw):
    i = pl.program_id(1)
    x = x_ref[0].astype(F32)
    row = lax.broadcasted_iota(jnp.int32, x.shape, 0)
    prev_edge = jnp.where(i > 0, hp_ref[0, RW_HALO - 1:RW_HALO, :].astype(F32), 0.0)
    next_edge = jnp.where(i < nb - 1, hn_ref[0, 0:1, :].astype(F32), 0.0)
    prev = jnp.where(row == 0, prev_edge, pltpu.roll(x, 1, 0))
    nxt = jnp.where(row == tm - 1, next_edge, pltpu.roll(x, tm - 1, 0))
    xs = x + mup_ref[...] * (prev - x) + mun_ref[...] * (nxt - x)
    r, k, v = xs[:, :bw], xs[:, bw:2 * bw], xs[:, 2 * bw:3 * bw]
    lr = xs[:, 3 * bw:3 * bw + LANE]
    gd = xs[:, 3 * bw + LANE:3 * bw + 3 * LANE]
    k_k, k_a, r_k = vec_ref[0:1], vec_ref[1:2], vec_ref[2:3]
    w0f, a0f, w0b, a0b = vec_ref[3:4], vec_ref[4:5], vec_ref[5:6], vec_ref[6:7]
    sel, selt = sel_ref[...], selt_ref[...]

    def head_sum(z):
        return _mm_sel(_mm_sel(z, sel), selt)

    kk = k * k_k
    kk = kk * lax.rsqrt(head_sum(kk * kk) + 1e-12)
    g_ref[0] = _mm_bf(jax.nn.sigmoid(gd), g2_ref[...]).astype(g_ref.dtype)
    th = jnp.tanh(lr)
    ksum = jnp.zeros_like(k)
    for (w0, a0, w2_ref, a2_ref, lw_ref, b_ref, kd_ref) in (
            (w0f, a0f, w2f_ref, a2f_ref, lwf_ref, bf_ref, kf_ref),
            (w0b, a0b, w2b_ref, a2b_ref, lwb_ref, bb_ref, kb_ref)):
        w_log = -jax.nn.softplus(-(w0 + _mm_bf(th, w2_ref[...]))) - 0.5
        logw = -jnp.exp(w_log)
        a = jax.nn.sigmoid(a0 + _mm_bf(lr, a2_ref[...]))
        k_d = k * (1.0 + (a - 1.0) * k_a)
        ksum = ksum + k_d
        lw_ref[0] = logw
        b_ref[0] = kk * a
        kd_ref[0] = k_d
    r_ref[0] = r
    v_ref[0] = v
    kk_ref[0] = kk
    bonus_ref[0] = head_sum(r * ksum * r_k) * v


def _rwkv_prepare(pr, p, cols):
    bsz, n, _ = pr.shape
    bw = p["w0_f"].shape[0]
    nh = bw // B_HEAD
    tm = _pick(n, 256)
    nb = n // tm
    hb = tm // RW_HALO
    pad = cols - 3 * bw

    def padded(vec):
        return jnp.concatenate([vec.astype(F32), jnp.zeros((cols - vec.shape[0],), F32)]).reshape(1, cols)

    vecs = jnp.stack([p["k_k"], p["k_a"], p["r_k"].reshape(-1), p["w0_f"], p["a0_f"], p["w0_b"], p["a0_b"],
                      jnp.zeros((bw,), F32)]).astype(F32)
    zr = jnp.zeros((B_DECAY_RANK, bw), F32)
    w2 = {d: jnp.concatenate([p["w2_" + d].astype(F32), zr], axis=0) for d in "fb"}
    a2 = {d: jnp.concatenate([zr, p["a2_" + d].astype(F32)], axis=0) for d in "fb"}
    g2 = jnp.concatenate([p["g2"].astype(F32), jnp.zeros((2 * LANE - B_GATE_RANK, bw), F32)], axis=0)
    sel_np = np.zeros((bw, LANE), np.float32)
    sel_np[np.arange(bw), np.arange(bw) // B_HEAD] = 1.0
    sel, selt = jnp.asarray(sel_np, BF16), jnp.asarray(sel_np.T, BF16)
    w2 = {d: w2[d].astype(BF16) for d in "fb"}
    a2 = {d: a2[d].astype(BF16) for d in "fb"}
    g2 = g2.astype(BF16)
    assert pad >= 3 * LANE and B_DECAY_RANK + B_ICL_RANK == LANE

    full = lambda shape: pl.BlockSpec(shape, lambda b, i: tuple(0 for _ in shape))
    tok = pl.BlockSpec((1, tm, bw), lambda b, i: (b, i, 0))
    hm = tok
    hm_shape = jax.ShapeDtypeStruct((bsz, n, bw), F32)
    outs = pl.pallas_call(
        functools.partial(_rwprep_body, nb=nb, tm=tm, bw=bw),
        grid=(bsz, nb),
        in_specs=[pl.BlockSpec((1, tm, cols), lambda b, i: (b, i, 0)),
                  pl.BlockSpec((1, RW_HALO, cols), lambda b, i: (b, jnp.maximum(i * hb - 1, 0), 0)),
                  pl.BlockSpec((1, RW_HALO, cols), lambda b, i: (b, jnp.minimum((i + 1) * hb, nb * hb - 1), 0)),
                  full((1, cols)), full((1, cols)), full((8, bw)),
                  full((LANE, bw)), full((LANE, bw)), full((LANE, bw)), full((LANE, bw)),
                  full((2 * LANE, bw)), full((bw, LANE)), full((LANE, bw))],
        out_specs=[hm] * 9 + [tok, tok],
        out_shape=[hm_shape] * 9 + [jax.ShapeDtypeStruct((bsz, n, bw), BF16),
                                    jax.ShapeDtypeStruct((bsz, n, bw), F32)],
        compiler_params=_cp(("parallel", "parallel")),
        name="rwkv_prepare",
    )(pr, pr, pr, padded(p["mu_prev"]), padded(p["mu_next"]), vecs, w2["f"], w2["b"], a2["f"], a2["b"],
      g2, sel, selt)
    r, v, kk, lwf, bf, kf, lwb, bb, kb, g, bonus = outs
    return dict(r=r, v=v, kk=kk, g=g, bonus=bonus, f=(lwf, bf, kf), b=(lwb, bb, kb))


RW_GROUP = 4


def _bd(x):
    t = jnp.concatenate([x.astype(BF16)] * RW_GROUP, axis=1)
    r = lax.broadcasted_iota(jnp.int32, t.shape, 1) // B_HEAD
    c = lax.broadcasted_iota(jnp.int32, t.shape, 2) // B_HEAD
    return jnp.where(r == c, t, jnp.zeros_like(t))


def _bdot(a, b, dims):
    return jnp.einsum(dims, a.astype(BF16), b.astype(BF16), preferred_element_type=F32)


def _groups(x):
    gw = RW_GROUP * B_HEAD
    return jnp.stack([x[:, g * gw:(g + 1) * gw] for g in range(x.shape[1] // gw)])


def _rw_decays(r, kk, b, kd, lw, reverse):
    ck = r.shape[0]
    ti = lax.broadcasted_iota(jnp.int32, (ck, ck), 0)
    si = lax.broadcasted_iota(jnp.int32, (ck, ck), 1)
    tri = ((si >= ti) if reverse else (si <= ti)).astype(BF16)
    l1, l2, l3 = _split3(lw)
    tdot = lambda z: jnp.dot(tri, z, preferred_element_type=F32)
    cum = tdot(l1) + (tdot(l2) + tdot(l3))
    total = cum[0:1] if reverse else cum[ck - 1:ck]
    winv = jnp.exp(-cum)
    wrest = jnp.exp(total - cum)
    return (-kk * jnp.exp(cum - lw), b * winv, kd * winv, r * jnp.exp(cum), b * wrest, kd * wrest, jnp.exp(total))


def _rw_chunks(fwd, bwd, s_f, s_b):
    nseq = len(fwd)
    ck, w = fwd[0][0].shape
    gw = RW_GROUP * B_HEAD
    ng = w // gw
    parts = []
    for probs, rev in ((fwd, False), (bwd, True)):
        for (r, v, kk, b, kd, lw) in probs:
            parts.append([_groups(z) for z in _rw_decays(r, kk, b, kd, lw, rev)] + [_groups(v)])
    at, bt, kt, rt, btw, ktw, decay, v = [jnp.concatenate(zs, axis=0) for zs in zip(*parts)]
    s = jnp.concatenate([_groups(z) for z in list(s_f) + list(s_b)], axis=0)
    nslab = 2 * nseq * ng
    shape = (nslab, ck, gw)
    lag = lax.broadcasted_iota(jnp.int32, shape, 1) - lax.broadcasted_iota(jnp.int32, shape, 2) % ck
    lag = jnp.where(lax.broadcasted_iota(jnp.int32, shape, 0) >= nseq * ng, -lag, lag)
    incl, strict = lag >= 0, lag > 0
    eye = (lag == 0).astype(F32)

    bdv = _bd(v)
    lhs = jnp.concatenate([at, rt], axis=1)
    lb, lk = _bdot(lhs, _bd(bt), "gtk,gnk->gtn"), _bdot(lhs, _bd(kt), "gtk,gnk->gtn")
    lab, mrb = jnp.where(strict, lb[:, :ck], 0.0), jnp.where(incl, lb[:, ck:], 0.0)
    lak, mrk = jnp.where(strict, lk[:, :ck], 0.0), jnp.where(incl, lk[:, ck:], 0.0)
    tinv, pw = eye + lab, lab
    span = 2
    while span < ck:
        pw = _bdot(pw, _bd(pw), "gtk,gkn->gtn")
        tinv = tinv + _bdot(tinv, _bd(pw), "gtk,gkn->gtn")
        span *= 2
    p1 = _bdot(tinv, _bd(at), "gtk,gkn->gtn")
    q1 = _bdot(tinv, _bd(_bdot(lak, bdv, "gtk,gkn->gtn")), "gtk,gkn->gtn")
    bds = _bd(s)
    u = _bdot(p1, bds, "gtk,gnk->gtn") + q1
    y = _bdot(rt, bds, "gtk,gnk->gtn") + _bdot(mrb, _bd(u), "gtk,gkn->gtn") + _bdot(mrk, bdv, "gtk,gkn->gtn")
    z = _bdot(jnp.concatenate([u, v], axis=1), jnp.concatenate([btw, ktw], axis=1), "gtm,gtn->gmn")
    lane_head = lax.broadcasted_iota(jnp.int32, (nslab, B_HEAD, gw), 2) // B_HEAD
    upd = jnp.zeros_like(s)
    for h in range(RW_GROUP):
        upd = upd + jnp.where(lane_head == h, z[:, h * B_HEAD:(h + 1) * B_HEAD], 0.0)
    s_new = s * decay + upd
    lanes = lambda x, q: jnp.concatenate([x[q * ng + g] for g in range(ng)], axis=1)
    seqs = range(nseq)
    return ([lanes(y, q) for q in seqs], [lanes(y, nseq + q) for q in seqs],
            [lanes(s_new, q) for q in seqs], [lanes(s_new, nseq + q) for q in seqs])


def _rwscan_body(rf_ref, vf_ref, kkf_ref, bf_ref, kdf_ref, lwf_ref, rb_ref, vb_ref, kkb_ref, bb_ref, kdb_ref, lwb_ref,
                 s0f_ref, s0b_ref, yf_ref, yb_ref, sTf_ref, sTb_ref, sf_ref, sb_ref, *, nc, nseq):
    c = pl.program_id(0)

    @pl.when(c == 0)
    def _():
        sf_ref[...] = s0f_ref[...]
        sb_ref[...] = s0b_ref[...]

    seqs = range(nseq)
    yf, yb, sf, sb = _rw_chunks(
        [(rf_ref[q], vf_ref[q], kkf_ref[q], bf_ref[q], kdf_ref[q], lwf_ref[q]) for q in seqs],
        [(rb_ref[q], vb_ref[q], kkb_ref[q], bb_ref[q], kdb_ref[q], lwb_ref[q]) for q in seqs],
        [sf_ref[q] for q in seqs], [sb_ref[q] for q in seqs])
    for q in seqs:
        yf_ref[q] = yf[q]
        yb_ref[q] = yb[q]
        sf_ref[q] = sf[q]
        sb_ref[q] = sb[q]

    @pl.when(c == nc - 1)
    def _():
        for q in seqs:
            sTf_ref[q] = sf[q]
            sTb_ref[q] = sb[q]


def _rwkv_scan(r, v, kk, fwd, bwd, s0f, s0b):
    bsz, n, w = r.shape
    ck = RW_CHUNK
    assert n % ck == 0 and ck == B_HEAD and w % (RW_GROUP * B_HEAD) == 0
    nc = n // ck
    seq_f = pl.BlockSpec((bsz, ck, w), lambda c: (0, c, 0))
    seq_b = pl.BlockSpec((bsz, ck, w), lambda c: (0, nc - 1 - c, 0))
    st = pl.BlockSpec((bsz, B_HEAD, w), lambda c: (0, 0, 0))
    y_shape = jax.ShapeDtypeStruct((bsz, n, w), F32)
    s_shape = jax.ShapeDtypeStruct((bsz, B_HEAD, w), F32)
    (lwf, bf, kf), (lwb, bb, kb) = fwd, bwd
    return pl.pallas_call(
        functools.partial(_rwscan_body, nc=nc, nseq=bsz),
        grid=(nc,),
        in_specs=[seq_f] * 6 + [seq_b] * 6 + [st, st],
        out_specs=[seq_f, seq_b, st, st],
        out_shape=[y_shape, y_shape, s_shape, s_shape],
        scratch_shapes=[pltpu.VMEM((bsz, B_HEAD, w), F32), pltpu.VMEM((bsz, B_HEAD, w), F32)],
        compiler_params=_cp(("arbitrary",)),
        name="rwkv_scan",
    )(r, v, kk, bf, kf, lwf, r, v, kk, bb, kb, lwb, s0f, s0b)


def _rwpost_body(yf_ref, yb_ref, g_ref, bonus_ref, lg_ref, lb_ref, sel_ref, selt_ref, o_ref):
    sel, selt = sel_ref[...], selt_ref[...]
    head_mean = lambda z: _mm_sel(_mm_sel(z, sel), selt) * (1.0 / B_HEAD)
    y = yf_ref[0] + yb_ref[0]
    d = y - head_mean(y)
    y = d * lax.rsqrt(head_mean(d * d) + B_GN_EPS)
    y = y * lg_ref[...] + lb_ref[...] + bonus_ref[0]
    o_ref[0] = (y * g_ref[0].astype(F32)).astype(o_ref.dtype)


def _head_selectors(bw):
    sel_np = np.zeros((bw, LANE), np.float32)
    sel_np[np.arange(bw), np.arange(bw) // B_HEAD] = 1.0
    return jnp.asarray(sel_np, BF16), jnp.asarray(sel_np.T, BF16)


def _rwkv_post(y_f, y_b, g, bonus, lnx_g, lnx_b):
    bsz, n, bw = y_f.shape
    tm = _pick(n, 256)
    tok = pl.BlockSpec((1, tm, bw), lambda b, i: (b, i, 0))
    vec = pl.BlockSpec((1, bw), lambda b, i: (0, 0))
    sel, selt = _head_selectors(bw)
    return pl.pallas_call(
        _rwpost_body,
        grid=(bsz, n // tm),
        in_specs=[tok, tok, tok, tok, vec, vec,
                  pl.BlockSpec((bw, LANE), lambda b, i: (0, 0)), pl.BlockSpec((LANE, bw), lambda b, i: (0, 0))],
        out_specs=tok,
        out_shape=jax.ShapeDtypeStruct((bsz, n, bw), BF16),
        compiler_params=_cp(("parallel", "parallel")),
        name="rwkv_post",
    )(y_f, y_b, g, bonus, lnx_g.reshape(1, bw).astype(F32), lnx_b.reshape(1, bw).astype(F32), sel, selt)


def _rwkv_run(prep, states0, p):
    y_f, y_b, s_f, s_b = _rwkv_scan(prep["r"], prep["v"], prep["kk"], prep["f"], prep["b"], states0[0], states0[1])
    return _rwkv_post(y_f, y_b, prep["g"], prep["bonus"], p["lnx_g"], p["lnx_b"]), (s_f, s_b)


HY_N2 = 128
HY_HALO = 8


def _hypre_body(x_ref, hp_ref, hn_ref, cw_ref, cb_ref, x0_ref, uv_ref, *, nb, tm, cw):
    i = pl.program_id(1)
    x = x_ref[0].astype(F32)
    row = lax.broadcasted_iota(jnp.int32, x.shape, 0)
    prev_edge = jnp.where(i > 0, hp_ref[0, HY_HALO - 1:HY_HALO, :].astype(F32), 0.0)
    next_edge = jnp.where(i < nb - 1, hn_ref[0, 0:1, :].astype(F32), 0.0)
    prev = jnp.where(row == 0, prev_edge, pltpu.roll(x, 1, 0))
    nxt = jnp.where(row == tm - 1, next_edge, pltpu.roll(x, tm - 1, 0))
    u = prev * cw_ref[0:1] + x * cw_ref[1:2] + nxt * cw_ref[2:3] + cb_ref[...]
    x0_ref[0] = u[:, :cw]
    uv_ref[0] = u[:, 2 * cw:3 * cw] * u[:, cw:2 * cw]


def _hyena_pre(p, conv_w, conv_b):
    bsz, n, _ = p.shape
    hc = conv_w.shape[1]
    cw = hc // 3
    tm = _pick(n, 256)
    nb = n // tm
    hb = tm // HY_HALO
    cwp = jnp.concatenate([conv_w.astype(F32), jnp.zeros((5, hc), F32)], axis=0)
    out = pl.BlockSpec((1, tm, cw), lambda b, i: (b, i, 0))
    return pl.pallas_call(
        functools.partial(_hypre_body, nb=nb, tm=tm, cw=cw),
        grid=(bsz, nb),
        in_specs=[pl.BlockSpec((1, tm, hc), lambda b, i: (b, i, 0)),
                  pl.BlockSpec((1, HY_HALO, hc), lambda b, i: (b, jnp.maximum(i * hb - 1, 0), 0)),
                  pl.BlockSpec((1, HY_HALO, hc), lambda b, i: (b, jnp.minimum((i + 1) * hb, nb * hb - 1), 0)),
                  pl.BlockSpec((8, hc), lambda b, i: (0, 0)),
                  pl.BlockSpec((1, hc), lambda b, i: (0, 0))],
        out_specs=[out, out],
        out_shape=[jax.ShapeDtypeStruct((bsz, n, cw), F32)] * 2,
        compiler_params=_cp(("parallel", "parallel")),
        name="hyena_pre",
    )(p, p, p, cwp, conv_b.reshape(1, hc).astype(F32))


def _hyfilt_body(z_ref, t_ref, z0_ref, w1_ref, w2_ref, w3_ref, w4_ref, w4b_ref, vec_ref, dl_ref,
                 k_ref, s_ref, *, half_tiles):
    i = pl.program_id(0)
    b1, b2, b3, fr = vec_ref[0:1], vec_ref[1:2], vec_ref[2:3], vec_ref[3:4]

    def mlp3(z):
        h = jnp.sin(fr * (_mm_hi(z, w1_ref[...]) + b1))
        h = jnp.sin(fr * (_mm_hi(h, w2_ref[...]) + b2))
        return jnp.sin(fr * (_mm_hi(h, w3_ref[...]) + b3))

    raw = _mm_hi(mlp3(z_ref[...]), w4_ref[...]) * jnp.exp(-t_ref[...] * dl_ref[...])

    @pl.when(i == 0)
    def _():
        s_ref[...] = jnp.zeros_like(s_ref)

    s_ref[...] += jnp.sum(jnp.abs(raw), axis=0, keepdims=True)
    hb0 = _mm_hi(mlp3(z0_ref[...]), w4b_ref[...])
    row = lax.broadcasted_iota(jnp.int32, raw.shape, 0)
    raw = raw + jnp.where((row == 0) & (i == 0), hb0[0:1], 0.0)
    k_ref[...] = jnp.where((row == 0) & (i == half_tiles), 0.0, raw)


def _hyena_kernel_taps(n, p):
    cw = p["f_w4"].shape[1] // 2
    f32 = lambda a: a.astype(F32)
    row = jnp.arange(2 * n, dtype=jnp.int32)
    pos = jnp.where(row < n, row, (2 * n - row) % n).astype(F32)[:, None]
    tt = pos * jnp.float32(1.0 / (n - 1))
    bands = (HY_EMB - 1) // 2
    ang = (2 * math.pi / n) * pos * jnp.linspace(1e-4, bands - 1, bands, dtype=F32)[None]
    zz = jnp.concatenate([tt, jnp.cos(ang), -jnp.sin(ang), jnp.zeros((2 * n, HY_ORDER - HY_EMB), F32)], axis=-1)
    w1 = jnp.concatenate([f32(p["f_w1"]), jnp.zeros((HY_ORDER - HY_EMB, HY_ORDER), F32)], axis=0)
    vecs = jnp.stack([f32(p["f_b1"]), f32(p["f_b2"]), f32(p["f_b3"]), f32(p["f_freq"])]
                     + [jnp.zeros((HY_ORDER,), F32)] * 4)
    deltas = jnp.abs(jnp.linspace(math.log(HY_TARGET) / HY_SLOW_PCT, math.log(HY_TARGET) / HY_FAST_PCT, cw,
                                  dtype=F32)).reshape(1, cw)
    tr = _pick(n, 512)
    half_tiles = n // tr
    full = lambda shape: pl.BlockSpec(shape, lambda i: tuple(0 for _ in shape))
    w4 = f32(p["f_w4"])
    return pl.pallas_call(
        functools.partial(_hyfilt_body, half_tiles=half_tiles),
        grid=(2 * half_tiles,),
        in_specs=[pl.BlockSpec((tr, HY_ORDER), lambda i: (i, 0)),
                  pl.BlockSpec((tr, 1), lambda i: (i, 0)),
                  full((8, HY_ORDER)), full((HY_ORDER, HY_ORDER)), full((HY_ORDER, HY_ORDER)),
                  full((HY_ORDER, HY_ORDER)),
                  pl.BlockSpec((HY_ORDER, cw), lambda i: (0, i // half_tiles)),
                  pl.BlockSpec((HY_ORDER, cw), lambda i: (0, 1)),
                  full((8, HY_ORDER)), full((1, cw))],
        out_specs=[pl.BlockSpec((tr, cw), lambda i: (i, 0)), pl.BlockSpec((1, cw), lambda i: (0, 0))],
        out_shape=[jax.ShapeDtypeStruct((2 * n, cw), F32), jax.ShapeDtypeStruct((1, cw), F32)],
        compiler_params=_cp(("arbitrary",)),
        name="hyena_filter",
    )(zz, tt, jnp.broadcast_to(zz[0:1], (8, HY_ORDER)), w1, f32(p["f_w2"]), f32(p["f_w3"]), w4, w4, vecs, deltas)


def _dft_consts(n):
    nn = 2 * n
    n2 = HY_N2
    n1 = nn // n2
    n1h = n1 // 2 + 1
    nf = -(-n1h // 8) * 8
    live = (np.arange(nf) < n1h).astype(np.float64)
    a1 = 2 * np.pi * np.outer(np.arange(nf), np.arange(n1)) / n1
    f1 = np.concatenate([np.cos(a1), -np.sin(a1)], axis=0) * np.tile(live, 2)[:, None]
    wgt = live * np.where((np.arange(nf) == 0) | (np.arange(nf) == n1 // 2), 1.0, 2.0)
    f1inv = np.concatenate([np.cos(a1.T) * wgt, -np.sin(a1.T) * wgt], axis=1) / nn
    a2 = 2 * np.pi * np.outer(np.arange(n2), np.arange(n2)) / n2
    c2, s2 = np.cos(a2), -np.sin(a2)
    m2 = np.block([[c2, -s2], [s2, c2]])
    m2inv = np.block([[c2, s2], [-s2, c2]])
    at = 2 * np.pi * np.outer(np.arange(n2), np.arange(nf)) / nn
    tw = np.stack([np.cos(at), -np.sin(at)])
    c = lambda a: jnp.asarray(a.astype(np.float32))
    return dict(n1=n1, n2=n2, nf=nf, f1=c(f1), f1inv=c(f1inv), m2=c(m2), m2inv=c(m2inv),
                tw_s2=c(tw[:, :, :, None]),
                tw_f1=c(np.transpose(tw, (0, 2, 1))[:, :, :, None]))


def _dft1_body(f_ref, x_ref, tw_ref, o_ref, *, nf):
    a = _mm_x3(f_ref[...], x_ref[0])
    ar, ai = a[:nf], a[nf:]
    twr, twi = tw_ref[0, 0], tw_ref[1, 0]
    o_ref[0, 0] = ar * twr - ai * twi
    o_ref[0, 1] = ar * twi + ai * twr


def _dft_stage1(x, consts, rows):
    bsz, _, cw = x.shape
    n2, nf = consts["n2"], consts["nf"]
    xv = x.reshape(bsz, rows, n2 * cw)
    f1 = consts["f1"][:, :rows]
    return pl.pallas_call(
        functools.partial(_dft1_body, nf=nf),
        grid=(bsz, n2),
        in_specs=[pl.BlockSpec((2 * nf, rows), lambda b, j: (0, 0)),
                  pl.BlockSpec((1, rows, cw), lambda b, j: (b, 0, j)),
                  pl.BlockSpec((2, 1, nf, 1), lambda b, j: (0, j, 0, 0))],
        out_specs=pl.BlockSpec((1, 2, nf, cw), lambda b, j: (b, 0, 0, j)),
        out_shape=jax.ShapeDtypeStruct((bsz, 2, nf, n2 * cw), F32),
        compiler_params=_cp(("parallel", "parallel")),
        name="hyena_dft1",
    )(f1, xv, consts["tw_s2"])


def _dftmid_body(a_ref, m2_ref, m2i_ref, h_ref, tw_ref, is_ref, o_ref, *, n2, filt):
    a = a_ref[0, :, 0].reshape(2 * n2, a_ref.shape[-1])
    x = _mm_x3(m2_ref[...], a)
    xr, xi = x[:n2], x[n2:]
    if filt:
        o_ref[0, 0, 0] = xr * is_ref[...]
        o_ref[0, 1, 0] = xi * is_ref[...]
        return
    hr, hi = h_ref[0, 0, 0], h_ref[0, 1, 0]
    y = jnp.concatenate([xr * hr - xi * hi, xr * hi + xi * hr], axis=0)
    z = _mm_x3(m2i_ref[...], y)
    zr, zi = z[:n2], z[n2:]
    twr, twi = tw_ref[0, 0], tw_ref[1, 0]
    o_ref[0, 0, 0] = zr * twr + zi * twi
    o_ref[0, 1, 0] = zi * twr - zr * twi


def _dft_mid(a, h, inv_s, consts, filt):
    bsz = a.shape[0]
    n1, n2 = consts["nf"], consts["n2"]
    cw = a.shape[-1] // n2
    av = a.reshape(bsz, 2, n1, n2, cw)
    blk = pl.BlockSpec((1, 2, 1, n2, cw), lambda b, f: (b, 0, f, 0, 0))
    hblk = pl.BlockSpec((1, 2, 1, n2, cw), (lambda b, f: (0, 0, 0, 0, 0)) if filt else (lambda b, f: (0, 0, f, 0, 0)))
    return pl.pallas_call(
        functools.partial(_dftmid_body, n2=n2, filt=filt),
        grid=(bsz, n1),
        in_specs=[blk,
                  pl.BlockSpec((2 * n2, 2 * n2), lambda b, f: (0, 0)),
                  pl.BlockSpec((2 * n2, 2 * n2), lambda b, f: (0, 0)),
                  hblk,
                  pl.BlockSpec((2, 1, n2, 1), lambda b, f: (0, f, 0, 0)),
                  pl.BlockSpec((1, cw), lambda b, f: (0, 0))],
        out_specs=blk,
        out_shape=jax.ShapeDtypeStruct((bsz, 2, n1, n2, cw), F32),
        compiler_params=_cp(("parallel", "parallel")),
        name="hyena_dftmid_" + ("filter" if filt else "conv"),
    )(av, consts["m2"], consts["m2inv"], h, consts["tw_f1"], inv_s)


def _dft3_body(f_ref, z_ref, x0_ref, uv_ref, bias_ref, o_ref):
    y = _mm_x3(f_ref[...], z_ref[0])
    o_ref[0] = (x0_ref[0] * (y + uv_ref[0] * bias_ref[...])).astype(o_ref.dtype)


def _dft_stage1_inv(z, x0, uv, bias, consts):
    bsz, n, cw = x0.shape
    n1, n2 = consts["nf"], consts["n2"]
    hr = consts["n1"] // 2
    zv = z.reshape(bsz, 2 * n1, n2 * cw)
    tokv = lambda t: t.reshape(bsz, hr, n2 * cw)
    tile = pl.BlockSpec((1, hr, cw), lambda b, j: (b, 0, j))
    out = pl.pallas_call(
        _dft3_body,
        grid=(bsz, n2),
        in_specs=[pl.BlockSpec((hr, 2 * n1), lambda b, j: (0, 0)),
                  pl.BlockSpec((1, 2 * n1, cw), lambda b, j: (b, 0, j)),
                  tile, tile,
                  pl.BlockSpec((1, cw), lambda b, j: (0, 0))],
        out_specs=tile,
        out_shape=jax.ShapeDtypeStruct((bsz, hr, n2 * cw), BF16),
        compiler_params=_cp(("parallel", "parallel")),
        name="hyena_dft3",
    )(consts["f1inv"][:hr], zv, tokv(x0), tokv(uv), bias.reshape(1, cw).astype(F32))
    return out.reshape(bsz, n, cw)


def _hyena(p_lat, p):
    n = p_lat.shape[1]
    consts = _dft_consts(n)
    n1 = consts["n1"]
    x0, uv = _hyena_pre(p_lat, p["conv_w"], p["conv_b"])
    taps, sabs = _hyena_kernel_taps(n, p)
    cw = taps.shape[1]
    ones = jnp.ones((1, cw), F32)
    hk = _dft_stage1(taps[None], consts, n1)
    hspec = _dft_mid(hk, jnp.zeros((1, 2, 1, HY_N2, cw), F32), 1.0 / sabs, consts, True)
    a = _dft_stage1(uv, consts, n1 // 2)
    z = _dft_mid(a, hspec, ones, consts, False)
    return _dft_stage1_inv(z, x0, uv, p["hy_bias"], consts)


def _mod_vectors(c, c_ctx, p):
    bsz, d = c.shape
    cvec = jnp.concatenate([c, c_ctx[None], jnp.zeros((8 - bsz - 1, d), c.dtype)], axis=0).astype(F32)
    m = _mods(cvec, p["w_mod"].astype(F32), p["b_mod"].astype(F32))
    lat = [v[:, None, :] for v in jnp.split(m[:bsz], N_MOD, axis=-1)]
    cx = [jnp.broadcast_to(v[:, None, :], (bsz, 1, d)) for v in jnp.split(m[bsz:bsz + 1], N_MOD, axis=-1)]
    return lat, cx


def _affine(norm_g, shift, scale):
    return norm_g.astype(F32) * (1.0 + scale), shift


def kernel(x, c, ctx, c_ctx,
           l0_w_mod, l0_b_mod, l0_norm1_g, l0_norm2_g, l0_w_in, l0_lam_q1, l0_lam_k1, l0_lam_q2, l0_lam_k2,
           l0_subln_g, l0_mu_prev, l0_mu_next, l0_w0_f, l0_w2_f, l0_a0_f, l0_a2_f, l0_w0_b, l0_w2_b, l0_a0_b,
           l0_a2_b, l0_g2, l0_k_k, l0_k_a, l0_r_k, l0_lnx_g, l0_lnx_b, l0_w_out, l0_mlp_w1, l0_mlp_w2,
           l1_w_mod, l1_b_mod, l1_norm1_g, l1_norm2_g, l1_w_in, l1_conv_w, l1_conv_b, l1_f_w1, l1_f_b1,
           l1_f_w2, l1_f_b2, l1_f_w3, l1_f_b3, l1_f_w4, l1_f_freq, l1_hy_bias, l1_q_norm_g, l1_k_norm_g,
           l1_w_out, l1_mlp_w1, l1_mlp_w2, final_g):
    bsz, n, d = x.shape
    x = x.astype(F32)
    ctx_s = ctx.astype(F32)
    bf = lambda w: w.astype(BF16)

    p0 = dict(w_mod=l0_w_mod, b_mod=l0_b_mod, mu_prev=l0_mu_prev, mu_next=l0_mu_next, w0_f=l0_w0_f, w2_f=l0_w2_f,
              a0_f=l0_a0_f, a2_f=l0_a2_f, w0_b=l0_w0_b, w2_b=l0_w2_b, a0_b=l0_a0_b, a2_b=l0_a2_b, g2=l0_g2,
              k_k=l0_k_k, k_a=l0_k_a, r_k=l0_r_k, lnx_g=l0_lnx_g, lnx_b=l0_lnx_b)
    (sh1, sc1, g1, sh2, sc2, g2), (csh1, csc1, cg1, csh2, csc2, cg2) = _mod_vectors(c, c_ctx, p0)
    aw = d // 2
    w_rw = l0_w_in[:, 3 * aw:]
    rw_cols = -(-(w_rw.shape[1] + 2 * LANE - B_GATE_RANK) // 512) * 512
    w_in0 = bf(jnp.concatenate([w_rw, jnp.zeros((d, rw_cols - w_rw.shape[1]), w_rw.dtype), l0_w_in[:, :3 * aw]],
                               axis=1))
    a1, b1 = _affine(l0_norm1_g, sh1, sc1)
    ca1, cb1 = _affine(l0_norm1_g, csh1, csc1)
    p_lat0 = _inproj(x, a1, b1, w_in0)
    p_ctx0 = _inproj(ctx_s, ca1, cb1, w_in0)
    lam_init = 0.8 - 0.6 * math.exp(-0.3 * 0)
    lam = (jnp.exp(jnp.sum(l0_lam_q1 * l0_lam_k1).astype(F32))
           - jnp.exp(jnp.sum(l0_lam_q2 * l0_lam_k2).astype(F32)) + lam_init)
    oa_lat, oa_ctx = _diff_attention(p_lat0, p_ctx0, rw_cols, aw, lam, lam_init, l0_subln_g)
    nh, hd = l0_r_k.shape
    zero = jnp.zeros((bsz, hd, nh * hd), F32)
    y_ctx, states_c = _rwkv_run(_rwkv_prepare(p_ctx0, p0, rw_cols), (zero, zero), p0)
    y_lat, _ = _rwkv_run(_rwkv_prepare(p_lat0, p0, rw_cols), states_c, p0)
    w_out0 = bf(l0_w_out)
    x = _outproj(oa_lat, y_lat, w_out0, x, g1)
    ctx_s = _outproj(oa_ctx, y_ctx, w_out0, ctx_s, cg1)
    w1, w2 = bf(l0_mlp_w1), bf(l0_mlp_w2)
    a2, b2 = _affine(l0_norm2_g, sh2, sc2)
    ca2, cb2 = _affine(l0_norm2_g, csh2, csc2)
    x = _mlp(x, a2, b2, g2, w1, w2)
    ctx_s = _mlp(ctx_s, ca2, cb2, cg2, w1, w2)

    p1 = dict(w_mod=l1_w_mod, b_mod=l1_b_mod, conv_w=l1_conv_w, conv_b=l1_conv_b, f_w1=l1_f_w1, f_b1=l1_f_b1,
              f_w2=l1_f_w2, f_b2=l1_f_b2, f_w3=l1_f_w3, f_b3=l1_f_b3, f_w4=l1_f_w4, f_freq=l1_f_freq,
              hy_bias=l1_hy_bias)
    (sh1, sc1, g1, sh2, sc2, g2), (csh1, csc1, _, _, _, _) = _mod_vectors(c, c_ctx, p1)
    hy_cols = l1_conv_w.shape[1]
    kv_cols = 2 * (2 * D_HEAD)
    w_in1 = bf(l1_w_in)
    a1, b1 = _affine(l1_norm1_g, sh1, sc1)
    ca1, cb1 = _affine(l1_norm1_g, csh1, csc1)
    p_lat = _inproj(x, a1, b1, w_in1)
    pkv_ctx = _inproj(ctx_s, ca1, cb1, w_in1[:, -kv_cols:])
    o_hy = _hyena(p_lat, p1)
    o_at = _gqa_attention(p_lat, pkv_ctx, hy_cols, l1_q_norm_g, l1_k_norm_g)
    x = _outproj(o_hy, o_at, bf(l1_w_out), x, g1)
    a2, b2 = _affine(l1_norm2_g, sh2, sc2)
    return _mlp(x, a2, b2, g2, bf(l1_mlp_w1), bf(l1_mlp_w2), final_g=final_g)
```

```python
import functools
import math

import numpy as np
import jax
import jax.numpy as jnp
from jax import lax
from jax.experimental import pallas as pl
from jax.experimental.pallas import tpu as pltpu

F32 = jnp.float32
BF16 = jnp.bfloat16
HI = lax.Precision.HIGHEST

NORM_EPS = 1e-6
ROPE_THETA = 10000.0
GRID_W = 64
N_MOD = 6
A_QK_DIM = 64
A_V_DIM = 128
B_HEAD = 64
B_DECAY_RANK = 64
B_ICL_RANK = 64
B_GATE_RANK = 160
B_GN_EPS = 64e-5
HY_EMB = 33
HY_ORDER = 64
HY_TARGET = 1e-2
HY_FAST_PCT = 0.3
HY_SLOW_PCT = 1.5
D_HEAD = 128
D_KV_GROUP = 4

LANE = 128
VMEM_LIMIT = 56 * 1024 * 1024
LOG2E = 1.4426950408889634


def _cp(sem, vmem=VMEM_LIMIT):
    return pltpu.CompilerParams(dimension_semantics=sem, vmem_limit_bytes=vmem)


def _pick(n, pref, step=8):
    t = max(step, min(n, pref) // step * step)
    while n % t:
        t -= step
    return t


def _mods_body(c_ref, w_ref, b_ref, o_ref):
    c = c_ref[...]
    s = c * jax.nn.sigmoid(c)
    o_ref[...] = jnp.dot(s, w_ref[...], preferred_element_type=F32, precision=HI) + b_ref[...]


def _mods(cvec, w_mod, b_mod):
    m, d = cvec.shape
    n = w_mod.shape[1]
    tn = _pick(n, 1024, LANE)
    return pl.pallas_call(
        _mods_body,
        grid=(n // tn,),
        in_specs=[pl.BlockSpec((m, d), lambda j: (0, 0)),
                  pl.BlockSpec((d, tn), lambda j: (0, j)),
                  pl.BlockSpec((1, tn), lambda j: (0, j))],
        out_specs=pl.BlockSpec((m, tn), lambda j: (0, j)),
        out_shape=jax.ShapeDtypeStruct((m, n), F32),
        compiler_params=_cp(("arbitrary",)),
        name="mods",
    )(cvec, w_mod, b_mod.reshape(1, n))


def _norm_mod(x, a, b):
    ms = jnp.mean(x * x, axis=-1, keepdims=True)
    return x * lax.rsqrt(ms + NORM_EPS) * a + b


def _inproj_body(x_ref, a_ref, b_ref, w_ref, o_ref, xn_ref):
    @pl.when(pl.program_id(2) == 0)
    def _():
        xn_ref[...] = _norm_mod(x_ref[0], a_ref[0], b_ref[0]).astype(BF16)

    o_ref[0] = jnp.dot(xn_ref[...], w_ref[...], preferred_element_type=F32).astype(o_ref.dtype)


def _inproj(x, a, b, w, out_dtype=BF16, tm_pref=1024, tn_pref=512):
    bsz, n, d = x.shape
    nn = w.shape[1]
    tm, tn = _pick(n, tm_pref), _pick(nn, tn_pref, LANE)
    return pl.pallas_call(
        _inproj_body,
        grid=(bsz, n // tm, nn // tn),
        in_specs=[pl.BlockSpec((1, tm, d), lambda bi, i, j: (bi, i, 0)),
                  pl.BlockSpec((1, 1, d), lambda bi, i, j: (bi, 0, 0)),
                  pl.BlockSpec((1, 1, d), lambda bi, i, j: (bi, 0, 0)),
                  pl.BlockSpec((d, tn), lambda bi, i, j: (0, j))],
        out_specs=pl.BlockSpec((1, tm, tn), lambda bi, i, j: (bi, i, j)),
        out_shape=jax.ShapeDtypeStruct((bsz, n, nn), out_dtype),
        scratch_shapes=[pltpu.VMEM((tm, d), BF16)],
        compiler_params=_cp(("parallel", "parallel", "arbitrary")),
        name="inproj",
    )(x, a, b, w)


def _outproj_body(oa_ref, ob_ref, wa_ref, wb_ref, x_ref, g_ref, y_ref):
    acc = jnp.dot(oa_ref[0].astype(BF16), wa_ref[...], preferred_element_type=F32)
    acc += jnp.dot(ob_ref[0].astype(BF16), wb_ref[...], preferred_element_type=F32)
    y_ref[0] = x_ref[0] + g_ref[0] * acc


def _outproj(oa, ob, w, x, g, tm_pref=1024, tn_pref=512):
    bsz, n, ka = oa.shape
    kb = ob.shape[2]
    d = w.shape[1]
    tm, tn = _pick(n, tm_pref), _pick(d, tn_pref, LANE)
    return pl.pallas_call(
        _outproj_body,
        grid=(bsz, n // tm, d // tn),
        in_specs=[pl.BlockSpec((1, tm, ka), lambda bi, i, j: (bi, i, 0)),
                  pl.BlockSpec((1, tm, kb), lambda bi, i, j: (bi, i, 0)),
                  pl.BlockSpec((ka, tn), lambda bi, i, j: (0, j)),
                  pl.BlockSpec((kb, tn), lambda bi, i, j: (0, j)),
                  pl.BlockSpec((1, tm, tn), lambda bi, i, j: (bi, i, j)),
                  pl.BlockSpec((1, 1, tn), lambda bi, i, j: (bi, 0, j))],
        out_specs=pl.BlockSpec((1, tm, tn), lambda bi, i, j: (bi, i, j)),
        out_shape=jax.ShapeDtypeStruct((bsz, n, d), F32),
        compiler_params=_cp(("parallel", "parallel", "parallel")),
        name="outproj",
    )(oa, ob, w[:ka], w[ka:], x, g)


def _mlp_body(x_ref, a_ref, b_ref, g_ref, w1_ref, w2_ref, fg_ref, y_ref, xn_ref, acc_ref, *, nf, final):
    f = pl.program_id(2)

    @pl.when(f == 0)
    def _():
        xn_ref[...] = _norm_mod(x_ref[0], a_ref[0], b_ref[0]).astype(BF16)
        acc_ref[...] = jnp.zeros_like(acc_ref)

    h = jnp.dot(xn_ref[...], w1_ref[...], preferred_element_type=F32)
    h = jnp.square(jnp.maximum(h, 0.0)).astype(BF16)
    acc_ref[...] += jnp.dot(h, w2_ref[...], preferred_element_type=F32)

    @pl.when(f == nf - 1)
    def _():
        y = x_ref[0] + g_ref[0] * acc_ref[...]
        if final:
            ms = jnp.mean(y * y, axis=-1, keepdims=True)
            y = y * lax.rsqrt(ms + NORM_EPS) * fg_ref[...]
        y_ref[0] = y


def _mlp(x, a, b, g, w1, w2, final_g=None, tm_pref=512, tf_pref=1024):
    bsz, n, d = x.shape
    dff = w1.shape[1]
    tm, tf = _pick(n, tm_pref), _pick(dff, tf_pref, LANE)
    nf = dff // tf
    final = final_g is not None
    fg = (final_g if final else jnp.ones((d,), F32)).reshape(1, d).astype(F32)
    return pl.pallas_call(
        functools.partial(_mlp_body, nf=nf, final=final),
        grid=(bsz, n // tm, nf),
        in_specs=[pl.BlockSpec((1, tm, d), lambda bi, i, f: (bi, i, 0)),
                  pl.BlockSpec((1, 1, d), lambda bi, i, f: (bi, 0, 0)),
                  pl.BlockSpec((1, 1, d), lambda bi, i, f: (bi, 0, 0)),
                  pl.BlockSpec((1, 1, d), lambda bi, i, f: (bi, 0, 0)),
                  pl.BlockSpec((d, tf), lambda bi, i, f: (0, f)),
                  pl.BlockSpec((tf, d), lambda bi, i, f: (f, 0)),
                  pl.BlockSpec((1, d), lambda bi, i, f: (0, 0))],
        out_specs=pl.BlockSpec((1, tm, d), lambda bi, i, f: (bi, i, 0)),
        out_shape=jax.ShapeDtypeStruct((bsz, n, d), F32),
        scratch_shapes=[pltpu.VMEM((tm, d), BF16), pltpu.VMEM((tm, d), F32)],
        compiler_params=_cp(("parallel", "parallel", "arbitrary")),
        name="mlp",
    )(x, a, b, g, w1, w2, fg)


def _rope_tables(n, dim, reps):
    rows = n // GRID_W
    row = jnp.repeat(jnp.arange(rows, dtype=F32), GRID_W)
    col = jnp.tile(jnp.arange(GRID_W, dtype=F32), rows)
    half = dim // 2
    inv = ROPE_THETA ** (-jnp.arange(0, half, 2, dtype=F32) / half)
    ar, ac = row[:, None] * inv, col[:, None] * inv
    cos = jnp.concatenate([jnp.cos(ar), jnp.cos(ar), jnp.cos(ac), jnp.cos(ac)], axis=-1)
    sin = jnp.concatenate([-jnp.sin(ar), jnp.sin(ar), -jnp.sin(ac), jnp.sin(ac)], axis=-1)
    return jnp.tile(cos, (1, reps)), jnp.tile(sin, (1, reps))


def _swap_matrix(dim, reps):
    q = dim // 4
    width = dim * reps
    p = np.zeros((width, width), np.float32)
    for j in range(width):
        base, r = (j // dim) * dim, j % dim
        axis, which, f = r // (2 * q), (r % (2 * q)) // q, r % q
        p[base + axis * 2 * q + (1 - which) * q + f, j] = 1.0
    return p


def _rope_norm(x, cos_ref, sin_ref, p_ref, g_ref, gs_ref, *, norm, rope, scale=1.0):
    y = x.astype(F32)
    if rope:
        ys = jnp.dot(x, p_ref[...], preferred_element_type=F32)
    if norm:
        rs = lax.rsqrt(jnp.mean(y * y, axis=-1, keepdims=True) + NORM_EPS)
        y = y * rs * g_ref[...]
        if rope:
            ys = ys * rs * gs_ref[...]
    if rope:
        y = y * cos_ref[...] + ys * sin_ref[...]
    return y * scale if scale != 1.0 else y


def _rope_operands(n, dim, gain, rope):
    reps = LANE // dim
    if rope:
        cos, sin = _rope_tables(n, dim, reps)
        pm = jnp.asarray(_swap_matrix(dim, reps), BF16)
    else:
        cos = sin = jnp.zeros((n, LANE), F32)
        pm = jnp.zeros((LANE, LANE), BF16)
    if gain is not None:
        g = jnp.tile(gain.astype(F32), reps).reshape(1, LANE)
        gs = jnp.dot(g, jnp.asarray(_swap_matrix(dim, reps)), precision=HI)
    else:
        g = gs = jnp.ones((1, LANE), F32)
    return cos, sin, pm, g, gs


def _kprep_body(x_ref, cos_ref, sin_ref, p_ref, g_ref, gs_ref, o_ref, *, heads, norm, rope):
    for h in range(heads):
        sl = slice(h * LANE, (h + 1) * LANE)
        y = _rope_norm(x_ref[0, :, sl], cos_ref, sin_ref, p_ref, g_ref, gs_ref, norm=norm, rope=rope)
        o_ref[0, :, sl] = y.astype(o_ref.dtype)


def _kprep(p, col0, heads, *, dim, gain=None, rope=True, tq_pref=512):
    bsz, n, _ = p.shape
    tq = _pick(n, tq_pref)
    hb = heads
    while col0 % (hb * LANE):
        hb //= 2
    w = hb * LANE
    cb = col0 // w
    cos, sin, pm, g, gs = _rope_operands(n, dim, gain, rope)
    return pl.pallas_call(
        functools.partial(_kprep_body, heads=hb, norm=gain is not None, rope=rope),
        grid=(bsz, n // tq, heads // hb),
        in_specs=[pl.BlockSpec((1, tq, w), lambda b, i, j: (b, i, cb + j)),
                  pl.BlockSpec((tq, LANE), lambda b, i, j: (i, 0)),
                  pl.BlockSpec((tq, LANE), lambda b, i, j: (i, 0)),
                  pl.BlockSpec((LANE, LANE), lambda b, i, j: (0, 0)),
                  pl.BlockSpec((1, LANE), lambda b, i, j: (0, 0)),
                  pl.BlockSpec((1, LANE), lambda b, i, j: (0, 0))],
        out_specs=pl.BlockSpec((1, tq, w), lambda b, i, j: (b, i, j)),
        out_shape=jax.ShapeDtypeStruct((bsz, n, heads * LANE), BF16),
        compiler_params=_cp(("parallel", "parallel", "parallel")),
        name="kprep",
    )(p, cos, sin, pm, g, gs)


FLASH_KEYS = 1280
FLASH_HEADROOM = 64.0


def _knorm_body(k_ref, o_ref):
    k = k_ref[0].astype(F32)
    nrm2 = jnp.max(jnp.sum(k * k, axis=1, keepdims=True), axis=0, keepdims=True)
    o_ref[0, 0] = jnp.broadcast_to(jnp.sqrt(nrm2), (8, LANE))


def _key_block_norms(k, kcol0, heads, ts):
    bsz, nk, _ = k.shape
    nsub = nk // ts
    kb = kcol0 // LANE
    out = pl.pallas_call(
        _knorm_body,
        grid=(bsz, heads, nsub),
        in_specs=[pl.BlockSpec((1, ts, LANE), lambda b, h, c: (b, c, kb + h))],
        out_specs=pl.BlockSpec((1, 1, 8, LANE), lambda b, h, c: (b, h, c, 0)),
        out_shape=jax.ShapeDtypeStruct((bsz, heads, nsub * 8, LANE), F32),
        compiler_params=_cp(("parallel", "parallel", "parallel")),
        name="key_block_norms",
    )(k)
    return out[:, :, ::8, 0].reshape(-1)


def _flash_body(kmax_ref, qr_ref, cos_ref, sin_ref, pm_ref, g_ref, gs_ref, k_ref, v_ref, e1_ref, e2_ref, o_ref,
                q_ref, m_ref, l_ref, acc_ref, thr_ref, *, ts, nsub, mode, tq, norm, rope, scale):
    base = (pl.program_id(0) * pl.num_programs(1) + pl.program_id(1)) * nsub
    prep = functools.partial(_rope_norm, cos_ref=cos_ref, sin_ref=sin_ref, p_ref=pm_ref, g_ref=g_ref, gs_ref=gs_ref,
                             norm=norm, rope=rope, scale=scale)
    if mode == "diff":
        y = prep(qr_ref[0])
        lane = lax.broadcasted_iota(jnp.int32, y.shape, 1)
        q_ref[:tq] = jnp.where(lane < A_QK_DIM, y, 0.0).astype(BF16)
        q_ref[tq:] = jnp.where(lane >= A_QK_DIM, y, 0.0).astype(BF16)
    else:
        for j in range(D_KV_GROUP):
            q_ref[j * tq:(j + 1) * tq] = prep(qr_ref[0, :, j * LANE:(j + 1) * LANE]).astype(BF16)
    q = q_ref[...]

    def scores(off, size):
        k = k_ref[0, pl.ds(off, size), :]
        return lax.dot_general(q, k, (((1,), (1,)), ((), ())), preferred_element_type=F32)

    def block(c):
        off = pl.multiple_of(c * ts, ts)
        return scores(off, ts), v_ref[0, pl.ds(off, ts), :]

    def wide(m):
        return jnp.concatenate([m] * (ts // LANE), axis=1)

    m0 = jnp.broadcast_to(jnp.max(scores(0, 2 * LANE), axis=1, keepdims=True), m_ref.shape)
    m_ref[...] = m0
    l_ref[...] = jnp.zeros_like(l_ref)
    acc_ref[...] = jnp.zeros_like(acc_ref)
    qf = q.astype(F32)
    qn = jnp.sqrt(jnp.sum(qf * qf, axis=1, keepdims=True))
    thr_ref[0] = jnp.min((m0[:, :1] + FLASH_HEADROOM) / qn)

    def step(c, carry):
        fixed = kmax_ref[base + c] * 1.001 <= thr_ref[0]

        @pl.when(fixed)
        def _():
            s, v = block(c)
            p = jnp.exp2(s - wide(m_ref[...]))
            l_ref[...] += jnp.sum(p, axis=1, keepdims=True)
            acc_ref[...] += jnp.dot(p.astype(BF16), v, preferred_element_type=F32)

        @pl.when(jnp.logical_not(fixed))
        def _():
            s, v = block(c)
            m_prev = m_ref[...]
            m_new = jnp.maximum(m_prev, jnp.max(s, axis=1, keepdims=True))
            alpha = jnp.exp2(m_prev - m_new)
            p = jnp.exp2(s - wide(m_new))
            l_ref[...] = alpha * l_ref[...] + jnp.sum(p, axis=1, keepdims=True)
            acc_ref[...] = alpha * acc_ref[...] + jnp.dot(p.astype(BF16), v, preferred_element_type=F32)
            m_ref[...] = m_new

        return carry

    lax.fori_loop(0, nsub, step, 0)

    o = acc_ref[...] / l_ref[...]
    if mode == "diff":
        d = o[:tq] - e1_ref[...] * o[tq:]
        ms = jnp.mean(d * d, axis=-1, keepdims=True)
        o_ref[0] = (d * lax.rsqrt(ms + NORM_EPS) * e2_ref[...]).astype(o_ref.dtype)
    else:
        for j in range(D_KV_GROUP):
            o_ref[0, :, j * LANE:(j + 1) * LANE] = o[j * tq:(j + 1) * tq].astype(o_ref.dtype)


def _flash(pq, qcol0, k, v, kcol0, vcol0, e1, e2, *, mode, hk, dim, tq, gain=None, rope=True):
    bsz, n, _ = pq.shape
    nk = k.shape[1]
    g = 2 if mode == "diff" else D_KV_GROUP
    mq = g * tq
    qw = LANE if mode == "diff" else D_KV_GROUP * LANE
    qb = qcol0 // qw
    ts = _pick(nk, FLASH_KEYS, 2 * LANE)
    kb, vb = kcol0 // LANE, vcol0 // LANE
    ow = LANE if mode == "diff" else D_KV_GROUP * LANE
    kmax = _key_block_norms(k, kcol0, hk, ts)
    cos, sin, pm, gq, gqs = _rope_operands(n, dim, gain, rope)
    const = lambda shape: pl.BlockSpec(shape, lambda b, h, i: (0, 0))
    return pl.pallas_call(
        functools.partial(_flash_body, ts=ts, nsub=nk // ts, mode=mode, tq=tq, norm=gain is not None, rope=rope,
                          scale=dim ** -0.5 * LOG2E),
        grid=(bsz, hk, n // tq),
        in_specs=[pl.BlockSpec(memory_space=pltpu.SMEM),
                  pl.BlockSpec((1, tq, qw), lambda b, h, i: (b, i, qb + h)),
                  pl.BlockSpec((tq, LANE), lambda b, h, i: (i, 0)),
                  pl.BlockSpec((tq, LANE), lambda b, h, i: (i, 0)),
                  const((LANE, LANE)), const((1, LANE)), const((1, LANE)),
                  pl.BlockSpec((1, nk, LANE), lambda b, h, i: (b, 0, kb + h)),
                  pl.BlockSpec((1, nk, LANE), lambda b, h, i: (b, 0, vb + h)),
                  const((1, LANE)), const((1, LANE))],
        out_specs=pl.BlockSpec((1, tq, ow), lambda b, h, i: (b, i, h)),
        out_shape=jax.ShapeDtypeStruct((bsz, n, hk * ow), BF16),
        scratch_shapes=[pltpu.VMEM((mq, LANE), BF16), pltpu.VMEM((mq, LANE), F32), pltpu.VMEM((mq, LANE), F32),
                        pltpu.VMEM((mq, LANE), F32), pltpu.SMEM((1,), F32)],
        compiler_params=_cp(("parallel", "parallel", "arbitrary")),
        name="flash_" + mode,
    )(kmax, pq, cos, sin, pm, gq, gqs, k, v, e1, e2)


def _diff_attention(pa_lat, pa_ctx, c0, aw, lam, lam_init, subln_g):
    n, nctx = pa_lat.shape[1], pa_ctx.shape[1]
    heads = aw // LANE
    k_l = _kprep(pa_lat, c0 + aw, heads, dim=A_QK_DIM)
    k_all = jnp.concatenate([k_l, pa_ctx[..., c0 + aw:c0 + 2 * aw]], axis=1)
    v_all = jnp.concatenate([pa_lat[..., c0 + 2 * aw:c0 + 3 * aw], pa_ctx[..., c0 + 2 * aw:c0 + 3 * aw]], axis=1)
    e1 = jnp.full((1, LANE), lam, F32)
    e2 = (subln_g.astype(F32) * (1.0 - lam_init)).reshape(1, LANE)
    o_lat = _flash(pa_lat, c0, k_all, v_all, 0, 0, e1, e2, mode="diff", hk=heads, dim=A_QK_DIM, tq=_pick(n, 1024))
    o_ctx = _flash(pa_ctx, c0, pa_ctx, pa_ctx, c0 + aw, c0 + 2 * aw, e1, e2, mode="diff", hk=heads, dim=A_QK_DIM,
                   tq=_pick(nctx, 512), rope=False)
    return o_lat, o_ctx


def _gqa_attention(p_lat, pkv_ctx, qcol0, q_norm_g, k_norm_g):
    n = p_lat.shape[1]
    hq, hk = 2 * D_KV_GROUP, 2
    kcol0 = qcol0 + hq * D_HEAD
    vcol0 = kcol0 + hk * D_HEAD
    k_l = _kprep(p_lat, kcol0, hk, dim=D_HEAD, gain=k_norm_g)
    k_c = _kprep(pkv_ctx, 0, hk, dim=D_HEAD, gain=k_norm_g, rope=False)
    k_all = jnp.concatenate([k_l, k_c], axis=1)
    v_all = jnp.concatenate([p_lat[..., vcol0:vcol0 + hk * D_HEAD], pkv_ctx[..., hk * D_HEAD:]], axis=1)
    dummy = jnp.zeros((1, LANE), F32)
    return _flash(p_lat, qcol0, k_all, v_all, 0, 0, dummy, dummy, mode="gqa", hk=hk, dim=D_HEAD,
                  tq=_pick(n, 512), gain=q_norm_g)


RW_CHUNK = 64
RW_HALO = 8


def _mm_hi(a, b):
    return jnp.dot(a, b, preferred_element_type=F32, precision=HI)


def _mm_bf(a, b):
    return jnp.dot(a.astype(BF16), b.astype(BF16), preferred_element_type=F32)


def _split3(x):
    x1 = x.astype(BF16)
    r1 = x - x1.astype(F32)
    x2 = r1.astype(BF16)
    return x1, x2, (r1 - x2.astype(F32)).astype(BF16)


def _mm_x3(a, b):
    ah, al, _ = _split3(a)
    bh, bl, _ = _split3(b)
    d = lambda u, w: jnp.dot(u, w, preferred_element_type=F32)
    return d(ah, bh) + (d(ah, bl) + d(al, bh))


def _mm_sel(z, sel):
    z1, z2, z3 = _split3(z)
    d = lambda u: jnp.dot(u, sel, preferred_element_type=F32)
    return d(z1) + (d(z2) + d(z3))


def _rwprep_body(x_ref, hp_ref, hn_ref, mup_ref, mun_ref, vec_ref, w2f_ref, w2b_ref, a2f_ref, a2b_ref,
                 g2_ref, sel_ref, selt_ref,
                 r_ref, v_ref, kk_ref, lwf_ref, bf_ref, kf_ref, lwb_ref, bb_ref, kb_ref, g_ref, bonus_ref,
                 *, nb, tm, bw):
    i = pl.program_id(1)
    x = x_ref[0].astype(F32)
    row = lax.broadcasted_iota(jnp.int32, x.shape, 0)
    prev_edge = jnp.where(i > 0, hp_ref[0, RW_HALO - 1:RW_HALO, :].astype(F32), 0.0)
    next_edge = jnp.where(i < nb - 1, hn_ref[0, 0:1, :].astype(F32), 0.0)
    prev = jnp.where(row == 0, prev_edge, pltpu.roll(x, 1, 0))
    nxt = jnp.where(row == tm - 1, next_edge, pltpu.roll(x, tm - 1, 0))
    xs = x + mup_ref[...] * (prev - x) + mun_ref[...] * (nxt - x)
    r, k, v = xs[:, :bw], xs[:, bw:2 * bw], xs[:, 2 * bw:3 * bw]
    lr = xs[:, 3 * bw:3 * bw + LANE]
    gd = xs[:, 3 * bw + LANE:3 * bw + 3 * LANE]
    k_k, k_a, r_k = vec_ref[0:1], vec_ref[1:2], vec_ref[2:3]
    w0f, a0f, w0b, a0b = vec_ref[3:4], vec_ref[4:5], vec_ref[5:6], vec_ref[6:7]
    sel, selt = sel_ref[...], selt_ref[...]

    def head_sum(z):
        return _mm_sel(_mm_sel(z, sel), selt)

    kk = k * k_k
    kk = kk * lax.rsqrt(head_sum(kk * kk) + 1e-12)
    g_ref[0] = _mm_bf(jax.nn.sigmoid(gd), g2_ref[...]).astype(g_ref.dtype)
    th = jnp.tanh(lr)
    ksum = jnp.zeros_like(k)
    for (w0, a0, w2_ref, a2_ref, lw_ref, b_ref, kd_ref) in (
            (w0f, a0f, w2f_ref, a2f_ref, lwf_ref, bf_ref, kf_ref),
            (w0b, a0b, w2b_ref, a2b_ref, lwb_ref, bb_ref, kb_ref)):
        w_log = -jax.nn.softplus(-(w0 + _mm_bf(th, w2_ref[...]))) - 0.5
        logw = -jnp.exp(w_log)
        a = jax.nn.sigmoid(a0 + _mm_bf(lr, a2_ref[...]))
        k_d = k * (1.0 + (a - 1.0) * k_a)
        ksum = ksum + k_d
        lw_ref[0] = logw
        b_ref[0] = kk * a
        kd_ref[0] = k_d
    r_ref[0] = r
    v_ref[0] = v
    kk_ref[0] = kk
    bonus_ref[0] = head_sum(r * ksum * r_k) * v


def _rwkv_prepare(pr, p, cols):
    bsz, n, _ = pr.shape
    bw = p["w0_f"].shape[0]
    nh = bw // B_HEAD
    tm = _pick(n, 256)
    nb = n // tm
    hb = tm // RW_HALO
    pad = cols - 3 * bw

    def padded(vec):
        return jnp.concatenate([vec.astype(F32), jnp.zeros((cols - vec.shape[0],), F32)]).reshape(1, cols)

    vecs = jnp.stack([p["k_k"], p["k_a"], p["r_k"].reshape(-1), p["w0_f"], p["a0_f"], p["w0_b"], p["a0_b"],
                      jnp.zeros((bw,), F32)]).astype(F32)
    zr = jnp.zeros((B_DECAY_RANK, bw), F32)
    w2 = {d: jnp.concatenate([p["w2_" + d].astype(F32), zr], axis=0) for d in "fb"}
    a2 = {d: jnp.concatenate([zr, p["a2_" + d].astype(F32)], axis=0) for d in "fb"}
    g2 = jnp.concatenate([p["g2"].astype(F32), jnp.zeros((2 * LANE - B_GATE_RANK, bw), F32)], axis=0)
    sel_np = np.zeros((bw, LANE), np.float32)
    sel_np[np.arange(bw), np.arange(bw) // B_HEAD] = 1.0
    sel, selt = jnp.asarray(sel_np, BF16), jnp.asarray(sel_np.T, BF16)
    w2 = {d: w2[d].astype(BF16) for d in "fb"}
    a2 = {d: a2[d].astype(BF16) for d in "fb"}
    g2 = g2.astype(BF16)
    assert pad >= 3 * LANE and B_DECAY_RANK + B_ICL_RANK == LANE

    full = lambda shape: pl.BlockSpec(shape, lambda b, i: tuple(0 for _ in shape))
    tok = pl.BlockSpec((1, tm, bw), lambda b, i: (b, i, 0))
    hm = tok
    hm_shape = jax.ShapeDtypeStruct((bsz, n, bw), F32)
    outs = pl.pallas_call(
        functools.partial(_rwprep_body, nb=nb, tm=tm, bw=bw),
        grid=(bsz, nb),
        in_specs=[pl.BlockSpec((1, tm, cols), lambda b, i: (b, i, 0)),
                  pl.BlockSpec((1, RW_HALO, cols), lambda b, i: (b, jnp.maximum(i * hb - 1, 0), 0)),
                  pl.BlockSpec((1, RW_HALO, cols), lambda b, i: (b, jnp.minimum((i + 1) * hb, nb * hb - 1), 0)),
                  full((1, cols)), full((1, cols)), full((8, bw)),
                  full((LANE, bw)), full((LANE, bw)), full((LANE, bw)), full((LANE, bw)),
                  full((2 * LANE, bw)), full((bw, LANE)), full((LANE, bw))],
        out_specs=[hm] * 9 + [tok, tok],
        out_shape=[hm_shape] * 9 + [jax.ShapeDtypeStruct((bsz, n, bw), BF16),
                                    jax.ShapeDtypeStruct((bsz, n, bw), F32)],
        compiler_params=_cp(("parallel", "parallel")),
        name="rwkv_prepare",
    )(pr, pr, pr, padded(p["mu_prev"]), padded(p["mu_next"]), vecs, w2["f"], w2["b"], a2["f"], a2["b"],
      g2, sel, selt)
    r, v, kk, lwf, bf, kf, lwb, bb, kb, g, bonus = outs
    return dict(r=r, v=v, kk=kk, g=g, bonus=bonus, f=(lwf, bf, kf), b=(lwb, bb, kb))


RW_GROUP = 4


def _bd(x):
    t = jnp.concatenate([x.astype(BF16)] * RW_GROUP, axis=1)
    r = lax.broadcasted_iota(jnp.int32, t.shape, 1) // B_HEAD
    c = lax.broadcasted_iota(jnp.int32, t.shape, 2) // B_HEAD
    return jnp.where(r == c, t, jnp.zeros_like(t))


def _bdot(a, b, dims):
    return jnp.einsum(dims, a.astype(BF16), b.astype(BF16), preferred_element_type=F32)


def _groups(x):
    gw = RW_GROUP * B_HEAD
    return jnp.stack([x[:, g * gw:(g + 1) * gw] for g in range(x.shape[1] // gw)])


def _rw_decays(r, kk, b, kd, lw, reverse):
    ck = r.shape[0]
    ti = lax.broadcasted_iota(jnp.int32, (ck, ck), 0)
    si = lax.broadcasted_iota(jnp.int32, (ck, ck), 1)
    tri = ((si >= ti) if reverse else (si <= ti)).astype(BF16)
    l1, l2, l3 = _split3(lw)
    tdot = lambda z: jnp.dot(tri, z, preferred_element_type=F32)
    cum = tdot(l1) + (tdot(l2) + tdot(l3))
    total = cum[0:1] if reverse else cum[ck - 1:ck]
    winv = jnp.exp(-cum)
    wrest = jnp.exp(total - cum)
    return (-kk * jnp.exp(cum - lw), b * winv, kd * winv, r * jnp.exp(cum), b * wrest, kd * wrest, jnp.exp(total))


def _rw_chunks(fwd, bwd, s_f, s_b):
    nseq = len(fwd)
    ck, w = fwd[0][0].shape
    gw = RW_GROUP * B_HEAD
    ng = w // gw
    parts = []
    for probs, rev in ((fwd, False), (bwd, True)):
        for (r, v, kk, b, kd, lw) in probs:
            parts.append([_groups(z) for z in _rw_decays(r, kk, b, kd, lw, rev)] + [_groups(v)])
    at, bt, kt, rt, btw, ktw, decay, v = [jnp.concatenate(zs, axis=0) for zs in zip(*parts)]
    s = jnp.concatenate([_groups(z) for z in list(s_f) + list(s_b)], axis=0)
    nslab = 2 * nseq * ng
    shape = (nslab, ck, gw)
    lag = lax.broadcasted_iota(jnp.int32, shape, 1) - lax.broadcasted_iota(jnp.int32, shape, 2) % ck
    lag = jnp.where(lax.broadcasted_iota(jnp.int32, shape, 0) >= nseq * ng, -lag, lag)
    incl, strict = lag >= 0, lag > 0
    eye = (lag == 0).astype(F32)

    bdv = _bd(v)
    lhs = jnp.concatenate([at, rt], axis=1)
    lb, lk = _bdot(lhs, _bd(bt), "gtk,gnk->gtn"), _bdot(lhs, _bd(kt), "gtk,gnk->gtn")
    lab, mrb = jnp.where(strict, lb[:, :ck], 0.0), jnp.where(incl, lb[:, ck:], 0.0)
    lak, mrk = jnp.where(strict, lk[:, :ck], 0.0), jnp.where(incl, lk[:, ck:], 0.0)
    tinv = eye + lab
    pw = _bdot(lab, _bd(lab), "gtk,gkn->gtn")
    span = 4
    while span < ck:
        both = _bdot(jnp.concatenate([tinv, pw], axis=1), _bd(pw), "gtk,gkn->gtn")
        tinv, pw = tinv + both[:, :ck], both[:, ck:]
        span *= 2
    tinv = tinv + _bdot(tinv, _bd(pw), "gtk,gkn->gtn")
    vmix = _bdot(jnp.concatenate([lak, mrk], axis=1), bdv, "gtk,gkn->gtn")
    p1 = _bdot(tinv, _bd(at), "gtk,gkn->gtn")
    q1 = _bdot(tinv, _bd(vmix[:, :ck]), "gtk,gkn->gtn")
    smix = _bdot(jnp.concatenate([p1, rt], axis=1), _bd(s), "gtk,gnk->gtn")
    u = smix[:, :ck] + q1
    y = smix[:, ck:] + _bdot(mrb, _bd(u), "gtk,gkn->gtn") + vmix[:, ck:]
    z = _bdot(jnp.concatenate([u, v], axis=1), jnp.concatenate([btw, ktw], axis=1), "gtm,gtn->gmn")
    lane_head = lax.broadcasted_iota(jnp.int32, (nslab, B_HEAD, gw), 2) // B_HEAD
    upd = jnp.zeros_like(s)
    for h in range(RW_GROUP):
        upd = upd + jnp.where(lane_head == h, z[:, h * B_HEAD:(h + 1) * B_HEAD], 0.0)
    s_new = s * decay + upd
    lanes = lambda x, q: jnp.concatenate([x[q * ng + g] for g in range(ng)], axis=1)
    seqs = range(nseq)
    return ([lanes(y, q) for q in seqs], [lanes(y, nseq + q) for q in seqs],
            [lanes(s_new, q) for q in seqs], [lanes(s_new, nseq + q) for q in seqs])


def _rwscan_body(rf_ref, vf_ref, kkf_ref, bf_ref, kdf_ref, lwf_ref, rb_ref, vb_ref, kkb_ref, bb_ref, kdb_ref, lwb_ref,
                 s0f_ref, s0b_ref, yf_ref, yb_ref, sTf_ref, sTb_ref, sf_ref, sb_ref, *, nc, nseq):
    c = pl.program_id(0)

    @pl.when(c == 0)
    def _():
        sf_ref[...] = s0f_ref[...]
        sb_ref[...] = s0b_ref[...]

    seqs = range(nseq)
    yf, yb, sf, sb = _rw_chunks(
        [(rf_ref[q], vf_ref[q], kkf_ref[q], bf_ref[q], kdf_ref[q], lwf_ref[q]) for q in seqs],
        [(rb_ref[q], vb_ref[q], kkb_ref[q], bb_ref[q], kdb_ref[q], lwb_ref[q]) for q in seqs],
        [sf_ref[q] for q in seqs], [sb_ref[q] for q in seqs])
    for q in seqs:
        yf_ref[q] = yf[q]
        yb_ref[q] = yb[q]
        sf_ref[q] = sf[q]
        sb_ref[q] = sb[q]

    @pl.when(c == nc - 1)
    def _():
        for q in seqs:
            sTf_ref[q] = sf[q]
            sTb_ref[q] = sb[q]


def _rwkv_scan(r, v, kk, fwd, bwd, s0f, s0b):
    bsz, n, w = r.shape
    ck = RW_CHUNK
    assert n % ck == 0 and ck == B_HEAD and w % (RW_GROUP * B_HEAD) == 0
    nc = n // ck
    seq_f = pl.BlockSpec((bsz, ck, w), lambda c: (0, c, 0))
    seq_b = pl.BlockSpec((bsz, ck, w), lambda c: (0, nc - 1 - c, 0))
    st = pl.BlockSpec((bsz, B_HEAD, w), lambda c: (0, 0, 0))
    y_shape = jax.ShapeDtypeStruct((bsz, n, w), F32)
    s_shape = jax.ShapeDtypeStruct((bsz, B_HEAD, w), F32)
    (lwf, bf, kf), (lwb, bb, kb) = fwd, bwd
    return pl.pallas_call(
        functools.partial(_rwscan_body, nc=nc, nseq=bsz),
        grid=(nc,),
        in_specs=[seq_f] * 6 + [seq_b] * 6 + [st, st],
        out_specs=[seq_f, seq_b, st, st],
        out_shape=[y_shape, y_shape, s_shape, s_shape],
        scratch_shapes=[pltpu.VMEM((bsz, B_HEAD, w), F32), pltpu.VMEM((bsz, B_HEAD, w), F32)],
        compiler_params=_cp(("arbitrary",)),
        name="rwkv_scan",
    )(r, v, kk, bf, kf, lwf, r, v, kk, bb, kb, lwb, s0f, s0b)


def _rwpost_body(yf_ref, yb_ref, g_ref, bonus_ref, lg_ref, lb_ref, sel_ref, selt_ref, o_ref):
    sel, selt = sel_ref[...], selt_ref[...]
    head_mean = lambda z: _mm_sel(_mm_sel(z, sel), selt) * (1.0 / B_HEAD)
    y = yf_ref[0] + yb_ref[0]
    d = y - head_mean(y)
    y = d * lax.rsqrt(head_mean(d * d) + B_GN_EPS)
    y = y * lg_ref[...] + lb_ref[...] + bonus_ref[0]
    o_ref[0] = (y * g_ref[0].astype(F32)).astype(o_ref.dtype)


def _head_selectors(bw):
    sel_np = np.zeros((bw, LANE), np.float32)
    sel_np[np.arange(bw), np.arange(bw) // B_HEAD] = 1.0
    return jnp.asarray(sel_np, BF16), jnp.asarray(sel_np.T, BF16)


def _rwkv_post(y_f, y_b, g, bonus, lnx_g, lnx_b):
    bsz, n, bw = y_f.shape
    tm = _pick(n, 256)
    tok = pl.BlockSpec((1, tm, bw), lambda b, i: (b, i, 0))
    vec = pl.BlockSpec((1, bw), lambda b, i: (0, 0))
    sel, selt = _head_selectors(bw)
    return pl.pallas_call(
        _rwpost_body,
        grid=(bsz, n // tm),
        in_specs=[tok, tok, tok, tok, vec, vec,
                  pl.BlockSpec((bw, LANE), lambda b, i: (0, 0)), pl.BlockSpec((LANE, bw), lambda b, i: (0, 0))],
        out_specs=tok,
        out_shape=jax.ShapeDtypeStruct((bsz, n, bw), BF16),
        compiler_params=_cp(("parallel", "parallel")),
        name="rwkv_post",
    )(y_f, y_b, g, bonus, lnx_g.reshape(1, bw).astype(F32), lnx_b.reshape(1, bw).astype(F32), sel, selt)


def _rwkv_run(prep, states0, p):
    y_f, y_b, s_f, s_b = _rwkv_scan(prep["r"], prep["v"], prep["kk"], prep["f"], prep["b"], states0[0], states0[1])
    return _rwkv_post(y_f, y_b, prep["g"], prep["bonus"], p["lnx_g"], p["lnx_b"]), (s_f, s_b)


HY_N2 = 128
HY_HALO = 8


def _hypre_body(x_ref, hp_ref, hn_ref, cw_ref, cb_ref, x0_ref, uv_ref, *, nb, tm, cw):
    i = pl.program_id(1)
    x = x_ref[0].astype(F32)
    row = lax.broadcasted_iota(jnp.int32, x.shape, 0)
    prev_edge = jnp.where(i > 0, hp_ref[0, HY_HALO - 1:HY_HALO, :].astype(F32), 0.0)
    next_edge = jnp.where(i < nb - 1, hn_ref[0, 0:1, :].astype(F32), 0.0)
    prev = jnp.where(row == 0, prev_edge, pltpu.roll(x, 1, 0))
    nxt = jnp.where(row == tm - 1, next_edge, pltpu.roll(x, tm - 1, 0))
    u = prev * cw_ref[0:1] + x * cw_ref[1:2] + nxt * cw_ref[2:3] + cb_ref[...]
    x0_ref[0] = u[:, :cw]
    uv_ref[0] = u[:, 2 * cw:3 * cw] * u[:, cw:2 * cw]


def _hyena_pre(p, conv_w, conv_b):
    bsz, n, _ = p.shape
    hc = conv_w.shape[1]
    cw = hc // 3
    tm = _pick(n, 256)
    nb = n // tm
    hb = tm // HY_HALO
    cwp = jnp.concatenate([conv_w.astype(F32), jnp.zeros((5, hc), F32)], axis=0)
    out = pl.BlockSpec((1, tm, cw), lambda b, i: (b, i, 0))
    return pl.pallas_call(
        functools.partial(_hypre_body, nb=nb, tm=tm, cw=cw),
        grid=(bsz, nb),
        in_specs=[pl.BlockSpec((1, tm, hc), lambda b, i: (b, i, 0)),
                  pl.BlockSpec((1, HY_HALO, hc), lambda b, i: (b, jnp.maximum(i * hb - 1, 0), 0)),
                  pl.BlockSpec((1, HY_HALO, hc), lambda b, i: (b, jnp.minimum((i + 1) * hb, nb * hb - 1), 0)),
                  pl.BlockSpec((8, hc), lambda b, i: (0, 0)),
                  pl.BlockSpec((1, hc), lambda b, i: (0, 0))],
        out_specs=[out, out],
        out_shape=[jax.ShapeDtypeStruct((bsz, n, cw), F32)] * 2,
        compiler_params=_cp(("parallel", "parallel")),
        name="hyena_pre",
    )(p, p, p, cwp, conv_b.reshape(1, hc).astype(F32))


def _hyfilt_body(z_ref, t_ref, z0_ref, w1_ref, w2_ref, w3_ref, w4_ref, w4b_ref, vec_ref, dl_ref,
                 k_ref, s_ref, *, half_tiles):
    i = pl.program_id(0)
    b1, b2, b3, fr = vec_ref[0:1], vec_ref[1:2], vec_ref[2:3], vec_ref[3:4]

    def mlp3(z):
        h = jnp.sin(fr * (_mm_hi(z, w1_ref[...]) + b1))
        h = jnp.sin(fr * (_mm_hi(h, w2_ref[...]) + b2))
        h = jnp.sin(fr * (_mm_hi(h, w3_ref[...]) + b3))
        lane = lax.broadcasted_iota(jnp.int32, h.shape, 1)
        return jnp.where(lane < HY_ORDER, h, 0.0), jnp.where(lane >= HY_ORDER, h, 0.0)

    h_lo, h_hi = mlp3(z_ref[...])
    raw = jnp.concatenate([_mm_hi(h_lo, w4_ref[...]), _mm_hi(h_hi, w4_ref[...])], axis=0)
    raw = raw * jnp.exp(-t_ref[...] * dl_ref[...])

    @pl.when(i == 0)
    def _():
        s_ref[...] = jnp.zeros_like(s_ref)

    s_ref[...] += jnp.sum(jnp.abs(raw), axis=0, keepdims=True)
    hb0 = _mm_hi(mlp3(z0_ref[...])[0], w4b_ref[...])
    row = lax.broadcasted_iota(jnp.int32, raw.shape, 0)
    raw = raw + jnp.where((row == 0) & (i == 0), hb0[0:1], 0.0)
    k_ref[...] = jnp.where((row == 0) & (i == half_tiles), 0.0, raw)


def _hyena_kernel_taps(n, p):
    cw = p["f_w4"].shape[1] // 2
    f32 = lambda a: a.astype(F32)
    row = jnp.arange(2 * n, dtype=jnp.int32)
    pos = jnp.where(row < n, row, (2 * n - row) % n).astype(F32)[:, None]
    tt = pos * jnp.float32(1.0 / (n - 1))
    bands = (HY_EMB - 1) // 2
    ang = (2 * math.pi / n) * pos * jnp.linspace(1e-4, bands - 1, bands, dtype=F32)[None]
    zz = jnp.concatenate([tt, jnp.cos(ang), -jnp.sin(ang), jnp.zeros((2 * n, HY_ORDER - HY_EMB), F32)], axis=-1)
    w1 = jnp.concatenate([f32(p["f_w1"]), jnp.zeros((HY_ORDER - HY_EMB, HY_ORDER), F32)], axis=0)
    twice = lambda w: jnp.kron(jnp.eye(2, dtype=F32), f32(w))
    vecs = jnp.stack([jnp.tile(f32(p[k]), 2) for k in ("f_b1", "f_b2", "f_b3", "f_freq")]
                     + [jnp.zeros((2 * HY_ORDER,), F32)] * 4)
    deltas = jnp.abs(jnp.linspace(math.log(HY_TARGET) / HY_SLOW_PCT, math.log(HY_TARGET) / HY_FAST_PCT, cw,
                                  dtype=F32)).reshape(1, cw)
    tr = _pick(n, 512)
    half_tiles = n // tr
    pk = 2 * HY_ORDER
    zp = zz.reshape(-1, 2, tr // 2, HY_ORDER).transpose(0, 2, 1, 3).reshape(-1, pk)
    full = lambda shape: pl.BlockSpec(shape, lambda i: tuple(0 for _ in shape))
    w4 = jnp.concatenate([f32(p["f_w4"])] * 2, axis=0)
    return pl.pallas_call(
        functools.partial(_hyfilt_body, half_tiles=half_tiles),
        grid=(2 * half_tiles,),
        in_specs=[pl.BlockSpec((tr // 2, pk), lambda i: (i, 0)),
                  pl.BlockSpec((tr, 1), lambda i: (i, 0)),
                  full((8, pk)), full((pk, pk)), full((pk, pk)), full((pk, pk)),
                  pl.BlockSpec((pk, cw), lambda i: (0, i // half_tiles)),
                  pl.BlockSpec((pk, cw), lambda i: (0, 1)),
                  full((8, pk)), full((1, cw))],
        out_specs=[pl.BlockSpec((tr, cw), lambda i: (i, 0)), pl.BlockSpec((1, cw), lambda i: (0, 0))],
        out_shape=[jax.ShapeDtypeStruct((2 * n, cw), F32), jax.ShapeDtypeStruct((1, cw), F32)],
        compiler_params=_cp(("arbitrary",)),
        name="hyena_filter",
    )(zp, tt, jnp.broadcast_to(jnp.tile(zz[0:1], (1, 2)), (8, pk)), twice(w1), twice(p["f_w2"]), twice(p["f_w3"]),
      w4, w4, vecs, deltas)


def _dft_consts(n):
    nn = 2 * n
    n2 = HY_N2
    n1 = nn // n2
    n1h = n1 // 2 + 1
    nf = -(-n1h // 8) * 8
    live = (np.arange(nf) < n1h).astype(np.float64)
    a1 = 2 * np.pi * np.outer(np.arange(nf), np.arange(n1)) / n1
    f1 = np.concatenate([np.cos(a1), -np.sin(a1)], axis=0) * np.tile(live, 2)[:, None]
    wgt = live * np.where((np.arange(nf) == 0) | (np.arange(nf) == n1 // 2), 1.0, 2.0)
    f1inv = np.concatenate([np.cos(a1.T) * wgt, -np.sin(a1.T) * wgt], axis=1) / nn
    a2 = 2 * np.pi * np.outer(np.arange(n2), np.arange(n2)) / n2
    c2, s2 = np.cos(a2), -np.sin(a2)
    m2 = np.block([[c2, -s2], [s2, c2]])
    m2inv = np.block([[c2, s2], [-s2, c2]])
    at = 2 * np.pi * np.outer(np.arange(n2), np.arange(nf)) / nn
    tw = np.stack([np.cos(at), -np.sin(at)])
    c = lambda a: jnp.asarray(a.astype(np.float32))
    return dict(n1=n1, n2=n2, nf=nf, f1=c(f1), f1inv=c(f1inv), m2=c(m2), m2inv=c(m2inv),
                tw_s2=c(tw[:, :, :, None]),
                tw_f1=c(np.transpose(tw, (0, 2, 1))[:, :, :, None]))


HY_S2 = 8
HY_CTILE = 512


def _dft1_body(f_ref, x_ref, tw_ref, o_ref, *, nf):
    x = jnp.swapaxes(x_ref[0], 0, 1)
    fh, fl, _ = _split3(f_ref[...])
    d = lambda u, w: jnp.dot(u, w, preferred_element_type=F32)
    re, im = [], []
    for jj in range(HY_S2):
        xh, xl, _ = _split3(x[jj])
        a = d(fh, xh) + (d(fh, xl) + d(fl, xh))
        ar, ai = a[:nf], a[nf:]
        twr, twi = tw_ref[0, jj], tw_ref[1, jj]
        re.append(ar * twr - ai * twi)
        im.append(ar * twi + ai * twr)
    o_ref[0, 0] = jnp.swapaxes(jnp.stack(re), 0, 1)
    o_ref[0, 1] = jnp.swapaxes(jnp.stack(im), 0, 1)


def _dft_stage1(x, consts, rows):
    bsz, _, cw = x.shape
    n2, nf = consts["n2"], consts["nf"]
    xv = x.reshape(bsz, rows, n2, cw)
    f1 = consts["f1"][:, :rows]
    tc = _pick(cw, HY_CTILE, LANE)
    return pl.pallas_call(
        functools.partial(_dft1_body, nf=nf),
        grid=(bsz, n2 // HY_S2, cw // tc),
        in_specs=[pl.BlockSpec((2 * nf, rows), lambda b, j, c: (0, 0)),
                  pl.BlockSpec((1, rows, HY_S2, tc), lambda b, j, c: (b, 0, j, c)),
                  pl.BlockSpec((2, HY_S2, nf, 1), lambda b, j, c: (0, j, 0, 0))],
        out_specs=pl.BlockSpec((1, 2, nf, HY_S2, tc), lambda b, j, c: (b, 0, 0, j, c)),
        out_shape=jax.ShapeDtypeStruct((bsz, 2, nf, n2, cw), F32),
        compiler_params=_cp(("parallel", "parallel", "parallel")),
        name="hyena_dft1",
    )(f1, xv, consts["tw_s2"])


def _dftmid_body(a_ref, m2_ref, m2i_ref, h_ref, tw_ref, is_ref, o_ref, *, n2, filt):
    a = a_ref[0, :, 0].reshape(2 * n2, a_ref.shape[-1])
    x = _mm_x3(m2_ref[...], a)
    xr, xi = x[:n2], x[n2:]
    if filt:
        o_ref[0, 0, 0] = xr * is_ref[...]
        o_ref[0, 1, 0] = xi * is_ref[...]
        return
    hr, hi = h_ref[0, 0, 0], h_ref[0, 1, 0]
    y = jnp.concatenate([xr * hr - xi * hi, xr * hi + xi * hr], axis=0)
    z = _mm_x3(m2i_ref[...], y)
    zr, zi = z[:n2], z[n2:]
    twr, twi = tw_ref[0, 0], tw_ref[1, 0]
    o_ref[0, 0, 0] = zr * twr + zi * twi
    o_ref[0, 1, 0] = zi * twr - zr * twi


def _dft_mid(a, h, inv_s, consts, filt):
    bsz = a.shape[0]
    n1, n2 = consts["nf"], consts["n2"]
    cw = a.shape[-1]
    av = a
    blk = pl.BlockSpec((1, 2, 1, n2, cw), lambda b, f: (b, 0, f, 0, 0))
    hblk = pl.BlockSpec((1, 2, 1, n2, cw), (lambda b, f: (0, 0, 0, 0, 0)) if filt else (lambda b, f: (0, 0, f, 0, 0)))
    return pl.pallas_call(
        functools.partial(_dftmid_body, n2=n2, filt=filt),
        grid=(bsz, n1),
        in_specs=[blk,
                  pl.BlockSpec((2 * n2, 2 * n2), lambda b, f: (0, 0)),
                  pl.BlockSpec((2 * n2, 2 * n2), lambda b, f: (0, 0)),
                  hblk,
                  pl.BlockSpec((2, 1, n2, 1), lambda b, f: (0, f, 0, 0)),
                  pl.BlockSpec((1, cw), lambda b, f: (0, 0))],
        out_specs=blk,
        out_shape=jax.ShapeDtypeStruct((bsz, 2, n1, n2, cw), F32),
        compiler_params=_cp(("parallel", "parallel")),
        name="hyena_dftmid_" + ("filter" if filt else "conv"),
    )(av, consts["m2"], consts["m2inv"], h, consts["tw_f1"], inv_s)


def _dft3_body(f_ref, z_ref, x0_ref, uv_ref, bias_ref, o_ref):
    zr, zi = jnp.swapaxes(z_ref[0, 0], 0, 1), jnp.swapaxes(z_ref[0, 1], 0, 1)
    x0, uv = jnp.swapaxes(x0_ref[0], 0, 1), jnp.swapaxes(uv_ref[0], 0, 1)
    fh, fl, _ = _split3(f_ref[...])
    d = lambda u, w: jnp.dot(u, w, preferred_element_type=F32)
    out = []
    for jj in range(HY_S2):
        zh, zl, _ = _split3(jnp.concatenate([zr[jj], zi[jj]], axis=0))
        y = d(fh, zh) + (d(fh, zl) + d(fl, zh))
        out.append(x0[jj] * (y + uv[jj] * bias_ref[...]))
    o_ref[0] = jnp.swapaxes(jnp.stack(out), 0, 1)


def _dft_stage1_inv(z, x0, uv, bias, consts):
    bsz, n, cw = x0.shape
    nf, n2 = consts["nf"], consts["n2"]
    hr = consts["n1"] // 2
    tokv = lambda t: t.reshape(bsz, hr, n2, cw)
    tc = _pick(cw, HY_CTILE, LANE)
    tile = pl.BlockSpec((1, hr, HY_S2, tc), lambda b, j, c: (b, 0, j, c))
    out = pl.pallas_call(
        _dft3_body,
        grid=(bsz, n2 // HY_S2, cw // tc),
        in_specs=[pl.BlockSpec((hr, 2 * nf), lambda b, j, c: (0, 0)),
                  pl.BlockSpec((1, 2, nf, HY_S2, tc), lambda b, j, c: (b, 0, 0, j, c)),
                  tile, tile,
                  pl.BlockSpec((1, tc), lambda b, j, c: (0, c))],
        out_specs=tile,
        out_shape=jax.ShapeDtypeStruct((bsz, hr, n2, cw), F32),
        compiler_params=_cp(("parallel", "parallel", "parallel")),
        name="hyena_dft3",
    )(consts["f1inv"][:hr], z, tokv(x0), tokv(uv), bias.reshape(1, cw).astype(F32))
    return out.reshape(bsz, n, cw)


def _hyena(p_lat, p):
    n = p_lat.shape[1]
    consts = _dft_consts(n)
    n1 = consts["n1"]
    x0, uv = _hyena_pre(p_lat, p["conv_w"], p["conv_b"])
    taps, sabs = _hyena_kernel_taps(n, p)
    cw = taps.shape[1]
    ones = jnp.ones((1, cw), F32)
    hk = _dft_stage1(taps[None], consts, n1)
    hspec = _dft_mid(hk, jnp.zeros((1, 2, 1, HY_N2, cw), F32), 1.0 / sabs, consts, True)
    a = _dft_stage1(uv, consts, n1 // 2)
    z = _dft_mid(a, hspec, ones, consts, False)
    return _dft_stage1_inv(z, x0, uv, p["hy_bias"], consts)


def _mod_vectors(c, c_ctx, p):
    bsz, d = c.shape
    cvec = jnp.concatenate([c, c_ctx[None], jnp.zeros((8 - bsz - 1, d), c.dtype)], axis=0).astype(F32)
    m = _mods(cvec, p["w_mod"].astype(F32), p["b_mod"].astype(F32))
    lat = [v[:, None, :] for v in jnp.split(m[:bsz], N_MOD, axis=-1)]
    cx = [jnp.broadcast_to(v[:, None, :], (bsz, 1, d)) for v in jnp.split(m[bsz:bsz + 1], N_MOD, axis=-1)]
    return lat, cx


def _affine(norm_g, shift, scale):
    return norm_g.astype(F32) * (1.0 + scale), shift


def kernel(x, c, ctx, c_ctx,
           l0_w_mod, l0_b_mod, l0_norm1_g, l0_norm2_g, l0_w_in, l0_lam_q1, l0_lam_k1, l0_lam_q2, l0_lam_k2,
           l0_subln_g, l0_mu_prev, l0_mu_next, l0_w0_f, l0_w2_f, l0_a0_f, l0_a2_f, l0_w0_b, l0_w2_b, l0_a0_b,
           l0_a2_b, l0_g2, l0_k_k, l0_k_a, l0_r_k, l0_lnx_g, l0_lnx_b, l0_w_out, l0_mlp_w1, l0_mlp_w2,
           l1_w_mod, l1_b_mod, l1_norm1_g, l1_norm2_g, l1_w_in, l1_conv_w, l1_conv_b, l1_f_w1, l1_f_b1,
           l1_f_w2, l1_f_b2, l1_f_w3, l1_f_b3, l1_f_w4, l1_f_freq, l1_hy_bias, l1_q_norm_g, l1_k_norm_g,
           l1_w_out, l1_mlp_w1, l1_mlp_w2, final_g):
    bsz, n, d = x.shape
    x = x.astype(F32)
    ctx_s = ctx.astype(F32)
    bf = lambda w: w.astype(BF16)

    p0 = dict(w_mod=l0_w_mod, b_mod=l0_b_mod, mu_prev=l0_mu_prev, mu_next=l0_mu_next, w0_f=l0_w0_f, w2_f=l0_w2_f,
              a0_f=l0_a0_f, a2_f=l0_a2_f, w0_b=l0_w0_b, w2_b=l0_w2_b, a0_b=l0_a0_b, a2_b=l0_a2_b, g2=l0_g2,
              k_k=l0_k_k, k_a=l0_k_a, r_k=l0_r_k, lnx_g=l0_lnx_g, lnx_b=l0_lnx_b)
    (sh1, sc1, g1, sh2, sc2, g2), (csh1, csc1, cg1, csh2, csc2, cg2) = _mod_vectors(c, c_ctx, p0)
    aw = d // 2
    w_rw = l0_w_in[:, 3 * aw:]
    rw_cols = -(-(w_rw.shape[1] + 2 * LANE - B_GATE_RANK) // 512) * 512
    w_in0 = bf(jnp.concatenate([w_rw, jnp.zeros((d, rw_cols - w_rw.shape[1]), w_rw.dtype), l0_w_in[:, :3 * aw]],
                               axis=1))
    a1, b1 = _affine(l0_norm1_g, sh1, sc1)
    ca1, cb1 = _affine(l0_norm1_g, csh1, csc1)
    p_lat0 = _inproj(x, a1, b1, w_in0)
    p_ctx0 = _inproj(ctx_s, ca1, cb1, w_in0)
    lam_init = 0.8 - 0.6 * math.exp(-0.3 * 0)
    lam = (jnp.exp(jnp.sum(l0_lam_q1 * l0_lam_k1).astype(F32))
           - jnp.exp(jnp.sum(l0_lam_q2 * l0_lam_k2).astype(F32)) + lam_init)
    oa_lat, oa_ctx = _diff_attention(p_lat0, p_ctx0, rw_cols, aw, lam, lam_init, l0_subln_g)
    nh, hd = l0_r_k.shape
    zero = jnp.zeros((bsz, hd, nh * hd), F32)
    y_ctx, states_c = _rwkv_run(_rwkv_prepare(p_ctx0, p0, rw_cols), (zero, zero), p0)
    y_lat, _ = _rwkv_run(_rwkv_prepare(p_lat0, p0, rw_cols), states_c, p0)
    w_out0 = bf(l0_w_out)
    x = _outproj(oa_lat, y_lat, w_out0, x, g1)
    ctx_s = _outproj(oa_ctx, y_ctx, w_out0, ctx_s, cg1)
    w1, w2 = bf(l0_mlp_w1), bf(l0_mlp_w2)
    a2, b2 = _affine(l0_norm2_g, sh2, sc2)
    ca2, cb2 = _affine(l0_norm2_g, csh2, csc2)
    x = _mlp(x, a2, b2, g2, w1, w2)
    ctx_s = _mlp(ctx_s, ca2, cb2, cg2, w1, w2)

    p1 = dict(w_mod=l1_w_mod, b_mod=l1_b_mod, conv_w=l1_conv_w, conv_b=l1_conv_b, f_w1=l1_f_w1, f_b1=l1_f_b1,
              f_w2=l1_f_w2, f_b2=l1_f_b2, f_w3=l1_f_w3, f_b3=l1_f_b3, f_w4=l1_f_w4, f_freq=l1_f_freq,
              hy_bias=l1_hy_bias)
    (sh1, sc1, g1, sh2, sc2, g2), (csh1, csc1, _, _, _, _) = _mod_vectors(c, c_ctx, p1)
    hy_cols = l1_conv_w.shape[1]
    kv_cols = 2 * (2 * D_HEAD)
    w_in1 = bf(l1_w_in)
    a1, b1 = _affine(l1_norm1_g, sh1, sc1)
    ca1, cb1 = _affine(l1_norm1_g, csh1, csc1)
    p_lat = _inproj(x, a1, b1, w_in1)
    pkv_ctx = _inproj(ctx_s, ca1, cb1, w_in1[:, -kv_cols:])
    o_hy = _hyena(p_lat, p1)
    o_at = _gqa_attention(p_lat, pkv_ctx, hy_cols, l1_q_norm_g, l1_k_norm_g)
    x = _outproj(o_hy, o_at, bf(l1_w_out), x, g1)
    a2, b2 = _affine(l1_norm2_g, sh2, sc2)
    return _mlp(x, a2, b2, g2, bf(l1_mlp_w1), bf(l1_mlp_w2), final_g=final_g)
```

```python
import functools
import math

import numpy as np
import jax
import jax.numpy as jnp
from jax import lax
from jax.experimental import pallas as pl
from jax.experimental.pallas import tpu as pltpu

F32 = jnp.float32
BF16 = jnp.bfloat16
HI = lax.Precision.HIGHEST

NORM_EPS = 1e-6
ROPE_THETA = 10000.0
GRID_W = 64
N_MOD = 6
A_QK_DIM = 64
A_V_DIM = 128
B_HEAD = 64
B_DECAY_RANK = 64
B_ICL_RANK = 64
B_GATE_RANK = 160
B_GN_EPS = 64e-5
HY_EMB = 33
HY_ORDER = 64
HY_TARGET = 1e-2
HY_FAST_PCT = 0.3
HY_SLOW_PCT = 1.5
D_HEAD = 128
D_KV_GROUP = 4

LANE = 128
VMEM_LIMIT = 56 * 1024 * 1024
LOG2E = 1.4426950408889634


def _cp(sem, vmem=VMEM_LIMIT):
    return pltpu.CompilerParams(dimension_semantics=sem, vmem_limit_bytes=vmem)


def _pick(n, pref, step=8):
    t = max(step, min(n, pref) // step * step)
    while n % t:
        t -= step
    return t


def _mods_body(c_ref, w_ref, b_ref, o_ref):
    c = c_ref[...]
    s = c * jax.nn.sigmoid(c)
    o_ref[...] = jnp.dot(s, w_ref[...], preferred_element_type=F32, precision=HI) + b_ref[...]


def _mods(cvec, w_mod, b_mod):
    m, d = cvec.shape
    n = w_mod.shape[1]
    tn = _pick(n, 1024, LANE)
    return pl.pallas_call(
        _mods_body,
        grid=(n // tn,),
        in_specs=[pl.BlockSpec((m, d), lambda j: (0, 0)),
                  pl.BlockSpec((d, tn), lambda j: (0, j)),
                  pl.BlockSpec((1, tn), lambda j: (0, j))],
        out_specs=pl.BlockSpec((m, tn), lambda j: (0, j)),
        out_shape=jax.ShapeDtypeStruct((m, n), F32),
        compiler_params=_cp(("arbitrary",)),
        name="mods",
    )(cvec, w_mod, b_mod.reshape(1, n))


def _norm_mod(x, a, b):
    ms = jnp.mean(x * x, axis=-1, keepdims=True)
    return x * lax.rsqrt(ms + NORM_EPS) * a + b


def _inproj_body(x_ref, a_ref, b_ref, w_ref, o_ref, xn_ref):
    @pl.when(pl.program_id(2) == 0)
    def _():
        xn_ref[...] = _norm_mod(x_ref[0], a_ref[0], b_ref[0]).astype(BF16)

    o_ref[0] = jnp.dot(xn_ref[...], w_ref[...], preferred_element_type=F32).astype(o_ref.dtype)


def _inproj(x, a, b, w, out_dtype=BF16, tm_pref=1024, tn_pref=512):
    bsz, n, d = x.shape
    nn = w.shape[1]
    tm, tn = _pick(n, tm_pref), _pick(nn, tn_pref, LANE)
    return pl.pallas_call(
        _inproj_body,
        grid=(bsz, n // tm, nn // tn),
        in_specs=[pl.BlockSpec((1, tm, d), lambda bi, i, j: (bi, i, 0)),
                  pl.BlockSpec((1, 1, d), lambda bi, i, j: (bi, 0, 0)),
                  pl.BlockSpec((1, 1, d), lambda bi, i, j: (bi, 0, 0)),
                  pl.BlockSpec((d, tn), lambda bi, i, j: (0, j))],
        out_specs=pl.BlockSpec((1, tm, tn), lambda bi, i, j: (bi, i, j)),
        out_shape=jax.ShapeDtypeStruct((bsz, n, nn), out_dtype),
        scratch_shapes=[pltpu.VMEM((tm, d), BF16)],
        compiler_params=_cp(("parallel", "parallel", "arbitrary")),
        name="inproj",
    )(x, a, b, w)


def _outproj_body(oa_ref, ob_ref, wa_ref, wb_ref, x_ref, g_ref, y_ref):
    acc = jnp.dot(oa_ref[0].astype(BF16), wa_ref[...], preferred_element_type=F32)
    acc += jnp.dot(ob_ref[0].astype(BF16), wb_ref[...], preferred_element_type=F32)
    y_ref[0] = x_ref[0] + g_ref[0] * acc


def _outproj(oa, ob, w, x, g, tm_pref=1024, tn_pref=512):
    bsz, n, ka = oa.shape
    kb = ob.shape[2]
    d = w.shape[1]
    tm, tn = _pick(n, tm_pref), _pick(d, tn_pref, LANE)
    return pl.pallas_call(
        _outproj_body,
        grid=(bsz, n // tm, d // tn),
        in_specs=[pl.BlockSpec((1, tm, ka), lambda bi, i, j: (bi, i, 0)),
                  pl.BlockSpec((1, tm, kb), lambda bi, i, j: (bi, i, 0)),
                  pl.BlockSpec((ka, tn), lambda bi, i, j: (0, j)),
                  pl.BlockSpec((kb, tn), lambda bi, i, j: (0, j)),
                  pl.BlockSpec((1, tm, tn), lambda bi, i, j: (bi, i, j)),
                  pl.BlockSpec((1, 1, tn), lambda bi, i, j: (bi, 0, j))],
        out_specs=pl.BlockSpec((1, tm, tn), lambda bi, i, j: (bi, i, j)),
        out_shape=jax.ShapeDtypeStruct((bsz, n, d), F32),
        compiler_params=_cp(("parallel", "parallel", "parallel")),
        name="outproj",
    )(oa, ob, w[:ka], w[ka:], x, g)


def _mlp_body(x_ref, a_ref, b_ref, g_ref, w1_ref, w2_ref, fg_ref, y_ref, xn_ref, acc_ref, *, nf, final):
    f = pl.program_id(2)

    @pl.when(f == 0)
    def _():
        xn_ref[...] = _norm_mod(x_ref[0], a_ref[0], b_ref[0]).astype(BF16)
        acc_ref[...] = jnp.zeros_like(acc_ref)

    h = jnp.dot(xn_ref[...], w1_ref[...], preferred_element_type=F32)
    h = jnp.square(jnp.maximum(h, 0.0)).astype(BF16)
    acc_ref[...] += jnp.dot(h, w2_ref[...], preferred_element_type=F32)

    @pl.when(f == nf - 1)
    def _():
        y = x_ref[0] + g_ref[0] * acc_ref[...]
        if final:
            ms = jnp.mean(y * y, axis=-1, keepdims=True)
            y = y * lax.rsqrt(ms + NORM_EPS) * fg_ref[...]
        y_ref[0] = y


def _mlp(x, a, b, g, w1, w2, final_g=None, tm_pref=512, tf_pref=1024):
    bsz, n, d = x.shape
    dff = w1.shape[1]
    tm, tf = _pick(n, tm_pref), _pick(dff, tf_pref, LANE)
    nf = dff // tf
    final = final_g is not None
    fg = (final_g if final else jnp.ones((d,), F32)).reshape(1, d).astype(F32)
    return pl.pallas_call(
        functools.partial(_mlp_body, nf=nf, final=final),
        grid=(bsz, n // tm, nf),
        in_specs=[pl.BlockSpec((1, tm, d), lambda bi, i, f: (bi, i, 0)),
                  pl.BlockSpec((1, 1, d), lambda bi, i, f: (bi, 0, 0)),
                  pl.BlockSpec((1, 1, d), lambda bi, i, f: (bi, 0, 0)),
                  pl.BlockSpec((1, 1, d), lambda bi, i, f: (bi, 0, 0)),
                  pl.BlockSpec((d, tf), lambda bi, i, f: (0, f)),
                  pl.BlockSpec((tf, d), lambda bi, i, f: (f, 0)),
                  pl.BlockSpec((1, d), lambda bi, i, f: (0, 0))],
        out_specs=pl.BlockSpec((1, tm, d), lambda bi, i, f: (bi, i, 0)),
        out_shape=jax.ShapeDtypeStruct((bsz, n, d), F32),
        scratch_shapes=[pltpu.VMEM((tm, d), BF16), pltpu.VMEM((tm, d), F32)],
        compiler_params=_cp(("parallel", "parallel", "arbitrary")),
        name="mlp",
    )(x, a, b, g, w1, w2, fg)


def _rope_tables(n, dim, reps):
    rows = n // GRID_W
    row = jnp.repeat(jnp.arange(rows, dtype=F32), GRID_W)
    col = jnp.tile(jnp.arange(GRID_W, dtype=F32), rows)
    half = dim // 2
    inv = ROPE_THETA ** (-jnp.arange(0, half, 2, dtype=F32) / half)
    ar, ac = row[:, None] * inv, col[:, None] * inv
    cos = jnp.concatenate([jnp.cos(ar), jnp.cos(ar), jnp.cos(ac), jnp.cos(ac)], axis=-1)
    sin = jnp.concatenate([-jnp.sin(ar), jnp.sin(ar), -jnp.sin(ac), jnp.sin(ac)], axis=-1)
    return jnp.tile(cos, (1, reps)), jnp.tile(sin, (1, reps))


def _swap_matrix(dim, reps):
    q = dim // 4
    width = dim * reps
    p = np.zeros((width, width), np.float32)
    for j in range(width):
        base, r = (j // dim) * dim, j % dim
        axis, which, f = r // (2 * q), (r % (2 * q)) // q, r % q
        p[base + axis * 2 * q + (1 - which) * q + f, j] = 1.0
    return p


def _rope_norm(x, cos_ref, sin_ref, p_ref, g_ref, gs_ref, *, norm, rope, scale=1.0):
    y = x.astype(F32)
    if rope:
        ys = jnp.dot(x, p_ref[...], preferred_element_type=F32)
    if norm:
        rs = lax.rsqrt(jnp.mean(y * y, axis=-1, keepdims=True) + NORM_EPS)
        y = y * rs * g_ref[...]
        if rope:
            ys = ys * rs * gs_ref[...]
    if rope:
        y = y * cos_ref[...] + ys * sin_ref[...]
    return y * scale if scale != 1.0 else y


def _rope_operands(n, dim, gain, rope):
    reps = LANE // dim
    if rope:
        cos, sin = _rope_tables(n, dim, reps)
        pm = jnp.asarray(_swap_matrix(dim, reps), BF16)
    else:
        cos = sin = jnp.zeros((n, LANE), F32)
        pm = jnp.zeros((LANE, LANE), BF16)
    if gain is not None:
        g = jnp.tile(gain.astype(F32), reps).reshape(1, LANE)
        gs = jnp.dot(g, jnp.asarray(_swap_matrix(dim, reps)), precision=HI)
    else:
        g = gs = jnp.ones((1, LANE), F32)
    return cos, sin, pm, g, gs


def _kprep_body(x_ref, cos_ref, sin_ref, p_ref, g_ref, gs_ref, o_ref, *, heads, norm, rope):
    for h in range(heads):
        sl = slice(h * LANE, (h + 1) * LANE)
        y = _rope_norm(x_ref[0, :, sl], cos_ref, sin_ref, p_ref, g_ref, gs_ref, norm=norm, rope=rope)
        o_ref[0, :, sl] = y.astype(o_ref.dtype)


def _kprep(p, col0, heads, *, dim, gain=None, rope=True, tq_pref=512):
    bsz, n, _ = p.shape
    tq = _pick(n, tq_pref)
    hb = heads
    while col0 % (hb * LANE):
        hb //= 2
    w = hb * LANE
    cb = col0 // w
    cos, sin, pm, g, gs = _rope_operands(n, dim, gain, rope)
    return pl.pallas_call(
        functools.partial(_kprep_body, heads=hb, norm=gain is not None, rope=rope),
        grid=(bsz, n // tq, heads // hb),
        in_specs=[pl.BlockSpec((1, tq, w), lambda b, i, j: (b, i, cb + j)),
                  pl.BlockSpec((tq, LANE), lambda b, i, j: (i, 0)),
                  pl.BlockSpec((tq, LANE), lambda b, i, j: (i, 0)),
                  pl.BlockSpec((LANE, LANE), lambda b, i, j: (0, 0)),
                  pl.BlockSpec((1, LANE), lambda b, i, j: (0, 0)),
                  pl.BlockSpec((1, LANE), lambda b, i, j: (0, 0))],
        out_specs=pl.BlockSpec((1, tq, w), lambda b, i, j: (b, i, j)),
        out_shape=jax.ShapeDtypeStruct((bsz, n, heads * LANE), BF16),
        compiler_params=_cp(("parallel", "parallel", "parallel")),
        name="kprep",
    )(p, cos, sin, pm, g, gs)


FLASH_KEYS = 1280
FLASH_HEADROOM = 64.0


def _knorm_body(k_ref, o_ref):
    k = k_ref[0].astype(F32)
    nrm2 = jnp.max(jnp.sum(k * k, axis=1, keepdims=True), axis=0, keepdims=True)
    o_ref[0, 0] = jnp.broadcast_to(jnp.sqrt(nrm2), (8, LANE))


def _key_block_norms(k, kcol0, heads, ts):
    bsz, nk, _ = k.shape
    nsub = nk // ts
    kb = kcol0 // LANE
    out = pl.pallas_call(
        _knorm_body,
        grid=(bsz, heads, nsub),
        in_specs=[pl.BlockSpec((1, ts, LANE), lambda b, h, c: (b, c, kb + h))],
        out_specs=pl.BlockSpec((1, 1, 8, LANE), lambda b, h, c: (b, h, c, 0)),
        out_shape=jax.ShapeDtypeStruct((bsz, heads, nsub * 8, LANE), F32),
        compiler_params=_cp(("parallel", "parallel", "parallel")),
        name="key_block_norms",
    )(k)
    return out[:, :, ::8, 0].reshape(-1)


def _flash_body(kmax_ref, qr_ref, cos_ref, sin_ref, pm_ref, g_ref, gs_ref, k_ref, v_ref, e1_ref, e2_ref, o_ref,
                q_ref, m_ref, l_ref, acc_ref, thr_ref, *, ts, nsub, mode, tq, norm, rope, scale):
    base = (pl.program_id(0) * pl.num_programs(1) + pl.program_id(1)) * nsub
    prep = functools.partial(_rope_norm, cos_ref=cos_ref, sin_ref=sin_ref, p_ref=pm_ref, g_ref=g_ref, gs_ref=gs_ref,
                             norm=norm, rope=rope, scale=scale)
    if mode == "diff":
        y = prep(qr_ref[0])
        lane = lax.broadcasted_iota(jnp.int32, y.shape, 1)
        q_ref[:tq] = jnp.where(lane < A_QK_DIM, y, 0.0).astype(BF16)
        q_ref[tq:] = jnp.where(lane >= A_QK_DIM, y, 0.0).astype(BF16)
    else:
        for j in range(D_KV_GROUP):
            q_ref[j * tq:(j + 1) * tq] = prep(qr_ref[0, :, j * LANE:(j + 1) * LANE]).astype(BF16)
    q = q_ref[...]

    def scores(off, size):
        k = k_ref[0, pl.ds(off, size), :]
        return lax.dot_general(q, k, (((1,), (1,)), ((), ())), preferred_element_type=F32)

    def block(c):
        off = pl.multiple_of(c * ts, ts)
        return scores(off, ts), v_ref[0, pl.ds(off, ts), :]

    def wide(m):
        return jnp.concatenate([m] * (ts // LANE), axis=1)

    m0 = jnp.broadcast_to(jnp.max(scores(0, 2 * LANE), axis=1, keepdims=True), m_ref.shape)
    m_ref[...] = m0
    l_ref[...] = jnp.zeros_like(l_ref)
    acc_ref[...] = jnp.zeros_like(acc_ref)
    qf = q.astype(F32)
    qn = jnp.sqrt(jnp.sum(qf * qf, axis=1, keepdims=True))
    thr_ref[0] = jnp.min((m0[:, :1] + FLASH_HEADROOM) / qn)

    def step(c, carry):
        fixed = kmax_ref[base + c] * 1.001 <= thr_ref[0]

        @pl.when(fixed)
        def _():
            s, v = block(c)
            p = jnp.exp2(s - wide(m_ref[...]))
            l_ref[...] += jnp.sum(p, axis=1, keepdims=True)
            acc_ref[...] += jnp.dot(p.astype(BF16), v, preferred_element_type=F32)

        @pl.when(jnp.logical_not(fixed))
        def _():
            s, v = block(c)
            m_prev = m_ref[...]
            m_new = jnp.maximum(m_prev, jnp.max(s, axis=1, keepdims=True))
            alpha = jnp.exp2(m_prev - m_new)
            p = jnp.exp2(s - wide(m_new))
            l_ref[...] = alpha * l_ref[...] + jnp.sum(p, axis=1, keepdims=True)
            acc_ref[...] = alpha * acc_ref[...] + jnp.dot(p.astype(BF16), v, preferred_element_type=F32)
            m_ref[...] = m_new

        return carry

    lax.fori_loop(0, nsub, step, 0)

    o = acc_ref[...] / l_ref[...]
    if mode == "diff":
        d = o[:tq] - e1_ref[...] * o[tq:]
        ms = jnp.mean(d * d, axis=-1, keepdims=True)
        o_ref[0] = (d * lax.rsqrt(ms + NORM_EPS) * e2_ref[...]).astype(o_ref.dtype)
    else:
        for j in range(D_KV_GROUP):
            o_ref[0, :, j * LANE:(j + 1) * LANE] = o[j * tq:(j + 1) * tq].astype(o_ref.dtype)


def _flash(pq, qcol0, k, v, kcol0, vcol0, e1, e2, *, mode, hk, dim, tq, gain=None, rope=True):
    bsz, n, _ = pq.shape
    nk = k.shape[1]
    g = 2 if mode == "diff" else D_KV_GROUP
    mq = g * tq
    qw = LANE if mode == "diff" else D_KV_GROUP * LANE
    qb = qcol0 // qw
    ts = _pick(nk, FLASH_KEYS, 2 * LANE)
    kb, vb = kcol0 // LANE, vcol0 // LANE
    ow = LANE if mode == "diff" else D_KV_GROUP * LANE
    kmax = _key_block_norms(k, kcol0, hk, ts)
    cos, sin, pm, gq, gqs = _rope_operands(n, dim, gain, rope)
    const = lambda shape: pl.BlockSpec(shape, lambda b, h, i: (0, 0))
    return pl.pallas_call(
        functools.partial(_flash_body, ts=ts, nsub=nk // ts, mode=mode, tq=tq, norm=gain is not None, rope=rope,
                          scale=dim ** -0.5 * LOG2E),
        grid=(bsz, hk, n // tq),
        in_specs=[pl.BlockSpec(memory_space=pltpu.SMEM),
                  pl.BlockSpec((1, tq, qw), lambda b, h, i: (b, i, qb + h)),
                  pl.BlockSpec((tq, LANE), lambda b, h, i: (i, 0)),
                  pl.BlockSpec((tq, LANE), lambda b, h, i: (i, 0)),
                  const((LANE, LANE)), const((1, LANE)), const((1, LANE)),
                  pl.BlockSpec((1, nk, LANE), lambda b, h, i: (b, 0, kb + h)),
                  pl.BlockSpec((1, nk, LANE), lambda b, h, i: (b, 0, vb + h)),
                  const((1, LANE)), const((1, LANE))],
        out_specs=pl.BlockSpec((1, tq, ow), lambda b, h, i: (b, i, h)),
        out_shape=jax.ShapeDtypeStruct((bsz, n, hk * ow), BF16),
        scratch_shapes=[pltpu.VMEM((mq, LANE), BF16), pltpu.VMEM((mq, LANE), F32), pltpu.VMEM((mq, LANE), F32),
                        pltpu.VMEM((mq, LANE), F32), pltpu.SMEM((1,), F32)],
        compiler_params=_cp(("parallel", "parallel", "arbitrary")),
        name="flash_" + mode,
    )(kmax, pq, cos, sin, pm, gq, gqs, k, v, e1, e2)


def _diff_attention(pa_lat, pa_ctx, c0, aw, lam, lam_init, subln_g):
    n, nctx = pa_lat.shape[1], pa_ctx.shape[1]
    heads = aw // LANE
    k_l = _kprep(pa_lat, c0 + aw, heads, dim=A_QK_DIM)
    k_all = jnp.concatenate([k_l, pa_ctx[..., c0 + aw:c0 + 2 * aw]], axis=1)
    v_all = jnp.concatenate([pa_lat[..., c0 + 2 * aw:c0 + 3 * aw], pa_ctx[..., c0 + 2 * aw:c0 + 3 * aw]], axis=1)
    e1 = jnp.full((1, LANE), lam, F32)
    e2 = (subln_g.astype(F32) * (1.0 - lam_init)).reshape(1, LANE)
    o_lat = _flash(pa_lat, c0, k_all, v_all, 0, 0, e1, e2, mode="diff", hk=heads, dim=A_QK_DIM, tq=_pick(n, 1024))
    o_ctx = _flash(pa_ctx, c0, pa_ctx, pa_ctx, c0 + aw, c0 + 2 * aw, e1, e2, mode="diff", hk=heads, dim=A_QK_DIM,
                   tq=_pick(nctx, 512), rope=False)
    return o_lat, o_ctx


def _gqa_attention(p_lat, pkv_ctx, qcol0, q_norm_g, k_norm_g):
    n = p_lat.shape[1]
    hq, hk = 2 * D_KV_GROUP, 2
    kcol0 = qcol0 + hq * D_HEAD
    vcol0 = kcol0 + hk * D_HEAD
    k_l = _kprep(p_lat, kcol0, hk, dim=D_HEAD, gain=k_norm_g)
    k_c = _kprep(pkv_ctx, 0, hk, dim=D_HEAD, gain=k_norm_g, rope=False)
    k_all = jnp.concatenate([k_l, k_c], axis=1)
    v_all = jnp.concatenate([p_lat[..., vcol0:vcol0 + hk * D_HEAD], pkv_ctx[..., hk * D_HEAD:]], axis=1)
    dummy = jnp.zeros((1, LANE), F32)
    return _flash(p_lat, qcol0, k_all, v_all, 0, 0, dummy, dummy, mode="gqa", hk=hk, dim=D_HEAD,
                  tq=_pick(n, 512), gain=q_norm_g)


RW_CHUNK = 64
RW_HALO = 8


def _mm_hi(a, b):
    return jnp.dot(a, b, preferred_element_type=F32, precision=HI)


def _mm_bf(a, b):
    return jnp.dot(a.astype(BF16), b.astype(BF16), preferred_element_type=F32)


def _split3(x):
    x1 = x.astype(BF16)
    r1 = x - x1.astype(F32)
    x2 = r1.astype(BF16)
    return x1, x2, (r1 - x2.astype(F32)).astype(BF16)


def _mm_x3(a, b):
    ah, al, _ = _split3(a)
    bh, bl, _ = _split3(b)
    d = lambda u, w: jnp.dot(u, w, preferred_element_type=F32)
    return d(ah, bh) + (d(ah, bl) + d(al, bh))


def _mm_sel(z, sel):
    z1, z2, z3 = _split3(z)
    d = lambda u: jnp.dot(u, sel, preferred_element_type=F32)
    return d(z1) + (d(z2) + d(z3))


def _rwprep_body(x_ref, hp_ref, hn_ref, mup_ref, mun_ref, vec_ref, w2f_ref, w2b_ref, a2f_ref, a2b_ref,
                 g2_ref, sel_ref, selt_ref,
                 r_ref, v_ref, kk_ref, lwf_ref, bf_ref, kf_ref, lwb_ref, bb_ref, kb_ref, g_ref, bonus_ref,
                 *, nb, tm, bw):
    i = pl.program_id(1)
    x = x_ref[0].astype(F32)
    row = lax.broadcasted_iota(jnp.int32, x.shape, 0)
    prev_edge = jnp.where(i > 0, hp_ref[0, RW_HALO - 1:RW_HALO, :].astype(F32), 0.0)
    next_edge = jnp.where(i < nb - 1, hn_ref[0, 0:1, :].astype(F32), 0.0)
    prev = jnp.where(row == 0, prev_edge, pltpu.roll(x, 1, 0))
    nxt = jnp.where(row == tm - 1, next_edge, pltpu.roll(x, tm - 1, 0))
    xs = x + mup_ref[...] * (prev - x) + mun_ref[...] * (nxt - x)
    r, k, v = xs[:, :bw], xs[:, bw:2 * bw], xs[:, 2 * bw:3 * bw]
    lr = xs[:, 3 * bw:3 * bw + LANE]
    gd = xs[:, 3 * bw + LANE:3 * bw + 3 * LANE]
    k_k, k_a, r_k = vec_ref[0:1], vec_ref[1:2], vec_ref[2:3]
    w0f, a0f, w0b, a0b = vec_ref[3:4], vec_ref[4:5], vec_ref[5:6], vec_ref[6:7]
    sel, selt = sel_ref[...], selt_ref[...]

    def head_sum(z):
        return _mm_sel(_mm_sel(z, sel), selt)

    kk = k * k_k
    kk = kk * lax.rsqrt(head_sum(kk * kk) + 1e-12)
    g_ref[0] = _mm_bf(jax.nn.sigmoid(gd), g2_ref[...]).astype(g_ref.dtype)
    th = jnp.tanh(lr)
    ksum = jnp.zeros_like(k)
    for (w0, a0, w2_ref, a2_ref, lw_ref, b_ref, kd_ref) in (
            (w0f, a0f, w2f_ref, a2f_ref, lwf_ref, bf_ref, kf_ref),
            (w0b, a0b, w2b_ref, a2b_ref, lwb_ref, bb_ref, kb_ref)):
        logw = -math.exp(-0.5) * jax.nn.sigmoid(w0 + _mm_bf(th, w2_ref[...]))
        a = jax.nn.sigmoid(a0 + _mm_bf(lr, a2_ref[...]))
        k_d = k * (1.0 + (a - 1.0) * k_a)
        ksum = ksum + k_d
        lw_ref[0] = logw
        b_ref[0] = kk * a
        kd_ref[0] = k_d
    r_ref[0] = r
    v_ref[0] = v
    kk_ref[0] = kk
    bonus_ref[0] = head_sum(r * ksum * r_k) * v


def _rwkv_prepare(pr, p, cols):
    bsz, n, _ = pr.shape
    bw = p["w0_f"].shape[0]
    tm = _pick(n, 256)
    nb = n // tm
    hb = tm // RW_HALO
    assert cols - 3 * bw >= 3 * LANE and B_DECAY_RANK + B_ICL_RANK == LANE

    def padded(vec):
        return jnp.concatenate([vec.astype(F32), jnp.zeros((cols - vec.shape[0],), F32)]).reshape(1, cols)

    vecs = jnp.stack([p["k_k"], p["k_a"], p["r_k"].reshape(-1), p["w0_f"], p["a0_f"], p["w0_b"], p["a0_b"],
                      jnp.zeros((bw,), F32)]).astype(F32)
    zr = jnp.zeros((B_DECAY_RANK, bw), F32)
    w2 = {d: jnp.concatenate([p["w2_" + d].astype(F32), zr], axis=0).astype(BF16) for d in "fb"}
    a2 = {d: jnp.concatenate([zr, p["a2_" + d].astype(F32)], axis=0).astype(BF16) for d in "fb"}
    g2 = jnp.concatenate([p["g2"].astype(F32), jnp.zeros((2 * LANE - B_GATE_RANK, bw), F32)], axis=0).astype(BF16)
    sel, selt = _head_selectors(bw)

    full = lambda shape: pl.BlockSpec(shape, lambda b, i: tuple(0 for _ in shape))
    tok = pl.BlockSpec((1, tm, bw), lambda b, i: (b, i, 0))
    tok_f32 = jax.ShapeDtypeStruct((bsz, n, bw), F32)
    outs = pl.pallas_call(
        functools.partial(_rwprep_body, nb=nb, tm=tm, bw=bw),
        grid=(bsz, nb),
        in_specs=[pl.BlockSpec((1, tm, cols), lambda b, i: (b, i, 0)),
                  pl.BlockSpec((1, RW_HALO, cols), lambda b, i: (b, jnp.maximum(i * hb - 1, 0), 0)),
                  pl.BlockSpec((1, RW_HALO, cols), lambda b, i: (b, jnp.minimum((i + 1) * hb, nb * hb - 1), 0)),
                  full((1, cols)), full((1, cols)), full((8, bw)),
                  full((LANE, bw)), full((LANE, bw)), full((LANE, bw)), full((LANE, bw)),
                  full((2 * LANE, bw)), full((bw, LANE)), full((LANE, bw))],
        out_specs=[tok] * 11,
        out_shape=[tok_f32] * 9 + [jax.ShapeDtypeStruct((bsz, n, bw), BF16), tok_f32],
        compiler_params=_cp(("parallel", "parallel")),
        name="rwkv_prepare",
    )(pr, pr, pr, padded(p["mu_prev"]), padded(p["mu_next"]), vecs, w2["f"], w2["b"], a2["f"], a2["b"],
      g2, sel, selt)
    r, v, kk, lwf, bf, kf, lwb, bb, kb, g, bonus = outs
    return dict(r=r, v=v, kk=kk, g=g, bonus=bonus, f=(lwf, bf, kf), b=(lwb, bb, kb))


RW_GROUP = 4


def _bd(x):
    t = jnp.concatenate([x.astype(BF16)] * RW_GROUP, axis=1)
    r = lax.broadcasted_iota(jnp.int32, t.shape, 1) // B_HEAD
    c = lax.broadcasted_iota(jnp.int32, t.shape, 2) // B_HEAD
    return jnp.where(r == c, t, jnp.zeros_like(t))


def _bdot(a, b, dims):
    return jnp.einsum(dims, a.astype(BF16), b.astype(BF16), preferred_element_type=F32)


def _groups(x):
    gw = RW_GROUP * B_HEAD
    return jnp.stack([x[:, g * gw:(g + 1) * gw] for g in range(x.shape[1] // gw)])


def _rw_decays(r, kk, b, kd, lw, reverse):
    ck = r.shape[0]
    ti = lax.broadcasted_iota(jnp.int32, (ck, ck), 0)
    si = lax.broadcasted_iota(jnp.int32, (ck, ck), 1)
    tri = ((si >= ti) if reverse else (si <= ti)).astype(BF16)
    l1, l2, l3 = _split3(lw)
    tdot = lambda z: jnp.dot(tri, z, preferred_element_type=F32)
    cum = tdot(l1) + (tdot(l2) + tdot(l3))
    total = cum[0:1] if reverse else cum[ck - 1:ck]
    winv = jnp.exp(-cum)
    wrest = jnp.exp(total - cum)
    return (-kk * jnp.exp(cum - lw), b * winv, kd * winv, r * jnp.exp(cum), b * wrest, kd * wrest, jnp.exp(total))


def _rw_chunks(fwd, bwd, s_f, s_b):
    nseq = len(fwd)
    ck, w = fwd[0][0].shape
    gw = RW_GROUP * B_HEAD
    ng = w // gw
    parts = []
    for probs, rev in ((fwd, False), (bwd, True)):
        for (r, v, kk, b, kd, lw) in probs:
            parts.append([_groups(z) for z in _rw_decays(r, kk, b, kd, lw, rev)] + [_groups(v)])
    at, bt, kt, rt, btw, ktw, decay, v = [jnp.concatenate(zs, axis=0) for zs in zip(*parts)]
    s = jnp.concatenate([_groups(z) for z in list(s_f) + list(s_b)], axis=0)
    nslab = 2 * nseq * ng
    shape = (nslab, ck, gw)
    lag = lax.broadcasted_iota(jnp.int32, shape, 1) - lax.broadcasted_iota(jnp.int32, shape, 2) % ck
    lag = jnp.where(lax.broadcasted_iota(jnp.int32, shape, 0) >= nseq * ng, -lag, lag)
    incl, strict = lag >= 0, lag > 0
    eye = (lag == 0).astype(F32)

    bdv = _bd(v)
    lhs = jnp.concatenate([at, rt], axis=1)
    lb, lk = _bdot(lhs, _bd(bt), "gtk,gnk->gtn"), _bdot(lhs, _bd(kt), "gtk,gnk->gtn")
    lab, mrb = jnp.where(strict, lb[:, :ck], 0.0), jnp.where(incl, lb[:, ck:], 0.0)
    lak, mrk = jnp.where(strict, lk[:, :ck], 0.0), jnp.where(incl, lk[:, ck:], 0.0)
    tinv = eye + lab
    pw = _bdot(lab, _bd(lab), "gtk,gkn->gtn")
    span = 4
    while span < ck:
        both = _bdot(jnp.concatenate([tinv, pw], axis=1), _bd(pw), "gtk,gkn->gtn")
        tinv, pw = tinv + both[:, :ck], both[:, ck:]
        span *= 2
    tinv = tinv + _bdot(tinv, _bd(pw), "gtk,gkn->gtn")
    vmix = _bdot(jnp.concatenate([lak, mrk], axis=1), bdv, "gtk,gkn->gtn")
    p1 = _bdot(tinv, _bd(at), "gtk,gkn->gtn")
    q1 = _bdot(tinv, _bd(vmix[:, :ck]), "gtk,gkn->gtn")
    smix = _bdot(jnp.concatenate([p1, rt], axis=1), _bd(s), "gtk,gnk->gtn")
    u = smix[:, :ck] + q1
    y = smix[:, ck:] + _bdot(mrb, _bd(u), "gtk,gkn->gtn") + vmix[:, ck:]
    z = _bdot(jnp.concatenate([u, v], axis=1), jnp.concatenate([btw, ktw], axis=1), "gtm,gtn->gmn")
    lane_head = lax.broadcasted_iota(jnp.int32, (nslab, B_HEAD, gw), 2) // B_HEAD
    upd = jnp.zeros_like(s)
    for h in range(RW_GROUP):
        upd = upd + jnp.where(lane_head == h, z[:, h * B_HEAD:(h + 1) * B_HEAD], 0.0)
    s_new = s * decay + upd
    lanes = lambda x, q: jnp.concatenate([x[q * ng + g] for g in range(ng)], axis=1)
    seqs = range(nseq)
    return ([lanes(y, q) for q in seqs], [lanes(y, nseq + q) for q in seqs],
            [lanes(s_new, q) for q in seqs], [lanes(s_new, nseq + q) for q in seqs])


def _rwscan_body(rf_ref, vf_ref, kkf_ref, bf_ref, kdf_ref, lwf_ref, rb_ref, vb_ref, kkb_ref, bb_ref, kdb_ref, lwb_ref,
                 s0f_ref, s0b_ref, yf_ref, yb_ref, sTf_ref, sTb_ref, sf_ref, sb_ref, *, nc, nseq):
    c = pl.program_id(0)

    @pl.when(c == 0)
    def _():
        sf_ref[...] = s0f_ref[...]
        sb_ref[...] = s0b_ref[...]

    seqs = range(nseq)
    yf, yb, sf, sb = _rw_chunks(
        [(rf_ref[q], vf_ref[q], kkf_ref[q], bf_ref[q], kdf_ref[q], lwf_ref[q]) for q in seqs],
        [(rb_ref[q], vb_ref[q], kkb_ref[q], bb_ref[q], kdb_ref[q], lwb_ref[q]) for q in seqs],
        [sf_ref[q] for q in seqs], [sb_ref[q] for q in seqs])
    for q in seqs:
        yf_ref[q] = yf[q]
        yb_ref[q] = yb[q]
        sf_ref[q] = sf[q]
        sb_ref[q] = sb[q]

    @pl.when(c == nc - 1)
    def _():
        for q in seqs:
            sTf_ref[q] = sf[q]
            sTb_ref[q] = sb[q]


def _rwkv_scan(r, v, kk, fwd, bwd, s0f, s0b):
    bsz, n, w = r.shape
    ck = RW_CHUNK
    assert n % ck == 0 and ck == B_HEAD and w % (RW_GROUP * B_HEAD) == 0
    nc = n // ck
    seq_f = pl.BlockSpec((bsz, ck, w), lambda c: (0, c, 0))
    seq_b = pl.BlockSpec((bsz, ck, w), lambda c: (0, nc - 1 - c, 0))
    st = pl.BlockSpec((bsz, B_HEAD, w), lambda c: (0, 0, 0))
    y_shape = jax.ShapeDtypeStruct((bsz, n, w), F32)
    s_shape = jax.ShapeDtypeStruct((bsz, B_HEAD, w), F32)
    (lwf, bf, kf), (lwb, bb, kb) = fwd, bwd
    return pl.pallas_call(
        functools.partial(_rwscan_body, nc=nc, nseq=bsz),
        grid=(nc,),
        in_specs=[seq_f] * 6 + [seq_b] * 6 + [st, st],
        out_specs=[seq_f, seq_b, st, st],
        out_shape=[y_shape, y_shape, s_shape, s_shape],
        scratch_shapes=[pltpu.VMEM((bsz, B_HEAD, w), F32), pltpu.VMEM((bsz, B_HEAD, w), F32)],
        compiler_params=_cp(("arbitrary",)),
        name="rwkv_scan",
    )(r, v, kk, bf, kf, lwf, r, v, kk, bb, kb, lwb, s0f, s0b)


def _rwpost_body(yf_ref, yb_ref, g_ref, bonus_ref, lg_ref, lb_ref, sel_ref, selt_ref, o_ref):
    sel, selt = sel_ref[...], selt_ref[...]
    head_mean = lambda z: _mm_sel(_mm_sel(z, sel), selt) * (1.0 / B_HEAD)
    y = yf_ref[0] + yb_ref[0]
    d = y - head_mean(y)
    y = d * lax.rsqrt(head_mean(d * d) + B_GN_EPS)
    y = y * lg_ref[...] + lb_ref[...] + bonus_ref[0]
    o_ref[0] = (y * g_ref[0].astype(F32)).astype(o_ref.dtype)


def _head_selectors(bw):
    sel_np = np.zeros((bw, LANE), np.float32)
    sel_np[np.arange(bw), np.arange(bw) // B_HEAD] = 1.0
    return jnp.asarray(sel_np, BF16), jnp.asarray(sel_np.T, BF16)


def _rwkv_post(y_f, y_b, g, bonus, lnx_g, lnx_b):
    bsz, n, bw = y_f.shape
    tm = _pick(n, 256)
    tok = pl.BlockSpec((1, tm, bw), lambda b, i: (b, i, 0))
    vec = pl.BlockSpec((1, bw), lambda b, i: (0, 0))
    sel, selt = _head_selectors(bw)
    return pl.pallas_call(
        _rwpost_body,
        grid=(bsz, n // tm),
        in_specs=[tok, tok, tok, tok, vec, vec,
                  pl.BlockSpec((bw, LANE), lambda b, i: (0, 0)), pl.BlockSpec((LANE, bw), lambda b, i: (0, 0))],
        out_specs=tok,
        out_shape=jax.ShapeDtypeStruct((bsz, n, bw), BF16),
        compiler_params=_cp(("parallel", "parallel")),
        name="rwkv_post",
    )(y_f, y_b, g, bonus, lnx_g.reshape(1, bw).astype(F32), lnx_b.reshape(1, bw).astype(F32), sel, selt)


def _rwkv_run(prep, states0, p):
    y_f, y_b, s_f, s_b = _rwkv_scan(prep["r"], prep["v"], prep["kk"], prep["f"], prep["b"], states0[0], states0[1])
    return _rwkv_post(y_f, y_b, prep["g"], prep["bonus"], p["lnx_g"], p["lnx_b"]), (s_f, s_b)


HY_N2 = 128
HY_HALO = 8


def _hypre_body(x_ref, hp_ref, hn_ref, cw_ref, cb_ref, x0_ref, uv_ref, *, nb, tm, cw):
    i = pl.program_id(1)
    x = x_ref[0].astype(F32)
    row = lax.broadcasted_iota(jnp.int32, x.shape, 0)
    prev_edge = jnp.where(i > 0, hp_ref[0, HY_HALO - 1:HY_HALO, :].astype(F32), 0.0)
    next_edge = jnp.where(i < nb - 1, hn_ref[0, 0:1, :].astype(F32), 0.0)
    prev = jnp.where(row == 0, prev_edge, pltpu.roll(x, 1, 0))
    nxt = jnp.where(row == tm - 1, next_edge, pltpu.roll(x, tm - 1, 0))
    u = prev * cw_ref[0:1] + x * cw_ref[1:2] + nxt * cw_ref[2:3] + cb_ref[...]
    x0_ref[0] = u[:, :cw]
    uv_ref[0] = u[:, 2 * cw:3 * cw] * u[:, cw:2 * cw]


def _hyena_pre(p, conv_w, conv_b):
    bsz, n, _ = p.shape
    hc = conv_w.shape[1]
    cw = hc // 3
    tm = _pick(n, 256)
    nb = n // tm
    hb = tm // HY_HALO
    cwp = jnp.concatenate([conv_w.astype(F32), jnp.zeros((5, hc), F32)], axis=0)
    out = pl.BlockSpec((1, tm, cw), lambda b, i: (b, i, 0))
    return pl.pallas_call(
        functools.partial(_hypre_body, nb=nb, tm=tm, cw=cw),
        grid=(bsz, nb),
        in_specs=[pl.BlockSpec((1, tm, hc), lambda b, i: (b, i, 0)),
                  pl.BlockSpec((1, HY_HALO, hc), lambda b, i: (b, jnp.maximum(i * hb - 1, 0), 0)),
                  pl.BlockSpec((1, HY_HALO, hc), lambda b, i: (b, jnp.minimum((i + 1) * hb, nb * hb - 1), 0)),
                  pl.BlockSpec((8, hc), lambda b, i: (0, 0)),
                  pl.BlockSpec((1, hc), lambda b, i: (0, 0))],
        out_specs=[out, out],
        out_shape=[jax.ShapeDtypeStruct((bsz, n, cw), F32)] * 2,
        compiler_params=_cp(("parallel", "parallel")),
        name="hyena_pre",
    )(p, p, p, cwp, conv_b.reshape(1, hc).astype(F32))


def _hyfilt_body(z_ref, t_ref, z0_ref, w1_ref, w2_ref, w3_ref, w4_ref, w4b_ref, vec_ref, dl_ref,
                 k_ref, s_ref, *, half_tiles):
    i = pl.program_id(0)
    b1, b2, b3, fr = vec_ref[0:1], vec_ref[1:2], vec_ref[2:3], vec_ref[3:4]

    def mlp3(z):
        h = jnp.sin(fr * (_mm_hi(z, w1_ref[...]) + b1))
        h = jnp.sin(fr * (_mm_hi(h, w2_ref[...]) + b2))
        h = jnp.sin(fr * (_mm_hi(h, w3_ref[...]) + b3))
        lane = lax.broadcasted_iota(jnp.int32, h.shape, 1)
        return jnp.where(lane < HY_ORDER, h, 0.0), jnp.where(lane >= HY_ORDER, h, 0.0)

    h_lo, h_hi = mlp3(z_ref[...])
    raw = jnp.concatenate([_mm_hi(h_lo, w4_ref[...]), _mm_hi(h_hi, w4_ref[...])], axis=0)
    raw = raw * jnp.exp(-t_ref[...] * dl_ref[...])

    @pl.when(i == 0)
    def _():
        s_ref[...] = jnp.zeros_like(s_ref)

    s_ref[...] += jnp.sum(jnp.abs(raw), axis=0, keepdims=True)
    hb0 = _mm_hi(mlp3(z0_ref[...])[0], w4b_ref[...])
    row = lax.broadcasted_iota(jnp.int32, raw.shape, 0)
    raw = raw + jnp.where((row == 0) & (i == 0), hb0[0:1], 0.0)
    k_ref[...] = jnp.where((row == 0) & (i == half_tiles), 0.0, raw)


def _hyena_kernel_taps(n, p):
    cw = p["f_w4"].shape[1] // 2
    f32 = lambda a: a.astype(F32)
    row = jnp.arange(2 * n, dtype=jnp.int32)
    pos = jnp.where(row < n, row, (2 * n - row) % n).astype(F32)[:, None]
    tt = pos * jnp.float32(1.0 / (n - 1))
    bands = (HY_EMB - 1) // 2
    ang = (2 * math.pi / n) * pos * jnp.linspace(1e-4, bands - 1, bands, dtype=F32)[None]
    zz = jnp.concatenate([tt, jnp.cos(ang), -jnp.sin(ang), jnp.zeros((2 * n, HY_ORDER - HY_EMB), F32)], axis=-1)
    w1 = jnp.concatenate([f32(p["f_w1"]), jnp.zeros((HY_ORDER - HY_EMB, HY_ORDER), F32)], axis=0)
    twice = lambda w: jnp.kron(jnp.eye(2, dtype=F32), f32(w))
    vecs = jnp.stack([jnp.tile(f32(p[k]), 2) for k in ("f_b1", "f_b2", "f_b3", "f_freq")]
                     + [jnp.zeros((2 * HY_ORDER,), F32)] * 4)
    deltas = jnp.abs(jnp.linspace(math.log(HY_TARGET) / HY_SLOW_PCT, math.log(HY_TARGET) / HY_FAST_PCT, cw,
                                  dtype=F32)).reshape(1, cw)
    tr = _pick(n, 512)
    half_tiles = n // tr
    pk = 2 * HY_ORDER
    zp = zz.reshape(-1, 2, tr // 2, HY_ORDER).transpose(0, 2, 1, 3).reshape(-1, pk)
    full = lambda shape: pl.BlockSpec(shape, lambda i: tuple(0 for _ in shape))
    w4 = jnp.concatenate([f32(p["f_w4"])] * 2, axis=0)
    return pl.pallas_call(
        functools.partial(_hyfilt_body, half_tiles=half_tiles),
        grid=(2 * half_tiles,),
        in_specs=[pl.BlockSpec((tr // 2, pk), lambda i: (i, 0)),
                  pl.BlockSpec((tr, 1), lambda i: (i, 0)),
                  full((8, pk)), full((pk, pk)), full((pk, pk)), full((pk, pk)),
                  pl.BlockSpec((pk, cw), lambda i: (0, i // half_tiles)),
                  pl.BlockSpec((pk, cw), lambda i: (0, 1)),
                  full((8, pk)), full((1, cw))],
        out_specs=[pl.BlockSpec((tr, cw), lambda i: (i, 0)), pl.BlockSpec((1, cw), lambda i: (0, 0))],
        out_shape=[jax.ShapeDtypeStruct((2 * n, cw), F32), jax.ShapeDtypeStruct((1, cw), F32)],
        compiler_params=_cp(("arbitrary",)),
        name="hyena_filter",
    )(zp, tt, jnp.broadcast_to(jnp.tile(zz[0:1], (1, 2)), (8, pk)), twice(w1), twice(p["f_w2"]), twice(p["f_w3"]),
      w4, w4, vecs, deltas)


def _dft_consts(n):
    nn = 2 * n
    n2 = HY_N2
    n1 = nn // n2
    n1h = n1 // 2 + 1
    nf = -(-n1h // 8) * 8
    live = (np.arange(nf) < n1h).astype(np.float64)
    a1 = 2 * np.pi * np.outer(np.arange(nf), np.arange(n1)) / n1
    f1 = np.concatenate([np.cos(a1), -np.sin(a1)], axis=0) * np.tile(live, 2)[:, None]
    wgt = live * np.where((np.arange(nf) == 0) | (np.arange(nf) == n1 // 2), 1.0, 2.0)
    f1inv = np.concatenate([np.cos(a1.T) * wgt, -np.sin(a1.T) * wgt], axis=1) / nn
    a2 = 2 * np.pi * np.outer(np.arange(n2), np.arange(n2)) / n2
    c2, s2 = np.cos(a2), -np.sin(a2)
    m2 = np.block([[c2, -s2], [s2, c2]])
    m2inv = np.block([[c2, s2], [-s2, c2]])
    at = 2 * np.pi * np.outer(np.arange(n2), np.arange(nf)) / nn
    tw = np.stack([np.cos(at), -np.sin(at)])
    c = lambda a: jnp.asarray(a.astype(np.float32))
    return dict(n1=n1, n2=n2, nf=nf, f1=c(f1), f1inv=c(f1inv), m2=c(m2), m2inv=c(m2inv),
                tw_s2=c(tw[:, :, :, None]),
                tw_f1=c(np.transpose(tw, (0, 2, 1))[:, :, :, None]))


HY_S2 = 8
HY_CTILE = 512


def _dft1_body(f_ref, x_ref, tw_ref, o_ref, *, nf):
    x = jnp.swapaxes(x_ref[0], 0, 1)
    fh, fl, _ = _split3(f_ref[...])
    d = lambda u, w: jnp.dot(u, w, preferred_element_type=F32)
    re, im = [], []
    for jj in range(HY_S2):
        xh, xl, _ = _split3(x[jj])
        a = d(fh, xh) + (d(fh, xl) + d(fl, xh))
        ar, ai = a[:nf], a[nf:]
        twr, twi = tw_ref[0, jj], tw_ref[1, jj]
        re.append(ar * twr - ai * twi)
        im.append(ar * twi + ai * twr)
    o_ref[0, 0] = jnp.swapaxes(jnp.stack(re), 0, 1)
    o_ref[0, 1] = jnp.swapaxes(jnp.stack(im), 0, 1)


def _dft_stage1(x, consts, rows):
    bsz, _, cw = x.shape
    n2, nf = consts["n2"], consts["nf"]
    xv = x.reshape(bsz, rows, n2, cw)
    f1 = consts["f1"][:, :rows]
    tc = _pick(cw, HY_CTILE, LANE)
    return pl.pallas_call(
        functools.partial(_dft1_body, nf=nf),
        grid=(bsz, n2 // HY_S2, cw // tc),
        in_specs=[pl.BlockSpec((2 * nf, rows), lambda b, j, c: (0, 0)),
                  pl.BlockSpec((1, rows, HY_S2, tc), lambda b, j, c: (b, 0, j, c)),
                  pl.BlockSpec((2, HY_S2, nf, 1), lambda b, j, c: (0, j, 0, 0))],
        out_specs=pl.BlockSpec((1, 2, nf, HY_S2, tc), lambda b, j, c: (b, 0, 0, j, c)),
        out_shape=jax.ShapeDtypeStruct((bsz, 2, nf, n2, cw), F32),
        compiler_params=_cp(("parallel", "parallel", "parallel")),
        name="hyena_dft1",
    )(f1, xv, consts["tw_s2"])


def _dftmid_body(a_ref, m2_ref, m2i_ref, h_ref, tw_ref, is_ref, o_ref, *, n2, filt):
    a = a_ref[0, :, 0].reshape(2 * n2, a_ref.shape[-1])
    x = _mm_x3(m2_ref[...], a)
    xr, xi = x[:n2], x[n2:]
    if filt:
        o_ref[0, 0, 0] = xr * is_ref[...]
        o_ref[0, 1, 0] = xi * is_ref[...]
        return
    hr, hi = h_ref[0, 0, 0], h_ref[0, 1, 0]
    y = jnp.concatenate([xr * hr - xi * hi, xr * hi + xi * hr], axis=0)
    z = _mm_x3(m2i_ref[...], y)
    zr, zi = z[:n2], z[n2:]
    twr, twi = tw_ref[0, 0], tw_ref[1, 0]
    o_ref[0, 0, 0] = zr * twr + zi * twi
    o_ref[0, 1, 0] = zi * twr - zr * twi


def _dft_mid(a, h, inv_s, consts, filt):
    bsz = a.shape[0]
    n1, n2 = consts["nf"], consts["n2"]
    cw = a.shape[-1]
    av = a
    blk = pl.BlockSpec((1, 2, 1, n2, cw), lambda b, f: (b, 0, f, 0, 0))
    hblk = pl.BlockSpec((1, 2, 1, n2, cw), (lambda b, f: (0, 0, 0, 0, 0)) if filt else (lambda b, f: (0, 0, f, 0, 0)))
    return pl.pallas_call(
        functools.partial(_dftmid_body, n2=n2, filt=filt),
        grid=(bsz, n1),
        in_specs=[blk,
                  pl.BlockSpec((2 * n2, 2 * n2), lambda b, f: (0, 0)),
                  pl.BlockSpec((2 * n2, 2 * n2), lambda b, f: (0, 0)),
                  hblk,
                  pl.BlockSpec((2, 1, n2, 1), lambda b, f: (0, f, 0, 0)),
                  pl.BlockSpec((1, cw), lambda b, f: (0, 0))],
        out_specs=blk,
        out_shape=jax.ShapeDtypeStruct((bsz, 2, n1, n2, cw), F32),
        compiler_params=_cp(("parallel", "parallel")),
        name="hyena_dftmid_" + ("filter" if filt else "conv"),
    )(av, consts["m2"], consts["m2inv"], h, consts["tw_f1"], inv_s)


def _dft3_body(f_ref, z_ref, x0_ref, uv_ref, bias_ref, o_ref):
    zr, zi = jnp.swapaxes(z_ref[0, 0], 0, 1), jnp.swapaxes(z_ref[0, 1], 0, 1)
    x0, uv = jnp.swapaxes(x0_ref[0], 0, 1), jnp.swapaxes(uv_ref[0], 0, 1)
    fh, fl, _ = _split3(f_ref[...])
    d = lambda u, w: jnp.dot(u, w, preferred_element_type=F32)
    out = []
    for jj in range(HY_S2):
        zh, zl, _ = _split3(jnp.concatenate([zr[jj], zi[jj]], axis=0))
        y = d(fh, zh) + (d(fh, zl) + d(fl, zh))
        out.append(x0[jj] * (y + uv[jj] * bias_ref[...]))
    o_ref[0] = jnp.swapaxes(jnp.stack(out), 0, 1)


def _dft_stage1_inv(z, x0, uv, bias, consts):
    bsz, n, cw = x0.shape
    nf, n2 = consts["nf"], consts["n2"]
    hr = consts["n1"] // 2
    tokv = lambda t: t.reshape(bsz, hr, n2, cw)
    tc = _pick(cw, HY_CTILE, LANE)
    tile = pl.BlockSpec((1, hr, HY_S2, tc), lambda b, j, c: (b, 0, j, c))
    out = pl.pallas_call(
        _dft3_body,
        grid=(bsz, n2 // HY_S2, cw // tc),
        in_specs=[pl.BlockSpec((hr, 2 * nf), lambda b, j, c: (0, 0)),
                  pl.BlockSpec((1, 2, nf, HY_S2, tc), lambda b, j, c: (b, 0, 0, j, c)),
                  tile, tile,
                  pl.BlockSpec((1, tc), lambda b, j, c: (0, c))],
        out_specs=tile,
        out_shape=jax.ShapeDtypeStruct((bsz, hr, n2, cw), F32),
        compiler_params=_cp(("parallel", "parallel", "parallel")),
        name="hyena_dft3",
    )(consts["f1inv"][:hr], z, tokv(x0), tokv(uv), bias.reshape(1, cw).astype(F32))
    return out.reshape(bsz, n, cw)


def _hyena(p_lat, p):
    n = p_lat.shape[1]
    consts = _dft_consts(n)
    n1 = consts["n1"]
    x0, uv = _hyena_pre(p_lat, p["conv_w"], p["conv_b"])
    taps, sabs = _hyena_kernel_taps(n, p)
    cw = taps.shape[1]
    ones = jnp.ones((1, cw), F32)
    hk = _dft_stage1(taps[None], consts, n1)
    hspec = _dft_mid(hk, jnp.zeros((1, 2, 1, HY_N2, cw), F32), 1.0 / sabs, consts, True)
    a = _dft_stage1(uv, consts, n1 // 2)
    z = _dft_mid(a, hspec, ones, consts, False)
    return _dft_stage1_inv(z, x0, uv, p["hy_bias"], consts)


def _mod_vectors(c, c_ctx, p):
    bsz, d = c.shape
    cvec = jnp.concatenate([c, c_ctx[None], jnp.zeros((8 - bsz - 1, d), c.dtype)], axis=0).astype(F32)
    m = _mods(cvec, p["w_mod"].astype(F32), p["b_mod"].astype(F32))
    lat = [v[:, None, :] for v in jnp.split(m[:bsz], N_MOD, axis=-1)]
    cx = [jnp.broadcast_to(v[:, None, :], (bsz, 1, d)) for v in jnp.split(m[bsz:bsz + 1], N_MOD, axis=-1)]
    return lat, cx


def _affine(norm_g, shift, scale):
    return norm_g.astype(F32) * (1.0 + scale), shift


def kernel(x, c, ctx, c_ctx,
           l0_w_mod, l0_b_mod, l0_norm1_g, l0_norm2_g, l0_w_in, l0_lam_q1, l0_lam_k1, l0_lam_q2, l0_lam_k2,
           l0_subln_g, l0_mu_prev, l0_mu_next, l0_w0_f, l0_w2_f, l0_a0_f, l0_a2_f, l0_w0_b, l0_w2_b, l0_a0_b,
           l0_a2_b, l0_g2, l0_k_k, l0_k_a, l0_r_k, l0_lnx_g, l0_lnx_b, l0_w_out, l0_mlp_w1, l0_mlp_w2,
           l1_w_mod, l1_b_mod, l1_norm1_g, l1_norm2_g, l1_w_in, l1_conv_w, l1_conv_b, l1_f_w1, l1_f_b1,
           l1_f_w2, l1_f_b2, l1_f_w3, l1_f_b3, l1_f_w4, l1_f_freq, l1_hy_bias, l1_q_norm_g, l1_k_norm_g,
           l1_w_out, l1_mlp_w1, l1_mlp_w2, final_g):
    bsz, n, d = x.shape
    x = x.astype(F32)
    ctx_s = ctx.astype(F32)
    bf = lambda w: w.astype(BF16)

    p0 = dict(w_mod=l0_w_mod, b_mod=l0_b_mod, mu_prev=l0_mu_prev, mu_next=l0_mu_next, w0_f=l0_w0_f, w2_f=l0_w2_f,
              a0_f=l0_a0_f, a2_f=l0_a2_f, w0_b=l0_w0_b, w2_b=l0_w2_b, a0_b=l0_a0_b, a2_b=l0_a2_b, g2=l0_g2,
              k_k=l0_k_k, k_a=l0_k_a, r_k=l0_r_k, lnx_g=l0_lnx_g, lnx_b=l0_lnx_b)
    (sh1, sc1, g1, sh2, sc2, g2), (csh1, csc1, cg1, csh2, csc2, cg2) = _mod_vectors(c, c_ctx, p0)
    aw = d // 2
    w_rw = l0_w_in[:, 3 * aw:]
    rw_cols = -(-(w_rw.shape[1] + 2 * LANE - B_GATE_RANK) // 512) * 512
    w_in0 = bf(jnp.concatenate([w_rw, jnp.zeros((d, rw_cols - w_rw.shape[1]), w_rw.dtype), l0_w_in[:, :3 * aw]],
                               axis=1))
    a1, b1 = _affine(l0_norm1_g, sh1, sc1)
    ca1, cb1 = _affine(l0_norm1_g, csh1, csc1)
    p_lat0 = _inproj(x, a1, b1, w_in0)
    p_ctx0 = _inproj(ctx_s, ca1, cb1, w_in0)
    lam_init = 0.8 - 0.6 * math.exp(-0.3 * 0)
    lam = (jnp.exp(jnp.sum(l0_lam_q1 * l0_lam_k1).astype(F32))
           - jnp.exp(jnp.sum(l0_lam_q2 * l0_lam_k2).astype(F32)) + lam_init)
    oa_lat, oa_ctx = _diff_attention(p_lat0, p_ctx0, rw_cols, aw, lam, lam_init, l0_subln_g)
    nh, hd = l0_r_k.shape
    zero = jnp.zeros((bsz, hd, nh * hd), F32)
    y_ctx, states_c = _rwkv_run(_rwkv_prepare(p_ctx0, p0, rw_cols), (zero, zero), p0)
    y_lat, _ = _rwkv_run(_rwkv_prepare(p_lat0, p0, rw_cols), states_c, p0)
    w_out0 = bf(l0_w_out)
    x = _outproj(oa_lat, y_lat, w_out0, x, g1)
    ctx_s = _outproj(oa_ctx, y_ctx, w_out0, ctx_s, cg1)
    w1, w2 = bf(l0_mlp_w1), bf(l0_mlp_w2)
    a2, b2 = _affine(l0_norm2_g, sh2, sc2)
    ca2, cb2 = _affine(l0_norm2_g, csh2, csc2)
    x = _mlp(x, a2, b2, g2, w1, w2)
    ctx_s = _mlp(ctx_s, ca2, cb2, cg2, w1, w2)

    p1 = dict(w_mod=l1_w_mod, b_mod=l1_b_mod, conv_w=l1_conv_w, conv_b=l1_conv_b, f_w1=l1_f_w1, f_b1=l1_f_b1,
              f_w2=l1_f_w2, f_b2=l1_f_b2, f_w3=l1_f_w3, f_b3=l1_f_b3, f_w4=l1_f_w4, f_freq=l1_f_freq,
              hy_bias=l1_hy_bias)
    (sh1, sc1, g1, sh2, sc2, g2), (csh1, csc1, _, _, _, _) = _mod_vectors(c, c_ctx, p1)
    hy_cols = l1_conv_w.shape[1]
    kv_cols = 2 * (2 * D_HEAD)
    w_in1 = bf(l1_w_in)
    a1, b1 = _affine(l1_norm1_g, sh1, sc1)
    ca1, cb1 = _affine(l1_norm1_g, csh1, csc1)
    p_lat = _inproj(x, a1, b1, w_in1)
    pkv_ctx = _inproj(ctx_s, ca1, cb1, w_in1[:, -kv_cols:])
    o_hy = _hyena(p_lat, p1)
    o_at = _gqa_attention(p_lat, pkv_ctx, hy_cols, l1_q_norm_g, l1_k_norm_g)
    x = _outproj(o_hy, o_at, bf(l1_w_out), x, g1)
    a2, b2 = _affine(l1_norm2_g, sh2, sc2)
    return _mlp(x, a2, b2, g2, bf(l1_mlp_w1), bf(l1_mlp_w2), final_g=final_g)
```

```python
import functools
import math

import numpy as np
import jax
import jax.numpy as jnp
from jax import lax
from jax.experimental import pallas as pl
from jax.experimental.pallas import tpu as pltpu

F32 = jnp.float32
BF16 = jnp.bfloat16
HI = lax.Precision.HIGHEST

NORM_EPS = 1e-6
ROPE_THETA = 10000.0
GRID_W = 64
N_MOD = 6
A_QK_DIM = 64
A_V_DIM = 128
B_HEAD = 64
B_DECAY_RANK = 64
B_ICL_RANK = 64
B_GATE_RANK = 160
B_GN_EPS = 64e-5
HY_EMB = 33
HY_ORDER = 64
HY_TARGET = 1e-2
HY_FAST_PCT = 0.3
HY_SLOW_PCT = 1.5
D_HEAD = 128
D_KV_GROUP = 4

LANE = 128
VMEM_LIMIT = 56 * 1024 * 1024
LOG2E = 1.4426950408889634


def _cp(sem, vmem=VMEM_LIMIT):
    return pltpu.CompilerParams(dimension_semantics=sem, vmem_limit_bytes=vmem)


def _pick(n, pref, step=8):
    t = max(step, min(n, pref) // step * step)
    while n % t:
        t -= step
    return t


def _mods_body(c_ref, w_ref, b_ref, o_ref):
    c = c_ref[...]
    s = c * jax.nn.sigmoid(c)
    o_ref[...] = jnp.dot(s, w_ref[...], preferred_element_type=F32, precision=HI) + b_ref[...]


def _mods(cvec, w_mod, b_mod):
    m, d = cvec.shape
    n = w_mod.shape[1]
    tn = _pick(n, 1024, LANE)
    return pl.pallas_call(
        _mods_body,
        grid=(n // tn,),
        in_specs=[pl.BlockSpec((m, d), lambda j: (0, 0)),
                  pl.BlockSpec((d, tn), lambda j: (0, j)),
                  pl.BlockSpec((1, tn), lambda j: (0, j))],
        out_specs=pl.BlockSpec((m, tn), lambda j: (0, j)),
        out_shape=jax.ShapeDtypeStruct((m, n), F32),
        compiler_params=_cp(("arbitrary",)),
        name="mods",
    )(cvec, w_mod, b_mod.reshape(1, n))


def _norm_mod(x, a, b):
    ms = jnp.mean(x * x, axis=-1, keepdims=True)
    return x * lax.rsqrt(ms + NORM_EPS) * a + b


def _inproj_body(x_ref, a_ref, b_ref, w_ref, o_ref, xn_ref):
    @pl.when(pl.program_id(2) == 0)
    def _():
        xn_ref[...] = _norm_mod(x_ref[0], a_ref[0], b_ref[0]).astype(BF16)

    o_ref[0] = jnp.dot(xn_ref[...], w_ref[...], preferred_element_type=F32).astype(o_ref.dtype)


def _inproj(x, a, b, w, out_dtype=BF16, tm_pref=1024, tn_pref=512):
    bsz, n, d = x.shape
    nn = w.shape[1]
    tm, tn = _pick(n, tm_pref), _pick(nn, tn_pref, LANE)
    return pl.pallas_call(
        _inproj_body,
        grid=(bsz, n // tm, nn // tn),
        in_specs=[pl.BlockSpec((1, tm, d), lambda bi, i, j: (bi, i, 0)),
                  pl.BlockSpec((1, 1, d), lambda bi, i, j: (bi, 0, 0)),
                  pl.BlockSpec((1, 1, d), lambda bi, i, j: (bi, 0, 0)),
                  pl.BlockSpec((d, tn), lambda bi, i, j: (0, j))],
        out_specs=pl.BlockSpec((1, tm, tn), lambda bi, i, j: (bi, i, j)),
        out_shape=jax.ShapeDtypeStruct((bsz, n, nn), out_dtype),
        scratch_shapes=[pltpu.VMEM((tm, d), BF16)],
        compiler_params=_cp(("parallel", "parallel", "arbitrary")),
        name="inproj",
    )(x, a, b, w)


def _outproj_body(oa_ref, ob_ref, wa_ref, wb_ref, x_ref, g_ref, y_ref):
    acc = jnp.dot(oa_ref[0].astype(BF16), wa_ref[...], preferred_element_type=F32)
    acc += jnp.dot(ob_ref[0].astype(BF16), wb_ref[...], preferred_element_type=F32)
    y_ref[0] = x_ref[0] + g_ref[0] * acc


def _outproj(oa, ob, w, x, g, tm_pref=1024, tn_pref=512):
    bsz, n, ka = oa.shape
    kb = ob.shape[2]
    d = w.shape[1]
    tm, tn = _pick(n, tm_pref), _pick(d, tn_pref, LANE)
    return pl.pallas_call(
        _outproj_body,
        grid=(bsz, n // tm, d // tn),
        in_specs=[pl.BlockSpec((1, tm, ka), lambda bi, i, j: (bi, i, 0)),
                  pl.BlockSpec((1, tm, kb), lambda bi, i, j: (bi, i, 0)),
                  pl.BlockSpec((ka, tn), lambda bi, i, j: (0, j)),
                  pl.BlockSpec((kb, tn), lambda bi, i, j: (0, j)),
                  pl.BlockSpec((1, tm, tn), lambda bi, i, j: (bi, i, j)),
                  pl.BlockSpec((1, 1, tn), lambda bi, i, j: (bi, 0, j))],
        out_specs=pl.BlockSpec((1, tm, tn), lambda bi, i, j: (bi, i, j)),
        out_shape=jax.ShapeDtypeStruct((bsz, n, d), F32),
        compiler_params=_cp(("parallel", "parallel", "parallel")),
        name="outproj",
    )(oa, ob, w[:ka], w[ka:], x, g)


def _mlp_body(x_ref, a_ref, b_ref, g_ref, w1_ref, w2_ref, fg_ref, y_ref, xn_ref, acc_ref, *, nf, final):
    f = pl.program_id(2)

    @pl.when(f == 0)
    def _():
        xn_ref[...] = _norm_mod(x_ref[0], a_ref[0], b_ref[0]).astype(BF16)
        acc_ref[...] = jnp.zeros_like(acc_ref)

    h = jnp.dot(xn_ref[...], w1_ref[...], preferred_element_type=F32)
    h = jnp.square(jnp.maximum(h, 0.0)).astype(BF16)
    acc_ref[...] += jnp.dot(h, w2_ref[...], preferred_element_type=F32)

    @pl.when(f == nf - 1)
    def _():
        y = x_ref[0] + g_ref[0] * acc_ref[...]
        if final:
            ms = jnp.mean(y * y, axis=-1, keepdims=True)
            y = y * lax.rsqrt(ms + NORM_EPS) * fg_ref[...]
        y_ref[0] = y


def _mlp(x, a, b, g, w1, w2, final_g=None, tm_pref=512, tf_pref=1024):
    bsz, n, d = x.shape
    dff = w1.shape[1]
    tm, tf = _pick(n, tm_pref), _pick(dff, tf_pref, LANE)
    nf = dff // tf
    final = final_g is not None
    fg = (final_g if final else jnp.ones((d,), F32)).reshape(1, d).astype(F32)
    return pl.pallas_call(
        functools.partial(_mlp_body, nf=nf, final=final),
        grid=(bsz, n // tm, nf),
        in_specs=[pl.BlockSpec((1, tm, d), lambda bi, i, f: (bi, i, 0)),
                  pl.BlockSpec((1, 1, d), lambda bi, i, f: (bi, 0, 0)),
                  pl.BlockSpec((1, 1, d), lambda bi, i, f: (bi, 0, 0)),
                  pl.BlockSpec((1, 1, d), lambda bi, i, f: (bi, 0, 0)),
                  pl.BlockSpec((d, tf), lambda bi, i, f: (0, f)),
                  pl.BlockSpec((tf, d), lambda bi, i, f: (f, 0)),
                  pl.BlockSpec((1, d), lambda bi, i, f: (0, 0))],
        out_specs=pl.BlockSpec((1, tm, d), lambda bi, i, f: (bi, i, 0)),
        out_shape=jax.ShapeDtypeStruct((bsz, n, d), F32),
        scratch_shapes=[pltpu.VMEM((tm, d), BF16), pltpu.VMEM((tm, d), F32)],
        compiler_params=_cp(("parallel", "parallel", "arbitrary")),
        name="mlp",
    )(x, a, b, g, w1, w2, fg)


def _rope_tables(n, dim, reps):
    rows = n // GRID_W
    row = jnp.repeat(jnp.arange(rows, dtype=F32), GRID_W)
    col = jnp.tile(jnp.arange(GRID_W, dtype=F32), rows)
    half = dim // 2
    inv = ROPE_THETA ** (-jnp.arange(0, half, 2, dtype=F32) / half)
    ar, ac = row[:, None] * inv, col[:, None] * inv
    cos = jnp.concatenate([jnp.cos(ar), jnp.cos(ar), jnp.cos(ac), jnp.cos(ac)], axis=-1)
    sin = jnp.concatenate([-jnp.sin(ar), jnp.sin(ar), -jnp.sin(ac), jnp.sin(ac)], axis=-1)
    return jnp.tile(cos, (1, reps)), jnp.tile(sin, (1, reps))


def _swap_matrix(dim, reps):
    q = dim // 4
    width = dim * reps
    p = np.zeros((width, width), np.float32)
    for j in range(width):
        base, r = (j // dim) * dim, j % dim
        axis, which, f = r // (2 * q), (r % (2 * q)) // q, r % q
        p[base + axis * 2 * q + (1 - which) * q + f, j] = 1.0
    return p


def _rope_norm(x, cos_ref, sin_ref, p_ref, g_ref, gs_ref, *, norm, rope, scale=1.0):
    y = x.astype(F32)
    if rope:
        ys = jnp.dot(x, p_ref[...], preferred_element_type=F32)
    if norm:
        rs = lax.rsqrt(jnp.mean(y * y, axis=-1, keepdims=True) + NORM_EPS)
        y = y * rs * g_ref[...]
        if rope:
            ys = ys * rs * gs_ref[...]
    if rope:
        y = y * cos_ref[...] + ys * sin_ref[...]
    return y * scale if scale != 1.0 else y


def _rope_operands(n, dim, gain, rope):
    reps = LANE // dim
    if rope:
        cos, sin = _rope_tables(n, dim, reps)
        pm = jnp.asarray(_swap_matrix(dim, reps), BF16)
    else:
        cos = sin = jnp.zeros((n, LANE), F32)
        pm = jnp.zeros((LANE, LANE), BF16)
    if gain is not None:
        g = jnp.tile(gain.astype(F32), reps).reshape(1, LANE)
        gs = jnp.dot(g, jnp.asarray(_swap_matrix(dim, reps)), precision=HI)
    else:
        g = gs = jnp.ones((1, LANE), F32)
    return cos, sin, pm, g, gs


def _kprep_body(x_ref, cos_ref, sin_ref, p_ref, g_ref, gs_ref, o_ref, *, heads, norm, rope):
    for h in range(heads):
        sl = slice(h * LANE, (h + 1) * LANE)
        y = _rope_norm(x_ref[0, :, sl], cos_ref, sin_ref, p_ref, g_ref, gs_ref, norm=norm, rope=rope)
        o_ref[0, :, sl] = y.astype(o_ref.dtype)


def _kprep(p, col0, heads, *, dim, gain=None, rope=True, tq_pref=512):
    bsz, n, _ = p.shape
    tq = _pick(n, tq_pref)
    hb = heads
    while col0 % (hb * LANE):
        hb //= 2
    w = hb * LANE
    cb = col0 // w
    cos, sin, pm, g, gs = _rope_operands(n, dim, gain, rope)
    return pl.pallas_call(
        functools.partial(_kprep_body, heads=hb, norm=gain is not None, rope=rope),
        grid=(bsz, n // tq, heads // hb),
        in_specs=[pl.BlockSpec((1, tq, w), lambda b, i, j: (b, i, cb + j)),
                  pl.BlockSpec((tq, LANE), lambda b, i, j: (i, 0)),
                  pl.BlockSpec((tq, LANE), lambda b, i, j: (i, 0)),
                  pl.BlockSpec((LANE, LANE), lambda b, i, j: (0, 0)),
                  pl.BlockSpec((1, LANE), lambda b, i, j: (0, 0)),
                  pl.BlockSpec((1, LANE), lambda b, i, j: (0, 0))],
        out_specs=pl.BlockSpec((1, tq, w), lambda b, i, j: (b, i, j)),
        out_shape=jax.ShapeDtypeStruct((bsz, n, heads * LANE), BF16),
        compiler_params=_cp(("parallel", "parallel", "parallel")),
        name="kprep",
    )(p, cos, sin, pm, g, gs)


FLASH_KEYS = 1280
FLASH_HEADROOM = 64.0


def _knorm_body(k_ref, o_ref, *, heads):
    for h in range(heads):
        k = k_ref[0, :, h * LANE:(h + 1) * LANE].astype(F32)
        nrm2 = jnp.max(jnp.sum(k * k, axis=1, keepdims=True), axis=0, keepdims=True)
        o_ref[0, h] = jnp.broadcast_to(jnp.sqrt(nrm2), (8, LANE))


def _key_block_norms(k, kcol0, heads, ts):
    bsz, nk, _ = k.shape
    nsub = nk // ts
    hb = heads
    while kcol0 % (hb * LANE):
        hb //= 2
    kb = kcol0 // (hb * LANE)
    out = pl.pallas_call(
        functools.partial(_knorm_body, heads=hb),
        grid=(bsz, heads // hb, nsub),
        in_specs=[pl.BlockSpec((1, ts, hb * LANE), lambda b, j, c: (b, c, kb + j))],
        out_specs=pl.BlockSpec((1, hb, 8, LANE), lambda b, j, c: (b, j, c, 0)),
        out_shape=jax.ShapeDtypeStruct((bsz, heads, nsub * 8, LANE), F32),
        compiler_params=_cp(("parallel", "parallel", "parallel")),
        name="key_block_norms",
    )(k)
    return out[:, :, ::8, 0].reshape(-1)


def _flash_body(kmax_ref, qr_ref, cos_ref, sin_ref, pm_ref, g_ref, gs_ref, k_ref, v_ref, e1_ref, e2_ref, o_ref,
                q_ref, m_ref, l_ref, acc_ref, thr_ref, *, ts, nsub, mode, tq, norm, rope, scale):
    base = (pl.program_id(0) * pl.num_programs(1) + pl.program_id(1)) * nsub
    prep = functools.partial(_rope_norm, cos_ref=cos_ref, sin_ref=sin_ref, p_ref=pm_ref, g_ref=g_ref, gs_ref=gs_ref,
                             norm=norm, rope=rope, scale=scale)
    if mode == "diff":
        y = prep(qr_ref[0])
        lane = lax.broadcasted_iota(jnp.int32, y.shape, 1)
        q_ref[:tq] = jnp.where(lane < A_QK_DIM, y, 0.0).astype(BF16)
        q_ref[tq:] = jnp.where(lane >= A_QK_DIM, y, 0.0).astype(BF16)
    else:
        for j in range(D_KV_GROUP):
            q_ref[j * tq:(j + 1) * tq] = prep(qr_ref[0, :, j * LANE:(j + 1) * LANE]).astype(BF16)
    q = q_ref[...]

    def scores(off, size):
        k = k_ref[0, pl.ds(off, size), :]
        return lax.dot_general(q, k, (((1,), (1,)), ((), ())), preferred_element_type=F32)

    def block(c):
        off = pl.multiple_of(c * ts, ts)
        return scores(off, ts), v_ref[0, pl.ds(off, ts), :]

    def wide(m):
        return jnp.concatenate([m] * (ts // LANE), axis=1)

    m0 = jnp.broadcast_to(jnp.max(scores(0, 2 * LANE), axis=1, keepdims=True), m_ref.shape)
    m_ref[...] = m0
    l_ref[...] = jnp.zeros_like(l_ref)
    acc_ref[...] = jnp.zeros_like(acc_ref)
    qf = q.astype(F32)
    qn = jnp.sqrt(jnp.sum(qf * qf, axis=1, keepdims=True))
    thr_ref[0] = jnp.min((m0[:, :1] + FLASH_HEADROOM) / qn)

    def step(c, carry):
        fixed = kmax_ref[base + c] * 1.001 <= thr_ref[0]

        @pl.when(fixed)
        def _():
            s, v = block(c)
            p = jnp.exp2(s - wide(m_ref[...]))
            l_ref[...] += jnp.sum(p, axis=1, keepdims=True)
            acc_ref[...] += jnp.dot(p.astype(BF16), v, preferred_element_type=F32)

        @pl.when(jnp.logical_not(fixed))
        def _():
            s, v = block(c)
            m_prev = m_ref[...]
            m_new = jnp.maximum(m_prev, jnp.max(s, axis=1, keepdims=True))
            alpha = jnp.exp2(m_prev - m_new)
            p = jnp.exp2(s - wide(m_new))
            l_ref[...] = alpha * l_ref[...] + jnp.sum(p, axis=1, keepdims=True)
            acc_ref[...] = alpha * acc_ref[...] + jnp.dot(p.astype(BF16), v, preferred_element_type=F32)
            m_ref[...] = m_new

        return carry

    lax.fori_loop(0, nsub, step, 0)

    o = acc_ref[...] / l_ref[...]
    if mode == "diff":
        d = o[:tq] - e1_ref[...] * o[tq:]
        ms = jnp.mean(d * d, axis=-1, keepdims=True)
        o_ref[0] = (d * lax.rsqrt(ms + NORM_EPS) * e2_ref[...]).astype(o_ref.dtype)
    else:
        for j in range(D_KV_GROUP):
            o_ref[0, :, j * LANE:(j + 1) * LANE] = o[j * tq:(j + 1) * tq].astype(o_ref.dtype)


def _flash(pq, qcol0, k, v, kcol0, vcol0, e1, e2, *, mode, hk, dim, tq, gain=None, rope=True):
    bsz, n, _ = pq.shape
    nk = k.shape[1]
    g = 2 if mode == "diff" else D_KV_GROUP
    mq = g * tq
    qw = LANE if mode == "diff" else D_KV_GROUP * LANE
    qb = qcol0 // qw
    ts = _pick(nk, FLASH_KEYS, 2 * LANE)
    kb, vb = kcol0 // LANE, vcol0 // LANE
    ow = LANE if mode == "diff" else D_KV_GROUP * LANE
    kmax = _key_block_norms(k, kcol0, hk, ts)
    cos, sin, pm, gq, gqs = _rope_operands(n, dim, gain, rope)
    const = lambda shape: pl.BlockSpec(shape, lambda b, h, i: (0, 0))
    return pl.pallas_call(
        functools.partial(_flash_body, ts=ts, nsub=nk // ts, mode=mode, tq=tq, norm=gain is not None, rope=rope,
                          scale=dim ** -0.5 * LOG2E),
        grid=(bsz, hk, n // tq),
        in_specs=[pl.BlockSpec(memory_space=pltpu.SMEM),
                  pl.BlockSpec((1, tq, qw), lambda b, h, i: (b, i, qb + h)),
                  pl.BlockSpec((tq, LANE), lambda b, h, i: (i, 0)),
                  pl.BlockSpec((tq, LANE), lambda b, h, i: (i, 0)),
                  const((LANE, LANE)), const((1, LANE)), const((1, LANE)),
                  pl.BlockSpec((1, nk, LANE), lambda b, h, i: (b, 0, kb + h)),
                  pl.BlockSpec((1, nk, LANE), lambda b, h, i: (b, 0, vb + h)),
                  const((1, LANE)), const((1, LANE))],
        out_specs=pl.BlockSpec((1, tq, ow), lambda b, h, i: (b, i, h)),
        out_shape=jax.ShapeDtypeStruct((bsz, n, hk * ow), BF16),
        scratch_shapes=[pltpu.VMEM((mq, LANE), BF16), pltpu.VMEM((mq, LANE), F32), pltpu.VMEM((mq, LANE), F32),
                        pltpu.VMEM((mq, LANE), F32), pltpu.SMEM((1,), F32)],
        compiler_params=_cp(("parallel", "parallel", "arbitrary")),
        name="flash_" + mode,
    )(kmax, pq, cos, sin, pm, gq, gqs, k, v, e1, e2)


def _diff_attention(pa_lat, pa_ctx, c0, aw, lam, lam_init, subln_g):
    n, nctx = pa_lat.shape[1], pa_ctx.shape[1]
    heads = aw // LANE
    k_l = _kprep(pa_lat, c0 + aw, heads, dim=A_QK_DIM)
    k_all = jnp.concatenate([k_l, pa_ctx[..., c0 + aw:c0 + 2 * aw]], axis=1)
    v_all = jnp.concatenate([pa_lat[..., c0 + 2 * aw:c0 + 3 * aw], pa_ctx[..., c0 + 2 * aw:c0 + 3 * aw]], axis=1)
    e1 = jnp.full((1, LANE), lam, F32)
    e2 = (subln_g.astype(F32) * (1.0 - lam_init)).reshape(1, LANE)
    o_lat = _flash(pa_lat, c0, k_all, v_all, 0, 0, e1, e2, mode="diff", hk=heads, dim=A_QK_DIM, tq=_pick(n, 1024))
    o_ctx = _flash(pa_ctx, c0, pa_ctx, pa_ctx, c0 + aw, c0 + 2 * aw, e1, e2, mode="diff", hk=heads, dim=A_QK_DIM,
                   tq=_pick(nctx, 512), rope=False)
    return o_lat, o_ctx


def _gqa_attention(p_lat, pkv_ctx, qcol0, q_norm_g, k_norm_g):
    n = p_lat.shape[1]
    hq, hk = 2 * D_KV_GROUP, 2
    kcol0 = qcol0 + hq * D_HEAD
    vcol0 = kcol0 + hk * D_HEAD
    k_l = _kprep(p_lat, kcol0, hk, dim=D_HEAD, gain=k_norm_g)
    k_c = _kprep(pkv_ctx, 0, hk, dim=D_HEAD, gain=k_norm_g, rope=False)
    k_all = jnp.concatenate([k_l, k_c], axis=1)
    v_all = jnp.concatenate([p_lat[..., vcol0:vcol0 + hk * D_HEAD], pkv_ctx[..., hk * D_HEAD:]], axis=1)
    dummy = jnp.zeros((1, LANE), F32)
    return _flash(p_lat, qcol0, k_all, v_all, 0, 0, dummy, dummy, mode="gqa", hk=hk, dim=D_HEAD,
                  tq=_pick(n, 512), gain=q_norm_g)


RW_CHUNK = 64
RW_HALO = 8


def _mm_hi(a, b):
    return jnp.dot(a, b, preferred_element_type=F32, precision=HI)


def _mm_bf(a, b):
    return jnp.dot(a.astype(BF16), b.astype(BF16), preferred_element_type=F32)


def _split3(x):
    x1 = x.astype(BF16)
    r1 = x - x1.astype(F32)
    x2 = r1.astype(BF16)
    return x1, x2, (r1 - x2.astype(F32)).astype(BF16)


def _mm_x3(a, b):
    ah, al, _ = _split3(a)
    bh, bl, _ = _split3(b)
    d = lambda u, w: jnp.dot(u, w, preferred_element_type=F32)
    return d(ah, bh) + (d(ah, bl) + d(al, bh))


def _mm_sel(z, sel):
    z1, z2, z3 = _split3(z)
    d = lambda u: jnp.dot(u, sel, preferred_element_type=F32)
    return d(z1) + (d(z2) + d(z3))


def _rwprep_body(x_ref, hp_ref, hn_ref, mup_ref, mun_ref, vec_ref, w2f_ref, w2b_ref, a2f_ref, a2b_ref,
                 g2_ref, sel_ref, selt_ref,
                 r_ref, v_ref, kk_ref, lwf_ref, bf_ref, kf_ref, lwb_ref, bb_ref, kb_ref, g_ref, bonus_ref,
                 *, nb, tm, bw):
    i = pl.program_id(1)
    x = x_ref[0].astype(F32)
    row = lax.broadcasted_iota(jnp.int32, x.shape, 0)
    prev_edge = jnp.where(i > 0, hp_ref[0, RW_HALO - 1:RW_HALO, :].astype(F32), 0.0)
    next_edge = jnp.where(i < nb - 1, hn_ref[0, 0:1, :].astype(F32), 0.0)
    prev = jnp.where(row == 0, prev_edge, pltpu.roll(x, 1, 0))
    nxt = jnp.where(row == tm - 1, next_edge, pltpu.roll(x, tm - 1, 0))
    xs = x + mup_ref[...] * (prev - x) + mun_ref[...] * (nxt - x)
    r, k, v = xs[:, :bw], xs[:, bw:2 * bw], xs[:, 2 * bw:3 * bw]
    lr = xs[:, 3 * bw:3 * bw + LANE]
    gd = xs[:, 3 * bw + LANE:3 * bw + 3 * LANE]
    k_k, k_a, r_k = vec_ref[0:1], vec_ref[1:2], vec_ref[2:3]
    w0f, a0f, w0b, a0b = vec_ref[3:4], vec_ref[4:5], vec_ref[5:6], vec_ref[6:7]
    sel, selt = sel_ref[...], selt_ref[...]

    def head_sum(z):
        return _mm_sel(_mm_sel(z, sel), selt)

    kk = k * k_k
    kk = kk * lax.rsqrt(head_sum(kk * kk) + 1e-12)
    g_ref[0] = _mm_bf(jax.nn.sigmoid(gd), g2_ref[...]).astype(g_ref.dtype)
    th = jnp.tanh(lr)
    ksum = jnp.zeros_like(k)
    for (w0, a0, w2_ref, a2_ref, lw_ref, b_ref, kd_ref) in (
            (w0f, a0f, w2f_ref, a2f_ref, lwf_ref, bf_ref, kf_ref),
            (w0b, a0b, w2b_ref, a2b_ref, lwb_ref, bb_ref, kb_ref)):
        logw = -math.exp(-0.5) * jax.nn.sigmoid(w0 + _mm_bf(th, w2_ref[...]))
        a = jax.nn.sigmoid(a0 + _mm_bf(lr, a2_ref[...]))
        k_d = k * (1.0 + (a - 1.0) * k_a)
        ksum = ksum + k_d
        lw_ref[0] = logw
        b_ref[0] = kk * a
        kd_ref[0] = k_d
    r_ref[0] = r
    v_ref[0] = v
    kk_ref[0] = kk
    bonus_ref[0] = head_sum(r * ksum * r_k) * v


def _rwkv_prepare(pr, p, cols):
    bsz, n, _ = pr.shape
    bw = p["w0_f"].shape[0]
    tm = _pick(n, 256)
    nb = n // tm
    hb = tm // RW_HALO
    assert cols - 3 * bw >= 3 * LANE and B_DECAY_RANK + B_ICL_RANK == LANE

    def padded(vec):
        return jnp.concatenate([vec.astype(F32), jnp.zeros((cols - vec.shape[0],), F32)]).reshape(1, cols)

    vecs = jnp.stack([p["k_k"], p["k_a"], p["r_k"].reshape(-1), p["w0_f"], p["a0_f"], p["w0_b"], p["a0_b"],
                      jnp.zeros((bw,), F32)]).astype(F32)
    zr = jnp.zeros((B_DECAY_RANK, bw), F32)
    w2 = {d: jnp.concatenate([p["w2_" + d].astype(F32), zr], axis=0).astype(BF16) for d in "fb"}
    a2 = {d: jnp.concatenate([zr, p["a2_" + d].astype(F32)], axis=0).astype(BF16) for d in "fb"}
    g2 = jnp.concatenate([p["g2"].astype(F32), jnp.zeros((2 * LANE - B_GATE_RANK, bw), F32)], axis=0).astype(BF16)
    sel, selt = _head_selectors(bw)

    full = lambda shape: pl.BlockSpec(shape, lambda b, i: tuple(0 for _ in shape))
    tok = pl.BlockSpec((1, tm, bw), lambda b, i: (b, i, 0))
    tok_f32 = jax.ShapeDtypeStruct((bsz, n, bw), F32)
    outs = pl.pallas_call(
        functools.partial(_rwprep_body, nb=nb, tm=tm, bw=bw),
        grid=(bsz, nb),
        in_specs=[pl.BlockSpec((1, tm, cols), lambda b, i: (b, i, 0)),
                  pl.BlockSpec((1, RW_HALO, cols), lambda b, i: (b, jnp.maximum(i * hb - 1, 0), 0)),
                  pl.BlockSpec((1, RW_HALO, cols), lambda b, i: (b, jnp.minimum((i + 1) * hb, nb * hb - 1), 0)),
                  full((1, cols)), full((1, cols)), full((8, bw)),
                  full((LANE, bw)), full((LANE, bw)), full((LANE, bw)), full((LANE, bw)),
                  full((2 * LANE, bw)), full((bw, LANE)), full((LANE, bw))],
        out_specs=[tok] * 11,
        out_shape=[tok_f32] * 9 + [jax.ShapeDtypeStruct((bsz, n, bw), BF16), tok_f32],
        compiler_params=_cp(("parallel", "parallel")),
        name="rwkv_prepare",
    )(pr, pr, pr, padded(p["mu_prev"]), padded(p["mu_next"]), vecs, w2["f"], w2["b"], a2["f"], a2["b"],
      g2, sel, selt)
    r, v, kk, lwf, bf, kf, lwb, bb, kb, g, bonus = outs
    return dict(r=r, v=v, kk=kk, g=g, bonus=bonus, f=(lwf, bf, kf), b=(lwb, bb, kb))


RW_GROUP = 4


def _bd(x):
    t = jnp.concatenate([x.astype(BF16)] * RW_GROUP, axis=1)
    r = lax.broadcasted_iota(jnp.int32, t.shape, 1) // B_HEAD
    c = lax.broadcasted_iota(jnp.int32, t.shape, 2) // B_HEAD
    return jnp.where(r == c, t, jnp.zeros_like(t))


def _bdot(a, b, dims):
    return jnp.einsum(dims, a.astype(BF16), b.astype(BF16), preferred_element_type=F32)


def _groups(x):
    gw = RW_GROUP * B_HEAD
    return jnp.stack([x[:, g * gw:(g + 1) * gw] for g in range(x.shape[1] // gw)])


def _rw_decays(r, kk, b, kd, lw, reverse):
    ck = r.shape[0]
    ti = lax.broadcasted_iota(jnp.int32, (ck, ck), 0)
    si = lax.broadcasted_iota(jnp.int32, (ck, ck), 1)
    tri = ((si >= ti) if reverse else (si <= ti)).astype(BF16)
    l1, l2, l3 = _split3(lw)
    tdot = lambda z: jnp.dot(tri, z, preferred_element_type=F32)
    cum = tdot(l1) + (tdot(l2) + tdot(l3))
    total = cum[0:1] if reverse else cum[ck - 1:ck]
    winv = jnp.exp(-cum)
    wrest = jnp.exp(total - cum)
    return (-kk * jnp.exp(cum - lw), b * winv, kd * winv, r * jnp.exp(cum), b * wrest, kd * wrest, jnp.exp(total))


def _rw_chunks(fwd, bwd, s_f, s_b):
    nseq = len(fwd)
    ck, w = fwd[0][0].shape
    gw = RW_GROUP * B_HEAD
    ng = w // gw
    parts = []
    for probs, rev in ((fwd, False), (bwd, True)):
        for (r, v, kk, b, kd, lw) in probs:
            parts.append([_groups(z) for z in _rw_decays(r, kk, b, kd, lw, rev)] + [_groups(v)])
    at, bt, kt, rt, btw, ktw, decay, v = [jnp.concatenate(zs, axis=0) for zs in zip(*parts)]
    s = jnp.concatenate([_groups(z) for z in list(s_f) + list(s_b)], axis=0)
    nslab = 2 * nseq * ng
    shape = (nslab, ck, gw)
    lag = lax.broadcasted_iota(jnp.int32, shape, 1) - lax.broadcasted_iota(jnp.int32, shape, 2) % ck
    lag = jnp.where(lax.broadcasted_iota(jnp.int32, shape, 0) >= nseq * ng, -lag, lag)
    incl, strict = lag >= 0, lag > 0
    eye = (lag == 0).astype(F32)

    bdv = _bd(v)
    lhs = jnp.concatenate([at, rt], axis=1)
    lb, lk = _bdot(lhs, _bd(bt), "gtk,gnk->gtn"), _bdot(lhs, _bd(kt), "gtk,gnk->gtn")
    lab, mrb = jnp.where(strict, lb[:, :ck], 0.0), jnp.where(incl, lb[:, ck:], 0.0)
    lak, mrk = jnp.where(strict, lk[:, :ck], 0.0), jnp.where(incl, lk[:, ck:], 0.0)
    tinv = eye + lab
    pw = _bdot(lab, _bd(lab), "gtk,gkn->gtn")
    span = 4
    while span < ck:
        both = _bdot(jnp.concatenate([tinv, pw], axis=1), _bd(pw), "gtk,gkn->gtn")
        tinv, pw = tinv + both[:, :ck], both[:, ck:]
        span *= 2
    tinv = tinv + _bdot(tinv, _bd(pw), "gtk,gkn->gtn")
    vmix = _bdot(jnp.concatenate([lak, mrk], axis=1), bdv, "gtk,gkn->gtn")
    p1 = _bdot(tinv, _bd(at), "gtk,gkn->gtn")
    q1 = _bdot(tinv, _bd(vmix[:, :ck]), "gtk,gkn->gtn")
    smix = _bdot(jnp.concatenate([p1, rt], axis=1), _bd(s), "gtk,gnk->gtn")
    u = smix[:, :ck] + q1
    y = smix[:, ck:] + _bdot(mrb, _bd(u), "gtk,gkn->gtn") + vmix[:, ck:]
    z = _bdot(jnp.concatenate([u, v], axis=1), jnp.concatenate([btw, ktw], axis=1), "gtm,gtn->gmn")
    lane_head = lax.broadcasted_iota(jnp.int32, (nslab, B_HEAD, gw), 2) // B_HEAD
    upd = jnp.zeros_like(s)
    for h in range(RW_GROUP):
        upd = upd + jnp.where(lane_head == h, z[:, h * B_HEAD:(h + 1) * B_HEAD], 0.0)
    s_new = s * decay + upd
    lanes = lambda x, q: jnp.concatenate([x[q * ng + g] for g in range(ng)], axis=1)
    seqs = range(nseq)
    return ([lanes(y, q) for q in seqs], [lanes(y, nseq + q) for q in seqs],
            [lanes(s_new, q) for q in seqs], [lanes(s_new, nseq + q) for q in seqs])


def _rwscan_body(rf_ref, vf_ref, kkf_ref, bf_ref, kdf_ref, lwf_ref, rb_ref, vb_ref, kkb_ref, bb_ref, kdb_ref, lwb_ref,
                 s0f_ref, s0b_ref, yf_ref, yb_ref, sTf_ref, sTb_ref, sf_ref, sb_ref, *, nc, nseq):
    c = pl.program_id(0)

    @pl.when(c == 0)
    def _():
        sf_ref[...] = s0f_ref[...]
        sb_ref[...] = s0b_ref[...]

    seqs = range(nseq)
    yf, yb, sf, sb = _rw_chunks(
        [(rf_ref[q], vf_ref[q], kkf_ref[q], bf_ref[q], kdf_ref[q], lwf_ref[q]) for q in seqs],
        [(rb_ref[q], vb_ref[q], kkb_ref[q], bb_ref[q], kdb_ref[q], lwb_ref[q]) for q in seqs],
        [sf_ref[q] for q in seqs], [sb_ref[q] for q in seqs])
    for q in seqs:
        yf_ref[q] = yf[q]
        yb_ref[q] = yb[q]
        sf_ref[q] = sf[q]
        sb_ref[q] = sb[q]

    @pl.when(c == nc - 1)
    def _():
        for q in seqs:
            sTf_ref[q] = sf[q]
            sTb_ref[q] = sb[q]


def _rwkv_scan(r, v, kk, fwd, bwd, s0f, s0b):
    bsz, n, w = r.shape
    ck = RW_CHUNK
    assert n % ck == 0 and ck == B_HEAD and w % (RW_GROUP * B_HEAD) == 0
    nc = n // ck
    seq_f = pl.BlockSpec((bsz, ck, w), lambda c: (0, c, 0))
    seq_b = pl.BlockSpec((bsz, ck, w), lambda c: (0, nc - 1 - c, 0))
    st = pl.BlockSpec((bsz, B_HEAD, w), lambda c: (0, 0, 0))
    y_shape = jax.ShapeDtypeStruct((bsz, n, w), F32)
    s_shape = jax.ShapeDtypeStruct((bsz, B_HEAD, w), F32)
    (lwf, bf, kf), (lwb, bb, kb) = fwd, bwd
    return pl.pallas_call(
        functools.partial(_rwscan_body, nc=nc, nseq=bsz),
        grid=(nc,),
        in_specs=[seq_f] * 6 + [seq_b] * 6 + [st, st],
        out_specs=[seq_f, seq_b, st, st],
        out_shape=[y_shape, y_shape, s_shape, s_shape],
        scratch_shapes=[pltpu.VMEM((bsz, B_HEAD, w), F32), pltpu.VMEM((bsz, B_HEAD, w), F32)],
        compiler_params=_cp(("arbitrary",)),
        name="rwkv_scan",
    )(r, v, kk, bf, kf, lwf, r, v, kk, bb, kb, lwb, s0f, s0b)


def _rwpost_body(yf_ref, yb_ref, g_ref, bonus_ref, lg_ref, lb_ref, sel_ref, selt_ref, o_ref):
    sel, selt = sel_ref[...], selt_ref[...]
    head_mean = lambda z: _mm_sel(_mm_sel(z, sel), selt) * (1.0 / B_HEAD)
    y = yf_ref[0] + yb_ref[0]
    d = y - head_mean(y)
    y = d * lax.rsqrt(head_mean(d * d) + B_GN_EPS)
    y = y * lg_ref[...] + lb_ref[...] + bonus_ref[0]
    o_ref[0] = (y * g_ref[0].astype(F32)).astype(o_ref.dtype)


def _head_selectors(bw):
    sel_np = np.zeros((bw, LANE), np.float32)
    sel_np[np.arange(bw), np.arange(bw) // B_HEAD] = 1.0
    return jnp.asarray(sel_np, BF16), jnp.asarray(sel_np.T, BF16)


def _rwkv_post(y_f, y_b, g, bonus, lnx_g, lnx_b):
    bsz, n, bw = y_f.shape
    tm = _pick(n, 256)
    tok = pl.BlockSpec((1, tm, bw), lambda b, i: (b, i, 0))
    vec = pl.BlockSpec((1, bw), lambda b, i: (0, 0))
    sel, selt = _head_selectors(bw)
    return pl.pallas_call(
        _rwpost_body,
        grid=(bsz, n // tm),
        in_specs=[tok, tok, tok, tok, vec, vec,
                  pl.BlockSpec((bw, LANE), lambda b, i: (0, 0)), pl.BlockSpec((LANE, bw), lambda b, i: (0, 0))],
        out_specs=tok,
        out_shape=jax.ShapeDtypeStruct((bsz, n, bw), BF16),
        compiler_params=_cp(("parallel", "parallel")),
        name="rwkv_post",
    )(y_f, y_b, g, bonus, lnx_g.reshape(1, bw).astype(F32), lnx_b.reshape(1, bw).astype(F32), sel, selt)


def _rwkv_run(prep, states0, p):
    y_f, y_b, s_f, s_b = _rwkv_scan(prep["r"], prep["v"], prep["kk"], prep["f"], prep["b"], states0[0], states0[1])
    return _rwkv_post(y_f, y_b, prep["g"], prep["bonus"], p["lnx_g"], p["lnx_b"]), (s_f, s_b)


HY_N2 = 128
HY_HALO = 8


def _hypre_body(x_ref, hp_ref, hn_ref, cw_ref, cb_ref, x0_ref, uv_ref, *, nb, tm, cw):
    i = pl.program_id(1)
    x = x_ref[0].astype(F32)
    row = lax.broadcasted_iota(jnp.int32, x.shape, 0)
    prev_edge = jnp.where(i > 0, hp_ref[0, HY_HALO - 1:HY_HALO, :].astype(F32), 0.0)
    next_edge = jnp.where(i < nb - 1, hn_ref[0, 0:1, :].astype(F32), 0.0)
    prev = jnp.where(row == 0, prev_edge, pltpu.roll(x, 1, 0))
    nxt = jnp.where(row == tm - 1, next_edge, pltpu.roll(x, tm - 1, 0))
    u = prev * cw_ref[0:1] + x * cw_ref[1:2] + nxt * cw_ref[2:3] + cb_ref[...]
    x0_ref[0] = u[:, :cw]
    uv_ref[0] = u[:, 2 * cw:3 * cw] * u[:, cw:2 * cw]


def _hyena_pre(p, conv_w, conv_b):
    bsz, n, _ = p.shape
    hc = conv_w.shape[1]
    cw = hc // 3
    tm = _pick(n, 256)
    nb = n // tm
    hb = tm // HY_HALO
    cwp = jnp.concatenate([conv_w.astype(F32), jnp.zeros((5, hc), F32)], axis=0)
    out = pl.BlockSpec((1, tm, cw), lambda b, i: (b, i, 0))
    return pl.pallas_call(
        functools.partial(_hypre_body, nb=nb, tm=tm, cw=cw),
        grid=(bsz, nb),
        in_specs=[pl.BlockSpec((1, tm, hc), lambda b, i: (b, i, 0)),
                  pl.BlockSpec((1, HY_HALO, hc), lambda b, i: (b, jnp.maximum(i * hb - 1, 0), 0)),
                  pl.BlockSpec((1, HY_HALO, hc), lambda b, i: (b, jnp.minimum((i + 1) * hb, nb * hb - 1), 0)),
                  pl.BlockSpec((8, hc), lambda b, i: (0, 0)),
                  pl.BlockSpec((1, hc), lambda b, i: (0, 0))],
        out_specs=[out, out],
        out_shape=[jax.ShapeDtypeStruct((bsz, n, cw), F32)] * 2,
        compiler_params=_cp(("parallel", "parallel")),
        name="hyena_pre",
    )(p, p, p, cwp, conv_b.reshape(1, hc).astype(F32))


def _hyfilt_body(z_ref, t_ref, z0_ref, w1_ref, w2_ref, w3_ref, w4_ref, w4b_ref, vec_ref, dl_ref,
                 k_ref, s_ref, *, half_tiles):
    i = pl.program_id(0)
    b1, b2, b3, fr = vec_ref[0:1], vec_ref[1:2], vec_ref[2:3], vec_ref[3:4]

    def mlp3(z):
        h = jnp.sin(fr * (_mm_hi(z, w1_ref[...]) + b1))
        h = jnp.sin(fr * (_mm_hi(h, w2_ref[...]) + b2))
        h = jnp.sin(fr * (_mm_hi(h, w3_ref[...]) + b3))
        lane = lax.broadcasted_iota(jnp.int32, h.shape, 1)
        return jnp.where(lane < HY_ORDER, h, 0.0), jnp.where(lane >= HY_ORDER, h, 0.0)

    h_lo, h_hi = mlp3(z_ref[...])
    raw = jnp.concatenate([_mm_hi(h_lo, w4_ref[...]), _mm_hi(h_hi, w4_ref[...])], axis=0)
    raw = raw * jnp.exp(-t_ref[...] * dl_ref[...])

    @pl.when(i == 0)
    def _():
        s_ref[...] = jnp.zeros_like(s_ref)

    s_ref[...] += jnp.sum(jnp.abs(raw), axis=0, keepdims=True)
    hb0 = _mm_hi(mlp3(z0_ref[...])[0], w4b_ref[...])
    row = lax.broadcasted_iota(jnp.int32, raw.shape, 0)
    raw = raw + jnp.where((row == 0) & (i == 0), hb0[0:1], 0.0)
    k_ref[...] = jnp.where((row == 0) & (i == half_tiles), 0.0, raw)


def _hyena_kernel_taps(n, p):
    cw = p["f_w4"].shape[1] // 2
    f32 = lambda a: a.astype(F32)
    row = jnp.arange(2 * n, dtype=jnp.int32)
    pos = jnp.where(row < n, row, (2 * n - row) % n).astype(F32)[:, None]
    tt = pos * jnp.float32(1.0 / (n - 1))
    bands = (HY_EMB - 1) // 2
    ang = (2 * math.pi / n) * pos * jnp.linspace(1e-4, bands - 1, bands, dtype=F32)[None]
    zz = jnp.concatenate([tt, jnp.cos(ang), -jnp.sin(ang), jnp.zeros((2 * n, HY_ORDER - HY_EMB), F32)], axis=-1)
    w1 = jnp.concatenate([f32(p["f_w1"]), jnp.zeros((HY_ORDER - HY_EMB, HY_ORDER), F32)], axis=0)
    twice = lambda w: jnp.kron(jnp.eye(2, dtype=F32), f32(w))
    vecs = jnp.stack([jnp.tile(f32(p[k]), 2) for k in ("f_b1", "f_b2", "f_b3", "f_freq")]
                     + [jnp.zeros((2 * HY_ORDER,), F32)] * 4)
    deltas = jnp.abs(jnp.linspace(math.log(HY_TARGET) / HY_SLOW_PCT, math.log(HY_TARGET) / HY_FAST_PCT, cw,
                                  dtype=F32)).reshape(1, cw)
    tr = _pick(n, 512)
    half_tiles = n // tr
    pk = 2 * HY_ORDER
    zp = zz.reshape(-1, 2, tr // 2, HY_ORDER).transpose(0, 2, 1, 3).reshape(-1, pk)
    full = lambda shape: pl.BlockSpec(shape, lambda i: tuple(0 for _ in shape))
    w4 = jnp.concatenate([f32(p["f_w4"])] * 2, axis=0)
    return pl.pallas_call(
        functools.partial(_hyfilt_body, half_tiles=half_tiles),
        grid=(2 * half_tiles,),
        in_specs=[pl.BlockSpec((tr // 2, pk), lambda i: (i, 0)),
                  pl.BlockSpec((tr, 1), lambda i: (i, 0)),
                  full((8, pk)), full((pk, pk)), full((pk, pk)), full((pk, pk)),
                  pl.BlockSpec((pk, cw), lambda i: (0, i // half_tiles)),
                  pl.BlockSpec((pk, cw), lambda i: (0, 1)),
                  full((8, pk)), full((1, cw))],
        out_specs=[pl.BlockSpec((tr, cw), lambda i: (i, 0)), pl.BlockSpec((1, cw), lambda i: (0, 0))],
        out_shape=[jax.ShapeDtypeStruct((2 * n, cw), F32), jax.ShapeDtypeStruct((1, cw), F32)],
        compiler_params=_cp(("arbitrary",)),
        name="hyena_filter",
    )(zp, tt, jnp.broadcast_to(jnp.tile(zz[0:1], (1, 2)), (8, pk)), twice(w1), twice(p["f_w2"]), twice(p["f_w3"]),
      w4, w4, vecs, deltas)


def _dft_consts(n):
    nn = 2 * n
    n2 = HY_N2
    n1 = nn // n2
    n1h = n1 // 2 + 1
    nf = -(-n1h // 8) * 8
    live = (np.arange(nf) < n1h).astype(np.float64)
    a1 = 2 * np.pi * np.outer(np.arange(nf), np.arange(n1)) / n1
    f1 = np.concatenate([np.cos(a1), -np.sin(a1)], axis=0) * np.tile(live, 2)[:, None]
    wgt = live * np.where((np.arange(nf) == 0) | (np.arange(nf) == n1 // 2), 1.0, 2.0)
    f1inv = np.concatenate([np.cos(a1.T) * wgt, -np.sin(a1.T) * wgt], axis=1) / nn
    a2 = 2 * np.pi * np.outer(np.arange(n2), np.arange(n2)) / n2
    c2, s2 = np.cos(a2), -np.sin(a2)
    m2 = np.block([[c2, -s2], [s2, c2]])
    m2inv = np.block([[c2, s2], [-s2, c2]])
    at = 2 * np.pi * np.outer(np.arange(n2), np.arange(nf)) / nn
    tw = np.stack([np.cos(at), -np.sin(at)])
    c = lambda a: jnp.asarray(a.astype(np.float32))
    return dict(n1=n1, n2=n2, nf=nf, f1=c(f1), f1inv=c(f1inv), m2=c(m2), m2inv=c(m2inv),
                tw_s2=c(tw[:, :, :, None]),
                tw_f1=c(np.transpose(tw, (0, 2, 1))[:, :, :, None]))


HY_S2 = 8
HY_CTILE = 512


def _dft1_body(f_ref, x_ref, tw_ref, o_ref, *, nf):
    x = jnp.swapaxes(x_ref[0], 0, 1)
    fh, fl, _ = _split3(f_ref[...])
    d = lambda u, w: jnp.dot(u, w, preferred_element_type=F32)
    re, im = [], []
    for jj in range(HY_S2):
        xh, xl, _ = _split3(x[jj])
        a = d(fh, xh) + (d(fh, xl) + d(fl, xh))
        ar, ai = a[:nf], a[nf:]
        twr, twi = tw_ref[0, jj], tw_ref[1, jj]
        re.append(ar * twr - ai * twi)
        im.append(ar * twi + ai * twr)
    o_ref[0, 0] = jnp.swapaxes(jnp.stack(re), 0, 1)
    o_ref[0, 1] = jnp.swapaxes(jnp.stack(im), 0, 1)


def _dft_stage1(x, consts, rows):
    bsz, _, cw = x.shape
    n2, nf = consts["n2"], consts["nf"]
    xv = x.reshape(bsz, rows, n2, cw)
    f1 = consts["f1"][:, :rows]
    tc = _pick(cw, HY_CTILE, LANE)
    return pl.pallas_call(
        functools.partial(_dft1_body, nf=nf),
        grid=(bsz, n2 // HY_S2, cw // tc),
        in_specs=[pl.BlockSpec((2 * nf, rows), lambda b, j, c: (0, 0)),
                  pl.BlockSpec((1, rows, HY_S2, tc), lambda b, j, c: (b, 0, j, c)),
                  pl.BlockSpec((2, HY_S2, nf, 1), lambda b, j, c: (0, j, 0, 0))],
        out_specs=pl.BlockSpec((1, 2, nf, HY_S2, tc), lambda b, j, c: (b, 0, 0, j, c)),
        out_shape=jax.ShapeDtypeStruct((bsz, 2, nf, n2, cw), F32),
        compiler_params=_cp(("parallel", "parallel", "parallel")),
        name="hyena_dft1",
    )(f1, xv, consts["tw_s2"])


def _dftmid_body(a_ref, m2_ref, m2i_ref, h_ref, tw_ref, is_ref, o_ref, *, n2, filt):
    a = a_ref[0, :, 0].reshape(2 * n2, a_ref.shape[-1])
    x = _mm_x3(m2_ref[...], a)
    xr, xi = x[:n2], x[n2:]
    if filt:
        o_ref[0, 0, 0] = xr * is_ref[...]
        o_ref[0, 1, 0] = xi * is_ref[...]
        return
    hr, hi = h_ref[0, 0, 0], h_ref[0, 1, 0]
    y = jnp.concatenate([xr * hr - xi * hi, xr * hi + xi * hr], axis=0)
    z = _mm_x3(m2i_ref[...], y)
    zr, zi = z[:n2], z[n2:]
    twr, twi = tw_ref[0, 0], tw_ref[1, 0]
    o_ref[0, 0, 0] = zr * twr + zi * twi
    o_ref[0, 1, 0] = zi * twr - zr * twi


def _dft_mid(a, h, inv_s, consts, filt):
    bsz = a.shape[0]
    n1, n2 = consts["nf"], consts["n2"]
    cw = a.shape[-1]
    blk = pl.BlockSpec((1, 2, 1, n2, cw), lambda f, b: (b, 0, f, 0, 0))
    hblk = pl.BlockSpec((1, 2, 1, n2, cw), (lambda f, b: (0, 0, 0, 0, 0)) if filt else (lambda f, b: (0, 0, f, 0, 0)))
    return pl.pallas_call(
        functools.partial(_dftmid_body, n2=n2, filt=filt),
        grid=(n1, bsz),
        in_specs=[blk,
                  pl.BlockSpec((2 * n2, 2 * n2), lambda f, b: (0, 0)),
                  pl.BlockSpec((2 * n2, 2 * n2), lambda f, b: (0, 0)),
                  hblk,
                  pl.BlockSpec((2, 1, n2, 1), lambda f, b: (0, f, 0, 0)),
                  pl.BlockSpec((1, cw), lambda f, b: (0, 0))],
        out_specs=blk,
        out_shape=jax.ShapeDtypeStruct((bsz, 2, n1, n2, cw), F32),
        compiler_params=_cp(("parallel", "parallel")),
        name="hyena_dftmid_" + ("filter" if filt else "conv"),
    )(a, consts["m2"], consts["m2inv"], h, consts["tw_f1"], inv_s)


def _dft3_body(f_ref, z_ref, x0_ref, uv_ref, bias_ref, o_ref):
    zr, zi = jnp.swapaxes(z_ref[0, 0], 0, 1), jnp.swapaxes(z_ref[0, 1], 0, 1)
    x0, uv = jnp.swapaxes(x0_ref[0], 0, 1), jnp.swapaxes(uv_ref[0], 0, 1)
    fh, fl, _ = _split3(f_ref[...])
    d = lambda u, w: jnp.dot(u, w, preferred_element_type=F32)
    out = []
    for jj in range(HY_S2):
        zh, zl, _ = _split3(jnp.concatenate([zr[jj], zi[jj]], axis=0))
        y = d(fh, zh) + (d(fh, zl) + d(fl, zh))
        out.append(x0[jj] * (y + uv[jj] * bias_ref[...]))
    o_ref[0] = jnp.swapaxes(jnp.stack(out), 0, 1)


def _dft_stage1_inv(z, x0, uv, bias, consts):
    bsz, n, cw = x0.shape
    nf, n2 = consts["nf"], consts["n2"]
    hr = consts["n1"] // 2
    tokv = lambda t: t.reshape(bsz, hr, n2, cw)
    tc = _pick(cw, HY_CTILE, LANE)
    tile = pl.BlockSpec((1, hr, HY_S2, tc), lambda b, j, c: (b, 0, j, c))
    out = pl.pallas_call(
        _dft3_body,
        grid=(bsz, n2 // HY_S2, cw // tc),
        in_specs=[pl.BlockSpec((hr, 2 * nf), lambda b, j, c: (0, 0)),
                  pl.BlockSpec((1, 2, nf, HY_S2, tc), lambda b, j, c: (b, 0, 0, j, c)),
                  tile, tile,
                  pl.BlockSpec((1, tc), lambda b, j, c: (0, c))],
        out_specs=tile,
        out_shape=jax.ShapeDtypeStruct((bsz, hr, n2, cw), F32),
        compiler_params=_cp(("parallel", "parallel", "parallel")),
        name="hyena_dft3",
    )(consts["f1inv"][:hr], z, tokv(x0), tokv(uv), bias.reshape(1, cw).astype(F32))
    return out.reshape(bsz, n, cw)


def _hyena(p_lat, p):
    n = p_lat.shape[1]
    consts = _dft_consts(n)
    n1 = consts["n1"]
    x0, uv = _hyena_pre(p_lat, p["conv_w"], p["conv_b"])
    taps, sabs = _hyena_kernel_taps(n, p)
    cw = taps.shape[1]
    ones = jnp.ones((1, cw), F32)
    hk = _dft_stage1(taps[None], consts, n1)
    hspec = _dft_mid(hk, jnp.zeros((1, 2, 1, HY_N2, cw), F32), 1.0 / sabs, consts, True)
    a = _dft_stage1(uv, consts, n1 // 2)
    z = _dft_mid(a, hspec, ones, consts, False)
    return _dft_stage1_inv(z, x0, uv, p["hy_bias"], consts)


def _mod_vectors(c, c_ctx, p):
    bsz, d = c.shape
    cvec = jnp.concatenate([c, c_ctx[None], jnp.zeros((8 - bsz - 1, d), c.dtype)], axis=0).astype(F32)
    m = _mods(cvec, p["w_mod"].astype(F32), p["b_mod"].astype(F32))
    lat = [v[:, None, :] for v in jnp.split(m[:bsz], N_MOD, axis=-1)]
    cx = [jnp.broadcast_to(v[:, None, :], (bsz, 1, d)) for v in jnp.split(m[bsz:bsz + 1], N_MOD, axis=-1)]
    return lat, cx


def _affine(norm_g, shift, scale):
    return norm_g.astype(F32) * (1.0 + scale), shift


def kernel(x, c, ctx, c_ctx,
           l0_w_mod, l0_b_mod, l0_norm1_g, l0_norm2_g, l0_w_in, l0_lam_q1, l0_lam_k1, l0_lam_q2, l0_lam_k2,
           l0_subln_g, l0_mu_prev, l0_mu_next, l0_w0_f, l0_w2_f, l0_a0_f, l0_a2_f, l0_w0_b, l0_w2_b, l0_a0_b,
           l0_a2_b, l0_g2, l0_k_k, l0_k_a, l0_r_k, l0_lnx_g, l0_lnx_b, l0_w_out, l0_mlp_w1, l0_mlp_w2,
           l1_w_mod, l1_b_mod, l1_norm1_g, l1_norm2_g, l1_w_in, l1_conv_w, l1_conv_b, l1_f_w1, l1_f_b1,
           l1_f_w2, l1_f_b2, l1_f_w3, l1_f_b3, l1_f_w4, l1_f_freq, l1_hy_bias, l1_q_norm_g, l1_k_norm_g,
           l1_w_out, l1_mlp_w1, l1_mlp_w2, final_g):
    bsz, n, d = x.shape
    x = x.astype(F32)
    ctx_s = ctx.astype(F32)
    bf = lambda w: w.astype(BF16)

    p0 = dict(w_mod=l0_w_mod, b_mod=l0_b_mod, mu_prev=l0_mu_prev, mu_next=l0_mu_next, w0_f=l0_w0_f, w2_f=l0_w2_f,
              a0_f=l0_a0_f, a2_f=l0_a2_f, w0_b=l0_w0_b, w2_b=l0_w2_b, a0_b=l0_a0_b, a2_b=l0_a2_b, g2=l0_g2,
              k_k=l0_k_k, k_a=l0_k_a, r_k=l0_r_k, lnx_g=l0_lnx_g, lnx_b=l0_lnx_b)
    (sh1, sc1, g1, sh2, sc2, g2), (csh1, csc1, cg1, csh2, csc2, cg2) = _mod_vectors(c, c_ctx, p0)
    aw = d // 2
    w_rw = l0_w_in[:, 3 * aw:]
    rw_cols = -(-(w_rw.shape[1] + 2 * LANE - B_GATE_RANK) // 512) * 512
    w_in0 = bf(jnp.concatenate([w_rw, jnp.zeros((d, rw_cols - w_rw.shape[1]), w_rw.dtype), l0_w_in[:, :3 * aw]],
                               axis=1))
    a1, b1 = _affine(l0_norm1_g, sh1, sc1)
    ca1, cb1 = _affine(l0_norm1_g, csh1, csc1)
    p_lat0 = _inproj(x, a1, b1, w_in0)
    p_ctx0 = _inproj(ctx_s, ca1, cb1, w_in0)
    lam_init = 0.8 - 0.6 * math.exp(-0.3 * 0)
    lam = (jnp.exp(jnp.sum(l0_lam_q1 * l0_lam_k1).astype(F32))
           - jnp.exp(jnp.sum(l0_lam_q2 * l0_lam_k2).astype(F32)) + lam_init)
    oa_lat, oa_ctx = _diff_attention(p_lat0, p_ctx0, rw_cols, aw, lam, lam_init, l0_subln_g)
    nh, hd = l0_r_k.shape
    zero = jnp.zeros((bsz, hd, nh * hd), F32)
    y_ctx, states_c = _rwkv_run(_rwkv_prepare(p_ctx0, p0, rw_cols), (zero, zero), p0)
    y_lat, _ = _rwkv_run(_rwkv_prepare(p_lat0, p0, rw_cols), states_c, p0)
    w_out0 = bf(l0_w_out)
    x = _outproj(oa_lat, y_lat, w_out0, x, g1)
    ctx_s = _outproj(oa_ctx, y_ctx, w_out0, ctx_s, cg1)
    w1, w2 = bf(l0_mlp_w1), bf(l0_mlp_w2)
    a2, b2 = _affine(l0_norm2_g, sh2, sc2)
    ca2, cb2 = _affine(l0_norm2_g, csh2, csc2)
    x = _mlp(x, a2, b2, g2, w1, w2)
    ctx_s = _mlp(ctx_s, ca2, cb2, cg2, w1, w2)

    p1 = dict(w_mod=l1_w_mod, b_mod=l1_b_mod, conv_w=l1_conv_w, conv_b=l1_conv_b, f_w1=l1_f_w1, f_b1=l1_f_b1,
              f_w2=l1_f_w2, f_b2=l1_f_b2, f_w3=l1_f_w3, f_b3=l1_f_b3, f_w4=l1_f_w4, f_freq=l1_f_freq,
              hy_bias=l1_hy_bias)
    (sh1, sc1, g1, sh2, sc2, g2), (csh1, csc1, _, _, _, _) = _mod_vectors(c, c_ctx, p1)
    hy_cols = l1_conv_w.shape[1]
    kv_cols = 2 * (2 * D_HEAD)
    w_in1 = bf(l1_w_in)
    a1, b1 = _affine(l1_norm1_g, sh1, sc1)
    ca1, cb1 = _affine(l1_norm1_g, csh1, csc1)
    p_lat = _inproj(x, a1, b1, w_in1)
    pkv_ctx = _inproj(ctx_s, ca1, cb1, w_in1[:, -kv_cols:])
    o_hy = _hyena(p_lat, p1)
    o_at = _gqa_attention(p_lat, pkv_ctx, hy_cols, l1_q_norm_g, l1_k_norm_g)
    x = _outproj(o_hy, o_at, bf(l1_w_out), x, g1)
    a2, b2 = _affine(l1_norm2_g, sh2, sc2)
    return _mlp(x, a2, b2, g2, bf(l1_mlp_w1), bf(l1_mlp_w2), final_g=final_g)
```

```python
import functools
import math

import numpy as np
import jax
import jax.numpy as jnp
from jax import lax
from jax.experimental import pallas as pl
from jax.experimental.pallas import tpu as pltpu

F32 = jnp.float32
BF16 = jnp.bfloat16
HI = lax.Precision.HIGHEST

NORM_EPS = 1e-6
ROPE_THETA = 10000.0
GRID_W = 64
N_MOD = 6
A_QK_DIM = 64
B_HEAD = 64
B_DECAY_RANK = 64
B_ICL_RANK = 64
B_GATE_RANK = 160
B_GN_EPS = 64e-5
HY_EMB = 33
HY_ORDER = 64
HY_TARGET = 1e-2
HY_FAST_PCT = 0.3
HY_SLOW_PCT = 1.5
D_HEAD = 128
D_KV_GROUP = 4

LANE = 128
VMEM_LIMIT = 56 * 1024 * 1024
LOG2E = 1.4426950408889634


def _cp(sem, vmem=VMEM_LIMIT):
    return pltpu.CompilerParams(dimension_semantics=sem, vmem_limit_bytes=vmem)


def _pick(n, pref, step=8):
    t = max(step, min(n, pref) // step * step)
    while n % t:
        t -= step
    return t


def _mods_body(c_ref, w_ref, b_ref, o_ref):
    c = c_ref[...]
    s = c * jax.nn.sigmoid(c)
    o_ref[...] = jnp.dot(s, w_ref[...], preferred_element_type=F32, precision=HI) + b_ref[...]


def _mods(cvec, w_mod, b_mod):
    m, d = cvec.shape
    n = w_mod.shape[1]
    tn = _pick(n, 1024, LANE)
    return pl.pallas_call(
        _mods_body,
        grid=(n // tn,),
        in_specs=[pl.BlockSpec((m, d), lambda j: (0, 0)),
                  pl.BlockSpec((d, tn), lambda j: (0, j)),
                  pl.BlockSpec((1, tn), lambda j: (0, j))],
        out_specs=pl.BlockSpec((m, tn), lambda j: (0, j)),
        out_shape=jax.ShapeDtypeStruct((m, n), F32),
        compiler_params=_cp(("arbitrary",)),
        name="mods",
    )(cvec, w_mod, b_mod.reshape(1, n))


def _norm_mod(x, a, b):
    ms = jnp.mean(x * x, axis=-1, keepdims=True)
    return x * lax.rsqrt(ms + NORM_EPS) * a + b


def _inproj_body(x_ref, a_ref, b_ref, w_ref, o_ref, xn_ref):
    @pl.when(pl.program_id(2) == 0)
    def _():
        xn_ref[...] = _norm_mod(x_ref[0], a_ref[0], b_ref[0]).astype(BF16)

    o_ref[0] = jnp.dot(xn_ref[...], w_ref[...], preferred_element_type=F32).astype(o_ref.dtype)


def _inproj(x, a, b, w, out_dtype=BF16, tm_pref=1024, tn_pref=512):
    bsz, n, d = x.shape
    nn = w.shape[1]
    tm, tn = _pick(n, tm_pref), _pick(nn, tn_pref, LANE)
    return pl.pallas_call(
        _inproj_body,
        grid=(bsz, n // tm, nn // tn),
        in_specs=[pl.BlockSpec((1, tm, d), lambda bi, i, j: (bi, i, 0)),
                  pl.BlockSpec((1, 1, d), lambda bi, i, j: (bi, 0, 0)),
                  pl.BlockSpec((1, 1, d), lambda bi, i, j: (bi, 0, 0)),
                  pl.BlockSpec((d, tn), lambda bi, i, j: (0, j))],
        out_specs=pl.BlockSpec((1, tm, tn), lambda bi, i, j: (bi, i, j)),
        out_shape=jax.ShapeDtypeStruct((bsz, n, nn), out_dtype),
        scratch_shapes=[pltpu.VMEM((tm, d), BF16)],
        compiler_params=_cp(("parallel", "parallel", "arbitrary")),
        name="inproj",
    )(x, a, b, w)


def _outproj_body(oa_ref, ob_ref, wa_ref, wb_ref, x_ref, g_ref, y_ref):
    acc = jnp.dot(oa_ref[0].astype(BF16), wa_ref[...], preferred_element_type=F32)
    acc += jnp.dot(ob_ref[0].astype(BF16), wb_ref[...], preferred_element_type=F32)
    y_ref[0] = x_ref[0] + g_ref[0] * acc


def _outproj(oa, ob, w, x, g, tm_pref=1024, tn_pref=512):
    bsz, n, ka = oa.shape
    kb = ob.shape[2]
    d = w.shape[1]
    tm, tn = _pick(n, tm_pref), _pick(d, tn_pref, LANE)
    return pl.pallas_call(
        _outproj_body,
        grid=(bsz, n // tm, d // tn),
        in_specs=[pl.BlockSpec((1, tm, ka), lambda bi, i, j: (bi, i, 0)),
                  pl.BlockSpec((1, tm, kb), lambda bi, i, j: (bi, i, 0)),
                  pl.BlockSpec((ka, tn), lambda bi, i, j: (0, j)),
                  pl.BlockSpec((kb, tn), lambda bi, i, j: (0, j)),
                  pl.BlockSpec((1, tm, tn), lambda bi, i, j: (bi, i, j)),
                  pl.BlockSpec((1, 1, tn), lambda bi, i, j: (bi, 0, j))],
        out_specs=pl.BlockSpec((1, tm, tn), lambda bi, i, j: (bi, i, j)),
        out_shape=jax.ShapeDtypeStruct((bsz, n, d), F32),
        compiler_params=_cp(("parallel", "parallel", "parallel")),
        name="outproj",
    )(oa, ob, w[:ka], w[ka:], x, g)


def _mlp_body(x_ref, a_ref, b_ref, g_ref, w1_ref, w2_ref, fg_ref, y_ref, xn_ref, acc_ref, *, nf, final):
    f = pl.program_id(2)

    @pl.when(f == 0)
    def _():
        xn_ref[...] = _norm_mod(x_ref[0], a_ref[0], b_ref[0]).astype(BF16)
        acc_ref[...] = jnp.zeros_like(acc_ref)

    h = jnp.dot(xn_ref[...], w1_ref[...], preferred_element_type=F32)
    h = jnp.square(jnp.maximum(h, 0.0)).astype(BF16)
    acc_ref[...] += jnp.dot(h, w2_ref[...], preferred_element_type=F32)

    @pl.when(f == nf - 1)
    def _():
        y = x_ref[0] + g_ref[0] * acc_ref[...]
        if final:
            ms = jnp.mean(y * y, axis=-1, keepdims=True)
            y = y * lax.rsqrt(ms + NORM_EPS) * fg_ref[...]
        y_ref[0] = y


def _mlp(x, a, b, g, w1, w2, final_g=None, tm_pref=512, tf_pref=1024):
    bsz, n, d = x.shape
    dff = w1.shape[1]
    tm, tf = _pick(n, tm_pref), _pick(dff, tf_pref, LANE)
    nf = dff // tf
    final = final_g is not None
    fg = (final_g if final else jnp.ones((d,), F32)).reshape(1, d).astype(F32)
    return pl.pallas_call(
        functools.partial(_mlp_body, nf=nf, final=final),
        grid=(bsz, n // tm, nf),
        in_specs=[pl.BlockSpec((1, tm, d), lambda bi, i, f: (bi, i, 0)),
                  pl.BlockSpec((1, 1, d), lambda bi, i, f: (bi, 0, 0)),
                  pl.BlockSpec((1, 1, d), lambda bi, i, f: (bi, 0, 0)),
                  pl.BlockSpec((1, 1, d), lambda bi, i, f: (bi, 0, 0)),
                  pl.BlockSpec((d, tf), lambda bi, i, f: (0, f)),
                  pl.BlockSpec((tf, d), lambda bi, i, f: (f, 0)),
                  pl.BlockSpec((1, d), lambda bi, i, f: (0, 0))],
        out_specs=pl.BlockSpec((1, tm, d), lambda bi, i, f: (bi, i, 0)),
        out_shape=jax.ShapeDtypeStruct((bsz, n, d), F32),
        scratch_shapes=[pltpu.VMEM((tm, d), BF16), pltpu.VMEM((tm, d), F32)],
        compiler_params=_cp(("parallel", "parallel", "arbitrary")),
        name="mlp",
    )(x, a, b, g, w1, w2, fg)


def _rope_tables(n, dim, reps):
    rows = n // GRID_W
    row = jnp.repeat(jnp.arange(rows, dtype=F32), GRID_W)
    col = jnp.tile(jnp.arange(GRID_W, dtype=F32), rows)
    half = dim // 2
    inv = ROPE_THETA ** (-jnp.arange(0, half, 2, dtype=F32) / half)
    ar, ac = row[:, None] * inv, col[:, None] * inv
    cos = jnp.concatenate([jnp.cos(ar), jnp.cos(ar), jnp.cos(ac), jnp.cos(ac)], axis=-1)
    sin = jnp.concatenate([-jnp.sin(ar), jnp.sin(ar), -jnp.sin(ac), jnp.sin(ac)], axis=-1)
    return jnp.tile(cos, (1, reps)), jnp.tile(sin, (1, reps))


def _swap_matrix(dim, reps):
    q = dim // 4
    width = dim * reps
    p = np.zeros((width, width), np.float32)
    for j in range(width):
        base, r = (j // dim) * dim, j % dim
        axis, which, f = r // (2 * q), (r % (2 * q)) // q, r % q
        p[base + axis * 2 * q + (1 - which) * q + f, j] = 1.0
    return p


def _rope_norm(x, cos_ref, sin_ref, p_ref, g_ref, gs_ref, *, norm, rope, scale=1.0):
    y = x.astype(F32)
    if rope:
        ys = jnp.dot(x, p_ref[...], preferred_element_type=F32)
    if norm:
        rs = lax.rsqrt(jnp.mean(y * y, axis=-1, keepdims=True) + NORM_EPS)
        y = y * rs * g_ref[...]
        if rope:
            ys = ys * rs * gs_ref[...]
    if rope:
        y = y * cos_ref[...] + ys * sin_ref[...]
    return y * scale if scale != 1.0 else y


def _rope_operands(n, dim, gain, rope):
    reps = LANE // dim
    if rope:
        cos, sin = _rope_tables(n, dim, reps)
        pm = jnp.asarray(_swap_matrix(dim, reps), BF16)
    else:
        cos = sin = jnp.zeros((n, LANE), F32)
        pm = jnp.zeros((LANE, LANE), BF16)
    if gain is not None:
        g = jnp.tile(gain.astype(F32), reps).reshape(1, LANE)
        gs = jnp.dot(g, jnp.asarray(_swap_matrix(dim, reps)), precision=HI)
    else:
        g = gs = jnp.ones((1, LANE), F32)
    return cos, sin, pm, g, gs


def _kprep_body(x_ref, cos_ref, sin_ref, p_ref, g_ref, gs_ref, o_ref, *, heads, norm, rope):
    for h in range(heads):
        sl = slice(h * LANE, (h + 1) * LANE)
        y = _rope_norm(x_ref[0, :, sl], cos_ref, sin_ref, p_ref, g_ref, gs_ref, norm=norm, rope=rope)
        o_ref[0, :, sl] = y.astype(o_ref.dtype)


def _kprep(p, col0, heads, *, dim, gain=None, rope=True, tq_pref=512):
    bsz, n, _ = p.shape
    tq = _pick(n, tq_pref)
    hb = heads
    while col0 % (hb * LANE):
        hb //= 2
    w = hb * LANE
    cb = col0 // w
    cos, sin, pm, g, gs = _rope_operands(n, dim, gain, rope)
    return pl.pallas_call(
        functools.partial(_kprep_body, heads=hb, norm=gain is not None, rope=rope),
        grid=(bsz, n // tq, heads // hb),
        in_specs=[pl.BlockSpec((1, tq, w), lambda b, i, j: (b, i, cb + j)),
                  pl.BlockSpec((tq, LANE), lambda b, i, j: (i, 0)),
                  pl.BlockSpec((tq, LANE), lambda b, i, j: (i, 0)),
                  pl.BlockSpec((LANE, LANE), lambda b, i, j: (0, 0)),
                  pl.BlockSpec((1, LANE), lambda b, i, j: (0, 0)),
                  pl.BlockSpec((1, LANE), lambda b, i, j: (0, 0))],
        out_specs=pl.BlockSpec((1, tq, w), lambda b, i, j: (b, i, j)),
        out_shape=jax.ShapeDtypeStruct((bsz, n, heads * LANE), BF16),
        compiler_params=_cp(("parallel", "parallel", "parallel")),
        name="kprep",
    )(p, cos, sin, pm, g, gs)


FLASH_KEYS = 1280
FLASH_HEADROOM = 64.0


def _knorm_body(k_ref, o_ref, *, heads):
    for h in range(heads):
        k = k_ref[0, :, h * LANE:(h + 1) * LANE].astype(F32)
        nrm2 = jnp.max(jnp.sum(k * k, axis=1, keepdims=True), axis=0, keepdims=True)
        o_ref[0, h] = jnp.broadcast_to(jnp.sqrt(nrm2), (8, LANE))


def _key_block_norms(k, kcol0, heads, ts):
    bsz, nk, _ = k.shape
    nsub = nk // ts
    hb = heads
    while kcol0 % (hb * LANE):
        hb //= 2
    kb = kcol0 // (hb * LANE)
    out = pl.pallas_call(
        functools.partial(_knorm_body, heads=hb),
        grid=(bsz, heads // hb, nsub),
        in_specs=[pl.BlockSpec((1, ts, hb * LANE), lambda b, j, c: (b, c, kb + j))],
        out_specs=pl.BlockSpec((1, hb, 8, LANE), lambda b, j, c: (b, j, c, 0)),
        out_shape=jax.ShapeDtypeStruct((bsz, heads, nsub * 8, LANE), F32),
        compiler_params=_cp(("parallel", "parallel", "parallel")),
        name="key_block_norms",
    )(k)
    return out[:, :, ::8, 0].reshape(-1)


def _flash_body(kmax_ref, qr_ref, cos_ref, sin_ref, pm_ref, g_ref, gs_ref, k_ref, v_ref, e1_ref, e2_ref, o_ref,
                q_ref, m_ref, l_ref, acc_ref, thr_ref, *, ts, nsub, mode, tq, norm, rope, scale):
    base = (pl.program_id(0) * pl.num_programs(1) + pl.program_id(1)) * nsub
    prep = functools.partial(_rope_norm, cos_ref=cos_ref, sin_ref=sin_ref, p_ref=pm_ref, g_ref=g_ref, gs_ref=gs_ref,
                             norm=norm, rope=rope, scale=scale)
    if mode == "diff":
        y = prep(qr_ref[0])
        lane = lax.broadcasted_iota(jnp.int32, y.shape, 1)
        q_ref[:tq] = jnp.where(lane < A_QK_DIM, y, 0.0).astype(BF16)
        q_ref[tq:] = jnp.where(lane >= A_QK_DIM, y, 0.0).astype(BF16)
    else:
        for j in range(D_KV_GROUP):
            q_ref[j * tq:(j + 1) * tq] = prep(qr_ref[0, :, j * LANE:(j + 1) * LANE]).astype(BF16)
    q = q_ref[...]

    def scores(off, size):
        k = k_ref[0, pl.ds(off, size), :]
        return lax.dot_general(q, k, (((1,), (1,)), ((), ())), preferred_element_type=F32)

    def block(c):
        off = pl.multiple_of(c * ts, ts)
        return scores(off, ts), v_ref[0, pl.ds(off, ts), :]

    def wide(m):
        return jnp.concatenate([m] * (ts // LANE), axis=1)

    m0 = jnp.broadcast_to(jnp.max(scores(0, 2 * LANE), axis=1, keepdims=True), m_ref.shape)
    m_ref[...] = m0
    l_ref[...] = jnp.zeros_like(l_ref)
    acc_ref[...] = jnp.zeros_like(acc_ref)
    qn2 = jnp.dot(q * q, jnp.ones((LANE, LANE), BF16), preferred_element_type=F32)
    ratio = jnp.where(qn2 > 0.0, (m0 + FLASH_HEADROOM) * lax.rsqrt(qn2), jnp.inf)
    thr_ref[0] = jnp.min(ratio)

    def step(c, carry):
        fixed = kmax_ref[base + c] * 1.02 <= thr_ref[0]

        @pl.when(fixed)
        def _():
            s, v = block(c)
            p = jnp.exp2(s - wide(m_ref[...]))
            l_ref[...] += jnp.sum(p, axis=1, keepdims=True)
            acc_ref[...] += jnp.dot(p.astype(BF16), v, preferred_element_type=F32)

        @pl.when(jnp.logical_not(fixed))
        def _():
            s, v = block(c)
            m_prev = m_ref[...]
            m_new = jnp.maximum(m_prev, jnp.max(s, axis=1, keepdims=True))
            alpha = jnp.exp2(m_prev - m_new)
            p = jnp.exp2(s - wide(m_new))
            l_ref[...] = alpha * l_ref[...] + jnp.sum(p, axis=1, keepdims=True)
            acc_ref[...] = alpha * acc_ref[...] + jnp.dot(p.astype(BF16), v, preferred_element_type=F32)
            m_ref[...] = m_new

        return carry

    lax.fori_loop(0, nsub, step, 0)

    o = acc_ref[...] / l_ref[...]
    if mode == "diff":
        d = o[:tq] - e1_ref[...] * o[tq:]
        ms = jnp.mean(d * d, axis=-1, keepdims=True)
        o_ref[0] = (d * lax.rsqrt(ms + NORM_EPS) * e2_ref[...]).astype(o_ref.dtype)
    else:
        for j in range(D_KV_GROUP):
            o_ref[0, :, j * LANE:(j + 1) * LANE] = o[j * tq:(j + 1) * tq].astype(o_ref.dtype)


def _flash(pq, qcol0, k, v, kcol0, vcol0, e1, e2, *, mode, hk, dim, tq, gain=None, rope=True):
    bsz, n, _ = pq.shape
    nk = k.shape[1]
    g = 2 if mode == "diff" else D_KV_GROUP
    mq = g * tq
    qw = LANE if mode == "diff" else D_KV_GROUP * LANE
    qb = qcol0 // qw
    ts = _pick(nk, FLASH_KEYS, 2 * LANE)
    kb, vb = kcol0 // LANE, vcol0 // LANE
    ow = LANE if mode == "diff" else D_KV_GROUP * LANE
    kmax = _key_block_norms(k, kcol0, hk, ts)
    cos, sin, pm, gq, gqs = _rope_operands(n, dim, gain, rope)
    const = lambda shape: pl.BlockSpec(shape, lambda b, h, i: (0, 0))
    return pl.pallas_call(
        functools.partial(_flash_body, ts=ts, nsub=nk // ts, mode=mode, tq=tq, norm=gain is not None, rope=rope,
                          scale=dim ** -0.5 * LOG2E),
        grid=(bsz, hk, n // tq),
        in_specs=[pl.BlockSpec(memory_space=pltpu.SMEM),
                  pl.BlockSpec((1, tq, qw), lambda b, h, i: (b, i, qb + h)),
                  pl.BlockSpec((tq, LANE), lambda b, h, i: (i, 0)),
                  pl.BlockSpec((tq, LANE), lambda b, h, i: (i, 0)),
                  const((LANE, LANE)), const((1, LANE)), const((1, LANE)),
                  pl.BlockSpec((1, nk, LANE), lambda b, h, i: (b, 0, kb + h)),
                  pl.BlockSpec((1, nk, LANE), lambda b, h, i: (b, 0, vb + h)),
                  const((1, LANE)), const((1, LANE))],
        out_specs=pl.BlockSpec((1, tq, ow), lambda b, h, i: (b, i, h)),
        out_shape=jax.ShapeDtypeStruct((bsz, n, hk * ow), BF16),
        scratch_shapes=[pltpu.VMEM((mq, LANE), BF16), pltpu.VMEM((mq, LANE), F32), pltpu.VMEM((mq, LANE), F32),
                        pltpu.VMEM((mq, LANE), F32), pltpu.SMEM((1,), F32)],
        compiler_params=_cp(("parallel", "parallel", "arbitrary")),
        name="flash_" + mode,
    )(kmax, pq, cos, sin, pm, gq, gqs, k, v, e1, e2)


def _diff_attention(pa_lat, pa_ctx, c0, aw, lam, lam_init, subln_g):
    n, nctx = pa_lat.shape[1], pa_ctx.shape[1]
    heads = aw // LANE
    k_l = _kprep(pa_lat, c0 + aw, heads, dim=A_QK_DIM)
    k_all = jnp.concatenate([k_l, pa_ctx[..., c0 + aw:c0 + 2 * aw]], axis=1)
    v_all = jnp.concatenate([pa_lat[..., c0 + 2 * aw:c0 + 3 * aw], pa_ctx[..., c0 + 2 * aw:c0 + 3 * aw]], axis=1)
    e1 = jnp.full((1, LANE), lam, F32)
    e2 = (subln_g.astype(F32) * (1.0 - lam_init)).reshape(1, LANE)
    o_lat = _flash(pa_lat, c0, k_all, v_all, 0, 0, e1, e2, mode="diff", hk=heads, dim=A_QK_DIM, tq=_pick(n, 1024))
    o_ctx = _flash(pa_ctx, c0, pa_ctx, pa_ctx, c0 + aw, c0 + 2 * aw, e1, e2, mode="diff", hk=heads, dim=A_QK_DIM,
                   tq=_pick(nctx, 512), rope=False)
    return o_lat, o_ctx


def _gqa_attention(p_lat, pkv_ctx, qcol0, q_norm_g, k_norm_g):
    n = p_lat.shape[1]
    hq, hk = 2 * D_KV_GROUP, 2
    kcol0 = qcol0 + hq * D_HEAD
    vcol0 = kcol0 + hk * D_HEAD
    k_l = _kprep(p_lat, kcol0, hk, dim=D_HEAD, gain=k_norm_g)
    k_c = _kprep(pkv_ctx, 0, hk, dim=D_HEAD, gain=k_norm_g, rope=False)
    k_all = jnp.concatenate([k_l, k_c], axis=1)
    v_all = jnp.concatenate([p_lat[..., vcol0:vcol0 + hk * D_HEAD], pkv_ctx[..., hk * D_HEAD:]], axis=1)
    dummy = jnp.zeros((1, LANE), F32)
    return _flash(p_lat, qcol0, k_all, v_all, 0, 0, dummy, dummy, mode="gqa", hk=hk, dim=D_HEAD,
                  tq=_pick(n, 512), gain=q_norm_g)


RW_CHUNK = 64
RW_HALO = 8


def _mm_hi(a, b):
    return jnp.dot(a, b, preferred_element_type=F32, precision=HI)


def _mm_bf(a, b):
    return jnp.dot(a.astype(BF16), b.astype(BF16), preferred_element_type=F32)


def _split3(x):
    x1 = x.astype(BF16)
    r1 = x - x1.astype(F32)
    x2 = r1.astype(BF16)
    return x1, x2, (r1 - x2.astype(F32)).astype(BF16)


def _mm_x3(a, b):
    ah, al, _ = _split3(a)
    bh, bl, _ = _split3(b)
    d = lambda u, w: jnp.dot(u, w, preferred_element_type=F32)
    return d(ah, bh) + (d(ah, bl) + d(al, bh))


def _mm_sel(z, sel):
    z1, z2, z3 = _split3(z)
    d = lambda u: jnp.dot(u, sel, preferred_element_type=F32)
    return d(z1) + (d(z2) + d(z3))


def _rwprep_body(x_ref, hp_ref, hn_ref, mup_ref, mun_ref, vec_ref, w2f_ref, w2b_ref, a2f_ref, a2b_ref,
                 g2_ref, sel_ref, selt_ref,
                 r_ref, v_ref, kk_ref, lwf_ref, bf_ref, kf_ref, lwb_ref, bb_ref, kb_ref, g_ref, bonus_ref,
                 *, nb, tm, bw):
    i = pl.program_id(1)
    x = x_ref[0].astype(F32)
    row = lax.broadcasted_iota(jnp.int32, x.shape, 0)
    prev_edge = jnp.where(i > 0, hp_ref[0, RW_HALO - 1:RW_HALO, :].astype(F32), 0.0)
    next_edge = jnp.where(i < nb - 1, hn_ref[0, 0:1, :].astype(F32), 0.0)
    prev = jnp.where(row == 0, prev_edge, pltpu.roll(x, 1, 0))
    nxt = jnp.where(row == tm - 1, next_edge, pltpu.roll(x, tm - 1, 0))
    xs = x + mup_ref[...] * (prev - x) + mun_ref[...] * (nxt - x)
    r, k, v = xs[:, :bw], xs[:, bw:2 * bw], xs[:, 2 * bw:3 * bw]
    lr = xs[:, 3 * bw:3 * bw + LANE]
    gd = xs[:, 3 * bw + LANE:3 * bw + 3 * LANE]
    k_k, k_a, r_k = vec_ref[0:1], vec_ref[1:2], vec_ref[2:3]
    w0f, a0f, w0b, a0b = vec_ref[3:4], vec_ref[4:5], vec_ref[5:6], vec_ref[6:7]
    sel, selt = sel_ref[...], selt_ref[...]

    def head_sum(z):
        return _mm_sel(_mm_sel(z, sel), selt)

    kk = k * k_k
    kk = kk * lax.rsqrt(head_sum(kk * kk) + 1e-12)
    g_ref[0] = _mm_bf(jax.nn.sigmoid(gd), g2_ref[...]).astype(g_ref.dtype)
    th = jnp.tanh(lr)
    ksum = jnp.zeros_like(k)
    for (w0, a0, w2_ref, a2_ref, lw_ref, b_ref, kd_ref) in (
            (w0f, a0f, w2f_ref, a2f_ref, lwf_ref, bf_ref, kf_ref),
            (w0b, a0b, w2b_ref, a2b_ref, lwb_ref, bb_ref, kb_ref)):
        logw = -math.exp(-0.5) * jax.nn.sigmoid(w0 + _mm_bf(th, w2_ref[...]))
        a = jax.nn.sigmoid(a0 + _mm_bf(lr, a2_ref[...]))
        k_d = k * (1.0 + (a - 1.0) * k_a)
        ksum = ksum + k_d
        lw_ref[0] = logw
        b_ref[0] = kk * a
        kd_ref[0] = k_d
    r_ref[0] = r
    v_ref[0] = v
    kk_ref[0] = kk
    bonus_ref[0] = head_sum(r * ksum * r_k) * v


def _rwkv_prepare(pr, p, cols):
    bsz, n, _ = pr.shape
    bw = p["w0_f"].shape[0]
    tm = _pick(n, 256)
    nb = n // tm
    hb = tm // RW_HALO
    assert cols - 3 * bw >= 3 * LANE and B_DECAY_RANK + B_ICL_RANK == LANE

    def padded(vec):
        return jnp.concatenate([vec.astype(F32), jnp.zeros((cols - vec.shape[0],), F32)]).reshape(1, cols)

    vecs = jnp.stack([p["k_k"], p["k_a"], p["r_k"].reshape(-1), p["w0_f"], p["a0_f"], p["w0_b"], p["a0_b"],
                      jnp.zeros((bw,), F32)]).astype(F32)
    zr = jnp.zeros((B_DECAY_RANK, bw), F32)
    w2 = {d: jnp.concatenate([p["w2_" + d].astype(F32), zr], axis=0).astype(BF16) for d in "fb"}
    a2 = {d: jnp.concatenate([zr, p["a2_" + d].astype(F32)], axis=0).astype(BF16) for d in "fb"}
    g2 = jnp.concatenate([p["g2"].astype(F32), jnp.zeros((2 * LANE - B_GATE_RANK, bw), F32)], axis=0).astype(BF16)
    sel, selt = _head_selectors(bw)

    full = lambda shape: pl.BlockSpec(shape, lambda b, i: tuple(0 for _ in shape))
    tok = pl.BlockSpec((1, tm, bw), lambda b, i: (b, i, 0))
    tok_f32 = jax.ShapeDtypeStruct((bsz, n, bw), F32)
    outs = pl.pallas_call(
        functools.partial(_rwprep_body, nb=nb, tm=tm, bw=bw),
        grid=(bsz, nb),
        in_specs=[pl.BlockSpec((1, tm, cols), lambda b, i: (b, i, 0)),
                  pl.BlockSpec((1, RW_HALO, cols), lambda b, i: (b, jnp.maximum(i * hb - 1, 0), 0)),
                  pl.BlockSpec((1, RW_HALO, cols), lambda b, i: (b, jnp.minimum((i + 1) * hb, nb * hb - 1), 0)),
                  full((1, cols)), full((1, cols)), full((8, bw)),
                  full((LANE, bw)), full((LANE, bw)), full((LANE, bw)), full((LANE, bw)),
                  full((2 * LANE, bw)), full((bw, LANE)), full((LANE, bw))],
        out_specs=[tok] * 11,
        out_shape=[tok_f32] * 9 + [jax.ShapeDtypeStruct((bsz, n, bw), BF16), tok_f32],
        compiler_params=_cp(("parallel", "parallel")),
        name="rwkv_prepare",
    )(pr, pr, pr, padded(p["mu_prev"]), padded(p["mu_next"]), vecs, w2["f"], w2["b"], a2["f"], a2["b"],
      g2, sel, selt)
    r, v, kk, lwf, bf, kf, lwb, bb, kb, g, bonus = outs
    return dict(r=r, v=v, kk=kk, g=g, bonus=bonus, f=(lwf, bf, kf), b=(lwb, bb, kb))


RW_GROUP = 4


def _bd(x):
    t = jnp.concatenate([x.astype(BF16)] * RW_GROUP, axis=1)
    r = lax.broadcasted_iota(jnp.int32, t.shape, 1) // B_HEAD
    c = lax.broadcasted_iota(jnp.int32, t.shape, 2) // B_HEAD
    return jnp.where(r == c, t, jnp.zeros_like(t))


def _bdot(a, b, dims):
    return jnp.einsum(dims, a.astype(BF16), b.astype(BF16), preferred_element_type=F32)


def _groups(x):
    gw = RW_GROUP * B_HEAD
    return jnp.stack([x[:, g * gw:(g + 1) * gw] for g in range(x.shape[1] // gw)])


def _rw_decays(r, kk, b, kd, lw, reverse):
    ck = r.shape[0]
    ti = lax.broadcasted_iota(jnp.int32, (ck, ck), 0)
    si = lax.broadcasted_iota(jnp.int32, (ck, ck), 1)
    tri = ((si >= ti) if reverse else (si <= ti)).astype(BF16)
    l1, l2, l3 = _split3(lw)
    tdot = lambda z: jnp.dot(tri, z, preferred_element_type=F32)
    cum = tdot(l1) + (tdot(l2) + tdot(l3))
    total = cum[0:1] if reverse else cum[ck - 1:ck]
    winv = jnp.exp(-cum)
    wrest = jnp.exp(total - cum)
    return (-kk * jnp.exp(cum - lw), b * winv, kd * winv, r * jnp.exp(cum), b * wrest, kd * wrest, jnp.exp(total))


def _rw_chunks(fwd, bwd, s_f, s_b):
    nseq = len(fwd)
    ck, w = fwd[0][0].shape
    gw = RW_GROUP * B_HEAD
    ng = w // gw
    parts = []
    for probs, rev in ((fwd, False), (bwd, True)):
        for (r, v, kk, b, kd, lw) in probs:
            parts.append([_groups(z) for z in _rw_decays(r, kk, b, kd, lw, rev)] + [_groups(v)])
    at, bt, kt, rt, btw, ktw, decay, v = [jnp.concatenate(zs, axis=0) for zs in zip(*parts)]
    s = jnp.concatenate([_groups(z) for z in list(s_f) + list(s_b)], axis=0)
    nslab = 2 * nseq * ng
    shape = (nslab, ck, gw)
    lag = lax.broadcasted_iota(jnp.int32, shape, 1) - lax.broadcasted_iota(jnp.int32, shape, 2) % ck
    lag = jnp.where(lax.broadcasted_iota(jnp.int32, shape, 0) >= nseq * ng, -lag, lag)
    incl, strict = lag >= 0, lag > 0
    eye = (lag == 0).astype(F32)

    bdv = _bd(v)
    lhs = jnp.concatenate([at, rt], axis=1)
    lb, lk = _bdot(lhs, _bd(bt), "gtk,gnk->gtn"), _bdot(lhs, _bd(kt), "gtk,gnk->gtn")
    lab, mrb = jnp.where(strict, lb[:, :ck], 0.0), jnp.where(incl, lb[:, ck:], 0.0)
    lak, mrk = jnp.where(strict, lk[:, :ck], 0.0), jnp.where(incl, lk[:, ck:], 0.0)
    tinv = eye + lab
    pw = _bdot(lab, _bd(lab), "gtk,gkn->gtn")
    span = 4
    while span < ck:
        both = _bdot(jnp.concatenate([tinv, pw], axis=1), _bd(pw), "gtk,gkn->gtn")
        tinv, pw = tinv + both[:, :ck], both[:, ck:]
        span *= 2
    tinv = tinv + _bdot(tinv, _bd(pw), "gtk,gkn->gtn")
    vmix = _bdot(jnp.concatenate([lak, mrk], axis=1), bdv, "gtk,gkn->gtn")
    p1 = _bdot(tinv, _bd(at), "gtk,gkn->gtn")
    q1 = _bdot(tinv, _bd(vmix[:, :ck]), "gtk,gkn->gtn")
    smix = _bdot(jnp.concatenate([p1, rt], axis=1), _bd(s), "gtk,gnk->gtn")
    u = smix[:, :ck] + q1
    y = smix[:, ck:] + _bdot(mrb, _bd(u), "gtk,gkn->gtn") + vmix[:, ck:]
    z = _bdot(jnp.concatenate([u, v], axis=1), jnp.concatenate([btw, ktw], axis=1), "gtm,gtn->gmn")
    lane_head = lax.broadcasted_iota(jnp.int32, (nslab, B_HEAD, gw), 2) // B_HEAD
    upd = jnp.zeros_like(s)
    for h in range(RW_GROUP):
        upd = upd + jnp.where(lane_head == h, z[:, h * B_HEAD:(h + 1) * B_HEAD], 0.0)
    s_new = s * decay + upd
    lanes = lambda x, q: jnp.concatenate([x[q * ng + g] for g in range(ng)], axis=1)
    seqs = range(nseq)
    return ([lanes(y, q) for q in seqs], [lanes(y, nseq + q) for q in seqs],
            [lanes(s_new, q) for q in seqs], [lanes(s_new, nseq + q) for q in seqs])


def _rwscan_body(rf_ref, vf_ref, kkf_ref, bf_ref, kdf_ref, lwf_ref, rb_ref, vb_ref, kkb_ref, bb_ref, kdb_ref, lwb_ref,
                 s0f_ref, s0b_ref, yf_ref, yb_ref, sTf_ref, sTb_ref, sf_ref, sb_ref, *, nc, nseq):
    c = pl.program_id(0)

    @pl.when(c == 0)
    def _():
        sf_ref[...] = s0f_ref[...]
        sb_ref[...] = s0b_ref[...]

    seqs = range(nseq)
    yf, yb, sf, sb = _rw_chunks(
        [(rf_ref[q], vf_ref[q], kkf_ref[q], bf_ref[q], kdf_ref[q], lwf_ref[q]) for q in seqs],
        [(rb_ref[q], vb_ref[q], kkb_ref[q], bb_ref[q], kdb_ref[q], lwb_ref[q]) for q in seqs],
        [sf_ref[q] for q in seqs], [sb_ref[q] for q in seqs])
    for q in seqs:
        yf_ref[q] = yf[q]
        yb_ref[q] = yb[q]
        sf_ref[q] = sf[q]
        sb_ref[q] = sb[q]

    @pl.when(c == nc - 1)
    def _():
        for q in seqs:
            sTf_ref[q] = sf[q]
            sTb_ref[q] = sb[q]


def _rwkv_scan(r, v, kk, fwd, bwd, s0f, s0b):
    bsz, n, w = r.shape
    ck = RW_CHUNK
    assert n % ck == 0 and ck == B_HEAD and w % (RW_GROUP * B_HEAD) == 0
    nc = n // ck
    seq_f = pl.BlockSpec((bsz, ck, w), lambda c: (0, c, 0))
    seq_b = pl.BlockSpec((bsz, ck, w), lambda c: (0, nc - 1 - c, 0))
    st = pl.BlockSpec((bsz, B_HEAD, w), lambda c: (0, 0, 0))
    y_shape = jax.ShapeDtypeStruct((bsz, n, w), F32)
    s_shape = jax.ShapeDtypeStruct((bsz, B_HEAD, w), F32)
    (lwf, bf, kf), (lwb, bb, kb) = fwd, bwd
    return pl.pallas_call(
        functools.partial(_rwscan_body, nc=nc, nseq=bsz),
        grid=(nc,),
        in_specs=[seq_f] * 6 + [seq_b] * 6 + [st, st],
        out_specs=[seq_f, seq_b, st, st],
        out_shape=[y_shape, y_shape, s_shape, s_shape],
        scratch_shapes=[pltpu.VMEM((bsz, B_HEAD, w), F32), pltpu.VMEM((bsz, B_HEAD, w), F32)],
        compiler_params=_cp(("arbitrary",)),
        name="rwkv_scan",
    )(r, v, kk, bf, kf, lwf, r, v, kk, bb, kb, lwb, s0f, s0b)


def _rwpost_body(yf_ref, yb_ref, g_ref, bonus_ref, lg_ref, lb_ref, sel_ref, selt_ref, o_ref):
    sel, selt = sel_ref[...], selt_ref[...]
    head_mean = lambda z: _mm_sel(_mm_sel(z, sel), selt) * (1.0 / B_HEAD)
    y = yf_ref[0] + yb_ref[0]
    d = y - head_mean(y)
    y = d * lax.rsqrt(head_mean(d * d) + B_GN_EPS)
    y = y * lg_ref[...] + lb_ref[...] + bonus_ref[0]
    o_ref[0] = (y * g_ref[0].astype(F32)).astype(o_ref.dtype)


def _head_selectors(bw):
    sel_np = np.zeros((bw, LANE), np.float32)
    sel_np[np.arange(bw), np.arange(bw) // B_HEAD] = 1.0
    return jnp.asarray(sel_np, BF16), jnp.asarray(sel_np.T, BF16)


def _rwkv_post(y_f, y_b, g, bonus, lnx_g, lnx_b):
    bsz, n, bw = y_f.shape
    tm = _pick(n, 256)
    tok = pl.BlockSpec((1, tm, bw), lambda b, i: (b, i, 0))
    vec = pl.BlockSpec((1, bw), lambda b, i: (0, 0))
    sel, selt = _head_selectors(bw)
    return pl.pallas_call(
        _rwpost_body,
        grid=(bsz, n // tm),
        in_specs=[tok, tok, tok, tok, vec, vec,
                  pl.BlockSpec((bw, LANE), lambda b, i: (0, 0)), pl.BlockSpec((LANE, bw), lambda b, i: (0, 0))],
        out_specs=tok,
        out_shape=jax.ShapeDtypeStruct((bsz, n, bw), BF16),
        compiler_params=_cp(("parallel", "parallel")),
        name="rwkv_post",
    )(y_f, y_b, g, bonus, lnx_g.reshape(1, bw).astype(F32), lnx_b.reshape(1, bw).astype(F32), sel, selt)


def _rwkv_run(prep, states0, p):
    y_f, y_b, s_f, s_b = _rwkv_scan(prep["r"], prep["v"], prep["kk"], prep["f"], prep["b"], states0[0], states0[1])
    return _rwkv_post(y_f, y_b, prep["g"], prep["bonus"], p["lnx_g"], p["lnx_b"]), (s_f, s_b)


HY_N2 = 128
HY_HALO = 8


def _hypre_body(x_ref, hp_ref, hn_ref, cw_ref, cb_ref, x0_ref, uv_ref, *, nb, tm, cw):
    i = pl.program_id(1)
    x = x_ref[0].astype(F32)
    row = lax.broadcasted_iota(jnp.int32, x.shape, 0)
    prev_edge = jnp.where(i > 0, hp_ref[0, HY_HALO - 1:HY_HALO, :].astype(F32), 0.0)
    next_edge = jnp.where(i < nb - 1, hn_ref[0, 0:1, :].astype(F32), 0.0)
    prev = jnp.where(row == 0, prev_edge, pltpu.roll(x, 1, 0))
    nxt = jnp.where(row == tm - 1, next_edge, pltpu.roll(x, tm - 1, 0))
    u = prev * cw_ref[0:1] + x * cw_ref[1:2] + nxt * cw_ref[2:3] + cb_ref[...]
    x0_ref[0] = u[:, :cw]
    uv_ref[0] = u[:, 2 * cw:3 * cw] * u[:, cw:2 * cw]


def _hyena_pre(p, conv_w, conv_b):
    bsz, n, _ = p.shape
    hc = conv_w.shape[1]
    cw = hc // 3
    tm = _pick(n, 256)
    nb = n // tm
    hb = tm // HY_HALO
    cwp = jnp.concatenate([conv_w.astype(F32), jnp.zeros((5, hc), F32)], axis=0)
    out = pl.BlockSpec((1, tm, cw), lambda b, i: (b, i, 0))
    return pl.pallas_call(
        functools.partial(_hypre_body, nb=nb, tm=tm, cw=cw),
        grid=(bsz, nb),
        in_specs=[pl.BlockSpec((1, tm, hc), lambda b, i: (b, i, 0)),
                  pl.BlockSpec((1, HY_HALO, hc), lambda b, i: (b, jnp.maximum(i * hb - 1, 0), 0)),
                  pl.BlockSpec((1, HY_HALO, hc), lambda b, i: (b, jnp.minimum((i + 1) * hb, nb * hb - 1), 0)),
                  pl.BlockSpec((8, hc), lambda b, i: (0, 0)),
                  pl.BlockSpec((1, hc), lambda b, i: (0, 0))],
        out_specs=[out, out],
        out_shape=[jax.ShapeDtypeStruct((bsz, n, cw), F32)] * 2,
        compiler_params=_cp(("parallel", "parallel")),
        name="hyena_pre",
    )(p, p, p, cwp, conv_b.reshape(1, hc).astype(F32))


def _hyfilt_body(z_ref, t_ref, z0_ref, w1_ref, w2_ref, w3_ref, w4_ref, w4b_ref, vec_ref, dl_ref,
                 k_ref, s_ref, *, half_tiles):
    i = pl.program_id(0)
    b1, b2, b3, fr = vec_ref[0:1], vec_ref[1:2], vec_ref[2:3], vec_ref[3:4]

    def mlp3(z):
        h = jnp.sin(fr * (_mm_hi(z, w1_ref[...]) + b1))
        h = jnp.sin(fr * (_mm_hi(h, w2_ref[...]) + b2))
        h = jnp.sin(fr * (_mm_hi(h, w3_ref[...]) + b3))
        lane = lax.broadcasted_iota(jnp.int32, h.shape, 1)
        return jnp.where(lane < HY_ORDER, h, 0.0), jnp.where(lane >= HY_ORDER, h, 0.0)

    h_lo, h_hi = mlp3(z_ref[...])
    raw = jnp.concatenate([_mm_hi(h_lo, w4_ref[...]), _mm_hi(h_hi, w4_ref[...])], axis=0)
    raw = raw * jnp.exp(-t_ref[...] * dl_ref[...])

    @pl.when(i == 0)
    def _():
        s_ref[...] = jnp.zeros_like(s_ref)

    s_ref[...] += jnp.sum(jnp.abs(raw), axis=0, keepdims=True)
    hb0 = _mm_hi(mlp3(z0_ref[...])[0], w4b_ref[...])
    row = lax.broadcasted_iota(jnp.int32, raw.shape, 0)
    raw = raw + jnp.where((row == 0) & (i == 0), hb0[0:1], 0.0)
    k_ref[...] = jnp.where((row == 0) & (i == half_tiles), 0.0, raw)


def _hyena_kernel_taps(n, p):
    cw = p["f_w4"].shape[1] // 2
    f32 = lambda a: a.astype(F32)
    row = jnp.arange(2 * n, dtype=jnp.int32)
    pos = jnp.where(row < n, row, (2 * n - row) % n).astype(F32)[:, None]
    tt = pos * jnp.float32(1.0 / (n - 1))
    bands = (HY_EMB - 1) // 2
    ang = (2 * math.pi / n) * pos * jnp.linspace(1e-4, bands - 1, bands, dtype=F32)[None]
    zz = jnp.concatenate([tt, jnp.cos(ang), -jnp.sin(ang), jnp.zeros((2 * n, HY_ORDER - HY_EMB), F32)], axis=-1)
    w1 = jnp.concatenate([f32(p["f_w1"]), jnp.zeros((HY_ORDER - HY_EMB, HY_ORDER), F32)], axis=0)
    twice = lambda w: jnp.kron(jnp.eye(2, dtype=F32), f32(w))
    vecs = jnp.stack([jnp.tile(f32(p[k]), 2) for k in ("f_b1", "f_b2", "f_b3", "f_freq")]
                     + [jnp.zeros((2 * HY_ORDER,), F32)] * 4)
    deltas = jnp.abs(jnp.linspace(math.log(HY_TARGET) / HY_SLOW_PCT, math.log(HY_TARGET) / HY_FAST_PCT, cw,
                                  dtype=F32)).reshape(1, cw)
    tr = _pick(n, 512)
    half_tiles = n // tr
    pk = 2 * HY_ORDER
    zp = zz.reshape(-1, 2, tr // 2, HY_ORDER).transpose(0, 2, 1, 3).reshape(-1, pk)
    full = lambda shape: pl.BlockSpec(shape, lambda i: tuple(0 for _ in shape))
    w4 = jnp.concatenate([f32(p["f_w4"])] * 2, axis=0)
    return pl.pallas_call(
        functools.partial(_hyfilt_body, half_tiles=half_tiles),
        grid=(2 * half_tiles,),
        in_specs=[pl.BlockSpec((tr // 2, pk), lambda i: (i, 0)),
                  pl.BlockSpec((tr, 1), lambda i: (i, 0)),
                  full((8, pk)), full((pk, pk)), full((pk, pk)), full((pk, pk)),
                  pl.BlockSpec((pk, cw), lambda i: (0, i // half_tiles)),
                  pl.BlockSpec((pk, cw), lambda i: (0, 1)),
                  full((8, pk)), full((1, cw))],
        out_specs=[pl.BlockSpec((tr, cw), lambda i: (i, 0)), pl.BlockSpec((1, cw), lambda i: (0, 0))],
        out_shape=[jax.ShapeDtypeStruct((2 * n, cw), F32), jax.ShapeDtypeStruct((1, cw), F32)],
        compiler_params=_cp(("arbitrary",)),
        name="hyena_filter",
    )(zp, tt, jnp.broadcast_to(jnp.tile(zz[0:1], (1, 2)), (8, pk)), twice(w1), twice(p["f_w2"]), twice(p["f_w3"]),
      w4, w4, vecs, deltas)


def _dft_consts(n):
    nn = 2 * n
    n2 = HY_N2
    n1 = nn // n2
    n1h = n1 // 2 + 1
    nf = -(-n1h // 8) * 8
    live = (np.arange(nf) < n1h).astype(np.float64)
    a1 = 2 * np.pi * np.outer(np.arange(nf), np.arange(n1)) / n1
    f1 = np.concatenate([np.cos(a1), -np.sin(a1)], axis=0) * np.tile(live, 2)[:, None]
    wgt = live * np.where((np.arange(nf) == 0) | (np.arange(nf) == n1 // 2), 1.0, 2.0)
    f1inv = np.concatenate([np.cos(a1.T) * wgt, -np.sin(a1.T) * wgt], axis=1) / nn
    a2 = 2 * np.pi * np.outer(np.arange(n2), np.arange(n2)) / n2
    c2, s2 = np.cos(a2), -np.sin(a2)
    m2 = np.block([[c2, -s2], [s2, c2]])
    m2inv = np.block([[c2, s2], [-s2, c2]])
    at = 2 * np.pi * np.outer(np.arange(n2), np.arange(nf)) / nn
    tw = np.stack([np.cos(at), -np.sin(at)])
    c = lambda a: jnp.asarray(a.astype(np.float32))
    return dict(n1=n1, n2=n2, nf=nf, f1=c(f1), f1inv=c(f1inv), m2=c(m2), m2inv=c(m2inv),
                tw_s2=c(tw[:, :, :, None]),
                tw_f1=c(np.transpose(tw, (0, 2, 1))[:, :, :, None]))


HY_S2 = 8
HY_CTILE = 512


def _dft1_body(f_ref, x_ref, tw_ref, o_ref, *, nf):
    x = jnp.swapaxes(x_ref[0], 0, 1)
    fh, fl, _ = _split3(f_ref[...])
    d = lambda u, w: jnp.dot(u, w, preferred_element_type=F32)
    re, im = [], []
    for jj in range(HY_S2):
        xh, xl, _ = _split3(x[jj])
        a = d(fh, xh) + (d(fh, xl) + d(fl, xh))
        ar, ai = a[:nf], a[nf:]
        twr, twi = tw_ref[0, jj], tw_ref[1, jj]
        re.append(ar * twr - ai * twi)
        im.append(ar * twi + ai * twr)
    o_ref[0, 0] = jnp.swapaxes(jnp.stack(re), 0, 1)
    o_ref[0, 1] = jnp.swapaxes(jnp.stack(im), 0, 1)


def _dft_stage1(x, consts, rows):
    bsz, _, cw = x.shape
    n2, nf = consts["n2"], consts["nf"]
    xv = x.reshape(bsz, rows, n2, cw)
    f1 = consts["f1"][:, :rows]
    tc = _pick(cw, HY_CTILE, LANE)
    return pl.pallas_call(
        functools.partial(_dft1_body, nf=nf),
        grid=(bsz, n2 // HY_S2, cw // tc),
        in_specs=[pl.BlockSpec((2 * nf, rows), lambda b, j, c: (0, 0)),
                  pl.BlockSpec((1, rows, HY_S2, tc), lambda b, j, c: (b, 0, j, c)),
                  pl.BlockSpec((2, HY_S2, nf, 1), lambda b, j, c: (0, j, 0, 0))],
        out_specs=pl.BlockSpec((1, 2, nf, HY_S2, tc), lambda b, j, c: (b, 0, 0, j, c)),
        out_shape=jax.ShapeDtypeStruct((bsz, 2, nf, n2, cw), F32),
        compiler_params=_cp(("parallel", "parallel", "parallel")),
        name="hyena_dft1",
    )(f1, xv, consts["tw_s2"])


def _dftmid_body(a_ref, m2_ref, m2i_ref, h_ref, tw_ref, is_ref, o_ref, *, n2, filt):
    a = a_ref[0, :, 0].reshape(2 * n2, a_ref.shape[-1])
    x = _mm_x3(m2_ref[...], a)
    xr, xi = x[:n2], x[n2:]
    if filt:
        o_ref[0, 0, 0] = xr * is_ref[...]
        o_ref[0, 1, 0] = xi * is_ref[...]
        return
    hr, hi = h_ref[0, 0, 0], h_ref[0, 1, 0]
    y = jnp.concatenate([xr * hr - xi * hi, xr * hi + xi * hr], axis=0)
    z = _mm_x3(m2i_ref[...], y)
    zr, zi = z[:n2], z[n2:]
    twr, twi = tw_ref[0, 0], tw_ref[1, 0]
    o_ref[0, 0, 0] = zr * twr + zi * twi
    o_ref[0, 1, 0] = zi * twr - zr * twi


def _dft_mid(a, h, inv_s, consts, filt):
    bsz = a.shape[0]
    n1, n2 = consts["nf"], consts["n2"]
    cw = a.shape[-1]
    blk = pl.BlockSpec((1, 2, 1, n2, cw), lambda f, b: (b, 0, f, 0, 0))
    hblk = pl.BlockSpec((1, 2, 1, n2, cw), (lambda f, b: (0, 0, 0, 0, 0)) if filt else (lambda f, b: (0, 0, f, 0, 0)))
    return pl.pallas_call(
        functools.partial(_dftmid_body, n2=n2, filt=filt),
        grid=(n1, bsz),
        in_specs=[blk,
                  pl.BlockSpec((2 * n2, 2 * n2), lambda f, b: (0, 0)),
                  pl.BlockSpec((2 * n2, 2 * n2), lambda f, b: (0, 0)),
                  hblk,
                  pl.BlockSpec((2, 1, n2, 1), lambda f, b: (0, f, 0, 0)),
                  pl.BlockSpec((1, cw), lambda f, b: (0, 0))],
        out_specs=blk,
        out_shape=jax.ShapeDtypeStruct((bsz, 2, n1, n2, cw), F32),
        compiler_params=_cp(("parallel", "parallel")),
        name="hyena_dftmid_" + ("filter" if filt else "conv"),
    )(a, consts["m2"], consts["m2inv"], h, consts["tw_f1"], inv_s)


def _dft3_body(f_ref, z_ref, x0_ref, uv_ref, bias_ref, o_ref):
    zr, zi = jnp.swapaxes(z_ref[0, 0], 0, 1), jnp.swapaxes(z_ref[0, 1], 0, 1)
    x0, uv = jnp.swapaxes(x0_ref[0], 0, 1), jnp.swapaxes(uv_ref[0], 0, 1)
    fh, fl, _ = _split3(f_ref[...])
    d = lambda u, w: jnp.dot(u, w, preferred_element_type=F32)
    out = []
    for jj in range(HY_S2):
        zh, zl, _ = _split3(jnp.concatenate([zr[jj], zi[jj]], axis=0))
        y = d(fh, zh) + (d(fh, zl) + d(fl, zh))
        out.append(x0[jj] * (y + uv[jj] * bias_ref[...]))
    o_ref[0] = jnp.swapaxes(jnp.stack(out), 0, 1)


def _dft_stage1_inv(z, x0, uv, bias, consts):
    bsz, n, cw = x0.shape
    nf, n2 = consts["nf"], consts["n2"]
    hr = consts["n1"] // 2
    tokv = lambda t: t.reshape(bsz, hr, n2, cw)
    tc = _pick(cw, HY_CTILE, LANE)
    tile = pl.BlockSpec((1, hr, HY_S2, tc), lambda b, j, c: (b, 0, j, c))
    out = pl.pallas_call(
        _dft3_body,
        grid=(bsz, n2 // HY_S2, cw // tc),
        in_specs=[pl.BlockSpec((hr, 2 * nf), lambda b, j, c: (0, 0)),
                  pl.BlockSpec((1, 2, nf, HY_S2, tc), lambda b, j, c: (b, 0, 0, j, c)),
                  tile, tile,
                  pl.BlockSpec((1, tc), lambda b, j, c: (0, c))],
        out_specs=tile,
        out_shape=jax.ShapeDtypeStruct((bsz, hr, n2, cw), F32),
        compiler_params=_cp(("parallel", "parallel", "parallel")),
        name="hyena_dft3",
    )(consts["f1inv"][:hr], z, tokv(x0), tokv(uv), bias.reshape(1, cw).astype(F32))
    return out.reshape(bsz, n, cw)


def _hyena(p_lat, p):
    n = p_lat.shape[1]
    consts = _dft_consts(n)
    n1 = consts["n1"]
    x0, uv = _hyena_pre(p_lat, p["conv_w"], p["conv_b"])
    taps, sabs = _hyena_kernel_taps(n, p)
    cw = taps.shape[1]
    ones = jnp.ones((1, cw), F32)
    hk = _dft_stage1(taps[None], consts, n1)
    hspec = _dft_mid(hk, jnp.zeros((1, 2, 1, HY_N2, cw), F32), 1.0 / sabs, consts, True)
    a = _dft_stage1(uv, consts, n1 // 2)
    z = _dft_mid(a, hspec, ones, consts, False)
    return _dft_stage1_inv(z, x0, uv, p["hy_bias"], consts)


def _mod_vectors(c, c_ctx, p):
    bsz, d = c.shape
    cvec = jnp.concatenate([c, c_ctx[None], jnp.zeros((8 - bsz - 1, d), c.dtype)], axis=0).astype(F32)
    m = _mods(cvec, p["w_mod"].astype(F32), p["b_mod"].astype(F32))
    lat = [v[:, None, :] for v in jnp.split(m[:bsz], N_MOD, axis=-1)]
    cx = [jnp.broadcast_to(v[:, None, :], (bsz, 1, d)) for v in jnp.split(m[bsz:bsz + 1], N_MOD, axis=-1)]
    return lat, cx


def _affine(norm_g, shift, scale):
    return norm_g.astype(F32) * (1.0 + scale), shift


def kernel(x, c, ctx, c_ctx,
           l0_w_mod, l0_b_mod, l0_norm1_g, l0_norm2_g, l0_w_in, l0_lam_q1, l0_lam_k1, l0_lam_q2, l0_lam_k2,
           l0_subln_g, l0_mu_prev, l0_mu_next, l0_w0_f, l0_w2_f, l0_a0_f, l0_a2_f, l0_w0_b, l0_w2_b, l0_a0_b,
           l0_a2_b, l0_g2, l0_k_k, l0_k_a, l0_r_k, l0_lnx_g, l0_lnx_b, l0_w_out, l0_mlp_w1, l0_mlp_w2,
           l1_w_mod, l1_b_mod, l1_norm1_g, l1_norm2_g, l1_w_in, l1_conv_w, l1_conv_b, l1_f_w1, l1_f_b1,
           l1_f_w2, l1_f_b2, l1_f_w3, l1_f_b3, l1_f_w4, l1_f_freq, l1_hy_bias, l1_q_norm_g, l1_k_norm_g,
           l1_w_out, l1_mlp_w1, l1_mlp_w2, final_g):
    bsz, n, d = x.shape
    x = x.astype(F32)
    ctx_s = ctx.astype(F32)
    bf = lambda w: w.astype(BF16)

    p0 = dict(w_mod=l0_w_mod, b_mod=l0_b_mod, mu_prev=l0_mu_prev, mu_next=l0_mu_next, w0_f=l0_w0_f, w2_f=l0_w2_f,
              a0_f=l0_a0_f, a2_f=l0_a2_f, w0_b=l0_w0_b, w2_b=l0_w2_b, a0_b=l0_a0_b, a2_b=l0_a2_b, g2=l0_g2,
              k_k=l0_k_k, k_a=l0_k_a, r_k=l0_r_k, lnx_g=l0_lnx_g, lnx_b=l0_lnx_b)
    (sh1, sc1, g1, sh2, sc2, g2), (csh1, csc1, cg1, csh2, csc2, cg2) = _mod_vectors(c, c_ctx, p0)
    aw = d // 2
    w_rw = l0_w_in[:, 3 * aw:]
    rw_cols = -(-(w_rw.shape[1] + 2 * LANE - B_GATE_RANK) // 512) * 512
    w_in0 = bf(jnp.concatenate([w_rw, jnp.zeros((d, rw_cols - w_rw.shape[1]), w_rw.dtype), l0_w_in[:, :3 * aw]],
                               axis=1))
    a1, b1 = _affine(l0_norm1_g, sh1, sc1)
    ca1, cb1 = _affine(l0_norm1_g, csh1, csc1)
    p_lat0 = _inproj(x, a1, b1, w_in0)
    p_ctx0 = _inproj(ctx_s, ca1, cb1, w_in0)
    lam_init = 0.8 - 0.6 * math.exp(-0.3 * 0)
    lam = (jnp.exp(jnp.sum(l0_lam_q1 * l0_lam_k1).astype(F32))
           - jnp.exp(jnp.sum(l0_lam_q2 * l0_lam_k2).astype(F32)) + lam_init)
    oa_lat, oa_ctx = _diff_attention(p_lat0, p_ctx0, rw_cols, aw, lam, lam_init, l0_subln_g)
    nh, hd = l0_r_k.shape
    zero = jnp.zeros((bsz, hd, nh * hd), F32)
    y_ctx, states_c = _rwkv_run(_rwkv_prepare(p_ctx0, p0, rw_cols), (zero, zero), p0)
    y_lat, _ = _rwkv_run(_rwkv_prepare(p_lat0, p0, rw_cols), states_c, p0)
    w_out0 = bf(l0_w_out)
    x = _outproj(oa_lat, y_lat, w_out0, x, g1)
    ctx_s = _outproj(oa_ctx, y_ctx, w_out0, ctx_s, cg1)
    w1, w2 = bf(l0_mlp_w1), bf(l0_mlp_w2)
    a2, b2 = _affine(l0_norm2_g, sh2, sc2)
    ca2, cb2 = _affine(l0_norm2_g, csh2, csc2)
    x = _mlp(x, a2, b2, g2, w1, w2)
    ctx_s = _mlp(ctx_s, ca2, cb2, cg2, w1, w2)

    p1 = dict(w_mod=l1_w_mod, b_mod=l1_b_mod, conv_w=l1_conv_w, conv_b=l1_conv_b, f_w1=l1_f_w1, f_b1=l1_f_b1,
              f_w2=l1_f_w2, f_b2=l1_f_b2, f_w3=l1_f_w3, f_b3=l1_f_b3, f_w4=l1_f_w4, f_freq=l1_f_freq,
              hy_bias=l1_hy_bias)
    (sh1, sc1, g1, sh2, sc2, g2), (csh1, csc1, _, _, _, _) = _mod_vectors(c, c_ctx, p1)
    hy_cols = l1_conv_w.shape[1]
    kv_cols = 2 * (2 * D_HEAD)
    w_in1 = bf(l1_w_in)
    a1, b1 = _affine(l1_norm1_g, sh1, sc1)
    ca1, cb1 = _affine(l1_norm1_g, csh1, csc1)
    p_lat = _inproj(x, a1, b1, w_in1)
    pkv_ctx = _inproj(ctx_s, ca1, cb1, w_in1[:, -kv_cols:])
    o_hy = _hyena(p_lat, p1)
    o_at = _gqa_attention(p_lat, pkv_ctx, hy_cols, l1_q_norm_g, l1_k_norm_g)
    x = _outproj(o_hy, o_at, bf(l1_w_out), x, g1)
    a2, b2 = _affine(l1_norm2_g, sh2, sc2)
    return _mlp(x, a2, b2, g2, bf(l1_mlp_w1), bf(l1_mlp_w2), final_g=final_g)
```

```python
import functools
import math

import numpy as np
import jax
import jax.numpy as jnp
from jax import lax
from jax.experimental import pallas as pl
from jax.experimental.pallas import tpu as pltpu

F32 = jnp.float32
BF16 = jnp.bfloat16
HI = lax.Precision.HIGHEST

NORM_EPS = 1e-6
ROPE_THETA = 10000.0
GRID_W = 64
N_MOD = 6
A_QK_DIM = 64
A_V_DIM = 128
B_HEAD = 64
B_DECAY_RANK = 64
B_ICL_RANK = 64
B_GATE_RANK = 160
B_GN_EPS = 64e-5
HY_EMB = 33
HY_ORDER = 64
HY_TARGET = 1e-2
HY_FAST_PCT = 0.3
HY_SLOW_PCT = 1.5
D_HEAD = 128
D_KV_GROUP = 4

LANE = 128
VMEM_LIMIT = 56 * 1024 * 1024
LOG2E = 1.4426950408889634


def _cp(sem, vmem=VMEM_LIMIT):
    return pltpu.CompilerParams(dimension_semantics=sem, vmem_limit_bytes=vmem)


def _pick(n, pref, step=8):
    t = max(step, min(n, pref) // step * step)
    while n % t:
        t -= step
    return t


def _mods_body(c_ref, w_ref, b_ref, o_ref):
    c = c_ref[...]
    s = c * jax.nn.sigmoid(c)
    o_ref[...] = _mm_x3(s, w_ref[...]) + b_ref[...]


def _mods(cvec, w_mod, b_mod):
    m, d = cvec.shape
    n = w_mod.shape[1]
    tn = _pick(n, 1024, LANE)
    return pl.pallas_call(
        _mods_body,
        grid=(n // tn,),
        in_specs=[pl.BlockSpec((m, d), lambda j: (0, 0)),
                  pl.BlockSpec((d, tn), lambda j: (0, j)),
                  pl.BlockSpec((1, tn), lambda j: (0, j))],
        out_specs=pl.BlockSpec((m, tn), lambda j: (0, j)),
        out_shape=jax.ShapeDtypeStruct((m, n), F32),
        compiler_params=_cp(("arbitrary",)),
        name="mods",
    )(cvec, w_mod, b_mod.reshape(1, n))


def _norm_mod(x, a, b):
    ms = jnp.mean(x * x, axis=-1, keepdims=True)
    return x * lax.rsqrt(ms + NORM_EPS) * a + b


def _inproj_body(x_ref, a_ref, b_ref, w_ref, o_ref, xn_ref):
    @pl.when(pl.program_id(2) == 0)
    def _():
        xn_ref[...] = _norm_mod(x_ref[0], a_ref[0], b_ref[0]).astype(BF16)

    o_ref[0] = jnp.dot(xn_ref[...], w_ref[...], preferred_element_type=F32).astype(o_ref.dtype)


def _inproj(x, a, b, w, out_dtype=BF16, tm_pref=1024, tn_pref=512):
    bsz, n, d = x.shape
    nn = w.shape[1]
    tm, tn = _pick(n, tm_pref), _pick(nn, tn_pref, LANE)
    return pl.pallas_call(
        _inproj_body,
        grid=(bsz, n // tm, nn // tn),
        in_specs=[pl.BlockSpec((1, tm, d), lambda bi, i, j: (bi, i, 0)),
                  pl.BlockSpec((1, 1, d), lambda bi, i, j: (bi, 0, 0)),
                  pl.BlockSpec((1, 1, d), lambda bi, i, j: (bi, 0, 0)),
                  pl.BlockSpec((d, tn), lambda bi, i, j: (0, j))],
        out_specs=pl.BlockSpec((1, tm, tn), lambda bi, i, j: (bi, i, j)),
        out_shape=jax.ShapeDtypeStruct((bsz, n, nn), out_dtype),
        scratch_shapes=[pltpu.VMEM((tm, d), BF16)],
        compiler_params=_cp(("parallel", "parallel", "arbitrary")),
        name="inproj",
    )(x, a, b, w)


def _outproj_body(oa_ref, ob_ref, wa_ref, wb_ref, x_ref, g_ref, y_ref):
    acc = jnp.dot(oa_ref[0].astype(BF16), wa_ref[...], preferred_element_type=F32)
    acc += jnp.dot(ob_ref[0].astype(BF16), wb_ref[...], preferred_element_type=F32)
    y_ref[0] = x_ref[0] + g_ref[0] * acc


def _outproj(oa, ob, w, x, g, tm_pref=1024, tn_pref=512):
    bsz, n, ka = oa.shape
    kb = ob.shape[2]
    d = w.shape[1]
    tm, tn = _pick(n, tm_pref), _pick(d, tn_pref, LANE)
    return pl.pallas_call(
        _outproj_body,
        grid=(bsz, n // tm, d // tn),
        in_specs=[pl.BlockSpec((1, tm, ka), lambda bi, i, j: (bi, i, 0)),
                  pl.BlockSpec((1, tm, kb), lambda bi, i, j: (bi, i, 0)),
                  pl.BlockSpec((ka, tn), lambda bi, i, j: (0, j)),
                  pl.BlockSpec((kb, tn), lambda bi, i, j: (0, j)),
                  pl.BlockSpec((1, tm, tn), lambda bi, i, j: (bi, i, j)),
                  pl.BlockSpec((1, 1, tn), lambda bi, i, j: (bi, 0, j))],
        out_specs=pl.BlockSpec((1, tm, tn), lambda bi, i, j: (bi, i, j)),
        out_shape=jax.ShapeDtypeStruct((bsz, n, d), F32),
        compiler_params=_cp(("parallel", "parallel", "parallel")),
        name="outproj",
    )(oa, ob, w[:ka], w[ka:], x, g)


def _mlp_body(x_ref, a_ref, b_ref, g_ref, w1_ref, w2_ref, fg_ref, y_ref, xn_ref, acc_ref, *, nf, final):
    f = pl.program_id(2)

    @pl.when(f == 0)
    def _():
        xn_ref[...] = _norm_mod(x_ref[0], a_ref[0], b_ref[0]).astype(BF16)
        acc_ref[...] = jnp.zeros_like(acc_ref)

    h = jnp.dot(xn_ref[...], w1_ref[...], preferred_element_type=F32)
    h = jnp.square(jnp.maximum(h, 0.0)).astype(BF16)
    acc_ref[...] += jnp.dot(h, w2_ref[...], preferred_element_type=F32)

    @pl.when(f == nf - 1)
    def _():
        y = x_ref[0] + g_ref[0] * acc_ref[...]
        if final:
            ms = jnp.mean(y * y, axis=-1, keepdims=True)
            y = y * lax.rsqrt(ms + NORM_EPS) * fg_ref[...]
        y_ref[0] = y


def _mlp(x, a, b, g, w1, w2, final_g=None, tm_pref=512, tf_pref=1024):
    bsz, n, d = x.shape
    dff = w1.shape[1]
    tm, tf = _pick(n, tm_pref), _pick(dff, tf_pref, LANE)
    nf = dff // tf
    final = final_g is not None
    fg = (final_g if final else jnp.ones((d,), F32)).reshape(1, d).astype(F32)
    return pl.pallas_call(
        functools.partial(_mlp_body, nf=nf, final=final),
        grid=(bsz, n // tm, nf),
        in_specs=[pl.BlockSpec((1, tm, d), lambda bi, i, f: (bi, i, 0)),
                  pl.BlockSpec((1, 1, d), lambda bi, i, f: (bi, 0, 0)),
                  pl.BlockSpec((1, 1, d), lambda bi, i, f: (bi, 0, 0)),
                  pl.BlockSpec((1, 1, d), lambda bi, i, f: (bi, 0, 0)),
                  pl.BlockSpec((d, tf), lambda bi, i, f: (0, f)),
                  pl.BlockSpec((tf, d), lambda bi, i, f: (f, 0)),
                  pl.BlockSpec((1, d), lambda bi, i, f: (0, 0))],
        out_specs=pl.BlockSpec((1, tm, d), lambda bi, i, f: (bi, i, 0)),
        out_shape=jax.ShapeDtypeStruct((bsz, n, d), F32),
        scratch_shapes=[pltpu.VMEM((tm, d), BF16), pltpu.VMEM((tm, d), F32)],
        compiler_params=_cp(("parallel", "parallel", "arbitrary")),
        name="mlp",
    )(x, a, b, g, w1, w2, fg)


def _rope_tables(n, dim, reps):
    rows = n // GRID_W
    row = jnp.repeat(jnp.arange(rows, dtype=F32), GRID_W)
    col = jnp.tile(jnp.arange(GRID_W, dtype=F32), rows)
    half = dim // 2
    inv = ROPE_THETA ** (-jnp.arange(0, half, 2, dtype=F32) / half)
    ar, ac = row[:, None] * inv, col[:, None] * inv
    cos = jnp.concatenate([jnp.cos(ar), jnp.cos(ar), jnp.cos(ac), jnp.cos(ac)], axis=-1)
    sin = jnp.concatenate([-jnp.sin(ar), jnp.sin(ar), -jnp.sin(ac), jnp.sin(ac)], axis=-1)
    return jnp.tile(cos, (1, reps)), jnp.tile(sin, (1, reps))


def _swap_matrix(dim, reps):
    q = dim // 4
    width = dim * reps
    p = np.zeros((width, width), np.float32)
    for j in range(width):
        base, r = (j // dim) * dim, j % dim
        axis, which, f = r // (2 * q), (r % (2 * q)) // q, r % q
        p[base + axis * 2 * q + (1 - which) * q + f, j] = 1.0
    return p


def _rope_norm(x, cos_ref, sin_ref, p_ref, g_ref, gs_ref, *, norm, rope, scale=1.0):
    y = x.astype(F32)
    if rope:
        ys = jnp.dot(x, p_ref[...], preferred_element_type=F32)
    if norm:
        rs = lax.rsqrt(jnp.mean(y * y, axis=-1, keepdims=True) + NORM_EPS)
        y = y * rs * g_ref[...]
        if rope:
            ys = ys * rs * gs_ref[...]
    if rope:
        y = y * cos_ref[...] + ys * sin_ref[...]
    return y * scale if scale != 1.0 else y


def _rope_operands(n, dim, gain, rope):
    reps = LANE // dim
    if rope:
        cos, sin = _rope_tables(n, dim, reps)
        pm = jnp.asarray(_swap_matrix(dim, reps), BF16)
    else:
        cos = sin = jnp.zeros((n, LANE), F32)
        pm = jnp.zeros((LANE, LANE), BF16)
    if gain is not None:
        g = jnp.tile(gain.astype(F32), reps).reshape(1, LANE)
        gs = jnp.dot(g, jnp.asarray(_swap_matrix(dim, reps)), precision=HI)
    else:
        g = gs = jnp.ones((1, LANE), F32)
    return cos, sin, pm, g, gs


def _kprep_body(x_ref, cos_ref, sin_ref, p_ref, g_ref, gs_ref, o_ref, *, heads, norm, rope):
    for h in range(heads):
        sl = slice(h * LANE, (h + 1) * LANE)
        y = _rope_norm(x_ref[0, :, sl], cos_ref, sin_ref, p_ref, g_ref, gs_ref, norm=norm, rope=rope)
        o_ref[0, :, sl] = y.astype(o_ref.dtype)


def _kprep(p, col0, heads, *, dim, gain=None, rope=True, tq_pref=512):
    bsz, n, _ = p.shape
    tq = _pick(n, tq_pref)
    hb = heads
    while col0 % (hb * LANE):
        hb //= 2
    w = hb * LANE
    cb = col0 // w
    cos, sin, pm, g, gs = _rope_operands(n, dim, gain, rope)
    return pl.pallas_call(
        functools.partial(_kprep_body, heads=hb, norm=gain is not None, rope=rope),
        grid=(bsz, n // tq, heads // hb),
        in_specs=[pl.BlockSpec((1, tq, w), lambda b, i, j: (b, i, cb + j)),
                  pl.BlockSpec((tq, LANE), lambda b, i, j: (i, 0)),
                  pl.BlockSpec((tq, LANE), lambda b, i, j: (i, 0)),
                  pl.BlockSpec((LANE, LANE), lambda b, i, j: (0, 0)),
                  pl.BlockSpec((1, LANE), lambda b, i, j: (0, 0)),
                  pl.BlockSpec((1, LANE), lambda b, i, j: (0, 0))],
        out_specs=pl.BlockSpec((1, tq, w), lambda b, i, j: (b, i, j)),
        out_shape=jax.ShapeDtypeStruct((bsz, n, heads * LANE), BF16),
        compiler_params=_cp(("parallel", "parallel", "parallel")),
        name="kprep",
    )(p, cos, sin, pm, g, gs)


FLASH_KEYS = 1280
FLASH_HEADROOM = 64.0


def _knorm_body(k_ref, o_ref, *, heads):
    for h in range(heads):
        k = k_ref[0, :, h * LANE:(h + 1) * LANE].astype(F32)
        nrm2 = jnp.max(jnp.sum(k * k, axis=1, keepdims=True), axis=0, keepdims=True)
        o_ref[0, h] = jnp.broadcast_to(jnp.sqrt(nrm2), (8, LANE))


def _key_block_norms(k, kcol0, heads, ts):
    bsz, nk, _ = k.shape
    nsub = nk // ts
    hb = heads
    while kcol0 % (hb * LANE):
        hb //= 2
    kb = kcol0 // (hb * LANE)
    out = pl.pallas_call(
        functools.partial(_knorm_body, heads=hb),
        grid=(bsz, heads // hb, nsub),
        in_specs=[pl.BlockSpec((1, ts, hb * LANE), lambda b, j, c: (b, c, kb + j))],
        out_specs=pl.BlockSpec((1, hb, 8, LANE), lambda b, j, c: (b, j, c, 0)),
        out_shape=jax.ShapeDtypeStruct((bsz, heads, nsub * 8, LANE), F32),
        compiler_params=_cp(("parallel", "parallel", "parallel")),
        name="key_block_norms",
    )(k)
    return out[:, :, ::8, 0].reshape(-1)


def _flash_body(kmax_ref, qr_ref, cos_ref, sin_ref, pm_ref, g_ref, gs_ref, k_ref, v_ref, e1_ref, e2_ref, o_ref,
                q_ref, m_ref, l_ref, acc_ref, thr_ref, *, ts, nsub, mode, tq, norm, rope, scale):
    base = (pl.program_id(0) * pl.num_programs(1) + pl.program_id(1)) * nsub
    prep = functools.partial(_rope_norm, cos_ref=cos_ref, sin_ref=sin_ref, p_ref=pm_ref, g_ref=g_ref, gs_ref=gs_ref,
                             norm=norm, rope=rope, scale=scale)
    if mode == "diff":
        y = prep(qr_ref[0])
        lane = lax.broadcasted_iota(jnp.int32, y.shape, 1)
        q_ref[:tq] = jnp.where(lane < A_QK_DIM, y, 0.0).astype(BF16)
        q_ref[tq:] = jnp.where(lane >= A_QK_DIM, y, 0.0).astype(BF16)
    else:
        for j in range(D_KV_GROUP):
            q_ref[j * tq:(j + 1) * tq] = prep(qr_ref[0, :, j * LANE:(j + 1) * LANE]).astype(BF16)
    q = q_ref[...]

    def scores(off, size):
        k = k_ref[0, pl.ds(off, size), :]
        return lax.dot_general(q, k, (((1,), (1,)), ((), ())), preferred_element_type=F32)

    def block(c):
        off = pl.multiple_of(c * ts, ts)
        return scores(off, ts), v_ref[0, pl.ds(off, ts), :]

    def wide(m):
        return jnp.concatenate([m] * (ts // LANE), axis=1)

    m0 = jnp.broadcast_to(jnp.max(scores(0, 2 * LANE), axis=1, keepdims=True), m_ref.shape)
    m_ref[...] = m0
    l_ref[...] = jnp.zeros_like(l_ref)
    acc_ref[...] = jnp.zeros_like(acc_ref)
    qf = q.astype(F32)
    qn = jnp.sqrt(jnp.sum(qf * qf, axis=1, keepdims=True))
    thr_ref[0] = jnp.min((m0[:, :1] + FLASH_HEADROOM) / qn)

    def step(c, carry):
        fixed = kmax_ref[base + c] * 1.001 <= thr_ref[0]

        @pl.when(fixed)
        def _():
            s, v = block(c)
            p = jnp.exp2(s - wide(m_ref[...]))
            l_ref[...] += jnp.sum(p, axis=1, keepdims=True)
            acc_ref[...] += jnp.dot(p.astype(BF16), v, preferred_element_type=F32)

        @pl.when(jnp.logical_not(fixed))
        def _():
            s, v = block(c)
            m_prev = m_ref[...]
            m_new = jnp.maximum(m_prev, jnp.max(s, axis=1, keepdims=True))
            alpha = jnp.exp2(m_prev - m_new)
            p = jnp.exp2(s - wide(m_new))
            l_ref[...] = alpha * l_ref[...] + jnp.sum(p, axis=1, keepdims=True)
            acc_ref[...] = alpha * acc_ref[...] + jnp.dot(p.astype(BF16), v, preferred_element_type=F32)
            m_ref[...] = m_new

        return carry

    lax.fori_loop(0, nsub, step, 0)

    o = acc_ref[...] / l_ref[...]
    if mode == "diff":
        d = o[:tq] - e1_ref[...] * o[tq:]
        ms = jnp.mean(d * d, axis=-1, keepdims=True)
        o_ref[0] = (d * lax.rsqrt(ms + NORM_EPS) * e2_ref[...]).astype(o_ref.dtype)
    else:
        for j in range(D_KV_GROUP):
            o_ref[0, :, j * LANE:(j + 1) * LANE] = o[j * tq:(j + 1) * tq].astype(o_ref.dtype)


def _flash(pq, qcol0, k, v, kcol0, vcol0, e1, e2, *, mode, hk, dim, tq, gain=None, rope=True):
    bsz, n, _ = pq.shape
    nk = k.shape[1]
    g = 2 if mode == "diff" else D_KV_GROUP
    mq = g * tq
    qw = LANE if mode == "diff" else D_KV_GROUP * LANE
    qb = qcol0 // qw
    ts = _pick(nk, FLASH_KEYS, 2 * LANE)
    kb, vb = kcol0 // LANE, vcol0 // LANE
    ow = LANE if mode == "diff" else D_KV_GROUP * LANE
    kmax = _key_block_norms(k, kcol0, hk, ts)
    cos, sin, pm, gq, gqs = _rope_operands(n, dim, gain, rope)
    const = lambda shape: pl.BlockSpec(shape, lambda b, h, i: (0, 0))
    return pl.pallas_call(
        functools.partial(_flash_body, ts=ts, nsub=nk // ts, mode=mode, tq=tq, norm=gain is not None, rope=rope,
                          scale=dim ** -0.5 * LOG2E),
        grid=(bsz, hk, n // tq),
        in_specs=[pl.BlockSpec(memory_space=pltpu.SMEM),
                  pl.BlockSpec((1, tq, qw), lambda b, h, i: (b, i, qb + h)),
                  pl.BlockSpec((tq, LANE), lambda b, h, i: (i, 0)),
                  pl.BlockSpec((tq, LANE), lambda b, h, i: (i, 0)),
                  const((LANE, LANE)), const((1, LANE)), const((1, LANE)),
                  pl.BlockSpec((1, nk, LANE), lambda b, h, i: (b, 0, kb + h)),
                  pl.BlockSpec((1, nk, LANE), lambda b, h, i: (b, 0, vb + h)),
                  const((1, LANE)), const((1, LANE))],
        out_specs=pl.BlockSpec((1, tq, ow), lambda b, h, i: (b, i, h)),
        out_shape=jax.ShapeDtypeStruct((bsz, n, hk * ow), BF16),
        scratch_shapes=[pltpu.VMEM((mq, LANE), BF16), pltpu.VMEM((mq, LANE), F32), pltpu.VMEM((mq, LANE), F32),
                        pltpu.VMEM((mq, LANE), F32), pltpu.SMEM((1,), F32)],
        compiler_params=_cp(("parallel", "parallel", "arbitrary")),
        name="flash_" + mode,
    )(kmax, pq, cos, sin, pm, gq, gqs, k, v, e1, e2)


def _diff_attention(pa_lat, pa_ctx, c0, aw, lam, lam_init, subln_g):
    n, nctx = pa_lat.shape[1], pa_ctx.shape[1]
    heads = aw // LANE
    k_l = _kprep(pa_lat, c0 + aw, heads, dim=A_QK_DIM)
    k_all = jnp.concatenate([k_l, pa_ctx[..., c0 + aw:c0 + 2 * aw]], axis=1)
    v_all = jnp.concatenate([pa_lat[..., c0 + 2 * aw:c0 + 3 * aw], pa_ctx[..., c0 + 2 * aw:c0 + 3 * aw]], axis=1)
    e1 = jnp.full((1, LANE), lam, F32)
    e2 = (subln_g.astype(F32) * (1.0 - lam_init)).reshape(1, LANE)
    o_lat = _flash(pa_lat, c0, k_all, v_all, 0, 0, e1, e2, mode="diff", hk=heads, dim=A_QK_DIM, tq=_pick(n, 1024))
    o_ctx = _flash(pa_ctx, c0, pa_ctx, pa_ctx, c0 + aw, c0 + 2 * aw, e1, e2, mode="diff", hk=heads, dim=A_QK_DIM,
                   tq=_pick(nctx, 512), rope=False)
    return o_lat, o_ctx


def _gqa_attention(p_lat, pkv_ctx, qcol0, q_norm_g, k_norm_g):
    n = p_lat.shape[1]
    hq, hk = 2 * D_KV_GROUP, 2
    kcol0 = qcol0 + hq * D_HEAD
    vcol0 = kcol0 + hk * D_HEAD
    k_l = _kprep(p_lat, kcol0, hk, dim=D_HEAD, gain=k_norm_g)
    k_c = _kprep(pkv_ctx, 0, hk, dim=D_HEAD, gain=k_norm_g, rope=False)
    k_all = jnp.concatenate([k_l, k_c], axis=1)
    v_all = jnp.concatenate([p_lat[..., vcol0:vcol0 + hk * D_HEAD], pkv_ctx[..., hk * D_HEAD:]], axis=1)
    dummy = jnp.zeros((1, LANE), F32)
    return _flash(p_lat, qcol0, k_all, v_all, 0, 0, dummy, dummy, mode="gqa", hk=hk, dim=D_HEAD,
                  tq=_pick(n, 512), gain=q_norm_g)


RW_CHUNK = 64
RW_HALO = 8


def _mm_hi(a, b):
    return jnp.dot(a, b, preferred_element_type=F32, precision=HI)


def _mm_bf(a, b):
    return jnp.dot(a.astype(BF16), b.astype(BF16), preferred_element_type=F32)


def _split3(x):
    x1 = x.astype(BF16)
    r1 = x - x1.astype(F32)
    x2 = r1.astype(BF16)
    return x1, x2, (r1 - x2.astype(F32)).astype(BF16)


def _mm_x3(a, b):
    ah, al, _ = _split3(a)
    bh, bl, _ = _split3(b)
    d = lambda u, w: jnp.dot(u, w, preferred_element_type=F32)
    return d(ah, bh) + (d(ah, bl) + d(al, bh))


def _mm_sel(z, sel):
    z1, z2, z3 = _split3(z)
    d = lambda u: jnp.dot(u, sel, preferred_element_type=F32)
    return d(z1) + (d(z2) + d(z3))


def _rwprep_body(x_ref, hp_ref, hn_ref, mup_ref, mun_ref, vec_ref, w2f_ref, w2b_ref, a2f_ref, a2b_ref,
                 g2_ref, sel_ref, selt_ref,
                 r_ref, v_ref, kk_ref, lwf_ref, bf_ref, kf_ref, lwb_ref, bb_ref, kb_ref, g_ref, bonus_ref,
                 *, nb, tm, bw):
    i = pl.program_id(1)
    x = x_ref[0].astype(F32)
    row = lax.broadcasted_iota(jnp.int32, x.shape, 0)
    prev_edge = jnp.where(i > 0, hp_ref[0, RW_HALO - 1:RW_HALO, :].astype(F32), 0.0)
    next_edge = jnp.where(i < nb - 1, hn_ref[0, 0:1, :].astype(F32), 0.0)
    prev = jnp.where(row == 0, prev_edge, pltpu.roll(x, 1, 0))
    nxt = jnp.where(row == tm - 1, next_edge, pltpu.roll(x, tm - 1, 0))
    xs = x + mup_ref[...] * (prev - x) + mun_ref[...] * (nxt - x)
    r, k, v = xs[:, :bw], xs[:, bw:2 * bw], xs[:, 2 * bw:3 * bw]
    lr = xs[:, 3 * bw:3 * bw + LANE]
    gd = xs[:, 3 * bw + LANE:3 * bw + 3 * LANE]
    k_k, k_a, r_k = vec_ref[0:1], vec_ref[1:2], vec_ref[2:3]
    w0f, a0f, w0b, a0b = vec_ref[3:4], vec_ref[4:5], vec_ref[5:6], vec_ref[6:7]
    sel, selt = sel_ref[...], selt_ref[...]

    def head_sum(z):
        return _mm_sel(_mm_sel(z, sel), selt)

    kk = k * k_k
    kk = kk * lax.rsqrt(head_sum(kk * kk) + 1e-12)
    g_ref[0] = _mm_bf(jax.nn.sigmoid(gd), g2_ref[...]).astype(g_ref.dtype)
    th = jnp.tanh(lr)
    ksum = jnp.zeros_like(k)
    for (w0, a0, w2_ref, a2_ref, lw_ref, b_ref, kd_ref) in (
            (w0f, a0f, w2f_ref, a2f_ref, lwf_ref, bf_ref, kf_ref),
            (w0b, a0b, w2b_ref, a2b_ref, lwb_ref, bb_ref, kb_ref)):
        logw = -math.exp(-0.5) * jax.nn.sigmoid(w0 + _mm_bf(th, w2_ref[...]))
        a = jax.nn.sigmoid(a0 + _mm_bf(lr, a2_ref[...]))
        k_d = k * (1.0 + (a - 1.0) * k_a)
        ksum = ksum + k_d
        lw_ref[0] = logw
        b_ref[0] = kk * a
        kd_ref[0] = k_d
    r_ref[0] = r
    v_ref[0] = v
    kk_ref[0] = kk
    bonus_ref[0] = head_sum(r * ksum * r_k) * v


def _rwkv_prepare(pr, p, cols):
    bsz, n, _ = pr.shape
    bw = p["w0_f"].shape[0]
    tm = _pick(n, 256)
    nb = n // tm
    hb = tm // RW_HALO
    assert cols - 3 * bw >= 3 * LANE and B_DECAY_RANK + B_ICL_RANK == LANE

    def padded(vec):
        return jnp.concatenate([vec.astype(F32), jnp.zeros((cols - vec.shape[0],), F32)]).reshape(1, cols)

    vecs = jnp.stack([p["k_k"], p["k_a"], p["r_k"].reshape(-1), p["w0_f"], p["a0_f"], p["w0_b"], p["a0_b"],
                      jnp.zeros((bw,), F32)]).astype(F32)
    zr = jnp.zeros((B_DECAY_RANK, bw), F32)
    w2 = {d: jnp.concatenate([p["w2_" + d].astype(F32), zr], axis=0).astype(BF16) for d in "fb"}
    a2 = {d: jnp.concatenate([zr, p["a2_" + d].astype(F32)], axis=0).astype(BF16) for d in "fb"}
    g2 = jnp.concatenate([p["g2"].astype(F32), jnp.zeros((2 * LANE - B_GATE_RANK, bw), F32)], axis=0).astype(BF16)
    sel, selt = _head_selectors(bw)

    full = lambda shape: pl.BlockSpec(shape, lambda b, i: tuple(0 for _ in shape))
    tok = pl.BlockSpec((1, tm, bw), lambda b, i: (b, i, 0))
    tok_f32 = jax.ShapeDtypeStruct((bsz, n, bw), F32)
    outs = pl.pallas_call(
        functools.partial(_rwprep_body, nb=nb, tm=tm, bw=bw),
        grid=(bsz, nb),
        in_specs=[pl.BlockSpec((1, tm, cols), lambda b, i: (b, i, 0)),
                  pl.BlockSpec((1, RW_HALO, cols), lambda b, i: (b, jnp.maximum(i * hb - 1, 0), 0)),
                  pl.BlockSpec((1, RW_HALO, cols), lambda b, i: (b, jnp.minimum((i + 1) * hb, nb * hb - 1), 0)),
                  full((1, cols)), full((1, cols)), full((8, bw)),
                  full((LANE, bw)), full((LANE, bw)), full((LANE, bw)), full((LANE, bw)),
                  full((2 * LANE, bw)), full((bw, LANE)), full((LANE, bw))],
        out_specs=[tok] * 11,
        out_shape=[tok_f32] * 9 + [jax.ShapeDtypeStruct((bsz, n, bw), BF16), tok_f32],
        compiler_params=_cp(("parallel", "parallel")),
        name="rwkv_prepare",
    )(pr, pr, pr, padded(p["mu_prev"]), padded(p["mu_next"]), vecs, w2["f"], w2["b"], a2["f"], a2["b"],
      g2, sel, selt)
    r, v, kk, lwf, bf, kf, lwb, bb, kb, g, bonus = outs
    return dict(r=r, v=v, kk=kk, g=g, bonus=bonus, f=(lwf, bf, kf), b=(lwb, bb, kb))


RW_GROUP = 4


def _bd(x):
    t = jnp.concatenate([x.astype(BF16)] * RW_GROUP, axis=1)
    r = lax.broadcasted_iota(jnp.int32, t.shape, 1) // B_HEAD
    c = lax.broadcasted_iota(jnp.int32, t.shape, 2) // B_HEAD
    return jnp.where(r == c, t, jnp.zeros_like(t))


def _bdot(a, b, dims):
    return jnp.einsum(dims, a.astype(BF16), b.astype(BF16), preferred_element_type=F32)


def _groups(x):
    gw = RW_GROUP * B_HEAD
    return jnp.stack([x[:, g * gw:(g + 1) * gw] for g in range(x.shape[1] // gw)])


def _rw_decays(r, kk, b, kd, lw, reverse):
    ck = r.shape[0]
    ti = lax.broadcasted_iota(jnp.int32, (ck, ck), 0)
    si = lax.broadcasted_iota(jnp.int32, (ck, ck), 1)
    tri = ((si >= ti) if reverse else (si <= ti)).astype(BF16)
    l1, l2, l3 = _split3(lw)
    tdot = lambda z: jnp.dot(tri, z, preferred_element_type=F32)
    cum = tdot(l1) + (tdot(l2) + tdot(l3))
    total = cum[0:1] if reverse else cum[ck - 1:ck]
    winv = jnp.exp(-cum)
    wrest = jnp.exp(total - cum)
    return (-kk * jnp.exp(cum - lw), b * winv, kd * winv, r * jnp.exp(cum), b * wrest, kd * wrest, jnp.exp(total))


def _rw_chunks(fwd, bwd, s_f, s_b):
    nseq = len(fwd)
    ck, w = fwd[0][0].shape
    gw = RW_GROUP * B_HEAD
    ng = w // gw
    parts = []
    for probs, rev in ((fwd, False), (bwd, True)):
        for (r, v, kk, b, kd, lw) in probs:
            parts.append([_groups(z) for z in _rw_decays(r, kk, b, kd, lw, rev)] + [_groups(v)])
    at, bt, kt, rt, btw, ktw, decay, v = [jnp.concatenate(zs, axis=0) for zs in zip(*parts)]
    s = jnp.concatenate([_groups(z) for z in list(s_f) + list(s_b)], axis=0)
    nslab = 2 * nseq * ng
    shape = (nslab, ck, gw)
    lag = lax.broadcasted_iota(jnp.int32, shape, 1) - lax.broadcasted_iota(jnp.int32, shape, 2) % ck
    lag = jnp.where(lax.broadcasted_iota(jnp.int32, shape, 0) >= nseq * ng, -lag, lag)
    incl, strict = lag >= 0, lag > 0
    eye = (lag == 0).astype(F32)

    bdv = _bd(v)
    lhs = jnp.concatenate([at, rt], axis=1)
    lb, lk = _bdot(lhs, _bd(bt), "gtk,gnk->gtn"), _bdot(lhs, _bd(kt), "gtk,gnk->gtn")
    lab, mrb = jnp.where(strict, lb[:, :ck], 0.0), jnp.where(incl, lb[:, ck:], 0.0)
    lak, mrk = jnp.where(strict, lk[:, :ck], 0.0), jnp.where(incl, lk[:, ck:], 0.0)
    tinv = eye + lab
    pw = _bdot(lab, _bd(lab), "gtk,gkn->gtn")
    span = 4
    while span < ck:
        both = _bdot(jnp.concatenate([tinv, pw], axis=1), _bd(pw), "gtk,gkn->gtn")
        tinv, pw = tinv + both[:, :ck], both[:, ck:]
        span *= 2
    tinv = tinv + _bdot(tinv, _bd(pw), "gtk,gkn->gtn")
    vmix = _bdot(jnp.concatenate([lak, mrk], axis=1), bdv, "gtk,gkn->gtn")
    p1 = _bdot(tinv, _bd(at), "gtk,gkn->gtn")
    q1 = _bdot(tinv, _bd(vmix[:, :ck]), "gtk,gkn->gtn")
    smix = _bdot(jnp.concatenate([p1, rt], axis=1), _bd(s), "gtk,gnk->gtn")
    u = smix[:, :ck] + q1
    y = smix[:, ck:] + _bdot(mrb, _bd(u), "gtk,gkn->gtn") + vmix[:, ck:]
    z = _bdot(jnp.concatenate([u, v], axis=1), jnp.concatenate([btw, ktw], axis=1), "gtm,gtn->gmn")
    lane_head = lax.broadcasted_iota(jnp.int32, (nslab, B_HEAD, gw), 2) // B_HEAD
    upd = jnp.zeros_like(s)
    for h in range(RW_GROUP):
        upd = upd + jnp.where(lane_head == h, z[:, h * B_HEAD:(h + 1) * B_HEAD], 0.0)
    s_new = s * decay + upd
    lanes = lambda x, q: jnp.concatenate([x[q * ng + g] for g in range(ng)], axis=1)
    seqs = range(nseq)
    return ([lanes(y, q) for q in seqs], [lanes(y, nseq + q) for q in seqs],
            [lanes(s_new, q) for q in seqs], [lanes(s_new, nseq + q) for q in seqs])


def _rwscan_body(rf_ref, vf_ref, kkf_ref, bf_ref, kdf_ref, lwf_ref, rb_ref, vb_ref, kkb_ref, bb_ref, kdb_ref, lwb_ref,
                 s0f_ref, s0b_ref, yf_ref, yb_ref, sTf_ref, sTb_ref, sf_ref, sb_ref, *, nc, nseq):
    c = pl.program_id(0)

    @pl.when(c == 0)
    def _():
        sf_ref[...] = s0f_ref[...]
        sb_ref[...] = s0b_ref[...]

    seqs = range(nseq)
    yf, yb, sf, sb = _rw_chunks(
        [(rf_ref[q], vf_ref[q], kkf_ref[q], bf_ref[q], kdf_ref[q], lwf_ref[q]) for q in seqs],
        [(rb_ref[q], vb_ref[q], kkb_ref[q], bb_ref[q], kdb_ref[q], lwb_ref[q]) for q in seqs],
        [sf_ref[q] for q in seqs], [sb_ref[q] for q in seqs])
    for q in seqs:
        yf_ref[q] = yf[q]
        yb_ref[q] = yb[q]
        sf_ref[q] = sf[q]
        sb_ref[q] = sb[q]

    @pl.when(c == nc - 1)
    def _():
        for q in seqs:
            sTf_ref[q] = sf[q]
            sTb_ref[q] = sb[q]


def _rwkv_scan(r, v, kk, fwd, bwd, s0f, s0b):
    bsz, n, w = r.shape
    ck = RW_CHUNK
    assert n % ck == 0 and ck == B_HEAD and w % (RW_GROUP * B_HEAD) == 0
    nc = n // ck
    seq_f = pl.BlockSpec((bsz, ck, w), lambda c: (0, c, 0))
    seq_b = pl.BlockSpec((bsz, ck, w), lambda c: (0, nc - 1 - c, 0))
    st = pl.BlockSpec((bsz, B_HEAD, w), lambda c: (0, 0, 0))
    y_shape = jax.ShapeDtypeStruct((bsz, n, w), F32)
    s_shape = jax.ShapeDtypeStruct((bsz, B_HEAD, w), F32)
    (lwf, bf, kf), (lwb, bb, kb) = fwd, bwd
    return pl.pallas_call(
        functools.partial(_rwscan_body, nc=nc, nseq=bsz),
        grid=(nc,),
        in_specs=[seq_f] * 6 + [seq_b] * 6 + [st, st],
        out_specs=[seq_f, seq_b, st, st],
        out_shape=[y_shape, y_shape, s_shape, s_shape],
        scratch_shapes=[pltpu.VMEM((bsz, B_HEAD, w), F32), pltpu.VMEM((bsz, B_HEAD, w), F32)],
        compiler_params=_cp(("arbitrary",)),
        name="rwkv_scan",
    )(r, v, kk, bf, kf, lwf, r, v, kk, bb, kb, lwb, s0f, s0b)


def _rwpost_body(yf_ref, yb_ref, g_ref, bonus_ref, lg_ref, lb_ref, sel_ref, selt_ref, o_ref):
    sel, selt = sel_ref[...], selt_ref[...]
    head_mean = lambda z: _mm_sel(_mm_sel(z, sel), selt) * (1.0 / B_HEAD)
    y = yf_ref[0] + yb_ref[0]
    d = y - head_mean(y)
    y = d * lax.rsqrt(head_mean(d * d) + B_GN_EPS)
    y = y * lg_ref[...] + lb_ref[...] + bonus_ref[0]
    o_ref[0] = (y * g_ref[0].astype(F32)).astype(o_ref.dtype)


def _head_selectors(bw):
    sel_np = np.zeros((bw, LANE), np.float32)
    sel_np[np.arange(bw), np.arange(bw) // B_HEAD] = 1.0
    return jnp.asarray(sel_np, BF16), jnp.asarray(sel_np.T, BF16)


def _rwkv_post(y_f, y_b, g, bonus, lnx_g, lnx_b):
    bsz, n, bw = y_f.shape
    tm = _pick(n, 256)
    tok = pl.BlockSpec((1, tm, bw), lambda b, i: (b, i, 0))
    vec = pl.BlockSpec((1, bw), lambda b, i: (0, 0))
    sel, selt = _head_selectors(bw)
    return pl.pallas_call(
        _rwpost_body,
        grid=(bsz, n // tm),
        in_specs=[tok, tok, tok, tok, vec, vec,
                  pl.BlockSpec((bw, LANE), lambda b, i: (0, 0)), pl.BlockSpec((LANE, bw), lambda b, i: (0, 0))],
        out_specs=tok,
        out_shape=jax.ShapeDtypeStruct((bsz, n, bw), BF16),
        compiler_params=_cp(("parallel", "parallel")),
        name="rwkv_post",
    )(y_f, y_b, g, bonus, lnx_g.reshape(1, bw).astype(F32), lnx_b.reshape(1, bw).astype(F32), sel, selt)


def _rwkv_run(prep, states0, p):
    y_f, y_b, s_f, s_b = _rwkv_scan(prep["r"], prep["v"], prep["kk"], prep["f"], prep["b"], states0[0], states0[1])
    return _rwkv_post(y_f, y_b, prep["g"], prep["bonus"], p["lnx_g"], p["lnx_b"]), (s_f, s_b)


HY_N2 = 128
HY_HALO = 8


def _hypre_body(x_ref, hp_ref, hn_ref, cw_ref, cb_ref, x0_ref, uv_ref, *, nb, tm, cw):
    i = pl.program_id(1)
    x = x_ref[0].astype(F32)
    row = lax.broadcasted_iota(jnp.int32, x.shape, 0)
    prev_edge = jnp.where(i > 0, hp_ref[0, HY_HALO - 1:HY_HALO, :].astype(F32), 0.0)
    next_edge = jnp.where(i < nb - 1, hn_ref[0, 0:1, :].astype(F32), 0.0)
    prev = jnp.where(row == 0, prev_edge, pltpu.roll(x, 1, 0))
    nxt = jnp.where(row == tm - 1, next_edge, pltpu.roll(x, tm - 1, 0))
    u = prev * cw_ref[0:1] + x * cw_ref[1:2] + nxt * cw_ref[2:3] + cb_ref[...]
    x0_ref[0] = u[:, :cw]
    uv_ref[0] = u[:, 2 * cw:3 * cw] * u[:, cw:2 * cw]


def _hyena_pre(p, conv_w, conv_b):
    bsz, n, _ = p.shape
    hc = conv_w.shape[1]
    cw = hc // 3
    tm = _pick(n, 256)
    nb = n // tm
    hb = tm // HY_HALO
    cwp = jnp.concatenate([conv_w.astype(F32), jnp.zeros((5, hc), F32)], axis=0)
    out = pl.BlockSpec((1, tm, cw), lambda b, i: (b, i, 0))
    return pl.pallas_call(
        functools.partial(_hypre_body, nb=nb, tm=tm, cw=cw),
        grid=(bsz, nb),
        in_specs=[pl.BlockSpec((1, tm, hc), lambda b, i: (b, i, 0)),
                  pl.BlockSpec((1, HY_HALO, hc), lambda b, i: (b, jnp.maximum(i * hb - 1, 0), 0)),
                  pl.BlockSpec((1, HY_HALO, hc), lambda b, i: (b, jnp.minimum((i + 1) * hb, nb * hb - 1), 0)),
                  pl.BlockSpec((8, hc), lambda b, i: (0, 0)),
                  pl.BlockSpec((1, hc), lambda b, i: (0, 0))],
        out_specs=[out, out],
        out_shape=[jax.ShapeDtypeStruct((bsz, n, cw), F32)] * 2,
        compiler_params=_cp(("parallel", "parallel")),
        name="hyena_pre",
    )(p, p, p, cwp, conv_b.reshape(1, hc).astype(F32))


def _hyfilt_body(z_ref, t_ref, z0_ref, w1_ref, w2_ref, w3_ref, w4_ref, w4b_ref, vec_ref, dl_ref,
                 k_ref, s_ref, *, half_tiles):
    i = pl.program_id(0)
    b1, b2, b3, fr = vec_ref[0:1], vec_ref[1:2], vec_ref[2:3], vec_ref[3:4]

    def mlp3(z):
        h = jnp.sin(fr * (_mm_hi(z, w1_ref[...]) + b1))
        h = jnp.sin(fr * (_mm_hi(h, w2_ref[...]) + b2))
        h = jnp.sin(fr * (_mm_hi(h, w3_ref[...]) + b3))
        lane = lax.broadcasted_iota(jnp.int32, h.shape, 1)
        return jnp.where(lane < HY_ORDER, h, 0.0), jnp.where(lane >= HY_ORDER, h, 0.0)

    h_lo, h_hi = mlp3(z_ref[...])
    raw = jnp.concatenate([_mm_hi(h_lo, w4_ref[...]), _mm_hi(h_hi, w4_ref[...])], axis=0)
    raw = raw * jnp.exp(-t_ref[...] * dl_ref[...])

    @pl.when(i == 0)
    def _():
        s_ref[...] = jnp.zeros_like(s_ref)

    s_ref[...] += jnp.sum(jnp.abs(raw), axis=0, keepdims=True)
    hb0 = _mm_hi(mlp3(z0_ref[...])[0], w4b_ref[...])
    row = lax.broadcasted_iota(jnp.int32, raw.shape, 0)
    raw = raw + jnp.where((row == 0) & (i == 0), hb0[0:1], 0.0)
    k_ref[...] = jnp.where((row == 0) & (i == half_tiles), 0.0, raw)


def _hyena_kernel_taps(n, p):
    cw = p["f_w4"].shape[1] // 2
    f32 = lambda a: a.astype(F32)
    row = jnp.arange(2 * n, dtype=jnp.int32)
    pos = jnp.where(row < n, row, (2 * n - row) % n).astype(F32)[:, None]
    tt = pos * jnp.float32(1.0 / (n - 1))
    bands = (HY_EMB - 1) // 2
    ang = (2 * math.pi / n) * pos * jnp.linspace(1e-4, bands - 1, bands, dtype=F32)[None]
    zz = jnp.concatenate([tt, jnp.cos(ang), -jnp.sin(ang), jnp.zeros((2 * n, HY_ORDER - HY_EMB), F32)], axis=-1)
    w1 = jnp.concatenate([f32(p["f_w1"]), jnp.zeros((HY_ORDER - HY_EMB, HY_ORDER), F32)], axis=0)
    twice = lambda w: jnp.kron(jnp.eye(2, dtype=F32), f32(w))
    vecs = jnp.stack([jnp.tile(f32(p[k]), 2) for k in ("f_b1", "f_b2", "f_b3", "f_freq")]
                     + [jnp.zeros((2 * HY_ORDER,), F32)] * 4)
    deltas = jnp.abs(jnp.linspace(math.log(HY_TARGET) / HY_SLOW_PCT, math.log(HY_TARGET) / HY_FAST_PCT, cw,
                                  dtype=F32)).reshape(1, cw)
    tr = _pick(n, 512)
    half_tiles = n // tr
    pk = 2 * HY_ORDER
    zp = zz.reshape(-1, 2, tr // 2, HY_ORDER).transpose(0, 2, 1, 3).reshape(-1, pk)
    full = lambda shape: pl.BlockSpec(shape, lambda i: tuple(0 for _ in shape))
    w4 = jnp.concatenate([f32(p["f_w4"])] * 2, axis=0)
    return pl.pallas_call(
        functools.partial(_hyfilt_body, half_tiles=half_tiles),
        grid=(2 * half_tiles,),
        in_specs=[pl.BlockSpec((tr // 2, pk), lambda i: (i, 0)),
                  pl.BlockSpec((tr, 1), lambda i: (i, 0)),
                  full((8, pk)), full((pk, pk)), full((pk, pk)), full((pk, pk)),
                  pl.BlockSpec((pk, cw), lambda i: (0, i // half_tiles)),
                  pl.BlockSpec((pk, cw), lambda i: (0, 1)),
                  full((8, pk)), full((1, cw))],
        out_specs=[pl.BlockSpec((tr, cw), lambda i: (i, 0)), pl.BlockSpec((1, cw), lambda i: (0, 0))],
        out_shape=[jax.ShapeDtypeStruct((2 * n, cw), F32), jax.ShapeDtypeStruct((1, cw), F32)],
        compiler_params=_cp(("arbitrary",)),
        name="hyena_filter",
    )(zp, tt, jnp.broadcast_to(jnp.tile(zz[0:1], (1, 2)), (8, pk)), twice(w1), twice(p["f_w2"]), twice(p["f_w3"]),
      w4, w4, vecs, deltas)


def _dft_consts(n):
    nn = 2 * n
    n2 = HY_N2
    n1 = nn // n2
    n1h = n1 // 2 + 1
    nf = -(-n1h // 8) * 8
    live = (np.arange(nf) < n1h).astype(np.float64)
    a1 = 2 * np.pi * np.outer(np.arange(nf), np.arange(n1)) / n1
    f1 = np.concatenate([np.cos(a1), -np.sin(a1)], axis=0) * np.tile(live, 2)[:, None]
    wgt = live * np.where((np.arange(nf) == 0) | (np.arange(nf) == n1 // 2), 1.0, 2.0)
    f1inv = np.concatenate([np.cos(a1.T) * wgt, -np.sin(a1.T) * wgt], axis=1) / nn
    a2 = 2 * np.pi * np.outer(np.arange(n2), np.arange(n2)) / n2
    c2, s2 = np.cos(a2), -np.sin(a2)
    m2 = np.block([[c2, -s2], [s2, c2]])
    m2inv = np.block([[c2, s2], [-s2, c2]])
    at = 2 * np.pi * np.outer(np.arange(n2), np.arange(nf)) / nn
    tw = np.stack([np.cos(at), -np.sin(at)])
    c = lambda a: jnp.asarray(a.astype(np.float32))
    return dict(n1=n1, n2=n2, nf=nf, f1=c(f1), f1inv=c(f1inv), m2=c(m2), m2inv=c(m2inv),
                tw_s2=c(tw[:, :, :, None]),
                tw_f1=c(np.transpose(tw, (0, 2, 1))[:, :, :, None]))


HY_S2 = 8
HY_CTILE = 512


def _dft1_body(f_ref, x_ref, tw_ref, o_ref, *, nf):
    x = jnp.swapaxes(x_ref[0], 0, 1)
    fh, fl, _ = _split3(f_ref[...])
    d = lambda u, w: jnp.dot(u, w, preferred_element_type=F32)
    re, im = [], []
    for jj in range(HY_S2):
        xh, xl, _ = _split3(x[jj])
        a = d(fh, xh) + (d(fh, xl) + d(fl, xh))
        ar, ai = a[:nf], a[nf:]
        twr, twi = tw_ref[0, jj], tw_ref[1, jj]
        re.append(ar * twr - ai * twi)
        im.append(ar * twi + ai * twr)
    o_ref[0, 0] = jnp.swapaxes(jnp.stack(re), 0, 1)
    o_ref[0, 1] = jnp.swapaxes(jnp.stack(im), 0, 1)


def _dft_stage1(x, consts, rows):
    bsz, _, cw = x.shape
    n2, nf = consts["n2"], consts["nf"]
    xv = x.reshape(bsz, rows, n2, cw)
    f1 = consts["f1"][:, :rows]
    tc = _pick(cw, HY_CTILE, LANE)
    return pl.pallas_call(
        functools.partial(_dft1_body, nf=nf),
        grid=(bsz, n2 // HY_S2, cw // tc),
        in_specs=[pl.BlockSpec((2 * nf, rows), lambda b, j, c: (0, 0)),
                  pl.BlockSpec((1, rows, HY_S2, tc), lambda b, j, c: (b, 0, j, c)),
                  pl.BlockSpec((2, HY_S2, nf, 1), lambda b, j, c: (0, j, 0, 0))],
        out_specs=pl.BlockSpec((1, 2, nf, HY_S2, tc), lambda b, j, c: (b, 0, 0, j, c)),
        out_shape=jax.ShapeDtypeStruct((bsz, 2, nf, n2, cw), F32),
        compiler_params=_cp(("parallel", "parallel", "parallel")),
        name="hyena_dft1",
    )(f1, xv, consts["tw_s2"])


HY_F1 = 2


def _dftmid_body(a_ref, m2_ref, m2i_ref, h_ref, tw_ref, is_ref, o_ref, *, n2, filt):
    for r in range(HY_F1):
        a = a_ref[0, :, r].reshape(2 * n2, a_ref.shape[-1])
        x = _mm_x3(m2_ref[...], a)
        xr, xi = x[:n2], x[n2:]
        if filt:
            o_ref[0, 0, r] = xr * is_ref[...]
            o_ref[0, 1, r] = xi * is_ref[...]
            continue
        hr, hi = h_ref[0, 0, r], h_ref[0, 1, r]
        y = jnp.concatenate([xr * hr - xi * hi, xr * hi + xi * hr], axis=0)
        z = _mm_x3(m2i_ref[...], y)
        zr, zi = z[:n2], z[n2:]
        twr, twi = tw_ref[0, r], tw_ref[1, r]
        o_ref[0, 0, r] = zr * twr + zi * twi
        o_ref[0, 1, r] = zi * twr - zr * twi


def _dft_mid(a, h, inv_s, consts, filt):
    bsz = a.shape[0]
    n1, n2 = consts["nf"], consts["n2"]
    cw = a.shape[-1]
    blk = pl.BlockSpec((1, 2, HY_F1, n2, cw), lambda f, b: (b, 0, f, 0, 0))
    hblk = pl.BlockSpec((1, 2, HY_F1, n2, cw),
                        (lambda f, b: (0, 0, 0, 0, 0)) if filt else (lambda f, b: (0, 0, f, 0, 0)))
    return pl.pallas_call(
        functools.partial(_dftmid_body, n2=n2, filt=filt),
        grid=(n1 // HY_F1, bsz),
        in_specs=[blk,
                  pl.BlockSpec((2 * n2, 2 * n2), lambda f, b: (0, 0)),
                  pl.BlockSpec((2 * n2, 2 * n2), lambda f, b: (0, 0)),
                  hblk,
                  pl.BlockSpec((2, HY_F1, n2, 1), lambda f, b: (0, f, 0, 0)),
                  pl.BlockSpec((1, cw), lambda f, b: (0, 0))],
        out_specs=blk,
        out_shape=jax.ShapeDtypeStruct((bsz, 2, n1, n2, cw), F32),
        compiler_params=_cp(("parallel", "parallel")),
        name="hyena_dftmid_" + ("filter" if filt else "conv"),
    )(a, consts["m2"], consts["m2inv"], h, consts["tw_f1"], inv_s)


def _dft3_body(f_ref, z_ref, x0_ref, uv_ref, bias_ref, o_ref):
    zr, zi = jnp.swapaxes(z_ref[0, 0], 0, 1), jnp.swapaxes(z_ref[0, 1], 0, 1)
    x0, uv = jnp.swapaxes(x0_ref[0], 0, 1), jnp.swapaxes(uv_ref[0], 0, 1)
    fh, fl, _ = _split3(f_ref[...])
    d = lambda u, w: jnp.dot(u, w, preferred_element_type=F32)
    out = []
    for jj in range(HY_S2):
        zh, zl, _ = _split3(jnp.concatenate([zr[jj], zi[jj]], axis=0))
        y = d(fh, zh) + (d(fh, zl) + d(fl, zh))
        out.append(x0[jj] * (y + uv[jj] * bias_ref[...]))
    o_ref[0] = jnp.swapaxes(jnp.stack(out), 0, 1)


def _dft_stage1_inv(z, x0, uv, bias, consts):
    bsz, n, cw = x0.shape
    nf, n2 = consts["nf"], consts["n2"]
    hr = consts["n1"] // 2
    tokv = lambda t: t.reshape(bsz, hr, n2, cw)
    tc = _pick(cw, HY_CTILE, LANE)
    tile = pl.BlockSpec((1, hr, HY_S2, tc), lambda b, j, c: (b, 0, j, c))
    out = pl.pallas_call(
        _dft3_body,
        grid=(bsz, n2 // HY_S2, cw // tc),
        in_specs=[pl.BlockSpec((hr, 2 * nf), lambda b, j, c: (0, 0)),
                  pl.BlockSpec((1, 2, nf, HY_S2, tc), lambda b, j, c: (b, 0, 0, j, c)),
                  tile, tile,
                  pl.BlockSpec((1, tc), lambda b, j, c: (0, c))],
        out_specs=tile,
        out_shape=jax.ShapeDtypeStruct((bsz, hr, n2, cw), F32),
        compiler_params=_cp(("parallel", "parallel", "parallel")),
        name="hyena_dft3",
    )(consts["f1inv"][:hr], z, tokv(x0), tokv(uv), bias.reshape(1, cw).astype(F32))
    return out.reshape(bsz, n, cw)


def _hyena(p_lat, p):
    n = p_lat.shape[1]
    consts = _dft_consts(n)
    n1 = consts["n1"]
    x0, uv = _hyena_pre(p_lat, p["conv_w"], p["conv_b"])
    taps, sabs = _hyena_kernel_taps(n, p)
    cw = taps.shape[1]
    ones = jnp.ones((1, cw), F32)
    hk = _dft_stage1(taps[None], consts, n1)
    hspec = _dft_mid(hk, jnp.zeros((1, 2, HY_F1, HY_N2, cw), F32), 1.0 / sabs, consts, True)
    a = _dft_stage1(uv, consts, n1 // 2)
    z = _dft_mid(a, hspec, ones, consts, False)
    return _dft_stage1_inv(z, x0, uv, p["hy_bias"], consts)


def _mod_vectors(c, c_ctx, p):
    bsz, d = c.shape
    cvec = jnp.concatenate([c, c_ctx[None], jnp.zeros((8 - bsz - 1, d), c.dtype)], axis=0).astype(F32)
    m = _mods(cvec, p["w_mod"].astype(F32), p["b_mod"].astype(F32))
    lat = [v[:, None, :] for v in jnp.split(m[:bsz], N_MOD, axis=-1)]
    cx = [jnp.broadcast_to(v[:, None, :], (bsz, 1, d)) for v in jnp.split(m[bsz:bsz + 1], N_MOD, axis=-1)]
    return lat, cx


def _affine(norm_g, shift, scale):
    return norm_g.astype(F32) * (1.0 + scale), shift


def kernel(x, c, ctx, c_ctx,
           l0_w_mod, l0_b_mod, l0_norm1_g, l0_norm2_g, l0_w_in, l0_lam_q1, l0_lam_k1, l0_lam_q2, l0_lam_k2,
           l0_subln_g, l0_mu_prev, l0_mu_next, l0_w0_f, l0_w2_f, l0_a0_f, l0_a2_f, l0_w0_b, l0_w2_b, l0_a0_b,
           l0_a2_b, l0_g2, l0_k_k, l0_k_a, l0_r_k, l0_lnx_g, l0_lnx_b, l0_w_out, l0_mlp_w1, l0_mlp_w2,
           l1_w_mod, l1_b_mod, l1_norm1_g, l1_norm2_g, l1_w_in, l1_conv_w, l1_conv_b, l1_f_w1, l1_f_b1,
           l1_f_w2, l1_f_b2, l1_f_w3, l1_f_b3, l1_f_w4, l1_f_freq, l1_hy_bias, l1_q_norm_g, l1_k_norm_g,
           l1_w_out, l1_mlp_w1, l1_mlp_w2, final_g):
    bsz, n, d = x.shape
    x = x.astype(F32)
    ctx_s = ctx.astype(F32)
    bf = lambda w: w.astype(BF16)

    p0 = dict(w_mod=l0_w_mod, b_mod=l0_b_mod, mu_prev=l0_mu_prev, mu_next=l0_mu_next, w0_f=l0_w0_f, w2_f=l0_w2_f,
              a0_f=l0_a0_f, a2_f=l0_a2_f, w0_b=l0_w0_b, w2_b=l0_w2_b, a0_b=l0_a0_b, a2_b=l0_a2_b, g2=l0_g2,
              k_k=l0_k_k, k_a=l0_k_a, r_k=l0_r_k, lnx_g=l0_lnx_g, lnx_b=l0_lnx_b)
    (sh1, sc1, g1, sh2, sc2, g2), (csh1, csc1, cg1, csh2, csc2, cg2) = _mod_vectors(c, c_ctx, p0)
    aw = d // 2
    w_rw = l0_w_in[:, 3 * aw:]
    rw_cols = -(-(w_rw.shape[1] + 2 * LANE - B_GATE_RANK) // 512) * 512
    w_in0 = bf(jnp.concatenate([w_rw, jnp.zeros((d, rw_cols - w_rw.shape[1]), w_rw.dtype), l0_w_in[:, :3 * aw]],
                               axis=1))
    a1, b1 = _affine(l0_norm1_g, sh1, sc1)
    ca1, cb1 = _affine(l0_norm1_g, csh1, csc1)
    p_lat0 = _inproj(x, a1, b1, w_in0)
    p_ctx0 = _inproj(ctx_s, ca1, cb1, w_in0)
    lam_init = 0.8 - 0.6 * math.exp(-0.3 * 0)
    lam = (jnp.exp(jnp.sum(l0_lam_q1 * l0_lam_k1).astype(F32))
           - jnp.exp(jnp.sum(l0_lam_q2 * l0_lam_k2).astype(F32)) + lam_init)
    oa_lat, oa_ctx = _diff_attention(p_lat0, p_ctx0, rw_cols, aw, lam, lam_init, l0_subln_g)
    nh, hd = l0_r_k.shape
    zero = jnp.zeros((bsz, hd, nh * hd), F32)
    y_ctx, states_c = _rwkv_run(_rwkv_prepare(p_ctx0, p0, rw_cols), (zero, zero), p0)
    y_lat, _ = _rwkv_run(_rwkv_prepare(p_lat0, p0, rw_cols), states_c, p0)
    w_out0 = bf(l0_w_out)
    x = _outproj(oa_lat, y_lat, w_out0, x, g1)
    ctx_s = _outproj(oa_ctx, y_ctx, w_out0, ctx_s, cg1)
    w1, w2 = bf(l0_mlp_w1), bf(l0_mlp_w2)
    a2, b2 = _affine(l0_norm2_g, sh2, sc2)
    ca2, cb2 = _affine(l0_norm2_g, csh2, csc2)
    x = _mlp(x, a2, b2, g2, w1, w2)
    ctx_s = _mlp(ctx_s, ca2, cb2, cg2, w1, w2)

    p1 = dict(w_mod=l1_w_mod, b_mod=l1_b_mod, conv_w=l1_conv_w, conv_b=l1_conv_b, f_w1=l1_f_w1, f_b1=l1_f_b1,
              f_w2=l1_f_w2, f_b2=l1_f_b2, f_w3=l1_f_w3, f_b3=l1_f_b3, f_w4=l1_f_w4, f_freq=l1_f_freq,
              hy_bias=l1_hy_bias)
    (sh1, sc1, g1, sh2, sc2, g2), (csh1, csc1, _, _, _, _) = _mod_vectors(c, c_ctx, p1)
    hy_cols = l1_conv_w.shape[1]
    kv_cols = 2 * (2 * D_HEAD)
    w_in1 = bf(l1_w_in)
    a1, b1 = _affine(l1_norm1_g, sh1, sc1)
    ca1, cb1 = _affine(l1_norm1_g, csh1, csc1)
    p_lat = _inproj(x, a1, b1, w_in1)
    pkv_ctx = _inproj(ctx_s, ca1, cb1, w_in1[:, -kv_cols:])
    o_hy = _hyena(p_lat, p1)
    o_at = _gqa_attention(p_lat, pkv_ctx, hy_cols, l1_q_norm_g, l1_k_norm_g)
    x = _outproj(o_hy, o_at, bf(l1_w_out), x, g1)
    a2, b2 = _affine(l1_norm2_g, sh2, sc2)
    return _mlp(x, a2, b2, g2, bf(l1_mlp_w1), bf(l1_mlp_w2), final_g=final_g)
```

```python
import functools
import math

import numpy as np
import jax
import jax.numpy as jnp
from jax import lax
from jax.experimental import pallas as pl
from jax.experimental.pallas import tpu as pltpu

F32 = jnp.float32
BF16 = jnp.bfloat16
HI = lax.Precision.HIGHEST

NORM_EPS = 1e-6
ROPE_THETA = 10000.0
GRID_W = 64
N_MOD = 6
A_QK_DIM = 64
A_V_DIM = 128
B_HEAD = 64
B_DECAY_RANK = 64
B_ICL_RANK = 64
B_GATE_RANK = 160
B_GN_EPS = 64e-5
HY_EMB = 33
HY_ORDER = 64
HY_TARGET = 1e-2
HY_FAST_PCT = 0.3
HY_SLOW_PCT = 1.5
D_HEAD = 128
D_KV_GROUP = 4

LANE = 128
VMEM_LIMIT = 56 * 1024 * 1024
LOG2E = 1.4426950408889634


def _cp(sem, vmem=VMEM_LIMIT):
    return pltpu.CompilerParams(dimension_semantics=sem, vmem_limit_bytes=vmem)


def _pick(n, pref, step=8):
    t = max(step, min(n, pref) // step * step)
    while n % t:
        t -= step
    return t


def _mods_body(c_ref, w_ref, b_ref, o_ref):
    c = c_ref[...]
    s = c * jax.nn.sigmoid(c)
    o_ref[...] = _mm_x3(s, w_ref[...]) + b_ref[...]


def _mods(cvec, w_mod, b_mod):
    m, d = cvec.shape
    n = w_mod.shape[1]
    tn = _pick(n, 1024, LANE)
    return pl.pallas_call(
        _mods_body,
        grid=(n // tn,),
        in_specs=[pl.BlockSpec((m, d), lambda j: (0, 0)),
                  pl.BlockSpec((d, tn), lambda j: (0, j)),
                  pl.BlockSpec((1, tn), lambda j: (0, j))],
        out_specs=pl.BlockSpec((m, tn), lambda j: (0, j)),
        out_shape=jax.ShapeDtypeStruct((m, n), F32),
        compiler_params=_cp(("arbitrary",)),
        name="mods",
    )(cvec, w_mod, b_mod.reshape(1, n))


def _norm_mod(x, a, b):
    ms = jnp.mean(x * x, axis=-1, keepdims=True)
    return x * lax.rsqrt(ms + NORM_EPS) * a + b


def _inproj_body(x_ref, a_ref, b_ref, w_ref, o_ref, xn_ref):
    @pl.when(pl.program_id(2) == 0)
    def _():
        xn_ref[...] = _norm_mod(x_ref[0], a_ref[0], b_ref[0]).astype(BF16)

    o_ref[0] = jnp.dot(xn_ref[...], w_ref[...], preferred_element_type=F32).astype(o_ref.dtype)


def _inproj(x, a, b, w, out_dtype=BF16, tm_pref=1024, tn_pref=512):
    bsz, n, d = x.shape
    nn = w.shape[1]
    tm, tn = _pick(n, tm_pref), _pick(nn, tn_pref, LANE)
    return pl.pallas_call(
        _inproj_body,
        grid=(bsz, n // tm, nn // tn),
        in_specs=[pl.BlockSpec((1, tm, d), lambda bi, i, j: (bi, i, 0)),
                  pl.BlockSpec((1, 1, d), lambda bi, i, j: (bi, 0, 0)),
                  pl.BlockSpec((1, 1, d), lambda bi, i, j: (bi, 0, 0)),
                  pl.BlockSpec((d, tn), lambda bi, i, j: (0, j))],
        out_specs=pl.BlockSpec((1, tm, tn), lambda bi, i, j: (bi, i, j)),
        out_shape=jax.ShapeDtypeStruct((bsz, n, nn), out_dtype),
        scratch_shapes=[pltpu.VMEM((tm, d), BF16)],
        compiler_params=_cp(("parallel", "parallel", "arbitrary")),
        name="inproj",
    )(x, a, b, w)


def _outproj_body(oa_ref, ob_ref, wa_ref, wb_ref, x_ref, g_ref, y_ref):
    acc = jnp.dot(oa_ref[0].astype(BF16), wa_ref[...], preferred_element_type=F32)
    acc += jnp.dot(ob_ref[0].astype(BF16), wb_ref[...], preferred_element_type=F32)
    y_ref[0] = x_ref[0] + g_ref[0] * acc


def _outproj(oa, ob, w, x, g, tm_pref=1024, tn_pref=512):
    bsz, n, ka = oa.shape
    kb = ob.shape[2]
    d = w.shape[1]
    tm, tn = _pick(n, tm_pref), _pick(d, tn_pref, LANE)
    return pl.pallas_call(
        _outproj_body,
        grid=(bsz, n // tm, d // tn),
        in_specs=[pl.BlockSpec((1, tm, ka), lambda bi, i, j: (bi, i, 0)),
                  pl.BlockSpec((1, tm, kb), lambda bi, i, j: (bi, i, 0)),
                  pl.BlockSpec((ka, tn), lambda bi, i, j: (0, j)),
                  pl.BlockSpec((kb, tn), lambda bi, i, j: (0, j)),
                  pl.BlockSpec((1, tm, tn), lambda bi, i, j: (bi, i, j)),
                  pl.BlockSpec((1, 1, tn), lambda bi, i, j: (bi, 0, j))],
        out_specs=pl.BlockSpec((1, tm, tn), lambda bi, i, j: (bi, i, j)),
        out_shape=jax.ShapeDtypeStruct((bsz, n, d), F32),
        compiler_params=_cp(("parallel", "parallel", "parallel")),
        name="outproj",
    )(oa, ob, w[:ka], w[ka:], x, g)


def _mlp_body(x_ref, a_ref, b_ref, g_ref, w1_ref, w2_ref, fg_ref, y_ref, xn_ref, acc_ref, *, nf, final):
    f = pl.program_id(2)

    @pl.when(f == 0)
    def _():
        xn_ref[...] = _norm_mod(x_ref[0], a_ref[0], b_ref[0]).astype(BF16)
        acc_ref[...] = jnp.zeros_like(acc_ref)

    h = jnp.dot(xn_ref[...], w1_ref[...], preferred_element_type=F32)
    h = jnp.square(jnp.maximum(h, 0.0)).astype(BF16)
    acc_ref[...] += jnp.dot(h, w2_ref[...], preferred_element_type=F32)

    @pl.when(f == nf - 1)
    def _():
        y = x_ref[0] + g_ref[0] * acc_ref[...]
        if final:
            ms = jnp.mean(y * y, axis=-1, keepdims=True)
            y = y * lax.rsqrt(ms + NORM_EPS) * fg_ref[...]
        y_ref[0] = y


def _mlp(x, a, b, g, w1, w2, final_g=None, tm_pref=512, tf_pref=1024):
    bsz, n, d = x.shape
    dff = w1.shape[1]
    tm, tf = _pick(n, tm_pref), _pick(dff, tf_pref, LANE)
    nf = dff // tf
    final = final_g is not None
    fg = (final_g if final else jnp.ones((d,), F32)).reshape(1, d).astype(F32)
    return pl.pallas_call(
        functools.partial(_mlp_body, nf=nf, final=final),
        grid=(bsz, n // tm, nf),
        in_specs=[pl.BlockSpec((1, tm, d), lambda bi, i, f: (bi, i, 0)),
                  pl.BlockSpec((1, 1, d), lambda bi, i, f: (bi, 0, 0)),
                  pl.BlockSpec((1, 1, d), lambda bi, i, f: (bi, 0, 0)),
                  pl.BlockSpec((1, 1, d), lambda bi, i, f: (bi, 0, 0)),
                  pl.BlockSpec((d, tf), lambda bi, i, f: (0, f)),
                  pl.BlockSpec((tf, d), lambda bi, i, f: (f, 0)),
                  pl.BlockSpec((1, d), lambda bi, i, f: (0, 0))],
        out_specs=pl.BlockSpec((1, tm, d), lambda bi, i, f: (bi, i, 0)),
        out_shape=jax.ShapeDtypeStruct((bsz, n, d), F32),
        scratch_shapes=[pltpu.VMEM((tm, d), BF16), pltpu.VMEM((tm, d), F32)],
        compiler_params=_cp(("parallel", "parallel", "arbitrary")),
        name="mlp",
    )(x, a, b, g, w1, w2, fg)


def _rope_tables(n, dim, reps):
    rows = n // GRID_W
    row = jnp.repeat(jnp.arange(rows, dtype=F32), GRID_W)
    col = jnp.tile(jnp.arange(GRID_W, dtype=F32), rows)
    half = dim // 2
    inv = ROPE_THETA ** (-jnp.arange(0, half, 2, dtype=F32) / half)
    ar, ac = row[:, None] * inv, col[:, None] * inv
    cos = jnp.concatenate([jnp.cos(ar), jnp.cos(ar), jnp.cos(ac), jnp.cos(ac)], axis=-1)
    sin = jnp.concatenate([-jnp.sin(ar), jnp.sin(ar), -jnp.sin(ac), jnp.sin(ac)], axis=-1)
    return jnp.tile(cos, (1, reps)), jnp.tile(sin, (1, reps))


def _swap_matrix(dim, reps):
    q = dim // 4
    width = dim * reps
    p = np.zeros((width, width), np.float32)
    for j in range(width):
        base, r = (j // dim) * dim, j % dim
        axis, which, f = r // (2 * q), (r % (2 * q)) // q, r % q
        p[base + axis * 2 * q + (1 - which) * q + f, j] = 1.0
    return p


def _rope_norm(x, cos_ref, sin_ref, p_ref, g_ref, gs_ref, *, norm, rope, scale=1.0):
    y = x.astype(F32)
    if rope:
        ys = jnp.dot(x, p_ref[...], preferred_element_type=F32)
    if norm:
        rs = lax.rsqrt(jnp.mean(y * y, axis=-1, keepdims=True) + NORM_EPS)
        y = y * rs * g_ref[...]
        if rope:
            ys = ys * rs * gs_ref[...]
    if rope:
        y = y * cos_ref[...] + ys * sin_ref[...]
    return y * scale if scale != 1.0 else y


def _rope_operands(n, dim, gain, rope):
    reps = LANE // dim
    if rope:
        cos, sin = _rope_tables(n, dim, reps)
        pm = jnp.asarray(_swap_matrix(dim, reps), BF16)
    else:
        cos = sin = jnp.zeros((n, LANE), F32)
        pm = jnp.zeros((LANE, LANE), BF16)
    if gain is not None:
        g = jnp.tile(gain.astype(F32), reps).reshape(1, LANE)
        gs = jnp.dot(g, jnp.asarray(_swap_matrix(dim, reps)), precision=HI)
    else:
        g = gs = jnp.ones((1, LANE), F32)
    return cos, sin, pm, g, gs


def _kprep_body(x_ref, cos_ref, sin_ref, p_ref, g_ref, gs_ref, o_ref, *, heads, norm, rope):
    for h in range(heads):
        sl = slice(h * LANE, (h + 1) * LANE)
        y = _rope_norm(x_ref[0, :, sl], cos_ref, sin_ref, p_ref, g_ref, gs_ref, norm=norm, rope=rope)
        o_ref[0, :, sl] = y.astype(o_ref.dtype)


def _kprep(p, col0, heads, *, dim, gain=None, rope=True, tq_pref=512):
    bsz, n, _ = p.shape
    tq = _pick(n, tq_pref)
    hb = heads
    while col0 % (hb * LANE):
        hb //= 2
    w = hb * LANE
    cb = col0 // w
    cos, sin, pm, g, gs = _rope_operands(n, dim, gain, rope)
    return pl.pallas_call(
        functools.partial(_kprep_body, heads=hb, norm=gain is not None, rope=rope),
        grid=(bsz, n // tq, heads // hb),
        in_specs=[pl.BlockSpec((1, tq, w), lambda b, i, j: (b, i, cb + j)),
                  pl.BlockSpec((tq, LANE), lambda b, i, j: (i, 0)),
                  pl.BlockSpec((tq, LANE), lambda b, i, j: (i, 0)),
                  pl.BlockSpec((LANE, LANE), lambda b, i, j: (0, 0)),
                  pl.BlockSpec((1, LANE), lambda b, i, j: (0, 0)),
                  pl.BlockSpec((1, LANE), lambda b, i, j: (0, 0))],
        out_specs=pl.BlockSpec((1, tq, w), lambda b, i, j: (b, i, j)),
        out_shape=jax.ShapeDtypeStruct((bsz, n, heads * LANE), BF16),
        compiler_params=_cp(("parallel", "parallel", "parallel")),
        name="kprep",
    )(p, cos, sin, pm, g, gs)


FLASH_KEYS = 1280
FLASH_HEADROOM = 64.0


def _knorm_body(k_ref, o_ref, *, heads):
    for h in range(heads):
        k = k_ref[0, :, h * LANE:(h + 1) * LANE].astype(F32)
        nrm2 = jnp.max(jnp.sum(k * k, axis=1, keepdims=True), axis=0, keepdims=True)
        o_ref[0, h] = jnp.broadcast_to(jnp.sqrt(nrm2), (8, LANE))


def _key_block_norms(k, kcol0, heads, ts):
    bsz, nk, _ = k.shape
    nsub = nk // ts
    hb = heads
    while kcol0 % (hb * LANE):
        hb //= 2
    kb = kcol0 // (hb * LANE)
    out = pl.pallas_call(
        functools.partial(_knorm_body, heads=hb),
        grid=(bsz, heads // hb, nsub),
        in_specs=[pl.BlockSpec((1, ts, hb * LANE), lambda b, j, c: (b, c, kb + j))],
        out_specs=pl.BlockSpec((1, hb, 8, LANE), lambda b, j, c: (b, j, c, 0)),
        out_shape=jax.ShapeDtypeStruct((bsz, heads, nsub * 8, LANE), F32),
        compiler_params=_cp(("parallel", "parallel", "parallel")),
        name="key_block_norms",
    )(k)
    return out[:, :, ::8, 0].reshape(-1)


def _flash_body(kmax_ref, qr_ref, cos_ref, sin_ref, pm_ref, g_ref, gs_ref, k_ref, v_ref, e1_ref, e2_ref, o_ref,
                q_ref, m_ref, l_ref, acc_ref, thr_ref, *, ts, nsub, mode, tq, norm, rope, scale):
    base = (pl.program_id(0) * pl.num_programs(1) + pl.program_id(1)) * nsub
    prep = functools.partial(_rope_norm, cos_ref=cos_ref, sin_ref=sin_ref, p_ref=pm_ref, g_ref=g_ref, gs_ref=gs_ref,
                             norm=norm, rope=rope, scale=scale)
    if mode == "diff":
        y = prep(qr_ref[0])
        lane = lax.broadcasted_iota(jnp.int32, y.shape, 1)
        q_ref[:tq] = jnp.where(lane < A_QK_DIM, y, 0.0).astype(BF16)
        q_ref[tq:] = jnp.where(lane >= A_QK_DIM, y, 0.0).astype(BF16)
    else:
        for j in range(D_KV_GROUP):
            q_ref[j * tq:(j + 1) * tq] = prep(qr_ref[0, :, j * LANE:(j + 1) * LANE]).astype(BF16)
    q = q_ref[...]

    def scores(off, size):
        k = k_ref[0, pl.ds(off, size), :]
        return lax.dot_general(q, k, (((1,), (1,)), ((), ())), preferred_element_type=F32)

    def block(c):
        off = pl.multiple_of(c * ts, ts)
        return scores(off, ts), v_ref[0, pl.ds(off, ts), :]

    def wide(m):
        return jnp.concatenate([m] * (ts // LANE), axis=1)

    m0 = jnp.broadcast_to(jnp.max(scores(0, 2 * LANE), axis=1, keepdims=True), m_ref.shape)
    m_ref[...] = m0
    l_ref[...] = jnp.zeros_like(l_ref)
    acc_ref[...] = jnp.zeros_like(acc_ref)
    qf = q.astype(F32)
    qn = jnp.sqrt(jnp.sum(qf * qf, axis=1, keepdims=True))
    thr_ref[0] = jnp.min((m0[:, :1] + FLASH_HEADROOM) / qn)

    def step(c, carry):
        fixed = kmax_ref[base + c] * 1.001 <= thr_ref[0]

        @pl.when(fixed)
        def _():
            s, v = block(c)
            p = jnp.exp2(s - wide(m_ref[...]))
            l_ref[...] += jnp.sum(p, axis=1, keepdims=True)
            acc_ref[...] += jnp.dot(p.astype(BF16), v, preferred_element_type=F32)

        @pl.when(jnp.logical_not(fixed))
        def _():
            s, v = block(c)
            m_prev = m_ref[...]
            m_new = jnp.maximum(m_prev, jnp.max(s, axis=1, keepdims=True))
            alpha = jnp.exp2(m_prev - m_new)
            p = jnp.exp2(s - wide(m_new))
            l_ref[...] = alpha * l_ref[...] + jnp.sum(p, axis=1, keepdims=True)
            acc_ref[...] = alpha * acc_ref[...] + jnp.dot(p.astype(BF16), v, preferred_element_type=F32)
            m_ref[...] = m_new

        return carry

    lax.fori_loop(0, nsub, step, 0)

    o = acc_ref[...] / l_ref[...]
    if mode == "diff":
        d = o[:tq] - e1_ref[...] * o[tq:]
        ms = jnp.mean(d * d, axis=-1, keepdims=True)
        o_ref[0] = (d * lax.rsqrt(ms + NORM_EPS) * e2_ref[...]).astype(o_ref.dtype)
    else:
        for j in range(D_KV_GROUP):
            o_ref[0, :, j * LANE:(j + 1) * LANE] = o[j * tq:(j + 1) * tq].astype(o_ref.dtype)


def _flash(pq, qcol0, k, v, kcol0, vcol0, e1, e2, *, mode, hk, dim, tq, gain=None, rope=True):
    bsz, n, _ = pq.shape
    nk = k.shape[1]
    g = 2 if mode == "diff" else D_KV_GROUP
    mq = g * tq
    qw = LANE if mode == "diff" else D_KV_GROUP * LANE
    qb = qcol0 // qw
    ts = _pick(nk, FLASH_KEYS, 2 * LANE)
    kb, vb = kcol0 // LANE, vcol0 // LANE
    ow = LANE if mode == "diff" else D_KV_GROUP * LANE
    kmax = _key_block_norms(k, kcol0, hk, ts)
    cos, sin, pm, gq, gqs = _rope_operands(n, dim, gain, rope)
    const = lambda shape: pl.BlockSpec(shape, lambda b, h, i: (0, 0))
    return pl.pallas_call(
        functools.partial(_flash_body, ts=ts, nsub=nk // ts, mode=mode, tq=tq, norm=gain is not None, rope=rope,
                          scale=dim ** -0.5 * LOG2E),
        grid=(bsz, hk, n // tq),
        in_specs=[pl.BlockSpec(memory_space=pltpu.SMEM),
                  pl.BlockSpec((1, tq, qw), lambda b, h, i: (b, i, qb + h)),
                  pl.BlockSpec((tq, LANE), lambda b, h, i: (i, 0)),
                  pl.BlockSpec((tq, LANE), lambda b, h, i: (i, 0)),
                  const((LANE, LANE)), const((1, LANE)), const((1, LANE)),
                  pl.BlockSpec((1, nk, LANE), lambda b, h, i: (b, 0, kb + h)),
                  pl.BlockSpec((1, nk, LANE), lambda b, h, i: (b, 0, vb + h)),
                  const((1, LANE)), const((1, LANE))],
        out_specs=pl.BlockSpec((1, tq, ow), lambda b, h, i: (b, i, h)),
        out_shape=jax.ShapeDtypeStruct((bsz, n, hk * ow), BF16),
        scratch_shapes=[pltpu.VMEM((mq, LANE), BF16), pltpu.VMEM((mq, LANE), F32), pltpu.VMEM((mq, LANE), F32),
                        pltpu.VMEM((mq, LANE), F32), pltpu.SMEM((1,), F32)],
        compiler_params=_cp(("parallel", "parallel", "arbitrary")),
        name="flash_" + mode,
    )(kmax, pq, cos, sin, pm, gq, gqs, k, v, e1, e2)


def _diff_attention(pa_lat, pa_ctx, c0, aw, lam, lam_init, subln_g):
    n, nctx = pa_lat.shape[1], pa_ctx.shape[1]
    heads = aw // LANE
    k_l = _kprep(pa_lat, c0 + aw, heads, dim=A_QK_DIM)
    k_all = jnp.concatenate([k_l, pa_ctx[..., c0 + aw:c0 + 2 * aw]], axis=1)
    v_all = jnp.concatenate([pa_lat[..., c0 + 2 * aw:c0 + 3 * aw], pa_ctx[..., c0 + 2 * aw:c0 + 3 * aw]], axis=1)
    e1 = jnp.full((1, LANE), lam, F32)
    e2 = (subln_g.astype(F32) * (1.0 - lam_init)).reshape(1, LANE)
    o_lat = _flash(pa_lat, c0, k_all, v_all, 0, 0, e1, e2, mode="diff", hk=heads, dim=A_QK_DIM, tq=_pick(n, 1024))
    o_ctx = _flash(pa_ctx, c0, pa_ctx, pa_ctx, c0 + aw, c0 + 2 * aw, e1, e2, mode="diff", hk=heads, dim=A_QK_DIM,
                   tq=_pick(nctx, 512), rope=False)
    return o_lat, o_ctx


def _gqa_attention(p_lat, pkv_ctx, qcol0, q_norm_g, k_norm_g):
    n = p_lat.shape[1]
    hq, hk = 2 * D_KV_GROUP, 2
    kcol0 = qcol0 + hq * D_HEAD
    vcol0 = kcol0 + hk * D_HEAD
    k_l = _kprep(p_lat, kcol0, hk, dim=D_HEAD, gain=k_norm_g)
    k_c = _kprep(pkv_ctx, 0, hk, dim=D_HEAD, gain=k_norm_g, rope=False)
    k_all = jnp.concatenate([k_l, k_c], axis=1)
    v_all = jnp.concatenate([p_lat[..., vcol0:vcol0 + hk * D_HEAD], pkv_ctx[..., hk * D_HEAD:]], axis=1)
    dummy = jnp.zeros((1, LANE), F32)
    return _flash(p_lat, qcol0, k_all, v_all, 0, 0, dummy, dummy, mode="gqa", hk=hk, dim=D_HEAD,
                  tq=_pick(n, 512), gain=q_norm_g)


RW_CHUNK = 64
RW_HALO = 8


def _mm_bf(a, b):
    return jnp.dot(a.astype(BF16), b.astype(BF16), preferred_element_type=F32)


def _split3(x):
    x1 = x.astype(BF16)
    r1 = x - x1.astype(F32)
    x2 = r1.astype(BF16)
    return x1, x2, (r1 - x2.astype(F32)).astype(BF16)


def _mm_x3(a, b):
    ah, al, _ = _split3(a)
    bh, bl, _ = _split3(b)
    d = lambda u, w: jnp.dot(u, w, preferred_element_type=F32)
    return d(ah, bh) + (d(ah, bl) + d(al, bh))


def _mm_sel(z, sel):
    z1, z2, z3 = _split3(z)
    d = lambda u: jnp.dot(u, sel, preferred_element_type=F32)
    return d(z1) + (d(z2) + d(z3))


def _rwprep_body(x_ref, hp_ref, hn_ref, mup_ref, mun_ref, vec_ref, w2f_ref, w2b_ref, a2f_ref, a2b_ref,
                 g2_ref, sel_ref, selt_ref,
                 r_ref, v_ref, kk_ref, lwf_ref, bf_ref, kf_ref, lwb_ref, bb_ref, kb_ref, g_ref, bonus_ref,
                 *, nb, tm, bw):
    i = pl.program_id(1)
    x = x_ref[0].astype(F32)
    row = lax.broadcasted_iota(jnp.int32, x.shape, 0)
    prev_edge = jnp.where(i > 0, hp_ref[0, RW_HALO - 1:RW_HALO, :].astype(F32), 0.0)
    next_edge = jnp.where(i < nb - 1, hn_ref[0, 0:1, :].astype(F32), 0.0)
    prev = jnp.where(row == 0, prev_edge, pltpu.roll(x, 1, 0))
    nxt = jnp.where(row == tm - 1, next_edge, pltpu.roll(x, tm - 1, 0))
    xs = x + mup_ref[...] * (prev - x) + mun_ref[...] * (nxt - x)
    r, k, v = xs[:, :bw], xs[:, bw:2 * bw], xs[:, 2 * bw:3 * bw]
    lr = xs[:, 3 * bw:3 * bw + LANE]
    gd = xs[:, 3 * bw + LANE:3 * bw + 3 * LANE]
    k_k, k_a, r_k = vec_ref[0:1], vec_ref[1:2], vec_ref[2:3]
    w0f, a0f, w0b, a0b = vec_ref[3:4], vec_ref[4:5], vec_ref[5:6], vec_ref[6:7]
    sel, selt = sel_ref[...], selt_ref[...]

    def head_sum(z):
        return _mm_sel(_mm_sel(z, sel), selt)

    kk = k * k_k
    kk = kk * lax.rsqrt(head_sum(kk * kk) + 1e-12)
    g_ref[0] = _mm_bf(jax.nn.sigmoid(gd), g2_ref[...]).astype(g_ref.dtype)
    th = jnp.tanh(lr)
    ksum = jnp.zeros_like(k)
    for (w0, a0, w2_ref, a2_ref, lw_ref, b_ref, kd_ref) in (
            (w0f, a0f, w2f_ref, a2f_ref, lwf_ref, bf_ref, kf_ref),
            (w0b, a0b, w2b_ref, a2b_ref, lwb_ref, bb_ref, kb_ref)):
        logw = -math.exp(-0.5) * jax.nn.sigmoid(w0 + _mm_bf(th, w2_ref[...]))
        a = jax.nn.sigmoid(a0 + _mm_bf(lr, a2_ref[...]))
        k_d = k * (1.0 + (a - 1.0) * k_a)
        ksum = ksum + k_d
        lw_ref[0] = logw
        b_ref[0] = kk * a
        kd_ref[0] = k_d
    r_ref[0] = r
    v_ref[0] = v
    kk_ref[0] = kk
    bonus_ref[0] = head_sum(r * ksum * r_k) * v


def _rwkv_prepare(pr, p, cols):
    bsz, n, _ = pr.shape
    bw = p["w0_f"].shape[0]
    tm = _pick(n, 256)
    nb = n // tm
    hb = tm // RW_HALO
    assert cols - 3 * bw >= 3 * LANE and B_DECAY_RANK + B_ICL_RANK == LANE

    def padded(vec):
        return jnp.concatenate([vec.astype(F32), jnp.zeros((cols - vec.shape[0],), F32)]).reshape(1, cols)

    vecs = jnp.stack([p["k_k"], p["k_a"], p["r_k"].reshape(-1), p["w0_f"], p["a0_f"], p["w0_b"], p["a0_b"],
                      jnp.zeros((bw,), F32)]).astype(F32)
    zr = jnp.zeros((B_DECAY_RANK, bw), F32)
    w2 = {d: jnp.concatenate([p["w2_" + d].astype(F32), zr], axis=0).astype(BF16) for d in "fb"}
    a2 = {d: jnp.concatenate([zr, p["a2_" + d].astype(F32)], axis=0).astype(BF16) for d in "fb"}
    g2 = jnp.concatenate([p["g2"].astype(F32), jnp.zeros((2 * LANE - B_GATE_RANK, bw), F32)], axis=0).astype(BF16)
    sel, selt = _head_selectors(bw)

    full = lambda shape: pl.BlockSpec(shape, lambda b, i: tuple(0 for _ in shape))
    tok = pl.BlockSpec((1, tm, bw), lambda b, i: (b, i, 0))
    tok_f32 = jax.ShapeDtypeStruct((bsz, n, bw), F32)
    outs = pl.pallas_call(
        functools.partial(_rwprep_body, nb=nb, tm=tm, bw=bw),
        grid=(bsz, nb),
        in_specs=[pl.BlockSpec((1, tm, cols), lambda b, i: (b, i, 0)),
                  pl.BlockSpec((1, RW_HALO, cols), lambda b, i: (b, jnp.maximum(i * hb - 1, 0), 0)),
                  pl.BlockSpec((1, RW_HALO, cols), lambda b, i: (b, jnp.minimum((i + 1) * hb, nb * hb - 1), 0)),
                  full((1, cols)), full((1, cols)), full((8, bw)),
                  full((LANE, bw)), full((LANE, bw)), full((LANE, bw)), full((LANE, bw)),
                  full((2 * LANE, bw)), full((bw, LANE)), full((LANE, bw))],
        out_specs=[tok] * 11,
        out_shape=[tok_f32] * 9 + [jax.ShapeDtypeStruct((bsz, n, bw), BF16), tok_f32],
        compiler_params=_cp(("parallel", "parallel")),
        name="rwkv_prepare",
    )(pr, pr, pr, padded(p["mu_prev"]), padded(p["mu_next"]), vecs, w2["f"], w2["b"], a2["f"], a2["b"],
      g2, sel, selt)
    r, v, kk, lwf, bf, kf, lwb, bb, kb, g, bonus = outs
    return dict(r=r, v=v, kk=kk, g=g, bonus=bonus, f=(lwf, bf, kf), b=(lwb, bb, kb))


RW_GROUP = 4


def _bd(x):
    t = jnp.concatenate([x.astype(BF16)] * RW_GROUP, axis=1)
    r = lax.broadcasted_iota(jnp.int32, t.shape, 1) // B_HEAD
    c = lax.broadcasted_iota(jnp.int32, t.shape, 2) // B_HEAD
    return jnp.where(r == c, t, jnp.zeros_like(t))


def _bdot(a, b, dims):
    return jnp.einsum(dims, a.astype(BF16), b.astype(BF16), preferred_element_type=F32)


def _groups(x):
    gw = RW_GROUP * B_HEAD
    return jnp.stack([x[:, g * gw:(g + 1) * gw] for g in range(x.shape[1] // gw)])


def _rw_decays(r, kk, b, kd, lw, reverse):
    ck = r.shape[0]
    ti = lax.broadcasted_iota(jnp.int32, (ck, ck), 0)
    si = lax.broadcasted_iota(jnp.int32, (ck, ck), 1)
    tri = ((si >= ti) if reverse else (si <= ti)).astype(BF16)
    l1, l2, l3 = _split3(lw)
    tdot = lambda z: jnp.dot(tri, z, preferred_element_type=F32)
    cum = tdot(l1) + (tdot(l2) + tdot(l3))
    total = cum[0:1] if reverse else cum[ck - 1:ck]
    winv = jnp.exp(-cum)
    wrest = jnp.exp(total - cum)
    return (-kk * jnp.exp(cum - lw), b * winv, kd * winv, r * jnp.exp(cum), b * wrest, kd * wrest, jnp.exp(total))


def _rw_chunks(fwd, bwd, s_f, s_b):
    nseq = len(fwd)
    ck, w = fwd[0][0].shape
    gw = RW_GROUP * B_HEAD
    ng = w // gw
    parts = []
    for probs, rev in ((fwd, False), (bwd, True)):
        for (r, v, kk, b, kd, lw) in probs:
            parts.append([_groups(z) for z in _rw_decays(r, kk, b, kd, lw, rev)] + [_groups(v)])
    at, bt, kt, rt, btw, ktw, decay, v = [jnp.concatenate(zs, axis=0) for zs in zip(*parts)]
    s = jnp.concatenate([_groups(z) for z in list(s_f) + list(s_b)], axis=0)
    nslab = 2 * nseq * ng
    shape = (nslab, ck, gw)
    lag = lax.broadcasted_iota(jnp.int32, shape, 1) - lax.broadcasted_iota(jnp.int32, shape, 2) % ck
    lag = jnp.where(lax.broadcasted_iota(jnp.int32, shape, 0) >= nseq * ng, -lag, lag)
    incl, strict = lag >= 0, lag > 0
    eye = (lag == 0).astype(F32)

    bdv = _bd(v)
    lhs = jnp.concatenate([at, rt], axis=1)
    lb, lk = _bdot(lhs, _bd(bt), "gtk,gnk->gtn"), _bdot(lhs, _bd(kt), "gtk,gnk->gtn")
    lab, mrb = jnp.where(strict, lb[:, :ck], 0.0), jnp.where(incl, lb[:, ck:], 0.0)
    lak, mrk = jnp.where(strict, lk[:, :ck], 0.0), jnp.where(incl, lk[:, ck:], 0.0)
    tinv = eye + lab
    pw = _bdot(lab, _bd(lab), "gtk,gkn->gtn")
    span = 4
    while span < ck:
        both = _bdot(jnp.concatenate([tinv, pw], axis=1), _bd(pw), "gtk,gkn->gtn")
        tinv, pw = tinv + both[:, :ck], both[:, ck:]
        span *= 2
    tinv = tinv + _bdot(tinv, _bd(pw), "gtk,gkn->gtn")
    vmix = _bdot(jnp.concatenate([lak, mrk], axis=1), bdv, "gtk,gkn->gtn")
    p1 = _bdot(tinv, _bd(at), "gtk,gkn->gtn")
    q1 = _bdot(tinv, _bd(vmix[:, :ck]), "gtk,gkn->gtn")
    smix = _bdot(jnp.concatenate([p1, rt], axis=1), _bd(s), "gtk,gnk->gtn")
    u = smix[:, :ck] + q1
    y = smix[:, ck:] + _bdot(mrb, _bd(u), "gtk,gkn->gtn") + vmix[:, ck:]
    z = _bdot(jnp.concatenate([u, v], axis=1), jnp.concatenate([btw, ktw], axis=1), "gtm,gtn->gmn")
    lane_head = lax.broadcasted_iota(jnp.int32, (nslab, B_HEAD, gw), 2) // B_HEAD
    upd = jnp.zeros_like(s)
    for h in range(RW_GROUP):
        upd = upd + jnp.where(lane_head == h, z[:, h * B_HEAD:(h + 1) * B_HEAD], 0.0)
    s_new = s * decay + upd
    lanes = lambda x, q: jnp.concatenate([x[q * ng + g] for g in range(ng)], axis=1)
    seqs = range(nseq)
    return ([lanes(y, q) for q in seqs], [lanes(y, nseq + q) for q in seqs],
            [lanes(s_new, q) for q in seqs], [lanes(s_new, nseq + q) for q in seqs])


def _rwscan_body(rf_ref, vf_ref, kkf_ref, bf_ref, kdf_ref, lwf_ref, rb_ref, vb_ref, kkb_ref, bb_ref, kdb_ref, lwb_ref,
                 s0f_ref, s0b_ref, yf_ref, yb_ref, sTf_ref, sTb_ref, sf_ref, sb_ref, *, nc, nseq):
    c = pl.program_id(0)

    @pl.when(c == 0)
    def _():
        sf_ref[...] = s0f_ref[...]
        sb_ref[...] = s0b_ref[...]

    seqs = range(nseq)
    yf, yb, sf, sb = _rw_chunks(
        [(rf_ref[q], vf_ref[q], kkf_ref[q], bf_ref[q], kdf_ref[q], lwf_ref[q]) for q in seqs],
        [(rb_ref[q], vb_ref[q], kkb_ref[q], bb_ref[q], kdb_ref[q], lwb_ref[q]) for q in seqs],
        [sf_ref[q] for q in seqs], [sb_ref[q] for q in seqs])
    for q in seqs:
        yf_ref[q] = yf[q]
        yb_ref[q] = yb[q]
        sf_ref[q] = sf[q]
        sb_ref[q] = sb[q]

    @pl.when(c == nc - 1)
    def _():
        for q in seqs:
            sTf_ref[q] = sf[q]
            sTb_ref[q] = sb[q]


def _rwkv_scan(r, v, kk, fwd, bwd, s0f, s0b):
    bsz, n, w = r.shape
    ck = RW_CHUNK
    assert n % ck == 0 and ck == B_HEAD and w % (RW_GROUP * B_HEAD) == 0
    nc = n // ck
    seq_f = pl.BlockSpec((bsz, ck, w), lambda c: (0, c, 0))
    seq_b = pl.BlockSpec((bsz, ck, w), lambda c: (0, nc - 1 - c, 0))
    st = pl.BlockSpec((bsz, B_HEAD, w), lambda c: (0, 0, 0))
    y_shape = jax.ShapeDtypeStruct((bsz, n, w), F32)
    s_shape = jax.ShapeDtypeStruct((bsz, B_HEAD, w), F32)
    (lwf, bf, kf), (lwb, bb, kb) = fwd, bwd
    return pl.pallas_call(
        functools.partial(_rwscan_body, nc=nc, nseq=bsz),
        grid=(nc,),
        in_specs=[seq_f] * 6 + [seq_b] * 6 + [st, st],
        out_specs=[seq_f, seq_b, st, st],
        out_shape=[y_shape, y_shape, s_shape, s_shape],
        scratch_shapes=[pltpu.VMEM((bsz, B_HEAD, w), F32), pltpu.VMEM((bsz, B_HEAD, w), F32)],
        compiler_params=_cp(("arbitrary",)),
        name="rwkv_scan",
    )(r, v, kk, bf, kf, lwf, r, v, kk, bb, kb, lwb, s0f, s0b)


def _rwpost_body(yf_ref, yb_ref, g_ref, bonus_ref, lg_ref, lb_ref, sel_ref, selt_ref, o_ref):
    sel, selt = sel_ref[...], selt_ref[...]
    head_mean = lambda z: _mm_sel(_mm_sel(z, sel), selt) * (1.0 / B_HEAD)
    y = yf_ref[0] + yb_ref[0]
    d = y - head_mean(y)
    y = d * lax.rsqrt(head_mean(d * d) + B_GN_EPS)
    y = y * lg_ref[...] + lb_ref[...] + bonus_ref[0]
    o_ref[0] = (y * g_ref[0].astype(F32)).astype(o_ref.dtype)


def _head_selectors(bw):
    sel_np = np.zeros((bw, LANE), np.float32)
    sel_np[np.arange(bw), np.arange(bw) // B_HEAD] = 1.0
    return jnp.asarray(sel_np, BF16), jnp.asarray(sel_np.T, BF16)


def _rwkv_post(y_f, y_b, g, bonus, lnx_g, lnx_b):
    bsz, n, bw = y_f.shape
    tm = _pick(n, 256)
    tok = pl.BlockSpec((1, tm, bw), lambda b, i: (b, i, 0))
    vec = pl.BlockSpec((1, bw), lambda b, i: (0, 0))
    sel, selt = _head_selectors(bw)
    return pl.pallas_call(
        _rwpost_body,
        grid=(bsz, n // tm),
        in_specs=[tok, tok, tok, tok, vec, vec,
                  pl.BlockSpec((bw, LANE), lambda b, i: (0, 0)), pl.BlockSpec((LANE, bw), lambda b, i: (0, 0))],
        out_specs=tok,
        out_shape=jax.ShapeDtypeStruct((bsz, n, bw), BF16),
        compiler_params=_cp(("parallel", "parallel")),
        name="rwkv_post",
    )(y_f, y_b, g, bonus, lnx_g.reshape(1, bw).astype(F32), lnx_b.reshape(1, bw).astype(F32), sel, selt)


def _rwkv_run(prep, states0, p):
    y_f, y_b, s_f, s_b = _rwkv_scan(prep["r"], prep["v"], prep["kk"], prep["f"], prep["b"], states0[0], states0[1])
    return _rwkv_post(y_f, y_b, prep["g"], prep["bonus"], p["lnx_g"], p["lnx_b"]), (s_f, s_b)


HY_N2 = 128
HY_HALO = 8


def _hypre_body(x_ref, hp_ref, hn_ref, cw_ref, cb_ref, x0_ref, uv_ref, *, nb, tm, cw):
    i = pl.program_id(1)
    x = x_ref[0].astype(F32)
    row = lax.broadcasted_iota(jnp.int32, x.shape, 0)
    prev_edge = jnp.where(i > 0, hp_ref[0, HY_HALO - 1:HY_HALO, :].astype(F32), 0.0)
    next_edge = jnp.where(i < nb - 1, hn_ref[0, 0:1, :].astype(F32), 0.0)
    prev = jnp.where(row == 0, prev_edge, pltpu.roll(x, 1, 0))
    nxt = jnp.where(row == tm - 1, next_edge, pltpu.roll(x, tm - 1, 0))
    u = prev * cw_ref[0:1] + x * cw_ref[1:2] + nxt * cw_ref[2:3] + cb_ref[...]
    x0_ref[0] = u[:, :cw]
    uv_ref[0] = u[:, 2 * cw:3 * cw] * u[:, cw:2 * cw]


def _hyena_pre(p, conv_w, conv_b):
    bsz, n, _ = p.shape
    hc = conv_w.shape[1]
    cw = hc // 3
    tm = _pick(n, 256)
    nb = n // tm
    hb = tm // HY_HALO
    cwp = jnp.concatenate([conv_w.astype(F32), jnp.zeros((5, hc), F32)], axis=0)
    out = pl.BlockSpec((1, tm, cw), lambda b, i: (b, i, 0))
    return pl.pallas_call(
        functools.partial(_hypre_body, nb=nb, tm=tm, cw=cw),
        grid=(bsz, nb),
        in_specs=[pl.BlockSpec((1, tm, hc), lambda b, i: (b, i, 0)),
                  pl.BlockSpec((1, HY_HALO, hc), lambda b, i: (b, jnp.maximum(i * hb - 1, 0), 0)),
                  pl.BlockSpec((1, HY_HALO, hc), lambda b, i: (b, jnp.minimum((i + 1) * hb, nb * hb - 1), 0)),
                  pl.BlockSpec((8, hc), lambda b, i: (0, 0)),
                  pl.BlockSpec((1, hc), lambda b, i: (0, 0))],
        out_specs=[out, out],
        out_shape=[jax.ShapeDtypeStruct((bsz, n, cw), F32)] * 2,
        compiler_params=_cp(("parallel", "parallel")),
        name="hyena_pre",
    )(p, p, p, cwp, conv_b.reshape(1, hc).astype(F32))


def _hyfilt_body(z_ref, t_ref, z0_ref, w1_ref, w2_ref, w3_ref, w4_ref, w4b_ref, vec_ref, dl_ref,
                 k_ref, s_ref, *, half_tiles):
    i = pl.program_id(0)
    b1, b2, b3, fr = vec_ref[0:1], vec_ref[1:2], vec_ref[2:3], vec_ref[3:4]

    def mlp3(z):
        h = jnp.sin(fr * (_mm_x3(z, w1_ref[...]) + b1))
        h = jnp.sin(fr * (_mm_x3(h, w2_ref[...]) + b2))
        h = jnp.sin(fr * (_mm_x3(h, w3_ref[...]) + b3))
        lane = lax.broadcasted_iota(jnp.int32, h.shape, 1)
        return jnp.where(lane < HY_ORDER, h, 0.0), jnp.where(lane >= HY_ORDER, h, 0.0)

    h_lo, h_hi = mlp3(z_ref[...])
    raw = jnp.concatenate([_mm_x3(h_lo, w4_ref[...]), _mm_x3(h_hi, w4_ref[...])], axis=0)
    raw = raw * jnp.exp(-t_ref[...] * dl_ref[...])

    @pl.when(i == 0)
    def _():
        s_ref[...] = jnp.zeros_like(s_ref)

    s_ref[...] += jnp.sum(jnp.abs(raw), axis=0, keepdims=True)
    hb0 = _mm_x3(mlp3(z0_ref[...])[0], w4b_ref[...])
    row = lax.broadcasted_iota(jnp.int32, raw.shape, 0)
    raw = raw + jnp.where((row == 0) & (i == 0), hb0[0:1], 0.0)
    k_ref[...] = jnp.where((row == 0) & (i == half_tiles), 0.0, raw)


def _hyena_kernel_taps(n, p):
    cw = p["f_w4"].shape[1] // 2
    f32 = lambda a: a.astype(F32)
    row = jnp.arange(2 * n, dtype=jnp.int32)
    pos = jnp.where(row < n, row, (2 * n - row) % n).astype(F32)[:, None]
    tt = pos * jnp.float32(1.0 / (n - 1))
    bands = (HY_EMB - 1) // 2
    ang = (2 * math.pi / n) * pos * jnp.linspace(1e-4, bands - 1, bands, dtype=F32)[None]
    zz = jnp.concatenate([tt, jnp.cos(ang), -jnp.sin(ang), jnp.zeros((2 * n, HY_ORDER - HY_EMB), F32)], axis=-1)
    w1 = jnp.concatenate([f32(p["f_w1"]), jnp.zeros((HY_ORDER - HY_EMB, HY_ORDER), F32)], axis=0)
    twice = lambda w: jnp.kron(jnp.eye(2, dtype=F32), f32(w))
    vecs = jnp.stack([jnp.tile(f32(p[k]), 2) for k in ("f_b1", "f_b2", "f_b3", "f_freq")]
                     + [jnp.zeros((2 * HY_ORDER,), F32)] * 4)
    deltas = jnp.abs(jnp.linspace(math.log(HY_TARGET) / HY_SLOW_PCT, math.log(HY_TARGET) / HY_FAST_PCT, cw,
                                  dtype=F32)).reshape(1, cw)
    tr = _pick(n, 512)
    half_tiles = n // tr
    pk = 2 * HY_ORDER
    zp = zz.reshape(-1, 2, tr // 2, HY_ORDER).transpose(0, 2, 1, 3).reshape(-1, pk)
    full = lambda shape: pl.BlockSpec(shape, lambda i: tuple(0 for _ in shape))
    w4 = jnp.concatenate([f32(p["f_w4"])] * 2, axis=0)
    return pl.pallas_call(
        functools.partial(_hyfilt_body, half_tiles=half_tiles),
        grid=(2 * half_tiles,),
        in_specs=[pl.BlockSpec((tr // 2, pk), lambda i: (i, 0)),
                  pl.BlockSpec((tr, 1), lambda i: (i, 0)),
                  full((8, pk)), full((pk, pk)), full((pk, pk)), full((pk, pk)),
                  pl.BlockSpec((pk, cw), lambda i: (0, i // half_tiles)),
                  pl.BlockSpec((pk, cw), lambda i: (0, 1)),
                  full((8, pk)), full((1, cw))],
        out_specs=[pl.BlockSpec((tr, cw), lambda i: (i, 0)), pl.BlockSpec((1, cw), lambda i: (0, 0))],
        out_shape=[jax.ShapeDtypeStruct((2 * n, cw), F32), jax.ShapeDtypeStruct((1, cw), F32)],
        compiler_params=_cp(("arbitrary",)),
        name="hyena_filter",
    )(zp, tt, jnp.broadcast_to(jnp.tile(zz[0:1], (1, 2)), (8, pk)), twice(w1), twice(p["f_w2"]), twice(p["f_w3"]),
      w4, w4, vecs, deltas)


def _dft_consts(n):
    nn = 2 * n
    n2 = HY_N2
    n1 = nn // n2
    n1h = n1 // 2 + 1
    nf = -(-n1h // 8) * 8
    live = (np.arange(nf) < n1h).astype(np.float64)
    a1 = 2 * np.pi * np.outer(np.arange(nf), np.arange(n1)) / n1
    f1 = np.concatenate([np.cos(a1), -np.sin(a1)], axis=0) * np.tile(live, 2)[:, None]
    wgt = live * np.where((np.arange(nf) == 0) | (np.arange(nf) == n1 // 2), 1.0, 2.0)
    f1inv = np.concatenate([np.cos(a1.T) * wgt, -np.sin(a1.T) * wgt], axis=1) / nn
    a2 = 2 * np.pi * np.outer(np.arange(n2), np.arange(n2)) / n2
    c2, s2 = np.cos(a2), -np.sin(a2)
    m2 = np.block([[c2, -s2], [s2, c2]])
    m2inv = np.block([[c2, s2], [-s2, c2]])
    at = 2 * np.pi * np.outer(np.arange(n2), np.arange(nf)) / nn
    tw = np.stack([np.cos(at), -np.sin(at)])
    c = lambda a: jnp.asarray(a.astype(np.float32))
    return dict(n1=n1, n2=n2, nf=nf, f1=c(f1), f1inv=c(f1inv), m2=c(m2), m2inv=c(m2inv),
                tw_s2=c(tw[:, :, :, None]),
                tw_f1=c(np.transpose(tw, (0, 2, 1))[:, :, :, None]))


HY_S2 = 8
HY_CTILE = 512


def _dft1_body(f_ref, x_ref, tw_ref, o_ref, *, nf):
    x = jnp.swapaxes(x_ref[0], 0, 1)
    fh, fl, _ = _split3(f_ref[...])
    d = lambda u, w: jnp.dot(u, w, preferred_element_type=F32)
    re, im = [], []
    for jj in range(HY_S2):
        xh, xl, _ = _split3(x[jj])
        a = d(fh, xh) + (d(fh, xl) + d(fl, xh))
        ar, ai = a[:nf], a[nf:]
        twr, twi = tw_ref[0, jj], tw_ref[1, jj]
        re.append(ar * twr - ai * twi)
        im.append(ar * twi + ai * twr)
    o_ref[0, 0] = jnp.swapaxes(jnp.stack(re), 0, 1)
    o_ref[0, 1] = jnp.swapaxes(jnp.stack(im), 0, 1)


def _dft_stage1(x, consts, rows):
    bsz, _, cw = x.shape
    n2, nf = consts["n2"], consts["nf"]
    xv = x.reshape(bsz, rows, n2, cw)
    f1 = consts["f1"][:, :rows]
    tc = _pick(cw, HY_CTILE, LANE)
    return pl.pallas_call(
        functools.partial(_dft1_body, nf=nf),
        grid=(bsz, n2 // HY_S2, cw // tc),
        in_specs=[pl.BlockSpec((2 * nf, rows), lambda b, j, c: (0, 0)),
                  pl.BlockSpec((1, rows, HY_S2, tc), lambda b, j, c: (b, 0, j, c)),
                  pl.BlockSpec((2, HY_S2, nf, 1), lambda b, j, c: (0, j, 0, 0))],
        out_specs=pl.BlockSpec((1, 2, nf, HY_S2, tc), lambda b, j, c: (b, 0, 0, j, c)),
        out_shape=jax.ShapeDtypeStruct((bsz, 2, nf, n2, cw), F32),
        compiler_params=_cp(("parallel", "parallel", "parallel")),
        name="hyena_dft1",
    )(f1, xv, consts["tw_s2"])


HY_F1 = 4


def _dftmid_body(a_ref, m2_ref, m2i_ref, h_ref, tw_ref, is_ref, o_ref, *, n2, filt):
    for r in range(HY_F1):
        a = a_ref[0, :, r].reshape(2 * n2, a_ref.shape[-1])
        x = _mm_x3(m2_ref[...], a)
        xr, xi = x[:n2], x[n2:]
        if filt:
            o_ref[0, 0, r] = xr * is_ref[...]
            o_ref[0, 1, r] = xi * is_ref[...]
            continue
        hr, hi = h_ref[0, 0, r], h_ref[0, 1, r]
        y = jnp.concatenate([xr * hr - xi * hi, xr * hi + xi * hr], axis=0)
        z = _mm_x3(m2i_ref[...], y)
        zr, zi = z[:n2], z[n2:]
        twr, twi = tw_ref[0, r], tw_ref[1, r]
        o_ref[0, 0, r] = zr * twr + zi * twi
        o_ref[0, 1, r] = zi * twr - zr * twi


def _dft_mid(a, h, inv_s, consts, filt):
    bsz = a.shape[0]
    n1, n2 = consts["nf"], consts["n2"]
    cw = a.shape[-1]
    blk = pl.BlockSpec((1, 2, HY_F1, n2, cw), lambda f, b: (b, 0, f, 0, 0))
    hblk = pl.BlockSpec((1, 2, HY_F1, n2, cw),
                        (lambda f, b: (0, 0, 0, 0, 0)) if filt else (lambda f, b: (0, 0, f, 0, 0)))
    return pl.pallas_call(
        functools.partial(_dftmid_body, n2=n2, filt=filt),
        grid=(n1 // HY_F1, bsz),
        in_specs=[blk,
                  pl.BlockSpec((2 * n2, 2 * n2), lambda f, b: (0, 0)),
                  pl.BlockSpec((2 * n2, 2 * n2), lambda f, b: (0, 0)),
                  hblk,
                  pl.BlockSpec((2, HY_F1, n2, 1), lambda f, b: (0, f, 0, 0)),
                  pl.BlockSpec((1, cw), lambda f, b: (0, 0))],
        out_specs=blk,
        out_shape=jax.ShapeDtypeStruct((bsz, 2, n1, n2, cw), F32),
        compiler_params=_cp(("parallel", "parallel")),
        name="hyena_dftmid_" + ("filter" if filt else "conv"),
    )(a, consts["m2"], consts["m2inv"], h, consts["tw_f1"], inv_s)


def _dft3_body(f_ref, z_ref, x0_ref, uv_ref, bias_ref, o_ref):
    zr, zi = jnp.swapaxes(z_ref[0, 0], 0, 1), jnp.swapaxes(z_ref[0, 1], 0, 1)
    x0, uv = jnp.swapaxes(x0_ref[0], 0, 1), jnp.swapaxes(uv_ref[0], 0, 1)
    fh, fl, _ = _split3(f_ref[...])
    d = lambda u, w: jnp.dot(u, w, preferred_element_type=F32)
    out = []
    for jj in range(HY_S2):
        zh, zl, _ = _split3(jnp.concatenate([zr[jj], zi[jj]], axis=0))
        y = d(fh, zh) + (d(fh, zl) + d(fl, zh))
        out.append(x0[jj] * (y + uv[jj] * bias_ref[...]))
    o_ref[0] = jnp.swapaxes(jnp.stack(out), 0, 1)


def _dft_stage1_inv(z, x0, uv, bias, consts):
    bsz, n, cw = x0.shape
    nf, n2 = consts["nf"], consts["n2"]
    hr = consts["n1"] // 2
    tokv = lambda t: t.reshape(bsz, hr, n2, cw)
    tc = _pick(cw, HY_CTILE, LANE)
    tile = pl.BlockSpec((1, hr, HY_S2, tc), lambda b, j, c: (b, 0, j, c))
    out = pl.pallas_call(
        _dft3_body,
        grid=(bsz, n2 // HY_S2, cw // tc),
        in_specs=[pl.BlockSpec((hr, 2 * nf), lambda b, j, c: (0, 0)),
                  pl.BlockSpec((1, 2, nf, HY_S2, tc), lambda b, j, c: (b, 0, 0, j, c)),
                  tile, tile,
                  pl.BlockSpec((1, tc), lambda b, j, c: (0, c))],
        out_specs=tile,
        out_shape=jax.ShapeDtypeStruct((bsz, hr, n2, cw), F32),
        compiler_params=_cp(("parallel", "parallel", "parallel")),
        name="hyena_dft3",
    )(consts["f1inv"][:hr], z, tokv(x0), tokv(uv), bias.reshape(1, cw).astype(F32))
    return out.reshape(bsz, n, cw)


def _hyena(p_lat, p):
    n = p_lat.shape[1]
    consts = _dft_consts(n)
    n1 = consts["n1"]
    x0, uv = _hyena_pre(p_lat, p["conv_w"], p["conv_b"])
    taps, sabs = _hyena_kernel_taps(n, p)
    cw = taps.shape[1]
    ones = jnp.ones((1, cw), F32)
    hk = _dft_stage1(taps[None], consts, n1)
    hspec = _dft_mid(hk, jnp.zeros((1, 2, HY_F1, HY_N2, cw), F32), 1.0 / sabs, consts, True)
    a = _dft_stage1(uv, consts, n1 // 2)
    z = _dft_mid(a, hspec, ones, consts, False)
    return _dft_stage1_inv(z, x0, uv, p["hy_bias"], consts)


def _mod_vectors(c, c_ctx, p):
    bsz, d = c.shape
    cvec = jnp.concatenate([c, c_ctx[None], jnp.zeros((8 - bsz - 1, d), c.dtype)], axis=0).astype(F32)
    m = _mods(cvec, p["w_mod"].astype(F32), p["b_mod"].astype(F32))
    lat = [v[:, None, :] for v in jnp.split(m[:bsz], N_MOD, axis=-1)]
    cx = [jnp.broadcast_to(v[:, None, :], (bsz, 1, d)) for v in jnp.split(m[bsz:bsz + 1], N_MOD, axis=-1)]
    return lat, cx


def _affine(norm_g, shift, scale):
    return norm_g.astype(F32) * (1.0 + scale), shift


def kernel(x, c, ctx, c_ctx,
           l0_w_mod, l0_b_mod, l0_norm1_g, l0_norm2_g, l0_w_in, l0_lam_q1, l0_lam_k1, l0_lam_q2, l0_lam_k2,
           l0_subln_g, l0_mu_prev, l0_mu_next, l0_w0_f, l0_w2_f, l0_a0_f, l0_a2_f, l0_w0_b, l0_w2_b, l0_a0_b,
           l0_a2_b, l0_g2, l0_k_k, l0_k_a, l0_r_k, l0_lnx_g, l0_lnx_b, l0_w_out, l0_mlp_w1, l0_mlp_w2,
           l1_w_mod, l1_b_mod, l1_norm1_g, l1_norm2_g, l1_w_in, l1_conv_w, l1_conv_b, l1_f_w1, l1_f_b1,
           l1_f_w2, l1_f_b2, l1_f_w3, l1_f_b3, l1_f_w4, l1_f_freq, l1_hy_bias, l1_q_norm_g, l1_k_norm_g,
           l1_w_out, l1_mlp_w1, l1_mlp_w2, final_g):
    bsz, n, d = x.shape
    x = x.astype(F32)
    ctx_s = ctx.astype(F32)
    bf = lambda w: w.astype(BF16)

    p0 = dict(w_mod=l0_w_mod, b_mod=l0_b_mod, mu_prev=l0_mu_prev, mu_next=l0_mu_next, w0_f=l0_w0_f, w2_f=l0_w2_f,
              a0_f=l0_a0_f, a2_f=l0_a2_f, w0_b=l0_w0_b, w2_b=l0_w2_b, a0_b=l0_a0_b, a2_b=l0_a2_b, g2=l0_g2,
              k_k=l0_k_k, k_a=l0_k_a, r_k=l0_r_k, lnx_g=l0_lnx_g, lnx_b=l0_lnx_b)
    (sh1, sc1, g1, sh2, sc2, g2), (csh1, csc1, cg1, csh2, csc2, cg2) = _mod_vectors(c, c_ctx, p0)
    aw = d // 2
    w_rw = l0_w_in[:, 3 * aw:]
    rw_cols = -(-(w_rw.shape[1] + 2 * LANE - B_GATE_RANK) // 512) * 512
    w_in0 = bf(jnp.concatenate([w_rw, jnp.zeros((d, rw_cols - w_rw.shape[1]), w_rw.dtype), l0_w_in[:, :3 * aw]],
                               axis=1))
    a1, b1 = _affine(l0_norm1_g, sh1, sc1)
    ca1, cb1 = _affine(l0_norm1_g, csh1, csc1)
    p_lat0 = _inproj(x, a1, b1, w_in0)
    p_ctx0 = _inproj(ctx_s, ca1, cb1, w_in0)
    lam_init = 0.8 - 0.6 * math.exp(-0.3 * 0)
    lam = (jnp.exp(jnp.sum(l0_lam_q1 * l0_lam_k1).astype(F32))
           - jnp.exp(jnp.sum(l0_lam_q2 * l0_lam_k2).astype(F32)) + lam_init)
    oa_lat, oa_ctx = _diff_attention(p_lat0, p_ctx0, rw_cols, aw, lam, lam_init, l0_subln_g)
    nh, hd = l0_r_k.shape
    zero = jnp.zeros((bsz, hd, nh * hd), F32)
    y_ctx, states_c = _rwkv_run(_rwkv_prepare(p_ctx0, p0, rw_cols), (zero, zero), p0)
    y_lat, _ = _rwkv_run(_rwkv_prepare(p_lat0, p0, rw_cols), states_c, p0)
    w_out0 = bf(l0_w_out)
    x = _outproj(oa_lat, y_lat, w_out0, x, g1)
    ctx_s = _outproj(oa_ctx, y_ctx, w_out0, ctx_s, cg1)
    w1, w2 = bf(l0_mlp_w1), bf(l0_mlp_w2)
    a2, b2 = _affine(l0_norm2_g, sh2, sc2)
    ca2, cb2 = _affine(l0_norm2_g, csh2, csc2)
    x = _mlp(x, a2, b2, g2, w1, w2)
    ctx_s = _mlp(ctx_s, ca2, cb2, cg2, w1, w2)

    p1 = dict(w_mod=l1_w_mod, b_mod=l1_b_mod, conv_w=l1_conv_w, conv_b=l1_conv_b, f_w1=l1_f_w1, f_b1=l1_f_b1,
              f_w2=l1_f_w2, f_b2=l1_f_b2, f_w3=l1_f_w3, f_b3=l1_f_b3, f_w4=l1_f_w4, f_freq=l1_f_freq,
              hy_bias=l1_hy_bias)
    (sh1, sc1, g1, sh2, sc2, g2), (csh1, csc1, _, _, _, _) = _mod_vectors(c, c_ctx, p1)
    hy_cols = l1_conv_w.shape[1]
    kv_cols = 2 * (2 * D_HEAD)
    w_in1 = bf(l1_w_in)
    a1, b1 = _affine(l1_norm1_g, sh1, sc1)
    ca1, cb1 = _affine(l1_norm1_g, csh1, csc1)
    p_lat = _inproj(x, a1, b1, w_in1)
    pkv_ctx = _inproj(ctx_s, ca1, cb1, w_in1[:, -kv_cols:])
    o_hy = _hyena(p_lat, p1)
    o_at = _gqa_attention(p_lat, pkv_ctx, hy_cols, l1_q_norm_g, l1_k_norm_g)
    x = _outproj(o_hy, o_at, bf(l1_w_out), x, g1)
    a2, b2 = _affine(l1_norm2_g, sh2, sc2)
    return _mlp(x, a2, b2, g2, bf(l1_mlp_w1), bf(l1_mlp_w2), final_g=final_g)
```

```python
import functools
import math

import numpy as np
import jax
import jax.numpy as jnp
from jax import lax
from jax.experimental import pallas as pl
from jax.experimental.pallas import tpu as pltpu

F32 = jnp.float32
BF16 = jnp.bfloat16
HI = lax.Precision.HIGHEST

NORM_EPS = 1e-6
ROPE_THETA = 10000.0
GRID_W = 64
N_MOD = 6
A_QK_DIM = 64
A_V_DIM = 128
B_HEAD = 64
B_DECAY_RANK = 64
B_ICL_RANK = 64
B_GATE_RANK = 160
B_GN_EPS = 64e-5
HY_EMB = 33
HY_ORDER = 64
HY_TARGET = 1e-2
HY_FAST_PCT = 0.3
HY_SLOW_PCT = 1.5
D_HEAD = 128
D_KV_GROUP = 4

LANE = 128
VMEM_LIMIT = 56 * 1024 * 1024
LOG2E = 1.4426950408889634


def _cp(sem, vmem=VMEM_LIMIT):
    return pltpu.CompilerParams(dimension_semantics=sem, vmem_limit_bytes=vmem)


def _pick(n, pref, step=8):
    t = max(step, min(n, pref) // step * step)
    while n % t:
        t -= step
    return t


def _mods_body(c_ref, w_ref, b_ref, o_ref):
    c = c_ref[...]
    s = c * jax.nn.sigmoid(c)
    o_ref[...] = _mm_x3(s, w_ref[...]) + b_ref[...]


def _mods(cvec, w_mod, b_mod):
    m, d = cvec.shape
    n = w_mod.shape[1]
    tn = _pick(n, 1024, LANE)
    return pl.pallas_call(
        _mods_body,
        grid=(n // tn,),
        in_specs=[pl.BlockSpec((m, d), lambda j: (0, 0)),
                  pl.BlockSpec((d, tn), lambda j: (0, j)),
                  pl.BlockSpec((1, tn), lambda j: (0, j))],
        out_specs=pl.BlockSpec((m, tn), lambda j: (0, j)),
        out_shape=jax.ShapeDtypeStruct((m, n), F32),
        compiler_params=_cp(("arbitrary",)),
        name="mods",
    )(cvec, w_mod, b_mod.reshape(1, n))


def _norm_mod(x, a, b):
    ms = jnp.mean(x * x, axis=-1, keepdims=True)
    return x * lax.rsqrt(ms + NORM_EPS) * a + b


def _inproj_body(x_ref, a_ref, b_ref, w_ref, o_ref, xn_ref):
    @pl.when(pl.program_id(2) == 0)
    def _():
        xn_ref[...] = _norm_mod(x_ref[0], a_ref[0], b_ref[0]).astype(BF16)

    o_ref[0] = jnp.dot(xn_ref[...], w_ref[...], preferred_element_type=F32).astype(o_ref.dtype)


def _inproj(x, a, b, w, out_dtype=BF16, tm_pref=1024, tn_pref=512):
    bsz, n, d = x.shape
    nn = w.shape[1]
    tm, tn = _pick(n, tm_pref), _pick(nn, tn_pref, LANE)
    return pl.pallas_call(
        _inproj_body,
        grid=(bsz, n // tm, nn // tn),
        in_specs=[pl.BlockSpec((1, tm, d), lambda bi, i, j: (bi, i, 0)),
                  pl.BlockSpec((1, 1, d), lambda bi, i, j: (bi, 0, 0)),
                  pl.BlockSpec((1, 1, d), lambda bi, i, j: (bi, 0, 0)),
                  pl.BlockSpec((d, tn), lambda bi, i, j: (0, j))],
        out_specs=pl.BlockSpec((1, tm, tn), lambda bi, i, j: (bi, i, j)),
        out_shape=jax.ShapeDtypeStruct((bsz, n, nn), out_dtype),
        scratch_shapes=[pltpu.VMEM((tm, d), BF16)],
        compiler_params=_cp(("parallel", "parallel", "arbitrary")),
        name="inproj",
    )(x, a, b, w)


def _outproj_body(oa_ref, ob_ref, wa_ref, wb_ref, x_ref, g_ref, y_ref):
    acc = jnp.dot(oa_ref[0].astype(BF16), wa_ref[...], preferred_element_type=F32)
    acc += jnp.dot(ob_ref[0].astype(BF16), wb_ref[...], preferred_element_type=F32)
    y_ref[0] = x_ref[0] + g_ref[0] * acc


def _outproj(oa, ob, w, x, g, tm_pref=1024, tn_pref=512):
    bsz, n, ka = oa.shape
    kb = ob.shape[2]
    d = w.shape[1]
    tm, tn = _pick(n, tm_pref), _pick(d, tn_pref, LANE)
    return pl.pallas_call(
        _outproj_body,
        grid=(bsz, n // tm, d // tn),
        in_specs=[pl.BlockSpec((1, tm, ka), lambda bi, i, j: (bi, i, 0)),
                  pl.BlockSpec((1, tm, kb), lambda bi, i, j: (bi, i, 0)),
                  pl.BlockSpec((ka, tn), lambda bi, i, j: (0, j)),
                  pl.BlockSpec((kb, tn), lambda bi, i, j: (0, j)),
                  pl.BlockSpec((1, tm, tn), lambda bi, i, j: (bi, i, j)),
                  pl.BlockSpec((1, 1, tn), lambda bi, i, j: (bi, 0, j))],
        out_specs=pl.BlockSpec((1, tm, tn), lambda bi, i, j: (bi, i, j)),
        out_shape=jax.ShapeDtypeStruct((bsz, n, d), F32),
        compiler_params=_cp(("parallel", "parallel", "parallel")),
        name="outproj",
    )(oa, ob, w[:ka], w[ka:], x, g)


def _mlp_body(x_ref, a_ref, b_ref, g_ref, w1_ref, w2_ref, fg_ref, y_ref, xn_ref, acc_ref, *, nf, final):
    f = pl.program_id(2)

    @pl.when(f == 0)
    def _():
        xn_ref[...] = _norm_mod(x_ref[0], a_ref[0], b_ref[0]).astype(BF16)
        acc_ref[...] = jnp.zeros_like(acc_ref)

    h = jnp.dot(xn_ref[...], w1_ref[...], preferred_element_type=F32)
    h = jnp.square(jnp.maximum(h, 0.0)).astype(BF16)
    acc_ref[...] += jnp.dot(h, w2_ref[...], preferred_element_type=F32)

    @pl.when(f == nf - 1)
    def _():
        y = x_ref[0] + g_ref[0] * acc_ref[...]
        if final:
            ms = jnp.mean(y * y, axis=-1, keepdims=True)
            y = y * lax.rsqrt(ms + NORM_EPS) * fg_ref[...]
        y_ref[0] = y


def _mlp(x, a, b, g, w1, w2, final_g=None, tm_pref=512, tf_pref=1024):
    bsz, n, d = x.shape
    dff = w1.shape[1]
    tm, tf = _pick(n, tm_pref), _pick(dff, tf_pref, LANE)
    nf = dff // tf
    final = final_g is not None
    fg = (final_g if final else jnp.ones((d,), F32)).reshape(1, d).astype(F32)
    return pl.pallas_call(
        functools.partial(_mlp_body, nf=nf, final=final),
        grid=(bsz, n // tm, nf),
        in_specs=[pl.BlockSpec((1, tm, d), lambda bi, i, f: (bi, i, 0)),
                  pl.BlockSpec((1, 1, d), lambda bi, i, f: (bi, 0, 0)),
                  pl.BlockSpec((1, 1, d), lambda bi, i, f: (bi, 0, 0)),
                  pl.BlockSpec((1, 1, d), lambda bi, i, f: (bi, 0, 0)),
                  pl.BlockSpec((d, tf), lambda bi, i, f: (0, f)),
                  pl.BlockSpec((tf, d), lambda bi, i, f: (f, 0)),
                  pl.BlockSpec((1, d), lambda bi, i, f: (0, 0))],
        out_specs=pl.BlockSpec((1, tm, d), lambda bi, i, f: (bi, i, 0)),
        out_shape=jax.ShapeDtypeStruct((bsz, n, d), F32),
        scratch_shapes=[pltpu.VMEM((tm, d), BF16), pltpu.VMEM((tm, d), F32)],
        compiler_params=_cp(("parallel", "parallel", "arbitrary")),
        name="mlp",
    )(x, a, b, g, w1, w2, fg)


def _rope_tables(n, dim, reps):
    rows = n // GRID_W
    row = jnp.repeat(jnp.arange(rows, dtype=F32), GRID_W)
    col = jnp.tile(jnp.arange(GRID_W, dtype=F32), rows)
    half = dim // 2
    inv = ROPE_THETA ** (-jnp.arange(0, half, 2, dtype=F32) / half)
    ar, ac = row[:, None] * inv, col[:, None] * inv
    cos = jnp.concatenate([jnp.cos(ar), jnp.cos(ar), jnp.cos(ac), jnp.cos(ac)], axis=-1)
    sin = jnp.concatenate([-jnp.sin(ar), jnp.sin(ar), -jnp.sin(ac), jnp.sin(ac)], axis=-1)
    return jnp.tile(cos, (1, reps)), jnp.tile(sin, (1, reps))


def _swap_matrix(dim, reps):
    q = dim // 4
    width = dim * reps
    p = np.zeros((width, width), np.float32)
    for j in range(width):
        base, r = (j // dim) * dim, j % dim
        axis, which, f = r // (2 * q), (r % (2 * q)) // q, r % q
        p[base + axis * 2 * q + (1 - which) * q + f, j] = 1.0
    return p


def _rope_norm(x, cos_ref, sin_ref, p_ref, g_ref, gs_ref, *, norm, rope, scale=1.0):
    y = x.astype(F32)
    if rope:
        ys = jnp.dot(x, p_ref[...], preferred_element_type=F32)
    if norm:
        rs = lax.rsqrt(jnp.mean(y * y, axis=-1, keepdims=True) + NORM_EPS)
        y = y * rs * g_ref[...]
        if rope:
            ys = ys * rs * gs_ref[...]
    if rope:
        y = y * cos_ref[...] + ys * sin_ref[...]
    return y * scale if scale != 1.0 else y


def _rope_operands(n, dim, gain, rope):
    reps = LANE // dim
    if rope:
        cos, sin = _rope_tables(n, dim, reps)
        pm = jnp.asarray(_swap_matrix(dim, reps), BF16)
    else:
        cos = sin = jnp.zeros((n, LANE), F32)
        pm = jnp.zeros((LANE, LANE), BF16)
    if gain is not None:
        g = jnp.tile(gain.astype(F32), reps).reshape(1, LANE)
        gs = jnp.dot(g, jnp.asarray(_swap_matrix(dim, reps)), precision=HI)
    else:
        g = gs = jnp.ones((1, LANE), F32)
    return cos, sin, pm, g, gs


def _kprep_body(x_ref, cos_ref, sin_ref, p_ref, g_ref, gs_ref, *rest, heads, norm, rope):
    o_ref = rest[-1]
    for h in range(heads):
        sl = slice(h * LANE, (h + 1) * LANE)
        y = _rope_norm(x_ref[0, :, sl], cos_ref, sin_ref, p_ref, g_ref, gs_ref, norm=norm, rope=rope)
        o_ref[0, :, sl] = y.astype(o_ref.dtype)


def _kprep(p, col0, heads, *, dim, gain=None, rope=True, tq_pref=512, rows=None, into=None, row0=0):
    bsz, n, _ = p.shape
    tq = _pick(n, tq_pref)
    hb = heads
    while col0 % (hb * LANE):
        hb //= 2
    w = hb * LANE
    cb = col0 // w
    cos, sin, pm, g, gs = _rope_operands(n, dim, gain, rope)
    assert row0 % tq == 0
    rb = row0 // tq
    out_rows = (rows or n) if into is None else into.shape[1]
    extra_specs, extra_args, aliases = [], [], {}
    if into is not None:
        extra_specs, extra_args, aliases = [pl.BlockSpec(memory_space=pl.ANY)], [into], {6: 0}
    return pl.pallas_call(
        functools.partial(_kprep_body, heads=hb, norm=gain is not None, rope=rope),
        grid=(bsz, n // tq, heads // hb),
        in_specs=[pl.BlockSpec((1, tq, w), lambda b, i, j: (b, i, cb + j)),
                  pl.BlockSpec((tq, LANE), lambda b, i, j: (i, 0)),
                  pl.BlockSpec((tq, LANE), lambda b, i, j: (i, 0)),
                  pl.BlockSpec((LANE, LANE), lambda b, i, j: (0, 0)),
                  pl.BlockSpec((1, LANE), lambda b, i, j: (0, 0)),
                  pl.BlockSpec((1, LANE), lambda b, i, j: (0, 0))] + extra_specs,
        out_specs=pl.BlockSpec((1, tq, w), lambda b, i, j: (b, rb + i, j)),
        out_shape=jax.ShapeDtypeStruct((bsz, out_rows, heads * LANE), BF16),
        input_output_aliases=aliases,
        compiler_params=_cp(("parallel", "parallel", "parallel")),
        name="kprep",
    )(p, cos, sin, pm, g, gs, *extra_args)


FLASH_KEYS = 1280
FLASH_HEADROOM = 64.0


def _knorm_body(k_ref, o_ref, *, heads):
    for h in range(heads):
        k = k_ref[0, :, h * LANE:(h + 1) * LANE].astype(F32)
        nrm2 = jnp.max(jnp.sum(k * k, axis=1, keepdims=True), axis=0, keepdims=True)
        o_ref[0, h] = jnp.broadcast_to(jnp.sqrt(nrm2), (8, LANE))


def _key_block_norms(k, kcol0, heads, ts):
    bsz, nk, _ = k.shape
    nsub = nk // ts
    hb = heads
    while kcol0 % (hb * LANE):
        hb //= 2
    kb = kcol0 // (hb * LANE)
    out = pl.pallas_call(
        functools.partial(_knorm_body, heads=hb),
        grid=(bsz, heads // hb, nsub),
        in_specs=[pl.BlockSpec((1, ts, hb * LANE), lambda b, j, c: (b, c, kb + j))],
        out_specs=pl.BlockSpec((1, hb, 8, LANE), lambda b, j, c: (b, j, c, 0)),
        out_shape=jax.ShapeDtypeStruct((bsz, heads, nsub * 8, LANE), F32),
        compiler_params=_cp(("parallel", "parallel", "parallel")),
        name="key_block_norms",
    )(k)
    return out[:, :, ::8, 0].reshape(-1)


def _flash_body(kmax_ref, qr_ref, cos_ref, sin_ref, pm_ref, g_ref, gs_ref, k_ref, v_ref, e1_ref, e2_ref, o_ref,
                q_ref, m_ref, l_ref, acc_ref, thr_ref, *, ts, nsub, mode, tq, norm, rope, scale):
    base = (pl.program_id(0) * pl.num_programs(1) + pl.program_id(1)) * nsub
    prep = functools.partial(_rope_norm, cos_ref=cos_ref, sin_ref=sin_ref, p_ref=pm_ref, g_ref=g_ref, gs_ref=gs_ref,
                             norm=norm, rope=rope, scale=scale)
    if mode == "diff":
        y = prep(qr_ref[0])
        lane = lax.broadcasted_iota(jnp.int32, y.shape, 1)
        q_ref[:tq] = jnp.where(lane < A_QK_DIM, y, 0.0).astype(BF16)
        q_ref[tq:] = jnp.where(lane >= A_QK_DIM, y, 0.0).astype(BF16)
    else:
        for j in range(D_KV_GROUP):
            q_ref[j * tq:(j + 1) * tq] = prep(qr_ref[0, :, j * LANE:(j + 1) * LANE]).astype(BF16)
    q = q_ref[...]

    def scores(off, size):
        k = k_ref[0, pl.ds(off, size), :]
        return lax.dot_general(q, k, (((1,), (1,)), ((), ())), preferred_element_type=F32)

    def block(c):
        off = pl.multiple_of(c * ts, ts)
        return scores(off, ts), v_ref[0, pl.ds(off, ts), :]

    def wide(m):
        return jnp.concatenate([m] * (ts // LANE), axis=1)

    m0 = jnp.broadcast_to(jnp.max(scores(0, 2 * LANE), axis=1, keepdims=True), m_ref.shape)
    m_ref[...] = m0
    l_ref[...] = jnp.zeros_like(l_ref)
    acc_ref[...] = jnp.zeros_like(acc_ref)
    qf = q.astype(F32)
    qn = jnp.sqrt(jnp.sum(qf * qf, axis=1, keepdims=True))
    thr_ref[0] = jnp.min((m0[:, :1] + FLASH_HEADROOM) / qn)

    def step(c, carry):
        fixed = kmax_ref[base + c] * 1.001 <= thr_ref[0]

        @pl.when(fixed)
        def _():
            s, v = block(c)
            p = jnp.exp2(s - wide(m_ref[...]))
            l_ref[...] += jnp.sum(p, axis=1, keepdims=True)
            acc_ref[...] += jnp.dot(p.astype(BF16), v, preferred_element_type=F32)

        @pl.when(jnp.logical_not(fixed))
        def _():
            s, v = block(c)
            m_prev = m_ref[...]
            m_new = jnp.maximum(m_prev, jnp.max(s, axis=1, keepdims=True))
            alpha = jnp.exp2(m_prev - m_new)
            p = jnp.exp2(s - wide(m_new))
            l_ref[...] = alpha * l_ref[...] + jnp.sum(p, axis=1, keepdims=True)
            acc_ref[...] = alpha * acc_ref[...] + jnp.dot(p.astype(BF16), v, preferred_element_type=F32)
            m_ref[...] = m_new

        return carry

    lax.fori_loop(0, nsub, step, 0)

    o = acc_ref[...] / l_ref[...]
    if mode == "diff":
        d = o[:tq] - e1_ref[...] * o[tq:]
        ms = jnp.mean(d * d, axis=-1, keepdims=True)
        o_ref[0] = (d * lax.rsqrt(ms + NORM_EPS) * e2_ref[...]).astype(o_ref.dtype)
    else:
        for j in range(D_KV_GROUP):
            o_ref[0, :, j * LANE:(j + 1) * LANE] = o[j * tq:(j + 1) * tq].astype(o_ref.dtype)


def _flash(pq, qcol0, k, v, kcol0, vcol0, e1, e2, *, mode, hk, dim, tq, gain=None, rope=True):
    bsz, n, _ = pq.shape
    nk = k.shape[1]
    g = 2 if mode == "diff" else D_KV_GROUP
    mq = g * tq
    qw = LANE if mode == "diff" else D_KV_GROUP * LANE
    qb = qcol0 // qw
    ts = _pick(nk, FLASH_KEYS, 2 * LANE)
    kb, vb = kcol0 // LANE, vcol0 // LANE
    ow = LANE if mode == "diff" else D_KV_GROUP * LANE
    kmax = _key_block_norms(k, kcol0, hk, ts)
    cos, sin, pm, gq, gqs = _rope_operands(n, dim, gain, rope)
    const = lambda shape: pl.BlockSpec(shape, lambda b, h, i: (0, 0))
    return pl.pallas_call(
        functools.partial(_flash_body, ts=ts, nsub=nk // ts, mode=mode, tq=tq, norm=gain is not None, rope=rope,
                          scale=dim ** -0.5 * LOG2E),
        grid=(bsz, hk, n // tq),
        in_specs=[pl.BlockSpec(memory_space=pltpu.SMEM),
                  pl.BlockSpec((1, tq, qw), lambda b, h, i: (b, i, qb + h)),
                  pl.BlockSpec((tq, LANE), lambda b, h, i: (i, 0)),
                  pl.BlockSpec((tq, LANE), lambda b, h, i: (i, 0)),
                  const((LANE, LANE)), const((1, LANE)), const((1, LANE)),
                  pl.BlockSpec((1, nk, LANE), lambda b, h, i: (b, 0, kb + h)),
                  pl.BlockSpec((1, nk, LANE), lambda b, h, i: (b, 0, vb + h)),
                  const((1, LANE)), const((1, LANE))],
        out_specs=pl.BlockSpec((1, tq, ow), lambda b, h, i: (b, i, h)),
        out_shape=jax.ShapeDtypeStruct((bsz, n, hk * ow), BF16),
        scratch_shapes=[pltpu.VMEM((mq, LANE), BF16), pltpu.VMEM((mq, LANE), F32), pltpu.VMEM((mq, LANE), F32),
                        pltpu.VMEM((mq, LANE), F32), pltpu.SMEM((1,), F32)],
        compiler_params=_cp(("parallel", "parallel", "arbitrary")),
        name="flash_" + mode,
    )(kmax, pq, cos, sin, pm, gq, gqs, k, v, e1, e2)


def _diff_attention(pa_lat, pa_ctx, c0, aw, lam, lam_init, subln_g):
    n, nctx = pa_lat.shape[1], pa_ctx.shape[1]
    heads = aw // LANE
    k_all = _kprep(pa_lat, c0 + aw, heads, dim=A_QK_DIM, rows=n + nctx)
    k_all = _kprep(pa_ctx, c0 + aw, heads, dim=A_QK_DIM, rope=False, into=k_all, row0=n)
    v_all = jnp.concatenate([pa_lat[..., c0 + 2 * aw:c0 + 3 * aw], pa_ctx[..., c0 + 2 * aw:c0 + 3 * aw]], axis=1)
    e1 = jnp.full((1, LANE), lam, F32)
    e2 = (subln_g.astype(F32) * (1.0 - lam_init)).reshape(1, LANE)
    o_lat = _flash(pa_lat, c0, k_all, v_all, 0, 0, e1, e2, mode="diff", hk=heads, dim=A_QK_DIM, tq=_pick(n, 1024))
    o_ctx = _flash(pa_ctx, c0, pa_ctx, pa_ctx, c0 + aw, c0 + 2 * aw, e1, e2, mode="diff", hk=heads, dim=A_QK_DIM,
                   tq=_pick(nctx, 512), rope=False)
    return o_lat, o_ctx


def _gqa_attention(p_lat, pkv_ctx, qcol0, q_norm_g, k_norm_g):
    n = p_lat.shape[1]
    hq, hk = 2 * D_KV_GROUP, 2
    kcol0 = qcol0 + hq * D_HEAD
    vcol0 = kcol0 + hk * D_HEAD
    k_all = _kprep(p_lat, kcol0, hk, dim=D_HEAD, gain=k_norm_g, rows=n + pkv_ctx.shape[1])
    k_all = _kprep(pkv_ctx, 0, hk, dim=D_HEAD, gain=k_norm_g, rope=False, into=k_all, row0=n)
    v_all = jnp.concatenate([p_lat[..., vcol0:vcol0 + hk * D_HEAD], pkv_ctx[..., hk * D_HEAD:]], axis=1)
    dummy = jnp.zeros((1, LANE), F32)
    return _flash(p_lat, qcol0, k_all, v_all, 0, 0, dummy, dummy, mode="gqa", hk=hk, dim=D_HEAD,
                  tq=_pick(n, 512), gain=q_norm_g)


RW_CHUNK = 64
RW_HALO = 8


def _mm_bf(a, b):
    return jnp.dot(a.astype(BF16), b.astype(BF16), preferred_element_type=F32)


def _split3(x):
    x1 = x.astype(BF16)
    r1 = x - x1.astype(F32)
    x2 = r1.astype(BF16)
    return x1, x2, (r1 - x2.astype(F32)).astype(BF16)


def _mm_x3(a, b):
    ah, al, _ = _split3(a)
    bh, bl, _ = _split3(b)
    d = lambda u, w: jnp.dot(u, w, preferred_element_type=F32)
    return d(ah, bh) + (d(ah, bl) + d(al, bh))


def _mm_sel(z, sel):
    z1, z2, _ = _split3(z)
    d = lambda u: jnp.dot(u, sel, preferred_element_type=F32)
    return d(z1) + d(z2)


def _rwprep_body(x_ref, hp_ref, hn_ref, mup_ref, mun_ref, vec_ref, w2f_ref, w2b_ref, a2f_ref, a2b_ref,
                 g2_ref, sel_ref, selt_ref,
                 r_ref, v_ref, kk_ref, lwf_ref, bf_ref, kf_ref, lwb_ref, bb_ref, kb_ref, g_ref, bonus_ref,
                 *, nb, tm, bw):
    i = pl.program_id(1)
    x = x_ref[0].astype(F32)
    row = lax.broadcasted_iota(jnp.int32, x.shape, 0)
    prev_edge = jnp.where(i > 0, hp_ref[0, RW_HALO - 1:RW_HALO, :].astype(F32), 0.0)
    next_edge = jnp.where(i < nb - 1, hn_ref[0, 0:1, :].astype(F32), 0.0)
    prev = jnp.where(row == 0, prev_edge, pltpu.roll(x, 1, 0))
    nxt = jnp.where(row == tm - 1, next_edge, pltpu.roll(x, tm - 1, 0))
    xs = x + mup_ref[...] * (prev - x) + mun_ref[...] * (nxt - x)
    r, k, v = xs[:, :bw], xs[:, bw:2 * bw], xs[:, 2 * bw:3 * bw]
    lr = xs[:, 3 * bw:3 * bw + LANE]
    gd = xs[:, 3 * bw + LANE:3 * bw + 3 * LANE]
    k_k, k_a, r_k = vec_ref[0:1], vec_ref[1:2], vec_ref[2:3]
    w0f, a0f, w0b, a0b = vec_ref[3:4], vec_ref[4:5], vec_ref[5:6], vec_ref[6:7]
    sel, selt = sel_ref[...], selt_ref[...]

    def head_sum(z):
        return _mm_sel(_mm_sel(z, sel), selt)

    kk = k * k_k
    kk = kk * lax.rsqrt(head_sum(kk * kk) + 1e-12)
    g_ref[0] = _mm_bf(jax.nn.sigmoid(gd), g2_ref[...]).astype(g_ref.dtype)
    th = jnp.tanh(lr)
    ksum = jnp.zeros_like(k)
    for (w0, a0, w2_ref, a2_ref, lw_ref, b_ref, kd_ref) in (
            (w0f, a0f, w2f_ref, a2f_ref, lwf_ref, bf_ref, kf_ref),
            (w0b, a0b, w2b_ref, a2b_ref, lwb_ref, bb_ref, kb_ref)):
        logw = -math.exp(-0.5) * jax.nn.sigmoid(w0 + _mm_bf(th, w2_ref[...]))
        a = jax.nn.sigmoid(a0 + _mm_bf(lr, a2_ref[...]))
        k_d = k * (1.0 + (a - 1.0) * k_a)
        ksum = ksum + k_d
        lw_ref[0] = logw
        b_ref[0] = kk * a
        kd_ref[0] = k_d
    r_ref[0] = r
    v_ref[0] = v
    kk_ref[0] = kk
    bonus_ref[0] = head_sum(r * ksum * r_k) * v


def _rwkv_prepare(pr, p, cols):
    bsz, n, _ = pr.shape
    bw = p["w0_f"].shape[0]
    tm = _pick(n, 256)
    nb = n // tm
    hb = tm // RW_HALO
    assert cols - 3 * bw >= 3 * LANE and B_DECAY_RANK + B_ICL_RANK == LANE

    def padded(vec):
        return jnp.concatenate([vec.astype(F32), jnp.zeros((cols - vec.shape[0],), F32)]).reshape(1, cols)

    vecs = jnp.stack([p["k_k"], p["k_a"], p["r_k"].reshape(-1), p["w0_f"], p["a0_f"], p["w0_b"], p["a0_b"],
                      jnp.zeros((bw,), F32)]).astype(F32)
    zr = jnp.zeros((B_DECAY_RANK, bw), F32)
    w2 = {d: jnp.concatenate([p["w2_" + d].astype(F32), zr], axis=0).astype(BF16) for d in "fb"}
    a2 = {d: jnp.concatenate([zr, p["a2_" + d].astype(F32)], axis=0).astype(BF16) for d in "fb"}
    g2 = jnp.concatenate([p["g2"].astype(F32), jnp.zeros((2 * LANE - B_GATE_RANK, bw), F32)], axis=0).astype(BF16)
    sel, selt = _head_selectors(bw)

    full = lambda shape: pl.BlockSpec(shape, lambda b, i: tuple(0 for _ in shape))
    tok = pl.BlockSpec((1, tm, bw), lambda b, i: (b, i, 0))
    tok_f32 = jax.ShapeDtypeStruct((bsz, n, bw), F32)
    outs = pl.pallas_call(
        functools.partial(_rwprep_body, nb=nb, tm=tm, bw=bw),
        grid=(bsz, nb),
        in_specs=[pl.BlockSpec((1, tm, cols), lambda b, i: (b, i, 0)),
                  pl.BlockSpec((1, RW_HALO, cols), lambda b, i: (b, jnp.maximum(i * hb - 1, 0), 0)),
                  pl.BlockSpec((1, RW_HALO, cols), lambda b, i: (b, jnp.minimum((i + 1) * hb, nb * hb - 1), 0)),
                  full((1, cols)), full((1, cols)), full((8, bw)),
                  full((LANE, bw)), full((LANE, bw)), full((LANE, bw)), full((LANE, bw)),
                  full((2 * LANE, bw)), full((bw, LANE)), full((LANE, bw))],
        out_specs=[tok] * 11,
        out_shape=[tok_f32] * 9 + [jax.ShapeDtypeStruct((bsz, n, bw), BF16), tok_f32],
        compiler_params=_cp(("parallel", "parallel")),
        name="rwkv_prepare",
    )(pr, pr, pr, padded(p["mu_prev"]), padded(p["mu_next"]), vecs, w2["f"], w2["b"], a2["f"], a2["b"],
      g2, sel, selt)
    r, v, kk, lwf, bf, kf, lwb, bb, kb, g, bonus = outs
    return dict(r=r, v=v, kk=kk, g=g, bonus=bonus, f=(lwf, bf, kf), b=(lwb, bb, kb))


RW_GROUP = 4


def _bd(x):
    t = jnp.concatenate([x.astype(BF16)] * RW_GROUP, axis=1)
    r = lax.broadcasted_iota(jnp.int32, t.shape, 1) // B_HEAD
    c = lax.broadcasted_iota(jnp.int32, t.shape, 2) // B_HEAD
    return jnp.where(r == c, t, jnp.zeros_like(t))


def _bdot(a, b, dims):
    return jnp.einsum(dims, a.astype(BF16), b.astype(BF16), preferred_element_type=F32)


def _groups(x):
    gw = RW_GROUP * B_HEAD
    return jnp.stack([x[:, g * gw:(g + 1) * gw] for g in range(x.shape[1] // gw)])


def _rw_decays(r, kk, b, kd, lw, reverse):
    ck = r.shape[0]
    ti = lax.broadcasted_iota(jnp.int32, (ck, ck), 0)
    si = lax.broadcasted_iota(jnp.int32, (ck, ck), 1)
    tri = ((si >= ti) if reverse else (si <= ti)).astype(BF16)
    l1, l2, l3 = _split3(lw)
    tdot = lambda z: jnp.dot(tri, z, preferred_element_type=F32)
    cum = tdot(l1) + (tdot(l2) + tdot(l3))
    total = cum[0:1] if reverse else cum[ck - 1:ck]
    winv = jnp.exp(-cum)
    wrest = jnp.exp(total - cum)
    return (-kk * jnp.exp(cum - lw), b * winv, kd * winv, r * jnp.exp(cum), b * wrest, kd * wrest, jnp.exp(total))


def _rw_chunks(fwd, bwd, s_f, s_b):
    nseq = len(fwd)
    ck, w = fwd[0][0].shape
    gw = RW_GROUP * B_HEAD
    ng = w // gw
    parts = []
    for probs, rev in ((fwd, False), (bwd, True)):
        for (r, v, kk, b, kd, lw) in probs:
            parts.append([_groups(z) for z in _rw_decays(r, kk, b, kd, lw, rev)] + [_groups(v)])
    at, bt, kt, rt, btw, ktw, decay, v = [jnp.concatenate(zs, axis=0) for zs in zip(*parts)]
    s = jnp.concatenate([_groups(z) for z in list(s_f) + list(s_b)], axis=0)
    nslab = 2 * nseq * ng
    shape = (nslab, ck, gw)
    lag = lax.broadcasted_iota(jnp.int32, shape, 1) - lax.broadcasted_iota(jnp.int32, shape, 2) % ck
    lag = jnp.where(lax.broadcasted_iota(jnp.int32, shape, 0) >= nseq * ng, -lag, lag)
    incl, strict = lag >= 0, lag > 0
    eye = (lag == 0).astype(F32)

    bdv = _bd(v)
    lhs = jnp.concatenate([at, rt], axis=1)
    lb, lk = _bdot(lhs, _bd(bt), "gtk,gnk->gtn"), _bdot(lhs, _bd(kt), "gtk,gnk->gtn")
    lab, mrb = jnp.where(strict, lb[:, :ck], 0.0), jnp.where(incl, lb[:, ck:], 0.0)
    lak, mrk = jnp.where(strict, lk[:, :ck], 0.0), jnp.where(incl, lk[:, ck:], 0.0)
    tinv = eye + lab
    pw = _bdot(lab, _bd(lab), "gtk,gkn->gtn")
    span = 4
    while span < ck:
        both = _bdot(jnp.concatenate([tinv, pw], axis=1), _bd(pw), "gtk,gkn->gtn")
        tinv, pw = tinv + both[:, :ck], both[:, ck:]
        span *= 2
    tinv = tinv + _bdot(tinv, _bd(pw), "gtk,gkn->gtn")
    vmix = _bdot(jnp.concatenate([lak, mrk], axis=1), bdv, "gtk,gkn->gtn")
    p1 = _bdot(tinv, _bd(at), "gtk,gkn->gtn")
    q1 = _bdot(tinv, _bd(vmix[:, :ck]), "gtk,gkn->gtn")
    smix = _bdot(jnp.concatenate([p1, rt], axis=1), _bd(s), "gtk,gnk->gtn")
    u = smix[:, :ck] + q1
    y = smix[:, ck:] + _bdot(mrb, _bd(u), "gtk,gkn->gtn") + vmix[:, ck:]
    z = _bdot(jnp.concatenate([u, v], axis=1), jnp.concatenate([btw, ktw], axis=1), "gtm,gtn->gmn")
    lane_head = lax.broadcasted_iota(jnp.int32, (nslab, B_HEAD, gw), 2) // B_HEAD
    upd = jnp.zeros_like(s)
    for h in range(RW_GROUP):
        upd = upd + jnp.where(lane_head == h, z[:, h * B_HEAD:(h + 1) * B_HEAD], 0.0)
    s_new = s * decay + upd
    lanes = lambda x, q: jnp.concatenate([x[q * ng + g] for g in range(ng)], axis=1)
    seqs = range(nseq)
    return ([lanes(y, q) for q in seqs], [lanes(y, nseq + q) for q in seqs],
            [lanes(s_new, q) for q in seqs], [lanes(s_new, nseq + q) for q in seqs])


def _rwscan_body(rf_ref, vf_ref, kkf_ref, bf_ref, kdf_ref, lwf_ref, rb_ref, vb_ref, kkb_ref, bb_ref, kdb_ref, lwb_ref,
                 s0f_ref, s0b_ref, yf_ref, yb_ref, sTf_ref, sTb_ref, sf_ref, sb_ref, *, nc, nseq):
    c = pl.program_id(0)

    @pl.when(c == 0)
    def _():
        sf_ref[...] = s0f_ref[...]
        sb_ref[...] = s0b_ref[...]

    seqs = range(nseq)
    yf, yb, sf, sb = _rw_chunks(
        [(rf_ref[q], vf_ref[q], kkf_ref[q], bf_ref[q], kdf_ref[q], lwf_ref[q]) for q in seqs],
        [(rb_ref[q], vb_ref[q], kkb_ref[q], bb_ref[q], kdb_ref[q], lwb_ref[q]) for q in seqs],
        [sf_ref[q] for q in seqs], [sb_ref[q] for q in seqs])
    for q in seqs:
        yf_ref[q] = yf[q]
        yb_ref[q] = yb[q]
        sf_ref[q] = sf[q]
        sb_ref[q] = sb[q]

    @pl.when(c == nc - 1)
    def _():
        for q in seqs:
            sTf_ref[q] = sf[q]
            sTb_ref[q] = sb[q]


def _rwkv_scan(r, v, kk, fwd, bwd, s0f, s0b):
    bsz, n, w = r.shape
    ck = RW_CHUNK
    assert n % ck == 0 and ck == B_HEAD and w % (RW_GROUP * B_HEAD) == 0
    nc = n // ck
    seq_f = pl.BlockSpec((bsz, ck, w), lambda c: (0, c, 0))
    seq_b = pl.BlockSpec((bsz, ck, w), lambda c: (0, nc - 1 - c, 0))
    st = pl.BlockSpec((bsz, B_HEAD, w), lambda c: (0, 0, 0))
    y_shape = jax.ShapeDtypeStruct((bsz, n, w), F32)
    s_shape = jax.ShapeDtypeStruct((bsz, B_HEAD, w), F32)
    (lwf, bf, kf), (lwb, bb, kb) = fwd, bwd
    return pl.pallas_call(
        functools.partial(_rwscan_body, nc=nc, nseq=bsz),
        grid=(nc,),
        in_specs=[seq_f] * 6 + [seq_b] * 6 + [st, st],
        out_specs=[seq_f, seq_b, st, st],
        out_shape=[y_shape, y_shape, s_shape, s_shape],
        scratch_shapes=[pltpu.VMEM((bsz, B_HEAD, w), F32), pltpu.VMEM((bsz, B_HEAD, w), F32)],
        compiler_params=_cp(("arbitrary",)),
        name="rwkv_scan",
    )(r, v, kk, bf, kf, lwf, r, v, kk, bb, kb, lwb, s0f, s0b)


def _rwpost_body(yf_ref, yb_ref, g_ref, bonus_ref, lg_ref, lb_ref, sel_ref, selt_ref, o_ref):
    sel, selt = sel_ref[...], selt_ref[...]
    head_mean = lambda z: _mm_sel(_mm_sel(z, sel), selt) * (1.0 / B_HEAD)
    y = yf_ref[0] + yb_ref[0]
    d = y - head_mean(y)
    y = d * lax.rsqrt(head_mean(d * d) + B_GN_EPS)
    y = y * lg_ref[...] + lb_ref[...] + bonus_ref[0]
    o_ref[0] = (y * g_ref[0].astype(F32)).astype(o_ref.dtype)


def _head_selectors(bw):
    sel_np = np.zeros((bw, LANE), np.float32)
    sel_np[np.arange(bw), np.arange(bw) // B_HEAD] = 1.0
    return jnp.asarray(sel_np, BF16), jnp.asarray(sel_np.T, BF16)


def _rwkv_post(y_f, y_b, g, bonus, lnx_g, lnx_b):
    bsz, n, bw = y_f.shape
    tm = _pick(n, 256)
    tok = pl.BlockSpec((1, tm, bw), lambda b, i: (b, i, 0))
    vec = pl.BlockSpec((1, bw), lambda b, i: (0, 0))
    sel, selt = _head_selectors(bw)
    return pl.pallas_call(
        _rwpost_body,
        grid=(bsz, n // tm),
        in_specs=[tok, tok, tok, tok, vec, vec,
                  pl.BlockSpec((bw, LANE), lambda b, i: (0, 0)), pl.BlockSpec((LANE, bw), lambda b, i: (0, 0))],
        out_specs=tok,
        out_shape=jax.ShapeDtypeStruct((bsz, n, bw), BF16),
        compiler_params=_cp(("parallel", "parallel")),
        name="rwkv_post",
    )(y_f, y_b, g, bonus, lnx_g.reshape(1, bw).astype(F32), lnx_b.reshape(1, bw).astype(F32), sel, selt)


def _rwkv_run(prep, states0, p):
    y_f, y_b, s_f, s_b = _rwkv_scan(prep["r"], prep["v"], prep["kk"], prep["f"], prep["b"], states0[0], states0[1])
    return _rwkv_post(y_f, y_b, prep["g"], prep["bonus"], p["lnx_g"], p["lnx_b"]), (s_f, s_b)


HY_N2 = 128
HY_HALO = 8


def _hypre_body(x_ref, hp_ref, hn_ref, cw_ref, cb_ref, x0_ref, uv_ref, *, nb, tm, cw):
    i = pl.program_id(1)
    x = x_ref[0].astype(F32)
    row = lax.broadcasted_iota(jnp.int32, x.shape, 0)
    prev_edge = jnp.where(i > 0, hp_ref[0, HY_HALO - 1:HY_HALO, :].astype(F32), 0.0)
    next_edge = jnp.where(i < nb - 1, hn_ref[0, 0:1, :].astype(F32), 0.0)
    prev = jnp.where(row == 0, prev_edge, pltpu.roll(x, 1, 0))
    nxt = jnp.where(row == tm - 1, next_edge, pltpu.roll(x, tm - 1, 0))
    u = prev * cw_ref[0:1] + x * cw_ref[1:2] + nxt * cw_ref[2:3] + cb_ref[...]
    x0_ref[0] = u[:, :cw]
    uv_ref[0] = u[:, 2 * cw:3 * cw] * u[:, cw:2 * cw]


def _hyena_pre(p, conv_w, conv_b):
    bsz, n, _ = p.shape
    hc = conv_w.shape[1]
    cw = hc // 3
    tm = _pick(n, 256)
    nb = n // tm
    hb = tm // HY_HALO
    cwp = jnp.concatenate([conv_w.astype(F32), jnp.zeros((5, hc), F32)], axis=0)
    out = pl.BlockSpec((1, tm, cw), lambda b, i: (b, i, 0))
    return pl.pallas_call(
        functools.partial(_hypre_body, nb=nb, tm=tm, cw=cw),
        grid=(bsz, nb),
        in_specs=[pl.BlockSpec((1, tm, hc), lambda b, i: (b, i, 0)),
                  pl.BlockSpec((1, HY_HALO, hc), lambda b, i: (b, jnp.maximum(i * hb - 1, 0), 0)),
                  pl.BlockSpec((1, HY_HALO, hc), lambda b, i: (b, jnp.minimum((i + 1) * hb, nb * hb - 1), 0)),
                  pl.BlockSpec((8, hc), lambda b, i: (0, 0)),
                  pl.BlockSpec((1, hc), lambda b, i: (0, 0))],
        out_specs=[out, out],
        out_shape=[jax.ShapeDtypeStruct((bsz, n, cw), F32)] * 2,
        compiler_params=_cp(("parallel", "parallel")),
        name="hyena_pre",
    )(p, p, p, cwp, conv_b.reshape(1, hc).astype(F32))


def _hyfilt_body(z_ref, t_ref, z0_ref, w1_ref, w2_ref, w3_ref, w4_ref, w4b_ref, vec_ref, dl_ref,
                 k_ref, s_ref, *, half_tiles):
    i = pl.program_id(0)
    b1, b2, b3, fr = vec_ref[0:1], vec_ref[1:2], vec_ref[2:3], vec_ref[3:4]

    def mlp3(z):
        h = jnp.sin(fr * (_mm_x3(z, w1_ref[...]) + b1))
        h = jnp.sin(fr * (_mm_x3(h, w2_ref[...]) + b2))
        h = jnp.sin(fr * (_mm_x3(h, w3_ref[...]) + b3))
        lane = lax.broadcasted_iota(jnp.int32, h.shape, 1)
        return jnp.where(lane < HY_ORDER, h, 0.0), jnp.where(lane >= HY_ORDER, h, 0.0)

    h_lo, h_hi = mlp3(z_ref[...])
    raw = jnp.concatenate([_mm_x3(h_lo, w4_ref[...]), _mm_x3(h_hi, w4_ref[...])], axis=0)
    raw = raw * jnp.exp(-t_ref[...] * dl_ref[...])

    @pl.when(i == 0)
    def _():
        s_ref[...] = jnp.zeros_like(s_ref)

    s_ref[...] += jnp.sum(jnp.abs(raw), axis=0, keepdims=True)
    hb0 = _mm_x3(mlp3(z0_ref[...])[0], w4b_ref[...])
    row = lax.broadcasted_iota(jnp.int32, raw.shape, 0)
    raw = raw + jnp.where((row == 0) & (i == 0), hb0[0:1], 0.0)
    k_ref[...] = jnp.where((row == 0) & (i == half_tiles), 0.0, raw)


def _hyena_kernel_taps(n, p):
    cw = p["f_w4"].shape[1] // 2
    f32 = lambda a: a.astype(F32)
    row = jnp.arange(2 * n, dtype=jnp.int32)
    pos = jnp.where(row < n, row, (2 * n - row) % n).astype(F32)[:, None]
    tt = pos * jnp.float32(1.0 / (n - 1))
    bands = (HY_EMB - 1) // 2
    ang = (2 * math.pi / n) * pos * jnp.linspace(1e-4, bands - 1, bands, dtype=F32)[None]
    zz = jnp.concatenate([tt, jnp.cos(ang), -jnp.sin(ang), jnp.zeros((2 * n, HY_ORDER - HY_EMB), F32)], axis=-1)
    w1 = jnp.concatenate([f32(p["f_w1"]), jnp.zeros((HY_ORDER - HY_EMB, HY_ORDER), F32)], axis=0)
    twice = lambda w: jnp.kron(jnp.eye(2, dtype=F32), f32(w))
    vecs = jnp.stack([jnp.tile(f32(p[k]), 2) for k in ("f_b1", "f_b2", "f_b3", "f_freq")]
                     + [jnp.zeros((2 * HY_ORDER,), F32)] * 4)
    deltas = jnp.abs(jnp.linspace(math.log(HY_TARGET) / HY_SLOW_PCT, math.log(HY_TARGET) / HY_FAST_PCT, cw,
                                  dtype=F32)).reshape(1, cw)
    tr = _pick(n, 512)
    half_tiles = n // tr
    pk = 2 * HY_ORDER
    zp = zz.reshape(-1, 2, tr // 2, HY_ORDER).transpose(0, 2, 1, 3).reshape(-1, pk)
    full = lambda shape: pl.BlockSpec(shape, lambda i: tuple(0 for _ in shape))
    w4 = jnp.concatenate([f32(p["f_w4"])] * 2, axis=0)
    return pl.pallas_call(
        functools.partial(_hyfilt_body, half_tiles=half_tiles),
        grid=(2 * half_tiles,),
        in_specs=[pl.BlockSpec((tr // 2, pk), lambda i: (i, 0)),
                  pl.BlockSpec((tr, 1), lambda i: (i, 0)),
                  full((8, pk)), full((pk, pk)), full((pk, pk)), full((pk, pk)),
                  pl.BlockSpec((pk, cw), lambda i: (0, i // half_tiles)),
                  pl.BlockSpec((pk, cw), lambda i: (0, 1)),
                  full((8, pk)), full((1, cw))],
        out_specs=[pl.BlockSpec((tr, cw), lambda i: (i, 0)), pl.BlockSpec((1, cw), lambda i: (0, 0))],
        out_shape=[jax.ShapeDtypeStruct((2 * n, cw), F32), jax.ShapeDtypeStruct((1, cw), F32)],
        compiler_params=_cp(("arbitrary",)),
        name="hyena_filter",
    )(zp, tt, jnp.broadcast_to(jnp.tile(zz[0:1], (1, 2)), (8, pk)), twice(w1), twice(p["f_w2"]), twice(p["f_w3"]),
      w4, w4, vecs, deltas)


def _dft_consts(n):
    nn = 2 * n
    n2 = HY_N2
    n1 = nn // n2
    n1h = n1 // 2 + 1
    nf = -(-n1h // 8) * 8
    live = (np.arange(nf) < n1h).astype(np.float64)
    a1 = 2 * np.pi * np.outer(np.arange(nf), np.arange(n1)) / n1
    f1 = np.concatenate([np.cos(a1), -np.sin(a1)], axis=0) * np.tile(live, 2)[:, None]
    wgt = live * np.where((np.arange(nf) == 0) | (np.arange(nf) == n1 // 2), 1.0, 2.0)
    f1inv = np.concatenate([np.cos(a1.T) * wgt, -np.sin(a1.T) * wgt], axis=1) / nn
    a2 = 2 * np.pi * np.outer(np.arange(n2), np.arange(n2)) / n2
    c2, s2 = np.cos(a2), -np.sin(a2)
    m2 = np.block([[c2, -s2], [s2, c2]])
    m2inv = np.block([[c2, s2], [-s2, c2]])
    at = 2 * np.pi * np.outer(np.arange(n2), np.arange(nf)) / nn
    tw = np.stack([np.cos(at), -np.sin(at)])
    c = lambda a: jnp.asarray(a.astype(np.float32))
    return dict(n1=n1, n2=n2, nf=nf, f1=c(f1), f1inv=c(f1inv), m2=c(m2), m2inv=c(m2inv),
                tw_s2=c(tw[:, :, :, None]),
                tw_f1=c(np.transpose(tw, (0, 2, 1))[:, :, :, None]))


HY_S2 = 8
HY_CTILE = 512


def _dft1_body(f_ref, x_ref, tw_ref, o_ref, *, nf):
    x = jnp.swapaxes(x_ref[0], 0, 1)
    fh, fl, _ = _split3(f_ref[...])
    d = lambda u, w: jnp.dot(u, w, preferred_element_type=F32)
    re, im = [], []
    for jj in range(HY_S2):
        xh, xl, _ = _split3(x[jj])
        a = d(fh, xh) + (d(fh, xl) + d(fl, xh))
        ar, ai = a[:nf], a[nf:]
        twr, twi = tw_ref[0, jj], tw_ref[1, jj]
        re.append(ar * twr - ai * twi)
        im.append(ar * twi + ai * twr)
    o_ref[0, 0] = jnp.swapaxes(jnp.stack(re), 0, 1)
    o_ref[0, 1] = jnp.swapaxes(jnp.stack(im), 0, 1)


def _dft_stage1(x, consts, rows):
    bsz, _, cw = x.shape
    n2, nf = consts["n2"], consts["nf"]
    xv = x.reshape(bsz, rows, n2, cw)
    f1 = consts["f1"][:, :rows]
    tc = _pick(cw, HY_CTILE, LANE)
    return pl.pallas_call(
        functools.partial(_dft1_body, nf=nf),
        grid=(bsz, n2 // HY_S2, cw // tc),
        in_specs=[pl.BlockSpec((2 * nf, rows), lambda b, j, c: (0, 0)),
                  pl.BlockSpec((1, rows, HY_S2, tc), lambda b, j, c: (b, 0, j, c)),
                  pl.BlockSpec((2, HY_S2, nf, 1), lambda b, j, c: (0, j, 0, 0))],
        out_specs=pl.BlockSpec((1, 2, nf, HY_S2, tc), lambda b, j, c: (b, 0, 0, j, c)),
        out_shape=jax.ShapeDtypeStruct((bsz, 2, nf, n2, cw), F32),
        compiler_params=_cp(("parallel", "parallel", "parallel")),
        name="hyena_dft1",
    )(f1, xv, consts["tw_s2"])


HY_F1 = 4


def _dftmid_body(a_ref, m2_ref, m2i_ref, h_ref, tw_ref, is_ref, o_ref, *, n2, filt):
    for r in range(HY_F1):
        a = a_ref[0, :, r].reshape(2 * n2, a_ref.shape[-1])
        x = _mm_x3(m2_ref[...], a)
        xr, xi = x[:n2], x[n2:]
        if filt:
            o_ref[0, 0, r] = xr * is_ref[...]
            o_ref[0, 1, r] = xi * is_ref[...]
            continue
        hr, hi = h_ref[0, 0, r], h_ref[0, 1, r]
        y = jnp.concatenate([xr * hr - xi * hi, xr * hi + xi * hr], axis=0)
        z = _mm_x3(m2i_ref[...], y)
        zr, zi = z[:n2], z[n2:]
        twr, twi = tw_ref[0, r], tw_ref[1, r]
        o_ref[0, 0, r] = zr * twr + zi * twi
        o_ref[0, 1, r] = zi * twr - zr * twi


def _dft_mid(a, h, inv_s, consts, filt):
    bsz = a.shape[0]
    n1, n2 = consts["nf"], consts["n2"]
    cw = a.shape[-1]
    blk = pl.BlockSpec((1, 2, HY_F1, n2, cw), lambda f, b: (b, 0, f, 0, 0))
    hblk = pl.BlockSpec((1, 2, HY_F1, n2, cw),
                        (lambda f, b: (0, 0, 0, 0, 0)) if filt else (lambda f, b: (0, 0, f, 0, 0)))
    return pl.pallas_call(
        functools.partial(_dftmid_body, n2=n2, filt=filt),
        grid=(n1 // HY_F1, bsz),
        in_specs=[blk,
                  pl.BlockSpec((2 * n2, 2 * n2), lambda f, b: (0, 0)),
                  pl.BlockSpec((2 * n2, 2 * n2), lambda f, b: (0, 0)),
                  hblk,
                  pl.BlockSpec((2, HY_F1, n2, 1), lambda f, b: (0, f, 0, 0)),
                  pl.BlockSpec((1, cw), lambda f, b: (0, 0))],
        out_specs=blk,
        out_shape=jax.ShapeDtypeStruct((bsz, 2, n1, n2, cw), F32),
        compiler_params=_cp(("parallel", "parallel")),
        name="hyena_dftmid_" + ("filter" if filt else "conv"),
    )(a, consts["m2"], consts["m2inv"], h, consts["tw_f1"], inv_s)


def _dft3_body(f_ref, z_ref, x0_ref, uv_ref, bias_ref, o_ref):
    zr, zi = jnp.swapaxes(z_ref[0, 0], 0, 1), jnp.swapaxes(z_ref[0, 1], 0, 1)
    x0, uv = jnp.swapaxes(x0_ref[0], 0, 1), jnp.swapaxes(uv_ref[0], 0, 1)
    fh, fl, _ = _split3(f_ref[...])
    d = lambda u, w: jnp.dot(u, w, preferred_element_type=F32)
    out = []
    for jj in range(HY_S2):
        zh, zl, _ = _split3(jnp.concatenate([zr[jj], zi[jj]], axis=0))
        y = d(fh, zh) + (d(fh, zl) + d(fl, zh))
        out.append(x0[jj] * (y + uv[jj] * bias_ref[...]))
    o_ref[0] = jnp.swapaxes(jnp.stack(out), 0, 1)


def _dft_stage1_inv(z, x0, uv, bias, consts):
    bsz, n, cw = x0.shape
    nf, n2 = consts["nf"], consts["n2"]
    hr = consts["n1"] // 2
    tokv = lambda t: t.reshape(bsz, hr, n2, cw)
    tc = _pick(cw, HY_CTILE, LANE)
    tile = pl.BlockSpec((1, hr, HY_S2, tc), lambda b, j, c: (b, 0, j, c))
    out = pl.pallas_call(
        _dft3_body,
        grid=(bsz, n2 // HY_S2, cw // tc),
        in_specs=[pl.BlockSpec((hr, 2 * nf), lambda b, j, c: (0, 0)),
                  pl.BlockSpec((1, 2, nf, HY_S2, tc), lambda b, j, c: (b, 0, 0, j, c)),
                  tile, tile,
                  pl.BlockSpec((1, tc), lambda b, j, c: (0, c))],
        out_specs=tile,
        out_shape=jax.ShapeDtypeStruct((bsz, hr, n2, cw), F32),
        compiler_params=_cp(("parallel", "parallel", "parallel")),
        name="hyena_dft3",
    )(consts["f1inv"][:hr], z, tokv(x0), tokv(uv), bias.reshape(1, cw).astype(F32))
    return out.reshape(bsz, n, cw)


def _hyena(p_lat, p):
    n = p_lat.shape[1]
    consts = _dft_consts(n)
    n1 = consts["n1"]
    x0, uv = _hyena_pre(p_lat, p["conv_w"], p["conv_b"])
    taps, sabs = _hyena_kernel_taps(n, p)
    cw = taps.shape[1]
    ones = jnp.ones((1, cw), F32)
    hk = _dft_stage1(taps[None], consts, n1)
    hspec = _dft_mid(hk, jnp.zeros((1, 2, HY_F1, HY_N2, cw), F32), 1.0 / sabs, consts, True)
    a = _dft_stage1(uv, consts, n1 // 2)
    z = _dft_mid(a, hspec, ones, consts, False)
    return _dft_stage1_inv(z, x0, uv, p["hy_bias"], consts)


def _mod_vectors(c, c_ctx, p):
    bsz, d = c.shape
    cvec = jnp.concatenate([c, c_ctx[None], jnp.zeros((8 - bsz - 1, d), c.dtype)], axis=0).astype(F32)
    m = _mods(cvec, p["w_mod"].astype(F32), p["b_mod"].astype(F32))
    lat = [v[:, None, :] for v in jnp.split(m[:bsz], N_MOD, axis=-1)]
    cx = [jnp.broadcast_to(v[:, None, :], (bsz, 1, d)) for v in jnp.split(m[bsz:bsz + 1], N_MOD, axis=-1)]
    return lat, cx


def _affine(norm_g, shift, scale):
    return norm_g.astype(F32) * (1.0 + scale), shift


def kernel(x, c, ctx, c_ctx,
           l0_w_mod, l0_b_mod, l0_norm1_g, l0_norm2_g, l0_w_in, l0_lam_q1, l0_lam_k1, l0_lam_q2, l0_lam_k2,
           l0_subln_g, l0_mu_prev, l0_mu_next, l0_w0_f, l0_w2_f, l0_a0_f, l0_a2_f, l0_w0_b, l0_w2_b, l0_a0_b,
           l0_a2_b, l0_g2, l0_k_k, l0_k_a, l0_r_k, l0_lnx_g, l0_lnx_b, l0_w_out, l0_mlp_w1, l0_mlp_w2,
           l1_w_mod, l1_b_mod, l1_norm1_g, l1_norm2_g, l1_w_in, l1_conv_w, l1_conv_b, l1_f_w1, l1_f_b1,
           l1_f_w2, l1_f_b2, l1_f_w3, l1_f_b3, l1_f_w4, l1_f_freq, l1_hy_bias, l1_q_norm_g, l1_k_norm_g,
           l1_w_out, l1_mlp_w1, l1_mlp_w2, final_g):
    bsz, n, d = x.shape
    x = x.astype(F32)
    ctx_s = ctx.astype(F32)
    bf = lambda w: w.astype(BF16)

    p0 = dict(w_mod=l0_w_mod, b_mod=l0_b_mod, mu_prev=l0_mu_prev, mu_next=l0_mu_next, w0_f=l0_w0_f, w2_f=l0_w2_f,
              a0_f=l0_a0_f, a2_f=l0_a2_f, w0_b=l0_w0_b, w2_b=l0_w2_b, a0_b=l0_a0_b, a2_b=l0_a2_b, g2=l0_g2,
              k_k=l0_k_k, k_a=l0_k_a, r_k=l0_r_k, lnx_g=l0_lnx_g, lnx_b=l0_lnx_b)
    (sh1, sc1, g1, sh2, sc2, g2), (csh1, csc1, cg1, csh2, csc2, cg2) = _mod_vectors(c, c_ctx, p0)
    aw = d // 2
    w_rw = l0_w_in[:, 3 * aw:]
    rw_cols = -(-(w_rw.shape[1] + 2 * LANE - B_GATE_RANK) // 512) * 512
    w_in0 = bf(jnp.concatenate([w_rw, jnp.zeros((d, rw_cols - w_rw.shape[1]), w_rw.dtype), l0_w_in[:, :3 * aw]],
                               axis=1))
    a1, b1 = _affine(l0_norm1_g, sh1, sc1)
    ca1, cb1 = _affine(l0_norm1_g, csh1, csc1)
    p_lat0 = _inproj(x, a1, b1, w_in0)
    p_ctx0 = _inproj(ctx_s, ca1, cb1, w_in0)
    lam_init = 0.8 - 0.6 * math.exp(-0.3 * 0)
    lam = (jnp.exp(jnp.sum(l0_lam_q1 * l0_lam_k1).astype(F32))
           - jnp.exp(jnp.sum(l0_lam_q2 * l0_lam_k2).astype(F32)) + lam_init)
    oa_lat, oa_ctx = _diff_attention(p_lat0, p_ctx0, rw_cols, aw, lam, lam_init, l0_subln_g)
    nh, hd = l0_r_k.shape
    zero = jnp.zeros((bsz, hd, nh * hd), F32)
    y_ctx, states_c = _rwkv_run(_rwkv_prepare(p_ctx0, p0, rw_cols), (zero, zero), p0)
    y_lat, _ = _rwkv_run(_rwkv_prepare(p_lat0, p0, rw_cols), states_c, p0)
    w_out0 = bf(l0_w_out)
    x = _outproj(oa_lat, y_lat, w_out0, x, g1)
    ctx_s = _outproj(oa_ctx, y_ctx, w_out0, ctx_s, cg1)
    w1, w2 = bf(l0_mlp_w1), bf(l0_mlp_w2)
    a2, b2 = _affine(l0_norm2_g, sh2, sc2)
    ca2, cb2 = _affine(l0_norm2_g, csh2, csc2)
    x = _mlp(x, a2, b2, g2, w1, w2)
    ctx_s = _mlp(ctx_s, ca2, cb2, cg2, w1, w2)

    p1 = dict(w_mod=l1_w_mod, b_mod=l1_b_mod, conv_w=l1_conv_w, conv_b=l1_conv_b, f_w1=l1_f_w1, f_b1=l1_f_b1,
              f_w2=l1_f_w2, f_b2=l1_f_b2, f_w3=l1_f_w3, f_b3=l1_f_b3, f_w4=l1_f_w4, f_freq=l1_f_freq,
              hy_bias=l1_hy_bias)
    (sh1, sc1, g1, sh2, sc2, g2), (csh1, csc1, _, _, _, _) = _mod_vectors(c, c_ctx, p1)
    hy_cols = l1_conv_w.shape[1]
    kv_cols = 2 * (2 * D_HEAD)
    w_in1 = bf(l1_w_in)
    a1, b1 = _affine(l1_norm1_g, sh1, sc1)
    ca1, cb1 = _affine(l1_norm1_g, csh1, csc1)
    p_lat = _inproj(x, a1, b1, w_in1)
    pkv_ctx = _inproj(ctx_s, ca1, cb1, w_in1[:, -kv_cols:])
    o_hy = _hyena(p_lat, p1)
    o_at = _gqa_attention(p_lat, pkv_ctx, hy_cols, l1_q_norm_g, l1_k_norm_g)
    x = _outproj(o_hy, o_at, bf(l1_w_out), x, g1)
    a2, b2 = _affine(l1_norm2_g, sh2, sc2)
    return _mlp(x, a2, b2, g2, bf(l1_mlp_w1), bf(l1_mlp_w2), final_g=final_g)
```

```python
import functools
import math

import numpy as np
import jax
import jax.numpy as jnp
from jax import lax
from jax.experimental import pallas as pl
from jax.experimental.pallas import tpu as pltpu

F32 = jnp.float32
BF16 = jnp.bfloat16
HI = lax.Precision.HIGHEST

NORM_EPS = 1e-6
ROPE_THETA = 10000.0
GRID_W = 64
N_MOD = 6
A_QK_DIM = 64
A_V_DIM = 128
B_HEAD = 64
B_DECAY_RANK = 64
B_ICL_RANK = 64
B_GATE_RANK = 160
B_GN_EPS = 64e-5
HY_EMB = 33
HY_ORDER = 64
HY_TARGET = 1e-2
HY_FAST_PCT = 0.3
HY_SLOW_PCT = 1.5
D_HEAD = 128
D_KV_GROUP = 4

LANE = 128
VMEM_LIMIT = 56 * 1024 * 1024
LOG2E = 1.4426950408889634


def _cp(sem, vmem=VMEM_LIMIT):
    return pltpu.CompilerParams(dimension_semantics=sem, vmem_limit_bytes=vmem)


def _pick(n, pref, step=8):
    t = max(step, min(n, pref) // step * step)
    while n % t:
        t -= step
    return t


def _mods_body(c_ref, w_ref, b_ref, o_ref):
    c = c_ref[...]
    s = c * jax.nn.sigmoid(c)
    o_ref[...] = _mm_x3(s, w_ref[...]) + b_ref[...]


def _mods(cvec, w_mod, b_mod):
    m, d = cvec.shape
    n = w_mod.shape[1]
    tn = _pick(n, 1024, LANE)
    return pl.pallas_call(
        _mods_body,
        grid=(n // tn,),
        in_specs=[pl.BlockSpec((m, d), lambda j: (0, 0)),
                  pl.BlockSpec((d, tn), lambda j: (0, j)),
                  pl.BlockSpec((1, tn), lambda j: (0, j))],
        out_specs=pl.BlockSpec((m, tn), lambda j: (0, j)),
        out_shape=jax.ShapeDtypeStruct((m, n), F32),
        compiler_params=_cp(("arbitrary",)),
        name="mods",
    )(cvec, w_mod, b_mod.reshape(1, n))


def _norm_mod(x, a, b):
    ms = jnp.mean(x * x, axis=-1, keepdims=True)
    return x * lax.rsqrt(ms + NORM_EPS) * a + b


def _inproj_body(x_ref, a_ref, b_ref, w_ref, o_ref, xn_ref):
    @pl.when(pl.program_id(2) == 0)
    def _():
        xn_ref[...] = _norm_mod(x_ref[0], a_ref[0], b_ref[0]).astype(BF16)

    o_ref[0] = jnp.dot(xn_ref[...], w_ref[...], preferred_element_type=F32).astype(o_ref.dtype)


def _inproj(x, a, b, w, out_dtype=BF16, tm_pref=1024, tn_pref=512):
    bsz, n, d = x.shape
    nn = w.shape[1]
    tm, tn = _pick(n, tm_pref), _pick(nn, tn_pref, LANE)
    return pl.pallas_call(
        _inproj_body,
        grid=(bsz, n // tm, nn // tn),
        in_specs=[pl.BlockSpec((1, tm, d), lambda bi, i, j: (bi, i, 0)),
                  pl.BlockSpec((1, 1, d), lambda bi, i, j: (bi, 0, 0)),
                  pl.BlockSpec((1, 1, d), lambda bi, i, j: (bi, 0, 0)),
                  pl.BlockSpec((d, tn), lambda bi, i, j: (0, j))],
        out_specs=pl.BlockSpec((1, tm, tn), lambda bi, i, j: (bi, i, j)),
        out_shape=jax.ShapeDtypeStruct((bsz, n, nn), out_dtype),
        scratch_shapes=[pltpu.VMEM((tm, d), BF16)],
        compiler_params=_cp(("parallel", "parallel", "arbitrary")),
        name="inproj",
    )(x, a, b, w)


def _outproj_body(oa_ref, ob_ref, wa_ref, wb_ref, x_ref, g_ref, y_ref):
    acc = jnp.dot(oa_ref[0].astype(BF16), wa_ref[...], preferred_element_type=F32)
    acc += jnp.dot(ob_ref[0].astype(BF16), wb_ref[...], preferred_element_type=F32)
    y_ref[0] = x_ref[0] + g_ref[0] * acc


def _outproj(oa, ob, w, x, g, tm_pref=1024, tn_pref=512):
    bsz, n, ka = oa.shape
    kb = ob.shape[2]
    d = w.shape[1]
    tm, tn = _pick(n, tm_pref), _pick(d, tn_pref, LANE)
    return pl.pallas_call(
        _outproj_body,
        grid=(bsz, n // tm, d // tn),
        in_specs=[pl.BlockSpec((1, tm, ka), lambda bi, i, j: (bi, i, 0)),
                  pl.BlockSpec((1, tm, kb), lambda bi, i, j: (bi, i, 0)),
                  pl.BlockSpec((ka, tn), lambda bi, i, j: (0, j)),
                  pl.BlockSpec((kb, tn), lambda bi, i, j: (0, j)),
                  pl.BlockSpec((1, tm, tn), lambda bi, i, j: (bi, i, j)),
                  pl.BlockSpec((1, 1, tn), lambda bi, i, j: (bi, 0, j))],
        out_specs=pl.BlockSpec((1, tm, tn), lambda bi, i, j: (bi, i, j)),
        out_shape=jax.ShapeDtypeStruct((bsz, n, d), F32),
        compiler_params=_cp(("parallel", "parallel", "parallel")),
        name="outproj",
    )(oa, ob, w[:ka], w[ka:], x, g)


def _mlp_body(x_ref, a_ref, b_ref, g_ref, w1_ref, w2_ref, fg_ref, y_ref, xn_ref, acc_ref, *, nf, final):
    f = pl.program_id(2)

    @pl.when(f == 0)
    def _():
        xn_ref[...] = _norm_mod(x_ref[0], a_ref[0], b_ref[0]).astype(BF16)
        acc_ref[...] = jnp.zeros_like(acc_ref)

    h = jnp.dot(xn_ref[...], w1_ref[...], preferred_element_type=F32)
    h = jnp.square(jnp.maximum(h, 0.0)).astype(BF16)
    acc_ref[...] += jnp.dot(h, w2_ref[...], preferred_element_type=F32)

    @pl.when(f == nf - 1)
    def _():
        y = x_ref[0] + g_ref[0] * acc_ref[...]
        if final:
            ms = jnp.mean(y * y, axis=-1, keepdims=True)
            y = y * lax.rsqrt(ms + NORM_EPS) * fg_ref[...]
        y_ref[0] = y


def _mlp(x, a, b, g, w1, w2, final_g=None, tm_pref=512, tf_pref=1024):
    bsz, n, d = x.shape
    dff = w1.shape[1]
    tm, tf = _pick(n, tm_pref), _pick(dff, tf_pref, LANE)
    nf = dff // tf
    final = final_g is not None
    fg = (final_g if final else jnp.ones((d,), F32)).reshape(1, d).astype(F32)
    return pl.pallas_call(
        functools.partial(_mlp_body, nf=nf, final=final),
        grid=(bsz, n // tm, nf),
        in_specs=[pl.BlockSpec((1, tm, d), lambda bi, i, f: (bi, i, 0)),
                  pl.BlockSpec((1, 1, d), lambda bi, i, f: (bi, 0, 0)),
                  pl.BlockSpec((1, 1, d), lambda bi, i, f: (bi, 0, 0)),
                  pl.BlockSpec((1, 1, d), lambda bi, i, f: (bi, 0, 0)),
                  pl.BlockSpec((d, tf), lambda bi, i, f: (0, f)),
                  pl.BlockSpec((tf, d), lambda bi, i, f: (f, 0)),
                  pl.BlockSpec((1, d), lambda bi, i, f: (0, 0))],
        out_specs=pl.BlockSpec((1, tm, d), lambda bi, i, f: (bi, i, 0)),
        out_shape=jax.ShapeDtypeStruct((bsz, n, d), F32),
        scratch_shapes=[pltpu.VMEM((tm, d), BF16), pltpu.VMEM((tm, d), F32)],
        compiler_params=_cp(("parallel", "parallel", "arbitrary")),
        name="mlp",
    )(x, a, b, g, w1, w2, fg)


def _rope_tables(n, dim, reps):
    rows = n // GRID_W
    row = jnp.repeat(jnp.arange(rows, dtype=F32), GRID_W)
    col = jnp.tile(jnp.arange(GRID_W, dtype=F32), rows)
    half = dim // 2
    inv = ROPE_THETA ** (-jnp.arange(0, half, 2, dtype=F32) / half)
    ar, ac = row[:, None] * inv, col[:, None] * inv
    cos = jnp.concatenate([jnp.cos(ar), jnp.cos(ar), jnp.cos(ac), jnp.cos(ac)], axis=-1)
    sin = jnp.concatenate([-jnp.sin(ar), jnp.sin(ar), -jnp.sin(ac), jnp.sin(ac)], axis=-1)
    return jnp.tile(cos, (1, reps)), jnp.tile(sin, (1, reps))


def _swap_matrix(dim, reps):
    q = dim // 4
    width = dim * reps
    p = np.zeros((width, width), np.float32)
    for j in range(width):
        base, r = (j // dim) * dim, j % dim
        axis, which, f = r // (2 * q), (r % (2 * q)) // q, r % q
        p[base + axis * 2 * q + (1 - which) * q + f, j] = 1.0
    return p


def _rope_norm(x, cos_ref, sin_ref, p_ref, g_ref, gs_ref, *, norm, rope, scale=1.0):
    y = x.astype(F32)
    if rope:
        ys = jnp.dot(x, p_ref[...], preferred_element_type=F32)
    if norm:
        rs = lax.rsqrt(jnp.mean(y * y, axis=-1, keepdims=True) + NORM_EPS)
        y = y * rs * g_ref[...]
        if rope:
            ys = ys * rs * gs_ref[...]
    if rope:
        y = y * cos_ref[...] + ys * sin_ref[...]
    return y * scale if scale != 1.0 else y


def _rope_operands(n, dim, gain, rope):
    reps = LANE // dim
    if rope:
        cos, sin = _rope_tables(n, dim, reps)
        pm = jnp.asarray(_swap_matrix(dim, reps), BF16)
    else:
        cos = sin = jnp.zeros((n, LANE), F32)
        pm = jnp.zeros((LANE, LANE), BF16)
    if gain is not None:
        g = jnp.tile(gain.astype(F32), reps).reshape(1, LANE)
        gs = jnp.dot(g, jnp.asarray(_swap_matrix(dim, reps)), precision=HI)
    else:
        g = gs = jnp.ones((1, LANE), F32)
    return cos, sin, pm, g, gs


def _kprep_body(xl_ref, xc_ref, cos_ref, sin_ref, p_ref, g_ref, gs_ref, o_ref, *, heads, norm, nb):
    is_lat = pl.program_id(1) < nb
    for h in range(heads):
        sl = slice(h * LANE, (h + 1) * LANE)
        x = jnp.where(is_lat, xl_ref[0, :, sl], xc_ref[0, :, sl])
        y = _rope_norm(x, cos_ref, sin_ref, p_ref, g_ref, gs_ref, norm=norm, rope=True)
        o_ref[0, :, sl] = y.astype(o_ref.dtype)


def _kprep(p_lat, col_lat, p_ctx, col_ctx, heads, *, dim, gain=None):
    bsz, n, _ = p_lat.shape
    nctx = p_ctx.shape[1]
    tq = math.gcd(_pick(n, 512), nctx)
    nb, nbc = n // tq, nctx // tq
    hb = heads
    while col_lat % (hb * LANE) or col_ctx % (hb * LANE):
        hb //= 2
    w = hb * LANE
    cbl, cbc = col_lat // w, col_ctx // w
    cos, sin, pm, g, gs = _rope_operands(n, dim, gain, True)
    cos = jnp.concatenate([cos, jnp.ones((nctx, LANE), F32)], axis=0)
    sin = jnp.concatenate([sin, jnp.zeros((nctx, LANE), F32)], axis=0)
    const = lambda shape: pl.BlockSpec(shape, lambda b, i, j: (0, 0))
    return pl.pallas_call(
        functools.partial(_kprep_body, heads=hb, norm=gain is not None, nb=nb),
        grid=(bsz, nb + nbc, heads // hb),
        in_specs=[pl.BlockSpec((1, tq, w), lambda b, i, j: (b, jnp.minimum(i, nb - 1), cbl + j)),
                  pl.BlockSpec((1, tq, w), lambda b, i, j: (b, jnp.maximum(i - nb, 0), cbc + j)),
                  pl.BlockSpec((tq, LANE), lambda b, i, j: (i, 0)),
                  pl.BlockSpec((tq, LANE), lambda b, i, j: (i, 0)),
                  const((LANE, LANE)), const((1, LANE)), const((1, LANE))],
        out_specs=pl.BlockSpec((1, tq, w), lambda b, i, j: (b, i, j)),
        out_shape=jax.ShapeDtypeStruct((bsz, n + nctx, heads * LANE), BF16),
        compiler_params=_cp(("parallel", "parallel", "parallel")),
        name="kprep",
    )(p_lat, p_ctx, cos, sin, pm, g, gs)


FLASH_KEYS = 1280
FLASH_HEADROOM = 64.0


def _knorm_body(k_ref, o_ref, *, heads):
    for h in range(heads):
        k = k_ref[0, :, h * LANE:(h + 1) * LANE].astype(F32)
        nrm2 = jnp.max(jnp.sum(k * k, axis=1, keepdims=True), axis=0, keepdims=True)
        o_ref[0, h] = jnp.broadcast_to(jnp.sqrt(nrm2), (8, LANE))


def _key_block_norms(k, kcol0, heads, ts):
    bsz, nk, _ = k.shape
    nsub = nk // ts
    hb = heads
    while kcol0 % (hb * LANE):
        hb //= 2
    kb = kcol0 // (hb * LANE)
    out = pl.pallas_call(
        functools.partial(_knorm_body, heads=hb),
        grid=(bsz, heads // hb, nsub),
        in_specs=[pl.BlockSpec((1, ts, hb * LANE), lambda b, j, c: (b, c, kb + j))],
        out_specs=pl.BlockSpec((1, hb, 8, LANE), lambda b, j, c: (b, j, c, 0)),
        out_shape=jax.ShapeDtypeStruct((bsz, heads, nsub * 8, LANE), F32),
        compiler_params=_cp(("parallel", "parallel", "parallel")),
        name="key_block_norms",
    )(k)
    return out[:, :, ::8, 0].reshape(-1)


def _flash_body(kmax_ref, qr_ref, cos_ref, sin_ref, pm_ref, g_ref, gs_ref, k_ref, v_ref, e1_ref, e2_ref, o_ref,
                q_ref, m_ref, l_ref, acc_ref, thr_ref, *, ts, nsub, mode, tq, norm, rope, scale):
    base = (pl.program_id(0) * pl.num_programs(1) + pl.program_id(1)) * nsub
    prep = functools.partial(_rope_norm, cos_ref=cos_ref, sin_ref=sin_ref, p_ref=pm_ref, g_ref=g_ref, gs_ref=gs_ref,
                             norm=norm, rope=rope, scale=scale)
    if mode == "diff":
        y = prep(qr_ref[0])
        lane = lax.broadcasted_iota(jnp.int32, y.shape, 1)
        q_ref[:tq] = jnp.where(lane < A_QK_DIM, y, 0.0).astype(BF16)
        q_ref[tq:] = jnp.where(lane >= A_QK_DIM, y, 0.0).astype(BF16)
    else:
        for j in range(D_KV_GROUP):
            q_ref[j * tq:(j + 1) * tq] = prep(qr_ref[0, :, j * LANE:(j + 1) * LANE]).astype(BF16)
    q = q_ref[...]

    def scores(off, size):
        k = k_ref[0, pl.ds(off, size), :]
        return lax.dot_general(q, k, (((1,), (1,)), ((), ())), preferred_element_type=F32)

    def block(c):
        off = pl.multiple_of(c * ts, ts)
        return scores(off, ts), v_ref[0, pl.ds(off, ts), :]

    def wide(m):
        return jnp.concatenate([m] * (ts // LANE), axis=1)

    m0 = jnp.broadcast_to(jnp.max(scores(0, 2 * LANE), axis=1, keepdims=True), m_ref.shape)
    m_ref[...] = m0
    l_ref[...] = jnp.zeros_like(l_ref)
    acc_ref[...] = jnp.zeros_like(acc_ref)
    qf = q.astype(F32)
    qn = jnp.sqrt(jnp.sum(qf * qf, axis=1, keepdims=True))
    thr_ref[0] = jnp.min((m0[:, :1] + FLASH_HEADROOM) / qn)

    def step(c, carry):
        fixed = kmax_ref[base + c] * 1.001 <= thr_ref[0]

        @pl.when(fixed)
        def _():
            s, v = block(c)
            p = jnp.exp2(s - wide(m_ref[...]))
            l_ref[...] += jnp.sum(p, axis=1, keepdims=True)
            acc_ref[...] += jnp.dot(p.astype(BF16), v, preferred_element_type=F32)

        @pl.when(jnp.logical_not(fixed))
        def _():
            s, v = block(c)
            m_prev = m_ref[...]
            m_new = jnp.maximum(m_prev, jnp.max(s, axis=1, keepdims=True))
            alpha = jnp.exp2(m_prev - m_new)
            p = jnp.exp2(s - wide(m_new))
            l_ref[...] = alpha * l_ref[...] + jnp.sum(p, axis=1, keepdims=True)
            acc_ref[...] = alpha * acc_ref[...] + jnp.dot(p.astype(BF16), v, preferred_element_type=F32)
            m_ref[...] = m_new

        return carry

    lax.fori_loop(0, nsub, step, 0)

    o = acc_ref[...] / l_ref[...]
    if mode == "diff":
        d = o[:tq] - e1_ref[...] * o[tq:]
        ms = jnp.mean(d * d, axis=-1, keepdims=True)
        o_ref[0] = (d * lax.rsqrt(ms + NORM_EPS) * e2_ref[...]).astype(o_ref.dtype)
    else:
        for j in range(D_KV_GROUP):
            o_ref[0, :, j * LANE:(j + 1) * LANE] = o[j * tq:(j + 1) * tq].astype(o_ref.dtype)


def _flash(pq, qcol0, k, v, kcol0, vcol0, e1, e2, *, mode, hk, dim, tq, gain=None, rope=True):
    bsz, n, _ = pq.shape
    nk = k.shape[1]
    g = 2 if mode == "diff" else D_KV_GROUP
    mq = g * tq
    qw = LANE if mode == "diff" else D_KV_GROUP * LANE
    qb = qcol0 // qw
    ts = _pick(nk, FLASH_KEYS, 2 * LANE)
    kb, vb = kcol0 // LANE, vcol0 // LANE
    ow = LANE if mode == "diff" else D_KV_GROUP * LANE
    kmax = _key_block_norms(k, kcol0, hk, ts)
    cos, sin, pm, gq, gqs = _rope_operands(n, dim, gain, rope)
    const = lambda shape: pl.BlockSpec(shape, lambda b, h, i: (0, 0))
    return pl.pallas_call(
        functools.partial(_flash_body, ts=ts, nsub=nk // ts, mode=mode, tq=tq, norm=gain is not None, rope=rope,
                          scale=dim ** -0.5 * LOG2E),
        grid=(bsz, hk, n // tq),
        in_specs=[pl.BlockSpec(memory_space=pltpu.SMEM),
                  pl.BlockSpec((1, tq, qw), lambda b, h, i: (b, i, qb + h)),
                  pl.BlockSpec((tq, LANE), lambda b, h, i: (i, 0)),
                  pl.BlockSpec((tq, LANE), lambda b, h, i: (i, 0)),
                  const((LANE, LANE)), const((1, LANE)), const((1, LANE)),
                  pl.BlockSpec((1, nk, LANE), lambda b, h, i: (b, 0, kb + h)),
                  pl.BlockSpec((1, nk, LANE), lambda b, h, i: (b, 0, vb + h)),
                  const((1, LANE)), const((1, LANE))],
        out_specs=pl.BlockSpec((1, tq, ow), lambda b, h, i: (b, i, h)),
        out_shape=jax.ShapeDtypeStruct((bsz, n, hk * ow), BF16),
        scratch_shapes=[pltpu.VMEM((mq, LANE), BF16), pltpu.VMEM((mq, LANE), F32), pltpu.VMEM((mq, LANE), F32),
                        pltpu.VMEM((mq, LANE), F32), pltpu.SMEM((1,), F32)],
        compiler_params=_cp(("parallel", "parallel", "arbitrary")),
        name="flash_" + mode,
    )(kmax, pq, cos, sin, pm, gq, gqs, k, v, e1, e2)


def _diff_attention(pa_lat, pa_ctx, c0, aw, lam, lam_init, subln_g):
    n, nctx = pa_lat.shape[1], pa_ctx.shape[1]
    heads = aw // LANE
    k_all = _kprep(pa_lat, c0 + aw, pa_ctx, c0 + aw, heads, dim=A_QK_DIM)
    v_all = jnp.concatenate([pa_lat[..., c0 + 2 * aw:c0 + 3 * aw], pa_ctx[..., c0 + 2 * aw:c0 + 3 * aw]], axis=1)
    e1 = jnp.full((1, LANE), lam, F32)
    e2 = (subln_g.astype(F32) * (1.0 - lam_init)).reshape(1, LANE)
    o_lat = _flash(pa_lat, c0, k_all, v_all, 0, 0, e1, e2, mode="diff", hk=heads, dim=A_QK_DIM, tq=_pick(n, 1024))
    o_ctx = _flash(pa_ctx, c0, pa_ctx, pa_ctx, c0 + aw, c0 + 2 * aw, e1, e2, mode="diff", hk=heads, dim=A_QK_DIM,
                   tq=_pick(nctx, 512), rope=False)
    return o_lat, o_ctx


def _gqa_attention(p_lat, pkv_ctx, qcol0, q_norm_g, k_norm_g):
    n = p_lat.shape[1]
    hq, hk = 2 * D_KV_GROUP, 2
    kcol0 = qcol0 + hq * D_HEAD
    vcol0 = kcol0 + hk * D_HEAD
    k_all = _kprep(p_lat, kcol0, pkv_ctx, 0, hk, dim=D_HEAD, gain=k_norm_g)
    v_all = jnp.concatenate([p_lat[..., vcol0:vcol0 + hk * D_HEAD], pkv_ctx[..., hk * D_HEAD:]], axis=1)
    dummy = jnp.zeros((1, LANE), F32)
    return _flash(p_lat, qcol0, k_all, v_all, 0, 0, dummy, dummy, mode="gqa", hk=hk, dim=D_HEAD,
                  tq=_pick(n, 512), gain=q_norm_g)


RW_CHUNK = 64
RW_HALO = 8


def _mm_bf(a, b):
    return jnp.dot(a.astype(BF16), b.astype(BF16), preferred_element_type=F32)


def _split3(x):
    x1 = x.astype(BF16)
    r1 = x - x1.astype(F32)
    x2 = r1.astype(BF16)
    return x1, x2, (r1 - x2.astype(F32)).astype(BF16)


def _mm_x3(a, b):
    ah, al, _ = _split3(a)
    bh, bl, _ = _split3(b)
    d = lambda u, w: jnp.dot(u, w, preferred_element_type=F32)
    return d(ah, bh) + (d(ah, bl) + d(al, bh))


def _mm_sel(z, sel):
    z1, z2, _ = _split3(z)
    d = lambda u: jnp.dot(u, sel, preferred_element_type=F32)
    return d(z1) + d(z2)


def _rwprep_body(x_ref, hp_ref, hn_ref, mup_ref, mun_ref, vec_ref, w2f_ref, w2b_ref, a2f_ref, a2b_ref,
                 g2_ref, sel_ref, selt_ref,
                 r_ref, v_ref, kk_ref, lwf_ref, bf_ref, kf_ref, lwb_ref, bb_ref, kb_ref, g_ref, bonus_ref,
                 *, nb, tm, bw):
    i = pl.program_id(1)
    x = x_ref[0].astype(F32)
    row = lax.broadcasted_iota(jnp.int32, x.shape, 0)
    prev_edge = jnp.where(i > 0, hp_ref[0, RW_HALO - 1:RW_HALO, :].astype(F32), 0.0)
    next_edge = jnp.where(i < nb - 1, hn_ref[0, 0:1, :].astype(F32), 0.0)
    prev = jnp.where(row == 0, prev_edge, pltpu.roll(x, 1, 0))
    nxt = jnp.where(row == tm - 1, next_edge, pltpu.roll(x, tm - 1, 0))
    xs = x + mup_ref[...] * (prev - x) + mun_ref[...] * (nxt - x)
    r, k, v = xs[:, :bw], xs[:, bw:2 * bw], xs[:, 2 * bw:3 * bw]
    lr = xs[:, 3 * bw:3 * bw + LANE]
    gd = xs[:, 3 * bw + LANE:3 * bw + 3 * LANE]
    k_k, k_a, r_k = vec_ref[0:1], vec_ref[1:2], vec_ref[2:3]
    w0f, a0f, w0b, a0b = vec_ref[3:4], vec_ref[4:5], vec_ref[5:6], vec_ref[6:7]
    sel, selt = sel_ref[...], selt_ref[...]

    def head_sum(z):
        return _mm_sel(_mm_sel(z, sel), selt)

    kk = k * k_k
    kk = kk * lax.rsqrt(head_sum(kk * kk) + 1e-12)
    g_ref[0] = _mm_bf(jax.nn.sigmoid(gd), g2_ref[...]).astype(g_ref.dtype)
    th = jnp.tanh(lr)
    ksum = jnp.zeros_like(k)
    for (w0, a0, w2_ref, a2_ref, lw_ref, b_ref, kd_ref) in (
            (w0f, a0f, w2f_ref, a2f_ref, lwf_ref, bf_ref, kf_ref),
            (w0b, a0b, w2b_ref, a2b_ref, lwb_ref, bb_ref, kb_ref)):
        logw = -math.exp(-0.5) * jax.nn.sigmoid(w0 + _mm_bf(th, w2_ref[...]))
        a = jax.nn.sigmoid(a0 + _mm_bf(lr, a2_ref[...]))
        k_d = k * (1.0 + (a - 1.0) * k_a)
        ksum = ksum + k_d
        lw_ref[0] = logw
        b_ref[0] = kk * a
        kd_ref[0] = k_d
    r_ref[0] = r
    v_ref[0] = v
    kk_ref[0] = kk
    bonus_ref[0] = head_sum(r * ksum * r_k) * v


def _rwkv_prepare(pr, p, cols):
    bsz, n, _ = pr.shape
    bw = p["w0_f"].shape[0]
    tm = _pick(n, 256)
    nb = n // tm
    hb = tm // RW_HALO
    assert cols - 3 * bw >= 3 * LANE and B_DECAY_RANK + B_ICL_RANK == LANE

    def padded(vec):
        return jnp.concatenate([vec.astype(F32), jnp.zeros((cols - vec.shape[0],), F32)]).reshape(1, cols)

    vecs = jnp.stack([p["k_k"], p["k_a"], p["r_k"].reshape(-1), p["w0_f"], p["a0_f"], p["w0_b"], p["a0_b"],
                      jnp.zeros((bw,), F32)]).astype(F32)
    zr = jnp.zeros((B_DECAY_RANK, bw), F32)
    w2 = {d: jnp.concatenate([p["w2_" + d].astype(F32), zr], axis=0).astype(BF16) for d in "fb"}
    a2 = {d: jnp.concatenate([zr, p["a2_" + d].astype(F32)], axis=0).astype(BF16) for d in "fb"}
    g2 = jnp.concatenate([p["g2"].astype(F32), jnp.zeros((2 * LANE - B_GATE_RANK, bw), F32)], axis=0).astype(BF16)
    sel, selt = _head_selectors(bw)

    full = lambda shape: pl.BlockSpec(shape, lambda b, i: tuple(0 for _ in shape))
    tok = pl.BlockSpec((1, tm, bw), lambda b, i: (b, i, 0))
    tok_f32 = jax.ShapeDtypeStruct((bsz, n, bw), F32)
    outs = pl.pallas_call(
        functools.partial(_rwprep_body, nb=nb, tm=tm, bw=bw),
        grid=(bsz, nb),
        in_specs=[pl.BlockSpec((1, tm, cols), lambda b, i: (b, i, 0)),
                  pl.BlockSpec((1, RW_HALO, cols), lambda b, i: (b, jnp.maximum(i * hb - 1, 0), 0)),
                  pl.BlockSpec((1, RW_HALO, cols), lambda b, i: (b, jnp.minimum((i + 1) * hb, nb * hb - 1), 0)),
                  full((1, cols)), full((1, cols)), full((8, bw)),
                  full((LANE, bw)), full((LANE, bw)), full((LANE, bw)), full((LANE, bw)),
                  full((2 * LANE, bw)), full((bw, LANE)), full((LANE, bw))],
        out_specs=[tok] * 11,
        out_shape=[tok_f32] * 9 + [jax.ShapeDtypeStruct((bsz, n, bw), BF16), tok_f32],
        compiler_params=_cp(("parallel", "parallel")),
        name="rwkv_prepare",
    )(pr, pr, pr, padded(p["mu_prev"]), padded(p["mu_next"]), vecs, w2["f"], w2["b"], a2["f"], a2["b"],
      g2, sel, selt)
    r, v, kk, lwf, bf, kf, lwb, bb, kb, g, bonus = outs
    return dict(r=r, v=v, kk=kk, g=g, bonus=bonus, f=(lwf, bf, kf), b=(lwb, bb, kb))


RW_GROUP = 4


def _bd(x):
    t = jnp.concatenate([x.astype(BF16)] * RW_GROUP, axis=1)
    r = lax.broadcasted_iota(jnp.int32, t.shape, 1) // B_HEAD
    c = lax.broadcasted_iota(jnp.int32, t.shape, 2) // B_HEAD
    return jnp.where(r == c, t, jnp.zeros_like(t))


def _bdot(a, b, dims):
    return jnp.einsum(dims, a.astype(BF16), b.astype(BF16), preferred_element_type=F32)


def _groups(x):
    gw = RW_GROUP * B_HEAD
    return jnp.stack([x[:, g * gw:(g + 1) * gw] for g in range(x.shape[1] // gw)])


def _rw_decays(r, kk, b, kd, lw, reverse):
    ck = r.shape[0]
    ti = lax.broadcasted_iota(jnp.int32, (ck, ck), 0)
    si = lax.broadcasted_iota(jnp.int32, (ck, ck), 1)
    tri = ((si >= ti) if reverse else (si <= ti)).astype(BF16)
    l1, l2, l3 = _split3(lw)
    tdot = lambda z: jnp.dot(tri, z, preferred_element_type=F32)
    cum = tdot(l1) + (tdot(l2) + tdot(l3))
    total = cum[0:1] if reverse else cum[ck - 1:ck]
    winv = jnp.exp(-cum)
    wrest = jnp.exp(total - cum)
    return (-kk * jnp.exp(cum - lw), b * winv, kd * winv, r * jnp.exp(cum), b * wrest, kd * wrest, jnp.exp(total))


def _rw_chunks(fwd, bwd, s_f, s_b):
    nseq = len(fwd)
    ck, w = fwd[0][0].shape
    gw = RW_GROUP * B_HEAD
    ng = w // gw
    parts = []
    for probs, rev in ((fwd, False), (bwd, True)):
        for (r, v, kk, b, kd, lw) in probs:
            parts.append([_groups(z) for z in _rw_decays(r, kk, b, kd, lw, rev)] + [_groups(v)])
    at, bt, kt, rt, btw, ktw, decay, v = [jnp.concatenate(zs, axis=0) for zs in zip(*parts)]
    s = jnp.concatenate([_groups(z) for z in list(s_f) + list(s_b)], axis=0)
    nslab = 2 * nseq * ng
    shape = (nslab, ck, gw)
    lag = lax.broadcasted_iota(jnp.int32, shape, 1) - lax.broadcasted_iota(jnp.int32, shape, 2) % ck
    lag = jnp.where(lax.broadcasted_iota(jnp.int32, shape, 0) >= nseq * ng, -lag, lag)
    incl, strict = lag >= 0, lag > 0
    eye = (lag == 0).astype(F32)

    bdv = _bd(v)
    lhs = jnp.concatenate([at, rt], axis=1)
    lb, lk = _bdot(lhs, _bd(bt), "gtk,gnk->gtn"), _bdot(lhs, _bd(kt), "gtk,gnk->gtn")
    lab, mrb = jnp.where(strict, lb[:, :ck], 0.0), jnp.where(incl, lb[:, ck:], 0.0)
    lak, mrk = jnp.where(strict, lk[:, :ck], 0.0), jnp.where(incl, lk[:, ck:], 0.0)
    tinv = eye + lab
    pw = _bdot(lab, _bd(lab), "gtk,gkn->gtn")
    span = 4
    while span < ck:
        both = _bdot(jnp.concatenate([tinv, pw], axis=1), _bd(pw), "gtk,gkn->gtn")
        tinv, pw = tinv + both[:, :ck], both[:, ck:]
        span *= 2
    tinv = tinv + _bdot(tinv, _bd(pw), "gtk,gkn->gtn")
    vmix = _bdot(jnp.concatenate([lak, mrk], axis=1), bdv, "gtk,gkn->gtn")
    p1 = _bdot(tinv, _bd(at), "gtk,gkn->gtn")
    q1 = _bdot(tinv, _bd(vmix[:, :ck]), "gtk,gkn->gtn")
    smix = _bdot(jnp.concatenate([p1, rt], axis=1), _bd(s), "gtk,gnk->gtn")
    u = smix[:, :ck] + q1
    y = smix[:, ck:] + _bdot(mrb, _bd(u), "gtk,gkn->gtn") + vmix[:, ck:]
    z = _bdot(jnp.concatenate([u, v], axis=1), jnp.concatenate([btw, ktw], axis=1), "gtm,gtn->gmn")
    lane_head = lax.broadcasted_iota(jnp.int32, (nslab, B_HEAD, gw), 2) // B_HEAD
    upd = jnp.zeros_like(s)
    for h in range(RW_GROUP):
        upd = upd + jnp.where(lane_head == h, z[:, h * B_HEAD:(h + 1) * B_HEAD], 0.0)
    s_new = s * decay + upd
    lanes = lambda x, q: jnp.concatenate([x[q * ng + g] for g in range(ng)], axis=1)
    seqs = range(nseq)
    return ([lanes(y, q) for q in seqs], [lanes(y, nseq + q) for q in seqs],
            [lanes(s_new, q) for q in seqs], [lanes(s_new, nseq + q) for q in seqs])


def _rwscan_body(rf_ref, vf_ref, kkf_ref, bf_ref, kdf_ref, lwf_ref, rb_ref, vb_ref, kkb_ref, bb_ref, kdb_ref, lwb_ref,
                 s0f_ref, s0b_ref, yf_ref, yb_ref, sTf_ref, sTb_ref, sf_ref, sb_ref, *, nc, nseq):
    c = pl.program_id(0)

    @pl.when(c == 0)
    def _():
        sf_ref[...] = s0f_ref[...]
        sb_ref[...] = s0b_ref[...]

    seqs = range(nseq)
    yf, yb, sf, sb = _rw_chunks(
        [(rf_ref[q], vf_ref[q], kkf_ref[q], bf_ref[q], kdf_ref[q], lwf_ref[q]) for q in seqs],
        [(rb_ref[q], vb_ref[q], kkb_ref[q], bb_ref[q], kdb_ref[q], lwb_ref[q]) for q in seqs],
        [sf_ref[q] for q in seqs], [sb_ref[q] for q in seqs])
    for q in seqs:
        yf_ref[q] = yf[q]
        yb_ref[q] = yb[q]
        sf_ref[q] = sf[q]
        sb_ref[q] = sb[q]

    @pl.when(c == nc - 1)
    def _():
        for q in seqs:
            sTf_ref[q] = sf[q]
            sTb_ref[q] = sb[q]


def _rwkv_scan(r, v, kk, fwd, bwd, s0f, s0b):
    bsz, n, w = r.shape
    ck = RW_CHUNK
    assert n % ck == 0 and ck == B_HEAD and w % (RW_GROUP * B_HEAD) == 0
    nc = n // ck
    seq_f = pl.BlockSpec((bsz, ck, w), lambda c: (0, c, 0))
    seq_b = pl.BlockSpec((bsz, ck, w), lambda c: (0, nc - 1 - c, 0))
    st = pl.BlockSpec((bsz, B_HEAD, w), lambda c: (0, 0, 0))
    y_shape = jax.ShapeDtypeStruct((bsz, n, w), F32)
    s_shape = jax.ShapeDtypeStruct((bsz, B_HEAD, w), F32)
    (lwf, bf, kf), (lwb, bb, kb) = fwd, bwd
    return pl.pallas_call(
        functools.partial(_rwscan_body, nc=nc, nseq=bsz),
        grid=(nc,),
        in_specs=[seq_f] * 6 + [seq_b] * 6 + [st, st],
        out_specs=[seq_f, seq_b, st, st],
        out_shape=[y_shape, y_shape, s_shape, s_shape],
        scratch_shapes=[pltpu.VMEM((bsz, B_HEAD, w), F32), pltpu.VMEM((bsz, B_HEAD, w), F32)],
        compiler_params=_cp(("arbitrary",)),
        name="rwkv_scan",
    )(r, v, kk, bf, kf, lwf, r, v, kk, bb, kb, lwb, s0f, s0b)


def _rwpost_body(yf_ref, yb_ref, g_ref, bonus_ref, lg_ref, lb_ref, sel_ref, selt_ref, o_ref):
    sel, selt = sel_ref[...], selt_ref[...]
    head_mean = lambda z: _mm_sel(_mm_sel(z, sel), selt) * (1.0 / B_HEAD)
    y = yf_ref[0] + yb_ref[0]
    d = y - head_mean(y)
    y = d * lax.rsqrt(head_mean(d * d) + B_GN_EPS)
    y = y * lg_ref[...] + lb_ref[...] + bonus_ref[0]
    o_ref[0] = (y * g_ref[0].astype(F32)).astype(o_ref.dtype)


def _head_selectors(bw):
    sel_np = np.zeros((bw, LANE), np.float32)
    sel_np[np.arange(bw), np.arange(bw) // B_HEAD] = 1.0
    return jnp.asarray(sel_np, BF16), jnp.asarray(sel_np.T, BF16)


def _rwkv_post(y_f, y_b, g, bonus, lnx_g, lnx_b):
    bsz, n, bw = y_f.shape
    tm = _pick(n, 256)
    tok = pl.BlockSpec((1, tm, bw), lambda b, i: (b, i, 0))
    vec = pl.BlockSpec((1, bw), lambda b, i: (0, 0))
    sel, selt = _head_selectors(bw)
    return pl.pallas_call(
        _rwpost_body,
        grid=(bsz, n // tm),
        in_specs=[tok, tok, tok, tok, vec, vec,
                  pl.BlockSpec((bw, LANE), lambda b, i: (0, 0)), pl.BlockSpec((LANE, bw), lambda b, i: (0, 0))],
        out_specs=tok,
        out_shape=jax.ShapeDtypeStruct((bsz, n, bw), BF16),
        compiler_params=_cp(("parallel", "parallel")),
        name="rwkv_post",
    )(y_f, y_b, g, bonus, lnx_g.reshape(1, bw).astype(F32), lnx_b.reshape(1, bw).astype(F32), sel, selt)


def _rwkv_run(prep, states0, p):
    y_f, y_b, s_f, s_b = _rwkv_scan(prep["r"], prep["v"], prep["kk"], prep["f"], prep["b"], states0[0], states0[1])
    return _rwkv_post(y_f, y_b, prep["g"], prep["bonus"], p["lnx_g"], p["lnx_b"]), (s_f, s_b)


HY_N2 = 128
HY_HALO = 8


def _hypre_body(x_ref, hp_ref, hn_ref, cw_ref, cb_ref, x0_ref, uv_ref, *, nb, tm, cw):
    i = pl.program_id(1)
    x = x_ref[0].astype(F32)
    row = lax.broadcasted_iota(jnp.int32, x.shape, 0)
    prev_edge = jnp.where(i > 0, hp_ref[0, HY_HALO - 1:HY_HALO, :].astype(F32), 0.0)
    next_edge = jnp.where(i < nb - 1, hn_ref[0, 0:1, :].astype(F32), 0.0)
    prev = jnp.where(row == 0, prev_edge, pltpu.roll(x, 1, 0))
    nxt = jnp.where(row == tm - 1, next_edge, pltpu.roll(x, tm - 1, 0))
    u = prev * cw_ref[0:1] + x * cw_ref[1:2] + nxt * cw_ref[2:3] + cb_ref[...]
    x0_ref[0] = u[:, :cw]
    uv_ref[0] = u[:, 2 * cw:3 * cw] * u[:, cw:2 * cw]


def _hyena_pre(p, conv_w, conv_b):
    bsz, n, _ = p.shape
    hc = conv_w.shape[1]
    cw = hc // 3
    tm = _pick(n, 256)
    nb = n // tm
    hb = tm // HY_HALO
    cwp = jnp.concatenate([conv_w.astype(F32), jnp.zeros((5, hc), F32)], axis=0)
    out = pl.BlockSpec((1, tm, cw), lambda b, i: (b, i, 0))
    return pl.pallas_call(
        functools.partial(_hypre_body, nb=nb, tm=tm, cw=cw),
        grid=(bsz, nb),
        in_specs=[pl.BlockSpec((1, tm, hc), lambda b, i: (b, i, 0)),
                  pl.BlockSpec((1, HY_HALO, hc), lambda b, i: (b, jnp.maximum(i * hb - 1, 0), 0)),
                  pl.BlockSpec((1, HY_HALO, hc), lambda b, i: (b, jnp.minimum((i + 1) * hb, nb * hb - 1), 0)),
                  pl.BlockSpec((8, hc), lambda b, i: (0, 0)),
                  pl.BlockSpec((1, hc), lambda b, i: (0, 0))],
        out_specs=[out, out],
        out_shape=[jax.ShapeDtypeStruct((bsz, n, cw), F32)] * 2,
        compiler_params=_cp(("parallel", "parallel")),
        name="hyena_pre",
    )(p, p, p, cwp, conv_b.reshape(1, hc).astype(F32))


def _hyfilt_body(z_ref, t_ref, z0_ref, w1_ref, w2_ref, w3_ref, w4_ref, w4b_ref, vec_ref, dl_ref,
                 k_ref, s_ref, *, half_tiles):
    i = pl.program_id(0)
    b1, b2, b3, fr = vec_ref[0:1], vec_ref[1:2], vec_ref[2:3], vec_ref[3:4]

    def mlp3(z):
        h = jnp.sin(fr * (_mm_x3(z, w1_ref[...]) + b1))
        h = jnp.sin(fr * (_mm_x3(h, w2_ref[...]) + b2))
        h = jnp.sin(fr * (_mm_x3(h, w3_ref[...]) + b3))
        lane = lax.broadcasted_iota(jnp.int32, h.shape, 1)
        return jnp.where(lane < HY_ORDER, h, 0.0), jnp.where(lane >= HY_ORDER, h, 0.0)

    h_lo, h_hi = mlp3(z_ref[...])
    raw = jnp.concatenate([_mm_x3(h_lo, w4_ref[...]), _mm_x3(h_hi, w4_ref[...])], axis=0)
    raw = raw * jnp.exp(-t_ref[...] * dl_ref[...])

    @pl.when(i == 0)
    def _():
        s_ref[...] = jnp.zeros_like(s_ref)

    s_ref[...] += jnp.sum(jnp.abs(raw), axis=0, keepdims=True)
    hb0 = _mm_x3(mlp3(z0_ref[...])[0], w4b_ref[...])
    row = lax.broadcasted_iota(jnp.int32, raw.shape, 0)
    raw = raw + jnp.where((row == 0) & (i == 0), hb0[0:1], 0.0)
    k_ref[...] = jnp.where((row == 0) & (i == half_tiles), 0.0, raw)


def _hyena_kernel_taps(n, p):
    cw = p["f_w4"].shape[1] // 2
    f32 = lambda a: a.astype(F32)
    row = jnp.arange(2 * n, dtype=jnp.int32)
    pos = jnp.where(row < n, row, (2 * n - row) % n).astype(F32)[:, None]
    tt = pos * jnp.float32(1.0 / (n - 1))
    bands = (HY_EMB - 1) // 2
    ang = (2 * math.pi / n) * pos * jnp.linspace(1e-4, bands - 1, bands, dtype=F32)[None]
    zz = jnp.concatenate([tt, jnp.cos(ang), -jnp.sin(ang), jnp.zeros((2 * n, HY_ORDER - HY_EMB), F32)], axis=-1)
    w1 = jnp.concatenate([f32(p["f_w1"]), jnp.zeros((HY_ORDER - HY_EMB, HY_ORDER), F32)], axis=0)
    twice = lambda w: jnp.kron(jnp.eye(2, dtype=F32), f32(w))
    vecs = jnp.stack([jnp.tile(f32(p[k]), 2) for k in ("f_b1", "f_b2", "f_b3", "f_freq")]
                     + [jnp.zeros((2 * HY_ORDER,), F32)] * 4)
    deltas = jnp.abs(jnp.linspace(math.log(HY_TARGET) / HY_SLOW_PCT, math.log(HY_TARGET) / HY_FAST_PCT, cw,
                                  dtype=F32)).reshape(1, cw)
    tr = _pick(n, 512)
    half_tiles = n // tr
    pk = 2 * HY_ORDER
    zp = zz.reshape(-1, 2, tr // 2, HY_ORDER).transpose(0, 2, 1, 3).reshape(-1, pk)
    full = lambda shape: pl.BlockSpec(shape, lambda i: tuple(0 for _ in shape))
    w4 = jnp.concatenate([f32(p["f_w4"])] * 2, axis=0)
    return pl.pallas_call(
        functools.partial(_hyfilt_body, half_tiles=half_tiles),
        grid=(2 * half_tiles,),
        in_specs=[pl.BlockSpec((tr // 2, pk), lambda i: (i, 0)),
                  pl.BlockSpec((tr, 1), lambda i: (i, 0)),
                  full((8, pk)), full((pk, pk)), full((pk, pk)), full((pk, pk)),
                  pl.BlockSpec((pk, cw), lambda i: (0, i // half_tiles)),
                  pl.BlockSpec((pk, cw), lambda i: (0, 1)),
                  full((8, pk)), full((1, cw))],
        out_specs=[pl.BlockSpec((tr, cw), lambda i: (i, 0)), pl.BlockSpec((1, cw), lambda i: (0, 0))],
        out_shape=[jax.ShapeDtypeStruct((2 * n, cw), F32), jax.ShapeDtypeStruct((1, cw), F32)],
        compiler_params=_cp(("arbitrary",)),
        name="hyena_filter",
    )(zp, tt, jnp.broadcast_to(jnp.tile(zz[0:1], (1, 2)), (8, pk)), twice(w1), twice(p["f_w2"]), twice(p["f_w3"]),
      w4, w4, vecs, deltas)


def _dft_consts(n):
    nn = 2 * n
    n2 = HY_N2
    n1 = nn // n2
    n1h = n1 // 2 + 1
    nf = -(-n1h // 8) * 8
    live = (np.arange(nf) < n1h).astype(np.float64)
    a1 = 2 * np.pi * np.outer(np.arange(nf), np.arange(n1)) / n1
    f1 = np.concatenate([np.cos(a1), -np.sin(a1)], axis=0) * np.tile(live, 2)[:, None]
    wgt = live * np.where((np.arange(nf) == 0) | (np.arange(nf) == n1 // 2), 1.0, 2.0)
    f1inv = np.concatenate([np.cos(a1.T) * wgt, -np.sin(a1.T) * wgt], axis=1) / nn
    a2 = 2 * np.pi * np.outer(np.arange(n2), np.arange(n2)) / n2
    c2, s2 = np.cos(a2), -np.sin(a2)
    m2 = np.block([[c2, -s2], [s2, c2]])
    m2inv = np.block([[c2, s2], [-s2, c2]])
    at = 2 * np.pi * np.outer(np.arange(n2), np.arange(nf)) / nn
    tw = np.stack([np.cos(at), -np.sin(at)])
    c = lambda a: jnp.asarray(a.astype(np.float32))
    return dict(n1=n1, n2=n2, nf=nf, f1=c(f1), f1inv=c(f1inv), m2=c(m2), m2inv=c(m2inv),
                tw_s2=c(tw[:, :, :, None]),
                tw_f1=c(np.transpose(tw, (0, 2, 1))[:, :, :, None]))


HY_S2 = 8
HY_CTILE = 512


def _dft1_body(f_ref, x_ref, tw_ref, o_ref, *, nf):
    x = jnp.swapaxes(x_ref[0], 0, 1)
    fh, fl, _ = _split3(f_ref[...])
    d = lambda u, w: jnp.dot(u, w, preferred_element_type=F32)
    re, im = [], []
    for jj in range(HY_S2):
        xh, xl, _ = _split3(x[jj])
        a = d(fh, xh) + (d(fh, xl) + d(fl, xh))
        ar, ai = a[:nf], a[nf:]
        twr, twi = tw_ref[0, jj], tw_ref[1, jj]
        re.append(ar * twr - ai * twi)
        im.append(ar * twi + ai * twr)
    o_ref[0, 0] = jnp.swapaxes(jnp.stack(re), 0, 1)
    o_ref[0, 1] = jnp.swapaxes(jnp.stack(im), 0, 1)


def _dft_stage1(x, consts, rows):
    bsz, _, cw = x.shape
    n2, nf = consts["n2"], consts["nf"]
    xv = x.reshape(bsz, rows, n2, cw)
    f1 = consts["f1"][:, :rows]
    tc = _pick(cw, HY_CTILE, LANE)
    return pl.pallas_call(
        functools.partial(_dft1_body, nf=nf),
        grid=(bsz, n2 // HY_S2, cw // tc),
        in_specs=[pl.BlockSpec((2 * nf, rows), lambda b, j, c: (0, 0)),
                  pl.BlockSpec((1, rows, HY_S2, tc), lambda b, j, c: (b, 0, j, c)),
                  pl.BlockSpec((2, HY_S2, nf, 1), lambda b, j, c: (0, j, 0, 0))],
        out_specs=pl.BlockSpec((1, 2, nf, HY_S2, tc), lambda b, j, c: (b, 0, 0, j, c)),
        out_shape=jax.ShapeDtypeStruct((bsz, 2, nf, n2, cw), F32),
        compiler_params=_cp(("parallel", "parallel", "parallel")),
        name="hyena_dft1",
    )(f1, xv, consts["tw_s2"])


HY_F1 = 4


def _dftmid_body(a_ref, m2_ref, m2i_ref, h_ref, tw_ref, is_ref, o_ref, *, n2, filt):
    for r in range(HY_F1):
        a = a_ref[0, :, r].reshape(2 * n2, a_ref.shape[-1])
        x = _mm_x3(m2_ref[...], a)
        xr, xi = x[:n2], x[n2:]
        if filt:
            o_ref[0, 0, r] = xr * is_ref[...]
            o_ref[0, 1, r] = xi * is_ref[...]
            continue
        hr, hi = h_ref[0, 0, r], h_ref[0, 1, r]
        y = jnp.concatenate([xr * hr - xi * hi, xr * hi + xi * hr], axis=0)
        z = _mm_x3(m2i_ref[...], y)
        zr, zi = z[:n2], z[n2:]
        twr, twi = tw_ref[0, r], tw_ref[1, r]
        o_ref[0, 0, r] = zr * twr + zi * twi
        o_ref[0, 1, r] = zi * twr - zr * twi


def _dft_mid(a, h, inv_s, consts, filt):
    bsz = a.shape[0]
    n1, n2 = consts["nf"], consts["n2"]
    cw = a.shape[-1]
    blk = pl.BlockSpec((1, 2, HY_F1, n2, cw), lambda f, b: (b, 0, f, 0, 0))
    hblk = pl.BlockSpec((1, 2, HY_F1, n2, cw),
                        (lambda f, b: (0, 0, 0, 0, 0)) if filt else (lambda f, b: (0, 0, f, 0, 0)))
    return pl.pallas_call(
        functools.partial(_dftmid_body, n2=n2, filt=filt),
        grid=(n1 // HY_F1, bsz),
        in_specs=[blk,
                  pl.BlockSpec((2 * n2, 2 * n2), lambda f, b: (0, 0)),
                  pl.BlockSpec((2 * n2, 2 * n2), lambda f, b: (0, 0)),
                  hblk,
                  pl.BlockSpec((2, HY_F1, n2, 1), lambda f, b: (0, f, 0, 0)),
                  pl.BlockSpec((1, cw), lambda f, b: (0, 0))],
        out_specs=blk,
        out_shape=jax.ShapeDtypeStruct((bsz, 2, n1, n2, cw), F32),
        compiler_params=_cp(("parallel", "parallel")),
        name="hyena_dftmid_" + ("filter" if filt else "conv"),
    )(a, consts["m2"], consts["m2inv"], h, consts["tw_f1"], inv_s)


def _dft3_body(f_ref, z_ref, x0_ref, uv_ref, bias_ref, o_ref):
    zr, zi = jnp.swapaxes(z_ref[0, 0], 0, 1), jnp.swapaxes(z_ref[0, 1], 0, 1)
    x0, uv = jnp.swapaxes(x0_ref[0], 0, 1), jnp.swapaxes(uv_ref[0], 0, 1)
    fh, fl, _ = _split3(f_ref[...])
    d = lambda u, w: jnp.dot(u, w, preferred_element_type=F32)
    out = []
    for jj in range(HY_S2):
        zh, zl, _ = _split3(jnp.concatenate([zr[jj], zi[jj]], axis=0))
        y = d(fh, zh) + (d(fh, zl) + d(fl, zh))
        out.append(x0[jj] * (y + uv[jj] * bias_ref[...]))
    o_ref[0] = jnp.swapaxes(jnp.stack(out), 0, 1)


def _dft_stage1_inv(z, x0, uv, bias, consts):
    bsz, n, cw = x0.shape
    nf, n2 = consts["nf"], consts["n2"]
    hr = consts["n1"] // 2
    tokv = lambda t: t.reshape(bsz, hr, n2, cw)
    tc = _pick(cw, HY_CTILE, LANE)
    tile = pl.BlockSpec((1, hr, HY_S2, tc), lambda b, j, c: (b, 0, j, c))
    out = pl.pallas_call(
        _dft3_body,
        grid=(bsz, n2 // HY_S2, cw // tc),
        in_specs=[pl.BlockSpec((hr, 2 * nf), lambda b, j, c: (0, 0)),
                  pl.BlockSpec((1, 2, nf, HY_S2, tc), lambda b, j, c: (b, 0, 0, j, c)),
                  tile, tile,
                  pl.BlockSpec((1, tc), lambda b, j, c: (0, c))],
        out_specs=tile,
        out_shape=jax.ShapeDtypeStruct((bsz, hr, n2, cw), F32),
        compiler_params=_cp(("parallel", "parallel", "parallel")),
        name="hyena_dft3",
    )(consts["f1inv"][:hr], z, tokv(x0), tokv(uv), bias.reshape(1, cw).astype(F32))
    return out.reshape(bsz, n, cw)


def _hyena(p_lat, p):
    n = p_lat.shape[1]
    consts = _dft_consts(n)
    n1 = consts["n1"]
    x0, uv = _hyena_pre(p_lat, p["conv_w"], p["conv_b"])
    taps, sabs = _hyena_kernel_taps(n, p)
    cw = taps.shape[1]
    ones = jnp.ones((1, cw), F32)
    hk = _dft_stage1(taps[None], consts, n1)
    hspec = _dft_mid(hk, jnp.zeros((1, 2, HY_F1, HY_N2, cw), F32), 1.0 / sabs, consts, True)
    a = _dft_stage1(uv, consts, n1 // 2)
    z = _dft_mid(a, hspec, ones, consts, False)
    return _dft_stage1_inv(z, x0, uv, p["hy_bias"], consts)


def _mod_vectors(c, c_ctx, p):
    bsz, d = c.shape
    cvec = jnp.concatenate([c, c_ctx[None], jnp.zeros((8 - bsz - 1, d), c.dtype)], axis=0).astype(F32)
    m = _mods(cvec, p["w_mod"].astype(F32), p["b_mod"].astype(F32))
    lat = [v[:, None, :] for v in jnp.split(m[:bsz], N_MOD, axis=-1)]
    cx = [jnp.broadcast_to(v[:, None, :], (bsz, 1, d)) for v in jnp.split(m[bsz:bsz + 1], N_MOD, axis=-1)]
    return lat, cx


def _affine(norm_g, shift, scale):
    return norm_g.astype(F32) * (1.0 + scale), shift


def kernel(x, c, ctx, c_ctx,
           l0_w_mod, l0_b_mod, l0_norm1_g, l0_norm2_g, l0_w_in, l0_lam_q1, l0_lam_k1, l0_lam_q2, l0_lam_k2,
           l0_subln_g, l0_mu_prev, l0_mu_next, l0_w0_f, l0_w2_f, l0_a0_f, l0_a2_f, l0_w0_b, l0_w2_b, l0_a0_b,
           l0_a2_b, l0_g2, l0_k_k, l0_k_a, l0_r_k, l0_lnx_g, l0_lnx_b, l0_w_out, l0_mlp_w1, l0_mlp_w2,
           l1_w_mod, l1_b_mod, l1_norm1_g, l1_norm2_g, l1_w_in, l1_conv_w, l1_conv_b, l1_f_w1, l1_f_b1,
           l1_f_w2, l1_f_b2, l1_f_w3, l1_f_b3, l1_f_w4, l1_f_freq, l1_hy_bias, l1_q_norm_g, l1_k_norm_g,
           l1_w_out, l1_mlp_w1, l1_mlp_w2, final_g):
    bsz, n, d = x.shape
    x = x.astype(F32)
    ctx_s = ctx.astype(F32)
    bf = lambda w: w.astype(BF16)

    p0 = dict(w_mod=l0_w_mod, b_mod=l0_b_mod, mu_prev=l0_mu_prev, mu_next=l0_mu_next, w0_f=l0_w0_f, w2_f=l0_w2_f,
              a0_f=l0_a0_f, a2_f=l0_a2_f, w0_b=l0_w0_b, w2_b=l0_w2_b, a0_b=l0_a0_b, a2_b=l0_a2_b, g2=l0_g2,
              k_k=l0_k_k, k_a=l0_k_a, r_k=l0_r_k, lnx_g=l0_lnx_g, lnx_b=l0_lnx_b)
    (sh1, sc1, g1, sh2, sc2, g2), (csh1, csc1, cg1, csh2, csc2, cg2) = _mod_vectors(c, c_ctx, p0)
    aw = d // 2
    w_rw = l0_w_in[:, 3 * aw:]
    rw_cols = -(-(w_rw.shape[1] + 2 * LANE - B_GATE_RANK) // 512) * 512
    w_in0 = bf(jnp.concatenate([w_rw, jnp.zeros((d, rw_cols - w_rw.shape[1]), w_rw.dtype), l0_w_in[:, :3 * aw]],
                               axis=1))
    a1, b1 = _affine(l0_norm1_g, sh1, sc1)
    ca1, cb1 = _affine(l0_norm1_g, csh1, csc1)
    p_lat0 = _inproj(x, a1, b1, w_in0)
    p_ctx0 = _inproj(ctx_s, ca1, cb1, w_in0)
    lam_init = 0.8 - 0.6 * math.exp(-0.3 * 0)
    lam = (jnp.exp(jnp.sum(l0_lam_q1 * l0_lam_k1).astype(F32))
           - jnp.exp(jnp.sum(l0_lam_q2 * l0_lam_k2).astype(F32)) + lam_init)
    oa_lat, oa_ctx = _diff_attention(p_lat0, p_ctx0, rw_cols, aw, lam, lam_init, l0_subln_g)
    nh, hd = l0_r_k.shape
    zero = jnp.zeros((bsz, hd, nh * hd), F32)
    y_ctx, states_c = _rwkv_run(_rwkv_prepare(p_ctx0, p0, rw_cols), (zero, zero), p0)
    y_lat, _ = _rwkv_run(_rwkv_prepare(p_lat0, p0, rw_cols), states_c, p0)
    w_out0 = bf(l0_w_out)
    x = _outproj(oa_lat, y_lat, w_out0, x, g1)
    ctx_s = _outproj(oa_ctx, y_ctx, w_out0, ctx_s, cg1)
    w1, w2 = bf(l0_mlp_w1), bf(l0_mlp_w2)
    a2, b2 = _affine(l0_norm2_g, sh2, sc2)
    ca2, cb2 = _affine(l0_norm2_g, csh2, csc2)
    x = _mlp(x, a2, b2, g2, w1, w2)
    ctx_s = _mlp(ctx_s, ca2, cb2, cg2, w1, w2)

    p1 = dict(w_mod=l1_w_mod, b_mod=l1_b_mod, conv_w=l1_conv_w, conv_b=l1_conv_b, f_w1=l1_f_w1, f_b1=l1_f_b1,
              f_w2=l1_f_w2, f_b2=l1_f_b2, f_w3=l1_f_w3, f_b3=l1_f_b3, f_w4=l1_f_w4, f_freq=l1_f_freq,
              hy_bias=l1_hy_bias)
    (sh1, sc1, g1, sh2, sc2, g2), (csh1, csc1, _, _, _, _) = _mod_vectors(c, c_ctx, p1)
    hy_cols = l1_conv_w.shape[1]
    kv_cols = 2 * (2 * D_HEAD)
    w_in1 = bf(l1_w_in)
    a1, b1 = _affine(l1_norm1_g, sh1, sc1)
    ca1, cb1 = _affine(l1_norm1_g, csh1, csc1)
    p_lat = _inproj(x, a1, b1, w_in1)
    pkv_ctx = _inproj(ctx_s, ca1, cb1, w_in1[:, -kv_cols:])
    o_hy = _hyena(p_lat, p1)
    o_at = _gqa_attention(p_lat, pkv_ctx, hy_cols, l1_q_norm_g, l1_k_norm_g)
    x = _outproj(o_hy, o_at, bf(l1_w_out), x, g1)
    a2, b2 = _affine(l1_norm2_g, sh2, sc2)
    return _mlp(x, a2, b2, g2, bf(l1_mlp_w1), bf(l1_mlp_w2), final_g=final_g)
```

```python
import functools
import math

import numpy as np
import jax
import jax.numpy as jnp
from jax import lax
from jax.experimental import pallas as pl
from jax.experimental.pallas import tpu as pltpu

F32 = jnp.float32
BF16 = jnp.bfloat16
HI = lax.Precision.HIGHEST

NORM_EPS = 1e-6
ROPE_THETA = 10000.0
GRID_W = 64
N_MOD = 6
A_QK_DIM = 64
A_V_DIM = 128
B_HEAD = 64
B_DECAY_RANK = 64
B_ICL_RANK = 64
B_GATE_RANK = 160
B_GN_EPS = 64e-5
HY_EMB = 33
HY_ORDER = 64
HY_TARGET = 1e-2
HY_FAST_PCT = 0.3
HY_SLOW_PCT = 1.5
D_HEAD = 128
D_KV_GROUP = 4

LANE = 128
VMEM_LIMIT = 56 * 1024 * 1024
LOG2E = 1.4426950408889634


def _cp(sem, vmem=VMEM_LIMIT):
    return pltpu.CompilerParams(dimension_semantics=sem, vmem_limit_bytes=vmem)


def _pick(n, pref, step=8):
    t = max(step, min(n, pref) // step * step)
    while n % t:
        t -= step
    return t


def _mods_body(c_ref, w_ref, b_ref, o_ref):
    c = c_ref[...]
    s = c * jax.nn.sigmoid(c)
    o_ref[...] = _mm_x3(s, w_ref[...]) + b_ref[...]


def _mods(cvec, w_mod, b_mod):
    m, d = cvec.shape
    n = w_mod.shape[1]
    tn = _pick(n, 1024, LANE)
    return pl.pallas_call(
        _mods_body,
        grid=(n // tn,),
        in_specs=[pl.BlockSpec((m, d), lambda j: (0, 0)),
                  pl.BlockSpec((d, tn), lambda j: (0, j)),
                  pl.BlockSpec((1, tn), lambda j: (0, j))],
        out_specs=pl.BlockSpec((m, tn), lambda j: (0, j)),
        out_shape=jax.ShapeDtypeStruct((m, n), F32),
        compiler_params=_cp(("arbitrary",)),
        name="mods",
    )(cvec, w_mod, b_mod.reshape(1, n))


def _norm_mod(x, a, b):
    ms = jnp.mean(x * x, axis=-1, keepdims=True)
    return x * lax.rsqrt(ms + NORM_EPS) * a + b


def _inproj_body(x_ref, a_ref, b_ref, w_ref, o_ref, xn_ref):
    @pl.when(pl.program_id(2) == 0)
    def _():
        xn_ref[...] = _norm_mod(x_ref[0], a_ref[0], b_ref[0]).astype(BF16)

    o_ref[0] = jnp.dot(xn_ref[...], w_ref[...], preferred_element_type=F32).astype(o_ref.dtype)


def _inproj(x, a, b, w, out_dtype=BF16, tm_pref=1024, tn_pref=512):
    bsz, n, d = x.shape
    nn = w.shape[1]
    tm, tn = _pick(n, tm_pref), _pick(nn, tn_pref, LANE)
    return pl.pallas_call(
        _inproj_body,
        grid=(bsz, n // tm, nn // tn),
        in_specs=[pl.BlockSpec((1, tm, d), lambda bi, i, j: (bi, i, 0)),
                  pl.BlockSpec((1, 1, d), lambda bi, i, j: (bi, 0, 0)),
                  pl.BlockSpec((1, 1, d), lambda bi, i, j: (bi, 0, 0)),
                  pl.BlockSpec((d, tn), lambda bi, i, j: (0, j))],
        out_specs=pl.BlockSpec((1, tm, tn), lambda bi, i, j: (bi, i, j)),
        out_shape=jax.ShapeDtypeStruct((bsz, n, nn), out_dtype),
        scratch_shapes=[pltpu.VMEM((tm, d), BF16)],
        compiler_params=_cp(("parallel", "parallel", "arbitrary")),
        name="inproj",
    )(x, a, b, w)


def _outproj_body(oa_ref, ob_ref, wa_ref, wb_ref, x_ref, g_ref, y_ref):
    acc = jnp.dot(oa_ref[0].astype(BF16), wa_ref[...], preferred_element_type=F32)
    acc += jnp.dot(ob_ref[0].astype(BF16), wb_ref[...], preferred_element_type=F32)
    y_ref[0] = x_ref[0] + g_ref[0] * acc


def _outproj(oa, ob, w, x, g, tm_pref=1024, tn_pref=512):
    bsz, n, ka = oa.shape
    kb = ob.shape[2]
    d = w.shape[1]
    tm, tn = _pick(n, tm_pref), _pick(d, tn_pref, LANE)
    return pl.pallas_call(
        _outproj_body,
        grid=(bsz, n // tm, d // tn),
        in_specs=[pl.BlockSpec((1, tm, ka), lambda bi, i, j: (bi, i, 0)),
                  pl.BlockSpec((1, tm, kb), lambda bi, i, j: (bi, i, 0)),
                  pl.BlockSpec((ka, tn), lambda bi, i, j: (0, j)),
                  pl.BlockSpec((kb, tn), lambda bi, i, j: (0, j)),
                  pl.BlockSpec((1, tm, tn), lambda bi, i, j: (bi, i, j)),
                  pl.BlockSpec((1, 1, tn), lambda bi, i, j: (bi, 0, j))],
        out_specs=pl.BlockSpec((1, tm, tn), lambda bi, i, j: (bi, i, j)),
        out_shape=jax.ShapeDtypeStruct((bsz, n, d), F32),
        compiler_params=_cp(("parallel", "parallel", "parallel")),
        name="outproj",
    )(oa, ob, w[:ka], w[ka:], x, g)


def _mlp_body(x_ref, a_ref, b_ref, g_ref, w1_ref, w2_ref, fg_ref, y_ref, xn_ref, acc_ref, *, nf, final):
    f = pl.program_id(2)

    @pl.when(f == 0)
    def _():
        xn_ref[...] = _norm_mod(x_ref[0], a_ref[0], b_ref[0]).astype(BF16)
        acc_ref[...] = jnp.zeros_like(acc_ref)

    tf = w1_ref.shape[1]
    part = None
    for lo in range(0, tf, tf // 2):
        h = jnp.dot(xn_ref[...], w1_ref[:, lo:lo + tf // 2], preferred_element_type=F32)
        h = jnp.square(jnp.maximum(h, 0.0)).astype(BF16)
        d = jnp.dot(h, w2_ref[lo:lo + tf // 2, :], preferred_element_type=F32)
        part = d if part is None else part + d
    acc_ref[...] += part

    @pl.when(f == nf - 1)
    def _():
        y = x_ref[0] + g_ref[0] * acc_ref[...]
        if final:
            ms = jnp.mean(y * y, axis=-1, keepdims=True)
            y = y * lax.rsqrt(ms + NORM_EPS) * fg_ref[...]
        y_ref[0] = y


def _mlp(x, a, b, g, w1, w2, final_g=None, tm_pref=512, tf_pref=1024):
    bsz, n, d = x.shape
    dff = w1.shape[1]
    tm, tf = _pick(n, tm_pref), _pick(dff, tf_pref, LANE)
    nf = dff // tf
    final = final_g is not None
    fg = (final_g if final else jnp.ones((d,), F32)).reshape(1, d).astype(F32)
    return pl.pallas_call(
        functools.partial(_mlp_body, nf=nf, final=final),
        grid=(bsz, n // tm, nf),
        in_specs=[pl.BlockSpec((1, tm, d), lambda bi, i, f: (bi, i, 0)),
                  pl.BlockSpec((1, 1, d), lambda bi, i, f: (bi, 0, 0)),
                  pl.BlockSpec((1, 1, d), lambda bi, i, f: (bi, 0, 0)),
                  pl.BlockSpec((1, 1, d), lambda bi, i, f: (bi, 0, 0)),
                  pl.BlockSpec((d, tf), lambda bi, i, f: (0, f)),
                  pl.BlockSpec((tf, d), lambda bi, i, f: (f, 0)),
                  pl.BlockSpec((1, d), lambda bi, i, f: (0, 0))],
        out_specs=pl.BlockSpec((1, tm, d), lambda bi, i, f: (bi, i, 0)),
        out_shape=jax.ShapeDtypeStruct((bsz, n, d), F32),
        scratch_shapes=[pltpu.VMEM((tm, d), BF16), pltpu.VMEM((tm, d), F32)],
        compiler_params=_cp(("parallel", "parallel", "arbitrary")),
        name="mlp",
    )(x, a, b, g, w1, w2, fg)


def _rope_tables(n, dim, reps):
    rows = n // GRID_W
    row = jnp.repeat(jnp.arange(rows, dtype=F32), GRID_W)
    col = jnp.tile(jnp.arange(GRID_W, dtype=F32), rows)
    half = dim // 2
    inv = ROPE_THETA ** (-jnp.arange(0, half, 2, dtype=F32) / half)
    ar, ac = row[:, None] * inv, col[:, None] * inv
    cos = jnp.concatenate([jnp.cos(ar), jnp.cos(ar), jnp.cos(ac), jnp.cos(ac)], axis=-1)
    sin = jnp.concatenate([-jnp.sin(ar), jnp.sin(ar), -jnp.sin(ac), jnp.sin(ac)], axis=-1)
    return jnp.tile(cos, (1, reps)), jnp.tile(sin, (1, reps))


def _swap_matrix(dim, reps):
    q = dim // 4
    width = dim * reps
    p = np.zeros((width, width), np.float32)
    for j in range(width):
        base, r = (j // dim) * dim, j % dim
        axis, which, f = r // (2 * q), (r % (2 * q)) // q, r % q
        p[base + axis * 2 * q + (1 - which) * q + f, j] = 1.0
    return p


def _rope_norm(x, cos_ref, sin_ref, p_ref, g_ref, gs_ref, *, norm, rope, scale=1.0):
    y = x.astype(F32)
    if rope:
        ys = jnp.dot(x, p_ref[...], preferred_element_type=F32)
    if norm:
        rs = lax.rsqrt(jnp.mean(y * y, axis=-1, keepdims=True) + NORM_EPS)
        y = y * rs * g_ref[...]
        if rope:
            ys = ys * rs * gs_ref[...]
    if rope:
        y = y * cos_ref[...] + ys * sin_ref[...]
    return y * scale if scale != 1.0 else y


def _rope_operands(n, dim, gain, rope):
    reps = LANE // dim
    if rope:
        cos, sin = _rope_tables(n, dim, reps)
        pm = jnp.asarray(_swap_matrix(dim, reps), BF16)
    else:
        cos = sin = jnp.zeros((n, LANE), F32)
        pm = jnp.zeros((LANE, LANE), BF16)
    if gain is not None:
        g = jnp.tile(gain.astype(F32), reps).reshape(1, LANE)
        gs = jnp.dot(g, jnp.asarray(_swap_matrix(dim, reps)), precision=HI)
    else:
        g = gs = jnp.ones((1, LANE), F32)
    return cos, sin, pm, g, gs


def _kprep_body(xl_ref, xc_ref, cos_ref, sin_ref, p_ref, g_ref, gs_ref, o_ref, *, heads, norm, nb):
    is_lat = pl.program_id(1) < nb
    for h in range(heads):
        sl = slice(h * LANE, (h + 1) * LANE)
        x = jnp.where(is_lat, xl_ref[0, :, sl], xc_ref[0, :, sl])
        y = _rope_norm(x, cos_ref, sin_ref, p_ref, g_ref, gs_ref, norm=norm, rope=True)
        o_ref[0, :, sl] = y.astype(o_ref.dtype)


def _kprep(p_lat, col_lat, p_ctx, col_ctx, heads, *, dim, gain=None):
    bsz, n, _ = p_lat.shape
    nctx = p_ctx.shape[1]
    tq = math.gcd(_pick(n, 512), nctx)
    nb, nbc = n // tq, nctx // tq
    hb = heads
    while col_lat % (hb * LANE) or col_ctx % (hb * LANE):
        hb //= 2
    w = hb * LANE
    cbl, cbc = col_lat // w, col_ctx // w
    cos, sin, pm, g, gs = _rope_operands(n, dim, gain, True)
    cos = jnp.concatenate([cos, jnp.ones((nctx, LANE), F32)], axis=0)
    sin = jnp.concatenate([sin, jnp.zeros((nctx, LANE), F32)], axis=0)
    const = lambda shape: pl.BlockSpec(shape, lambda b, i, j: (0, 0))
    return pl.pallas_call(
        functools.partial(_kprep_body, heads=hb, norm=gain is not None, nb=nb),
        grid=(bsz, nb + nbc, heads // hb),
        in_specs=[pl.BlockSpec((1, tq, w), lambda b, i, j: (b, jnp.minimum(i, nb - 1), cbl + j)),
                  pl.BlockSpec((1, tq, w), lambda b, i, j: (b, jnp.maximum(i - nb, 0), cbc + j)),
                  pl.BlockSpec((tq, LANE), lambda b, i, j: (i, 0)),
                  pl.BlockSpec((tq, LANE), lambda b, i, j: (i, 0)),
                  const((LANE, LANE)), const((1, LANE)), const((1, LANE))],
        out_specs=pl.BlockSpec((1, tq, w), lambda b, i, j: (b, i, j)),
        out_shape=jax.ShapeDtypeStruct((bsz, n + nctx, heads * LANE), BF16),
        compiler_params=_cp(("parallel", "parallel", "parallel")),
        name="kprep",
    )(p_lat, p_ctx, cos, sin, pm, g, gs)


FLASH_KEYS = 1280
FLASH_HEADROOM = 64.0


def _knorm_body(k_ref, o_ref, *, heads):
    for h in range(heads):
        k = k_ref[0, :, h * LANE:(h + 1) * LANE].astype(F32)
        nrm2 = jnp.max(jnp.sum(k * k, axis=1, keepdims=True), axis=0, keepdims=True)
        o_ref[0, h] = jnp.broadcast_to(jnp.sqrt(nrm2), (8, LANE))


def _key_block_norms(k, kcol0, heads, ts):
    bsz, nk, _ = k.shape
    nsub = nk // ts
    hb = heads
    while kcol0 % (hb * LANE):
        hb //= 2
    kb = kcol0 // (hb * LANE)
    out = pl.pallas_call(
        functools.partial(_knorm_body, heads=hb),
        grid=(bsz, heads // hb, nsub),
        in_specs=[pl.BlockSpec((1, ts, hb * LANE), lambda b, j, c: (b, c, kb + j))],
        out_specs=pl.BlockSpec((1, hb, 8, LANE), lambda b, j, c: (b, j, c, 0)),
        out_shape=jax.ShapeDtypeStruct((bsz, heads, nsub * 8, LANE), F32),
        compiler_params=_cp(("parallel", "parallel", "parallel")),
        name="key_block_norms",
    )(k)
    return out[:, :, ::8, 0].reshape(-1)


def _flash_body(kmax_ref, qr_ref, cos_ref, sin_ref, pm_ref, g_ref, gs_ref, k_ref, v_ref, e1_ref, e2_ref, o_ref,
                q_ref, m_ref, l_ref, acc_ref, thr_ref, *, ts, nsub, mode, tq, norm, rope, scale):
    base = (pl.program_id(0) * pl.num_programs(1) + pl.program_id(1)) * nsub
    prep = functools.partial(_rope_norm, cos_ref=cos_ref, sin_ref=sin_ref, p_ref=pm_ref, g_ref=g_ref, gs_ref=gs_ref,
                             norm=norm, rope=rope, scale=scale)
    if mode == "diff":
        y = prep(qr_ref[0])
        lane = lax.broadcasted_iota(jnp.int32, y.shape, 1)
        q_ref[:tq] = jnp.where(lane < A_QK_DIM, y, 0.0).astype(BF16)
        q_ref[tq:] = jnp.where(lane >= A_QK_DIM, y, 0.0).astype(BF16)
    else:
        for j in range(D_KV_GROUP):
            q_ref[j * tq:(j + 1) * tq] = prep(qr_ref[0, :, j * LANE:(j + 1) * LANE]).astype(BF16)
    q = q_ref[...]

    def scores(off, size):
        k = k_ref[0, pl.ds(off, size), :]
        return lax.dot_general(q, k, (((1,), (1,)), ((), ())), preferred_element_type=F32)

    def block(c):
        off = pl.multiple_of(c * ts, ts)
        return scores(off, ts), v_ref[0, pl.ds(off, ts), :]

    def wide(m):
        return jnp.concatenate([m] * (ts // LANE), axis=1)

    m0 = jnp.broadcast_to(jnp.max(scores(0, 2 * LANE), axis=1, keepdims=True), m_ref.shape)
    m_ref[...] = m0
    l_ref[...] = jnp.zeros_like(l_ref)
    acc_ref[...] = jnp.zeros_like(acc_ref)
    qf = q.astype(F32)
    qn = jnp.sqrt(jnp.sum(qf * qf, axis=1, keepdims=True))
    thr_ref[0] = jnp.min((m0[:, :1] + FLASH_HEADROOM) / qn)

    def step(c, carry):
        fixed = kmax_ref[base + c] * 1.001 <= thr_ref[0]

        @pl.when(fixed)
        def _():
            s, v = block(c)
            p = jnp.exp2(s - wide(m_ref[...]))
            l_ref[...] += jnp.sum(p, axis=1, keepdims=True)
            acc_ref[...] += jnp.dot(p.astype(BF16), v, preferred_element_type=F32)

        @pl.when(jnp.logical_not(fixed))
        def _():
            s, v = block(c)
            m_prev = m_ref[...]
            m_new = jnp.maximum(m_prev, jnp.max(s, axis=1, keepdims=True))
            alpha = jnp.exp2(m_prev - m_new)
            p = jnp.exp2(s - wide(m_new))
            l_ref[...] = alpha * l_ref[...] + jnp.sum(p, axis=1, keepdims=True)
            acc_ref[...] = alpha * acc_ref[...] + jnp.dot(p.astype(BF16), v, preferred_element_type=F32)
            m_ref[...] = m_new

        return carry

    lax.fori_loop(0, nsub, step, 0)

    o = acc_ref[...] / l_ref[...]
    if mode == "diff":
        d = o[:tq] - e1_ref[...] * o[tq:]
        ms = jnp.mean(d * d, axis=-1, keepdims=True)
        o_ref[0] = (d * lax.rsqrt(ms + NORM_EPS) * e2_ref[...]).astype(o_ref.dtype)
    else:
        for j in range(D_KV_GROUP):
            o_ref[0, :, j * LANE:(j + 1) * LANE] = o[j * tq:(j + 1) * tq].astype(o_ref.dtype)


def _flash(pq, qcol0, k, v, kcol0, vcol0, e1, e2, *, mode, hk, dim, tq, gain=None, rope=True):
    bsz, n, _ = pq.shape
    nk = k.shape[1]
    g = 2 if mode == "diff" else D_KV_GROUP
    mq = g * tq
    qw = LANE if mode == "diff" else D_KV_GROUP * LANE
    qb = qcol0 // qw
    ts = _pick(nk, FLASH_KEYS, 2 * LANE)
    kb, vb = kcol0 // LANE, vcol0 // LANE
    ow = LANE if mode == "diff" else D_KV_GROUP * LANE
    kmax = _key_block_norms(k, kcol0, hk, ts)
    cos, sin, pm, gq, gqs = _rope_operands(n, dim, gain, rope)
    const = lambda shape: pl.BlockSpec(shape, lambda b, h, i: (0, 0))
    return pl.pallas_call(
        functools.partial(_flash_body, ts=ts, nsub=nk // ts, mode=mode, tq=tq, norm=gain is not None, rope=rope,
                          scale=dim ** -0.5 * LOG2E),
        grid=(bsz, hk, n // tq),
        in_specs=[pl.BlockSpec(memory_space=pltpu.SMEM),
                  pl.BlockSpec((1, tq, qw), lambda b, h, i: (b, i, qb + h)),
                  pl.BlockSpec((tq, LANE), lambda b, h, i: (i, 0)),
                  pl.BlockSpec((tq, LANE), lambda b, h, i: (i, 0)),
                  const((LANE, LANE)), const((1, LANE)), const((1, LANE)),
                  pl.BlockSpec((1, nk, LANE), lambda b, h, i: (b, 0, kb + h)),
                  pl.BlockSpec((1, nk, LANE), lambda b, h, i: (b, 0, vb + h)),
                  const((1, LANE)), const((1, LANE))],
        out_specs=pl.BlockSpec((1, tq, ow), lambda b, h, i: (b, i, h)),
        out_shape=jax.ShapeDtypeStruct((bsz, n, hk * ow), BF16),
        scratch_shapes=[pltpu.VMEM((mq, LANE), BF16), pltpu.VMEM((mq, LANE), F32), pltpu.VMEM((mq, LANE), F32),
                        pltpu.VMEM((mq, LANE), F32), pltpu.SMEM((1,), F32)],
        compiler_params=_cp(("parallel", "parallel", "arbitrary")),
        name="flash_" + mode,
    )(kmax, pq, cos, sin, pm, gq, gqs, k, v, e1, e2)


def _diff_attention(pa_lat, pa_ctx, c0, aw, lam, lam_init, subln_g):
    n, nctx = pa_lat.shape[1], pa_ctx.shape[1]
    heads = aw // LANE
    k_all = _kprep(pa_lat, c0 + aw, pa_ctx, c0 + aw, heads, dim=A_QK_DIM)
    v_all = jnp.concatenate([pa_lat[..., c0 + 2 * aw:c0 + 3 * aw], pa_ctx[..., c0 + 2 * aw:c0 + 3 * aw]], axis=1)
    e1 = jnp.full((1, LANE), lam, F32)
    e2 = (subln_g.astype(F32) * (1.0 - lam_init)).reshape(1, LANE)
    o_lat = _flash(pa_lat, c0, k_all, v_all, 0, 0, e1, e2, mode="diff", hk=heads, dim=A_QK_DIM, tq=_pick(n, 1024))
    o_ctx = _flash(pa_ctx, c0, pa_ctx, pa_ctx, c0 + aw, c0 + 2 * aw, e1, e2, mode="diff", hk=heads, dim=A_QK_DIM,
                   tq=_pick(nctx, 512), rope=False)
    return o_lat, o_ctx


def _gqa_attention(p_lat, pkv_ctx, qcol0, q_norm_g, k_norm_g):
    n = p_lat.shape[1]
    hq, hk = 2 * D_KV_GROUP, 2
    kcol0 = qcol0 + hq * D_HEAD
    vcol0 = kcol0 + hk * D_HEAD
    k_all = _kprep(p_lat, kcol0, pkv_ctx, 0, hk, dim=D_HEAD, gain=k_norm_g)
    v_all = jnp.concatenate([p_lat[..., vcol0:vcol0 + hk * D_HEAD], pkv_ctx[..., hk * D_HEAD:]], axis=1)
    dummy = jnp.zeros((1, LANE), F32)
    return _flash(p_lat, qcol0, k_all, v_all, 0, 0, dummy, dummy, mode="gqa", hk=hk, dim=D_HEAD,
                  tq=_pick(n, 512), gain=q_norm_g)


RW_CHUNK = 64
RW_HALO = 8


def _mm_bf(a, b):
    return jnp.dot(a.astype(BF16), b.astype(BF16), preferred_element_type=F32)


def _split3(x):
    x1 = x.astype(BF16)
    r1 = x - x1.astype(F32)
    x2 = r1.astype(BF16)
    return x1, x2, (r1 - x2.astype(F32)).astype(BF16)


def _mm_x3(a, b):
    ah, al, _ = _split3(a)
    bh, bl, _ = _split3(b)
    d = lambda u, w: jnp.dot(u, w, preferred_element_type=F32)
    return d(ah, bh) + (d(ah, bl) + d(al, bh))


def _mm_sel(z, sel):
    z1, z2, _ = _split3(z)
    d = lambda u: jnp.dot(u, sel, preferred_element_type=F32)
    return d(z1) + d(z2)


def _rwprep_body(x_ref, hp_ref, hn_ref, mup_ref, mun_ref, vec_ref, w2f_ref, w2b_ref, a2f_ref, a2b_ref,
                 g2_ref, sel_ref, selt_ref,
                 r_ref, v_ref, kk_ref, lwf_ref, bf_ref, kf_ref, lwb_ref, bb_ref, kb_ref, g_ref, bonus_ref,
                 *, nb, tm, bw):
    i = pl.program_id(1)
    x = x_ref[0].astype(F32)
    row = lax.broadcasted_iota(jnp.int32, x.shape, 0)
    prev_edge = jnp.where(i > 0, hp_ref[0, RW_HALO - 1:RW_HALO, :].astype(F32), 0.0)
    next_edge = jnp.where(i < nb - 1, hn_ref[0, 0:1, :].astype(F32), 0.0)
    prev = jnp.where(row == 0, prev_edge, pltpu.roll(x, 1, 0))
    nxt = jnp.where(row == tm - 1, next_edge, pltpu.roll(x, tm - 1, 0))
    xs = x + mup_ref[...] * (prev - x) + mun_ref[...] * (nxt - x)
    r, k, v = xs[:, :bw], xs[:, bw:2 * bw], xs[:, 2 * bw:3 * bw]
    lr = xs[:, 3 * bw:3 * bw + LANE]
    gd = xs[:, 3 * bw + LANE:3 * bw + 3 * LANE]
    k_k, k_a, r_k = vec_ref[0:1], vec_ref[1:2], vec_ref[2:3]
    w0f, a0f, w0b, a0b = vec_ref[3:4], vec_ref[4:5], vec_ref[5:6], vec_ref[6:7]
    sel, selt = sel_ref[...], selt_ref[...]

    def head_sum(z):
        return _mm_sel(_mm_sel(z, sel), selt)

    kk = k * k_k
    kk = kk * lax.rsqrt(head_sum(kk * kk) + 1e-12)
    g_ref[0] = _mm_bf(jax.nn.sigmoid(gd), g2_ref[...]).astype(g_ref.dtype)
    th = jnp.tanh(lr)
    ksum = jnp.zeros_like(k)
    for (w0, a0, w2_ref, a2_ref, lw_ref, b_ref, kd_ref) in (
            (w0f, a0f, w2f_ref, a2f_ref, lwf_ref, bf_ref, kf_ref),
            (w0b, a0b, w2b_ref, a2b_ref, lwb_ref, bb_ref, kb_ref)):
        logw = -math.exp(-0.5) * jax.nn.sigmoid(w0 + _mm_bf(th, w2_ref[...]))
        a = jax.nn.sigmoid(a0 + _mm_bf(lr, a2_ref[...]))
        k_d = k * (1.0 + (a - 1.0) * k_a)
        ksum = ksum + k_d
        lw_ref[0] = logw
        b_ref[0] = kk * a
        kd_ref[0] = k_d
    r_ref[0] = r
    v_ref[0] = v
    kk_ref[0] = kk
    bonus_ref[0] = head_sum(r * ksum * r_k) * v


def _rwkv_prepare(pr, p, cols):
    bsz, n, _ = pr.shape
    bw = p["w0_f"].shape[0]
    tm = _pick(n, 256)
    nb = n // tm
    hb = tm // RW_HALO
    assert cols - 3 * bw >= 3 * LANE and B_DECAY_RANK + B_ICL_RANK == LANE

    def padded(vec):
        return jnp.concatenate([vec.astype(F32), jnp.zeros((cols - vec.shape[0],), F32)]).reshape(1, cols)

    vecs = jnp.stack([p["k_k"], p["k_a"], p["r_k"].reshape(-1), p["w0_f"], p["a0_f"], p["w0_b"], p["a0_b"],
                      jnp.zeros((bw,), F32)]).astype(F32)
    zr = jnp.zeros((B_DECAY_RANK, bw), F32)
    w2 = {d: jnp.concatenate([p["w2_" + d].astype(F32), zr], axis=0).astype(BF16) for d in "fb"}
    a2 = {d: jnp.concatenate([zr, p["a2_" + d].astype(F32)], axis=0).astype(BF16) for d in "fb"}
    g2 = jnp.concatenate([p["g2"].astype(F32), jnp.zeros((2 * LANE - B_GATE_RANK, bw), F32)], axis=0).astype(BF16)
    sel, selt = _head_selectors(bw)

    full = lambda shape: pl.BlockSpec(shape, lambda b, i: tuple(0 for _ in shape))
    tok = pl.BlockSpec((1, tm, bw), lambda b, i: (b, i, 0))
    tok_f32 = jax.ShapeDtypeStruct((bsz, n, bw), F32)
    outs = pl.pallas_call(
        functools.partial(_rwprep_body, nb=nb, tm=tm, bw=bw),
        grid=(bsz, nb),
        in_specs=[pl.BlockSpec((1, tm, cols), lambda b, i: (b, i, 0)),
                  pl.BlockSpec((1, RW_HALO, cols), lambda b, i: (b, jnp.maximum(i * hb - 1, 0), 0)),
                  pl.BlockSpec((1, RW_HALO, cols), lambda b, i: (b, jnp.minimum((i + 1) * hb, nb * hb - 1), 0)),
                  full((1, cols)), full((1, cols)), full((8, bw)),
                  full((LANE, bw)), full((LANE, bw)), full((LANE, bw)), full((LANE, bw)),
                  full((2 * LANE, bw)), full((bw, LANE)), full((LANE, bw))],
        out_specs=[tok] * 11,
        out_shape=[tok_f32] * 9 + [jax.ShapeDtypeStruct((bsz, n, bw), BF16), tok_f32],
        compiler_params=_cp(("parallel", "parallel")),
        name="rwkv_prepare",
    )(pr, pr, pr, padded(p["mu_prev"]), padded(p["mu_next"]), vecs, w2["f"], w2["b"], a2["f"], a2["b"],
      g2, sel, selt)
    r, v, kk, lwf, bf, kf, lwb, bb, kb, g, bonus = outs
    return dict(r=r, v=v, kk=kk, g=g, bonus=bonus, f=(lwf, bf, kf), b=(lwb, bb, kb))


RW_GROUP = 4


def _bd(x):
    t = jnp.concatenate([x.astype(BF16)] * RW_GROUP, axis=1)
    r = lax.broadcasted_iota(jnp.int32, t.shape, 1) // B_HEAD
    c = lax.broadcasted_iota(jnp.int32, t.shape, 2) // B_HEAD
    return jnp.where(r == c, t, jnp.zeros_like(t))


def _bdot(a, b, dims):
    return jnp.einsum(dims, a.astype(BF16), b.astype(BF16), preferred_element_type=F32)


def _groups(x):
    gw = RW_GROUP * B_HEAD
    return jnp.stack([x[:, g * gw:(g + 1) * gw] for g in range(x.shape[1] // gw)])


def _rw_decays(r, kk, b, kd, lw, reverse):
    ck = r.shape[0]
    ti = lax.broadcasted_iota(jnp.int32, (ck, ck), 0)
    si = lax.broadcasted_iota(jnp.int32, (ck, ck), 1)
    tri = ((si >= ti) if reverse else (si <= ti)).astype(BF16)
    l1, l2, l3 = _split3(lw)
    tdot = lambda z: jnp.dot(tri, z, preferred_element_type=F32)
    cum = tdot(l1) + (tdot(l2) + tdot(l3))
    total = cum[0:1] if reverse else cum[ck - 1:ck]
    winv = jnp.exp(-cum)
    wrest = jnp.exp(total - cum)
    return (-kk * jnp.exp(cum - lw), b * winv, kd * winv, r * jnp.exp(cum), b * wrest, kd * wrest, jnp.exp(total))


def _rw_chunks(fwd, bwd, s_f, s_b):
    nseq = len(fwd)
    ck, w = fwd[0][0].shape
    gw = RW_GROUP * B_HEAD
    ng = w // gw
    parts = []
    for probs, rev in ((fwd, False), (bwd, True)):
        for (r, v, kk, b, kd, lw) in probs:
            parts.append([_groups(z) for z in _rw_decays(r, kk, b, kd, lw, rev)] + [_groups(v)])
    at, bt, kt, rt, btw, ktw, decay, v = [jnp.concatenate(zs, axis=0) for zs in zip(*parts)]
    s = jnp.concatenate([_groups(z) for z in list(s_f) + list(s_b)], axis=0)
    nslab = 2 * nseq * ng
    shape = (nslab, ck, gw)
    lag = lax.broadcasted_iota(jnp.int32, shape, 1) - lax.broadcasted_iota(jnp.int32, shape, 2) % ck
    lag = jnp.where(lax.broadcasted_iota(jnp.int32, shape, 0) >= nseq * ng, -lag, lag)
    incl, strict = lag >= 0, lag > 0
    eye = (lag == 0).astype(F32)

    bdv = _bd(v)
    lhs = jnp.concatenate([at, rt], axis=1)
    lb, lk = _bdot(lhs, _bd(bt), "gtk,gnk->gtn"), _bdot(lhs, _bd(kt), "gtk,gnk->gtn")
    lab, mrb = jnp.where(strict, lb[:, :ck], 0.0), jnp.where(incl, lb[:, ck:], 0.0)
    lak, mrk = jnp.where(strict, lk[:, :ck], 0.0), jnp.where(incl, lk[:, ck:], 0.0)
    tinv = eye + lab
    pw = _bdot(lab, _bd(lab), "gtk,gkn->gtn")
    span = 4
    while span < ck:
        both = _bdot(jnp.concatenate([tinv, pw], axis=1), _bd(pw), "gtk,gkn->gtn")
        tinv, pw = tinv + both[:, :ck], both[:, ck:]
        span *= 2
    tinv = tinv + _bdot(tinv, _bd(pw), "gtk,gkn->gtn")
    vmix = _bdot(jnp.concatenate([lak, mrk], axis=1), bdv, "gtk,gkn->gtn")
    p1 = _bdot(tinv, _bd(at), "gtk,gkn->gtn")
    q1 = _bdot(tinv, _bd(vmix[:, :ck]), "gtk,gkn->gtn")
    smix = _bdot(jnp.concatenate([p1, rt], axis=1), _bd(s), "gtk,gnk->gtn")
    u = smix[:, :ck] + q1
    y = smix[:, ck:] + _bdot(mrb, _bd(u), "gtk,gkn->gtn") + vmix[:, ck:]
    z = _bdot(jnp.concatenate([u, v], axis=1), jnp.concatenate([btw, ktw], axis=1), "gtm,gtn->gmn")
    lane_head = lax.broadcasted_iota(jnp.int32, (nslab, B_HEAD, gw), 2) // B_HEAD
    upd = jnp.zeros_like(s)
    for h in range(RW_GROUP):
        upd = upd + jnp.where(lane_head == h, z[:, h * B_HEAD:(h + 1) * B_HEAD], 0.0)
    s_new = s * decay + upd
    lanes = lambda x, q: jnp.concatenate([x[q * ng + g] for g in range(ng)], axis=1)
    seqs = range(nseq)
    return ([lanes(y, q) for q in seqs], [lanes(y, nseq + q) for q in seqs],
            [lanes(s_new, q) for q in seqs], [lanes(s_new, nseq + q) for q in seqs])


def _rwscan_body(rf_ref, vf_ref, kkf_ref, bf_ref, kdf_ref, lwf_ref, rb_ref, vb_ref, kkb_ref, bb_ref, kdb_ref, lwb_ref,
                 s0f_ref, s0b_ref, yf_ref, yb_ref, sTf_ref, sTb_ref, sf_ref, sb_ref, *, nc, nseq):
    c = pl.program_id(0)

    @pl.when(c == 0)
    def _():
        sf_ref[...] = s0f_ref[...]
        sb_ref[...] = s0b_ref[...]

    seqs = range(nseq)
    yf, yb, sf, sb = _rw_chunks(
        [(rf_ref[q], vf_ref[q], kkf_ref[q], bf_ref[q], kdf_ref[q], lwf_ref[q]) for q in seqs],
        [(rb_ref[q], vb_ref[q], kkb_ref[q], bb_ref[q], kdb_ref[q], lwb_ref[q]) for q in seqs],
        [sf_ref[q] for q in seqs], [sb_ref[q] for q in seqs])
    for q in seqs:
        yf_ref[q] = yf[q]
        yb_ref[q] = yb[q]
        sf_ref[q] = sf[q]
        sb_ref[q] = sb[q]

    @pl.when(c == nc - 1)
    def _():
        for q in seqs:
            sTf_ref[q] = sf[q]
            sTb_ref[q] = sb[q]


def _rwkv_scan(r, v, kk, fwd, bwd, s0f, s0b):
    bsz, n, w = r.shape
    ck = RW_CHUNK
    assert n % ck == 0 and ck == B_HEAD and w % (RW_GROUP * B_HEAD) == 0
    nc = n // ck
    seq_f = pl.BlockSpec((bsz, ck, w), lambda c: (0, c, 0))
    seq_b = pl.BlockSpec((bsz, ck, w), lambda c: (0, nc - 1 - c, 0))
    st = pl.BlockSpec((bsz, B_HEAD, w), lambda c: (0, 0, 0))
    y_shape = jax.ShapeDtypeStruct((bsz, n, w), F32)
    s_shape = jax.ShapeDtypeStruct((bsz, B_HEAD, w), F32)
    (lwf, bf, kf), (lwb, bb, kb) = fwd, bwd
    return pl.pallas_call(
        functools.partial(_rwscan_body, nc=nc, nseq=bsz),
        grid=(nc,),
        in_specs=[seq_f] * 6 + [seq_b] * 6 + [st, st],
        out_specs=[seq_f, seq_b, st, st],
        out_shape=[y_shape, y_shape, s_shape, s_shape],
        scratch_shapes=[pltpu.VMEM((bsz, B_HEAD, w), F32), pltpu.VMEM((bsz, B_HEAD, w), F32)],
        compiler_params=_cp(("arbitrary",)),
        name="rwkv_scan",
    )(r, v, kk, bf, kf, lwf, r, v, kk, bb, kb, lwb, s0f, s0b)


def _rwpost_body(yf_ref, yb_ref, g_ref, bonus_ref, lg_ref, lb_ref, sel_ref, selt_ref, o_ref):
    sel, selt = sel_ref[...], selt_ref[...]
    head_mean = lambda z: _mm_sel(_mm_sel(z, sel), selt) * (1.0 / B_HEAD)
    y = yf_ref[0] + yb_ref[0]
    d = y - head_mean(y)
    y = d * lax.rsqrt(head_mean(d * d) + B_GN_EPS)
    y = y * lg_ref[...] + lb_ref[...] + bonus_ref[0]
    o_ref[0] = (y * g_ref[0].astype(F32)).astype(o_ref.dtype)


def _head_selectors(bw):
    sel_np = np.zeros((bw, LANE), np.float32)
    sel_np[np.arange(bw), np.arange(bw) // B_HEAD] = 1.0
    return jnp.asarray(sel_np, BF16), jnp.asarray(sel_np.T, BF16)


def _rwkv_post(y_f, y_b, g, bonus, lnx_g, lnx_b):
    bsz, n, bw = y_f.shape
    tm = _pick(n, 256)
    tok = pl.BlockSpec((1, tm, bw), lambda b, i: (b, i, 0))
    vec = pl.BlockSpec((1, bw), lambda b, i: (0, 0))
    sel, selt = _head_selectors(bw)
    return pl.pallas_call(
        _rwpost_body,
        grid=(bsz, n // tm),
        in_specs=[tok, tok, tok, tok, vec, vec,
                  pl.BlockSpec((bw, LANE), lambda b, i: (0, 0)), pl.BlockSpec((LANE, bw), lambda b, i: (0, 0))],
        out_specs=tok,
        out_shape=jax.ShapeDtypeStruct((bsz, n, bw), BF16),
        compiler_params=_cp(("parallel", "parallel")),
        name="rwkv_post",
    )(y_f, y_b, g, bonus, lnx_g.reshape(1, bw).astype(F32), lnx_b.reshape(1, bw).astype(F32), sel, selt)


def _rwkv_run(prep, states0, p):
    y_f, y_b, s_f, s_b = _rwkv_scan(prep["r"], prep["v"], prep["kk"], prep["f"], prep["b"], states0[0], states0[1])
    return _rwkv_post(y_f, y_b, prep["g"], prep["bonus"], p["lnx_g"], p["lnx_b"]), (s_f, s_b)


HY_N2 = 128
HY_HALO = 8


def _hypre_body(x_ref, hp_ref, hn_ref, cw_ref, cb_ref, x0_ref, uv_ref, *, nb, tm, cw):
    i = pl.program_id(1)
    x = x_ref[0].astype(F32)
    row = lax.broadcasted_iota(jnp.int32, x.shape, 0)
    prev_edge = jnp.where(i > 0, hp_ref[0, HY_HALO - 1:HY_HALO, :].astype(F32), 0.0)
    next_edge = jnp.where(i < nb - 1, hn_ref[0, 0:1, :].astype(F32), 0.0)
    prev = jnp.where(row == 0, prev_edge, pltpu.roll(x, 1, 0))
    nxt = jnp.where(row == tm - 1, next_edge, pltpu.roll(x, tm - 1, 0))
    u = prev * cw_ref[0:1] + x * cw_ref[1:2] + nxt * cw_ref[2:3] + cb_ref[...]
    x0_ref[0] = u[:, :cw]
    uv_ref[0] = u[:, 2 * cw:3 * cw] * u[:, cw:2 * cw]


def _hyena_pre(p, conv_w, conv_b):
    bsz, n, _ = p.shape
    hc = conv_w.shape[1]
    cw = hc // 3
    tm = _pick(n, 256)
    nb = n // tm
    hb = tm // HY_HALO
    cwp = jnp.concatenate([conv_w.astype(F32), jnp.zeros((5, hc), F32)], axis=0)
    out = pl.BlockSpec((1, tm, cw), lambda b, i: (b, i, 0))
    return pl.pallas_call(
        functools.partial(_hypre_body, nb=nb, tm=tm, cw=cw),
        grid=(bsz, nb),
        in_specs=[pl.BlockSpec((1, tm, hc), lambda b, i: (b, i, 0)),
                  pl.BlockSpec((1, HY_HALO, hc), lambda b, i: (b, jnp.maximum(i * hb - 1, 0), 0)),
                  pl.BlockSpec((1, HY_HALO, hc), lambda b, i: (b, jnp.minimum((i + 1) * hb, nb * hb - 1), 0)),
                  pl.BlockSpec((8, hc), lambda b, i: (0, 0)),
                  pl.BlockSpec((1, hc), lambda b, i: (0, 0))],
        out_specs=[out, out],
        out_shape=[jax.ShapeDtypeStruct((bsz, n, cw), F32)] * 2,
        compiler_params=_cp(("parallel", "parallel")),
        name="hyena_pre",
    )(p, p, p, cwp, conv_b.reshape(1, hc).astype(F32))


def _hyfilt_body(z_ref, t_ref, z0_ref, w1_ref, w2_ref, w3_ref, w4_ref, w4b_ref, vec_ref, dl_ref,
                 k_ref, s_ref, *, half_tiles):
    i = pl.program_id(0)
    b1, b2, b3, fr = vec_ref[0:1], vec_ref[1:2], vec_ref[2:3], vec_ref[3:4]

    def mlp3(z):
        h = jnp.sin(fr * (_mm_x3(z, w1_ref[...]) + b1))
        h = jnp.sin(fr * (_mm_x3(h, w2_ref[...]) + b2))
        h = jnp.sin(fr * (_mm_x3(h, w3_ref[...]) + b3))
        lane = lax.broadcasted_iota(jnp.int32, h.shape, 1)
        return jnp.where(lane < HY_ORDER, h, 0.0), jnp.where(lane >= HY_ORDER, h, 0.0)

    h_lo, h_hi = mlp3(z_ref[...])
    raw = jnp.concatenate([_mm_x3(h_lo, w4_ref[...]), _mm_x3(h_hi, w4_ref[...])], axis=0)
    raw = raw * jnp.exp(-t_ref[...] * dl_ref[...])

    @pl.when(i == 0)
    def _():
        s_ref[...] = jnp.zeros_like(s_ref)

    s_ref[...] += jnp.sum(jnp.abs(raw), axis=0, keepdims=True)
    hb0 = _mm_x3(mlp3(z0_ref[...])[0], w4b_ref[...])
    row = lax.broadcasted_iota(jnp.int32, raw.shape, 0)
    raw = raw + jnp.where((row == 0) & (i == 0), hb0[0:1], 0.0)
    k_ref[...] = jnp.where((row == 0) & (i == half_tiles), 0.0, raw)


def _hyena_kernel_taps(n, p):
    cw = p["f_w4"].shape[1] // 2
    f32 = lambda a: a.astype(F32)
    row = jnp.arange(2 * n, dtype=jnp.int32)
    pos = jnp.where(row < n, row, (2 * n - row) % n).astype(F32)[:, None]
    tt = pos * jnp.float32(1.0 / (n - 1))
    bands = (HY_EMB - 1) // 2
    ang = (2 * math.pi / n) * pos * jnp.linspace(1e-4, bands - 1, bands, dtype=F32)[None]
    zz = jnp.concatenate([tt, jnp.cos(ang), -jnp.sin(ang), jnp.zeros((2 * n, HY_ORDER - HY_EMB), F32)], axis=-1)
    w1 = jnp.concatenate([f32(p["f_w1"]), jnp.zeros((HY_ORDER - HY_EMB, HY_ORDER), F32)], axis=0)
    twice = lambda w: jnp.kron(jnp.eye(2, dtype=F32), f32(w))
    vecs = jnp.stack([jnp.tile(f32(p[k]), 2) for k in ("f_b1", "f_b2", "f_b3", "f_freq")]
                     + [jnp.zeros((2 * HY_ORDER,), F32)] * 4)
    deltas = jnp.abs(jnp.linspace(math.log(HY_TARGET) / HY_SLOW_PCT, math.log(HY_TARGET) / HY_FAST_PCT, cw,
                                  dtype=F32)).reshape(1, cw)
    tr = _pick(n, 512)
    half_tiles = n // tr
    pk = 2 * HY_ORDER
    zp = zz.reshape(-1, 2, tr // 2, HY_ORDER).transpose(0, 2, 1, 3).reshape(-1, pk)
    full = lambda shape: pl.BlockSpec(shape, lambda i: tuple(0 for _ in shape))
    w4 = jnp.concatenate([f32(p["f_w4"])] * 2, axis=0)
    return pl.pallas_call(
        functools.partial(_hyfilt_body, half_tiles=half_tiles),
        grid=(2 * half_tiles,),
        in_specs=[pl.BlockSpec((tr // 2, pk), lambda i: (i, 0)),
                  pl.BlockSpec((tr, 1), lambda i: (i, 0)),
                  full((8, pk)), full((pk, pk)), full((pk, pk)), full((pk, pk)),
                  pl.BlockSpec((pk, cw), lambda i: (0, i // half_tiles)),
                  pl.BlockSpec((pk, cw), lambda i: (0, 1)),
                  full((8, pk)), full((1, cw))],
        out_specs=[pl.BlockSpec((tr, cw), lambda i: (i, 0)), pl.BlockSpec((1, cw), lambda i: (0, 0))],
        out_shape=[jax.ShapeDtypeStruct((2 * n, cw), F32), jax.ShapeDtypeStruct((1, cw), F32)],
        compiler_params=_cp(("arbitrary",)),
        name="hyena_filter",
    )(zp, tt, jnp.broadcast_to(jnp.tile(zz[0:1], (1, 2)), (8, pk)), twice(w1), twice(p["f_w2"]), twice(p["f_w3"]),
      w4, w4, vecs, deltas)


def _dft_consts(n):
    nn = 2 * n
    n2 = HY_N2
    n1 = nn // n2
    n1h = n1 // 2 + 1
    nf = -(-n1h // 8) * 8
    live = (np.arange(nf) < n1h).astype(np.float64)
    a1 = 2 * np.pi * np.outer(np.arange(nf), np.arange(n1)) / n1
    f1 = np.concatenate([np.cos(a1), -np.sin(a1)], axis=0) * np.tile(live, 2)[:, None]
    wgt = live * np.where((np.arange(nf) == 0) | (np.arange(nf) == n1 // 2), 1.0, 2.0)
    f1inv = np.concatenate([np.cos(a1.T) * wgt, -np.sin(a1.T) * wgt], axis=1) / nn
    a2 = 2 * np.pi * np.outer(np.arange(n2), np.arange(n2)) / n2
    c2, s2 = np.cos(a2), -np.sin(a2)
    m2 = np.block([[c2, -s2], [s2, c2]])
    m2inv = np.block([[c2, s2], [-s2, c2]])
    at = 2 * np.pi * np.outer(np.arange(n2), np.arange(nf)) / nn
    tw = np.stack([np.cos(at), -np.sin(at)])
    c = lambda a: jnp.asarray(a.astype(np.float32))
    return dict(n1=n1, n2=n2, nf=nf, f1=c(f1), f1inv=c(f1inv), m2=c(m2), m2inv=c(m2inv),
                tw_s2=c(tw[:, :, :, None]),
                tw_f1=c(np.transpose(tw, (0, 2, 1))[:, :, :, None]))


HY_S2 = 8
HY_CTILE = 512


def _dft1_body(f_ref, x_ref, tw_ref, o_ref, *, nf):
    x = jnp.swapaxes(x_ref[0], 0, 1)
    fh, fl, _ = _split3(f_ref[...])
    d = lambda u, w: jnp.dot(u, w, preferred_element_type=F32)
    re, im = [], []
    for jj in range(HY_S2):
        xh, xl, _ = _split3(x[jj])
        a = d(fh, xh) + (d(fh, xl) + d(fl, xh))
        ar, ai = a[:nf], a[nf:]
        twr, twi = tw_ref[0, jj], tw_ref[1, jj]
        re.append(ar * twr - ai * twi)
        im.append(ar * twi + ai * twr)
    o_ref[0, 0] = jnp.swapaxes(jnp.stack(re), 0, 1)
    o_ref[0, 1] = jnp.swapaxes(jnp.stack(im), 0, 1)


def _dft_stage1(x, consts, rows):
    bsz, _, cw = x.shape
    n2, nf = consts["n2"], consts["nf"]
    xv = x.reshape(bsz, rows, n2, cw)
    f1 = consts["f1"][:, :rows]
    tc = _pick(cw, HY_CTILE, LANE)
    return pl.pallas_call(
        functools.partial(_dft1_body, nf=nf),
        grid=(bsz, n2 // HY_S2, cw // tc),
        in_specs=[pl.BlockSpec((2 * nf, rows), lambda b, j, c: (0, 0)),
                  pl.BlockSpec((1, rows, HY_S2, tc), lambda b, j, c: (b, 0, j, c)),
                  pl.BlockSpec((2, HY_S2, nf, 1), lambda b, j, c: (0, j, 0, 0))],
        out_specs=pl.BlockSpec((1, 2, nf, HY_S2, tc), lambda b, j, c: (b, 0, 0, j, c)),
        out_shape=jax.ShapeDtypeStruct((bsz, 2, nf, n2, cw), F32),
        compiler_params=_cp(("parallel", "parallel", "parallel")),
        name="hyena_dft1",
    )(f1, xv, consts["tw_s2"])


HY_F1 = 4


def _dftmid_body(a_ref, m2_ref, m2i_ref, h_ref, tw_ref, is_ref, o_ref, *, n2, filt):
    for r in range(HY_F1):
        a = a_ref[0, :, r].reshape(2 * n2, a_ref.shape[-1])
        x = _mm_x3(m2_ref[...], a)
        xr, xi = x[:n2], x[n2:]
        if filt:
            o_ref[0, 0, r] = xr * is_ref[...]
            o_ref[0, 1, r] = xi * is_ref[...]
            continue
        hr, hi = h_ref[0, 0, r], h_ref[0, 1, r]
        y = jnp.concatenate([xr * hr - xi * hi, xr * hi + xi * hr], axis=0)
        z = _mm_x3(m2i_ref[...], y)
        zr, zi = z[:n2], z[n2:]
        twr, twi = tw_ref[0, r], tw_ref[1, r]
        o_ref[0, 0, r] = zr * twr + zi * twi
        o_ref[0, 1, r] = zi * twr - zr * twi


def _dft_mid(a, h, inv_s, consts, filt):
    bsz = a.shape[0]
    n1, n2 = consts["nf"], consts["n2"]
    cw = a.shape[-1]
    blk = pl.BlockSpec((1, 2, HY_F1, n2, cw), lambda f, b: (b, 0, f, 0, 0))
    hblk = pl.BlockSpec((1, 2, HY_F1, n2, cw),
                        (lambda f, b: (0, 0, 0, 0, 0)) if filt else (lambda f, b: (0, 0, f, 0, 0)))
    return pl.pallas_call(
        functools.partial(_dftmid_body, n2=n2, filt=filt),
        grid=(n1 // HY_F1, bsz),
        in_specs=[blk,
                  pl.BlockSpec((2 * n2, 2 * n2), lambda f, b: (0, 0)),
                  pl.BlockSpec((2 * n2, 2 * n2), lambda f, b: (0, 0)),
                  hblk,
                  pl.BlockSpec((2, HY_F1, n2, 1), lambda f, b: (0, f, 0, 0)),
                  pl.BlockSpec((1, cw), lambda f, b: (0, 0))],
        out_specs=blk,
        out_shape=jax.ShapeDtypeStruct((bsz, 2, n1, n2, cw), F32),
        compiler_params=_cp(("parallel", "parallel")),
        name="hyena_dftmid_" + ("filter" if filt else "conv"),
    )(a, consts["m2"], consts["m2inv"], h, consts["tw_f1"], inv_s)


def _dft3_body(f_ref, z_ref, x0_ref, uv_ref, bias_ref, o_ref):
    zr, zi = jnp.swapaxes(z_ref[0, 0], 0, 1), jnp.swapaxes(z_ref[0, 1], 0, 1)
    x0, uv = jnp.swapaxes(x0_ref[0], 0, 1), jnp.swapaxes(uv_ref[0], 0, 1)
    fh, fl, _ = _split3(f_ref[...])
    d = lambda u, w: jnp.dot(u, w, preferred_element_type=F32)
    out = []
    for jj in range(HY_S2):
        zh, zl, _ = _split3(jnp.concatenate([zr[jj], zi[jj]], axis=0))
        y = d(fh, zh) + (d(fh, zl) + d(fl, zh))
        out.append(x0[jj] * (y + uv[jj] * bias_ref[...]))
    o_ref[0] = jnp.swapaxes(jnp.stack(out), 0, 1)


def _dft_stage1_inv(z, x0, uv, bias, consts):
    bsz, n, cw = x0.shape
    nf, n2 = consts["nf"], consts["n2"]
    hr = consts["n1"] // 2
    tokv = lambda t: t.reshape(bsz, hr, n2, cw)
    tc = _pick(cw, HY_CTILE, LANE)
    tile = pl.BlockSpec((1, hr, HY_S2, tc), lambda b, j, c: (b, 0, j, c))
    out = pl.pallas_call(
        _dft3_body,
        grid=(bsz, n2 // HY_S2, cw // tc),
        in_specs=[pl.BlockSpec((hr, 2 * nf), lambda b, j, c: (0, 0)),
                  pl.BlockSpec((1, 2, nf, HY_S2, tc), lambda b, j, c: (b, 0, 0, j, c)),
                  tile, tile,
                  pl.BlockSpec((1, tc), lambda b, j, c: (0, c))],
        out_specs=tile,
        out_shape=jax.ShapeDtypeStruct((bsz, hr, n2, cw), F32),
        compiler_params=_cp(("parallel", "parallel", "parallel")),
        name="hyena_dft3",
    )(consts["f1inv"][:hr], z, tokv(x0), tokv(uv), bias.reshape(1, cw).astype(F32))
    return out.reshape(bsz, n, cw)


def _hyena(p_lat, p):
    n = p_lat.shape[1]
    consts = _dft_consts(n)
    n1 = consts["n1"]
    x0, uv = _hyena_pre(p_lat, p["conv_w"], p["conv_b"])
    taps, sabs = _hyena_kernel_taps(n, p)
    cw = taps.shape[1]
    ones = jnp.ones((1, cw), F32)
    hk = _dft_stage1(taps[None], consts, n1)
    hspec = _dft_mid(hk, jnp.zeros((1, 2, HY_F1, HY_N2, cw), F32), 1.0 / sabs, consts, True)
    a = _dft_stage1(uv, consts, n1 // 2)
    z = _dft_mid(a, hspec, ones, consts, False)
    return _dft_stage1_inv(z, x0, uv, p["hy_bias"], consts)


def _mod_vectors(c, c_ctx, p):
    bsz, d = c.shape
    cvec = jnp.concatenate([c, c_ctx[None], jnp.zeros((8 - bsz - 1, d), c.dtype)], axis=0).astype(F32)
    m = _mods(cvec, p["w_mod"].astype(F32), p["b_mod"].astype(F32))
    lat = [v[:, None, :] for v in jnp.split(m[:bsz], N_MOD, axis=-1)]
    cx = [jnp.broadcast_to(v[:, None, :], (bsz, 1, d)) for v in jnp.split(m[bsz:bsz + 1], N_MOD, axis=-1)]
    return lat, cx


def _affine(norm_g, shift, scale):
    return norm_g.astype(F32) * (1.0 + scale), shift


def kernel(x, c, ctx, c_ctx,
           l0_w_mod, l0_b_mod, l0_norm1_g, l0_norm2_g, l0_w_in, l0_lam_q1, l0_lam_k1, l0_lam_q2, l0_lam_k2,
           l0_subln_g, l0_mu_prev, l0_mu_next, l0_w0_f, l0_w2_f, l0_a0_f, l0_a2_f, l0_w0_b, l0_w2_b, l0_a0_b,
           l0_a2_b, l0_g2, l0_k_k, l0_k_a, l0_r_k, l0_lnx_g, l0_lnx_b, l0_w_out, l0_mlp_w1, l0_mlp_w2,
           l1_w_mod, l1_b_mod, l1_norm1_g, l1_norm2_g, l1_w_in, l1_conv_w, l1_conv_b, l1_f_w1, l1_f_b1,
           l1_f_w2, l1_f_b2, l1_f_w3, l1_f_b3, l1_f_w4, l1_f_freq, l1_hy_bias, l1_q_norm_g, l1_k_norm_g,
           l1_w_out, l1_mlp_w1, l1_mlp_w2, final_g):
    bsz, n, d = x.shape
    x = x.astype(F32)
    ctx_s = ctx.astype(F32)
    bf = lambda w: w.astype(BF16)

    p0 = dict(w_mod=l0_w_mod, b_mod=l0_b_mod, mu_prev=l0_mu_prev, mu_next=l0_mu_next, w0_f=l0_w0_f, w2_f=l0_w2_f,
              a0_f=l0_a0_f, a2_f=l0_a2_f, w0_b=l0_w0_b, w2_b=l0_w2_b, a0_b=l0_a0_b, a2_b=l0_a2_b, g2=l0_g2,
              k_k=l0_k_k, k_a=l0_k_a, r_k=l0_r_k, lnx_g=l0_lnx_g, lnx_b=l0_lnx_b)
    (sh1, sc1, g1, sh2, sc2, g2), (csh1, csc1, cg1, csh2, csc2, cg2) = _mod_vectors(c, c_ctx, p0)
    aw = d // 2
    w_rw = l0_w_in[:, 3 * aw:]
    rw_cols = -(-(w_rw.shape[1] + 2 * LANE - B_GATE_RANK) // 512) * 512
    w_in0 = bf(jnp.concatenate([w_rw, jnp.zeros((d, rw_cols - w_rw.shape[1]), w_rw.dtype), l0_w_in[:, :3 * aw]],
                               axis=1))
    a1, b1 = _affine(l0_norm1_g, sh1, sc1)
    ca1, cb1 = _affine(l0_norm1_g, csh1, csc1)
    p_lat0 = _inproj(x, a1, b1, w_in0)
    p_ctx0 = _inproj(ctx_s, ca1, cb1, w_in0)
    lam_init = 0.8 - 0.6 * math.exp(-0.3 * 0)
    lam = (jnp.exp(jnp.sum(l0_lam_q1 * l0_lam_k1).astype(F32))
           - jnp.exp(jnp.sum(l0_lam_q2 * l0_lam_k2).astype(F32)) + lam_init)
    oa_lat, oa_ctx = _diff_attention(p_lat0, p_ctx0, rw_cols, aw, lam, lam_init, l0_subln_g)
    nh, hd = l0_r_k.shape
    zero = jnp.zeros((bsz, hd, nh * hd), F32)
    y_ctx, states_c = _rwkv_run(_rwkv_prepare(p_ctx0, p0, rw_cols), (zero, zero), p0)
    y_lat, _ = _rwkv_run(_rwkv_prepare(p_lat0, p0, rw_cols), states_c, p0)
    w_out0 = bf(l0_w_out)
    x = _outproj(oa_lat, y_lat, w_out0, x, g1)
    ctx_s = _outproj(oa_ctx, y_ctx, w_out0, ctx_s, cg1)
    w1, w2 = bf(l0_mlp_w1), bf(l0_mlp_w2)
    a2, b2 = _affine(l0_norm2_g, sh2, sc2)
    ca2, cb2 = _affine(l0_norm2_g, csh2, csc2)
    x = _mlp(x, a2, b2, g2, w1, w2)
    ctx_s = _mlp(ctx_s, ca2, cb2, cg2, w1, w2)

    p1 = dict(w_mod=l1_w_mod, b_mod=l1_b_mod, conv_w=l1_conv_w, conv_b=l1_conv_b, f_w1=l1_f_w1, f_b1=l1_f_b1,
              f_w2=l1_f_w2, f_b2=l1_f_b2, f_w3=l1_f_w3, f_b3=l1_f_b3, f_w4=l1_f_w4, f_freq=l1_f_freq,
              hy_bias=l1_hy_bias)
    (sh1, sc1, g1, sh2, sc2, g2), (csh1, csc1, _, _, _, _) = _mod_vectors(c, c_ctx, p1)
    hy_cols = l1_conv_w.shape[1]
    kv_cols = 2 * (2 * D_HEAD)
    w_in1 = bf(l1_w_in)
    a1, b1 = _affine(l1_norm1_g, sh1, sc1)
    ca1, cb1 = _affine(l1_norm1_g, csh1, csc1)
    p_lat = _inproj(x, a1, b1, w_in1)
    pkv_ctx = _inproj(ctx_s, ca1, cb1, w_in1[:, -kv_cols:])
    o_hy = _hyena(p_lat, p1)
    o_at = _gqa_attention(p_lat, pkv_ctx, hy_cols, l1_q_norm_g, l1_k_norm_g)
    x = _outproj(o_hy, o_at, bf(l1_w_out), x, g1)
    a2, b2 = _affine(l1_norm2_g, sh2, sc2)
    return _mlp(x, a2, b2, g2, bf(l1_mlp_w1), bf(l1_mlp_w2), final_g=final_g)
```

```python
import functools
import math

import numpy as np
import jax
import jax.numpy as jnp
from jax import lax
from jax.experimental import pallas as pl
from jax.experimental.pallas import tpu as pltpu

F32 = jnp.float32
BF16 = jnp.bfloat16
HI = lax.Precision.HIGHEST

NORM_EPS = 1e-6
ROPE_THETA = 10000.0
GRID_W = 64
N_MOD = 6
A_QK_DIM = 64
A_V_DIM = 128
B_HEAD = 64
B_DECAY_RANK = 64
B_ICL_RANK = 64
B_GATE_RANK = 160
B_GN_EPS = 64e-5
HY_EMB = 33
HY_ORDER = 64
HY_TARGET = 1e-2
HY_FAST_PCT = 0.3
HY_SLOW_PCT = 1.5
D_HEAD = 128
D_KV_GROUP = 4

LANE = 128
VMEM_LIMIT = 56 * 1024 * 1024
LOG2E = 1.4426950408889634


def _cp(sem, vmem=VMEM_LIMIT):
    return pltpu.CompilerParams(dimension_semantics=sem, vmem_limit_bytes=vmem)


def _pick(n, pref, step=8):
    t = max(step, min(n, pref) // step * step)
    while n % t:
        t -= step
    return t


def _mods_body(c_ref, w_ref, b_ref, o_ref):
    c = c_ref[...]
    s = c * jax.nn.sigmoid(c)
    o_ref[...] = _mm_x3(s, w_ref[...]) + b_ref[...]


def _mods(cvec, w_mod, b_mod):
    m, d = cvec.shape
    n = w_mod.shape[1]
    tn = _pick(n, 1024, LANE)
    return pl.pallas_call(
        _mods_body,
        grid=(n // tn,),
        in_specs=[pl.BlockSpec((m, d), lambda j: (0, 0)),
                  pl.BlockSpec((d, tn), lambda j: (0, j)),
                  pl.BlockSpec((1, tn), lambda j: (0, j))],
        out_specs=pl.BlockSpec((m, tn), lambda j: (0, j)),
        out_shape=jax.ShapeDtypeStruct((m, n), F32),
        compiler_params=_cp(("arbitrary",)),
        name="mods",
    )(cvec, w_mod, b_mod.reshape(1, n))


def _norm_mod(x, a, b):
    ms = jnp.mean(x * x, axis=-1, keepdims=True)
    return x * lax.rsqrt(ms + NORM_EPS) * a + b


def _inproj_body(x_ref, a_ref, b_ref, w_ref, o_ref, xn_ref):
    @pl.when(pl.program_id(2) == 0)
    def _():
        xn_ref[...] = _norm_mod(x_ref[0], a_ref[0], b_ref[0]).astype(BF16)

    o_ref[0] = jnp.dot(xn_ref[...], w_ref[...], preferred_element_type=F32).astype(o_ref.dtype)


def _inproj(x, a, b, w, out_dtype=BF16, tm_pref=1024, tn_pref=512):
    bsz, n, d = x.shape
    nn = w.shape[1]
    tm, tn = _pick(n, tm_pref), _pick(nn, tn_pref, LANE)
    return pl.pallas_call(
        _inproj_body,
        grid=(bsz, n // tm, nn // tn),
        in_specs=[pl.BlockSpec((1, tm, d), lambda bi, i, j: (bi, i, 0)),
                  pl.BlockSpec((1, 1, d), lambda bi, i, j: (bi, 0, 0)),
                  pl.BlockSpec((1, 1, d), lambda bi, i, j: (bi, 0, 0)),
                  pl.BlockSpec((d, tn), lambda bi, i, j: (0, j))],
        out_specs=pl.BlockSpec((1, tm, tn), lambda bi, i, j: (bi, i, j)),
        out_shape=jax.ShapeDtypeStruct((bsz, n, nn), out_dtype),
        scratch_shapes=[pltpu.VMEM((tm, d), BF16)],
        compiler_params=_cp(("parallel", "parallel", "arbitrary")),
        name="inproj",
    )(x, a, b, w)


def _outproj_body(oa_ref, ob_ref, wa_ref, wb_ref, x_ref, g_ref, y_ref):
    acc = jnp.dot(oa_ref[0].astype(BF16), wa_ref[...], preferred_element_type=F32)
    acc += jnp.dot(ob_ref[0].astype(BF16), wb_ref[...], preferred_element_type=F32)
    y_ref[0] = x_ref[0] + g_ref[0] * acc


def _outproj(oa, ob, w, x, g, tm_pref=1024, tn_pref=1024):
    bsz, n, ka = oa.shape
    kb = ob.shape[2]
    d = w.shape[1]
    tm, tn = _pick(n, tm_pref), _pick(d, tn_pref, LANE)
    return pl.pallas_call(
        _outproj_body,
        grid=(bsz, n // tm, d // tn),
        in_specs=[pl.BlockSpec((1, tm, ka), lambda bi, i, j: (bi, i, 0)),
                  pl.BlockSpec((1, tm, kb), lambda bi, i, j: (bi, i, 0)),
                  pl.BlockSpec((ka, tn), lambda bi, i, j: (0, j)),
                  pl.BlockSpec((kb, tn), lambda bi, i, j: (0, j)),
                  pl.BlockSpec((1, tm, tn), lambda bi, i, j: (bi, i, j)),
                  pl.BlockSpec((1, 1, tn), lambda bi, i, j: (bi, 0, j))],
        out_specs=pl.BlockSpec((1, tm, tn), lambda bi, i, j: (bi, i, j)),
        out_shape=jax.ShapeDtypeStruct((bsz, n, d), F32),
        compiler_params=_cp(("parallel", "parallel", "parallel")),
        name="outproj",
    )(oa, ob, w[:ka], w[ka:], x, g)


def _mlp_body(x_ref, a_ref, b_ref, g_ref, w1_ref, w2_ref, fg_ref, y_ref, xn_ref, acc_ref, *, nf, final):
    f = pl.program_id(2)

    @pl.when(f == 0)
    def _():
        xn_ref[...] = _norm_mod(x_ref[0], a_ref[0], b_ref[0]).astype(BF16)
        acc_ref[...] = jnp.zeros_like(acc_ref)

    h = jnp.dot(xn_ref[...], w1_ref[...], preferred_element_type=F32)
    h = jnp.square(jnp.maximum(h, 0.0)).astype(BF16)
    acc_ref[...] += jnp.dot(h, w2_ref[...], preferred_element_type=F32)

    @pl.when(f == nf - 1)
    def _():
        y = x_ref[0] + g_ref[0] * acc_ref[...]
        if final:
            ms = jnp.mean(y * y, axis=-1, keepdims=True)
            y = y * lax.rsqrt(ms + NORM_EPS) * fg_ref[...]
        y_ref[0] = y


def _mlp(x, a, b, g, w1, w2, final_g=None, tm_pref=512, tf_pref=1024):
    bsz, n, d = x.shape
    dff = w1.shape[1]
    tm, tf = _pick(n, tm_pref), _pick(dff, tf_pref, LANE)
    nf = dff // tf
    final = final_g is not None
    fg = (final_g if final else jnp.ones((d,), F32)).reshape(1, d).astype(F32)
    return pl.pallas_call(
        functools.partial(_mlp_body, nf=nf, final=final),
        grid=(bsz, n // tm, nf),
        in_specs=[pl.BlockSpec((1, tm, d), lambda bi, i, f: (bi, i, 0)),
                  pl.BlockSpec((1, 1, d), lambda bi, i, f: (bi, 0, 0)),
                  pl.BlockSpec((1, 1, d), lambda bi, i, f: (bi, 0, 0)),
                  pl.BlockSpec((1, 1, d), lambda bi, i, f: (bi, 0, 0)),
                  pl.BlockSpec((d, tf), lambda bi, i, f: (0, f)),
                  pl.BlockSpec((tf, d), lambda bi, i, f: (f, 0)),
                  pl.BlockSpec((1, d), lambda bi, i, f: (0, 0))],
        out_specs=pl.BlockSpec((1, tm, d), lambda bi, i, f: (bi, i, 0)),
        out_shape=jax.ShapeDtypeStruct((bsz, n, d), F32),
        scratch_shapes=[pltpu.VMEM((tm, d), BF16), pltpu.VMEM((tm, d), F32)],
        compiler_params=_cp(("parallel", "parallel", "arbitrary")),
        name="mlp",
    )(x, a, b, g, w1, w2, fg)


def _rope_tables(n, dim, reps):
    rows = n // GRID_W
    row = jnp.repeat(jnp.arange(rows, dtype=F32), GRID_W)
    col = jnp.tile(jnp.arange(GRID_W, dtype=F32), rows)
    half = dim // 2
    inv = ROPE_THETA ** (-jnp.arange(0, half, 2, dtype=F32) / half)
    ar, ac = row[:, None] * inv, col[:, None] * inv
    cos = jnp.concatenate([jnp.cos(ar), jnp.cos(ar), jnp.cos(ac), jnp.cos(ac)], axis=-1)
    sin = jnp.concatenate([-jnp.sin(ar), jnp.sin(ar), -jnp.sin(ac), jnp.sin(ac)], axis=-1)
    return jnp.tile(cos, (1, reps)), jnp.tile(sin, (1, reps))


def _swap_matrix(dim, reps):
    q = dim // 4
    width = dim * reps
    p = np.zeros((width, width), np.float32)
    for j in range(width):
        base, r = (j // dim) * dim, j % dim
        axis, which, f = r // (2 * q), (r % (2 * q)) // q, r % q
        p[base + axis * 2 * q + (1 - which) * q + f, j] = 1.0
    return p


def _rope_norm(x, cos_ref, sin_ref, p_ref, g_ref, gs_ref, *, norm, rope, scale=1.0):
    y = x.astype(F32)
    if rope:
        ys = jnp.dot(x, p_ref[...], preferred_element_type=F32)
    if norm:
        rs = lax.rsqrt(jnp.mean(y * y, axis=-1, keepdims=True) + NORM_EPS)
        y = y * rs * g_ref[...]
        if rope:
            ys = ys * rs * gs_ref[...]
    if rope:
        y = y * cos_ref[...] + ys * sin_ref[...]
    return y * scale if scale != 1.0 else y


def _rope_operands(n, dim, gain, rope):
    reps = LANE // dim
    if rope:
        cos, sin = _rope_tables(n, dim, reps)
        pm = jnp.asarray(_swap_matrix(dim, reps), BF16)
    else:
        cos = sin = jnp.zeros((n, LANE), F32)
        pm = jnp.zeros((LANE, LANE), BF16)
    if gain is not None:
        g = jnp.tile(gain.astype(F32), reps).reshape(1, LANE)
        gs = jnp.dot(g, jnp.asarray(_swap_matrix(dim, reps)), precision=HI)
    else:
        g = gs = jnp.ones((1, LANE), F32)
    return cos, sin, pm, g, gs


def _kprep_body(xl_ref, xc_ref, cos_ref, sin_ref, p_ref, g_ref, gs_ref, o_ref, *, heads, norm, nb):
    is_lat = pl.program_id(1) < nb
    for h in range(heads):
        sl = slice(h * LANE, (h + 1) * LANE)
        x = jnp.where(is_lat, xl_ref[0, :, sl], xc_ref[0, :, sl])
        y = _rope_norm(x, cos_ref, sin_ref, p_ref, g_ref, gs_ref, norm=norm, rope=True)
        o_ref[0, :, sl] = y.astype(o_ref.dtype)


def _kprep(p_lat, col_lat, p_ctx, col_ctx, heads, *, dim, gain=None):
    bsz, n, _ = p_lat.shape
    nctx = p_ctx.shape[1]
    tq = math.gcd(_pick(n, 512), nctx)
    nb, nbc = n // tq, nctx // tq
    hb = heads
    while col_lat % (hb * LANE) or col_ctx % (hb * LANE):
        hb //= 2
    w = hb * LANE
    cbl, cbc = col_lat // w, col_ctx // w
    cos, sin, pm, g, gs = _rope_operands(n, dim, gain, True)
    cos = jnp.concatenate([cos, jnp.ones((nctx, LANE), F32)], axis=0)
    sin = jnp.concatenate([sin, jnp.zeros((nctx, LANE), F32)], axis=0)
    const = lambda shape: pl.BlockSpec(shape, lambda b, i, j: (0, 0))
    return pl.pallas_call(
        functools.partial(_kprep_body, heads=hb, norm=gain is not None, nb=nb),
        grid=(bsz, nb + nbc, heads // hb),
        in_specs=[pl.BlockSpec((1, tq, w), lambda b, i, j: (b, jnp.minimum(i, nb - 1), cbl + j)),
                  pl.BlockSpec((1, tq, w), lambda b, i, j: (b, jnp.maximum(i - nb, 0), cbc + j)),
                  pl.BlockSpec((tq, LANE), lambda b, i, j: (i, 0)),
                  pl.BlockSpec((tq, LANE), lambda b, i, j: (i, 0)),
                  const((LANE, LANE)), const((1, LANE)), const((1, LANE))],
        out_specs=pl.BlockSpec((1, tq, w), lambda b, i, j: (b, i, j)),
        out_shape=jax.ShapeDtypeStruct((bsz, n + nctx, heads * LANE), BF16),
        compiler_params=_cp(("parallel", "parallel", "parallel")),
        name="kprep",
    )(p_lat, p_ctx, cos, sin, pm, g, gs)


FLASH_KEYS = 1280
FLASH_HEADROOM = 64.0


def _knorm_body(k_ref, o_ref, *, heads):
    for h in range(heads):
        k = k_ref[0, :, h * LANE:(h + 1) * LANE].astype(F32)
        nrm2 = jnp.max(jnp.sum(k * k, axis=1, keepdims=True), axis=0, keepdims=True)
        o_ref[0, h] = jnp.broadcast_to(jnp.sqrt(nrm2), (8, LANE))


def _key_block_norms(k, kcol0, heads, ts):
    bsz, nk, _ = k.shape
    nsub = nk // ts
    hb = heads
    while kcol0 % (hb * LANE):
        hb //= 2
    kb = kcol0 // (hb * LANE)
    out = pl.pallas_call(
        functools.partial(_knorm_body, heads=hb),
        grid=(bsz, heads // hb, nsub),
        in_specs=[pl.BlockSpec((1, ts, hb * LANE), lambda b, j, c: (b, c, kb + j))],
        out_specs=pl.BlockSpec((1, hb, 8, LANE), lambda b, j, c: (b, j, c, 0)),
        out_shape=jax.ShapeDtypeStruct((bsz, heads, nsub * 8, LANE), F32),
        compiler_params=_cp(("parallel", "parallel", "parallel")),
        name="key_block_norms",
    )(k)
    return out[:, :, ::8, 0].reshape(-1)


def _flash_body(kmax_ref, qr_ref, cos_ref, sin_ref, pm_ref, g_ref, gs_ref, k_ref, v_ref, e1_ref, e2_ref, o_ref,
                q_ref, m_ref, l_ref, acc_ref, thr_ref, *, ts, nsub, mode, tq, norm, rope, scale):
    base = (pl.program_id(0) * pl.num_programs(1) + pl.program_id(1)) * nsub
    prep = functools.partial(_rope_norm, cos_ref=cos_ref, sin_ref=sin_ref, p_ref=pm_ref, g_ref=g_ref, gs_ref=gs_ref,
                             norm=norm, rope=rope, scale=scale)
    if mode == "diff":
        y = prep(qr_ref[0])
        lane = lax.broadcasted_iota(jnp.int32, y.shape, 1)
        q_ref[:tq] = jnp.where(lane < A_QK_DIM, y, 0.0).astype(BF16)
        q_ref[tq:] = jnp.where(lane >= A_QK_DIM, y, 0.0).astype(BF16)
    else:
        for j in range(D_KV_GROUP):
            q_ref[j * tq:(j + 1) * tq] = prep(qr_ref[0, :, j * LANE:(j + 1) * LANE]).astype(BF16)
    q = q_ref[...]

    def scores(off, size):
        k = k_ref[0, pl.ds(off, size), :]
        return lax.dot_general(q, k, (((1,), (1,)), ((), ())), preferred_element_type=F32)

    def block(c):
        off = pl.multiple_of(c * ts, ts)
        return scores(off, ts), v_ref[0, pl.ds(off, ts), :]

    def wide(m):
        return jnp.concatenate([m] * (ts // LANE), axis=1)

    m0 = jnp.broadcast_to(jnp.max(scores(0, 2 * LANE), axis=1, keepdims=True), m_ref.shape)
    m_ref[...] = m0
    l_ref[...] = jnp.zeros_like(l_ref)
    acc_ref[...] = jnp.zeros_like(acc_ref)
    qf = q.astype(F32)
    qn = jnp.sqrt(jnp.sum(qf * qf, axis=1, keepdims=True))
    thr_ref[0] = jnp.min((m0[:, :1] + FLASH_HEADROOM) / qn)

    def step(c, carry):
        fixed = kmax_ref[base + c] * 1.001 <= thr_ref[0]

        @pl.when(fixed)
        def _():
            s, v = block(c)
            p = jnp.exp2(s - wide(m_ref[...]))
            l_ref[...] += jnp.sum(p, axis=1, keepdims=True)
            acc_ref[...] += jnp.dot(p.astype(BF16), v, preferred_element_type=F32)

        @pl.when(jnp.logical_not(fixed))
        def _():
            s, v = block(c)
            m_prev = m_ref[...]
            m_new = jnp.maximum(m_prev, jnp.max(s, axis=1, keepdims=True))
            alpha = jnp.exp2(m_prev - m_new)
            p = jnp.exp2(s - wide(m_new))
            l_ref[...] = alpha * l_ref[...] + jnp.sum(p, axis=1, keepdims=True)
            acc_ref[...] = alpha * acc_ref[...] + jnp.dot(p.astype(BF16), v, preferred_element_type=F32)
            m_ref[...] = m_new

        return carry

    lax.fori_loop(0, nsub, step, 0)

    o = acc_ref[...] / l_ref[...]
    if mode == "diff":
        d = o[:tq] - e1_ref[...] * o[tq:]
        ms = jnp.mean(d * d, axis=-1, keepdims=True)
        o_ref[0] = (d * lax.rsqrt(ms + NORM_EPS) * e2_ref[...]).astype(o_ref.dtype)
    else:
        for j in range(D_KV_GROUP):
            o_ref[0, :, j * LANE:(j + 1) * LANE] = o[j * tq:(j + 1) * tq].astype(o_ref.dtype)


def _flash(pq, qcol0, k, v, kcol0, vcol0, e1, e2, *, mode, hk, dim, tq, gain=None, rope=True):
    bsz, n, _ = pq.shape
    nk = k.shape[1]
    g = 2 if mode == "diff" else D_KV_GROUP
    mq = g * tq
    qw = LANE if mode == "diff" else D_KV_GROUP * LANE
    qb = qcol0 // qw
    ts = _pick(nk, FLASH_KEYS, 2 * LANE)
    kb, vb = kcol0 // LANE, vcol0 // LANE
    ow = LANE if mode == "diff" else D_KV_GROUP * LANE
    kmax = _key_block_norms(k, kcol0, hk, ts)
    cos, sin, pm, gq, gqs = _rope_operands(n, dim, gain, rope)
    const = lambda shape: pl.BlockSpec(shape, lambda b, h, i: (0, 0))
    return pl.pallas_call(
        functools.partial(_flash_body, ts=ts, nsub=nk // ts, mode=mode, tq=tq, norm=gain is not None, rope=rope,
                          scale=dim ** -0.5 * LOG2E),
        grid=(bsz, hk, n // tq),
        in_specs=[pl.BlockSpec(memory_space=pltpu.SMEM),
                  pl.BlockSpec((1, tq, qw), lambda b, h, i: (b, i, qb + h)),
                  pl.BlockSpec((tq, LANE), lambda b, h, i: (i, 0)),
                  pl.BlockSpec((tq, LANE), lambda b, h, i: (i, 0)),
                  const((LANE, LANE)), const((1, LANE)), const((1, LANE)),
                  pl.BlockSpec((1, nk, LANE), lambda b, h, i: (b, 0, kb + h)),
                  pl.BlockSpec((1, nk, LANE), lambda b, h, i: (b, 0, vb + h)),
                  const((1, LANE)), const((1, LANE))],
        out_specs=pl.BlockSpec((1, tq, ow), lambda b, h, i: (b, i, h)),
        out_shape=jax.ShapeDtypeStruct((bsz, n, hk * ow), BF16),
        scratch_shapes=[pltpu.VMEM((mq, LANE), BF16), pltpu.VMEM((mq, LANE), F32), pltpu.VMEM((mq, LANE), F32),
                        pltpu.VMEM((mq, LANE), F32), pltpu.SMEM((1,), F32)],
        compiler_params=_cp(("parallel", "parallel", "arbitrary")),
        name="flash_" + mode,
    )(kmax, pq, cos, sin, pm, gq, gqs, k, v, e1, e2)


def _diff_attention(pa_lat, pa_ctx, c0, aw, lam, lam_init, subln_g):
    n, nctx = pa_lat.shape[1], pa_ctx.shape[1]
    heads = aw // LANE
    k_all = _kprep(pa_lat, c0 + aw, pa_ctx, c0 + aw, heads, dim=A_QK_DIM)
    v_all = jnp.concatenate([pa_lat[..., c0 + 2 * aw:c0 + 3 * aw], pa_ctx[..., c0 + 2 * aw:c0 + 3 * aw]], axis=1)
    e1 = jnp.full((1, LANE), lam, F32)
    e2 = (subln_g.astype(F32) * (1.0 - lam_init)).reshape(1, LANE)
    o_lat = _flash(pa_lat, c0, k_all, v_all, 0, 0, e1, e2, mode="diff", hk=heads, dim=A_QK_DIM, tq=_pick(n, 1024))
    o_ctx = _flash(pa_ctx, c0, pa_ctx, pa_ctx, c0 + aw, c0 + 2 * aw, e1, e2, mode="diff", hk=heads, dim=A_QK_DIM,
                   tq=_pick(nctx, 512), rope=False)
    return o_lat, o_ctx


def _gqa_attention(p_lat, pkv_ctx, qcol0, q_norm_g, k_norm_g):
    n = p_lat.shape[1]
    hq, hk = 2 * D_KV_GROUP, 2
    kcol0 = qcol0 + hq * D_HEAD
    vcol0 = kcol0 + hk * D_HEAD
    k_all = _kprep(p_lat, kcol0, pkv_ctx, 0, hk, dim=D_HEAD, gain=k_norm_g)
    v_all = jnp.concatenate([p_lat[..., vcol0:vcol0 + hk * D_HEAD], pkv_ctx[..., hk * D_HEAD:]], axis=1)
    dummy = jnp.zeros((1, LANE), F32)
    return _flash(p_lat, qcol0, k_all, v_all, 0, 0, dummy, dummy, mode="gqa", hk=hk, dim=D_HEAD,
                  tq=_pick(n, 512), gain=q_norm_g)


RW_CHUNK = 64
RW_HALO = 8


def _mm_bf(a, b):
    return jnp.dot(a.astype(BF16), b.astype(BF16), preferred_element_type=F32)


def _split3(x):
    x1 = x.astype(BF16)
    r1 = x - x1.astype(F32)
    x2 = r1.astype(BF16)
    return x1, x2, (r1 - x2.astype(F32)).astype(BF16)


def _mm_x3(a, b):
    ah, al, _ = _split3(a)
    bh, bl, _ = _split3(b)
    d = lambda u, w: jnp.dot(u, w, preferred_element_type=F32)
    return d(ah, bh) + (d(ah, bl) + d(al, bh))


def _mm_sel(z, sel):
    z1, z2, _ = _split3(z)
    d = lambda u: jnp.dot(u, sel, preferred_element_type=F32)
    return d(z1) + d(z2)


def _rwprep_body(x_ref, hp_ref, hn_ref, mup_ref, mun_ref, vec_ref, w2f_ref, w2b_ref, a2f_ref, a2b_ref,
                 g2_ref, sel_ref, selt_ref,
                 r_ref, v_ref, kk_ref, lwf_ref, bf_ref, kf_ref, lwb_ref, bb_ref, kb_ref, g_ref, bonus_ref,
                 *, nb, tm, bw):
    i = pl.program_id(1)
    x = x_ref[0].astype(F32)
    row = lax.broadcasted_iota(jnp.int32, x.shape, 0)
    prev_edge = jnp.where(i > 0, hp_ref[0, RW_HALO - 1:RW_HALO, :].astype(F32), 0.0)
    next_edge = jnp.where(i < nb - 1, hn_ref[0, 0:1, :].astype(F32), 0.0)
    prev = jnp.where(row == 0, prev_edge, pltpu.roll(x, 1, 0))
    nxt = jnp.where(row == tm - 1, next_edge, pltpu.roll(x, tm - 1, 0))
    xs = x + mup_ref[...] * (prev - x) + mun_ref[...] * (nxt - x)
    r, k, v = xs[:, :bw], xs[:, bw:2 * bw], xs[:, 2 * bw:3 * bw]
    lr = xs[:, 3 * bw:3 * bw + LANE]
    gd = xs[:, 3 * bw + LANE:3 * bw + 3 * LANE]
    k_k, k_a, r_k = vec_ref[0:1], vec_ref[1:2], vec_ref[2:3]
    w0f, a0f, w0b, a0b = vec_ref[3:4], vec_ref[4:5], vec_ref[5:6], vec_ref[6:7]
    sel, selt = sel_ref[...], selt_ref[...]

    def head_sum(z):
        return _mm_sel(_mm_sel(z, sel), selt)

    kk = k * k_k
    kk = kk * lax.rsqrt(head_sum(kk * kk) + 1e-12)
    g_ref[0] = _mm_bf(jax.nn.sigmoid(gd), g2_ref[...]).astype(g_ref.dtype)
    th = jnp.tanh(lr)
    ksum = jnp.zeros_like(k)
    for (w0, a0, w2_ref, a2_ref, lw_ref, b_ref, kd_ref) in (
            (w0f, a0f, w2f_ref, a2f_ref, lwf_ref, bf_ref, kf_ref),
            (w0b, a0b, w2b_ref, a2b_ref, lwb_ref, bb_ref, kb_ref)):
        logw = -math.exp(-0.5) * jax.nn.sigmoid(w0 + _mm_bf(th, w2_ref[...]))
        a = jax.nn.sigmoid(a0 + _mm_bf(lr, a2_ref[...]))
        k_d = k * (1.0 + (a - 1.0) * k_a)
        ksum = ksum + k_d
        lw_ref[0] = logw
        b_ref[0] = kk * a
        kd_ref[0] = k_d
    r_ref[0] = r
    v_ref[0] = v
    kk_ref[0] = kk
    bonus_ref[0] = head_sum(r * ksum * r_k) * v


def _rwkv_prepare(pr, p, cols):
    bsz, n, _ = pr.shape
    bw = p["w0_f"].shape[0]
    tm = _pick(n, 256)
    nb = n // tm
    hb = tm // RW_HALO
    assert cols - 3 * bw >= 3 * LANE and B_DECAY_RANK + B_ICL_RANK == LANE

    def padded(vec):
        return jnp.concatenate([vec.astype(F32), jnp.zeros((cols - vec.shape[0],), F32)]).reshape(1, cols)

    vecs = jnp.stack([p["k_k"], p["k_a"], p["r_k"].reshape(-1), p["w0_f"], p["a0_f"], p["w0_b"], p["a0_b"],
                      jnp.zeros((bw,), F32)]).astype(F32)
    zr = jnp.zeros((B_DECAY_RANK, bw), F32)
    w2 = {d: jnp.concatenate([p["w2_" + d].astype(F32), zr], axis=0).astype(BF16) for d in "fb"}
    a2 = {d: jnp.concatenate([zr, p["a2_" + d].astype(F32)], axis=0).astype(BF16) for d in "fb"}
    g2 = jnp.concatenate([p["g2"].astype(F32), jnp.zeros((2 * LANE - B_GATE_RANK, bw), F32)], axis=0).astype(BF16)
    sel, selt = _head_selectors(bw)

    full = lambda shape: pl.BlockSpec(shape, lambda b, i: tuple(0 for _ in shape))
    tok = pl.BlockSpec((1, tm, bw), lambda b, i: (b, i, 0))
    tok_f32 = jax.ShapeDtypeStruct((bsz, n, bw), F32)
    outs = pl.pallas_call(
        functools.partial(_rwprep_body, nb=nb, tm=tm, bw=bw),
        grid=(bsz, nb),
        in_specs=[pl.BlockSpec((1, tm, cols), lambda b, i: (b, i, 0)),
                  pl.BlockSpec((1, RW_HALO, cols), lambda b, i: (b, jnp.maximum(i * hb - 1, 0), 0)),
                  pl.BlockSpec((1, RW_HALO, cols), lambda b, i: (b, jnp.minimum((i + 1) * hb, nb * hb - 1), 0)),
                  full((1, cols)), full((1, cols)), full((8, bw)),
                  full((LANE, bw)), full((LANE, bw)), full((LANE, bw)), full((LANE, bw)),
                  full((2 * LANE, bw)), full((bw, LANE)), full((LANE, bw))],
        out_specs=[tok] * 11,
        out_shape=[tok_f32] * 9 + [jax.ShapeDtypeStruct((bsz, n, bw), BF16), tok_f32],
        compiler_params=_cp(("parallel", "parallel")),
        name="rwkv_prepare",
    )(pr, pr, pr, padded(p["mu_prev"]), padded(p["mu_next"]), vecs, w2["f"], w2["b"], a2["f"], a2["b"],
      g2, sel, selt)
    r, v, kk, lwf, bf, kf, lwb, bb, kb, g, bonus = outs
    return dict(r=r, v=v, kk=kk, g=g, bonus=bonus, f=(lwf, bf, kf), b=(lwb, bb, kb))


RW_GROUP = 4


def _bd(x):
    t = jnp.concatenate([x.astype(BF16)] * RW_GROUP, axis=1)
    r = lax.broadcasted_iota(jnp.int32, t.shape, 1) // B_HEAD
    c = lax.broadcasted_iota(jnp.int32, t.shape, 2) // B_HEAD
    return jnp.where(r == c, t, jnp.zeros_like(t))


def _bdot(a, b, dims):
    return jnp.einsum(dims, a.astype(BF16), b.astype(BF16), preferred_element_type=F32)


def _groups(x):
    gw = RW_GROUP * B_HEAD
    return jnp.stack([x[:, g * gw:(g + 1) * gw] for g in range(x.shape[1] // gw)])


def _rw_decays(r, kk, b, kd, lw, reverse):
    ck = r.shape[0]
    ti = lax.broadcasted_iota(jnp.int32, (ck, ck), 0)
    si = lax.broadcasted_iota(jnp.int32, (ck, ck), 1)
    tri = ((si >= ti) if reverse else (si <= ti)).astype(BF16)
    l1, l2, l3 = _split3(lw)
    tdot = lambda z: jnp.dot(tri, z, preferred_element_type=F32)
    cum = tdot(l1) + (tdot(l2) + tdot(l3))
    total = cum[0:1] if reverse else cum[ck - 1:ck]
    winv = jnp.exp(-cum)
    wrest = jnp.exp(total - cum)
    return (-kk * jnp.exp(cum - lw), b * winv, kd * winv, r * jnp.exp(cum), b * wrest, kd * wrest, jnp.exp(total))


def _rw_chunks(fwd, bwd, s_f, s_b):
    nseq = len(fwd)
    ck, w = fwd[0][0].shape
    gw = RW_GROUP * B_HEAD
    ng = w // gw
    parts = []
    for probs, rev in ((fwd, False), (bwd, True)):
        for (r, v, kk, b, kd, lw) in probs:
            parts.append([_groups(z) for z in _rw_decays(r, kk, b, kd, lw, rev)] + [_groups(v)])
    at, bt, kt, rt, btw, ktw, decay, v = [jnp.concatenate(zs, axis=0) for zs in zip(*parts)]
    s = jnp.concatenate([_groups(z) for z in list(s_f) + list(s_b)], axis=0)
    nslab = 2 * nseq * ng
    shape = (nslab, ck, gw)
    lag = lax.broadcasted_iota(jnp.int32, shape, 1) - lax.broadcasted_iota(jnp.int32, shape, 2) % ck
    lag = jnp.where(lax.broadcasted_iota(jnp.int32, shape, 0) >= nseq * ng, -lag, lag)
    incl, strict = lag >= 0, lag > 0
    eye = (lag == 0).astype(F32)

    bdv = _bd(v)
    lhs = jnp.concatenate([at, rt], axis=1)
    lb, lk = _bdot(lhs, _bd(bt), "gtk,gnk->gtn"), _bdot(lhs, _bd(kt), "gtk,gnk->gtn")
    lab, mrb = jnp.where(strict, lb[:, :ck], 0.0), jnp.where(incl, lb[:, ck:], 0.0)
    lak, mrk = jnp.where(strict, lk[:, :ck], 0.0), jnp.where(incl, lk[:, ck:], 0.0)
    tinv = eye + lab
    pw = _bdot(lab, _bd(lab), "gtk,gkn->gtn")
    span = 4
    while span < ck:
        both = _bdot(jnp.concatenate([tinv, pw], axis=1), _bd(pw), "gtk,gkn->gtn")
        tinv, pw = tinv + both[:, :ck], both[:, ck:]
        span *= 2
    tinv = tinv + _bdot(tinv, _bd(pw), "gtk,gkn->gtn")
    vmix = _bdot(jnp.concatenate([lak, mrk], axis=1), bdv, "gtk,gkn->gtn")
    p1 = _bdot(tinv, _bd(at), "gtk,gkn->gtn")
    q1 = _bdot(tinv, _bd(vmix[:, :ck]), "gtk,gkn->gtn")
    smix = _bdot(jnp.concatenate([p1, rt], axis=1), _bd(s), "gtk,gnk->gtn")
    u = smix[:, :ck] + q1
    y = smix[:, ck:] + _bdot(mrb, _bd(u), "gtk,gkn->gtn") + vmix[:, ck:]
    z = _bdot(jnp.concatenate([u, v], axis=1), jnp.concatenate([btw, ktw], axis=1), "gtm,gtn->gmn")
    lane_head = lax.broadcasted_iota(jnp.int32, (nslab, B_HEAD, gw), 2) // B_HEAD
    upd = jnp.zeros_like(s)
    for h in range(RW_GROUP):
        upd = upd + jnp.where(lane_head == h, z[:, h * B_HEAD:(h + 1) * B_HEAD], 0.0)
    s_new = s * decay + upd
    lanes = lambda x, q: jnp.concatenate([x[q * ng + g] for g in range(ng)], axis=1)
    seqs = range(nseq)
    return ([lanes(y, q) for q in seqs], [lanes(y, nseq + q) for q in seqs],
            [lanes(s_new, q) for q in seqs], [lanes(s_new, nseq + q) for q in seqs])


def _rwscan_body(rf_ref, vf_ref, kkf_ref, bf_ref, kdf_ref, lwf_ref, rb_ref, vb_ref, kkb_ref, bb_ref, kdb_ref, lwb_ref,
                 s0f_ref, s0b_ref, yf_ref, yb_ref, sTf_ref, sTb_ref, sf_ref, sb_ref, *, nc, nseq):
    c = pl.program_id(0)

    @pl.when(c == 0)
    def _():
        sf_ref[...] = s0f_ref[...]
        sb_ref[...] = s0b_ref[...]

    seqs = range(nseq)
    yf, yb, sf, sb = _rw_chunks(
        [(rf_ref[q], vf_ref[q], kkf_ref[q], bf_ref[q], kdf_ref[q], lwf_ref[q]) for q in seqs],
        [(rb_ref[q], vb_ref[q], kkb_ref[q], bb_ref[q], kdb_ref[q], lwb_ref[q]) for q in seqs],
        [sf_ref[q] for q in seqs], [sb_ref[q] for q in seqs])
    for q in seqs:
        yf_ref[q] = yf[q]
        yb_ref[q] = yb[q]
        sf_ref[q] = sf[q]
        sb_ref[q] = sb[q]

    @pl.when(c == nc - 1)
    def _():
        for q in seqs:
            sTf_ref[q] = sf[q]
            sTb_ref[q] = sb[q]


def _rwkv_scan(r, v, kk, fwd, bwd, s0f, s0b):
    bsz, n, w = r.shape
    ck = RW_CHUNK
    assert n % ck == 0 and ck == B_HEAD and w % (RW_GROUP * B_HEAD) == 0
    nc = n // ck
    seq_f = pl.BlockSpec((bsz, ck, w), lambda c: (0, c, 0))
    seq_b = pl.BlockSpec((bsz, ck, w), lambda c: (0, nc - 1 - c, 0))
    st = pl.BlockSpec((bsz, B_HEAD, w), lambda c: (0, 0, 0))
    y_shape = jax.ShapeDtypeStruct((bsz, n, w), F32)
    s_shape = jax.ShapeDtypeStruct((bsz, B_HEAD, w), F32)
    (lwf, bf, kf), (lwb, bb, kb) = fwd, bwd
    return pl.pallas_call(
        functools.partial(_rwscan_body, nc=nc, nseq=bsz),
        grid=(nc,),
        in_specs=[seq_f] * 6 + [seq_b] * 6 + [st, st],
        out_specs=[seq_f, seq_b, st, st],
        out_shape=[y_shape, y_shape, s_shape, s_shape],
        scratch_shapes=[pltpu.VMEM((bsz, B_HEAD, w), F32), pltpu.VMEM((bsz, B_HEAD, w), F32)],
        compiler_params=_cp(("arbitrary",)),
        name="rwkv_scan",
    )(r, v, kk, bf, kf, lwf, r, v, kk, bb, kb, lwb, s0f, s0b)


def _rwpost_body(yf_ref, yb_ref, g_ref, bonus_ref, lg_ref, lb_ref, sel_ref, selt_ref, o_ref):
    sel, selt = sel_ref[...], selt_ref[...]
    head_mean = lambda z: _mm_sel(_mm_sel(z, sel), selt) * (1.0 / B_HEAD)
    y = yf_ref[0] + yb_ref[0]
    d = y - head_mean(y)
    y = d * lax.rsqrt(head_mean(d * d) + B_GN_EPS)
    y = y * lg_ref[...] + lb_ref[...] + bonus_ref[0]
    o_ref[0] = (y * g_ref[0].astype(F32)).astype(o_ref.dtype)


def _head_selectors(bw):
    sel_np = np.zeros((bw, LANE), np.float32)
    sel_np[np.arange(bw), np.arange(bw) // B_HEAD] = 1.0
    return jnp.asarray(sel_np, BF16), jnp.asarray(sel_np.T, BF16)


def _rwkv_post(y_f, y_b, g, bonus, lnx_g, lnx_b):
    bsz, n, bw = y_f.shape
    tm = _pick(n, 256)
    tok = pl.BlockSpec((1, tm, bw), lambda b, i: (b, i, 0))
    vec = pl.BlockSpec((1, bw), lambda b, i: (0, 0))
    sel, selt = _head_selectors(bw)
    return pl.pallas_call(
        _rwpost_body,
        grid=(bsz, n // tm),
        in_specs=[tok, tok, tok, tok, vec, vec,
                  pl.BlockSpec((bw, LANE), lambda b, i: (0, 0)), pl.BlockSpec((LANE, bw), lambda b, i: (0, 0))],
        out_specs=tok,
        out_shape=jax.ShapeDtypeStruct((bsz, n, bw), BF16),
        compiler_params=_cp(("parallel", "parallel")),
        name="rwkv_post",
    )(y_f, y_b, g, bonus, lnx_g.reshape(1, bw).astype(F32), lnx_b.reshape(1, bw).astype(F32), sel, selt)


def _rwkv_run(prep, states0, p):
    y_f, y_b, s_f, s_b = _rwkv_scan(prep["r"], prep["v"], prep["kk"], prep["f"], prep["b"], states0[0], states0[1])
    return _rwkv_post(y_f, y_b, prep["g"], prep["bonus"], p["lnx_g"], p["lnx_b"]), (s_f, s_b)


HY_N2 = 128
HY_HALO = 8


def _hypre_body(x_ref, hp_ref, hn_ref, cw_ref, cb_ref, x0_ref, uv_ref, *, nb, tm, cw):
    i = pl.program_id(1)
    x = x_ref[0].astype(F32)
    row = lax.broadcasted_iota(jnp.int32, x.shape, 0)
    prev_edge = jnp.where(i > 0, hp_ref[0, HY_HALO - 1:HY_HALO, :].astype(F32), 0.0)
    next_edge = jnp.where(i < nb - 1, hn_ref[0, 0:1, :].astype(F32), 0.0)
    prev = jnp.where(row == 0, prev_edge, pltpu.roll(x, 1, 0))
    nxt = jnp.where(row == tm - 1, next_edge, pltpu.roll(x, tm - 1, 0))
    u = prev * cw_ref[0:1] + x * cw_ref[1:2] + nxt * cw_ref[2:3] + cb_ref[...]
    x0_ref[0] = u[:, :cw]
    uv_ref[0] = u[:, 2 * cw:3 * cw] * u[:, cw:2 * cw]


def _hyena_pre(p, conv_w, conv_b):
    bsz, n, _ = p.shape
    hc = conv_w.shape[1]
    cw = hc // 3
    tm = _pick(n, 256)
    nb = n // tm
    hb = tm // HY_HALO
    cwp = jnp.concatenate([conv_w.astype(F32), jnp.zeros((5, hc), F32)], axis=0)
    out = pl.BlockSpec((1, tm, cw), lambda b, i: (b, i, 0))
    return pl.pallas_call(
        functools.partial(_hypre_body, nb=nb, tm=tm, cw=cw),
        grid=(bsz, nb),
        in_specs=[pl.BlockSpec((1, tm, hc), lambda b, i: (b, i, 0)),
                  pl.BlockSpec((1, HY_HALO, hc), lambda b, i: (b, jnp.maximum(i * hb - 1, 0), 0)),
                  pl.BlockSpec((1, HY_HALO, hc), lambda b, i: (b, jnp.minimum((i + 1) * hb, nb * hb - 1), 0)),
                  pl.BlockSpec((8, hc), lambda b, i: (0, 0)),
                  pl.BlockSpec((1, hc), lambda b, i: (0, 0))],
        out_specs=[out, out],
        out_shape=[jax.ShapeDtypeStruct((bsz, n, cw), F32)] * 2,
        compiler_params=_cp(("parallel", "parallel")),
        name="hyena_pre",
    )(p, p, p, cwp, conv_b.reshape(1, hc).astype(F32))


def _hyfilt_body(z_ref, t_ref, z0_ref, w1_ref, w2_ref, w3_ref, w4_ref, w4b_ref, vec_ref, dl_ref,
                 k_ref, s_ref, *, half_tiles):
    i = pl.program_id(0)
    b1, b2, b3, fr = vec_ref[0:1], vec_ref[1:2], vec_ref[2:3], vec_ref[3:4]

    def mlp3(z):
        h = jnp.sin(fr * (_mm_x3(z, w1_ref[...]) + b1))
        h = jnp.sin(fr * (_mm_x3(h, w2_ref[...]) + b2))
        h = jnp.sin(fr * (_mm_x3(h, w3_ref[...]) + b3))
        lane = lax.broadcasted_iota(jnp.int32, h.shape, 1)
        return jnp.where(lane < HY_ORDER, h, 0.0), jnp.where(lane >= HY_ORDER, h, 0.0)

    h_lo, h_hi = mlp3(z_ref[...])
    raw = jnp.concatenate([_mm_x3(h_lo, w4_ref[...]), _mm_x3(h_hi, w4_ref[...])], axis=0)
    raw = raw * jnp.exp(-t_ref[...] * dl_ref[...])

    @pl.when(i == 0)
    def _():
        s_ref[...] = jnp.zeros_like(s_ref)

    s_ref[...] += jnp.sum(jnp.abs(raw), axis=0, keepdims=True)
    hb0 = _mm_x3(mlp3(z0_ref[...])[0], w4b_ref[...])
    row = lax.broadcasted_iota(jnp.int32, raw.shape, 0)
    raw = raw + jnp.where((row == 0) & (i == 0), hb0[0:1], 0.0)
    k_ref[...] = jnp.where((row == 0) & (i == half_tiles), 0.0, raw)


def _hyena_kernel_taps(n, p):
    cw = p["f_w4"].shape[1] // 2
    f32 = lambda a: a.astype(F32)
    row = jnp.arange(2 * n, dtype=jnp.int32)
    pos = jnp.where(row < n, row, (2 * n - row) % n).astype(F32)[:, None]
    tt = pos * jnp.float32(1.0 / (n - 1))
    bands = (HY_EMB - 1) // 2
    ang = (2 * math.pi / n) * pos * jnp.linspace(1e-4, bands - 1, bands, dtype=F32)[None]
    zz = jnp.concatenate([tt, jnp.cos(ang), -jnp.sin(ang), jnp.zeros((2 * n, HY_ORDER - HY_EMB), F32)], axis=-1)
    w1 = jnp.concatenate([f32(p["f_w1"]), jnp.zeros((HY_ORDER - HY_EMB, HY_ORDER), F32)], axis=0)
    twice = lambda w: jnp.kron(jnp.eye(2, dtype=F32), f32(w))
    vecs = jnp.stack([jnp.tile(f32(p[k]), 2) for k in ("f_b1", "f_b2", "f_b3", "f_freq")]
                     + [jnp.zeros((2 * HY_ORDER,), F32)] * 4)
    deltas = jnp.abs(jnp.linspace(math.log(HY_TARGET) / HY_SLOW_PCT, math.log(HY_TARGET) / HY_FAST_PCT, cw,
                                  dtype=F32)).reshape(1, cw)
    tr = _pick(n, 512)
    half_tiles = n // tr
    pk = 2 * HY_ORDER
    zp = zz.reshape(-1, 2, tr // 2, HY_ORDER).transpose(0, 2, 1, 3).reshape(-1, pk)
    full = lambda shape: pl.BlockSpec(shape, lambda i: tuple(0 for _ in shape))
    w4 = jnp.concatenate([f32(p["f_w4"])] * 2, axis=0)
    return pl.pallas_call(
        functools.partial(_hyfilt_body, half_tiles=half_tiles),
        grid=(2 * half_tiles,),
        in_specs=[pl.BlockSpec((tr // 2, pk), lambda i: (i, 0)),
                  pl.BlockSpec((tr, 1), lambda i: (i, 0)),
                  full((8, pk)), full((pk, pk)), full((pk, pk)), full((pk, pk)),
                  pl.BlockSpec((pk, cw), lambda i: (0, i // half_tiles)),
                  pl.BlockSpec((pk, cw), lambda i: (0, 1)),
                  full((8, pk)), full((1, cw))],
        out_specs=[pl.BlockSpec((tr, cw), lambda i: (i, 0)), pl.BlockSpec((1, cw), lambda i: (0, 0))],
        out_shape=[jax.ShapeDtypeStruct((2 * n, cw), F32), jax.ShapeDtypeStruct((1, cw), F32)],
        compiler_params=_cp(("arbitrary",)),
        name="hyena_filter",
    )(zp, tt, jnp.broadcast_to(jnp.tile(zz[0:1], (1, 2)), (8, pk)), twice(w1), twice(p["f_w2"]), twice(p["f_w3"]),
      w4, w4, vecs, deltas)


def _dft_consts(n):
    nn = 2 * n
    n2 = HY_N2
    n1 = nn // n2
    n1h = n1 // 2 + 1
    nf = -(-n1h // 8) * 8
    live = (np.arange(nf) < n1h).astype(np.float64)
    a1 = 2 * np.pi * np.outer(np.arange(nf), np.arange(n1)) / n1
    f1 = np.concatenate([np.cos(a1), -np.sin(a1)], axis=0) * np.tile(live, 2)[:, None]
    wgt = live * np.where((np.arange(nf) == 0) | (np.arange(nf) == n1 // 2), 1.0, 2.0)
    f1inv = np.concatenate([np.cos(a1.T) * wgt, -np.sin(a1.T) * wgt], axis=1) / nn
    a2 = 2 * np.pi * np.outer(np.arange(n2), np.arange(n2)) / n2
    c2, s2 = np.cos(a2), -np.sin(a2)
    m2 = np.block([[c2, -s2], [s2, c2]])
    m2inv = np.block([[c2, s2], [-s2, c2]])
    at = 2 * np.pi * np.outer(np.arange(n2), np.arange(nf)) / nn
    tw = np.stack([np.cos(at), -np.sin(at)])
    c = lambda a: jnp.asarray(a.astype(np.float32))
    return dict(n1=n1, n2=n2, nf=nf, f1=c(f1), f1inv=c(f1inv), m2=c(m2), m2inv=c(m2inv),
                tw_s2=c(tw[:, :, :, None]),
                tw_f1=c(np.transpose(tw, (0, 2, 1))[:, :, :, None]))


HY_S2 = 8
HY_CTILE = 512


def _dft1_body(f_ref, x_ref, tw_ref, o_ref, *, nf):
    x = jnp.swapaxes(x_ref[0], 0, 1)
    fh, fl, _ = _split3(f_ref[...])
    d = lambda u, w: jnp.dot(u, w, preferred_element_type=F32)
    re, im = [], []
    for jj in range(HY_S2):
        xh, xl, _ = _split3(x[jj])
        a = d(fh, xh) + (d(fh, xl) + d(fl, xh))
        ar, ai = a[:nf], a[nf:]
        twr, twi = tw_ref[0, jj], tw_ref[1, jj]
        re.append(ar * twr - ai * twi)
        im.append(ar * twi + ai * twr)
    o_ref[0, 0] = jnp.swapaxes(jnp.stack(re), 0, 1)
    o_ref[0, 1] = jnp.swapaxes(jnp.stack(im), 0, 1)


def _dft_stage1(x, consts, rows):
    bsz, _, cw = x.shape
    n2, nf = consts["n2"], consts["nf"]
    xv = x.reshape(bsz, rows, n2, cw)
    f1 = consts["f1"][:, :rows]
    tc = _pick(cw, HY_CTILE, LANE)
    return pl.pallas_call(
        functools.partial(_dft1_body, nf=nf),
        grid=(bsz, n2 // HY_S2, cw // tc),
        in_specs=[pl.BlockSpec((2 * nf, rows), lambda b, j, c: (0, 0)),
                  pl.BlockSpec((1, rows, HY_S2, tc), lambda b, j, c: (b, 0, j, c)),
                  pl.BlockSpec((2, HY_S2, nf, 1), lambda b, j, c: (0, j, 0, 0))],
        out_specs=pl.BlockSpec((1, 2, nf, HY_S2, tc), lambda b, j, c: (b, 0, 0, j, c)),
        out_shape=jax.ShapeDtypeStruct((bsz, 2, nf, n2, cw), F32),
        compiler_params=_cp(("parallel", "parallel", "parallel")),
        name="hyena_dft1",
    )(f1, xv, consts["tw_s2"])


HY_F1 = 4


def _dftmid_body(a_ref, m2_ref, m2i_ref, h_ref, tw_ref, is_ref, o_ref, *, n2, filt):
    for r in range(HY_F1):
        a = a_ref[0, :, r].reshape(2 * n2, a_ref.shape[-1])
        x = _mm_x3(m2_ref[...], a)
        xr, xi = x[:n2], x[n2:]
        if filt:
            o_ref[0, 0, r] = xr * is_ref[...]
            o_ref[0, 1, r] = xi * is_ref[...]
            continue
        hr, hi = h_ref[0, 0, r], h_ref[0, 1, r]
        y = jnp.concatenate([xr * hr - xi * hi, xr * hi + xi * hr], axis=0)
        z = _mm_x3(m2i_ref[...], y)
        zr, zi = z[:n2], z[n2:]
        twr, twi = tw_ref[0, r], tw_ref[1, r]
        o_ref[0, 0, r] = zr * twr + zi * twi
        o_ref[0, 1, r] = zi * twr - zr * twi


def _dft_mid(a, h, inv_s, consts, filt):
    bsz = a.shape[0]
    n1, n2 = consts["nf"], consts["n2"]
    cw = a.shape[-1]
    blk = pl.BlockSpec((1, 2, HY_F1, n2, cw), lambda f, b: (b, 0, f, 0, 0))
    hblk = pl.BlockSpec((1, 2, HY_F1, n2, cw),
                        (lambda f, b: (0, 0, 0, 0, 0)) if filt else (lambda f, b: (0, 0, f, 0, 0)))
    return pl.pallas_call(
        functools.partial(_dftmid_body, n2=n2, filt=filt),
        grid=(n1 // HY_F1, bsz),
        in_specs=[blk,
                  pl.BlockSpec((2 * n2, 2 * n2), lambda f, b: (0, 0)),
                  pl.BlockSpec((2 * n2, 2 * n2), lambda f, b: (0, 0)),
                  hblk,
                  pl.BlockSpec((2, HY_F1, n2, 1), lambda f, b: (0, f, 0, 0)),
                  pl.BlockSpec((1, cw), lambda f, b: (0, 0))],
        out_specs=blk,
        out_shape=jax.ShapeDtypeStruct((bsz, 2, n1, n2, cw), F32),
        compiler_params=_cp(("parallel", "parallel")),
        name="hyena_dftmid_" + ("filter" if filt else "conv"),
    )(a, consts["m2"], consts["m2inv"], h, consts["tw_f1"], inv_s)


def _dft3_body(f_ref, z_ref, x0_ref, uv_ref, bias_ref, o_ref):
    zr, zi = jnp.swapaxes(z_ref[0, 0], 0, 1), jnp.swapaxes(z_ref[0, 1], 0, 1)
    x0, uv = jnp.swapaxes(x0_ref[0], 0, 1), jnp.swapaxes(uv_ref[0], 0, 1)
    fh, fl, _ = _split3(f_ref[...])
    d = lambda u, w: jnp.dot(u, w, preferred_element_type=F32)
    out = []
    for jj in range(HY_S2):
        zh, zl, _ = _split3(jnp.concatenate([zr[jj], zi[jj]], axis=0))
        y = d(fh, zh) + (d(fh, zl) + d(fl, zh))
        out.append(x0[jj] * (y + uv[jj] * bias_ref[...]))
    o_ref[0] = jnp.swapaxes(jnp.stack(out), 0, 1)


def _dft_stage1_inv(z, x0, uv, bias, consts):
    bsz, n, cw = x0.shape
    nf, n2 = consts["nf"], consts["n2"]
    hr = consts["n1"] // 2
    tokv = lambda t: t.reshape(bsz, hr, n2, cw)
    tc = _pick(cw, HY_CTILE, LANE)
    tile = pl.BlockSpec((1, hr, HY_S2, tc), lambda b, j, c: (b, 0, j, c))
    out = pl.pallas_call(
        _dft3_body,
        grid=(bsz, n2 // HY_S2, cw // tc),
        in_specs=[pl.BlockSpec((hr, 2 * nf), lambda b, j, c: (0, 0)),
                  pl.BlockSpec((1, 2, nf, HY_S2, tc), lambda b, j, c: (b, 0, 0, j, c)),
                  tile, tile,
                  pl.BlockSpec((1, tc), lambda b, j, c: (0, c))],
        out_specs=tile,
        out_shape=jax.ShapeDtypeStruct((bsz, hr, n2, cw), F32),
        compiler_params=_cp(("parallel", "parallel", "parallel")),
        name="hyena_dft3",
    )(consts["f1inv"][:hr], z, tokv(x0), tokv(uv), bias.reshape(1, cw).astype(F32))
    return out.reshape(bsz, n, cw)


def _hyena(p_lat, p):
    n = p_lat.shape[1]
    consts = _dft_consts(n)
    n1 = consts["n1"]
    x0, uv = _hyena_pre(p_lat, p["conv_w"], p["conv_b"])
    taps, sabs = _hyena_kernel_taps(n, p)
    cw = taps.shape[1]
    ones = jnp.ones((1, cw), F32)
    hk = _dft_stage1(taps[None], consts, n1)
    hspec = _dft_mid(hk, jnp.zeros((1, 2, HY_F1, HY_N2, cw), F32), 1.0 / sabs, consts, True)
    a = _dft_stage1(uv, consts, n1 // 2)
    z = _dft_mid(a, hspec, ones, consts, False)
    return _dft_stage1_inv(z, x0, uv, p["hy_bias"], consts)


def _mod_vectors(c, c_ctx, p):
    bsz, d = c.shape
    cvec = jnp.concatenate([c, c_ctx[None], jnp.zeros((8 - bsz - 1, d), c.dtype)], axis=0).astype(F32)
    m = _mods(cvec, p["w_mod"].astype(F32), p["b_mod"].astype(F32))
    lat = [v[:, None, :] for v in jnp.split(m[:bsz], N_MOD, axis=-1)]
    cx = [jnp.broadcast_to(v[:, None, :], (bsz, 1, d)) for v in jnp.split(m[bsz:bsz + 1], N_MOD, axis=-1)]
    return lat, cx


def _affine(norm_g, shift, scale):
    return norm_g.astype(F32) * (1.0 + scale), shift


def kernel(x, c, ctx, c_ctx,
           l0_w_mod, l0_b_mod, l0_norm1_g, l0_norm2_g, l0_w_in, l0_lam_q1, l0_lam_k1, l0_lam_q2, l0_lam_k2,
           l0_subln_g, l0_mu_prev, l0_mu_next, l0_w0_f, l0_w2_f, l0_a0_f, l0_a2_f, l0_w0_b, l0_w2_b, l0_a0_b,
           l0_a2_b, l0_g2, l0_k_k, l0_k_a, l0_r_k, l0_lnx_g, l0_lnx_b, l0_w_out, l0_mlp_w1, l0_mlp_w2,
           l1_w_mod, l1_b_mod, l1_norm1_g, l1_norm2_g, l1_w_in, l1_conv_w, l1_conv_b, l1_f_w1, l1_f_b1,
           l1_f_w2, l1_f_b2, l1_f_w3, l1_f_b3, l1_f_w4, l1_f_freq, l1_hy_bias, l1_q_norm_g, l1_k_norm_g,
           l1_w_out, l1_mlp_w1, l1_mlp_w2, final_g):
    bsz, n, d = x.shape
    x = x.astype(F32)
    ctx_s = ctx.astype(F32)
    bf = lambda w: w.astype(BF16)

    p0 = dict(w_mod=l0_w_mod, b_mod=l0_b_mod, mu_prev=l0_mu_prev, mu_next=l0_mu_next, w0_f=l0_w0_f, w2_f=l0_w2_f,
              a0_f=l0_a0_f, a2_f=l0_a2_f, w0_b=l0_w0_b, w2_b=l0_w2_b, a0_b=l0_a0_b, a2_b=l0_a2_b, g2=l0_g2,
              k_k=l0_k_k, k_a=l0_k_a, r_k=l0_r_k, lnx_g=l0_lnx_g, lnx_b=l0_lnx_b)
    (sh1, sc1, g1, sh2, sc2, g2), (csh1, csc1, cg1, csh2, csc2, cg2) = _mod_vectors(c, c_ctx, p0)
    aw = d // 2
    w_rw = l0_w_in[:, 3 * aw:]
    rw_cols = -(-(w_rw.shape[1] + 2 * LANE - B_GATE_RANK) // 512) * 512
    w_in0 = bf(jnp.concatenate([w_rw, jnp.zeros((d, rw_cols - w_rw.shape[1]), w_rw.dtype), l0_w_in[:, :3 * aw]],
                               axis=1))
    a1, b1 = _affine(l0_norm1_g, sh1, sc1)
    ca1, cb1 = _affine(l0_norm1_g, csh1, csc1)
    p_lat0 = _inproj(x, a1, b1, w_in0)
    p_ctx0 = _inproj(ctx_s, ca1, cb1, w_in0)
    lam_init = 0.8 - 0.6 * math.exp(-0.3 * 0)
    lam = (jnp.exp(jnp.sum(l0_lam_q1 * l0_lam_k1).astype(F32))
           - jnp.exp(jnp.sum(l0_lam_q2 * l0_lam_k2).astype(F32)) + lam_init)
    oa_lat, oa_ctx = _diff_attention(p_lat0, p_ctx0, rw_cols, aw, lam, lam_init, l0_subln_g)
    nh, hd = l0_r_k.shape
    zero = jnp.zeros((bsz, hd, nh * hd), F32)
    y_ctx, states_c = _rwkv_run(_rwkv_prepare(p_ctx0, p0, rw_cols), (zero, zero), p0)
    y_lat, _ = _rwkv_run(_rwkv_prepare(p_lat0, p0, rw_cols), states_c, p0)
    w_out0 = bf(l0_w_out)
    x = _outproj(oa_lat, y_lat, w_out0, x, g1)
    ctx_s = _outproj(oa_ctx, y_ctx, w_out0, ctx_s, cg1)
    w1, w2 = bf(l0_mlp_w1), bf(l0_mlp_w2)
    a2, b2 = _affine(l0_norm2_g, sh2, sc2)
    ca2, cb2 = _affine(l0_norm2_g, csh2, csc2)
    x = _mlp(x, a2, b2, g2, w1, w2)
    ctx_s = _mlp(ctx_s, ca2, cb2, cg2, w1, w2)

    p1 = dict(w_mod=l1_w_mod, b_mod=l1_b_mod, conv_w=l1_conv_w, conv_b=l1_conv_b, f_w1=l1_f_w1, f_b1=l1_f_b1,
              f_w2=l1_f_w2, f_b2=l1_f_b2, f_w3=l1_f_w3, f_b3=l1_f_b3, f_w4=l1_f_w4, f_freq=l1_f_freq,
              hy_bias=l1_hy_bias)
    (sh1, sc1, g1, sh2, sc2, g2), (csh1, csc1, _, _, _, _) = _mod_vectors(c, c_ctx, p1)
    hy_cols = l1_conv_w.shape[1]
    kv_cols = 2 * (2 * D_HEAD)
    w_in1 = bf(l1_w_in)
    a1, b1 = _affine(l1_norm1_g, sh1, sc1)
    ca1, cb1 = _affine(l1_norm1_g, csh1, csc1)
    p_lat = _inproj(x, a1, b1, w_in1)
    pkv_ctx = _inproj(ctx_s, ca1, cb1, w_in1[:, -kv_cols:])
    o_hy = _hyena(p_lat, p1)
    o_at = _gqa_attention(p_lat, pkv_ctx, hy_cols, l1_q_norm_g, l1_k_norm_g)
    x = _outproj(o_hy, o_at, bf(l1_w_out), x, g1)
    a2, b2 = _affine(l1_norm2_g, sh2, sc2)
    return _mlp(x, a2, b2, g2, bf(l1_mlp_w1), bf(l1_mlp_w2), final_g=final_g)
```
